```python
import math
import jax, jax.numpy as jnp
from jax import lax
import numpy as np

D_MODEL = 1024
BATCH = 8
SEQ = 4096
DEPTH = 1

MIX_WIDTH = D_MODEL
POOL_WIDTH = MIX_WIDTH // 2
SGU_WIDTH = MIX_WIDTH - POOL_WIDTH
POOL_WINDOWS = (2, 4, 8, 16)
N_POOL_GROUPS = len(POOL_WINDOWS)
POOL_GROUP_DIM = POOL_WIDTH // N_POOL_GROUPS
SGU_HEADS = 4
SGU_HEAD_DIM = SGU_WIDTH // SGU_HEADS
SGU_CHUNK = 128
IN_WIDTH = POOL_WIDTH + 2 * SGU_WIDTH
PEER_HEADS = 8
PEER_N_KEYS = 128
PEER_N_EXPERTS = PEER_N_KEYS * PEER_N_KEYS
PEER_D_QUERY = 256
PEER_D_HALF = PEER_D_QUERY // 2
PEER_TOPK = 16
PEER_TOKEN_BLOCK = 128
NORM_EPS = 1e-6

kernel_name = "hybrid_pool_sgu_peer_block"


def _rmsnorm(x, g):
    xf = x.astype(jnp.float32)
    inv = lax.rsqrt(jnp.mean(xf * xf, axis=-1, keepdims=True) + NORM_EPS)
    return (xf * inv).astype(x.dtype) * g


def _layernorm(x, g, b):
    xf = x.astype(jnp.float32)
    mu = jnp.mean(xf, axis=-1, keepdims=True)
    var = jnp.mean(jnp.square(xf - mu), axis=-1, keepdims=True)
    return ((xf - mu) * lax.rsqrt(var + NORM_EPS)).astype(x.dtype) * g + b


def _pool_mixer(p, w_pool, pool_scale):
    B, S, _ = p.shape
    pg = p.reshape(B, S, N_POOL_GROUPS, POOL_GROUP_DIM)
    cs = jnp.cumsum(pg.astype(jnp.float32), axis=1)
    t = jnp.arange(S)
    means = []
    for g, win in enumerate(POOL_WINDOWS):
        c = cs[:, :, g]
        lag = jnp.pad(c, ((0, 0), (win, 0), (0, 0)))[:, :S]
        cnt = jnp.minimum(t + 1, win).astype(jnp.float32)[None, :, None]
        means.append((c - lag) / cnt)
    mean = jnp.stack(means, axis=2)
    d = (mean - pg.astype(jnp.float32)).astype(p.dtype)
    out = jnp.einsum('bsgc,gcd->bsgd', d, w_pool) * pool_scale
    return out.reshape(B, S, POOL_WIDTH)


def _spatial_gating(u, v, ln_g, ln_b, w_s, b_s):
    B, S, _ = u.shape
    nc = S // SGU_CHUNK
    u = u.reshape(B, nc, SGU_CHUNK, SGU_HEADS, SGU_HEAD_DIM)
    v = v.reshape(B, nc, SGU_CHUNK, SGU_HEADS, SGU_HEAD_DIM)
    v = _layernorm(v, ln_g, ln_b)
    mask = jnp.tril(jnp.ones((SGU_CHUNK, SGU_CHUNK), dtype=bool))
    w = jnp.where(mask[None], w_s, jnp.zeros_like(w_s))
    mixed = jnp.einsum('hts,bnshc->bnthc', w, v) + b_s.T[None, None, :, :, None]
    return (u * mixed).reshape(B, S, SGU_WIDTH)


def _peer(h, w_q, keys, u_tab, v_tab):
    B, S, D = h.shape
    T = B * S
    xs = h.reshape(T // PEER_TOKEN_BLOCK, PEER_TOKEN_BLOCK, D)

    def block(xc):
        q = (xc @ w_q).reshape(PEER_TOKEN_BLOCK, PEER_HEADS, 2, PEER_D_HALF)
        s = jnp.einsum('chpd,pkd->chpk', q, keys)
        s1, i1 = lax.top_k(s[:, :, 0], PEER_TOPK)
        s2, i2 = lax.top_k(s[:, :, 1], PEER_TOPK)
        cand = (s1[..., :, None] + s2[..., None, :]).reshape(PEER_TOKEN_BLOCK, PEER_HEADS, PEER_TOPK * PEER_TOPK)
        cv, ci = lax.top_k(cand, PEER_TOPK)
        a = jnp.take_along_axis(i1, ci // PEER_TOPK, axis=-1)
        b = jnp.take_along_axis(i2, ci % PEER_TOPK, axis=-1)
        expert = (a * PEER_N_KEYS + b).reshape(PEER_TOKEN_BLOCK, PEER_HEADS * PEER_TOPK)
        gate = jax.nn.softmax(cv.astype(jnp.float32), axis=-1).astype(xc.dtype)
        gate = gate.reshape(PEER_TOKEN_BLOCK, PEER_HEADS * PEER_TOPK)
        u_sel = jnp.take(u_tab, expert, axis=0)
        v_sel = jnp.take(v_tab, expert, axis=0)
        act = jax.nn.gelu(jnp.einsum('cd,ckd->ck', xc, u_sel), approximate=False)
        return jnp.einsum('ck,ckd->cd', gate * act, v_sel)

    out = lax.map(block, xs)
    return out.reshape(B, S, D)


def setup_inputs(seed: int = 0) -> dict:
    key = jax.random.key(seed)
    ks = jax.random.split(key, 20)
    f32 = jnp.float32
    L = DEPTH

    def nrm(k, shape, scale):
        return jax.random.normal(k, shape, f32) * scale

    def gain(k, shape):
        return 1.0 + 0.02 * jax.random.normal(k, shape, f32)

    return {
        "x": jax.random.normal(ks[0], (BATCH, SEQ, D_MODEL), f32),
        "norm_mix": gain(ks[1], (L, D_MODEL)),
        "w_in": nrm(ks[2], (L, D_MODEL, IN_WIDTH), D_MODEL ** -0.5),
        "pool_w": nrm(ks[3], (L, N_POOL_GROUPS, POOL_GROUP_DIM, POOL_GROUP_DIM), POOL_GROUP_DIM ** -0.5),
        "pool_scale": gain(ks[4], (L, N_POOL_GROUPS, POOL_GROUP_DIM)),
        "sgu_ln_g": gain(ks[5], (L, SGU_HEADS, SGU_HEAD_DIM)),
        "sgu_ln_b": nrm(ks[6], (L, SGU_HEADS, SGU_HEAD_DIM), 0.02),
        "sgu_w": nrm(ks[7], (L, SGU_HEADS, SGU_CHUNK, SGU_CHUNK), SGU_CHUNK ** -0.5),
        "sgu_b": gain(ks[8], (L, SGU_HEADS, SGU_CHUNK)),
        "out_norm_pool": gain(ks[9], (L, POOL_WIDTH)),
        "out_norm_sgu": gain(ks[10], (L, SGU_WIDTH)),
        "w_out": nrm(ks[11], (L, MIX_WIDTH, D_MODEL), MIX_WIDTH ** -0.5),
        "norm_ffn": gain(ks[12], (L, D_MODEL)),
        "peer_wq": nrm(ks[13], (L, D_MODEL, PEER_HEADS * PEER_D_QUERY), D_MODEL ** -0.5),
        "peer_keys": nrm(ks[14], (L, 2, PEER_N_KEYS, PEER_D_HALF), PEER_D_HALF ** -0.5),
        "peer_u": nrm(ks[15], (L, PEER_N_EXPERTS, D_MODEL), D_MODEL ** -0.5),
        "peer_v": nrm(ks[16], (L, PEER_N_EXPERTS, D_MODEL), PEER_TOPK ** -0.5),
        "norm_final": gain(ks[17], (D_MODEL,)),
    }


def reference(x, norm_mix, w_in, pool_w, pool_scale, sgu_ln_g, sgu_ln_b, sgu_w, sgu_b,
              out_norm_pool, out_norm_sgu, w_out, norm_ffn, peer_wq, peer_keys, peer_u, peer_v,
              norm_final):
    for l in range(DEPTH):
        h = _rmsnorm(x, norm_mix[l])
        z = h @ w_in[l]
        p = z[..., :POOL_WIDTH]
        gz = jax.nn.gelu(z[..., POOL_WIDTH:], approximate=False)
        gu = gz[..., :SGU_WIDTH]
        gv = gz[..., SGU_WIDTH:]
        a_out = _pool_mixer(p, pool_w[l], pool_scale[l])
        b_out = _spatial_gating(gu, gv, sgu_ln_g[l], sgu_ln_b[l], sgu_w[l], sgu_b[l])
        mixed = jnp.concatenate([_rmsnorm(a_out, out_norm_pool[l]),
                                 _rmsnorm(b_out, out_norm_sgu[l])], axis=-1)
        x = x + mixed @ w_out[l]
        h2 = _rmsnorm(x, norm_ffn[l])
        x = x + _peer(h2, peer_wq[l], peer_keys[l], peer_u[l], peer_v[l])
    return _rmsnorm(x, norm_final)
```

```python
import functools
import math

import jax
import jax.numpy as jnp
from jax import lax
from jax.experimental import pallas as pl
from jax.experimental.pallas import tpu as pltpu
from jax.experimental.pallas import tpu_sc as plsc

POOL_WINDOWS = (2, 4, 8, 16)
N_POOL_GROUPS = len(POOL_WINDOWS)
SGU_HEADS = 4
SGU_CHUNK = 128
PEER_HEADS = 8
PEER_N_KEYS = 128
PEER_D_HALF = 128
PEER_TOPK = 16
NORM_EPS = 1e-6
EXPERTS_PER_TOKEN = PEER_HEADS * PEER_TOPK

V7X_LANES = 128
V7X_SUBLANES = 8
V7X_SC_CORES = 2
V7X_SC_SUBCORES = 16
V7X_SC_LANES = 16
V7X_SC_WORKERS = V7X_SC_CORES * V7X_SC_SUBCORES

HALO = max(POOL_WINDOWS)
MIX_TILE = 512
ROUTE_TILE = 256
EW_TILE = 512
SC_GATHER = 32
SC_TOKENS = 8
SC_CHUNK = 256
SC_UNROLL = 2
TC_VMEM_LIMIT = 48 * 1024 * 1024


def _rms(x, g):
    inv = lax.rsqrt(jnp.mean(x * x, axis=-1, keepdims=True) + NORM_EPS)
    return x * inv * g


def _gelu(x):
    return 0.5 * x * (1.0 + lax.erf(x * math.sqrt(0.5)))


def _mixer_kernel(x_ref, xh_ref, nmix_ref, win_ref, poolw_ref, pscale_ref, lng_ref, lnb_ref,
                  sguw_ref, sgub_ref, onp_ref, ons_ref, wout_ref, o_ref, pext_ref, mix_ref):
    i = pl.program_id(1)
    ts = x_ref.shape[1]
    pool_w = pscale_ref.shape[1]
    gdim = pool_w // N_POOL_GROUPS
    sgu_w = lng_ref.shape[1]
    hdim = sgu_w // SGU_HEADS

    x = x_ref[0]
    h = _rms(x, nmix_ref[...]).astype(jnp.bfloat16)
    z = jnp.dot(h, win_ref[...], preferred_element_type=jnp.float32)
    p = z[:, :pool_w]

    hh = _rms(xh_ref[0], nmix_ref[...]).astype(jnp.bfloat16)
    ph = jnp.dot(hh, win_ref[:, :pool_w], preferred_element_type=jnp.float32)
    ph = jnp.where(i > 0, ph, 0.0)
    pext_ref[0:HALO, :] = ph
    pext_ref[HALO:HALO + ts, :] = p

    pos = i * ts + lax.broadcasted_iota(jnp.int32, (ts, 1), 0)
    ssq = jnp.zeros((ts, 1), jnp.float32)
    a_parts = []
    for g, win in enumerate(POOL_WINDOWS):
        cols = slice(g * gdim, (g + 1) * gdim)
        s = pext_ref[HALO:HALO + ts, cols]
        for j in range(1, win):
            s = s + pext_ref[HALO - j:HALO - j + ts, cols]
        cnt = jnp.minimum(pos + 1, win).astype(jnp.float32)
        d = (s / cnt - p[:, cols]).astype(jnp.bfloat16)
        a = jnp.dot(d, poolw_ref[g], preferred_element_type=jnp.float32) * pscale_ref[:, cols]
        ssq = ssq + jnp.sum(a * a, axis=-1, keepdims=True)
        a_parts.append(a)
    inv_a = lax.rsqrt(ssq / pool_w + NORM_EPS)
    for g in range(N_POOL_GROUPS):
        cols = slice(g * gdim, (g + 1) * gdim)
        mix_ref[:, cols] = (a_parts[g] * inv_a * onp_ref[:, cols]).astype(jnp.bfloat16)

    gz = _gelu(z[:, pool_w:])
    tril = (lax.broadcasted_iota(jnp.int32, (SGU_CHUNK, SGU_CHUNK), 0)
            >= lax.broadcasted_iota(jnp.int32, (SGU_CHUNK, SGU_CHUNK), 1))
    ssq = jnp.zeros((ts, 1), jnp.float32)
    b_parts = []
    for hd in range(SGU_HEADS):
        cols = slice(hd * hdim, (hd + 1) * hdim)
        u = gz[:, hd * hdim:(hd + 1) * hdim]
        v = gz[:, sgu_w + hd * hdim:sgu_w + (hd + 1) * hdim]
        mu = jnp.mean(v, axis=-1, keepdims=True)
        vc = v - mu
        var = jnp.mean(vc * vc, axis=-1, keepdims=True)
        vn = (vc * lax.rsqrt(var + NORM_EPS) * lng_ref[:, cols] + lnb_ref[:, cols]).astype(jnp.bfloat16)
        w = jnp.where(tril, sguw_ref[hd], jnp.zeros((), sguw_ref.dtype))
        mixed = [jnp.dot(w, vn[n * SGU_CHUNK:(n + 1) * SGU_CHUNK], preferred_element_type=jnp.float32)
                 + sgub_ref[hd] for n in range(ts // SGU_CHUNK)]
        b = u * jnp.concatenate(mixed, axis=0)
        ssq = ssq + jnp.sum(b * b, axis=-1, keepdims=True)
        b_parts.append(b)
    inv_b = lax.rsqrt(ssq / sgu_w + NORM_EPS)
    for hd in range(SGU_HEADS):
        cols = slice(hd * hdim, (hd + 1) * hdim)
        mix_ref[:, pool_w + hd * hdim:pool_w + (hd + 1) * hdim] = (
            b_parts[hd] * inv_b * ons_ref[:, cols]).astype(jnp.bfloat16)

    o_ref[0] = x + jnp.dot(mix_ref[...], wout_ref[...], preferred_element_type=jnp.float32)


def _mixer(x, norm_mix, w_in, pool_w, pool_scale, ln_g, ln_b, sgu_w, sgu_b, on_pool, on_sgu, w_out):
    B, S, D = x.shape
    ts = min(MIX_TILE, S)
    pool_width = pool_scale.size
    sgu_width = ln_g.size
    in_width = w_in.shape[1]
    gdim = pool_width // N_POOL_GROUPS
    halo_blocks = ts // HALO
    full = lambda shape: pl.BlockSpec(shape, lambda b, i: (0,) * len(shape))
    return pl.pallas_call(
        _mixer_kernel,
        grid=(B, S // ts),
        in_specs=[
            pl.BlockSpec((1, ts, D), lambda b, i: (b, i, 0)),
            pl.BlockSpec((1, HALO, D), lambda b, i: (b, jnp.maximum(i * halo_blocks - 1, 0), 0)),
            full((1, D)),
            full((D, in_width)),
            full((N_POOL_GROUPS, gdim, gdim)),
            full((1, pool_width)),
            full((1, sgu_width)),
            full((1, sgu_width)),
            full((SGU_HEADS, SGU_CHUNK, SGU_CHUNK)),
            full((SGU_HEADS, SGU_CHUNK, SGU_CHUNK)),
            full((1, pool_width)),
            full((1, sgu_width)),
            full((pool_width + sgu_width, D)),
        ],
        out_specs=pl.BlockSpec((1, ts, D), lambda b, i: (b, i, 0)),
        out_shape=jax.ShapeDtypeStruct((B, S, D), jnp.float32),
        scratch_shapes=[
            pltpu.VMEM((HALO + ts, pool_width), jnp.float32),
            pltpu.VMEM((ts, pool_width + sgu_width), jnp.bfloat16),
        ],
        compiler_params=pltpu.CompilerParams(
            dimension_semantics=("parallel", "arbitrary"), vmem_limit_bytes=TC_VMEM_LIMIT),
        name="mixer",
    )(x, x, norm_mix.reshape(1, D), w_in.astype(jnp.bfloat16), pool_w.astype(jnp.bfloat16),
      pool_scale.reshape(1, pool_width), ln_g.reshape(1, sgu_width), ln_b.reshape(1, sgu_width),
      sgu_w.astype(jnp.bfloat16),
      jnp.broadcast_to(sgu_b[:, :, None], (SGU_HEADS, SGU_CHUNK, SGU_CHUNK)),
      on_pool.reshape(1, pool_width), on_sgu.reshape(1, sgu_width), w_out.astype(jnp.bfloat16))


def _topk_rows(s, k):
    n = s.shape[0]
    iota = lax.broadcasted_iota(jnp.int32, s.shape, 0)
    vals, idxs = [], []
    for _ in range(k):
        m = jnp.max(s, axis=0, keepdims=True)
        ix = jnp.min(jnp.where(s == m, iota, n), axis=0, keepdims=True)
        vals.append(m)
        idxs.append(ix)
        s = jnp.where(iota == ix, -jnp.inf, s)
    return vals, idxs


def _router_kernel(x_ref, nffn_ref, wq_ref, keys_ref, h2_ref, idx_ref, gate_ref, q_ref, idxt_ref, gatet_ref):
    h2 = _rms(x_ref[...], nffn_ref[...])
    h2_ref[...] = h2
    q_ref[...] = jnp.dot(h2.astype(jnp.bfloat16), wq_ref[...],
                         preferred_element_type=jnp.float32).astype(jnp.bfloat16)
    dq = 2 * PEER_D_HALF
    nt = (((1,), (1,)), ((), ()))

    def head(hd, carry):
        off = pl.multiple_of(hd * dq, dq)
        s1 = lax.dot_general(keys_ref[0], q_ref[:, pl.ds(off, PEER_D_HALF)], nt,
                             preferred_element_type=jnp.float32)
        s2 = lax.dot_general(keys_ref[1], q_ref[:, pl.ds(off + PEER_D_HALF, PEER_D_HALF)], nt,
                             preferred_element_type=jnp.float32)
        v1, i1 = _topk_rows(s1, PEER_TOPK)
        v2, i2 = _topk_rows(s2, PEER_TOPK)
        v2c = jnp.concatenate(v2, axis=0)
        i2c = jnp.concatenate(i2, axis=0)
        cand = jnp.concatenate([v1[a] + v2c for a in range(PEER_TOPK)], axis=0)
        expert = jnp.concatenate([i1[a] * PEER_N_KEYS + i2c for a in range(PEER_TOPK)], axis=0)
        iota = lax.broadcasted_iota(jnp.int32, cand.shape, 0)
        n = cand.shape[0]
        cv, ce = [], []
        for _ in range(PEER_TOPK):
            m = jnp.max(cand, axis=0, keepdims=True)
            ix = jnp.min(jnp.where(cand == m, iota, n), axis=0, keepdims=True)
            hit = iota == ix
            cv.append(m)
            ce.append(jnp.max(jnp.where(hit, expert, -1), axis=0, keepdims=True))
            cand = jnp.where(hit, -jnp.inf, cand)
        cvc = jnp.concatenate(cv, axis=0)
        e = jnp.exp(cvc - cv[0])
        gate = e / jnp.sum(e, axis=0, keepdims=True)
        row = pl.multiple_of(hd * PEER_TOPK, PEER_TOPK)
        idxt_ref[pl.ds(row, PEER_TOPK), :] = jnp.concatenate(ce, axis=0)
        gatet_ref[pl.ds(row, PEER_TOPK), :] = gate
        return carry

    lax.fori_loop(0, PEER_HEADS, head, 0)
    idx_ref[...] = idxt_ref[...].T
    gate_ref[...] = gatet_ref[...].T


def _router(x1, norm_ffn, wq, keys):
    T, D = x1.shape
    tr = min(ROUTE_TILE, T)
    qw = wq.shape[1]
    full = lambda shape: pl.BlockSpec(shape, lambda i: (0,) * len(shape))
    return pl.pallas_call(
        _router_kernel,
        grid=(T // tr,),
        in_specs=[
            pl.BlockSpec((tr, D), lambda i: (i, 0)),
            full((1, D)),
            full((D, qw)),
            full((2, PEER_N_KEYS, PEER_D_HALF)),
        ],
        out_specs=[
            pl.BlockSpec((tr, D), lambda i: (i, 0)),
            pl.BlockSpec((tr, EXPERTS_PER_TOKEN), lambda i: (i, 0)),
            pl.BlockSpec((tr, EXPERTS_PER_TOKEN), lambda i: (i, 0)),
        ],
        out_shape=[
            jax.ShapeDtypeStruct((T, D), jnp.float32),
            jax.ShapeDtypeStruct((T, EXPERTS_PER_TOKEN), jnp.int32),
            jax.ShapeDtypeStruct((T, EXPERTS_PER_TOKEN), jnp.float32),
        ],
        scratch_shapes=[
            pltpu.VMEM((tr, qw), jnp.bfloat16),
            pltpu.VMEM((EXPERTS_PER_TOKEN, tr), jnp.int32),
            pltpu.VMEM((EXPERTS_PER_TOKEN, tr), jnp.float32),
        ],
        compiler_params=pltpu.CompilerParams(
            dimension_semantics=("parallel",), vmem_limit_bytes=TC_VMEM_LIMIT),
        name="router",
    )(x1, norm_ffn.reshape(1, D), wq.astype(jnp.bfloat16), keys.astype(jnp.bfloat16))


def _act_kernel(pre_ref, gate_ref, w_ref):
    w_ref[...] = gate_ref[...] * _gelu(pre_ref[...])


def _expert_weights(pre, gate):
    T, K = pre.shape
    te = min(EW_TILE, T)
    spec = pl.BlockSpec((te, K), lambda i: (i, 0))
    return pl.pallas_call(
        _act_kernel, grid=(T // te,), in_specs=[spec, spec], out_specs=spec,
        out_shape=jax.ShapeDtypeStruct((T, K), jnp.float32),
        compiler_params=pltpu.CompilerParams(dimension_semantics=("parallel",)),
        name="expert_weights",
    )(pre, gate)


def _final_kernel(x_ref, y_ref, g_ref, o_ref):
    o_ref[...] = _rms(x_ref[...] + y_ref[...], g_ref[...])


def _final(x1, peer, norm_final):
    T, D = x1.shape
    te = min(EW_TILE, T)
    spec = pl.BlockSpec((te, D), lambda i: (i, 0))
    return pl.pallas_call(
        _final_kernel, grid=(T // te,),
        in_specs=[spec, spec, pl.BlockSpec((1, D), lambda i: (0, 0))], out_specs=spec,
        out_shape=jax.ShapeDtypeStruct((T, D), jnp.float32),
        compiler_params=pltpu.CompilerParams(dimension_semantics=("parallel",)),
        name="final_norm",
    )(x1, peer, norm_final.reshape(1, D))


def _tree_sum(vals):
    while len(vals) > 1:
        nxt = [vals[i] + vals[i + 1] for i in range(0, len(vals) - 1, 2)]
        if len(vals) % 2:
            nxt.append(vals[-1])
        vals = nxt
    return vals[0]


def _sc_token_pipeline(nblk, tok_base, items_per_token, load_block, store_block, gather, compute):
    @pl.loop(0, nblk)
    def _(b):
        tok0 = tok_base + b * SC_TOKENS
        load_block(tok0)
        gather(0, 0, 0).start()

        @pl.loop(0, SC_TOKENS)
        def _(t):
            for q in range(items_per_token):
                buf = q % 2
                if q + 1 < items_per_token:
                    gather(t, q + 1, 1 - buf).start()
                else:
                    @pl.when(t + 1 < SC_TOKENS)
                    def _():
                        gather(t + 1, 0, 1 - buf).start()
                gather(t, q, buf).wait()
                compute(t, q, buf)

        store_block(tok0)


def _sc_mesh():
    return plsc.VectorSubcoreMesh(core_axis_name="c", subcore_axis_name="s")


def _sc_worker_id():
    return lax.axis_index("s") * V7X_SC_CORES + lax.axis_index("c")


def _expert_scores(h2, idx, table):
    T, D = h2.shape
    K = idx.shape[1]
    L, G = V7X_SC_LANES, SC_GATHER
    nj = SC_CHUNK // L
    tok_per_w = T // V7X_SC_WORKERS
    assert T % (V7X_SC_WORKERS * SC_TOKENS) == 0 and K % (2 * G) == 0 and D % SC_CHUNK == 0

    @functools.partial(
        pl.kernel, mesh=_sc_mesh(),
        out_type=jax.ShapeDtypeStruct((T, K), jnp.float32),
        compiler_params=pltpu.CompilerParams(needs_layout_passes=False),
        scratch_types=[
            pltpu.VMEM((SC_TOKENS, K), jnp.int32),
            pltpu.VMEM((SC_TOKENS, D), jnp.float32),
            pltpu.VMEM((2, G, D), jnp.float32),
            pltpu.VMEM((G * L,), jnp.float32),
            pltpu.VMEM((SC_TOKENS, K), jnp.float32),
            pltpu.SemaphoreType.DMA((2,)),
        ],
        name="expert_scores")
    def k(h2_hbm, idx_hbm, tab_hbm, pre_hbm, idx_v, x_v, rows_v, part_v, pre_v, sems):
        def load_block(tok0):
            pltpu.sync_copy(idx_hbm.at[pl.ds(tok0, SC_TOKENS)], idx_v)
            pltpu.sync_copy(h2_hbm.at[pl.ds(tok0, SC_TOKENS)], x_v)

        def store_block(tok0):
            pltpu.sync_copy(pre_v, pre_hbm.at[pl.ds(tok0, SC_TOKENS)])

        def gather(t, q, buf):
            return pltpu.make_async_copy(
                tab_hbm.at[idx_v.at[t, pl.ds(q * G, G)]], rows_v.at[buf], sems.at[buf])

        def compute(t, q, buf):
            for c in range(D // SC_CHUNK):
                xs = [x_v[t, pl.ds(c * SC_CHUNK + j * L, L)] for j in range(nj)]

                @plsc.parallel_loop(0, G, unroll=SC_UNROLL)
                def _(kk):
                    prods = [rows_v[buf, kk, pl.ds(c * SC_CHUNK + j * L, L)] * xs[j] for j in range(nj)]
                    off = pl.multiple_of(kk * L, L)
                    if c == 0:
                        part_v[pl.ds(off, L)] = _tree_sum(prods)
                    else:
                        plsc.addupdate(part_v.at[pl.ds(off, L)], _tree_sum(prods))

            lane = lax.iota(jnp.int32, L) * L
            for g in range(G // L):
                cols = [plsc.load_gather(part_v, [lane + (g * L * L + j)]) for j in range(L)]
                pre_v[t, pl.ds(q * G + g * L, L)] = _tree_sum(cols)

        _sc_token_pipeline(tok_per_w // SC_TOKENS, _sc_worker_id() * tok_per_w, K // G,
                           load_block, store_block, gather, compute)

    return k(h2, idx, table)


def _expert_mix(w, idx, table):
    T, K = w.shape
    D = table.shape[1]
    L, G = V7X_SC_LANES, SC_GATHER
    nj = SC_CHUNK // L
    tok_per_w = T // V7X_SC_WORKERS
    assert T % (V7X_SC_WORKERS * SC_TOKENS) == 0 and K % (2 * G) == 0 and D % SC_CHUNK == 0

    @functools.partial(
        pl.kernel, mesh=_sc_mesh(),
        out_type=jax.ShapeDtypeStruct((T, D), jnp.float32),
        compiler_params=pltpu.CompilerParams(needs_layout_passes=False),
        scratch_types=[
            pltpu.VMEM((SC_TOKENS, K), jnp.int32),
            pltpu.VMEM((SC_TOKENS, K), jnp.float32),
            pltpu.VMEM((2, G, D), jnp.float32),
            pltpu.VMEM((SC_TOKENS, D), jnp.float32),
            pltpu.SemaphoreType.DMA((2,)),
        ],
        name="expert_mix")
    def k(w_hbm, idx_hbm, tab_hbm, out_hbm, idx_v, w_v, rows_v, out_v, sems):
        def load_block(tok0):
            pltpu.sync_copy(idx_hbm.at[pl.ds(tok0, SC_TOKENS)], idx_v)
            pltpu.sync_copy(w_hbm.at[pl.ds(tok0, SC_TOKENS)], w_v)

        def store_block(tok0):
            pltpu.sync_copy(out_v, out_hbm.at[pl.ds(tok0, SC_TOKENS)])

        def gather(t, q, buf):
            return pltpu.make_async_copy(
                tab_hbm.at[idx_v.at[t, pl.ds(q * G, G)]], rows_v.at[buf], sems.at[buf])

        def compute(t, q, buf):
            tsplat = jnp.full((L,), t, jnp.int32)
            for c in range(D // SC_CHUNK):
                def body(kk, acc):
                    wk = plsc.load_gather(w_v, [tsplat, jnp.full((L,), q * G, jnp.int32) + kk])
                    return tuple(acc[j] + wk * rows_v[buf, kk, pl.ds(c * SC_CHUNK + j * L, L)]
                                 for j in range(nj))

                zero = jnp.zeros((L,), jnp.float32)
                acc = plsc.parallel_loop(0, G, unroll=SC_UNROLL, carry=(zero,) * nj)(body)
                for j in range(nj):
                    dst = out_v.at[t, pl.ds(c * SC_CHUNK + j * L, L)]
                    if q == 0:
                        dst[...] = acc[j]
                    else:
                        plsc.addupdate(dst, acc[j])

        _sc_token_pipeline(tok_per_w // SC_TOKENS, _sc_worker_id() * tok_per_w, K // G,
                           load_block, store_block, gather, compute)

    return k(w, idx, table)


def kernel(x, norm_mix, w_in, pool_w, pool_scale, sgu_ln_g, sgu_ln_b, sgu_w, sgu_b, out_norm_pool,
           out_norm_sgu, w_out, norm_ffn, peer_wq, peer_keys, peer_u, peer_v, norm_final):
    B, S, D = x.shape
    depth = norm_mix.shape[0]
    for l in range(depth):
        x = _mixer(x, norm_mix[l], w_in[l], pool_w[l], pool_scale[l], sgu_ln_g[l], sgu_ln_b[l],
                   sgu_w[l], sgu_b[l], out_norm_pool[l], out_norm_sgu[l], w_out[l])
        x1 = x.reshape(B * S, D)
        h2, idx, gate = _router(x1, norm_ffn[l], peer_wq[l], peer_keys[l])
        pre = _expert_scores(h2, idx, peer_u[l])
        w = _expert_weights(pre, gate)
        peer = _expert_mix(w, idx, peer_v[l])
        if l + 1 < depth:
            x = (x1 + peer).reshape(B, S, D)
    return _final(x1, peer, norm_final).reshape(B, S, D)
```

```python
import functools
import math

import jax
import jax.numpy as jnp
from jax import lax
from jax.experimental import pallas as pl
from jax.experimental.pallas import tpu as pltpu
from jax.experimental.pallas import tpu_sc as plsc

POOL_WINDOWS = (2, 4, 8, 16)
N_POOL_GROUPS = len(POOL_WINDOWS)
SGU_HEADS = 4
SGU_CHUNK = 128
PEER_HEADS = 8
PEER_N_KEYS = 128
PEER_D_HALF = 128
PEER_TOPK = 16
NORM_EPS = 1e-6
EXPERTS_PER_TOKEN = PEER_HEADS * PEER_TOPK

V7X_LANES = 128
V7X_SUBLANES = 8
V7X_SC_CORES = 2
V7X_SC_SUBCORES = 16
V7X_SC_LANES = 16
V7X_SC_WORKERS = V7X_SC_CORES * V7X_SC_SUBCORES

HALO = max(POOL_WINDOWS)
MIX_TILE = 512
ROUTE_TILE = 256
EW_TILE = 512
SC_GATHER = 32
SC_TOKENS = 8
SC_CHUNK = 256
SC_UNROLL = 2
TC_VMEM_LIMIT = 48 * 1024 * 1024
TOKEN_CHUNKS = 4


def _rms(x, g):
    inv = lax.rsqrt(jnp.mean(x * x, axis=-1, keepdims=True) + NORM_EPS)
    return x * inv * g


def _gelu(x):
    return 0.5 * x * (1.0 + lax.erf(x * math.sqrt(0.5)))


def _mixer_kernel(x_ref, xh_ref, nmix_ref, win_ref, poolw_ref, pscale_ref, lng_ref, lnb_ref,
                  sguw_ref, sgub_ref, onp_ref, ons_ref, wout_ref, o_ref, pext_ref, mix_ref):
    i = pl.program_id(1)
    ts = x_ref.shape[1]
    pool_w = pscale_ref.shape[1]
    gdim = pool_w // N_POOL_GROUPS
    sgu_w = lng_ref.shape[1]
    hdim = sgu_w // SGU_HEADS

    x = x_ref[0]
    h = _rms(x, nmix_ref[...]).astype(jnp.bfloat16)
    z = jnp.dot(h, win_ref[...], preferred_element_type=jnp.float32)
    p = z[:, :pool_w]

    hh = _rms(xh_ref[0], nmix_ref[...]).astype(jnp.bfloat16)
    ph = jnp.dot(hh, win_ref[:, :pool_w], preferred_element_type=jnp.float32)
    ph = jnp.where(i > 0, ph, 0.0)
    pext_ref[0:HALO, :] = ph
    pext_ref[HALO:HALO + ts, :] = p

    pos = i * ts + lax.broadcasted_iota(jnp.int32, (ts, 1), 0)
    ssq = jnp.zeros((ts, 1), jnp.float32)
    a_parts = []
    for g, win in enumerate(POOL_WINDOWS):
        cols = slice(g * gdim, (g + 1) * gdim)
        s = pext_ref[HALO:HALO + ts, cols]
        for j in range(1, win):
            s = s + pext_ref[HALO - j:HALO - j + ts, cols]
        cnt = jnp.minimum(pos + 1, win).astype(jnp.float32)
        d = (s / cnt - p[:, cols]).astype(jnp.bfloat16)
        a = jnp.dot(d, poolw_ref[g], preferred_element_type=jnp.float32) * pscale_ref[:, cols]
        ssq = ssq + jnp.sum(a * a, axis=-1, keepdims=True)
        a_parts.append(a)
    inv_a = lax.rsqrt(ssq / pool_w + NORM_EPS)
    for g in range(N_POOL_GROUPS):
        cols = slice(g * gdim, (g + 1) * gdim)
        mix_ref[:, cols] = (a_parts[g] * inv_a * onp_ref[:, cols]).astype(jnp.bfloat16)

    gz = _gelu(z[:, pool_w:])
    tril = (lax.broadcasted_iota(jnp.int32, (SGU_CHUNK, SGU_CHUNK), 0)
            >= lax.broadcasted_iota(jnp.int32, (SGU_CHUNK, SGU_CHUNK), 1))
    ssq = jnp.zeros((ts, 1), jnp.float32)
    b_parts = []
    for hd in range(SGU_HEADS):
        cols = slice(hd * hdim, (hd + 1) * hdim)
        u = gz[:, hd * hdim:(hd + 1) * hdim]
        v = gz[:, sgu_w + hd * hdim:sgu_w + (hd + 1) * hdim]
        mu = jnp.mean(v, axis=-1, keepdims=True)
        vc = v - mu
        var = jnp.mean(vc * vc, axis=-1, keepdims=True)
        vn = (vc * lax.rsqrt(var + NORM_EPS) * lng_ref[:, cols] + lnb_ref[:, cols]).astype(jnp.bfloat16)
        w = jnp.where(tril, sguw_ref[hd], jnp.zeros((), sguw_ref.dtype))
        mixed = [jnp.dot(w, vn[n * SGU_CHUNK:(n + 1) * SGU_CHUNK], preferred_element_type=jnp.float32)
                 + sgub_ref[hd] for n in range(ts // SGU_CHUNK)]
        b = u * jnp.concatenate(mixed, axis=0)
        ssq = ssq + jnp.sum(b * b, axis=-1, keepdims=True)
        b_parts.append(b)
    inv_b = lax.rsqrt(ssq / sgu_w + NORM_EPS)
    for hd in range(SGU_HEADS):
        cols = slice(hd * hdim, (hd + 1) * hdim)
        mix_ref[:, pool_w + hd * hdim:pool_w + (hd + 1) * hdim] = (
            b_parts[hd] * inv_b * ons_ref[:, cols]).astype(jnp.bfloat16)

    o_ref[0] = x + jnp.dot(mix_ref[...], wout_ref[...], preferred_element_type=jnp.float32)


def _mixer(x, norm_mix, w_in, pool_w, pool_scale, ln_g, ln_b, sgu_w, sgu_b, on_pool, on_sgu, w_out):
    B, S, D = x.shape
    ts = min(MIX_TILE, S)
    pool_width = pool_scale.size
    sgu_width = ln_g.size
    in_width = w_in.shape[1]
    gdim = pool_width // N_POOL_GROUPS
    halo_blocks = ts // HALO
    full = lambda shape: pl.BlockSpec(shape, lambda b, i: (0,) * len(shape))
    return pl.pallas_call(
        _mixer_kernel,
        grid=(B, S // ts),
        in_specs=[
            pl.BlockSpec((1, ts, D), lambda b, i: (b, i, 0)),
            pl.BlockSpec((1, HALO, D), lambda b, i: (b, jnp.maximum(i * halo_blocks - 1, 0), 0)),
            full((1, D)),
            full((D, in_width)),
            full((N_POOL_GROUPS, gdim, gdim)),
            full((1, pool_width)),
            full((1, sgu_width)),
            full((1, sgu_width)),
            full((SGU_HEADS, SGU_CHUNK, SGU_CHUNK)),
            full((SGU_HEADS, SGU_CHUNK, SGU_CHUNK)),
            full((1, pool_width)),
            full((1, sgu_width)),
            full((pool_width + sgu_width, D)),
        ],
        out_specs=pl.BlockSpec((1, ts, D), lambda b, i: (b, i, 0)),
        out_shape=jax.ShapeDtypeStruct((B, S, D), jnp.float32),
        scratch_shapes=[
            pltpu.VMEM((HALO + ts, pool_width), jnp.float32),
            pltpu.VMEM((ts, pool_width + sgu_width), jnp.bfloat16),
        ],
        compiler_params=pltpu.CompilerParams(
            dimension_semantics=("parallel", "arbitrary"), vmem_limit_bytes=TC_VMEM_LIMIT),
        name="mixer",
    )(x, x, norm_mix.reshape(1, D), w_in.astype(jnp.bfloat16), pool_w.astype(jnp.bfloat16),
      pool_scale.reshape(1, pool_width), ln_g.reshape(1, sgu_width), ln_b.reshape(1, sgu_width),
      sgu_w.astype(jnp.bfloat16),
      jnp.broadcast_to(sgu_b[:, :, None], (SGU_HEADS, SGU_CHUNK, SGU_CHUNK)),
      on_pool.reshape(1, pool_width), on_sgu.reshape(1, sgu_width), w_out.astype(jnp.bfloat16))


def _topk_rows(s, k):
    n = s.shape[0]
    iota = lax.broadcasted_iota(jnp.int32, s.shape, 0)
    vals, idxs = [], []
    for _ in range(k):
        m = jnp.max(s, axis=0, keepdims=True)
        ix = jnp.min(jnp.where(s == m, iota, n), axis=0, keepdims=True)
        vals.append(m)
        idxs.append(ix)
        s = jnp.where(iota == ix, -jnp.inf, s)
    return vals, idxs


def _router_kernel(x_ref, nffn_ref, wq_ref, keys_ref, h2_ref, idx_ref, gate_ref, q_ref, idxt_ref, gatet_ref):
    h2 = _rms(x_ref[...], nffn_ref[...])
    h2_ref[...] = h2
    q_ref[...] = jnp.dot(h2.astype(jnp.bfloat16), wq_ref[...],
                         preferred_element_type=jnp.float32).astype(jnp.bfloat16)
    dq = 2 * PEER_D_HALF
    nt = (((1,), (1,)), ((), ()))

    def head(hd, carry):
        off = pl.multiple_of(hd * dq, dq)
        s1 = lax.dot_general(keys_ref[0], q_ref[:, pl.ds(off, PEER_D_HALF)], nt,
                             preferred_element_type=jnp.float32)
        s2 = lax.dot_general(keys_ref[1], q_ref[:, pl.ds(off + PEER_D_HALF, PEER_D_HALF)], nt,
                             preferred_element_type=jnp.float32)
        v1, i1 = _topk_rows(s1, PEER_TOPK)
        v2, i2 = _topk_rows(s2, PEER_TOPK)
        v2c = jnp.concatenate(v2, axis=0)
        i2c = jnp.concatenate(i2, axis=0)
        cand = jnp.concatenate([v1[a] + v2c for a in range(PEER_TOPK)], axis=0)
        expert = jnp.concatenate([i1[a] * PEER_N_KEYS + i2c for a in range(PEER_TOPK)], axis=0)
        iota = lax.broadcasted_iota(jnp.int32, cand.shape, 0)
        n = cand.shape[0]
        cv, ce = [], []
        for _ in range(PEER_TOPK):
            m = jnp.max(cand, axis=0, keepdims=True)
            ix = jnp.min(jnp.where(cand == m, iota, n), axis=0, keepdims=True)
            hit = iota == ix
            cv.append(m)
            ce.append(jnp.max(jnp.where(hit, expert, -1), axis=0, keepdims=True))
            cand = jnp.where(hit, -jnp.inf, cand)
        cvc = jnp.concatenate(cv, axis=0)
        e = jnp.exp(cvc - cv[0])
        gate = e / jnp.sum(e, axis=0, keepdims=True)
        row = pl.multiple_of(hd * PEER_TOPK, PEER_TOPK)
        idxt_ref[pl.ds(row, PEER_TOPK), :] = jnp.concatenate(ce, axis=0)
        gatet_ref[pl.ds(row, PEER_TOPK), :] = gate
        return carry

    lax.fori_loop(0, PEER_HEADS, head, 0)
    idx_ref[...] = idxt_ref[...].T
    gate_ref[...] = gatet_ref[...].T


def _router(x1, norm_ffn, wq, keys, chunk, T):
    D = x1.shape[1]
    tr = min(ROUTE_TILE, T)
    qw = wq.shape[1]
    first = chunk * (T // tr)
    full = lambda shape: pl.BlockSpec(shape, lambda i: (0,) * len(shape))
    return pl.pallas_call(
        _router_kernel,
        grid=(T // tr,),
        in_specs=[
            pl.BlockSpec((tr, D), lambda i: (first + i, 0)),
            full((1, D)),
            full((D, qw)),
            full((2, PEER_N_KEYS, PEER_D_HALF)),
        ],
        out_specs=[
            pl.BlockSpec((tr, D), lambda i: (i, 0)),
            pl.BlockSpec((tr, EXPERTS_PER_TOKEN), lambda i: (i, 0)),
            pl.BlockSpec((tr, EXPERTS_PER_TOKEN), lambda i: (i, 0)),
        ],
        out_shape=[
            jax.ShapeDtypeStruct((T, D), jnp.float32),
            jax.ShapeDtypeStruct((T, EXPERTS_PER_TOKEN), jnp.int32),
            jax.ShapeDtypeStruct((T, EXPERTS_PER_TOKEN), jnp.float32),
        ],
        scratch_shapes=[
            pltpu.VMEM((tr, qw), jnp.bfloat16),
            pltpu.VMEM((EXPERTS_PER_TOKEN, tr), jnp.int32),
            pltpu.VMEM((EXPERTS_PER_TOKEN, tr), jnp.float32),
        ],
        compiler_params=pltpu.CompilerParams(
            dimension_semantics=("parallel",), vmem_limit_bytes=TC_VMEM_LIMIT),
        name="router",
    )(x1, norm_ffn.reshape(1, D), wq, keys)


def _act_kernel(pre_ref, gate_ref, w_ref):
    w_ref[...] = gate_ref[...] * _gelu(pre_ref[...])


def _expert_weights(pre, gate):
    T, K = pre.shape
    te = min(EW_TILE, T)
    spec = pl.BlockSpec((te, K), lambda i: (i, 0))
    return pl.pallas_call(
        _act_kernel, grid=(T // te,), in_specs=[spec, spec], out_specs=spec,
        out_shape=jax.ShapeDtypeStruct((T, K), jnp.float32),
        compiler_params=pltpu.CompilerParams(dimension_semantics=("parallel",)),
        name="expert_weights",
    )(pre, gate)


def _final_kernel(x_ref, y_ref, g_ref, *rest):
    o_ref = rest[-1]
    o_ref[...] = _rms(x_ref[...] + y_ref[...], g_ref[...])


def _final(x1, peer, norm_final, out_prev, chunk):
    Tc, D = peer.shape
    te = min(EW_TILE, Tc)
    nblk = Tc // te
    spec = pl.BlockSpec((te, D), lambda i: (chunk * nblk + i, 0))
    in_specs = [spec, pl.BlockSpec((te, D), lambda i: (i, 0)), pl.BlockSpec((1, D), lambda i: (0, 0))]
    args = [x1, peer, norm_final.reshape(1, D)]
    aliases = {}
    if out_prev is not None:
        in_specs.append(pl.BlockSpec(memory_space=pl.ANY))
        args.append(out_prev)
        aliases = {3: 0}
    return pl.pallas_call(
        _final_kernel, grid=(nblk,), in_specs=in_specs, out_specs=spec,
        out_shape=jax.ShapeDtypeStruct(x1.shape, jnp.float32),
        input_output_aliases=aliases,
        compiler_params=pltpu.CompilerParams(dimension_semantics=("parallel",)),
        name="final_norm",
    )(*args)


def _tree_sum(vals):
    while len(vals) > 1:
        nxt = [vals[i] + vals[i + 1] for i in range(0, len(vals) - 1, 2)]
        if len(vals) % 2:
            nxt.append(vals[-1])
        vals = nxt
    return vals[0]


def _sc_token_pipeline(nblk, tok_base, items_per_token, load_block, store_block, gather, compute):
    @pl.loop(0, nblk)
    def _(b):
        tok0 = tok_base + b * SC_TOKENS
        load_block(tok0)
        gather(0, 0, 0).start()

        @pl.loop(0, SC_TOKENS)
        def _(t):
            for q in range(items_per_token):
                buf = q % 2
                if q + 1 < items_per_token:
                    gather(t, q + 1, 1 - buf).start()
                else:
                    @pl.when(t + 1 < SC_TOKENS)
                    def _():
                        gather(t + 1, 0, 1 - buf).start()
                gather(t, q, buf).wait()
                compute(t, q, buf)

        store_block(tok0)


def _sc_mesh():
    return plsc.VectorSubcoreMesh(core_axis_name="c", subcore_axis_name="s")


def _sc_worker_id():
    return lax.axis_index("s") * V7X_SC_CORES + lax.axis_index("c")


def _expert_scores(h2, idx, table):
    T, D = h2.shape
    K = idx.shape[1]
    L, G = V7X_SC_LANES, SC_GATHER
    nj = SC_CHUNK // L
    tok_per_w = T // V7X_SC_WORKERS
    assert T % (V7X_SC_WORKERS * SC_TOKENS) == 0 and K % (2 * G) == 0 and D % SC_CHUNK == 0

    @functools.partial(
        pl.kernel, mesh=_sc_mesh(),
        out_type=jax.ShapeDtypeStruct((T, K), jnp.float32),
        compiler_params=pltpu.CompilerParams(needs_layout_passes=False),
        scratch_types=[
            pltpu.VMEM((SC_TOKENS, K), jnp.int32),
            pltpu.VMEM((SC_TOKENS, D), jnp.float32),
            pltpu.VMEM((2, G, D), jnp.float32),
            pltpu.VMEM((G * L,), jnp.float32),
            pltpu.VMEM((SC_TOKENS, K), jnp.float32),
            pltpu.SemaphoreType.DMA((2,)),
        ],
        name="expert_scores")
    def k(h2_hbm, idx_hbm, tab_hbm, pre_hbm, idx_v, x_v, rows_v, part_v, pre_v, sems):
        def load_block(tok0):
            pltpu.sync_copy(idx_hbm.at[pl.ds(tok0, SC_TOKENS)], idx_v)
            pltpu.sync_copy(h2_hbm.at[pl.ds(tok0, SC_TOKENS)], x_v)

        def store_block(tok0):
            pltpu.sync_copy(pre_v, pre_hbm.at[pl.ds(tok0, SC_TOKENS)])

        def gather(t, q, buf):
            return pltpu.make_async_copy(
                tab_hbm.at[idx_v.at[t, pl.ds(q * G, G)]], rows_v.at[buf], sems.at[buf])

        def compute(t, q, buf):
            for c in range(D // SC_CHUNK):
                xs = [x_v[t, pl.ds(c * SC_CHUNK + j * L, L)] for j in range(nj)]

                @plsc.parallel_loop(0, G, unroll=SC_UNROLL)
                def _(kk):
                    prods = [rows_v[buf, kk, pl.ds(c * SC_CHUNK + j * L, L)] * xs[j] for j in range(nj)]
                    off = pl.multiple_of(kk * L, L)
                    if c == 0:
                        part_v[pl.ds(off, L)] = _tree_sum(prods)
                    else:
                        plsc.addupdate(part_v.at[pl.ds(off, L)], _tree_sum(prods))

            lane = lax.iota(jnp.int32, L) * L
            for g in range(G // L):
                cols = [plsc.load_gather(part_v, [lane + (g * L * L + j)]) for j in range(L)]
                pre_v[t, pl.ds(q * G + g * L, L)] = _tree_sum(cols)

        _sc_token_pipeline(tok_per_w // SC_TOKENS, _sc_worker_id() * tok_per_w, K // G,
                           load_block, store_block, gather, compute)

    return k(h2, idx, table)


def _expert_mix(w, idx, table):
    T, K = w.shape
    D = table.shape[1]
    L, G = V7X_SC_LANES, SC_GATHER
    nj = SC_CHUNK // L
    tok_per_w = T // V7X_SC_WORKERS
    assert T % (V7X_SC_WORKERS * SC_TOKENS) == 0 and K % (2 * G) == 0 and D % SC_CHUNK == 0

    @functools.partial(
        pl.kernel, mesh=_sc_mesh(),
        out_type=jax.ShapeDtypeStruct((T, D), jnp.float32),
        compiler_params=pltpu.CompilerParams(needs_layout_passes=False),
        scratch_types=[
            pltpu.VMEM((SC_TOKENS, K), jnp.int32),
            pltpu.VMEM((SC_TOKENS, K), jnp.float32),
            pltpu.VMEM((2, G, D), jnp.float32),
            pltpu.VMEM((SC_TOKENS, D), jnp.float32),
            pltpu.SemaphoreType.DMA((2,)),
        ],
        name="expert_mix")
    def k(w_hbm, idx_hbm, tab_hbm, out_hbm, idx_v, w_v, rows_v, out_v, sems):
        def load_block(tok0):
            pltpu.sync_copy(idx_hbm.at[pl.ds(tok0, SC_TOKENS)], idx_v)
            pltpu.sync_copy(w_hbm.at[pl.ds(tok0, SC_TOKENS)], w_v)

        def store_block(tok0):
            pltpu.sync_copy(out_v, out_hbm.at[pl.ds(tok0, SC_TOKENS)])

        def gather(t, q, buf):
            return pltpu.make_async_copy(
                tab_hbm.at[idx_v.at[t, pl.ds(q * G, G)]], rows_v.at[buf], sems.at[buf])

        def compute(t, q, buf):
            tsplat = jnp.full((L,), t, jnp.int32)
            for c in range(D // SC_CHUNK):
                def body(kk, acc):
                    wk = plsc.load_gather(w_v, [tsplat, jnp.full((L,), q * G, jnp.int32) + kk])
                    return tuple(acc[j] + wk * rows_v[buf, kk, pl.ds(c * SC_CHUNK + j * L, L)]
                                 for j in range(nj))

                zero = jnp.zeros((L,), jnp.float32)
                acc = plsc.parallel_loop(0, G, unroll=SC_UNROLL, carry=(zero,) * nj)(body)
                for j in range(nj):
                    dst = out_v.at[t, pl.ds(c * SC_CHUNK + j * L, L)]
                    if q == 0:
                        dst[...] = acc[j]
                    else:
                        plsc.addupdate(dst, acc[j])

        _sc_token_pipeline(tok_per_w // SC_TOKENS, _sc_worker_id() * tok_per_w, K // G,
                           load_block, store_block, gather, compute)

    return k(w, idx, table)


def kernel(x, norm_mix, w_in, pool_w, pool_scale, sgu_ln_g, sgu_ln_b, sgu_w, sgu_b, out_norm_pool,
           out_norm_sgu, w_out, norm_ffn, peer_wq, peer_keys, peer_u, peer_v, norm_final):
    B, S, D = x.shape
    assert norm_mix.shape[0] == 1, "single-layer block"
    T = B * S
    tc = T // TOKEN_CHUNKS
    x1 = _mixer(x, norm_mix[0], w_in[0], pool_w[0], pool_scale[0], sgu_ln_g[0], sgu_ln_b[0],
                sgu_w[0], sgu_b[0], out_norm_pool[0], out_norm_sgu[0], w_out[0]).reshape(T, D)
    wq = peer_wq[0].astype(jnp.bfloat16)
    keys = peer_keys[0].astype(jnp.bfloat16)
    out = None
    for c in range(TOKEN_CHUNKS):
        h2, idx, gate = _router(x1, norm_ffn[0], wq, keys, c, tc)
        pre = _expert_scores(h2, idx, peer_u[0])
        w = _expert_weights(pre, gate)
        peer = _expert_mix(w, idx, peer_v[0])
        out = _final(x1, peer, norm_final, out, c)
    return out.reshape(B, S, D)
```

```python
import functools
import math

import jax
import jax.numpy as jnp
from jax import lax
from jax.experimental import pallas as pl
from jax.experimental.pallas import tpu as pltpu
from jax.experimental.pallas import tpu_sc as plsc

POOL_WINDOWS = (2, 4, 8, 16)
N_POOL_GROUPS = len(POOL_WINDOWS)
SGU_HEADS = 4
SGU_CHUNK = 128
PEER_HEADS = 8
PEER_N_KEYS = 128
PEER_D_HALF = 128
PEER_TOPK = 16
NORM_EPS = 1e-6
EXPERTS_PER_TOKEN = PEER_HEADS * PEER_TOPK

V7X_LANES = 128
V7X_SUBLANES = 8
V7X_SC_CORES = 2
V7X_SC_SUBCORES = 16
V7X_SC_LANES = 16
V7X_SC_WORKERS = V7X_SC_CORES * V7X_SC_SUBCORES

HALO = max(POOL_WINDOWS)
MIX_TILE = 512
ROUTE_TILE = 256
EW_TILE = 512
SC_GATHER = 64
SC_TOKENS = 8
SC_MIX_CHUNK = 128
SC_MIX_GROUP = 4
SC_UNROLL = 2
HI_HALF = 0xFFFF0000
TC_VMEM_LIMIT = 48 * 1024 * 1024
TOKEN_CHUNKS = 4


def _rms(x, g):
    inv = lax.rsqrt(jnp.mean(x * x, axis=-1, keepdims=True) + NORM_EPS)
    return x * inv * g


def _pack_halves(bits):
    half = bits.shape[1] // 2
    return (bits[:, :half] >> 16) | (bits[:, half:] & jnp.uint32(HI_HALF))


def _pack_table(a):
    return _pack_halves(lax.bitcast_convert_type(a.astype(jnp.bfloat16).astype(jnp.float32), jnp.uint32))


def _gelu(x):
    return 0.5 * x * (1.0 + lax.erf(x * math.sqrt(0.5)))


def _mixer_kernel(x_ref, xh_ref, nmix_ref, win_ref, poolw_ref, pscale_ref, lng_ref, lnb_ref,
                  sguw_ref, sgub_ref, onp_ref, ons_ref, wout_ref, o_ref, pext_ref, mix_ref):
    i = pl.program_id(1)
    ts = x_ref.shape[1]
    pool_w = pscale_ref.shape[1]
    gdim = pool_w // N_POOL_GROUPS
    sgu_w = lng_ref.shape[1]
    hdim = sgu_w // SGU_HEADS

    x = x_ref[0]
    h = _rms(x, nmix_ref[...]).astype(jnp.bfloat16)
    z = jnp.dot(h, win_ref[...], preferred_element_type=jnp.float32)
    p = z[:, :pool_w]

    hh = _rms(xh_ref[0], nmix_ref[...]).astype(jnp.bfloat16)
    ph = jnp.dot(hh, win_ref[:, :pool_w], preferred_element_type=jnp.float32)
    ph = jnp.where(i > 0, ph, 0.0)
    pext_ref[0:HALO, :] = ph
    pext_ref[HALO:HALO + ts, :] = p

    pos = i * ts + lax.broadcasted_iota(jnp.int32, (ts, 1), 0)
    ssq = jnp.zeros((ts, 1), jnp.float32)
    a_parts = []
    for g, win in enumerate(POOL_WINDOWS):
        cols = slice(g * gdim, (g + 1) * gdim)
        s = pext_ref[HALO:HALO + ts, cols]
        for j in range(1, win):
            s = s + pext_ref[HALO - j:HALO - j + ts, cols]
        cnt = jnp.minimum(pos + 1, win).astype(jnp.float32)
        d = (s / cnt - p[:, cols]).astype(jnp.bfloat16)
        a = jnp.dot(d, poolw_ref[g], preferred_element_type=jnp.float32) * pscale_ref[:, cols]
        ssq = ssq + jnp.sum(a * a, axis=-1, keepdims=True)
        a_parts.append(a)
    inv_a = lax.rsqrt(ssq / pool_w + NORM_EPS)
    for g in range(N_POOL_GROUPS):
        cols = slice(g * gdim, (g + 1) * gdim)
        mix_ref[:, cols] = (a_parts[g] * inv_a * onp_ref[:, cols]).astype(jnp.bfloat16)

    gz = _gelu(z[:, pool_w:])
    tril = (lax.broadcasted_iota(jnp.int32, (SGU_CHUNK, SGU_CHUNK), 0)
            >= lax.broadcasted_iota(jnp.int32, (SGU_CHUNK, SGU_CHUNK), 1))
    ssq = jnp.zeros((ts, 1), jnp.float32)
    b_parts = []
    for hd in range(SGU_HEADS):
        cols = slice(hd * hdim, (hd + 1) * hdim)
        u = gz[:, hd * hdim:(hd + 1) * hdim]
        v = gz[:, sgu_w + hd * hdim:sgu_w + (hd + 1) * hdim]
        mu = jnp.mean(v, axis=-1, keepdims=True)
        vc = v - mu
        var = jnp.mean(vc * vc, axis=-1, keepdims=True)
        vn = (vc * lax.rsqrt(var + NORM_EPS) * lng_ref[:, cols] + lnb_ref[:, cols]).astype(jnp.bfloat16)
        w = jnp.where(tril, sguw_ref[hd], jnp.zeros((), sguw_ref.dtype))
        mixed = [jnp.dot(w, vn[n * SGU_CHUNK:(n + 1) * SGU_CHUNK], preferred_element_type=jnp.float32)
                 + sgub_ref[hd] for n in range(ts // SGU_CHUNK)]
        b = u * jnp.concatenate(mixed, axis=0)
        ssq = ssq + jnp.sum(b * b, axis=-1, keepdims=True)
        b_parts.append(b)
    inv_b = lax.rsqrt(ssq / sgu_w + NORM_EPS)
    for hd in range(SGU_HEADS):
        cols = slice(hd * hdim, (hd + 1) * hdim)
        mix_ref[:, pool_w + hd * hdim:pool_w + (hd + 1) * hdim] = (
            b_parts[hd] * inv_b * ons_ref[:, cols]).astype(jnp.bfloat16)

    o_ref[0] = x + jnp.dot(mix_ref[...], wout_ref[...], preferred_element_type=jnp.float32)


def _mixer(x, norm_mix, w_in, pool_w, pool_scale, ln_g, ln_b, sgu_w, sgu_b, on_pool, on_sgu, w_out):
    B, S, D = x.shape
    ts = min(MIX_TILE, S)
    pool_width = pool_scale.size
    sgu_width = ln_g.size
    in_width = w_in.shape[1]
    gdim = pool_width // N_POOL_GROUPS
    halo_blocks = ts // HALO
    full = lambda shape: pl.BlockSpec(shape, lambda b, i: (0,) * len(shape))
    return pl.pallas_call(
        _mixer_kernel,
        grid=(B, S // ts),
        in_specs=[
            pl.BlockSpec((1, ts, D), lambda b, i: (b, i, 0)),
            pl.BlockSpec((1, HALO, D), lambda b, i: (b, jnp.maximum(i * halo_blocks - 1, 0), 0)),
            full((1, D)),
            full((D, in_width)),
            full((N_POOL_GROUPS, gdim, gdim)),
            full((1, pool_width)),
            full((1, sgu_width)),
            full((1, sgu_width)),
            full((SGU_HEADS, SGU_CHUNK, SGU_CHUNK)),
            full((SGU_HEADS, SGU_CHUNK, SGU_CHUNK)),
            full((1, pool_width)),
            full((1, sgu_width)),
            full((pool_width + sgu_width, D)),
        ],
        out_specs=pl.BlockSpec((1, ts, D), lambda b, i: (b, i, 0)),
        out_shape=jax.ShapeDtypeStruct((B, S, D), jnp.float32),
        scratch_shapes=[
            pltpu.VMEM((HALO + ts, pool_width), jnp.float32),
            pltpu.VMEM((ts, pool_width + sgu_width), jnp.bfloat16),
        ],
        compiler_params=pltpu.CompilerParams(
            dimension_semantics=("parallel", "arbitrary"), vmem_limit_bytes=TC_VMEM_LIMIT),
        name="mixer",
    )(x, x, norm_mix.reshape(1, D), w_in.astype(jnp.bfloat16), pool_w.astype(jnp.bfloat16),
      pool_scale.reshape(1, pool_width), ln_g.reshape(1, sgu_width), ln_b.reshape(1, sgu_width),
      sgu_w.astype(jnp.bfloat16),
      jnp.broadcast_to(sgu_b[:, :, None], (SGU_HEADS, SGU_CHUNK, SGU_CHUNK)),
      on_pool.reshape(1, pool_width), on_sgu.reshape(1, sgu_width), w_out.astype(jnp.bfloat16))


def _topk_rows(s, k):
    n = s.shape[0]
    iota = lax.broadcasted_iota(jnp.int32, s.shape, 0)
    vals, idxs = [], []
    for _ in range(k):
        m = jnp.max(s, axis=0, keepdims=True)
        ix = jnp.min(jnp.where(s == m, iota, n), axis=0, keepdims=True)
        vals.append(m)
        idxs.append(ix)
        s = jnp.where(iota == ix, -jnp.inf, s)
    return vals, idxs


def _router_kernel(x_ref, nffn_ref, wq_ref, keys_ref, h2_ref, idx_ref, gate_ref, q_ref, idxt_ref, gatet_ref):
    h2 = _rms(x_ref[...], nffn_ref[...]).astype(jnp.bfloat16)
    h2_ref[...] = _pack_halves(pltpu.bitcast(h2.astype(jnp.float32), jnp.uint32))
    q_ref[...] = jnp.dot(h2, wq_ref[...], preferred_element_type=jnp.float32).astype(jnp.bfloat16)
    dq = 2 * PEER_D_HALF
    nt = (((1,), (1,)), ((), ()))

    def head(hd, carry):
        off = pl.multiple_of(hd * dq, dq)
        s1 = lax.dot_general(keys_ref[0], q_ref[:, pl.ds(off, PEER_D_HALF)], nt,
                             preferred_element_type=jnp.float32)
        s2 = lax.dot_general(keys_ref[1], q_ref[:, pl.ds(off + PEER_D_HALF, PEER_D_HALF)], nt,
                             preferred_element_type=jnp.float32)
        v1, i1 = _topk_rows(s1, PEER_TOPK)
        v2, i2 = _topk_rows(s2, PEER_TOPK)
        v2c = jnp.concatenate(v2, axis=0)
        i2c = jnp.concatenate(i2, axis=0)
        cand = jnp.concatenate([v1[a] + v2c for a in range(PEER_TOPK)], axis=0)
        expert = jnp.concatenate([i1[a] * PEER_N_KEYS + i2c for a in range(PEER_TOPK)], axis=0)
        iota = lax.broadcasted_iota(jnp.int32, cand.shape, 0)
        n = cand.shape[0]
        cv, ce = [], []
        for _ in range(PEER_TOPK):
            m = jnp.max(cand, axis=0, keepdims=True)
            ix = jnp.min(jnp.where(cand == m, iota, n), axis=0, keepdims=True)
            hit = iota == ix
            cv.append(m)
            ce.append(jnp.max(jnp.where(hit, expert, -1), axis=0, keepdims=True))
            cand = jnp.where(hit, -jnp.inf, cand)
        cvc = jnp.concatenate(cv, axis=0)
        e = jnp.exp(cvc - cv[0])
        gate = e / jnp.sum(e, axis=0, keepdims=True)
        row = pl.multiple_of(hd * PEER_TOPK, PEER_TOPK)
        idxt_ref[pl.ds(row, PEER_TOPK), :] = jnp.concatenate(ce, axis=0)
        gatet_ref[pl.ds(row, PEER_TOPK), :] = gate
        return carry

    lax.fori_loop(0, PEER_HEADS, head, 0)
    idx_ref[...] = idxt_ref[...].T
    gate_ref[...] = gatet_ref[...].T


def _router(x1, norm_ffn, wq, keys, chunk, T):
    D = x1.shape[1]
    tr = min(ROUTE_TILE, T)
    qw = wq.shape[1]
    first = chunk * (T // tr)
    full = lambda shape: pl.BlockSpec(shape, lambda i: (0,) * len(shape))
    return pl.pallas_call(
        _router_kernel,
        grid=(T // tr,),
        in_specs=[
            pl.BlockSpec((tr, D), lambda i: (first + i, 0)),
            full((1, D)),
            full((D, qw)),
            full((2, PEER_N_KEYS, PEER_D_HALF)),
        ],
        out_specs=[
            pl.BlockSpec((tr, D // 2), lambda i: (i, 0)),
            pl.BlockSpec((tr, EXPERTS_PER_TOKEN), lambda i: (i, 0)),
            pl.BlockSpec((tr, EXPERTS_PER_TOKEN), lambda i: (i, 0)),
        ],
        out_shape=[
            jax.ShapeDtypeStruct((T, D // 2), jnp.uint32),
            jax.ShapeDtypeStruct((T, EXPERTS_PER_TOKEN), jnp.int32),
            jax.ShapeDtypeStruct((T, EXPERTS_PER_TOKEN), jnp.float32),
        ],
        scratch_shapes=[
            pltpu.VMEM((tr, qw), jnp.bfloat16),
            pltpu.VMEM((EXPERTS_PER_TOKEN, tr), jnp.int32),
            pltpu.VMEM((EXPERTS_PER_TOKEN, tr), jnp.float32),
        ],
        compiler_params=pltpu.CompilerParams(
            dimension_semantics=("parallel",), vmem_limit_bytes=TC_VMEM_LIMIT),
        name="router",
    )(x1, norm_ffn.reshape(1, D), wq, keys)


def _act_kernel(pre_ref, gate_ref, w_ref):
    w = (gate_ref[...] * _gelu(pre_ref[...])).astype(jnp.bfloat16).astype(jnp.float32)
    hi = pltpu.bitcast(w, jnp.uint32) & jnp.uint32(HI_HALF)
    w_ref[...] = pltpu.bitcast(hi | (hi >> 16), jnp.int32)


def _expert_weights(pre, gate):
    T, K = pre.shape
    te = min(EW_TILE, T)
    spec = pl.BlockSpec((te, K), lambda i: (i, 0))
    return pl.pallas_call(
        _act_kernel, grid=(T // te,), in_specs=[spec, spec], out_specs=spec,
        out_shape=jax.ShapeDtypeStruct((T, K), jnp.int32),
        compiler_params=pltpu.CompilerParams(dimension_semantics=("parallel",)),
        name="expert_weights",
    )(pre, gate)


def _final_kernel(x_ref, y_ref, g_ref, *rest):
    o_ref = rest[-1]
    o_ref[...] = _rms(x_ref[...] + y_ref[...], g_ref[...])


def _final(x1, peer, norm_final, out_prev, chunk):
    Tc, D = peer.shape
    te = min(EW_TILE, Tc)
    nblk = Tc // te
    spec = pl.BlockSpec((te, D), lambda i: (chunk * nblk + i, 0))
    in_specs = [spec, pl.BlockSpec((te, D), lambda i: (i, 0)), pl.BlockSpec((1, D), lambda i: (0, 0))]
    args = [x1, peer, norm_final.reshape(1, D)]
    aliases = {}
    if out_prev is not None:
        in_specs.append(pl.BlockSpec(memory_space=pl.ANY))
        args.append(out_prev)
        aliases = {3: 0}
    return pl.pallas_call(
        _final_kernel, grid=(nblk,), in_specs=in_specs, out_specs=spec,
        out_shape=jax.ShapeDtypeStruct(x1.shape, jnp.float32),
        input_output_aliases=aliases,
        compiler_params=pltpu.CompilerParams(dimension_semantics=("parallel",)),
        name="final_norm",
    )(*args)


def _tree_sum(vals):
    while len(vals) > 1:
        nxt = [vals[i] + vals[i + 1] for i in range(0, len(vals) - 1, 2)]
        if len(vals) % 2:
            nxt.append(vals[-1])
        vals = nxt
    return vals[0]


def _sc_token_pipeline(nblk, tok_base, items_per_token, load_block, store_block, gather, compute):
    @pl.loop(0, nblk)
    def _(b):
        tok0 = tok_base + b * SC_TOKENS
        load_block(tok0)
        gather(0, 0, 0).start()

        @pl.loop(0, SC_TOKENS)
        def _(t):
            for q in range(items_per_token):
                buf = q % 2
                if q + 1 < items_per_token:
                    gather(t, q + 1, 1 - buf).start()
                else:
                    @pl.when(t + 1 < SC_TOKENS)
                    def _():
                        gather(t + 1, 0, 1 - buf).start()
                gather(t, q, buf).wait()
                compute(t, q, buf)

        store_block(tok0)


def _sc_mesh():
    return plsc.VectorSubcoreMesh(core_axis_name="c", subcore_axis_name="s")


def _sc_worker_id():
    return lax.axis_index("s") * V7X_SC_CORES + lax.axis_index("c")


def _sc_bf16(words):
    return plsc.bitcast(words, jnp.bfloat16)


def _sc_halves_f32(pairs):
    words = plsc.bitcast(pairs, jnp.uint32)
    return (plsc.bitcast(words << 16, jnp.float32),
            plsc.bitcast(words & jnp.uint32(HI_HALF), jnp.float32))


def _expert_scores(h2, idx, table):
    T, DW = h2.shape
    K = idx.shape[1]
    L, G = V7X_SC_LANES, SC_GATHER
    nj = DW // L
    tok_per_w = T // V7X_SC_WORKERS
    assert T % (V7X_SC_WORKERS * SC_TOKENS) == 0 and K % (2 * G) == 0 and nj % 4 == 0

    @functools.partial(
        pl.kernel, mesh=_sc_mesh(),
        out_type=jax.ShapeDtypeStruct((T, K), jnp.float32),
        compiler_params=pltpu.CompilerParams(needs_layout_passes=False),
        scratch_types=[
            pltpu.VMEM((SC_TOKENS, K), jnp.int32),
            pltpu.VMEM((SC_TOKENS, DW), jnp.uint32),
            pltpu.VMEM((2, G, DW), jnp.uint32),
            pltpu.VMEM((G * L,), jnp.float32),
            pltpu.VMEM((SC_TOKENS, K), jnp.float32),
            pltpu.SemaphoreType.DMA((2,)),
        ],
        name="expert_scores")
    def k(h2_hbm, idx_hbm, tab_hbm, pre_hbm, idx_v, x_v, rows_v, part_v, pre_v, sems):
        def load_block(tok0):
            pltpu.sync_copy(idx_hbm.at[pl.ds(tok0, SC_TOKENS)], idx_v)
            pltpu.sync_copy(h2_hbm.at[pl.ds(tok0, SC_TOKENS)], x_v)

        def store_block(tok0):
            pltpu.sync_copy(pre_v, pre_hbm.at[pl.ds(tok0, SC_TOKENS)])

        def gather(t, q, buf):
            return pltpu.make_async_copy(
                tab_hbm.at[idx_v.at[t, pl.ds(q * G, G)]], rows_v.at[buf], sems.at[buf])

        def compute(t, q, buf):
            xs = [_sc_bf16(x_v[t, pl.ds(j * L, L)]) for j in range(nj)]

            @plsc.parallel_loop(0, G, unroll=SC_UNROLL)
            def _(kk):
                prods = [_sc_bf16(rows_v[buf, kk, pl.ds(j * L, L)]) * xs[j] for j in range(nj)]
                quads = [(prods[j] + prods[j + 1]) + (prods[j + 2] + prods[j + 3]) for j in range(0, nj, 4)]
                part_v[pl.ds(pl.multiple_of(kk * L, L), L)] = _tree_sum(
                    [h for p in quads for h in _sc_halves_f32(p)])

            lane = lax.iota(jnp.int32, L) * L
            for g in range(G // L):
                cols = [plsc.load_gather(part_v, [lane + (g * L * L + j)]) for j in range(L)]
                pre_v[t, pl.ds(q * G + g * L, L)] = _tree_sum(cols)

        _sc_token_pipeline(tok_per_w // SC_TOKENS, _sc_worker_id() * tok_per_w, K // G,
                           load_block, store_block, gather, compute)

    return k(h2, idx, table)


def _expert_mix(w, idx, table):
    T, K = w.shape
    DW = table.shape[1]
    D = 2 * DW
    L, G = V7X_SC_LANES, SC_GATHER
    nj = SC_MIX_CHUNK // L
    tok_per_w = T // V7X_SC_WORKERS
    assert T % (V7X_SC_WORKERS * SC_TOKENS) == 0 and K % (2 * G) == 0
    assert DW % SC_MIX_CHUNK == 0 and G % SC_MIX_GROUP == 0

    @functools.partial(
        pl.kernel, mesh=_sc_mesh(),
        out_type=jax.ShapeDtypeStruct((T, D), jnp.float32),
        compiler_params=pltpu.CompilerParams(needs_layout_passes=False),
        scratch_types=[
            pltpu.VMEM((SC_TOKENS, K), jnp.int32),
            pltpu.VMEM((SC_TOKENS, K), jnp.int32),
            pltpu.VMEM((2, G, DW), jnp.uint32),
            pltpu.VMEM((SC_TOKENS, D), jnp.float32),
            pltpu.SemaphoreType.DMA((2,)),
        ],
        name="expert_mix")
    def k(w_hbm, idx_hbm, tab_hbm, out_hbm, idx_v, w_v, rows_v, out_v, sems):
        def load_block(tok0):
            pltpu.sync_copy(idx_hbm.at[pl.ds(tok0, SC_TOKENS)], idx_v)
            pltpu.sync_copy(w_hbm.at[pl.ds(tok0, SC_TOKENS)], w_v)

        def store_block(tok0):
            pltpu.sync_copy(out_v, out_hbm.at[pl.ds(tok0, SC_TOKENS)])

        def gather(t, q, buf):
            return pltpu.make_async_copy(
                tab_hbm.at[idx_v.at[t, pl.ds(q * G, G)]], rows_v.at[buf], sems.at[buf])

        def compute(t, q, buf):
            tsplat = jnp.full((L,), t, jnp.int32)
            for c in range(DW // SC_MIX_CHUNK):
                def body(kg, acc):
                    kk = kg * SC_MIX_GROUP
                    wks = [_sc_bf16(plsc.load_gather(
                        w_v, [tsplat, jnp.full((L,), q * G + i, jnp.int32) + kk])) for i in range(SC_MIX_GROUP)]
                    out = []
                    for j in range(nj):
                        prods = [wks[i] * _sc_bf16(rows_v[buf, kk + i, pl.ds(c * SC_MIX_CHUNK + j * L, L)])
                                 for i in range(SC_MIX_GROUP)]
                        lo, hi = _sc_halves_f32(_tree_sum(prods))
                        out += [acc[2 * j] + lo, acc[2 * j + 1] + hi]
                    return tuple(out)

                zero = jnp.zeros((L,), jnp.float32)
                acc = plsc.parallel_loop(0, G // SC_MIX_GROUP, carry=(zero,) * (2 * nj))(body)
                for j in range(nj):
                    for half in range(2):
                        dst = out_v.at[t, pl.ds(half * DW + c * SC_MIX_CHUNK + j * L, L)]
                        if q == 0:
                            dst[...] = acc[2 * j + half]
                        else:
                            plsc.addupdate(dst, acc[2 * j + half])

        _sc_token_pipeline(tok_per_w // SC_TOKENS, _sc_worker_id() * tok_per_w, K // G,
                           load_block, store_block, gather, compute)

    return k(w, idx, table)


def kernel(x, norm_mix, w_in, pool_w, pool_scale, sgu_ln_g, sgu_ln_b, sgu_w, sgu_b, out_norm_pool,
           out_norm_sgu, w_out, norm_ffn, peer_wq, peer_keys, peer_u, peer_v, norm_final):
    B, S, D = x.shape
    assert norm_mix.shape[0] == 1, "single-layer block"
    T = B * S
    tc = T // TOKEN_CHUNKS
    x1 = _mixer(x, norm_mix[0], w_in[0], pool_w[0], pool_scale[0], sgu_ln_g[0], sgu_ln_b[0],
                sgu_w[0], sgu_b[0], out_norm_pool[0], out_norm_sgu[0], w_out[0]).reshape(T, D)
    wq = peer_wq[0].astype(jnp.bfloat16)
    keys = peer_keys[0].astype(jnp.bfloat16)
    u_tab = _pack_table(peer_u[0])
    v_tab = _pack_table(peer_v[0])
    out = None
    for c in range(TOKEN_CHUNKS):
        h2, idx, gate = _router(x1, norm_ffn[0], wq, keys, c, tc)
        pre = _expert_scores(h2, idx, u_tab)
        w = _expert_weights(pre, gate)
        peer = _expert_mix(w, idx, v_tab)
        out = _final(x1, peer, norm_final, out, c)
    return out.reshape(B, S, D)
```

```python
import functools
import math

import jax
import jax.numpy as jnp
from jax import lax
from jax.experimental import pallas as pl
from jax.experimental.pallas import tpu as pltpu
from jax.experimental.pallas import tpu_sc as plsc

POOL_WINDOWS = (2, 4, 8, 16)
N_POOL_GROUPS = len(POOL_WINDOWS)
SGU_HEADS = 4
SGU_CHUNK = 128
PEER_HEADS = 8
PEER_N_KEYS = 128
PEER_D_HALF = 128
PEER_TOPK = 16
NORM_EPS = 1e-6
EXPERTS_PER_TOKEN = PEER_HEADS * PEER_TOPK

V7X_LANES = 128
V7X_SUBLANES = 8
V7X_SC_CORES = 2
V7X_SC_SUBCORES = 16
V7X_SC_LANES = 16
V7X_SC_WORKERS = V7X_SC_CORES * V7X_SC_SUBCORES

HALO = max(POOL_WINDOWS)
MIX_TILE = 512
ROUTE_TILE = 256
EW_TILE = 512
SC_GATHER = 64
SC_TOKENS = 8
SC_MIX_CHUNK = 128
SC_MIX_GROUP = 4
SC_UNROLL = 2
HI_HALF = 0xFFFF0000
TC_VMEM_LIMIT = 48 * 1024 * 1024
FIRST_CHUNK = 1024
MAX_CHUNK = 8192


def _rms(x, g):
    inv = lax.rsqrt(jnp.mean(x * x, axis=-1, keepdims=True) + NORM_EPS)
    return x * inv * g


def _pack_halves(bits):
    half = bits.shape[1] // 2
    return (bits[:, :half] >> 16) | (bits[:, half:] & jnp.uint32(HI_HALF))


def _pack_table(a):
    return _pack_halves(lax.bitcast_convert_type(a.astype(jnp.bfloat16).astype(jnp.float32), jnp.uint32))


def _gelu(x):
    return 0.5 * x * (1.0 + lax.erf(x * math.sqrt(0.5)))


def _mixer_kernel(x_ref, xh_ref, nmix_ref, win_ref, poolw_ref, pscale_ref, lng_ref, lnb_ref,
                  sguw_ref, sgub_ref, onp_ref, ons_ref, wout_ref, o_ref, pext_ref, mix_ref):
    i = pl.program_id(1)
    ts = x_ref.shape[1]
    pool_w = pscale_ref.shape[1]
    gdim = pool_w // N_POOL_GROUPS
    sgu_w = lng_ref.shape[1]
    hdim = sgu_w // SGU_HEADS

    x = x_ref[0]
    h = _rms(x, nmix_ref[...]).astype(jnp.bfloat16)
    z = jnp.dot(h, win_ref[...], preferred_element_type=jnp.float32)
    p = z[:, :pool_w]

    hh = _rms(xh_ref[0], nmix_ref[...]).astype(jnp.bfloat16)
    ph = jnp.dot(hh, win_ref[:, :pool_w], preferred_element_type=jnp.float32)
    ph = jnp.where(i > 0, ph, 0.0)
    pext_ref[0:HALO, :] = ph
    pext_ref[HALO:HALO + ts, :] = p

    pos = i * ts + lax.broadcasted_iota(jnp.int32, (ts, 1), 0)
    ssq = jnp.zeros((ts, 1), jnp.float32)
    a_parts = []
    for g, win in enumerate(POOL_WINDOWS):
        cols = slice(g * gdim, (g + 1) * gdim)
        s = pext_ref[HALO:HALO + ts, cols]
        for j in range(1, win):
            s = s + pext_ref[HALO - j:HALO - j + ts, cols]
        cnt = jnp.minimum(pos + 1, win).astype(jnp.float32)
        d = (s / cnt - p[:, cols]).astype(jnp.bfloat16)
        a = jnp.dot(d, poolw_ref[g], preferred_element_type=jnp.float32) * pscale_ref[:, cols]
        ssq = ssq + jnp.sum(a * a, axis=-1, keepdims=True)
        a_parts.append(a)
    inv_a = lax.rsqrt(ssq / pool_w + NORM_EPS)
    for g in range(N_POOL_GROUPS):
        cols = slice(g * gdim, (g + 1) * gdim)
        mix_ref[:, cols] = (a_parts[g] * inv_a * onp_ref[:, cols]).astype(jnp.bfloat16)

    gz = _gelu(z[:, pool_w:])
    tril = (lax.broadcasted_iota(jnp.int32, (SGU_CHUNK, SGU_CHUNK), 0)
            >= lax.broadcasted_iota(jnp.int32, (SGU_CHUNK, SGU_CHUNK), 1))
    ssq = jnp.zeros((ts, 1), jnp.float32)
    b_parts = []
    for hd in range(SGU_HEADS):
        cols = slice(hd * hdim, (hd + 1) * hdim)
        u = gz[:, hd * hdim:(hd + 1) * hdim]
        v = gz[:, sgu_w + hd * hdim:sgu_w + (hd + 1) * hdim]
        mu = jnp.mean(v, axis=-1, keepdims=True)
        vc = v - mu
        var = jnp.mean(vc * vc, axis=-1, keepdims=True)
        vn = (vc * lax.rsqrt(var + NORM_EPS) * lng_ref[:, cols] + lnb_ref[:, cols]).astype(jnp.bfloat16)
        w = jnp.where(tril, sguw_ref[hd], jnp.zeros((), sguw_ref.dtype))
        mixed = [jnp.dot(w, vn[n * SGU_CHUNK:(n + 1) * SGU_CHUNK], preferred_element_type=jnp.float32)
                 + sgub_ref[hd] for n in range(ts // SGU_CHUNK)]
        b = u * jnp.concatenate(mixed, axis=0)
        ssq = ssq + jnp.sum(b * b, axis=-1, keepdims=True)
        b_parts.append(b)
    inv_b = lax.rsqrt(ssq / sgu_w + NORM_EPS)
    for hd in range(SGU_HEADS):
        cols = slice(hd * hdim, (hd + 1) * hdim)
        mix_ref[:, pool_w + hd * hdim:pool_w + (hd + 1) * hdim] = (
            b_parts[hd] * inv_b * ons_ref[:, cols]).astype(jnp.bfloat16)

    o_ref[0] = x + jnp.dot(mix_ref[...], wout_ref[...], preferred_element_type=jnp.float32)


def _mixer(x, norm_mix, w_in, pool_w, pool_scale, ln_g, ln_b, sgu_w, sgu_b, on_pool, on_sgu, w_out):
    B, S, D = x.shape
    ts = min(MIX_TILE, S)
    pool_width = pool_scale.size
    sgu_width = ln_g.size
    in_width = w_in.shape[1]
    gdim = pool_width // N_POOL_GROUPS
    halo_blocks = ts // HALO
    full = lambda shape: pl.BlockSpec(shape, lambda b, i: (0,) * len(shape))
    return pl.pallas_call(
        _mixer_kernel,
        grid=(B, S // ts),
        in_specs=[
            pl.BlockSpec((1, ts, D), lambda b, i: (b, i, 0)),
            pl.BlockSpec((1, HALO, D), lambda b, i: (b, jnp.maximum(i * halo_blocks - 1, 0), 0)),
            full((1, D)),
            full((D, in_width)),
            full((N_POOL_GROUPS, gdim, gdim)),
            full((1, pool_width)),
            full((1, sgu_width)),
            full((1, sgu_width)),
            full((SGU_HEADS, SGU_CHUNK, SGU_CHUNK)),
            full((SGU_HEADS, SGU_CHUNK, SGU_CHUNK)),
            full((1, pool_width)),
            full((1, sgu_width)),
            full((pool_width + sgu_width, D)),
        ],
        out_specs=pl.BlockSpec((1, ts, D), lambda b, i: (b, i, 0)),
        out_shape=jax.ShapeDtypeStruct((B, S, D), jnp.float32),
        scratch_shapes=[
            pltpu.VMEM((HALO + ts, pool_width), jnp.float32),
            pltpu.VMEM((ts, pool_width + sgu_width), jnp.bfloat16),
        ],
        compiler_params=pltpu.CompilerParams(
            dimension_semantics=("parallel", "arbitrary"), vmem_limit_bytes=TC_VMEM_LIMIT),
        name="mixer",
    )(x, x, norm_mix.reshape(1, D), w_in.astype(jnp.bfloat16), pool_w.astype(jnp.bfloat16),
      pool_scale.reshape(1, pool_width), ln_g.reshape(1, sgu_width), ln_b.reshape(1, sgu_width),
      sgu_w.astype(jnp.bfloat16),
      jnp.broadcast_to(sgu_b[:, :, None], (SGU_HEADS, SGU_CHUNK, SGU_CHUNK)),
      on_pool.reshape(1, pool_width), on_sgu.reshape(1, sgu_width), w_out.astype(jnp.bfloat16))


def _topk_rows(s, k):
    n = s.shape[0]
    iota = lax.broadcasted_iota(jnp.int32, s.shape, 0)
    vals, idxs = [], []
    for _ in range(k):
        m = jnp.max(s, axis=0, keepdims=True)
        ix = jnp.min(jnp.where(s == m, iota, n), axis=0, keepdims=True)
        vals.append(m)
        idxs.append(ix)
        s = jnp.where(iota == ix, -jnp.inf, s)
    return vals, idxs


def _router_kernel(x_ref, nffn_ref, wq_ref, keys_ref, h2_ref, idx_ref, gate_ref, q_ref, idxt_ref, gatet_ref):
    h2 = _rms(x_ref[...], nffn_ref[...]).astype(jnp.bfloat16)
    h2_ref[...] = _pack_halves(pltpu.bitcast(h2.astype(jnp.float32), jnp.uint32))
    q_ref[...] = jnp.dot(h2, wq_ref[...], preferred_element_type=jnp.float32).astype(jnp.bfloat16)
    dq = 2 * PEER_D_HALF
    nt = (((1,), (1,)), ((), ()))

    def head(hd, carry):
        off = pl.multiple_of(hd * dq, dq)
        s1 = lax.dot_general(keys_ref[0], q_ref[:, pl.ds(off, PEER_D_HALF)], nt,
                             preferred_element_type=jnp.float32)
        s2 = lax.dot_general(keys_ref[1], q_ref[:, pl.ds(off + PEER_D_HALF, PEER_D_HALF)], nt,
                             preferred_element_type=jnp.float32)
        v1, i1 = _topk_rows(s1, PEER_TOPK)
        v2, i2 = _topk_rows(s2, PEER_TOPK)
        v2c = jnp.concatenate(v2, axis=0)
        i2c = jnp.concatenate(i2, axis=0)
        cand = jnp.concatenate([v1[a] + v2c for a in range(PEER_TOPK)], axis=0)
        expert = jnp.concatenate([i1[a] * PEER_N_KEYS + i2c for a in range(PEER_TOPK)], axis=0)
        iota = lax.broadcasted_iota(jnp.int32, cand.shape, 0)
        n = cand.shape[0]
        cv, ce = [], []
        for _ in range(PEER_TOPK):
            m = jnp.max(cand, axis=0, keepdims=True)
            ix = jnp.min(jnp.where(cand == m, iota, n), axis=0, keepdims=True)
            hit = iota == ix
            cv.append(m)
            ce.append(jnp.max(jnp.where(hit, expert, -1), axis=0, keepdims=True))
            cand = jnp.where(hit, -jnp.inf, cand)
        cvc = jnp.concatenate(cv, axis=0)
        e = jnp.exp(cvc - cv[0])
        gate = e / jnp.sum(e, axis=0, keepdims=True)
        row = pl.multiple_of(hd * PEER_TOPK, PEER_TOPK)
        idxt_ref[pl.ds(row, PEER_TOPK), :] = jnp.concatenate(ce, axis=0)
        gatet_ref[pl.ds(row, PEER_TOPK), :] = gate
        return carry

    lax.fori_loop(0, PEER_HEADS, head, 0)
    idx_ref[...] = idxt_ref[...].T
    gate_ref[...] = gatet_ref[...].T


def _router(x1, norm_ffn, wq, keys, tok0, T):
    D = x1.shape[1]
    tr = min(ROUTE_TILE, T)
    qw = wq.shape[1]
    assert tok0 % tr == 0 and T % tr == 0
    first = tok0 // tr
    full = lambda shape: pl.BlockSpec(shape, lambda i: (0,) * len(shape))
    return pl.pallas_call(
        _router_kernel,
        grid=(T // tr,),
        in_specs=[
            pl.BlockSpec((tr, D), lambda i: (first + i, 0)),
            full((1, D)),
            full((D, qw)),
            full((2, PEER_N_KEYS, PEER_D_HALF)),
        ],
        out_specs=[
            pl.BlockSpec((tr, D // 2), lambda i: (i, 0)),
            pl.BlockSpec((tr, EXPERTS_PER_TOKEN), lambda i: (i, 0)),
            pl.BlockSpec((tr, EXPERTS_PER_TOKEN), lambda i: (i, 0)),
        ],
        out_shape=[
            jax.ShapeDtypeStruct((T, D // 2), jnp.uint32),
            jax.ShapeDtypeStruct((T, EXPERTS_PER_TOKEN), jnp.int32),
            jax.ShapeDtypeStruct((T, EXPERTS_PER_TOKEN), jnp.float32),
        ],
        scratch_shapes=[
            pltpu.VMEM((tr, qw), jnp.bfloat16),
            pltpu.VMEM((EXPERTS_PER_TOKEN, tr), jnp.int32),
            pltpu.VMEM((EXPERTS_PER_TOKEN, tr), jnp.float32),
        ],
        compiler_params=pltpu.CompilerParams(
            dimension_semantics=("parallel",), vmem_limit_bytes=TC_VMEM_LIMIT),
        name="router",
    )(x1, norm_ffn.reshape(1, D), wq, keys)


def _act_kernel(pre_ref, gate_ref, w_ref):
    w = (gate_ref[...] * _gelu(pre_ref[...])).astype(jnp.bfloat16).astype(jnp.float32)
    hi = pltpu.bitcast(w, jnp.uint32) & jnp.uint32(HI_HALF)
    w_ref[...] = pltpu.bitcast(hi | (hi >> 16), jnp.int32)


def _expert_weights(pre, gate):
    T, K = pre.shape
    te = min(EW_TILE, T)
    spec = pl.BlockSpec((te, K), lambda i: (i, 0))
    return pl.pallas_call(
        _act_kernel, grid=(T // te,), in_specs=[spec, spec], out_specs=spec,
        out_shape=jax.ShapeDtypeStruct((T, K), jnp.int32),
        compiler_params=pltpu.CompilerParams(dimension_semantics=("parallel",)),
        name="expert_weights",
    )(pre, gate)


def _final_kernel(x_ref, y_ref, g_ref, *rest):
    o_ref = rest[-1]
    o_ref[...] = _rms(x_ref[...] + y_ref[...], g_ref[...])


def _final(x1, peer, norm_final, out_prev, tok0):
    Tc, D = peer.shape
    te = min(EW_TILE, Tc)
    assert tok0 % te == 0 and Tc % te == 0
    nblk = Tc // te
    spec = pl.BlockSpec((te, D), lambda i: (tok0 // te + i, 0))
    in_specs = [spec, pl.BlockSpec((te, D), lambda i: (i, 0)), pl.BlockSpec((1, D), lambda i: (0, 0))]
    args = [x1, peer, norm_final.reshape(1, D)]
    aliases = {}
    if out_prev is not None:
        in_specs.append(pl.BlockSpec(memory_space=pl.ANY))
        args.append(out_prev)
        aliases = {3: 0}
    return pl.pallas_call(
        _final_kernel, grid=(nblk,), in_specs=in_specs, out_specs=spec,
        out_shape=jax.ShapeDtypeStruct(x1.shape, jnp.float32),
        input_output_aliases=aliases,
        compiler_params=pltpu.CompilerParams(dimension_semantics=("parallel",)),
        name="final_norm",
    )(*args)


def _tree_sum(vals):
    while len(vals) > 1:
        nxt = [vals[i] + vals[i + 1] for i in range(0, len(vals) - 1, 2)]
        if len(vals) % 2:
            nxt.append(vals[-1])
        vals = nxt
    return vals[0]


def _sc_block_pipeline(nblk, items_per_token, loads, store, gather, compute):
    assert items_per_token % 2 == 0 and nblk >= 1

    for c in loads(0, 0):
        c.start()
    for c in loads(0, 0):
        c.wait()
    if nblk > 1:
        for c in loads(1, 1):
            c.start()
    gather(0, 0, 0, 0).start()

    @pl.loop(0, nblk)
    def _(b):
        slot = b % 2

        @pl.when(b >= 2)
        def _():
            store(b - 2, slot).wait()

        @pl.loop(0, SC_TOKENS)
        def _(t):
            for q in range(items_per_token):
                buf = q % 2
                if q + 1 < items_per_token:
                    gather(slot, t, q + 1, 1 - buf).start()
                else:
                    @pl.when(t + 1 < SC_TOKENS)
                    def _():
                        gather(slot, t + 1, 0, 1 - buf).start()

                    @pl.when(jnp.logical_and(t + 1 == SC_TOKENS, b + 1 < nblk))
                    def _():
                        for c in loads(b + 1, 1 - slot):
                            c.wait()
                        gather(1 - slot, 0, 0, 1 - buf).start()

                gather(slot, t, q, buf).wait()
                compute(slot, t, q, buf)

        store(b, slot).start()

        @pl.when(b + 2 < nblk)
        def _():
            for c in loads(b + 2, slot):
                c.start()

    if nblk >= 2:
        store(nblk - 2, nblk % 2).wait()
    store(nblk - 1, (nblk - 1) % 2).wait()


def _sc_mesh():
    return plsc.VectorSubcoreMesh(core_axis_name="c", subcore_axis_name="s")


def _sc_worker_id():
    return lax.axis_index("s") * V7X_SC_CORES + lax.axis_index("c")


def _sc_bf16(words):
    return plsc.bitcast(words, jnp.bfloat16)


def _sc_halves_f32(pairs):
    words = plsc.bitcast(pairs, jnp.uint32)
    return (plsc.bitcast(words << 16, jnp.float32),
            plsc.bitcast(words & jnp.uint32(HI_HALF), jnp.float32))


def _expert_scores(h2, idx, table):
    T, DW = h2.shape
    K = idx.shape[1]
    L, G = V7X_SC_LANES, SC_GATHER
    nj = DW // L
    tok_per_w = T // V7X_SC_WORKERS
    assert T % (V7X_SC_WORKERS * SC_TOKENS) == 0 and K % (2 * G) == 0 and nj % 4 == 0

    @functools.partial(
        pl.kernel, mesh=_sc_mesh(),
        out_type=jax.ShapeDtypeStruct((T, K), jnp.float32),
        compiler_params=pltpu.CompilerParams(needs_layout_passes=False),
        scratch_types=[
            pltpu.VMEM((2, SC_TOKENS, K), jnp.int32),
            pltpu.VMEM((2, SC_TOKENS, DW), jnp.uint32),
            pltpu.VMEM((2, G, DW), jnp.uint32),
            pltpu.VMEM((G * L,), jnp.float32),
            pltpu.VMEM((2, SC_TOKENS, K), jnp.float32),
            pltpu.SemaphoreType.DMA((2,)),
            pltpu.SemaphoreType.DMA((2,)),
            pltpu.SemaphoreType.DMA((2,)),
            pltpu.SemaphoreType.DMA((2,)),
        ],
        name="expert_scores")
    def k(h2_hbm, idx_hbm, tab_hbm, pre_hbm, idx_v, x_v, rows_v, part_v, pre_v,
          idx_sems, x_sems, out_sems, row_sems):
        base = _sc_worker_id() * tok_per_w

        def loads(b, slot):
            toks = pl.ds(base + b * SC_TOKENS, SC_TOKENS)
            return [pltpu.make_async_copy(idx_hbm.at[toks], idx_v.at[slot], idx_sems.at[slot]),
                    pltpu.make_async_copy(h2_hbm.at[toks], x_v.at[slot], x_sems.at[slot])]

        def store(b, slot):
            toks = pl.ds(base + b * SC_TOKENS, SC_TOKENS)
            return pltpu.make_async_copy(pre_v.at[slot], pre_hbm.at[toks], out_sems.at[slot])

        def gather(slot, t, q, buf):
            return pltpu.make_async_copy(
                tab_hbm.at[idx_v.at[slot, t, pl.ds(q * G, G)]], rows_v.at[buf], row_sems.at[buf])

        def compute(slot, t, q, buf):
            xs = [_sc_bf16(x_v[slot, t, pl.ds(j * L, L)]) for j in range(nj)]

            @plsc.parallel_loop(0, G, unroll=SC_UNROLL)
            def _(kk):
                prods = [_sc_bf16(rows_v[buf, kk, pl.ds(j * L, L)]) * xs[j] for j in range(nj)]
                quads = [(prods[j] + prods[j + 1]) + (prods[j + 2] + prods[j + 3]) for j in range(0, nj, 4)]
                part_v[pl.ds(pl.multiple_of(kk * L, L), L)] = _tree_sum(
                    [h for p in quads for h in _sc_halves_f32(p)])

            lane = lax.iota(jnp.int32, L) * L
            for g in range(G // L):
                cols = [plsc.load_gather(part_v, [lane + (g * L * L + j)]) for j in range(L)]
                pre_v[slot, t, pl.ds(q * G + g * L, L)] = _tree_sum(cols)

        _sc_block_pipeline(tok_per_w // SC_TOKENS, K // G, loads, store, gather, compute)

    return k(h2, idx, table)


def _expert_mix(w, idx, table):
    T, K = w.shape
    DW = table.shape[1]
    D = 2 * DW
    L, G = V7X_SC_LANES, SC_GATHER
    nj = SC_MIX_CHUNK // L
    tok_per_w = T // V7X_SC_WORKERS
    assert T % (V7X_SC_WORKERS * SC_TOKENS) == 0 and K % (2 * G) == 0
    assert DW % SC_MIX_CHUNK == 0 and G % SC_MIX_GROUP == 0

    @functools.partial(
        pl.kernel, mesh=_sc_mesh(),
        out_type=jax.ShapeDtypeStruct((T, D), jnp.float32),
        compiler_params=pltpu.CompilerParams(needs_layout_passes=False),
        scratch_types=[
            pltpu.VMEM((2, SC_TOKENS, K), jnp.int32),
            pltpu.VMEM((2, SC_TOKENS, K), jnp.int32),
            pltpu.VMEM((2, G, DW), jnp.uint32),
            pltpu.VMEM((2, SC_TOKENS, D), jnp.float32),
            pltpu.SemaphoreType.DMA((2,)),
            pltpu.SemaphoreType.DMA((2,)),
            pltpu.SemaphoreType.DMA((2,)),
            pltpu.SemaphoreType.DMA((2,)),
        ],
        name="expert_mix")
    def k(w_hbm, idx_hbm, tab_hbm, out_hbm, idx_v, w_v, rows_v, out_v,
          idx_sems, w_sems, out_sems, row_sems):
        base = _sc_worker_id() * tok_per_w

        def loads(b, slot):
            toks = pl.ds(base + b * SC_TOKENS, SC_TOKENS)
            return [pltpu.make_async_copy(idx_hbm.at[toks], idx_v.at[slot], idx_sems.at[slot]),
                    pltpu.make_async_copy(w_hbm.at[toks], w_v.at[slot], w_sems.at[slot])]

        def store(b, slot):
            toks = pl.ds(base + b * SC_TOKENS, SC_TOKENS)
            return pltpu.make_async_copy(out_v.at[slot], out_hbm.at[toks], out_sems.at[slot])

        def gather(slot, t, q, buf):
            return pltpu.make_async_copy(
                tab_hbm.at[idx_v.at[slot, t, pl.ds(q * G, G)]], rows_v.at[buf], row_sems.at[buf])

        def compute(slot, t, q, buf):
            ssplat = jnp.full((L,), slot, jnp.int32)
            tsplat = jnp.full((L,), t, jnp.int32)
            for c in range(DW // SC_MIX_CHUNK):
                def body(kg, acc):
                    kk = kg * SC_MIX_GROUP
                    wks = [_sc_bf16(plsc.load_gather(
                        w_v, [ssplat, tsplat, jnp.full((L,), q * G + i, jnp.int32) + kk]))
                        for i in range(SC_MIX_GROUP)]
                    out = []
                    for j in range(nj):
                        prods = [wks[i] * _sc_bf16(rows_v[buf, kk + i, pl.ds(c * SC_MIX_CHUNK + j * L, L)])
                                 for i in range(SC_MIX_GROUP)]
                        lo, hi = _sc_halves_f32(_tree_sum(prods))
                        out += [acc[2 * j] + lo, acc[2 * j + 1] + hi]
                    return tuple(out)

                zero = jnp.zeros((L,), jnp.float32)
                acc = plsc.parallel_loop(0, G // SC_MIX_GROUP, carry=(zero,) * (2 * nj))(body)
                for j in range(nj):
                    for half in range(2):
                        dst = out_v.at[slot, t, pl.ds(half * DW + c * SC_MIX_CHUNK + j * L, L)]
                        if q == 0:
                            dst[...] = acc[2 * j + half]
                        else:
                            plsc.addupdate(dst, acc[2 * j + half])

        _sc_block_pipeline(tok_per_w // SC_TOKENS, K // G, loads, store, gather, compute)

    return k(w, idx, table)


def _chunk_sizes(total):
    sizes, size = [], FIRST_CHUNK
    while total > 0:
        sizes.append(min(size, total))
        total -= sizes[-1]
        size = min(2 * size, MAX_CHUNK)
    return sizes


def kernel(x, norm_mix, w_in, pool_w, pool_scale, sgu_ln_g, sgu_ln_b, sgu_w, sgu_b, out_norm_pool,
           out_norm_sgu, w_out, norm_ffn, peer_wq, peer_keys, peer_u, peer_v, norm_final):
    B, S, D = x.shape
    assert norm_mix.shape[0] == 1, "single-layer block"
    T = B * S
    x1 = _mixer(x, norm_mix[0], w_in[0], pool_w[0], pool_scale[0], sgu_ln_g[0], sgu_ln_b[0],
                sgu_w[0], sgu_b[0], out_norm_pool[0], out_norm_sgu[0], w_out[0]).reshape(T, D)
    wq = peer_wq[0].astype(jnp.bfloat16)
    keys = peer_keys[0].astype(jnp.bfloat16)
    u_tab = _pack_table(peer_u[0])
    v_tab = _pack_table(peer_v[0])
    out = None
    tok0 = 0
    for tc in _chunk_sizes(T):
        h2, idx, gate = _router(x1, norm_ffn[0], wq, keys, tok0, tc)
        pre = _expert_scores(h2, idx, u_tab)
        w = _expert_weights(pre, gate)
        peer = _expert_mix(w, idx, v_tab)
        out = _final(x1, peer, norm_final, out, tok0)
        tok0 += tc
    return out.reshape(B, S, D)
```

```python
import functools
import math

import jax
import jax.numpy as jnp
from jax import lax
from jax.experimental import pallas as pl
from jax.experimental.pallas import tpu as pltpu
from jax.experimental.pallas import tpu_sc as plsc

POOL_WINDOWS = (2, 4, 8, 16)
N_POOL_GROUPS = len(POOL_WINDOWS)
SGU_HEADS = 4
SGU_CHUNK = 128
PEER_HEADS = 8
PEER_N_KEYS = 128
PEER_D_HALF = 128
PEER_TOPK = 16
NORM_EPS = 1e-6
EXPERTS_PER_TOKEN = PEER_HEADS * PEER_TOPK

V7X_LANES = 128
V7X_SUBLANES = 8
V7X_SC_CORES = 2
V7X_SC_SUBCORES = 16
V7X_SC_LANES = 16
V7X_SC_WORKERS = V7X_SC_CORES * V7X_SC_SUBCORES

HALO = max(POOL_WINDOWS)
MIX_TILE = 512
ROUTE_TILE = 512
EW_TILE = 512
SC_GATHER = 64
SC_TOKENS = 8
SC_MIX_CHUNK = 128
SC_MIX_GROUP = 4
SC_UNROLL = 2
HI_HALF = 0xFFFF0000
TC_VMEM_LIMIT = 48 * 1024 * 1024
FIRST_CHUNK = 1024
MAX_CHUNK = 8192


def _rms(x, g):
    inv = lax.rsqrt(jnp.mean(x * x, axis=-1, keepdims=True) + NORM_EPS)
    return x * inv * g


def _pack_halves(bits):
    half = bits.shape[1] // 2
    return (bits[:, :half] >> 16) | (bits[:, half:] & jnp.uint32(HI_HALF))


def _pack_table(a):
    return _pack_halves(lax.bitcast_convert_type(a.astype(jnp.bfloat16).astype(jnp.float32), jnp.uint32))


def _gelu(x):
    return 0.5 * x * (1.0 + lax.erf(x * math.sqrt(0.5)))


def _mixer_kernel(x_ref, xh_ref, nmix_ref, win_ref, poolw_ref, pscale_ref, lng_ref, lnb_ref,
                  sguw_ref, sgub_ref, onp_ref, ons_ref, wout_ref, o_ref, pext_ref, mix_ref):
    i = pl.program_id(1)
    ts = x_ref.shape[1]
    pool_w = pscale_ref.shape[1]
    gdim = pool_w // N_POOL_GROUPS
    sgu_w = lng_ref.shape[1]
    hdim = sgu_w // SGU_HEADS

    x = x_ref[0]
    h = _rms(x, nmix_ref[...]).astype(jnp.bfloat16)
    z = jnp.dot(h, win_ref[...], preferred_element_type=jnp.float32)
    p = z[:, :pool_w]

    hh = _rms(xh_ref[0], nmix_ref[...]).astype(jnp.bfloat16)
    ph = jnp.dot(hh, win_ref[:, :pool_w], preferred_element_type=jnp.float32)
    ph = jnp.where(i > 0, ph, 0.0)
    pext_ref[0:HALO, :] = ph
    pext_ref[HALO:HALO + ts, :] = p

    pos = i * ts + lax.broadcasted_iota(jnp.int32, (ts, 1), 0)
    ssq = jnp.zeros((ts, 1), jnp.float32)
    a_parts = []
    for g, win in enumerate(POOL_WINDOWS):
        cols = slice(g * gdim, (g + 1) * gdim)
        s = pext_ref[HALO:HALO + ts, cols]
        for j in range(1, win):
            s = s + pext_ref[HALO - j:HALO - j + ts, cols]
        cnt = jnp.minimum(pos + 1, win).astype(jnp.float32)
        d = (s / cnt - p[:, cols]).astype(jnp.bfloat16)
        a = jnp.dot(d, poolw_ref[g], preferred_element_type=jnp.float32) * pscale_ref[:, cols]
        ssq = ssq + jnp.sum(a * a, axis=-1, keepdims=True)
        a_parts.append(a)
    inv_a = lax.rsqrt(ssq / pool_w + NORM_EPS)
    for g in range(N_POOL_GROUPS):
        cols = slice(g * gdim, (g + 1) * gdim)
        mix_ref[:, cols] = (a_parts[g] * inv_a * onp_ref[:, cols]).astype(jnp.bfloat16)

    gz = _gelu(z[:, pool_w:])
    tril = (lax.broadcasted_iota(jnp.int32, (SGU_CHUNK, SGU_CHUNK), 0)
            >= lax.broadcasted_iota(jnp.int32, (SGU_CHUNK, SGU_CHUNK), 1))
    ssq = jnp.zeros((ts, 1), jnp.float32)
    b_parts = []
    for hd in range(SGU_HEADS):
        cols = slice(hd * hdim, (hd + 1) * hdim)
        u = gz[:, hd * hdim:(hd + 1) * hdim]
        v = gz[:, sgu_w + hd * hdim:sgu_w + (hd + 1) * hdim]
        mu = jnp.mean(v, axis=-1, keepdims=True)
        vc = v - mu
        var = jnp.mean(vc * vc, axis=-1, keepdims=True)
        vn = (vc * lax.rsqrt(var + NORM_EPS) * lng_ref[:, cols] + lnb_ref[:, cols]).astype(jnp.bfloat16)
        w = jnp.where(tril, sguw_ref[hd], jnp.zeros((), sguw_ref.dtype))
        mixed = [jnp.dot(w, vn[n * SGU_CHUNK:(n + 1) * SGU_CHUNK], preferred_element_type=jnp.float32)
                 + sgub_ref[hd] for n in range(ts // SGU_CHUNK)]
        b = u * jnp.concatenate(mixed, axis=0)
        ssq = ssq + jnp.sum(b * b, axis=-1, keepdims=True)
        b_parts.append(b)
    inv_b = lax.rsqrt(ssq / sgu_w + NORM_EPS)
    for hd in range(SGU_HEADS):
        cols = slice(hd * hdim, (hd + 1) * hdim)
        mix_ref[:, pool_w + hd * hdim:pool_w + (hd + 1) * hdim] = (
            b_parts[hd] * inv_b * ons_ref[:, cols]).astype(jnp.bfloat16)

    o_ref[0] = x + jnp.dot(mix_ref[...], wout_ref[...], preferred_element_type=jnp.float32)


def _mixer(x, norm_mix, w_in, pool_w, pool_scale, ln_g, ln_b, sgu_w, sgu_b, on_pool, on_sgu, w_out):
    B, S, D = x.shape
    ts = min(MIX_TILE, S)
    pool_width = pool_scale.size
    sgu_width = ln_g.size
    in_width = w_in.shape[1]
    gdim = pool_width // N_POOL_GROUPS
    halo_blocks = ts // HALO
    full = lambda shape: pl.BlockSpec(shape, lambda b, i: (0,) * len(shape))
    return pl.pallas_call(
        _mixer_kernel,
        grid=(B, S // ts),
        in_specs=[
            pl.BlockSpec((1, ts, D), lambda b, i: (b, i, 0)),
            pl.BlockSpec((1, HALO, D), lambda b, i: (b, jnp.maximum(i * halo_blocks - 1, 0), 0)),
            full((1, D)),
            full((D, in_width)),
            full((N_POOL_GROUPS, gdim, gdim)),
            full((1, pool_width)),
            full((1, sgu_width)),
            full((1, sgu_width)),
            full((SGU_HEADS, SGU_CHUNK, SGU_CHUNK)),
            full((SGU_HEADS, SGU_CHUNK, SGU_CHUNK)),
            full((1, pool_width)),
            full((1, sgu_width)),
            full((pool_width + sgu_width, D)),
        ],
        out_specs=pl.BlockSpec((1, ts, D), lambda b, i: (b, i, 0)),
        out_shape=jax.ShapeDtypeStruct((B, S, D), jnp.float32),
        scratch_shapes=[
            pltpu.VMEM((HALO + ts, pool_width), jnp.float32),
            pltpu.VMEM((ts, pool_width + sgu_width), jnp.bfloat16),
        ],
        compiler_params=pltpu.CompilerParams(
            dimension_semantics=("parallel", "arbitrary"), vmem_limit_bytes=TC_VMEM_LIMIT),
        name="mixer",
    )(x, x, norm_mix.reshape(1, D), w_in.astype(jnp.bfloat16), pool_w.astype(jnp.bfloat16),
      pool_scale.reshape(1, pool_width), ln_g.reshape(1, sgu_width), ln_b.reshape(1, sgu_width),
      sgu_w.astype(jnp.bfloat16),
      jnp.broadcast_to(sgu_b[:, :, None], (SGU_HEADS, SGU_CHUNK, SGU_CHUNK)),
      on_pool.reshape(1, pool_width), on_sgu.reshape(1, sgu_width), w_out.astype(jnp.bfloat16))


def _topk_rows(s, k):
    n = s.shape[0]
    iota = lax.broadcasted_iota(jnp.int32, s.shape, 0)
    vals, idxs = [], []
    for _ in range(k):
        m = jnp.max(s, axis=0, keepdims=True)
        ix = jnp.min(jnp.where(s == m, iota, n), axis=0, keepdims=True)
        vals.append(m)
        idxs.append(ix)
        s = jnp.where(iota == ix, -jnp.inf, s)
    return vals, idxs


def _pair_candidates(v1, i1, v2, i2):
    k = PEER_TOPK
    v2c, i2c = jnp.concatenate(v2, axis=0), jnp.concatenate(i2, axis=0)
    m = v2c.shape[1]
    vals, experts, flats = [], [], []
    a = 0
    while k // (a + 1) > 1:
        rows = -(-(k // (a + 1)) // V7X_SUBLANES) * V7X_SUBLANES
        vals.append(v1[a] + v2c[:rows])
        experts.append(i1[a] * PEER_N_KEYS + i2c[:rows])
        flats.append(a * k + lax.broadcasted_iota(jnp.int32, (rows, m), 0))
        a += 1
    vals.append(jnp.concatenate(v1[a:], axis=0) + v2[0])
    experts.append(jnp.concatenate(i1[a:], axis=0) * PEER_N_KEYS + i2[0])
    flats.append((a + lax.broadcasted_iota(jnp.int32, (k - a, m), 0)) * k)
    return jnp.concatenate(vals, axis=0), jnp.concatenate(experts, axis=0), jnp.concatenate(flats, axis=0)


def _router_kernel(x_ref, nffn_ref, wq_ref, keys_ref, h2_ref, idx_ref, gate_ref, q_ref, idxt_ref, gatet_ref):
    h2 = _rms(x_ref[...], nffn_ref[...]).astype(jnp.bfloat16)
    h2_ref[...] = _pack_halves(pltpu.bitcast(h2.astype(jnp.float32), jnp.uint32))
    q_ref[...] = jnp.dot(h2, wq_ref[...], preferred_element_type=jnp.float32).astype(jnp.bfloat16)
    dq = 2 * PEER_D_HALF
    nt = (((1,), (1,)), ((), ()))

    def head(hd, carry):
        off = pl.multiple_of(hd * dq, dq)
        s1 = lax.dot_general(keys_ref[0], q_ref[:, pl.ds(off, PEER_D_HALF)], nt,
                             preferred_element_type=jnp.float32)
        s2 = lax.dot_general(keys_ref[1], q_ref[:, pl.ds(off + PEER_D_HALF, PEER_D_HALF)], nt,
                             preferred_element_type=jnp.float32)
        v1, i1 = _topk_rows(s1, PEER_TOPK)
        v2, i2 = _topk_rows(s2, PEER_TOPK)
        cand, expert, flat = _pair_candidates(v1, i1, v2, i2)
        cv, ce = [], []
        for _ in range(PEER_TOPK):
            m = jnp.max(cand, axis=0, keepdims=True)
            ix = jnp.min(jnp.where(cand == m, flat, PEER_TOPK * PEER_TOPK), axis=0, keepdims=True)
            hit = flat == ix
            cv.append(m)
            ce.append(jnp.max(jnp.where(hit, expert, -1), axis=0, keepdims=True))
            cand = jnp.where(hit, -jnp.inf, cand)
        cvc = jnp.concatenate(cv, axis=0)
        e = jnp.exp(cvc - cv[0])
        gate = e / jnp.sum(e, axis=0, keepdims=True)
        row = pl.multiple_of(hd * PEER_TOPK, PEER_TOPK)
        idxt_ref[pl.ds(row, PEER_TOPK), :] = jnp.concatenate(ce, axis=0)
        gatet_ref[pl.ds(row, PEER_TOPK), :] = gate
        return carry

    lax.fori_loop(0, PEER_HEADS, head, 0)
    idx_ref[...] = idxt_ref[...].T
    gate_ref[...] = gatet_ref[...].T


def _router(x1, norm_ffn, wq, keys, tok0, T):
    D = x1.shape[1]
    tr = min(ROUTE_TILE, T)
    qw = wq.shape[1]
    assert tok0 % tr == 0 and T % tr == 0
    first = tok0 // tr
    full = lambda shape: pl.BlockSpec(shape, lambda i: (0,) * len(shape))
    return pl.pallas_call(
        _router_kernel,
        grid=(T // tr,),
        in_specs=[
            pl.BlockSpec((tr, D), lambda i: (first + i, 0)),
            full((1, D)),
            full((D, qw)),
            full((2, PEER_N_KEYS, PEER_D_HALF)),
        ],
        out_specs=[
            pl.BlockSpec((tr, D // 2), lambda i: (i, 0)),
            pl.BlockSpec((tr, EXPERTS_PER_TOKEN), lambda i: (i, 0)),
            pl.BlockSpec((tr, EXPERTS_PER_TOKEN), lambda i: (i, 0)),
        ],
        out_shape=[
            jax.ShapeDtypeStruct((T, D // 2), jnp.uint32),
            jax.ShapeDtypeStruct((T, EXPERTS_PER_TOKEN), jnp.int32),
            jax.ShapeDtypeStruct((T, EXPERTS_PER_TOKEN), jnp.float32),
        ],
        scratch_shapes=[
            pltpu.VMEM((tr, qw), jnp.bfloat16),
            pltpu.VMEM((EXPERTS_PER_TOKEN, tr), jnp.int32),
            pltpu.VMEM((EXPERTS_PER_TOKEN, tr), jnp.float32),
        ],
        compiler_params=pltpu.CompilerParams(
            dimension_semantics=("parallel",), vmem_limit_bytes=TC_VMEM_LIMIT),
        name="router",
    )(x1, norm_ffn.reshape(1, D), wq, keys)


def _act_kernel(pre_ref, gate_ref, w_ref):
    w = (gate_ref[...] * _gelu(pre_ref[...])).astype(jnp.bfloat16).astype(jnp.float32)
    hi = pltpu.bitcast(w, jnp.uint32) & jnp.uint32(HI_HALF)
    w_ref[...] = pltpu.bitcast(hi | (hi >> 16), jnp.int32)


def _expert_weights(pre, gate):
    T, K = pre.shape
    te = min(EW_TILE, T)
    spec = pl.BlockSpec((te, K), lambda i: (i, 0))
    return pl.pallas_call(
        _act_kernel, grid=(T // te,), in_specs=[spec, spec], out_specs=spec,
        out_shape=jax.ShapeDtypeStruct((T, K), jnp.int32),
        compiler_params=pltpu.CompilerParams(dimension_semantics=("parallel",)),
        name="expert_weights",
    )(pre, gate)


def _final_kernel(x_ref, y_ref, g_ref, *rest):
    o_ref = rest[-1]
    o_ref[...] = _rms(x_ref[...] + y_ref[...], g_ref[...])


def _final(x1, peer, norm_final, out_prev, tok0):
    Tc, D = peer.shape
    te = min(EW_TILE, Tc)
    assert tok0 % te == 0 and Tc % te == 0
    nblk = Tc // te
    spec = pl.BlockSpec((te, D), lambda i: (tok0 // te + i, 0))
    in_specs = [spec, pl.BlockSpec((te, D), lambda i: (i, 0)), pl.BlockSpec((1, D), lambda i: (0, 0))]
    args = [x1, peer, norm_final.reshape(1, D)]
    aliases = {}
    if out_prev is not None:
        in_specs.append(pl.BlockSpec(memory_space=pl.ANY))
        args.append(out_prev)
        aliases = {3: 0}
    return pl.pallas_call(
        _final_kernel, grid=(nblk,), in_specs=in_specs, out_specs=spec,
        out_shape=jax.ShapeDtypeStruct(x1.shape, jnp.float32),
        input_output_aliases=aliases,
        compiler_params=pltpu.CompilerParams(dimension_semantics=("parallel",)),
        name="final_norm",
    )(*args)


def _tree_sum(vals):
    while len(vals) > 1:
        nxt = [vals[i] + vals[i + 1] for i in range(0, len(vals) - 1, 2)]
        if len(vals) % 2:
            nxt.append(vals[-1])
        vals = nxt
    return vals[0]


def _sc_block_pipeline(nblk, items_per_token, loads, store, gather, compute):
    assert items_per_token % 2 == 0 and nblk >= 1

    for c in loads(0, 0):
        c.start()
    for c in loads(0, 0):
        c.wait()
    if nblk > 1:
        for c in loads(1, 1):
            c.start()
    gather(0, 0, 0, 0).start()

    @pl.loop(0, nblk)
    def _(b):
        slot = b % 2

        @pl.when(b >= 2)
        def _():
            store(b - 2, slot).wait()

        @pl.loop(0, SC_TOKENS)
        def _(t):
            for q in range(items_per_token):
                buf = q % 2
                if q + 1 < items_per_token:
                    gather(slot, t, q + 1, 1 - buf).start()
                else:
                    @pl.when(t + 1 < SC_TOKENS)
                    def _():
                        gather(slot, t + 1, 0, 1 - buf).start()

                    @pl.when(jnp.logical_and(t + 1 == SC_TOKENS, b + 1 < nblk))
                    def _():
                        for c in loads(b + 1, 1 - slot):
                            c.wait()
                        gather(1 - slot, 0, 0, 1 - buf).start()

                gather(slot, t, q, buf).wait()
                compute(slot, t, q, buf)

        store(b, slot).start()

        @pl.when(b + 2 < nblk)
        def _():
            for c in loads(b + 2, slot):
                c.start()

    if nblk >= 2:
        store(nblk - 2, nblk % 2).wait()
    store(nblk - 1, (nblk - 1) % 2).wait()


def _sc_mesh():
    return plsc.VectorSubcoreMesh(core_axis_name="c", subcore_axis_name="s")


def _sc_worker_id():
    return lax.axis_index("s") * V7X_SC_CORES + lax.axis_index("c")


def _sc_bf16(words):
    return plsc.bitcast(words, jnp.bfloat16)


def _sc_halves_f32(pairs):
    words = plsc.bitcast(pairs, jnp.uint32)
    return (plsc.bitcast(words << 16, jnp.float32),
            plsc.bitcast(words & jnp.uint32(HI_HALF), jnp.float32))


def _expert_scores(h2, idx, table):
    T, DW = h2.shape
    K = idx.shape[1]
    L, G = V7X_SC_LANES, SC_GATHER
    nj = DW // L
    tok_per_w = T // V7X_SC_WORKERS
    assert T % (V7X_SC_WORKERS * SC_TOKENS) == 0 and K % (2 * G) == 0 and nj % 4 == 0

    @functools.partial(
        pl.kernel, mesh=_sc_mesh(),
        out_type=jax.ShapeDtypeStruct((T, K), jnp.float32),
        compiler_params=pltpu.CompilerParams(needs_layout_passes=False),
        scratch_types=[
            pltpu.VMEM((2, SC_TOKENS, K), jnp.int32),
            pltpu.VMEM((2, SC_TOKENS, DW), jnp.uint32),
            pltpu.VMEM((2, G, DW), jnp.uint32),
            pltpu.VMEM((G * L,), jnp.float32),
            pltpu.VMEM((2, SC_TOKENS, K), jnp.float32),
            pltpu.SemaphoreType.DMA((2,)),
            pltpu.SemaphoreType.DMA((2,)),
            pltpu.SemaphoreType.DMA((2,)),
            pltpu.SemaphoreType.DMA((2,)),
        ],
        name="expert_scores")
    def k(h2_hbm, idx_hbm, tab_hbm, pre_hbm, idx_v, x_v, rows_v, part_v, pre_v,
          idx_sems, x_sems, out_sems, row_sems):
        base = _sc_worker_id() * tok_per_w

        def loads(b, slot):
            toks = pl.ds(base + b * SC_TOKENS, SC_TOKENS)
            return [pltpu.make_async_copy(idx_hbm.at[toks], idx_v.at[slot], idx_sems.at[slot]),
                    pltpu.make_async_copy(h2_hbm.at[toks], x_v.at[slot], x_sems.at[slot])]

        def store(b, slot):
            toks = pl.ds(base + b * SC_TOKENS, SC_TOKENS)
            return pltpu.make_async_copy(pre_v.at[slot], pre_hbm.at[toks], out_sems.at[slot])

        def gather(slot, t, q, buf):
            return pltpu.make_async_copy(
                tab_hbm.at[idx_v.at[slot, t, pl.ds(q * G, G)]], rows_v.at[buf], row_sems.at[buf])

        def compute(slot, t, q, buf):
            xs = [_sc_bf16(x_v[slot, t, pl.ds(j * L, L)]) for j in range(nj)]

            @plsc.parallel_loop(0, G, unroll=SC_UNROLL)
            def _(kk):
                prods = [_sc_bf16(rows_v[buf, kk, pl.ds(j * L, L)]) * xs[j] for j in range(nj)]
                quads = [(prods[j] + prods[j + 1]) + (prods[j + 2] + prods[j + 3]) for j in range(0, nj, 4)]
                part_v[pl.ds(pl.multiple_of(kk * L, L), L)] = _tree_sum(
                    [h for p in quads for h in _sc_halves_f32(p)])

            lane = lax.iota(jnp.int32, L) * L
            for g in range(G // L):
                cols = [plsc.load_gather(part_v, [lane + (g * L * L + j)]) for j in range(L)]
                pre_v[slot, t, pl.ds(q * G + g * L, L)] = _tree_sum(cols)

        _sc_block_pipeline(tok_per_w // SC_TOKENS, K // G, loads, store, gather, compute)

    return k(h2, idx, table)


def _expert_mix(w, idx, table):
    T, K = w.shape
    DW = table.shape[1]
    D = 2 * DW
    L, G = V7X_SC_LANES, SC_GATHER
    nj = SC_MIX_CHUNK // L
    tok_per_w = T // V7X_SC_WORKERS
    assert T % (V7X_SC_WORKERS * SC_TOKENS) == 0 and K % (2 * G) == 0
    assert DW % SC_MIX_CHUNK == 0 and G % SC_MIX_GROUP == 0

    @functools.partial(
        pl.kernel, mesh=_sc_mesh(),
        out_type=jax.ShapeDtypeStruct((T, D), jnp.float32),
        compiler_params=pltpu.CompilerParams(needs_layout_passes=False),
        scratch_types=[
            pltpu.VMEM((2, SC_TOKENS, K), jnp.int32),
            pltpu.VMEM((2, SC_TOKENS, K), jnp.int32),
            pltpu.VMEM((2, G, DW), jnp.uint32),
            pltpu.VMEM((2, SC_TOKENS, D), jnp.float32),
            pltpu.SemaphoreType.DMA((2,)),
            pltpu.SemaphoreType.DMA((2,)),
            pltpu.SemaphoreType.DMA((2,)),
            pltpu.SemaphoreType.DMA((2,)),
        ],
        name="expert_mix")
    def k(w_hbm, idx_hbm, tab_hbm, out_hbm, idx_v, w_v, rows_v, out_v,
          idx_sems, w_sems, out_sems, row_sems):
        base = _sc_worker_id() * tok_per_w

        def loads(b, slot):
            toks = pl.ds(base + b * SC_TOKENS, SC_TOKENS)
            return [pltpu.make_async_copy(idx_hbm.at[toks], idx_v.at[slot], idx_sems.at[slot]),
                    pltpu.make_async_copy(w_hbm.at[toks], w_v.at[slot], w_sems.at[slot])]

        def store(b, slot):
            toks = pl.ds(base + b * SC_TOKENS, SC_TOKENS)
            return pltpu.make_async_copy(out_v.at[slot], out_hbm.at[toks], out_sems.at[slot])

        def gather(slot, t, q, buf):
            return pltpu.make_async_copy(
                tab_hbm.at[idx_v.at[slot, t, pl.ds(q * G, G)]], rows_v.at[buf], row_sems.at[buf])

        def compute(slot, t, q, buf):
            ssplat = jnp.full((L,), slot, jnp.int32)
            tsplat = jnp.full((L,), t, jnp.int32)
            for c in range(DW // SC_MIX_CHUNK):
                def body(kg, acc):
                    kk = kg * SC_MIX_GROUP
                    wks = [_sc_bf16(plsc.load_gather(
                        w_v, [ssplat, tsplat, jnp.full((L,), q * G + i, jnp.int32) + kk]))
                        for i in range(SC_MIX_GROUP)]
                    out = []
                    for j in range(nj):
                        prods = [wks[i] * _sc_bf16(rows_v[buf, kk + i, pl.ds(c * SC_MIX_CHUNK + j * L, L)])
                                 for i in range(SC_MIX_GROUP)]
                        lo, hi = _sc_halves_f32(_tree_sum(prods))
                        out += [acc[2 * j] + lo, acc[2 * j + 1] + hi]
                    return tuple(out)

                zero = jnp.zeros((L,), jnp.float32)
                acc = plsc.parallel_loop(0, G // SC_MIX_GROUP, carry=(zero,) * (2 * nj))(body)
                for j in range(nj):
                    for half in range(2):
                        dst = out_v.at[slot, t, pl.ds(half * DW + c * SC_MIX_CHUNK + j * L, L)]
                        if q == 0:
                            dst[...] = acc[2 * j + half]
                        else:
                            plsc.addupdate(dst, acc[2 * j + half])

        _sc_block_pipeline(tok_per_w // SC_TOKENS, K // G, loads, store, gather, compute)

    return k(w, idx, table)


def _chunk_sizes(total):
    sizes, size = [], FIRST_CHUNK
    while total > 0:
        sizes.append(min(size, total))
        total -= sizes[-1]
        size = min(2 * size, MAX_CHUNK)
    return sizes


def kernel(x, norm_mix, w_in, pool_w, pool_scale, sgu_ln_g, sgu_ln_b, sgu_w, sgu_b, out_norm_pool,
           out_norm_sgu, w_out, norm_ffn, peer_wq, peer_keys, peer_u, peer_v, norm_final):
    B, S, D = x.shape
    assert norm_mix.shape[0] == 1, "single-layer block"
    T = B * S
    x1 = _mixer(x, norm_mix[0], w_in[0], pool_w[0], pool_scale[0], sgu_ln_g[0], sgu_ln_b[0],
                sgu_w[0], sgu_b[0], out_norm_pool[0], out_norm_sgu[0], w_out[0]).reshape(T, D)
    wq = peer_wq[0].astype(jnp.bfloat16)
    keys = peer_keys[0].astype(jnp.bfloat16)
    u_tab = _pack_table(peer_u[0])
    v_tab = _pack_table(peer_v[0])
    out = None
    tok0 = 0
    for tc in _chunk_sizes(T):
        h2, idx, gate = _router(x1, norm_ffn[0], wq, keys, tok0, tc)
        pre = _expert_scores(h2, idx, u_tab)
        w = _expert_weights(pre, gate)
        peer = _expert_mix(w, idx, v_tab)
        out = _final(x1, peer, norm_final, out, tok0)
        tok0 += tc
    return out.reshape(B, S, D)
```

```python
import functools
import math

import jax
import jax.numpy as jnp
from jax import lax
from jax.experimental import pallas as pl
from jax.experimental.pallas import tpu as pltpu
from jax.experimental.pallas import tpu_sc as plsc

POOL_WINDOWS = (2, 4, 8, 16)
N_POOL_GROUPS = len(POOL_WINDOWS)
SGU_HEADS = 4
SGU_CHUNK = 128
PEER_HEADS = 8
PEER_N_KEYS = 128
PEER_D_HALF = 128
PEER_TOPK = 16
NORM_EPS = 1e-6
EXPERTS_PER_TOKEN = PEER_HEADS * PEER_TOPK

V7X_LANES = 128
V7X_SUBLANES = 8
V7X_SC_CORES = 2
V7X_SC_SUBCORES = 16
V7X_SC_LANES = 16
V7X_SC_WORKERS = V7X_SC_CORES * V7X_SC_SUBCORES

HALO = max(POOL_WINDOWS)
MIX_TILE = 512
ROUTE_TILE = 512
EW_TILE = 512
SCORE_TOK_TILE = 1024
SCORE_EXP_TILE = 1024
SC_PICK_TOKENS = 32
SC_GATHER = 64
SC_TOKENS = 8
SC_MIX_CHUNK = 128
SC_MIX_GROUP = 4
SC_UNROLL = 2
HI_HALF = 0xFFFF0000
TC_VMEM_LIMIT = 48 * 1024 * 1024
FIRST_CHUNK = 1024
MAX_CHUNK = 8192


def _rms(x, g):
    inv = lax.rsqrt(jnp.mean(x * x, axis=-1, keepdims=True) + NORM_EPS)
    return x * inv * g


def _pack_halves(bits):
    half = bits.shape[1] // 2
    return (bits[:, :half] >> 16) | (bits[:, half:] & jnp.uint32(HI_HALF))


def _pack_table(a):
    return _pack_halves(lax.bitcast_convert_type(a.astype(jnp.bfloat16).astype(jnp.float32), jnp.uint32))


def _gelu(x):
    return 0.5 * x * (1.0 + lax.erf(x * math.sqrt(0.5)))


def _mixer_kernel(x_ref, xh_ref, nmix_ref, win_ref, poolw_ref, pscale_ref, lng_ref, lnb_ref,
                  sguw_ref, sgub_ref, onp_ref, ons_ref, wout_ref, o_ref, pext_ref, mix_ref):
    i = pl.program_id(1)
    ts = x_ref.shape[1]
    pool_w = pscale_ref.shape[1]
    gdim = pool_w // N_POOL_GROUPS
    sgu_w = lng_ref.shape[1]
    hdim = sgu_w // SGU_HEADS

    x = x_ref[0]
    h = _rms(x, nmix_ref[...]).astype(jnp.bfloat16)
    z = jnp.dot(h, win_ref[...], preferred_element_type=jnp.float32)
    p = z[:, :pool_w]

    hh = _rms(xh_ref[0], nmix_ref[...]).astype(jnp.bfloat16)
    ph = jnp.dot(hh, win_ref[:, :pool_w], preferred_element_type=jnp.float32)
    ph = jnp.where(i > 0, ph, 0.0)
    pext_ref[0:HALO, :] = ph
    pext_ref[HALO:HALO + ts, :] = p

    pos = i * ts + lax.broadcasted_iota(jnp.int32, (ts, 1), 0)
    ssq = jnp.zeros((ts, 1), jnp.float32)
    a_parts = []
    for g, win in enumerate(POOL_WINDOWS):
        cols = slice(g * gdim, (g + 1) * gdim)
        s = pext_ref[HALO:HALO + ts, cols]
        for j in range(1, win):
            s = s + pext_ref[HALO - j:HALO - j + ts, cols]
        cnt = jnp.minimum(pos + 1, win).astype(jnp.float32)
        d = (s / cnt - p[:, cols]).astype(jnp.bfloat16)
        a = jnp.dot(d, poolw_ref[g], preferred_element_type=jnp.float32) * pscale_ref[:, cols]
        ssq = ssq + jnp.sum(a * a, axis=-1, keepdims=True)
        a_parts.append(a)
    inv_a = lax.rsqrt(ssq / pool_w + NORM_EPS)
    for g in range(N_POOL_GROUPS):
        cols = slice(g * gdim, (g + 1) * gdim)
        mix_ref[:, cols] = (a_parts[g] * inv_a * onp_ref[:, cols]).astype(jnp.bfloat16)

    gz = _gelu(z[:, pool_w:])
    tril = (lax.broadcasted_iota(jnp.int32, (SGU_CHUNK, SGU_CHUNK), 0)
            >= lax.broadcasted_iota(jnp.int32, (SGU_CHUNK, SGU_CHUNK), 1))
    ssq = jnp.zeros((ts, 1), jnp.float32)
    b_parts = []
    for hd in range(SGU_HEADS):
        cols = slice(hd * hdim, (hd + 1) * hdim)
        u = gz[:, hd * hdim:(hd + 1) * hdim]
        v = gz[:, sgu_w + hd * hdim:sgu_w + (hd + 1) * hdim]
        mu = jnp.mean(v, axis=-1, keepdims=True)
        vc = v - mu
        var = jnp.mean(vc * vc, axis=-1, keepdims=True)
        vn = (vc * lax.rsqrt(var + NORM_EPS) * lng_ref[:, cols] + lnb_ref[:, cols]).astype(jnp.bfloat16)
        w = jnp.where(tril, sguw_ref[hd], jnp.zeros((), sguw_ref.dtype))
        mixed = [jnp.dot(w, vn[n * SGU_CHUNK:(n + 1) * SGU_CHUNK], preferred_element_type=jnp.float32)
                 + sgub_ref[hd] for n in range(ts // SGU_CHUNK)]
        b = u * jnp.concatenate(mixed, axis=0)
        ssq = ssq + jnp.sum(b * b, axis=-1, keepdims=True)
        b_parts.append(b)
    inv_b = lax.rsqrt(ssq / sgu_w + NORM_EPS)
    for hd in range(SGU_HEADS):
        cols = slice(hd * hdim, (hd + 1) * hdim)
        mix_ref[:, pool_w + hd * hdim:pool_w + (hd + 1) * hdim] = (
            b_parts[hd] * inv_b * ons_ref[:, cols]).astype(jnp.bfloat16)

    o_ref[0] = x + jnp.dot(mix_ref[...], wout_ref[...], preferred_element_type=jnp.float32)


def _mixer(x, norm_mix, w_in, pool_w, pool_scale, ln_g, ln_b, sgu_w, sgu_b, on_pool, on_sgu, w_out):
    B, S, D = x.shape
    ts = min(MIX_TILE, S)
    pool_width = pool_scale.size
    sgu_width = ln_g.size
    in_width = w_in.shape[1]
    gdim = pool_width // N_POOL_GROUPS
    halo_blocks = ts // HALO
    full = lambda shape: pl.BlockSpec(shape, lambda b, i: (0,) * len(shape))
    return pl.pallas_call(
        _mixer_kernel,
        grid=(B, S // ts),
        in_specs=[
            pl.BlockSpec((1, ts, D), lambda b, i: (b, i, 0)),
            pl.BlockSpec((1, HALO, D), lambda b, i: (b, jnp.maximum(i * halo_blocks - 1, 0), 0)),
            full((1, D)),
            full((D, in_width)),
            full((N_POOL_GROUPS, gdim, gdim)),
            full((1, pool_width)),
            full((1, sgu_width)),
            full((1, sgu_width)),
            full((SGU_HEADS, SGU_CHUNK, SGU_CHUNK)),
            full((SGU_HEADS, SGU_CHUNK, SGU_CHUNK)),
            full((1, pool_width)),
            full((1, sgu_width)),
            full((pool_width + sgu_width, D)),
        ],
        out_specs=pl.BlockSpec((1, ts, D), lambda b, i: (b, i, 0)),
        out_shape=jax.ShapeDtypeStruct((B, S, D), jnp.float32),
        scratch_shapes=[
            pltpu.VMEM((HALO + ts, pool_width), jnp.float32),
            pltpu.VMEM((ts, pool_width + sgu_width), jnp.bfloat16),
        ],
        compiler_params=pltpu.CompilerParams(
            dimension_semantics=("parallel", "arbitrary"), vmem_limit_bytes=TC_VMEM_LIMIT),
        name="mixer",
    )(x, x, norm_mix.reshape(1, D), w_in.astype(jnp.bfloat16), pool_w.astype(jnp.bfloat16),
      pool_scale.reshape(1, pool_width), ln_g.reshape(1, sgu_width), ln_b.reshape(1, sgu_width),
      sgu_w.astype(jnp.bfloat16),
      jnp.broadcast_to(sgu_b[:, :, None], (SGU_HEADS, SGU_CHUNK, SGU_CHUNK)),
      on_pool.reshape(1, pool_width), on_sgu.reshape(1, sgu_width), w_out.astype(jnp.bfloat16))


def _topk_rows(s, k):
    n = s.shape[0]
    iota = lax.broadcasted_iota(jnp.int32, s.shape, 0)
    vals, idxs = [], []
    for _ in range(k):
        m = jnp.max(s, axis=0, keepdims=True)
        ix = jnp.min(jnp.where(s == m, iota, n), axis=0, keepdims=True)
        vals.append(m)
        idxs.append(ix)
        s = jnp.where(iota == ix, -jnp.inf, s)
    return vals, idxs


def _pair_candidates(v1, i1, v2, i2):
    k = PEER_TOPK
    v2c, i2c = jnp.concatenate(v2, axis=0), jnp.concatenate(i2, axis=0)
    m = v2c.shape[1]
    vals, experts, flats = [], [], []
    a = 0
    while k // (a + 1) > 1:
        rows = -(-(k // (a + 1)) // V7X_SUBLANES) * V7X_SUBLANES
        vals.append(v1[a] + v2c[:rows])
        experts.append(i1[a] * PEER_N_KEYS + i2c[:rows])
        flats.append(a * k + lax.broadcasted_iota(jnp.int32, (rows, m), 0))
        a += 1
    vals.append(jnp.concatenate(v1[a:], axis=0) + v2[0])
    experts.append(jnp.concatenate(i1[a:], axis=0) * PEER_N_KEYS + i2[0])
    flats.append((a + lax.broadcasted_iota(jnp.int32, (k - a, m), 0)) * k)
    return jnp.concatenate(vals, axis=0), jnp.concatenate(experts, axis=0), jnp.concatenate(flats, axis=0)


def _router_kernel(x_ref, nffn_ref, wq_ref, keys_ref, h2_ref, idx_ref, slot_ref, gate_ref,
                   q_ref, idxt_ref, gatet_ref):
    h2 = _rms(x_ref[...], nffn_ref[...]).astype(jnp.bfloat16)
    h2_ref[...] = h2
    q_ref[...] = jnp.dot(h2, wq_ref[...], preferred_element_type=jnp.float32).astype(jnp.bfloat16)
    dq = 2 * PEER_D_HALF
    nt = (((1,), (1,)), ((), ()))

    def head(hd, carry):
        off = pl.multiple_of(hd * dq, dq)
        s1 = lax.dot_general(keys_ref[0], q_ref[:, pl.ds(off, PEER_D_HALF)], nt,
                             preferred_element_type=jnp.float32)
        s2 = lax.dot_general(keys_ref[1], q_ref[:, pl.ds(off + PEER_D_HALF, PEER_D_HALF)], nt,
                             preferred_element_type=jnp.float32)
        v1, i1 = _topk_rows(s1, PEER_TOPK)
        v2, i2 = _topk_rows(s2, PEER_TOPK)
        cand, expert, flat = _pair_candidates(v1, i1, v2, i2)
        cv, ce = [], []
        for _ in range(PEER_TOPK):
            m = jnp.max(cand, axis=0, keepdims=True)
            ix = jnp.min(jnp.where(cand == m, flat, PEER_TOPK * PEER_TOPK), axis=0, keepdims=True)
            hit = flat == ix
            cv.append(m)
            ce.append(jnp.max(jnp.where(hit, expert, -1), axis=0, keepdims=True))
            cand = jnp.where(hit, -jnp.inf, cand)
        cvc = jnp.concatenate(cv, axis=0)
        e = jnp.exp(cvc - cv[0])
        gate = e / jnp.sum(e, axis=0, keepdims=True)
        row = pl.multiple_of(hd * PEER_TOPK, PEER_TOPK)
        idxt_ref[pl.ds(row, PEER_TOPK), :] = jnp.concatenate(ce, axis=0)
        gatet_ref[pl.ds(row, PEER_TOPK), :] = gate
        return carry

    lax.fori_loop(0, PEER_HEADS, head, 0)
    idx = idxt_ref[...].T
    idx_ref[...] = idx
    tok = pl.program_id(0) * idx.shape[0] + lax.broadcasted_iota(jnp.int32, idx.shape, 0)
    slot_ref[...] = _tile_linear_index(tok, idx, PEER_N_KEYS * PEER_N_KEYS)
    gate_ref[...] = gatet_ref[...].T


def _router(x1, norm_ffn, wq, keys, tok0, T):
    D = x1.shape[1]
    tr = min(ROUTE_TILE, T)
    qw = wq.shape[1]
    assert tok0 % tr == 0 and T % tr == 0
    first = tok0 // tr
    full = lambda shape: pl.BlockSpec(shape, lambda i: (0,) * len(shape))
    return pl.pallas_call(
        _router_kernel,
        grid=(T // tr,),
        in_specs=[
            pl.BlockSpec((tr, D), lambda i: (first + i, 0)),
            full((1, D)),
            full((D, qw)),
            full((2, PEER_N_KEYS, PEER_D_HALF)),
        ],
        out_specs=[
            pl.BlockSpec((tr, D), lambda i: (i, 0)),
            pl.BlockSpec((tr, EXPERTS_PER_TOKEN), lambda i: (i, 0)),
            pl.BlockSpec((tr, EXPERTS_PER_TOKEN), lambda i: (i, 0)),
            pl.BlockSpec((tr, EXPERTS_PER_TOKEN), lambda i: (i, 0)),
        ],
        out_shape=[
            jax.ShapeDtypeStruct((T, D), jnp.bfloat16),
            jax.ShapeDtypeStruct((T, EXPERTS_PER_TOKEN), jnp.int32),
            jax.ShapeDtypeStruct((T, EXPERTS_PER_TOKEN), jnp.int32),
            jax.ShapeDtypeStruct((T, EXPERTS_PER_TOKEN), jnp.float32),
        ],
        scratch_shapes=[
            pltpu.VMEM((tr, qw), jnp.bfloat16),
            pltpu.VMEM((EXPERTS_PER_TOKEN, tr), jnp.int32),
            pltpu.VMEM((EXPERTS_PER_TOKEN, tr), jnp.float32),
        ],
        compiler_params=pltpu.CompilerParams(
            dimension_semantics=("parallel",), vmem_limit_bytes=TC_VMEM_LIMIT),
        name="router",
    )(x1, norm_ffn.reshape(1, D), wq, keys)


def _tile_linear_index(row, col, ncols):
    return ((row >> 3) * (ncols * V7X_SUBLANES) + (col >> 7) * (V7X_SUBLANES * V7X_LANES)
            + (row & (V7X_SUBLANES - 1)) * V7X_LANES + (col & (V7X_LANES - 1)))


def _scores_kernel(h_ref, ut_ref, o_ref):
    acc = jnp.dot(h_ref[...], ut_ref[...], preferred_element_type=jnp.float32)
    tm, tn = acc.shape
    for n in range(tn // V7X_LANES):
        o_ref[:, n * V7X_SUBLANES:(n + 1) * V7X_SUBLANES, :] = acc[:, n * V7X_LANES:(n + 1) * V7X_LANES].reshape(
            tm // V7X_SUBLANES, V7X_SUBLANES, V7X_LANES)


def _dense_scores(h2, u_t):
    T, D = h2.shape
    E = u_t.shape[1]
    tm, tn = min(SCORE_TOK_TILE, T), min(SCORE_EXP_TILE, E)
    assert T % tm == 0 and E % tn == 0 and tm % V7X_SUBLANES == 0 and tn % V7X_LANES == 0
    out = pl.pallas_call(
        _scores_kernel,
        grid=(T // tm, E // tn),
        in_specs=[pl.BlockSpec((tm, D), lambda i, j: (i, 0)), pl.BlockSpec((D, tn), lambda i, j: (0, j))],
        out_specs=pl.BlockSpec((tm // V7X_SUBLANES, tn // V7X_LANES * V7X_SUBLANES, V7X_LANES),
                               lambda i, j: (i, j, 0)),
        out_shape=jax.ShapeDtypeStruct((T // V7X_SUBLANES, E // V7X_LANES * V7X_SUBLANES, V7X_LANES),
                                       jnp.float32),
        compiler_params=pltpu.CompilerParams(
            dimension_semantics=("parallel", "arbitrary"), vmem_limit_bytes=TC_VMEM_LIMIT),
        name="dense_scores",
    )(h2, u_t)
    return out.reshape(-1)


def _act_kernel(pre_ref, gate_ref, w_ref):
    w = (gate_ref[...] * _gelu(pre_ref[...])).astype(jnp.bfloat16).astype(jnp.float32)
    hi = pltpu.bitcast(w, jnp.uint32) & jnp.uint32(HI_HALF)
    w_ref[...] = pltpu.bitcast(hi | (hi >> 16), jnp.int32)


def _expert_weights(pre, gate):
    T, K = pre.shape
    te = min(EW_TILE, T)
    spec = pl.BlockSpec((te, K), lambda i: (i, 0))
    return pl.pallas_call(
        _act_kernel, grid=(T // te,), in_specs=[spec, spec], out_specs=spec,
        out_shape=jax.ShapeDtypeStruct((T, K), jnp.int32),
        compiler_params=pltpu.CompilerParams(dimension_semantics=("parallel",)),
        name="expert_weights",
    )(pre, gate)


def _final_kernel(x_ref, y_ref, g_ref, *rest):
    o_ref = rest[-1]
    o_ref[...] = _rms(x_ref[...] + y_ref[...], g_ref[...])


def _final(x1, peer, norm_final, out_prev, tok0):
    Tc, D = peer.shape
    te = min(EW_TILE, Tc)
    assert tok0 % te == 0 and Tc % te == 0
    nblk = Tc // te
    spec = pl.BlockSpec((te, D), lambda i: (tok0 // te + i, 0))
    in_specs = [spec, pl.BlockSpec((te, D), lambda i: (i, 0)), pl.BlockSpec((1, D), lambda i: (0, 0))]
    args = [x1, peer, norm_final.reshape(1, D)]
    aliases = {}
    if out_prev is not None:
        in_specs.append(pl.BlockSpec(memory_space=pl.ANY))
        args.append(out_prev)
        aliases = {3: 0}
    return pl.pallas_call(
        _final_kernel, grid=(nblk,), in_specs=in_specs, out_specs=spec,
        out_shape=jax.ShapeDtypeStruct(x1.shape, jnp.float32),
        input_output_aliases=aliases,
        compiler_params=pltpu.CompilerParams(dimension_semantics=("parallel",)),
        name="final_norm",
    )(*args)


def _tree_sum(vals):
    while len(vals) > 1:
        nxt = [vals[i] + vals[i + 1] for i in range(0, len(vals) - 1, 2)]
        if len(vals) % 2:
            nxt.append(vals[-1])
        vals = nxt
    return vals[0]


def _sc_block_pipeline(nblk, items_per_token, loads, store, gather, compute):
    assert items_per_token % 2 == 0 and nblk >= 1

    for c in loads(0, 0):
        c.start()
    for c in loads(0, 0):
        c.wait()
    if nblk > 1:
        for c in loads(1, 1):
            c.start()
    gather(0, 0, 0, 0).start()

    @pl.loop(0, nblk)
    def _(b):
        slot = b % 2

        @pl.when(b >= 2)
        def _():
            store(b - 2, slot).wait()

        @pl.loop(0, SC_TOKENS)
        def _(t):
            for q in range(items_per_token):
                buf = q % 2
                if q + 1 < items_per_token:
                    gather(slot, t, q + 1, 1 - buf).start()
                else:
                    @pl.when(t + 1 < SC_TOKENS)
                    def _():
                        gather(slot, t + 1, 0, 1 - buf).start()

                    @pl.when(jnp.logical_and(t + 1 == SC_TOKENS, b + 1 < nblk))
                    def _():
                        for c in loads(b + 1, 1 - slot):
                            c.wait()
                        gather(1 - slot, 0, 0, 1 - buf).start()

                gather(slot, t, q, buf).wait()
                compute(slot, t, q, buf)

        store(b, slot).start()

        @pl.when(b + 2 < nblk)
        def _():
            for c in loads(b + 2, slot):
                c.start()

    if nblk >= 2:
        store(nblk - 2, nblk % 2).wait()
    store(nblk - 1, (nblk - 1) % 2).wait()


def _sc_mesh():
    return plsc.VectorSubcoreMesh(core_axis_name="c", subcore_axis_name="s")


def _sc_worker_id():
    return lax.axis_index("s") * V7X_SC_CORES + lax.axis_index("c")


def _sc_bf16(words):
    return plsc.bitcast(words, jnp.bfloat16)


def _sc_halves_f32(pairs):
    words = plsc.bitcast(pairs, jnp.uint32)
    return (plsc.bitcast(words << 16, jnp.float32),
            plsc.bitcast(words & jnp.uint32(HI_HALF), jnp.float32))


def _pick_scores(flat, slots):
    T, K = slots.shape
    nb = SC_PICK_TOKENS
    tok_per_w = T // V7X_SC_WORKERS
    nblk = tok_per_w // nb
    assert T % (V7X_SC_WORKERS * nb) == 0

    @functools.partial(
        pl.kernel, mesh=_sc_mesh(),
        out_type=jax.ShapeDtypeStruct((T, K), jnp.float32),
        compiler_params=pltpu.CompilerParams(needs_layout_passes=False),
        scratch_types=[
            pltpu.VMEM((2, nb, K), jnp.int32),
            pltpu.VMEM((2, nb, K), jnp.float32),
            pltpu.SemaphoreType.DMA((2,)),
            pltpu.SemaphoreType.DMA((2,)),
            pltpu.SemaphoreType.DMA((2,)),
        ],
        name="pick_scores")
    def k(flat_hbm, slots_hbm, out_hbm, idx_v, val_v, idx_sems, out_sems, row_sems):
        base = _sc_worker_id() * tok_per_w

        def load(b, slot):
            return pltpu.make_async_copy(slots_hbm.at[pl.ds(base + b * nb, nb)], idx_v.at[slot], idx_sems.at[slot])

        def store(b, slot):
            return pltpu.make_async_copy(val_v.at[slot], out_hbm.at[pl.ds(base + b * nb, nb)], out_sems.at[slot])

        def gathers(slot):
            return [pltpu.make_async_copy(flat_hbm.at[idx_v.at[slot, t]], val_v.at[slot, t], row_sems.at[slot])
                    for t in range(nb)]

        load(0, 0).start()

        @pl.loop(0, nblk)
        def _(b):
            slot = b % 2
            load(b, slot).wait()

            @pl.when(b + 1 < nblk)
            def _():
                load(b + 1, 1 - slot).start()

            @pl.when(b >= 2)
            def _():
                store(b - 2, slot).wait()

            for c in gathers(slot):
                c.start()
            for c in gathers(slot):
                c.wait()
            store(b, slot).start()

        if nblk >= 2:
            store(nblk - 2, nblk % 2).wait()
        store(nblk - 1, (nblk - 1) % 2).wait()

    return k(flat, slots)


def _expert_scores(h2, idx, table):
    T, DW = h2.shape
    K = idx.shape[1]
    L, G = V7X_SC_LANES, SC_GATHER
    nj = DW // L
    tok_per_w = T // V7X_SC_WORKERS
    assert T % (V7X_SC_WORKERS * SC_TOKENS) == 0 and K % (2 * G) == 0 and nj % 4 == 0

    @functools.partial(
        pl.kernel, mesh=_sc_mesh(),
        out_type=jax.ShapeDtypeStruct((T, K), jnp.float32),
        compiler_params=pltpu.CompilerParams(needs_layout_passes=False),
        scratch_types=[
            pltpu.VMEM((2, SC_TOKENS, K), jnp.int32),
            pltpu.VMEM((2, SC_TOKENS, DW), jnp.uint32),
            pltpu.VMEM((2, G, DW), jnp.uint32),
            pltpu.VMEM((G * L,), jnp.float32),
            pltpu.VMEM((2, SC_TOKENS, K), jnp.float32),
            pltpu.SemaphoreType.DMA((2,)),
            pltpu.SemaphoreType.DMA((2,)),
            pltpu.SemaphoreType.DMA((2,)),
            pltpu.SemaphoreType.DMA((2,)),
        ],
        name="expert_scores")
    def k(h2_hbm, idx_hbm, tab_hbm, pre_hbm, idx_v, x_v, rows_v, part_v, pre_v,
          idx_sems, x_sems, out_sems, row_sems):
        base = _sc_worker_id() * tok_per_w

        def loads(b, slot):
            toks = pl.ds(base + b * SC_TOKENS, SC_TOKENS)
            return [pltpu.make_async_copy(idx_hbm.at[toks], idx_v.at[slot], idx_sems.at[slot]),
                    pltpu.make_async_copy(h2_hbm.at[toks], x_v.at[slot], x_sems.at[slot])]

        def store(b, slot):
            toks = pl.ds(base + b * SC_TOKENS, SC_TOKENS)
            return pltpu.make_async_copy(pre_v.at[slot], pre_hbm.at[toks], out_sems.at[slot])

        def gather(slot, t, q, buf):
            return pltpu.make_async_copy(
                tab_hbm.at[idx_v.at[slot, t, pl.ds(q * G, G)]], rows_v.at[buf], row_sems.at[buf])

        def compute(slot, t, q, buf):
            xs = [_sc_bf16(x_v[slot, t, pl.ds(j * L, L)]) for j in range(nj)]

            @plsc.parallel_loop(0, G, unroll=SC_UNROLL)
            def _(kk):
                prods = [_sc_bf16(rows_v[buf, kk, pl.ds(j * L, L)]) * xs[j] for j in range(nj)]
                quads = [(prods[j] + prods[j + 1]) + (prods[j + 2] + prods[j + 3]) for j in range(0, nj, 4)]
                part_v[pl.ds(pl.multiple_of(kk * L, L), L)] = _tree_sum(
                    [h for p in quads for h in _sc_halves_f32(p)])

            lane = lax.iota(jnp.int32, L) * L
            for g in range(G // L):
                cols = [plsc.load_gather(part_v, [lane + (g * L * L + j)]) for j in range(L)]
                pre_v[slot, t, pl.ds(q * G + g * L, L)] = _tree_sum(cols)

        _sc_block_pipeline(tok_per_w // SC_TOKENS, K // G, loads, store, gather, compute)

    return k(h2, idx, table)


def _expert_mix(w, idx, table):
    T, K = w.shape
    DW = table.shape[1]
    D = 2 * DW
    L, G = V7X_SC_LANES, SC_GATHER
    nj = SC_MIX_CHUNK // L
    tok_per_w = T // V7X_SC_WORKERS
    assert T % (V7X_SC_WORKERS * SC_TOKENS) == 0 and K % (2 * G) == 0
    assert DW % SC_MIX_CHUNK == 0 and G % SC_MIX_GROUP == 0

    @functools.partial(
        pl.kernel, mesh=_sc_mesh(),
        out_type=jax.ShapeDtypeStruct((T, D), jnp.float32),
        compiler_params=pltpu.CompilerParams(needs_layout_passes=False),
        scratch_types=[
            pltpu.VMEM((2, SC_TOKENS, K), jnp.int32),
            pltpu.VMEM((2, SC_TOKENS, K), jnp.int32),
            pltpu.VMEM((2, G, DW), jnp.uint32),
            pltpu.VMEM((2, SC_TOKENS, D), jnp.float32),
            pltpu.SemaphoreType.DMA((2,)),
            pltpu.SemaphoreType.DMA((2,)),
            pltpu.SemaphoreType.DMA((2,)),
            pltpu.SemaphoreType.DMA((2,)),
        ],
        name="expert_mix")
    def k(w_hbm, idx_hbm, tab_hbm, out_hbm, idx_v, w_v, rows_v, out_v,
          idx_sems, w_sems, out_sems, row_sems):
        base = _sc_worker_id() * tok_per_w

        def loads(b, slot):
            toks = pl.ds(base + b * SC_TOKENS, SC_TOKENS)
            return [pltpu.make_async_copy(idx_hbm.at[toks], idx_v.at[slot], idx_sems.at[slot]),
                    pltpu.make_async_copy(w_hbm.at[toks], w_v.at[slot], w_sems.at[slot])]

        def store(b, slot):
            toks = pl.ds(base + b * SC_TOKENS, SC_TOKENS)
            return pltpu.make_async_copy(out_v.at[slot], out_hbm.at[toks], out_sems.at[slot])

        def gather(slot, t, q, buf):
            return pltpu.make_async_copy(
                tab_hbm.at[idx_v.at[slot, t, pl.ds(q * G, G)]], rows_v.at[buf], row_sems.at[buf])

        def compute(slot, t, q, buf):
            ssplat = jnp.full((L,), slot, jnp.int32)
            tsplat = jnp.full((L,), t, jnp.int32)
            for c in range(DW // SC_MIX_CHUNK):
                def body(kg, acc):
                    kk = kg * SC_MIX_GROUP
                    wks = [_sc_bf16(plsc.load_gather(
                        w_v, [ssplat, tsplat, jnp.full((L,), q * G + i, jnp.int32) + kk]))
                        for i in range(SC_MIX_GROUP)]
                    out = []
                    for j in range(nj):
                        prods = [wks[i] * _sc_bf16(rows_v[buf, kk + i, pl.ds(c * SC_MIX_CHUNK + j * L, L)])
                                 for i in range(SC_MIX_GROUP)]
                        lo, hi = _sc_halves_f32(_tree_sum(prods))
                        out += [acc[2 * j] + lo, acc[2 * j + 1] + hi]
                    return tuple(out)

                zero = jnp.zeros((L,), jnp.float32)
                acc = plsc.parallel_loop(0, G // SC_MIX_GROUP, carry=(zero,) * (2 * nj))(body)
                for j in range(nj):
                    for half in range(2):
                        dst = out_v.at[slot, t, pl.ds(half * DW + c * SC_MIX_CHUNK + j * L, L)]
                        if q == 0:
                            dst[...] = acc[2 * j + half]
                        else:
                            plsc.addupdate(dst, acc[2 * j + half])

        _sc_block_pipeline(tok_per_w // SC_TOKENS, K // G, loads, store, gather, compute)

    return k(w, idx, table)


def _chunk_sizes(total):
    sizes, size = [], FIRST_CHUNK
    while total > 0:
        sizes.append(min(size, total))
        total -= sizes[-1]
        size = min(2 * size, MAX_CHUNK)
    return sizes


def kernel(x, norm_mix, w_in, pool_w, pool_scale, sgu_ln_g, sgu_ln_b, sgu_w, sgu_b, out_norm_pool,
           out_norm_sgu, w_out, norm_ffn, peer_wq, peer_keys, peer_u, peer_v, norm_final):
    B, S, D = x.shape
    assert norm_mix.shape[0] == 1, "single-layer block"
    T = B * S
    x1 = _mixer(x, norm_mix[0], w_in[0], pool_w[0], pool_scale[0], sgu_ln_g[0], sgu_ln_b[0],
                sgu_w[0], sgu_b[0], out_norm_pool[0], out_norm_sgu[0], w_out[0]).reshape(T, D)
    wq = peer_wq[0].astype(jnp.bfloat16)
    keys = peer_keys[0].astype(jnp.bfloat16)
    u_t = peer_u[0].astype(jnp.bfloat16).T
    v_tab = _pack_table(peer_v[0])
    out = None
    tok0 = 0
    for tc in _chunk_sizes(T):
        h2, idx, slots, gate = _router(x1, norm_ffn[0], wq, keys, tok0, tc)
        pre = _pick_scores(_dense_scores(h2, u_t), slots)
        w = _expert_weights(pre, gate)
        peer = _expert_mix(w, idx, v_tab)
        out = _final(x1, peer, norm_final, out, tok0)
        tok0 += tc
    return out.reshape(B, S, D)
```

```python
import functools
import math

import jax
import jax.numpy as jnp
from jax import lax
from jax.experimental import pallas as pl
from jax.experimental.pallas import tpu as pltpu
from jax.experimental.pallas import tpu_sc as plsc

POOL_WINDOWS = (2, 4, 8, 16)
N_POOL_GROUPS = len(POOL_WINDOWS)
SGU_HEADS = 4
SGU_CHUNK = 128
PEER_HEADS = 8
PEER_N_KEYS = 128
PEER_D_HALF = 128
PEER_TOPK = 16
NORM_EPS = 1e-6
EXPERTS_PER_TOKEN = PEER_HEADS * PEER_TOPK

V7X_LANES = 128
V7X_SUBLANES = 8
V7X_SC_CORES = 2
V7X_SC_SUBCORES = 16
V7X_SC_LANES = 16
V7X_SC_WORKERS = V7X_SC_CORES * V7X_SC_SUBCORES

HALO = max(POOL_WINDOWS)
MIX_TILE = 512
ROUTE_TILE = 512
EW_TILE = 512
SCORE_TOK_TILE = 1024
SCORE_EXP_TILE = 1024
SC_PICK_TOKENS = 32
SC_GATHER = 64
SC_TOKENS = 8
SC_MIX_CHUNK = 128
SC_MIX_GROUP = 4
SC_UNROLL = 2
HI_HALF = 0xFFFF0000
TC_VMEM_LIMIT = 48 * 1024 * 1024
FIRST_CHUNK = 1024
MAX_CHUNK = 8192


def _rms(x, g):
    inv = lax.rsqrt(jnp.mean(x * x, axis=-1, keepdims=True) + NORM_EPS)
    return x * inv * g


def _pack_halves(bits):
    half = bits.shape[1] // 2
    return (bits[:, :half] >> 16) | (bits[:, half:] & jnp.uint32(HI_HALF))


def _pack_table(a):
    return _pack_halves(lax.bitcast_convert_type(a.astype(jnp.bfloat16).astype(jnp.float32), jnp.uint32))


def _gelu(x):
    return 0.5 * x * (1.0 + lax.erf(x * math.sqrt(0.5)))


def _mixer_kernel(x_ref, xh_ref, nmix_ref, win_ref, poolw_ref, pscale_ref, lng_ref, lnb_ref,
                  sguw_ref, sgub_ref, onp_ref, ons_ref, wout_ref, o_ref, pext_ref, mix_ref):
    i = pl.program_id(1)
    ts = x_ref.shape[1]
    pool_w = pscale_ref.shape[1]
    gdim = pool_w // N_POOL_GROUPS
    sgu_w = lng_ref.shape[1]
    hdim = sgu_w // SGU_HEADS

    x = x_ref[0]
    h = _rms(x, nmix_ref[...]).astype(jnp.bfloat16)
    z = jnp.dot(h, win_ref[...], preferred_element_type=jnp.float32)
    p = z[:, :pool_w]

    hh = _rms(xh_ref[0], nmix_ref[...]).astype(jnp.bfloat16)
    ph = jnp.dot(hh, win_ref[:, :pool_w], preferred_element_type=jnp.float32)
    ph = jnp.where(i > 0, ph, 0.0)
    pext_ref[0:HALO, :] = ph
    pext_ref[HALO:HALO + ts, :] = p

    pos = i * ts + lax.broadcasted_iota(jnp.int32, (ts, 1), 0)
    ssq = jnp.zeros((ts, 1), jnp.float32)
    a_parts = []
    for g, win in enumerate(POOL_WINDOWS):
        cols = slice(g * gdim, (g + 1) * gdim)
        s = pext_ref[HALO:HALO + ts, cols]
        for j in range(1, win):
            s = s + pext_ref[HALO - j:HALO - j + ts, cols]
        cnt = jnp.minimum(pos + 1, win).astype(jnp.float32)
        d = (s / cnt - p[:, cols]).astype(jnp.bfloat16)
        a = jnp.dot(d, poolw_ref[g], preferred_element_type=jnp.float32) * pscale_ref[:, cols]
        ssq = ssq + jnp.sum(a * a, axis=-1, keepdims=True)
        a_parts.append(a)
    inv_a = lax.rsqrt(ssq / pool_w + NORM_EPS)
    for g in range(N_POOL_GROUPS):
        cols = slice(g * gdim, (g + 1) * gdim)
        mix_ref[:, cols] = (a_parts[g] * inv_a * onp_ref[:, cols]).astype(jnp.bfloat16)

    gz = _gelu(z[:, pool_w:])
    tril = (lax.broadcasted_iota(jnp.int32, (SGU_CHUNK, SGU_CHUNK), 0)
            >= lax.broadcasted_iota(jnp.int32, (SGU_CHUNK, SGU_CHUNK), 1))
    ssq = jnp.zeros((ts, 1), jnp.float32)
    b_parts = []
    for hd in range(SGU_HEADS):
        cols = slice(hd * hdim, (hd + 1) * hdim)
        u = gz[:, hd * hdim:(hd + 1) * hdim]
        v = gz[:, sgu_w + hd * hdim:sgu_w + (hd + 1) * hdim]
        mu = jnp.mean(v, axis=-1, keepdims=True)
        vc = v - mu
        var = jnp.mean(vc * vc, axis=-1, keepdims=True)
        vn = (vc * lax.rsqrt(var + NORM_EPS) * lng_ref[:, cols] + lnb_ref[:, cols]).astype(jnp.bfloat16)
        w = jnp.where(tril, sguw_ref[hd], jnp.zeros((), sguw_ref.dtype))
        mixed = [jnp.dot(w, vn[n * SGU_CHUNK:(n + 1) * SGU_CHUNK], preferred_element_type=jnp.float32)
                 + sgub_ref[hd] for n in range(ts // SGU_CHUNK)]
        b = u * jnp.concatenate(mixed, axis=0)
        ssq = ssq + jnp.sum(b * b, axis=-1, keepdims=True)
        b_parts.append(b)
    inv_b = lax.rsqrt(ssq / sgu_w + NORM_EPS)
    for hd in range(SGU_HEADS):
        cols = slice(hd * hdim, (hd + 1) * hdim)
        mix_ref[:, pool_w + hd * hdim:pool_w + (hd + 1) * hdim] = (
            b_parts[hd] * inv_b * ons_ref[:, cols]).astype(jnp.bfloat16)

    o_ref[0] = x + jnp.dot(mix_ref[...], wout_ref[...], preferred_element_type=jnp.float32)


def _mixer(x, norm_mix, w_in, pool_w, pool_scale, ln_g, ln_b, sgu_w, sgu_b, on_pool, on_sgu, w_out):
    B, S, D = x.shape
    ts = min(MIX_TILE, S)
    pool_width = pool_scale.size
    sgu_width = ln_g.size
    in_width = w_in.shape[1]
    gdim = pool_width // N_POOL_GROUPS
    halo_blocks = ts // HALO
    full = lambda shape: pl.BlockSpec(shape, lambda b, i: (0,) * len(shape))
    return pl.pallas_call(
        _mixer_kernel,
        grid=(B, S // ts),
        in_specs=[
            pl.BlockSpec((1, ts, D), lambda b, i: (b, i, 0)),
            pl.BlockSpec((1, HALO, D), lambda b, i: (b, jnp.maximum(i * halo_blocks - 1, 0), 0)),
            full((1, D)),
            full((D, in_width)),
            full((N_POOL_GROUPS, gdim, gdim)),
            full((1, pool_width)),
            full((1, sgu_width)),
            full((1, sgu_width)),
            full((SGU_HEADS, SGU_CHUNK, SGU_CHUNK)),
            full((SGU_HEADS, SGU_CHUNK, SGU_CHUNK)),
            full((1, pool_width)),
            full((1, sgu_width)),
            full((pool_width + sgu_width, D)),
        ],
        out_specs=pl.BlockSpec((1, ts, D), lambda b, i: (b, i, 0)),
        out_shape=jax.ShapeDtypeStruct((B, S, D), jnp.float32),
        scratch_shapes=[
            pltpu.VMEM((HALO + ts, pool_width), jnp.float32),
            pltpu.VMEM((ts, pool_width + sgu_width), jnp.bfloat16),
        ],
        compiler_params=pltpu.CompilerParams(
            dimension_semantics=("parallel", "arbitrary"), vmem_limit_bytes=TC_VMEM_LIMIT),
        name="mixer",
    )(x, x, norm_mix.reshape(1, D), w_in.astype(jnp.bfloat16), pool_w.astype(jnp.bfloat16),
      pool_scale.reshape(1, pool_width), ln_g.reshape(1, sgu_width), ln_b.reshape(1, sgu_width),
      sgu_w.astype(jnp.bfloat16),
      jnp.broadcast_to(sgu_b[:, :, None], (SGU_HEADS, SGU_CHUNK, SGU_CHUNK)),
      on_pool.reshape(1, pool_width), on_sgu.reshape(1, sgu_width), w_out.astype(jnp.bfloat16))


def _topk_rows(s, k):
    n = s.shape[0]
    iota = lax.broadcasted_iota(jnp.int32, s.shape, 0)
    vals, idxs = [], []
    for _ in range(k):
        m = jnp.max(s, axis=0, keepdims=True)
        ix = jnp.min(jnp.where(s == m, iota, n), axis=0, keepdims=True)
        vals.append(m)
        idxs.append(ix)
        s = jnp.where(iota == ix, -jnp.inf, s)
    return vals, idxs


def _pair_candidates(v1, i1, v2, i2):
    k = PEER_TOPK
    v2c, i2c = jnp.concatenate(v2, axis=0), jnp.concatenate(i2, axis=0)
    m = v2c.shape[1]
    vals, experts, flats = [], [], []
    a = 0
    while k // (a + 1) > 1:
        rows = -(-(k // (a + 1)) // V7X_SUBLANES) * V7X_SUBLANES
        vals.append(v1[a] + v2c[:rows])
        experts.append(i1[a] * PEER_N_KEYS + i2c[:rows])
        flats.append(a * k + lax.broadcasted_iota(jnp.int32, (rows, m), 0))
        a += 1
    vals.append(jnp.concatenate(v1[a:], axis=0) + v2[0])
    experts.append(jnp.concatenate(i1[a:], axis=0) * PEER_N_KEYS + i2[0])
    flats.append((a + lax.broadcasted_iota(jnp.int32, (k - a, m), 0)) * k)
    return jnp.concatenate(vals, axis=0), jnp.concatenate(experts, axis=0), jnp.concatenate(flats, axis=0)


def _router_kernel(x_ref, nffn_ref, wq_ref, keys_ref, after_ref, h2_ref, idx_ref, slot_ref, gate_ref,
                   q_ref, idxt_ref, gatet_ref):
    del after_ref
    h2 = _rms(x_ref[...], nffn_ref[...]).astype(jnp.bfloat16)
    h2_ref[...] = h2
    q_ref[...] = jnp.dot(h2, wq_ref[...], preferred_element_type=jnp.float32).astype(jnp.bfloat16)
    dq = 2 * PEER_D_HALF
    nt = (((1,), (1,)), ((), ()))

    def head(hd, carry):
        off = pl.multiple_of(hd * dq, dq)
        s1 = lax.dot_general(keys_ref[0], q_ref[:, pl.ds(off, PEER_D_HALF)], nt,
                             preferred_element_type=jnp.float32)
        s2 = lax.dot_general(keys_ref[1], q_ref[:, pl.ds(off + PEER_D_HALF, PEER_D_HALF)], nt,
                             preferred_element_type=jnp.float32)
        v1, i1 = _topk_rows(s1, PEER_TOPK)
        v2, i2 = _topk_rows(s2, PEER_TOPK)
        cand, expert, flat = _pair_candidates(v1, i1, v2, i2)
        cv, ce = [], []
        for _ in range(PEER_TOPK):
            m = jnp.max(cand, axis=0, keepdims=True)
            ix = jnp.min(jnp.where(cand == m, flat, PEER_TOPK * PEER_TOPK), axis=0, keepdims=True)
            hit = flat == ix
            cv.append(m)
            ce.append(jnp.max(jnp.where(hit, expert, -1), axis=0, keepdims=True))
            cand = jnp.where(hit, -jnp.inf, cand)
        cvc = jnp.concatenate(cv, axis=0)
        e = jnp.exp(cvc - cv[0])
        gate = e / jnp.sum(e, axis=0, keepdims=True)
        row = pl.multiple_of(hd * PEER_TOPK, PEER_TOPK)
        idxt_ref[pl.ds(row, PEER_TOPK), :] = jnp.concatenate(ce, axis=0)
        gatet_ref[pl.ds(row, PEER_TOPK), :] = gate
        return carry

    lax.fori_loop(0, PEER_HEADS, head, 0)
    idx = idxt_ref[...].T
    idx_ref[...] = idx
    tok = pl.program_id(0) * idx.shape[0] + lax.broadcasted_iota(jnp.int32, idx.shape, 0)
    slot_ref[...] = _tile_linear_index(tok, idx, PEER_N_KEYS * PEER_N_KEYS)
    gate_ref[...] = gatet_ref[...].T


def _router(x1, norm_ffn, wq, keys, tok0, T, after):
    D = x1.shape[1]
    tr = min(ROUTE_TILE, T)
    qw = wq.shape[1]
    assert tok0 % tr == 0 and T % tr == 0
    first = tok0 // tr
    full = lambda shape: pl.BlockSpec(shape, lambda i: (0,) * len(shape))
    return pl.pallas_call(
        _router_kernel,
        grid=(T // tr,),
        in_specs=[
            pl.BlockSpec((tr, D), lambda i: (first + i, 0)),
            full((1, D)),
            full((D, qw)),
            full((2, PEER_N_KEYS, PEER_D_HALF)),
            pl.BlockSpec(memory_space=pl.ANY),
        ],
        out_specs=[
            pl.BlockSpec((tr, D), lambda i: (i, 0)),
            pl.BlockSpec((tr, EXPERTS_PER_TOKEN), lambda i: (i, 0)),
            pl.BlockSpec((tr, EXPERTS_PER_TOKEN), lambda i: (i, 0)),
            pl.BlockSpec((tr, EXPERTS_PER_TOKEN), lambda i: (i, 0)),
        ],
        out_shape=[
            jax.ShapeDtypeStruct((T, D), jnp.bfloat16),
            jax.ShapeDtypeStruct((T, EXPERTS_PER_TOKEN), jnp.int32),
            jax.ShapeDtypeStruct((T, EXPERTS_PER_TOKEN), jnp.int32),
            jax.ShapeDtypeStruct((T, EXPERTS_PER_TOKEN), jnp.float32),
        ],
        scratch_shapes=[
            pltpu.VMEM((tr, qw), jnp.bfloat16),
            pltpu.VMEM((EXPERTS_PER_TOKEN, tr), jnp.int32),
            pltpu.VMEM((EXPERTS_PER_TOKEN, tr), jnp.float32),
        ],
        compiler_params=pltpu.CompilerParams(
            dimension_semantics=("parallel",), vmem_limit_bytes=TC_VMEM_LIMIT),
        name="router",
    )(x1, norm_ffn.reshape(1, D), wq, keys, after)


def _tile_linear_index(row, col, ncols):
    return ((row >> 3) * (ncols * V7X_SUBLANES) + (col >> 7) * (V7X_SUBLANES * V7X_LANES)
            + (row & (V7X_SUBLANES - 1)) * V7X_LANES + (col & (V7X_LANES - 1)))


def _scores_kernel(h_ref, ut_ref, o_ref):
    acc = jnp.dot(h_ref[...], ut_ref[...], preferred_element_type=jnp.float32)
    tm, tn = acc.shape
    for n in range(tn // V7X_LANES):
        o_ref[:, n * V7X_SUBLANES:(n + 1) * V7X_SUBLANES, :] = acc[:, n * V7X_LANES:(n + 1) * V7X_LANES].reshape(
            tm // V7X_SUBLANES, V7X_SUBLANES, V7X_LANES)


def _dense_scores(h2, u_t):
    T, D = h2.shape
    E = u_t.shape[1]
    tm, tn = min(SCORE_TOK_TILE, T), min(SCORE_EXP_TILE, E)
    assert T % tm == 0 and E % tn == 0 and tm % V7X_SUBLANES == 0 and tn % V7X_LANES == 0
    out = pl.pallas_call(
        _scores_kernel,
        grid=(T // tm, E // tn),
        in_specs=[pl.BlockSpec((tm, D), lambda i, j: (i, 0)), pl.BlockSpec((D, tn), lambda i, j: (0, j))],
        out_specs=pl.BlockSpec((tm // V7X_SUBLANES, tn // V7X_LANES * V7X_SUBLANES, V7X_LANES),
                               lambda i, j: (i, j, 0)),
        out_shape=jax.ShapeDtypeStruct((T // V7X_SUBLANES, E // V7X_LANES * V7X_SUBLANES, V7X_LANES),
                                       jnp.float32),
        compiler_params=pltpu.CompilerParams(
            dimension_semantics=("parallel", "arbitrary"), vmem_limit_bytes=TC_VMEM_LIMIT),
        name="dense_scores",
    )(h2, u_t)
    return out.reshape(-1)


def _act_kernel(pre_ref, gate_ref, w_ref):
    w = (gate_ref[...] * _gelu(pre_ref[...])).astype(jnp.bfloat16).astype(jnp.float32)
    hi = pltpu.bitcast(w, jnp.uint32) & jnp.uint32(HI_HALF)
    w_ref[...] = pltpu.bitcast(hi | (hi >> 16), jnp.int32)


def _expert_weights(pre, gate):
    T, K = pre.shape
    te = min(EW_TILE, T)
    spec = pl.BlockSpec((te, K), lambda i: (i, 0))
    return pl.pallas_call(
        _act_kernel, grid=(T // te,), in_specs=[spec, spec], out_specs=spec,
        out_shape=jax.ShapeDtypeStruct((T, K), jnp.int32),
        compiler_params=pltpu.CompilerParams(dimension_semantics=("parallel",)),
        name="expert_weights",
    )(pre, gate)


def _final_kernel(x_ref, y_ref, g_ref, *rest):
    o_ref = rest[-1]
    o_ref[...] = _rms(x_ref[...] + y_ref[...], g_ref[...])


def _final(x1, peer, norm_final, out_prev, tok0):
    Tc, D = peer.shape
    te = min(EW_TILE, Tc)
    assert tok0 % te == 0 and Tc % te == 0
    nblk = Tc // te
    spec = pl.BlockSpec((te, D), lambda i: (tok0 // te + i, 0))
    in_specs = [spec, pl.BlockSpec((te, D), lambda i: (i, 0)), pl.BlockSpec((1, D), lambda i: (0, 0))]
    args = [x1, peer, norm_final.reshape(1, D)]
    aliases = {}
    if out_prev is not None:
        in_specs.append(pl.BlockSpec(memory_space=pl.ANY))
        args.append(out_prev)
        aliases = {3: 0}
    return pl.pallas_call(
        _final_kernel, grid=(nblk,), in_specs=in_specs, out_specs=spec,
        out_shape=jax.ShapeDtypeStruct(x1.shape, jnp.float32),
        input_output_aliases=aliases,
        compiler_params=pltpu.CompilerParams(dimension_semantics=("parallel",)),
        name="final_norm",
    )(*args)


def _tree_sum(vals):
    while len(vals) > 1:
        nxt = [vals[i] + vals[i + 1] for i in range(0, len(vals) - 1, 2)]
        if len(vals) % 2:
            nxt.append(vals[-1])
        vals = nxt
    return vals[0]


def _sc_block_pipeline(nblk, items_per_token, loads, store, gather, compute):
    assert items_per_token % 2 == 0 and nblk >= 1

    for c in loads(0, 0):
        c.start()
    for c in loads(0, 0):
        c.wait()
    if nblk > 1:
        for c in loads(1, 1):
            c.start()
    gather(0, 0, 0, 0).start()

    @pl.loop(0, nblk)
    def _(b):
        slot = b % 2

        @pl.when(b >= 2)
        def _():
            store(b - 2, slot).wait()

        @pl.loop(0, SC_TOKENS)
        def _(t):
            for q in range(items_per_token):
                buf = q % 2
                if q + 1 < items_per_token:
                    gather(slot, t, q + 1, 1 - buf).start()
                else:
                    @pl.when(t + 1 < SC_TOKENS)
                    def _():
                        gather(slot, t + 1, 0, 1 - buf).start()

                    @pl.when(jnp.logical_and(t + 1 == SC_TOKENS, b + 1 < nblk))
                    def _():
                        for c in loads(b + 1, 1 - slot):
                            c.wait()
                        gather(1 - slot, 0, 0, 1 - buf).start()

                gather(slot, t, q, buf).wait()
                compute(slot, t, q, buf)

        store(b, slot).start()

        @pl.when(b + 2 < nblk)
        def _():
            for c in loads(b + 2, slot):
                c.start()

    if nblk >= 2:
        store(nblk - 2, nblk % 2).wait()
    store(nblk - 1, (nblk - 1) % 2).wait()


def _sc_mesh():
    return plsc.VectorSubcoreMesh(core_axis_name="c", subcore_axis_name="s")


def _sc_worker_id():
    return lax.axis_index("s") * V7X_SC_CORES + lax.axis_index("c")


def _sc_bf16(words):
    return plsc.bitcast(words, jnp.bfloat16)


def _sc_halves_f32(pairs):
    words = plsc.bitcast(pairs, jnp.uint32)
    return (plsc.bitcast(words << 16, jnp.float32),
            plsc.bitcast(words & jnp.uint32(HI_HALF), jnp.float32))


def _pick_scores(flat, slots):
    T, K = slots.shape
    nb = SC_PICK_TOKENS
    tok_per_w = T // V7X_SC_WORKERS
    nblk = tok_per_w // nb
    assert T % (V7X_SC_WORKERS * nb) == 0

    @functools.partial(
        pl.kernel, mesh=_sc_mesh(),
        out_type=jax.ShapeDtypeStruct((T, K), jnp.float32),
        compiler_params=pltpu.CompilerParams(needs_layout_passes=False),
        scratch_types=[
            pltpu.VMEM((2, nb, K), jnp.int32),
            pltpu.VMEM((2, nb, K), jnp.float32),
            pltpu.SemaphoreType.DMA((2,)),
            pltpu.SemaphoreType.DMA((2,)),
            pltpu.SemaphoreType.DMA((2,)),
        ],
        name="pick_scores")
    def k(flat_hbm, slots_hbm, out_hbm, idx_v, val_v, idx_sems, out_sems, row_sems):
        base = _sc_worker_id() * tok_per_w

        def load(b, slot):
            return pltpu.make_async_copy(slots_hbm.at[pl.ds(base + b * nb, nb)], idx_v.at[slot], idx_sems.at[slot])

        def store(b, slot):
            return pltpu.make_async_copy(val_v.at[slot], out_hbm.at[pl.ds(base + b * nb, nb)], out_sems.at[slot])

        def gathers(slot):
            return [pltpu.make_async_copy(flat_hbm.at[idx_v.at[slot, t]], val_v.at[slot, t], row_sems.at[slot])
                    for t in range(nb)]

        load(0, 0).start()

        @pl.loop(0, nblk)
        def _(b):
            slot = b % 2
            load(b, slot).wait()

            @pl.when(b + 1 < nblk)
            def _():
                load(b + 1, 1 - slot).start()

            @pl.when(b >= 2)
            def _():
                store(b - 2, slot).wait()

            for c in gathers(slot):
                c.start()
            for c in gathers(slot):
                c.wait()
            store(b, slot).start()

        if nblk >= 2:
            store(nblk - 2, nblk % 2).wait()
        store(nblk - 1, (nblk - 1) % 2).wait()

    return k(flat, slots)


def _expert_mix(w, idx, table):
    T, K = w.shape
    DW = table.shape[1]
    D = 2 * DW
    L, G = V7X_SC_LANES, SC_GATHER
    nj = SC_MIX_CHUNK // L
    tok_per_w = T // V7X_SC_WORKERS
    assert T % (V7X_SC_WORKERS * SC_TOKENS) == 0 and K % (2 * G) == 0
    assert DW % SC_MIX_CHUNK == 0 and G % SC_MIX_GROUP == 0

    @functools.partial(
        pl.kernel, mesh=_sc_mesh(),
        out_type=jax.ShapeDtypeStruct((T, D), jnp.float32),
        compiler_params=pltpu.CompilerParams(needs_layout_passes=False),
        scratch_types=[
            pltpu.VMEM((2, SC_TOKENS, K), jnp.int32),
            pltpu.VMEM((2, SC_TOKENS, K), jnp.int32),
            pltpu.VMEM((2, G, DW), jnp.uint32),
            pltpu.VMEM((2, SC_TOKENS, D), jnp.float32),
            pltpu.SemaphoreType.DMA((2,)),
            pltpu.SemaphoreType.DMA((2,)),
            pltpu.SemaphoreType.DMA((2,)),
            pltpu.SemaphoreType.DMA((2,)),
        ],
        name="expert_mix")
    def k(w_hbm, idx_hbm, tab_hbm, out_hbm, idx_v, w_v, rows_v, out_v,
          idx_sems, w_sems, out_sems, row_sems):
        base = _sc_worker_id() * tok_per_w

        def loads(b, slot):
            toks = pl.ds(base + b * SC_TOKENS, SC_TOKENS)
            return [pltpu.make_async_copy(idx_hbm.at[toks], idx_v.at[slot], idx_sems.at[slot]),
                    pltpu.make_async_copy(w_hbm.at[toks], w_v.at[slot], w_sems.at[slot])]

        def store(b, slot):
            toks = pl.ds(base + b * SC_TOKENS, SC_TOKENS)
            return pltpu.make_async_copy(out_v.at[slot], out_hbm.at[toks], out_sems.at[slot])

        def gather(slot, t, q, buf):
            return pltpu.make_async_copy(
                tab_hbm.at[idx_v.at[slot, t, pl.ds(q * G, G)]], rows_v.at[buf], row_sems.at[buf])

        def compute(slot, t, q, buf):
            ssplat = jnp.full((L,), slot, jnp.int32)
            tsplat = jnp.full((L,), t, jnp.int32)
            for c in range(DW // SC_MIX_CHUNK):
                def body(kg, acc):
                    kk = kg * SC_MIX_GROUP
                    wks = [_sc_bf16(plsc.load_gather(
                        w_v, [ssplat, tsplat, jnp.full((L,), q * G + i, jnp.int32) + kk]))
                        for i in range(SC_MIX_GROUP)]
                    out = []
                    for j in range(nj):
                        prods = [wks[i] * _sc_bf16(rows_v[buf, kk + i, pl.ds(c * SC_MIX_CHUNK + j * L, L)])
                                 for i in range(SC_MIX_GROUP)]
                        lo, hi = _sc_halves_f32(_tree_sum(prods))
                        out += [acc[2 * j] + lo, acc[2 * j + 1] + hi]
                    return tuple(out)

                zero = jnp.zeros((L,), jnp.float32)
                acc = plsc.parallel_loop(0, G // SC_MIX_GROUP, carry=(zero,) * (2 * nj))(body)
                for j in range(nj):
                    for half in range(2):
                        dst = out_v.at[slot, t, pl.ds(half * DW + c * SC_MIX_CHUNK + j * L, L)]
                        if q == 0:
                            dst[...] = acc[2 * j + half]
                        else:
                            plsc.addupdate(dst, acc[2 * j + half])

        _sc_block_pipeline(tok_per_w // SC_TOKENS, K // G, loads, store, gather, compute)

    return k(w, idx, table)


def _chunk_sizes(total):
    sizes, size = [], FIRST_CHUNK
    while total > 0:
        sizes.append(min(size, total))
        total -= sizes[-1]
        size = min(2 * size, MAX_CHUNK)
    return sizes


def kernel(x, norm_mix, w_in, pool_w, pool_scale, sgu_ln_g, sgu_ln_b, sgu_w, sgu_b, out_norm_pool,
           out_norm_sgu, w_out, norm_ffn, peer_wq, peer_keys, peer_u, peer_v, norm_final):
    B, S, D = x.shape
    assert norm_mix.shape[0] == 1, "single-layer block"
    T = B * S
    x1 = _mixer(x, norm_mix[0], w_in[0], pool_w[0], pool_scale[0], sgu_ln_g[0], sgu_ln_b[0],
                sgu_w[0], sgu_b[0], out_norm_pool[0], out_norm_sgu[0], w_out[0]).reshape(T, D)
    wq = peer_wq[0].astype(jnp.bfloat16)
    keys = peer_keys[0].astype(jnp.bfloat16)
    u_t = peer_u[0].astype(jnp.bfloat16).T
    v_tab = _pack_table(peer_v[0])
    out = None
    tok0 = 0
    ws = [norm_ffn, norm_ffn]
    for tc in _chunk_sizes(T):
        h2, idx, slots, gate = _router(x1, norm_ffn[0], wq, keys, tok0, tc, ws[-2])
        pre = _pick_scores(_dense_scores(h2, u_t), slots)
        w = _expert_weights(pre, gate)
        ws.append(w)
        peer = _expert_mix(w, idx, v_tab)
        out = _final(x1, peer, norm_final, out, tok0)
        tok0 += tc
    return out.reshape(B, S, D)
```

```python
import functools
import math

import jax
import jax.numpy as jnp
from jax import lax
from jax.experimental import pallas as pl
from jax.experimental.pallas import tpu as pltpu
from jax.experimental.pallas import tpu_sc as plsc

POOL_WINDOWS = (2, 4, 8, 16)
N_POOL_GROUPS = len(POOL_WINDOWS)
SGU_HEADS = 4
SGU_CHUNK = 128
PEER_HEADS = 8
PEER_N_KEYS = 128
PEER_D_HALF = 128
PEER_TOPK = 16
NORM_EPS = 1e-6
EXPERTS_PER_TOKEN = PEER_HEADS * PEER_TOPK

V7X_LANES = 128
V7X_SUBLANES = 8
V7X_SC_CORES = 2
V7X_SC_SUBCORES = 16
V7X_SC_LANES = 16
V7X_SC_WORKERS = V7X_SC_CORES * V7X_SC_SUBCORES

HALO = max(POOL_WINDOWS)
MIX_TILE = 512
ROUTE_TILE = 512
EW_TILE = 512
SCORE_TOK_TILE = 1024
SCORE_EXP_TILE = 1024
SC_PICK_TOKENS = 32
SC_GATHER = 64
SC_TOKENS = 8
SC_MIX_CHUNK = 128
SC_MIX_GROUP = 4
SC_UNROLL = 2
HI_HALF = 0xFFFF0000
TC_VMEM_LIMIT = 48 * 1024 * 1024
FIRST_CHUNK = 1024
MAX_CHUNK = 8192


def _rms(x, g):
    inv = lax.rsqrt(jnp.mean(x * x, axis=-1, keepdims=True) + NORM_EPS)
    return x * inv * g


def _pack_halves(bits):
    half = bits.shape[1] // 2
    return (bits[:, :half] >> 16) | (bits[:, half:] & jnp.uint32(HI_HALF))


def _pack_table(a):
    return _pack_halves(lax.bitcast_convert_type(a.astype(jnp.bfloat16).astype(jnp.float32), jnp.uint32))


def _gelu(x):
    return 0.5 * x * (1.0 + lax.erf(x * math.sqrt(0.5)))


def _mixer_kernel(x_ref, xh_ref, nmix_ref, win_ref, poolw_ref, pscale_ref, lng_ref, lnb_ref,
                  sguw_ref, sgub_ref, onp_ref, ons_ref, wout_ref, o_ref, pext_ref, mix_ref):
    i = pl.program_id(1)
    ts = x_ref.shape[1]
    pool_w = pscale_ref.shape[1]
    gdim = pool_w // N_POOL_GROUPS
    sgu_w = lng_ref.shape[1]
    hdim = sgu_w // SGU_HEADS

    x = x_ref[0]
    h = _rms(x, nmix_ref[...]).astype(jnp.bfloat16)
    z = jnp.dot(h, win_ref[...], preferred_element_type=jnp.float32)
    p = z[:, :pool_w]

    hh = _rms(xh_ref[0], nmix_ref[...]).astype(jnp.bfloat16)
    ph = jnp.dot(hh, win_ref[:, :pool_w], preferred_element_type=jnp.float32)
    ph = jnp.where(i > 0, ph, 0.0)
    pext_ref[0:HALO, :] = ph
    pext_ref[HALO:HALO + ts, :] = p

    pos = i * ts + lax.broadcasted_iota(jnp.int32, (ts, 1), 0)
    ssq = jnp.zeros((ts, 1), jnp.float32)
    a_parts = []
    for g, win in enumerate(POOL_WINDOWS):
        cols = slice(g * gdim, (g + 1) * gdim)
        s = pext_ref[HALO:HALO + ts, cols]
        for j in range(1, win):
            s = s + pext_ref[HALO - j:HALO - j + ts, cols]
        cnt = jnp.minimum(pos + 1, win).astype(jnp.float32)
        d = (s / cnt - p[:, cols]).astype(jnp.bfloat16)
        a = jnp.dot(d, poolw_ref[g], preferred_element_type=jnp.float32) * pscale_ref[:, cols]
        ssq = ssq + jnp.sum(a * a, axis=-1, keepdims=True)
        a_parts.append(a)
    inv_a = lax.rsqrt(ssq / pool_w + NORM_EPS)
    for g in range(N_POOL_GROUPS):
        cols = slice(g * gdim, (g + 1) * gdim)
        mix_ref[:, cols] = (a_parts[g] * inv_a * onp_ref[:, cols]).astype(jnp.bfloat16)

    gz = _gelu(z[:, pool_w:])
    tril = (lax.broadcasted_iota(jnp.int32, (SGU_CHUNK, SGU_CHUNK), 0)
            >= lax.broadcasted_iota(jnp.int32, (SGU_CHUNK, SGU_CHUNK), 1))
    ssq = jnp.zeros((ts, 1), jnp.float32)
    b_parts = []
    for hd in range(SGU_HEADS):
        cols = slice(hd * hdim, (hd + 1) * hdim)
        u = gz[:, hd * hdim:(hd + 1) * hdim]
        v = gz[:, sgu_w + hd * hdim:sgu_w + (hd + 1) * hdim]
        mu = jnp.mean(v, axis=-1, keepdims=True)
        vc = v - mu
        var = jnp.mean(vc * vc, axis=-1, keepdims=True)
        vn = (vc * lax.rsqrt(var + NORM_EPS) * lng_ref[:, cols] + lnb_ref[:, cols]).astype(jnp.bfloat16)
        w = jnp.where(tril, sguw_ref[hd], jnp.zeros((), sguw_ref.dtype))
        mixed = [jnp.dot(w, vn[n * SGU_CHUNK:(n + 1) * SGU_CHUNK], preferred_element_type=jnp.float32)
                 + sgub_ref[hd] for n in range(ts // SGU_CHUNK)]
        b = u * jnp.concatenate(mixed, axis=0)
        ssq = ssq + jnp.sum(b * b, axis=-1, keepdims=True)
        b_parts.append(b)
    inv_b = lax.rsqrt(ssq / sgu_w + NORM_EPS)
    for hd in range(SGU_HEADS):
        cols = slice(hd * hdim, (hd + 1) * hdim)
        mix_ref[:, pool_w + hd * hdim:pool_w + (hd + 1) * hdim] = (
            b_parts[hd] * inv_b * ons_ref[:, cols]).astype(jnp.bfloat16)

    o_ref[0] = x + jnp.dot(mix_ref[...], wout_ref[...], preferred_element_type=jnp.float32)


def _mixer(x, norm_mix, w_in, pool_w, pool_scale, ln_g, ln_b, sgu_w, sgu_b, on_pool, on_sgu, w_out):
    B, S, D = x.shape
    ts = min(MIX_TILE, S)
    pool_width = pool_scale.size
    sgu_width = ln_g.size
    in_width = w_in.shape[1]
    gdim = pool_width // N_POOL_GROUPS
    halo_blocks = ts // HALO
    full = lambda shape: pl.BlockSpec(shape, lambda b, i: (0,) * len(shape))
    return pl.pallas_call(
        _mixer_kernel,
        grid=(B, S // ts),
        in_specs=[
            pl.BlockSpec((1, ts, D), lambda b, i: (b, i, 0)),
            pl.BlockSpec((1, HALO, D), lambda b, i: (b, jnp.maximum(i * halo_blocks - 1, 0), 0)),
            full((1, D)),
            full((D, in_width)),
            full((N_POOL_GROUPS, gdim, gdim)),
            full((1, pool_width)),
            full((1, sgu_width)),
            full((1, sgu_width)),
            full((SGU_HEADS, SGU_CHUNK, SGU_CHUNK)),
            full((SGU_HEADS, SGU_CHUNK, SGU_CHUNK)),
            full((1, pool_width)),
            full((1, sgu_width)),
            full((pool_width + sgu_width, D)),
        ],
        out_specs=pl.BlockSpec((1, ts, D), lambda b, i: (b, i, 0)),
        out_shape=jax.ShapeDtypeStruct((B, S, D), jnp.float32),
        scratch_shapes=[
            pltpu.VMEM((HALO + ts, pool_width), jnp.float32),
            pltpu.VMEM((ts, pool_width + sgu_width), jnp.bfloat16),
        ],
        compiler_params=pltpu.CompilerParams(
            dimension_semantics=("parallel", "arbitrary"), vmem_limit_bytes=TC_VMEM_LIMIT),
        name="mixer",
    )(x, x, norm_mix.reshape(1, D), w_in.astype(jnp.bfloat16), pool_w.astype(jnp.bfloat16),
      pool_scale.reshape(1, pool_width), ln_g.reshape(1, sgu_width), ln_b.reshape(1, sgu_width),
      sgu_w.astype(jnp.bfloat16),
      jnp.broadcast_to(sgu_b[:, :, None], (SGU_HEADS, SGU_CHUNK, SGU_CHUNK)),
      on_pool.reshape(1, pool_width), on_sgu.reshape(1, sgu_width), w_out.astype(jnp.bfloat16))


def _topk_rows(s, k):
    n = s.shape[0]
    iota = lax.broadcasted_iota(jnp.int32, s.shape, 0)
    vals, idxs = [], []
    for _ in range(k):
        m = jnp.max(s, axis=0, keepdims=True)
        ix = jnp.min(jnp.where(s == m, iota, n), axis=0, keepdims=True)
        vals.append(m)
        idxs.append(ix)
        s = jnp.where(iota == ix, -jnp.inf, s)
    return vals, idxs


def _pair_candidates(v1, i1, v2, i2):
    k = PEER_TOPK
    v2c, i2c = jnp.concatenate(v2, axis=0), jnp.concatenate(i2, axis=0)
    m = v2c.shape[1]
    vals, experts, flats = [], [], []
    a = 0
    while k // (a + 1) > 1:
        rows = -(-(k // (a + 1)) // V7X_SUBLANES) * V7X_SUBLANES
        vals.append(v1[a] + v2c[:rows])
        experts.append(i1[a] * PEER_N_KEYS + i2c[:rows])
        flats.append(a * k + lax.broadcasted_iota(jnp.int32, (rows, m), 0))
        a += 1
    vals.append(jnp.concatenate(v1[a:], axis=0) + v2[0])
    experts.append(jnp.concatenate(i1[a:], axis=0) * PEER_N_KEYS + i2[0])
    flats.append((a + lax.broadcasted_iota(jnp.int32, (k - a, m), 0)) * k)
    return jnp.concatenate(vals, axis=0), jnp.concatenate(experts, axis=0), jnp.concatenate(flats, axis=0)


def _router_kernel(x_ref, nffn_ref, wq_ref, keys_ref, after_ref, h2_ref, idx_ref, slot_ref, gate_ref,
                   q_ref, idxt_ref, gatet_ref):
    del after_ref
    h2 = _rms(x_ref[...], nffn_ref[...]).astype(jnp.bfloat16)
    h2_ref[...] = h2
    q_ref[...] = jnp.dot(h2, wq_ref[...], preferred_element_type=jnp.float32).astype(jnp.bfloat16)
    dq = 2 * PEER_D_HALF
    nt = (((1,), (1,)), ((), ()))

    def head(hd, carry):
        off = pl.multiple_of(hd * dq, dq)
        s1 = lax.dot_general(keys_ref[0], q_ref[:, pl.ds(off, PEER_D_HALF)], nt,
                             preferred_element_type=jnp.float32)
        s2 = lax.dot_general(keys_ref[1], q_ref[:, pl.ds(off + PEER_D_HALF, PEER_D_HALF)], nt,
                             preferred_element_type=jnp.float32)
        v1, i1 = _topk_rows(s1, PEER_TOPK)
        v2, i2 = _topk_rows(s2, PEER_TOPK)
        cand, expert, flat = _pair_candidates(v1, i1, v2, i2)
        cv, ce = [], []
        for _ in range(PEER_TOPK):
            m = jnp.max(cand, axis=0, keepdims=True)
            ix = jnp.min(jnp.where(cand == m, flat, PEER_TOPK * PEER_TOPK), axis=0, keepdims=True)
            hit = flat == ix
            cv.append(m)
            ce.append(jnp.max(jnp.where(hit, expert, -1), axis=0, keepdims=True))
            cand = jnp.where(hit, -jnp.inf, cand)
        cvc = jnp.concatenate(cv, axis=0)
        e = jnp.exp(cvc - cv[0])
        gate = e / jnp.sum(e, axis=0, keepdims=True)
        row = pl.multiple_of(hd * PEER_TOPK, PEER_TOPK)
        idxt_ref[pl.ds(row, PEER_TOPK), :] = jnp.concatenate(ce, axis=0)
        gatet_ref[pl.ds(row, PEER_TOPK), :] = gate
        return carry

    lax.fori_loop(0, PEER_HEADS, head, 0)
    idx = idxt_ref[...].T
    idx_ref[...] = idx
    tok = pl.program_id(0) * idx.shape[0] + lax.broadcasted_iota(jnp.int32, idx.shape, 0)
    slot_ref[...] = _tile_linear_index(tok, idx, PEER_N_KEYS * PEER_N_KEYS)
    gate_ref[...] = gatet_ref[...].T


def _router(x1, norm_ffn, wq, keys, tok0, T, after):
    D = x1.shape[1]
    tr = min(ROUTE_TILE, T)
    qw = wq.shape[1]
    assert tok0 % tr == 0 and T % tr == 0
    first = tok0 // tr
    full = lambda shape: pl.BlockSpec(shape, lambda i: (0,) * len(shape))
    return pl.pallas_call(
        _router_kernel,
        grid=(T // tr,),
        in_specs=[
            pl.BlockSpec((tr, D), lambda i: (first + i, 0)),
            full((1, D)),
            full((D, qw)),
            full((2, PEER_N_KEYS, PEER_D_HALF)),
            pl.BlockSpec(memory_space=pl.ANY),
        ],
        out_specs=[
            pl.BlockSpec((tr, D), lambda i: (i, 0)),
            pl.BlockSpec((tr, EXPERTS_PER_TOKEN), lambda i: (i, 0)),
            pl.BlockSpec((tr, EXPERTS_PER_TOKEN), lambda i: (i, 0)),
            pl.BlockSpec((tr, EXPERTS_PER_TOKEN), lambda i: (i, 0)),
        ],
        out_shape=[
            jax.ShapeDtypeStruct((T, D), jnp.bfloat16),
            jax.ShapeDtypeStruct((T, EXPERTS_PER_TOKEN), jnp.int32),
            jax.ShapeDtypeStruct((T, EXPERTS_PER_TOKEN), jnp.int32),
            jax.ShapeDtypeStruct((T, EXPERTS_PER_TOKEN), jnp.float32),
        ],
        scratch_shapes=[
            pltpu.VMEM((tr, qw), jnp.bfloat16),
            pltpu.VMEM((EXPERTS_PER_TOKEN, tr), jnp.int32),
            pltpu.VMEM((EXPERTS_PER_TOKEN, tr), jnp.float32),
        ],
        compiler_params=pltpu.CompilerParams(
            dimension_semantics=("parallel",), vmem_limit_bytes=TC_VMEM_LIMIT),
        name="router",
    )(x1, norm_ffn.reshape(1, D), wq, keys, after)


def _tile_linear_index(row, col, ncols):
    return ((row >> 3) * (ncols * V7X_SUBLANES) + (col >> 7) * (V7X_SUBLANES * V7X_LANES)
            + (row & (V7X_SUBLANES - 1)) * V7X_LANES + (col & (V7X_LANES - 1)))


def _scores_kernel(h_ref, ut_ref, o_ref):
    acc = jnp.dot(h_ref[...], ut_ref[...], preferred_element_type=jnp.float32)
    tm, tn = acc.shape
    for n in range(tn // V7X_LANES):
        o_ref[:, n * V7X_SUBLANES:(n + 1) * V7X_SUBLANES, :] = acc[:, n * V7X_LANES:(n + 1) * V7X_LANES].reshape(
            tm // V7X_SUBLANES, V7X_SUBLANES, V7X_LANES)


def _dense_scores(h2, u_t):
    T, D = h2.shape
    E = u_t.shape[1]
    tm, tn = min(SCORE_TOK_TILE, T), min(SCORE_EXP_TILE, E)
    assert T % tm == 0 and E % tn == 0 and tm % V7X_SUBLANES == 0 and tn % V7X_LANES == 0
    out = pl.pallas_call(
        _scores_kernel,
        grid=(T // tm, E // tn),
        in_specs=[pl.BlockSpec((tm, D), lambda i, j: (i, 0)), pl.BlockSpec((D, tn), lambda i, j: (0, j))],
        out_specs=pl.BlockSpec((tm // V7X_SUBLANES, tn // V7X_LANES * V7X_SUBLANES, V7X_LANES),
                               lambda i, j: (i, j, 0)),
        out_shape=jax.ShapeDtypeStruct((T // V7X_SUBLANES, E // V7X_LANES * V7X_SUBLANES, V7X_LANES),
                                       jnp.float32),
        compiler_params=pltpu.CompilerParams(
            dimension_semantics=("parallel", "arbitrary"), vmem_limit_bytes=TC_VMEM_LIMIT),
        name="dense_scores",
    )(h2, u_t)
    return out.reshape(-1)


def _act_kernel(pre_ref, gate_ref, w_ref):
    w = (gate_ref[...] * _gelu(pre_ref[...])).astype(jnp.bfloat16).astype(jnp.float32)
    hi = pltpu.bitcast(w, jnp.uint32) & jnp.uint32(HI_HALF)
    w_ref[...] = pltpu.bitcast(hi | (hi >> 16), jnp.int32)


def _expert_weights(pre, gate):
    T, K = pre.shape
    te = min(EW_TILE, T)
    spec = pl.BlockSpec((te, K), lambda i: (i, 0))
    return pl.pallas_call(
        _act_kernel, grid=(T // te,), in_specs=[spec, spec], out_specs=spec,
        out_shape=jax.ShapeDtypeStruct((T, K), jnp.int32),
        compiler_params=pltpu.CompilerParams(dimension_semantics=("parallel",)),
        name="expert_weights",
    )(pre, gate)


def _final_kernel(x_ref, y_ref, g_ref, *rest):
    o_ref = rest[-1]
    o_ref[...] = _rms(x_ref[...] + y_ref[...], g_ref[...])


def _final(x1, peer, norm_final, out_prev, tok0):
    Tc, D = peer.shape
    te = min(EW_TILE, Tc)
    assert tok0 % te == 0 and Tc % te == 0
    nblk = Tc // te
    spec = pl.BlockSpec((te, D), lambda i: (tok0 // te + i, 0))
    in_specs = [spec, pl.BlockSpec((te, D), lambda i: (i, 0)), pl.BlockSpec((1, D), lambda i: (0, 0))]
    args = [x1, peer, norm_final.reshape(1, D)]
    aliases = {}
    if out_prev is not None:
        in_specs.append(pl.BlockSpec(memory_space=pl.ANY))
        args.append(out_prev)
        aliases = {3: 0}
    return pl.pallas_call(
        _final_kernel, grid=(nblk,), in_specs=in_specs, out_specs=spec,
        out_shape=jax.ShapeDtypeStruct(x1.shape, jnp.float32),
        input_output_aliases=aliases,
        compiler_params=pltpu.CompilerParams(dimension_semantics=("parallel",)),
        name="final_norm",
    )(*args)


def _tree_sum(vals):
    while len(vals) > 1:
        nxt = [vals[i] + vals[i + 1] for i in range(0, len(vals) - 1, 2)]
        if len(vals) % 2:
            nxt.append(vals[-1])
        vals = nxt
    return vals[0]


def _sc_block_pipeline(nblk, items_per_token, loads, store, gather, compute):
    assert items_per_token % 2 == 0 and nblk >= 1

    for c in loads(0, 0):
        c.start()
    for c in loads(0, 0):
        c.wait()
    if nblk > 1:
        for c in loads(1, 1):
            c.start()
    gather(0, 0, 0, 0).start()

    @pl.loop(0, nblk)
    def _(b):
        slot = b % 2

        @pl.when(b >= 2)
        def _():
            store(b - 2, slot).wait()

        @pl.loop(0, SC_TOKENS)
        def _(t):
            for q in range(items_per_token):
                buf = q % 2
                if q + 1 < items_per_token:
                    gather(slot, t, q + 1, 1 - buf).start()
                else:
                    @pl.when(t + 1 < SC_TOKENS)
                    def _():
                        gather(slot, t + 1, 0, 1 - buf).start()

                    @pl.when(jnp.logical_and(t + 1 == SC_TOKENS, b + 1 < nblk))
                    def _():
                        for c in loads(b + 1, 1 - slot):
                            c.wait()
                        gather(1 - slot, 0, 0, 1 - buf).start()

                gather(slot, t, q, buf).wait()
                compute(slot, t, q, buf)

        store(b, slot).start()

        @pl.when(b + 2 < nblk)
        def _():
            for c in loads(b + 2, slot):
                c.start()

    if nblk >= 2:
        store(nblk - 2, nblk % 2).wait()
    store(nblk - 1, (nblk - 1) % 2).wait()


def _sc_mesh():
    return plsc.VectorSubcoreMesh(core_axis_name="c", subcore_axis_name="s")


def _sc_worker_id():
    return lax.axis_index("s") * V7X_SC_CORES + lax.axis_index("c")


def _sc_bf16(words):
    return plsc.bitcast(words, jnp.bfloat16)


def _sc_halves_f32(pairs):
    words = plsc.bitcast(pairs, jnp.uint32)
    return (plsc.bitcast(words << 16, jnp.float32),
            plsc.bitcast(words & jnp.uint32(HI_HALF), jnp.float32))


def _pick_scores(flat, slots):
    T, K = slots.shape
    nb = SC_PICK_TOKENS
    tok_per_w = T // V7X_SC_WORKERS
    nblk = tok_per_w // nb
    assert T % (V7X_SC_WORKERS * nb) == 0

    @functools.partial(
        pl.kernel, mesh=_sc_mesh(),
        out_type=jax.ShapeDtypeStruct((T, K), jnp.float32),
        compiler_params=pltpu.CompilerParams(needs_layout_passes=False),
        scratch_types=[
            pltpu.VMEM((2, nb, K), jnp.int32),
            pltpu.VMEM((2, nb, K), jnp.float32),
            pltpu.SemaphoreType.DMA((2,)),
            pltpu.SemaphoreType.DMA((2,)),
            pltpu.SemaphoreType.DMA((2,)),
        ],
        name="pick_scores")
    def k(flat_hbm, slots_hbm, out_hbm, idx_v, val_v, idx_sems, out_sems, row_sems):
        base = _sc_worker_id() * tok_per_w

        def load(b, slot):
            return pltpu.make_async_copy(slots_hbm.at[pl.ds(base + b * nb, nb)], idx_v.at[slot], idx_sems.at[slot])

        def store(b, slot):
            return pltpu.make_async_copy(val_v.at[slot], out_hbm.at[pl.ds(base + b * nb, nb)], out_sems.at[slot])

        def gathers(slot):
            return [pltpu.make_async_copy(flat_hbm.at[idx_v.at[slot, t]], val_v.at[slot, t], row_sems.at[slot])
                    for t in range(nb)]

        load(0, 0).start()

        @pl.loop(0, nblk)
        def _(b):
            slot = b % 2
            load(b, slot).wait()

            @pl.when(b + 1 < nblk)
            def _():
                load(b + 1, 1 - slot).start()

            @pl.when(b >= 2)
            def _():
                store(b - 2, slot).wait()

            for c in gathers(slot):
                c.start()
            for c in gathers(slot):
                c.wait()
            store(b, slot).start()

        if nblk >= 2:
            store(nblk - 2, nblk % 2).wait()
        store(nblk - 1, (nblk - 1) % 2).wait()

    return k(flat, slots)


def _expert_mix(w, idx, table):
    T, K = w.shape
    DW = table.shape[1]
    D = 2 * DW
    L, G = V7X_SC_LANES, SC_GATHER
    nj = SC_MIX_CHUNK // L
    tok_per_w = T // V7X_SC_WORKERS
    assert T % (V7X_SC_WORKERS * SC_TOKENS) == 0 and K % (2 * G) == 0
    assert DW % SC_MIX_CHUNK == 0 and G % SC_MIX_GROUP == 0

    @functools.partial(
        pl.kernel, mesh=_sc_mesh(),
        out_type=jax.ShapeDtypeStruct((T, D), jnp.float32),
        compiler_params=pltpu.CompilerParams(needs_layout_passes=False),
        scratch_types=[
            pltpu.VMEM((2, SC_TOKENS, K), jnp.int32),
            pltpu.VMEM((2, SC_TOKENS, K), jnp.int32),
            pltpu.VMEM((2, G, DW), jnp.uint32),
            pltpu.VMEM((2, SC_TOKENS, D), jnp.float32),
            pltpu.SemaphoreType.DMA((2,)),
            pltpu.SemaphoreType.DMA((2,)),
            pltpu.SemaphoreType.DMA((2,)),
            pltpu.SemaphoreType.DMA((2,)),
        ],
        name="expert_mix")
    def k(w_hbm, idx_hbm, tab_hbm, out_hbm, idx_v, w_v, rows_v, out_v,
          idx_sems, w_sems, out_sems, row_sems):
        base = _sc_worker_id() * tok_per_w

        def loads(b, slot):
            toks = pl.ds(base + b * SC_TOKENS, SC_TOKENS)
            return [pltpu.make_async_copy(idx_hbm.at[toks], idx_v.at[slot], idx_sems.at[slot]),
                    pltpu.make_async_copy(w_hbm.at[toks], w_v.at[slot], w_sems.at[slot])]

        def store(b, slot):
            toks = pl.ds(base + b * SC_TOKENS, SC_TOKENS)
            return pltpu.make_async_copy(out_v.at[slot], out_hbm.at[toks], out_sems.at[slot])

        def gather(slot, t, q, buf):
            return pltpu.make_async_copy(
                tab_hbm.at[idx_v.at[slot, t, pl.ds(q * G, G)]], rows_v.at[buf], row_sems.at[buf])

        def compute(slot, t, q, buf):
            ssplat = jnp.full((L,), slot, jnp.int32)
            tsplat = jnp.full((L,), t, jnp.int32)
            for c in range(DW // SC_MIX_CHUNK):
                def body(kg, acc):
                    kk = kg * SC_MIX_GROUP
                    wks = [_sc_bf16(plsc.load_gather(
                        w_v, [ssplat, tsplat, jnp.full((L,), q * G + i, jnp.int32) + kk]))
                        for i in range(SC_MIX_GROUP)]
                    out = []
                    for j in range(nj):
                        prods = [wks[i] * _sc_bf16(rows_v[buf, kk + i, pl.ds(c * SC_MIX_CHUNK + j * L, L)])
                                 for i in range(SC_MIX_GROUP)]
                        lo, hi = _sc_halves_f32(_tree_sum(prods))
                        out += [acc[2 * j] + lo, acc[2 * j + 1] + hi]
                    return tuple(out)

                zero = jnp.zeros((L,), jnp.float32)
                acc = plsc.parallel_loop(0, G // SC_MIX_GROUP, carry=(zero,) * (2 * nj))(body)
                for j in range(nj):
                    for half in range(2):
                        dst = out_v.at[slot, t, pl.ds(half * DW + c * SC_MIX_CHUNK + j * L, L)]
                        if q == 0:
                            dst[...] = acc[2 * j + half]
                        else:
                            plsc.addupdate(dst, acc[2 * j + half])

        _sc_block_pipeline(tok_per_w // SC_TOKENS, K // G, loads, store, gather, compute)

    return k(w, idx, table)


def _chunk_sizes(total):
    sizes, size = [], FIRST_CHUNK
    while total > 0:
        sizes.append(min(size, total))
        total -= sizes[-1]
        size = min(2 * size, MAX_CHUNK)
    return sizes


def kernel(x, norm_mix, w_in, pool_w, pool_scale, sgu_ln_g, sgu_ln_b, sgu_w, sgu_b, out_norm_pool,
           out_norm_sgu, w_out, norm_ffn, peer_wq, peer_keys, peer_u, peer_v, norm_final):
    B, S, D = x.shape
    assert norm_mix.shape[0] == 1, "single-layer block"
    T = B * S
    x1 = _mixer(x, norm_mix[0], w_in[0], pool_w[0], pool_scale[0], sgu_ln_g[0], sgu_ln_b[0],
                sgu_w[0], sgu_b[0], out_norm_pool[0], out_norm_sgu[0], w_out[0]).reshape(T, D)
    wq = peer_wq[0].astype(jnp.bfloat16)
    keys = peer_keys[0].astype(jnp.bfloat16)
    u_t = peer_u[0].astype(jnp.bfloat16).T
    v_tab = _pack_table(peer_v[0])
    out = None
    tok0 = 0
    ws = [norm_ffn, v_tab]
    for tc in _chunk_sizes(T):
        h2, idx, slots, gate = _router(x1, norm_ffn[0], wq, keys, tok0, tc, ws[-2])
        pre = _pick_scores(_dense_scores(h2, u_t), slots)
        w = _expert_weights(pre, gate)
        ws.append(w)
        peer = _expert_mix(w, idx, v_tab)
        out = _final(x1, peer, norm_final, out, tok0)
        tok0 += tc
    return out.reshape(B, S, D)
```

```python
import functools
import math

import jax
import jax.numpy as jnp
from jax import lax
from jax.experimental import pallas as pl
from jax.experimental.pallas import tpu as pltpu
from jax.experimental.pallas import tpu_sc as plsc

POOL_WINDOWS = (2, 4, 8, 16)
N_POOL_GROUPS = len(POOL_WINDOWS)
SGU_HEADS = 4
SGU_CHUNK = 128
PEER_HEADS = 8
PEER_N_KEYS = 128
PEER_D_HALF = 128
PEER_TOPK = 16
NORM_EPS = 1e-6
EXPERTS_PER_TOKEN = PEER_HEADS * PEER_TOPK

V7X_LANES = 128
V7X_SUBLANES = 8
V7X_SC_CORES = 2
V7X_SC_SUBCORES = 16
V7X_SC_LANES = 16
V7X_SC_WORKERS = V7X_SC_CORES * V7X_SC_SUBCORES

HALO = max(POOL_WINDOWS)
MIX_TILE = 512
ROUTE_TILE = 512
EW_TILE = 512
SCORE_TOK_TILE = 1024
SCORE_EXP_TILE = 1024
SC_PICK_TOKENS = 32
SC_GATHER = 64
SC_TOKENS = 8
SC_MIX_CHUNK = 128
SC_MIX_GROUP = 4
SC_UNROLL = 2
HI_HALF = 0xFFFF0000
TC_VMEM_LIMIT = 48 * 1024 * 1024
FIRST_CHUNK = 1024
MAX_CHUNK = 8192


def _rms(x, g):
    inv = lax.rsqrt(jnp.mean(x * x, axis=-1, keepdims=True) + NORM_EPS)
    return x * inv * g


def _pack_halves(bits):
    half = bits.shape[1] // 2
    return (bits[:, :half] >> 16) | (bits[:, half:] & jnp.uint32(HI_HALF))


def _pack_table(a):
    return _pack_halves(lax.bitcast_convert_type(a.astype(jnp.bfloat16).astype(jnp.float32), jnp.uint32))


def _gelu(x):
    return 0.5 * x * (1.0 + lax.erf(x * math.sqrt(0.5)))


def _mixer_kernel(x_ref, xh_ref, nmix_ref, win_ref, poolw_ref, pscale_ref, lng_ref, lnb_ref,
                  sguw_ref, sgub_ref, onp_ref, ons_ref, wout_ref, o_ref, pext_ref, mix_ref):
    i = pl.program_id(1)
    ts = x_ref.shape[1]
    pool_w = pscale_ref.shape[1]
    gdim = pool_w // N_POOL_GROUPS
    sgu_w = lng_ref.shape[1]
    hdim = sgu_w // SGU_HEADS

    x = x_ref[0]
    h = _rms(x, nmix_ref[...]).astype(jnp.bfloat16)
    z = jnp.dot(h, win_ref[...], preferred_element_type=jnp.float32)
    p = z[:, :pool_w]

    hh = _rms(xh_ref[0], nmix_ref[...]).astype(jnp.bfloat16)
    ph = jnp.dot(hh, win_ref[:, :pool_w], preferred_element_type=jnp.float32)
    ph = jnp.where(i > 0, ph, 0.0)
    pext_ref[0:HALO, :] = ph
    pext_ref[HALO:HALO + ts, :] = p

    pos = i * ts + lax.broadcasted_iota(jnp.int32, (ts, 1), 0)
    ssq = jnp.zeros((ts, 1), jnp.float32)
    a_parts = []
    for g, win in enumerate(POOL_WINDOWS):
        cols = slice(g * gdim, (g + 1) * gdim)
        s = pext_ref[HALO:HALO + ts, cols]
        for j in range(1, win):
            s = s + pext_ref[HALO - j:HALO - j + ts, cols]
        cnt = jnp.minimum(pos + 1, win).astype(jnp.float32)
        d = (s / cnt - p[:, cols]).astype(jnp.bfloat16)
        a = jnp.dot(d, poolw_ref[g], preferred_element_type=jnp.float32) * pscale_ref[:, cols]
        ssq = ssq + jnp.sum(a * a, axis=-1, keepdims=True)
        a_parts.append(a)
    inv_a = lax.rsqrt(ssq / pool_w + NORM_EPS)
    for g in range(N_POOL_GROUPS):
        cols = slice(g * gdim, (g + 1) * gdim)
        mix_ref[:, cols] = (a_parts[g] * inv_a * onp_ref[:, cols]).astype(jnp.bfloat16)

    gz = _gelu(z[:, pool_w:])
    tril = (lax.broadcasted_iota(jnp.int32, (SGU_CHUNK, SGU_CHUNK), 0)
            >= lax.broadcasted_iota(jnp.int32, (SGU_CHUNK, SGU_CHUNK), 1))
    ssq = jnp.zeros((ts, 1), jnp.float32)
    b_parts = []
    for hd in range(SGU_HEADS):
        cols = slice(hd * hdim, (hd + 1) * hdim)
        u = gz[:, hd * hdim:(hd + 1) * hdim]
        v = gz[:, sgu_w + hd * hdim:sgu_w + (hd + 1) * hdim]
        mu = jnp.mean(v, axis=-1, keepdims=True)
        vc = v - mu
        var = jnp.mean(vc * vc, axis=-1, keepdims=True)
        vn = (vc * lax.rsqrt(var + NORM_EPS) * lng_ref[:, cols] + lnb_ref[:, cols]).astype(jnp.bfloat16)
        w = jnp.where(tril, sguw_ref[hd], jnp.zeros((), sguw_ref.dtype))
        mixed = [jnp.dot(w, vn[n * SGU_CHUNK:(n + 1) * SGU_CHUNK], preferred_element_type=jnp.float32)
                 + sgub_ref[hd] for n in range(ts // SGU_CHUNK)]
        b = u * jnp.concatenate(mixed, axis=0)
        ssq = ssq + jnp.sum(b * b, axis=-1, keepdims=True)
        b_parts.append(b)
    inv_b = lax.rsqrt(ssq / sgu_w + NORM_EPS)
    for hd in range(SGU_HEADS):
        cols = slice(hd * hdim, (hd + 1) * hdim)
        mix_ref[:, pool_w + hd * hdim:pool_w + (hd + 1) * hdim] = (
            b_parts[hd] * inv_b * ons_ref[:, cols]).astype(jnp.bfloat16)

    o_ref[0] = x + jnp.dot(mix_ref[...], wout_ref[...], preferred_element_type=jnp.float32)


def _mixer(x, norm_mix, w_in, pool_w, pool_scale, ln_g, ln_b, sgu_w, sgu_b, on_pool, on_sgu, w_out):
    B, S, D = x.shape
    ts = min(MIX_TILE, S)
    pool_width = pool_scale.size
    sgu_width = ln_g.size
    in_width = w_in.shape[1]
    gdim = pool_width // N_POOL_GROUPS
    halo_blocks = ts // HALO
    full = lambda shape: pl.BlockSpec(shape, lambda b, i: (0,) * len(shape))
    return pl.pallas_call(
        _mixer_kernel,
        grid=(B, S // ts),
        in_specs=[
            pl.BlockSpec((1, ts, D), lambda b, i: (b, i, 0)),
            pl.BlockSpec((1, HALO, D), lambda b, i: (b, jnp.maximum(i * halo_blocks - 1, 0), 0)),
            full((1, D)),
            full((D, in_width)),
            full((N_POOL_GROUPS, gdim, gdim)),
            full((1, pool_width)),
            full((1, sgu_width)),
            full((1, sgu_width)),
            full((SGU_HEADS, SGU_CHUNK, SGU_CHUNK)),
            full((SGU_HEADS, SGU_CHUNK, SGU_CHUNK)),
            full((1, pool_width)),
            full((1, sgu_width)),
            full((pool_width + sgu_width, D)),
        ],
        out_specs=pl.BlockSpec((1, ts, D), lambda b, i: (b, i, 0)),
        out_shape=jax.ShapeDtypeStruct((B, S, D), jnp.float32),
        scratch_shapes=[
            pltpu.VMEM((HALO + ts, pool_width), jnp.float32),
            pltpu.VMEM((ts, pool_width + sgu_width), jnp.bfloat16),
        ],
        compiler_params=pltpu.CompilerParams(
            dimension_semantics=("parallel", "arbitrary"), vmem_limit_bytes=TC_VMEM_LIMIT),
        name="mixer",
    )(x, x, norm_mix.reshape(1, D), w_in.astype(jnp.bfloat16), pool_w.astype(jnp.bfloat16),
      pool_scale.reshape(1, pool_width), ln_g.reshape(1, sgu_width), ln_b.reshape(1, sgu_width),
      sgu_w.astype(jnp.bfloat16),
      jnp.broadcast_to(sgu_b[:, :, None], (SGU_HEADS, SGU_CHUNK, SGU_CHUNK)),
      on_pool.reshape(1, pool_width), on_sgu.reshape(1, sgu_width), w_out.astype(jnp.bfloat16))


def _topk_rows(s, k):
    n = s.shape[0]
    iota = lax.broadcasted_iota(jnp.int32, s.shape, 0)
    vals, idxs = [], []
    for _ in range(k):
        m = jnp.max(s, axis=0, keepdims=True)
        ix = jnp.min(jnp.where(s == m, iota, n), axis=0, keepdims=True)
        vals.append(m)
        idxs.append(ix)
        s = jnp.where(iota == ix, -jnp.inf, s)
    return vals, idxs


def _pair_candidates(v1, i1, v2, i2):
    k = PEER_TOPK
    v2c, i2c = jnp.concatenate(v2, axis=0), jnp.concatenate(i2, axis=0)
    m = v2c.shape[1]
    vals, experts, flats = [], [], []
    a = 0
    while k // (a + 1) > 1:
        rows = -(-(k // (a + 1)) // V7X_SUBLANES) * V7X_SUBLANES
        vals.append(v1[a] + v2c[:rows])
        experts.append(i1[a] * PEER_N_KEYS + i2c[:rows])
        flats.append(a * k + lax.broadcasted_iota(jnp.int32, (rows, m), 0))
        a += 1
    vals.append(jnp.concatenate(v1[a:], axis=0) + v2[0])
    experts.append(jnp.concatenate(i1[a:], axis=0) * PEER_N_KEYS + i2[0])
    flats.append((a + lax.broadcasted_iota(jnp.int32, (k - a, m), 0)) * k)
    return jnp.concatenate(vals, axis=0), jnp.concatenate(experts, axis=0), jnp.concatenate(flats, axis=0)


def _router_kernel(x_ref, nffn_ref, wq_ref, keys_ref, after_ref, h2_ref, idx_ref, slot_ref, gate_ref,
                   q_ref, idxt_ref, gatet_ref):
    del after_ref
    h2 = _rms(x_ref[...], nffn_ref[...]).astype(jnp.bfloat16)
    h2_ref[...] = h2
    q_ref[...] = jnp.dot(h2, wq_ref[...], preferred_element_type=jnp.float32).astype(jnp.bfloat16)
    dq = 2 * PEER_D_HALF
    nt = (((1,), (1,)), ((), ()))

    def head(hd, carry):
        off = pl.multiple_of(hd * dq, dq)
        s1 = lax.dot_general(keys_ref[0], q_ref[:, pl.ds(off, PEER_D_HALF)], nt,
                             preferred_element_type=jnp.float32)
        s2 = lax.dot_general(keys_ref[1], q_ref[:, pl.ds(off + PEER_D_HALF, PEER_D_HALF)], nt,
                             preferred_element_type=jnp.float32)
        v1, i1 = _topk_rows(s1, PEER_TOPK)
        v2, i2 = _topk_rows(s2, PEER_TOPK)
        cand, expert, flat = _pair_candidates(v1, i1, v2, i2)
        cv, ce = [], []
        for _ in range(PEER_TOPK):
            m = jnp.max(cand, axis=0, keepdims=True)
            ix = jnp.min(jnp.where(cand == m, flat, PEER_TOPK * PEER_TOPK), axis=0, keepdims=True)
            hit = flat == ix
            cv.append(m)
            ce.append(jnp.max(jnp.where(hit, expert, -1), axis=0, keepdims=True))
            cand = jnp.where(hit, -jnp.inf, cand)
        cvc = jnp.concatenate(cv, axis=0)
        e = jnp.exp(cvc - cv[0])
        gate = e / jnp.sum(e, axis=0, keepdims=True)
        row = pl.multiple_of(hd * PEER_TOPK, PEER_TOPK)
        idxt_ref[pl.ds(row, PEER_TOPK), :] = jnp.concatenate(ce, axis=0)
        gatet_ref[pl.ds(row, PEER_TOPK), :] = gate
        return carry

    lax.fori_loop(0, PEER_HEADS, head, 0)
    idx = idxt_ref[...].T
    idx_ref[...] = idx
    tok = pl.program_id(0) * idx.shape[0] + lax.broadcasted_iota(jnp.int32, idx.shape, 0)
    slot_ref[...] = _tile_linear_index(tok, idx, PEER_N_KEYS * PEER_N_KEYS)
    gate_ref[...] = gatet_ref[...].T


def _router(x1, norm_ffn, wq, keys, tok0, T, after):
    D = x1.shape[1]
    tr = min(ROUTE_TILE, T)
    qw = wq.shape[1]
    assert tok0 % tr == 0 and T % tr == 0
    first = tok0 // tr
    full = lambda shape: pl.BlockSpec(shape, lambda i: (0,) * len(shape))
    return pl.pallas_call(
        _router_kernel,
        grid=(T // tr,),
        in_specs=[
            pl.BlockSpec((tr, D), lambda i: (first + i, 0)),
            full((1, D)),
            full((D, qw)),
            full((2, PEER_N_KEYS, PEER_D_HALF)),
            pl.BlockSpec(memory_space=pl.ANY),
        ],
        out_specs=[
            pl.BlockSpec((tr, D), lambda i: (i, 0)),
            pl.BlockSpec((tr, EXPERTS_PER_TOKEN), lambda i: (i, 0)),
            pl.BlockSpec((tr, EXPERTS_PER_TOKEN), lambda i: (i, 0)),
            pl.BlockSpec((tr, EXPERTS_PER_TOKEN), lambda i: (i, 0)),
        ],
        out_shape=[
            jax.ShapeDtypeStruct((T, D), jnp.bfloat16),
            jax.ShapeDtypeStruct((T, EXPERTS_PER_TOKEN), jnp.int32),
            jax.ShapeDtypeStruct((T, EXPERTS_PER_TOKEN), jnp.int32),
            jax.ShapeDtypeStruct((T, EXPERTS_PER_TOKEN), jnp.float32),
        ],
        scratch_shapes=[
            pltpu.VMEM((tr, qw), jnp.bfloat16),
            pltpu.VMEM((EXPERTS_PER_TOKEN, tr), jnp.int32),
            pltpu.VMEM((EXPERTS_PER_TOKEN, tr), jnp.float32),
        ],
        compiler_params=pltpu.CompilerParams(
            dimension_semantics=("parallel",), vmem_limit_bytes=TC_VMEM_LIMIT),
        name="router",
    )(x1, norm_ffn.reshape(1, D), wq, keys, after)


def _tile_linear_index(row, col, ncols):
    return ((row >> 3) * (ncols * V7X_SUBLANES) + (col >> 7) * (V7X_SUBLANES * V7X_LANES)
            + (row & (V7X_SUBLANES - 1)) * V7X_LANES + (col & (V7X_LANES - 1)))


def _scores_kernel(h_ref, ut_ref, o_ref):
    acc = jnp.dot(h_ref[...], ut_ref[...], preferred_element_type=jnp.float32)
    tm, tn = acc.shape
    for n in range(tn // V7X_LANES):
        o_ref[:, n * V7X_SUBLANES:(n + 1) * V7X_SUBLANES, :] = acc[:, n * V7X_LANES:(n + 1) * V7X_LANES].reshape(
            tm // V7X_SUBLANES, V7X_SUBLANES, V7X_LANES)


def _dense_scores(h2, u_t):
    T, D = h2.shape
    E = u_t.shape[1]
    tm, tn = min(SCORE_TOK_TILE, T), min(SCORE_EXP_TILE, E)
    assert T % tm == 0 and E % tn == 0 and tm % V7X_SUBLANES == 0 and tn % V7X_LANES == 0
    out = pl.pallas_call(
        _scores_kernel,
        grid=(T // tm, E // tn),
        in_specs=[pl.BlockSpec((tm, D), lambda i, j: (i, 0)), pl.BlockSpec((D, tn), lambda i, j: (0, j))],
        out_specs=pl.BlockSpec((tm // V7X_SUBLANES, tn // V7X_LANES * V7X_SUBLANES, V7X_LANES),
                               lambda i, j: (i, j, 0)),
        out_shape=jax.ShapeDtypeStruct((T // V7X_SUBLANES, E // V7X_LANES * V7X_SUBLANES, V7X_LANES),
                                       jnp.float32),
        compiler_params=pltpu.CompilerParams(
            dimension_semantics=("parallel", "arbitrary"), vmem_limit_bytes=TC_VMEM_LIMIT),
        name="dense_scores",
    )(h2, u_t)
    return out.reshape(-1)


def _act_kernel(pre_ref, gate_ref, w_ref):
    w = (gate_ref[...] * _gelu(pre_ref[...])).astype(jnp.bfloat16).astype(jnp.float32)
    hi = pltpu.bitcast(w, jnp.uint32) & jnp.uint32(HI_HALF)
    w_ref[...] = pltpu.bitcast(hi | (hi >> 16), jnp.int32)


def _expert_weights(pre, gate):
    T, K = pre.shape
    te = min(EW_TILE, T)
    spec = pl.BlockSpec((te, K), lambda i: (i, 0))
    return pl.pallas_call(
        _act_kernel, grid=(T // te,), in_specs=[spec, spec], out_specs=spec,
        out_shape=jax.ShapeDtypeStruct((T, K), jnp.int32),
        compiler_params=pltpu.CompilerParams(dimension_semantics=("parallel",)),
        name="expert_weights",
    )(pre, gate)


def _final_kernel(x_ref, y_ref, g_ref, *rest):
    o_ref = rest[-1]
    o_ref[...] = _rms(x_ref[...] + y_ref[...], g_ref[...])


def _final(x1, peer, norm_final, out_prev, tok0):
    Tc, D = peer.shape
    te = min(EW_TILE, Tc)
    assert tok0 % te == 0 and Tc % te == 0
    nblk = Tc // te
    spec = pl.BlockSpec((te, D), lambda i: (tok0 // te + i, 0))
    in_specs = [spec, pl.BlockSpec((te, D), lambda i: (i, 0)), pl.BlockSpec((1, D), lambda i: (0, 0))]
    args = [x1, peer, norm_final.reshape(1, D)]
    aliases = {}
    if out_prev is not None:
        in_specs.append(pl.BlockSpec(memory_space=pl.ANY))
        args.append(out_prev)
        aliases = {3: 0}
    return pl.pallas_call(
        _final_kernel, grid=(nblk,), in_specs=in_specs, out_specs=spec,
        out_shape=jax.ShapeDtypeStruct(x1.shape, jnp.float32),
        input_output_aliases=aliases,
        compiler_params=pltpu.CompilerParams(dimension_semantics=("parallel",)),
        name="final_norm",
    )(*args)


def _tree_sum(vals):
    while len(vals) > 1:
        nxt = [vals[i] + vals[i + 1] for i in range(0, len(vals) - 1, 2)]
        if len(vals) % 2:
            nxt.append(vals[-1])
        vals = nxt
    return vals[0]


def _sc_block_pipeline(nblk, items_per_token, loads, store, gather, compute):
    assert items_per_token % 2 == 0 and nblk >= 1

    for c in loads(0, 0):
        c.start()
    for c in loads(0, 0):
        c.wait()
    if nblk > 1:
        for c in loads(1, 1):
            c.start()
    gather(0, 0, 0, 0).start()

    @pl.loop(0, nblk)
    def _(b):
        slot = b % 2

        @pl.when(b >= 2)
        def _():
            store(b - 2, slot).wait()

        @pl.loop(0, SC_TOKENS)
        def _(t):
            for q in range(items_per_token):
                buf = q % 2
                if q + 1 < items_per_token:
                    gather(slot, t, q + 1, 1 - buf).start()
                else:
                    @pl.when(t + 1 < SC_TOKENS)
                    def _():
                        gather(slot, t + 1, 0, 1 - buf).start()

                    @pl.when(jnp.logical_and(t + 1 == SC_TOKENS, b + 1 < nblk))
                    def _():
                        for c in loads(b + 1, 1 - slot):
                            c.wait()
                        gather(1 - slot, 0, 0, 1 - buf).start()

                gather(slot, t, q, buf).wait()
                compute(slot, t, q, buf)

        store(b, slot).start()

        @pl.when(b + 2 < nblk)
        def _():
            for c in loads(b + 2, slot):
                c.start()

    if nblk >= 2:
        store(nblk - 2, nblk % 2).wait()
    store(nblk - 1, (nblk - 1) % 2).wait()


def _sc_mesh():
    return plsc.VectorSubcoreMesh(core_axis_name="c", subcore_axis_name="s")


def _sc_worker_id():
    return lax.axis_index("s") * V7X_SC_CORES + lax.axis_index("c")


def _sc_bf16(words):
    return plsc.bitcast(words, jnp.bfloat16)


def _sc_halves_f32(pairs):
    words = plsc.bitcast(pairs, jnp.uint32)
    return (plsc.bitcast(words << 16, jnp.float32),
            plsc.bitcast(words & jnp.uint32(HI_HALF), jnp.float32))


def _pick_scores(flat, slots, after):
    T, K = slots.shape
    nb = SC_PICK_TOKENS
    tok_per_w = T // V7X_SC_WORKERS
    nblk = tok_per_w // nb
    assert T % (V7X_SC_WORKERS * nb) == 0

    @functools.partial(
        pl.kernel, mesh=_sc_mesh(),
        out_type=jax.ShapeDtypeStruct((T, K), jnp.float32),
        compiler_params=pltpu.CompilerParams(needs_layout_passes=False),
        scratch_types=[
            pltpu.VMEM((2, nb, K), jnp.int32),
            pltpu.VMEM((2, nb, K), jnp.float32),
            pltpu.SemaphoreType.DMA((2,)),
            pltpu.SemaphoreType.DMA((2,)),
            pltpu.SemaphoreType.DMA((2,)),
        ],
        name="pick_scores")
    def k(flat_hbm, slots_hbm, after_hbm, out_hbm, idx_v, val_v, idx_sems, out_sems, row_sems):
        del after_hbm
        base = _sc_worker_id() * tok_per_w

        def load(b, slot):
            return pltpu.make_async_copy(slots_hbm.at[pl.ds(base + b * nb, nb)], idx_v.at[slot], idx_sems.at[slot])

        def store(b, slot):
            return pltpu.make_async_copy(val_v.at[slot], out_hbm.at[pl.ds(base + b * nb, nb)], out_sems.at[slot])

        def gathers(slot):
            return [pltpu.make_async_copy(flat_hbm.at[idx_v.at[slot, t]], val_v.at[slot, t], row_sems.at[slot])
                    for t in range(nb)]

        load(0, 0).start()

        @pl.loop(0, nblk)
        def _(b):
            slot = b % 2
            load(b, slot).wait()

            @pl.when(b + 1 < nblk)
            def _():
                load(b + 1, 1 - slot).start()

            @pl.when(b >= 2)
            def _():
                store(b - 2, slot).wait()

            for c in gathers(slot):
                c.start()
            for c in gathers(slot):
                c.wait()
            store(b, slot).start()

        if nblk >= 2:
            store(nblk - 2, nblk % 2).wait()
        store(nblk - 1, (nblk - 1) % 2).wait()

    return k(flat, slots, after)


def _expert_mix(w, idx, table):
    T, K = w.shape
    DW = table.shape[1]
    D = 2 * DW
    L, G = V7X_SC_LANES, SC_GATHER
    nj = SC_MIX_CHUNK // L
    tok_per_w = T // V7X_SC_WORKERS
    assert T % (V7X_SC_WORKERS * SC_TOKENS) == 0 and K % (2 * G) == 0
    assert DW % SC_MIX_CHUNK == 0 and G % SC_MIX_GROUP == 0

    @functools.partial(
        pl.kernel, mesh=_sc_mesh(),
        out_type=jax.ShapeDtypeStruct((T, D), jnp.float32),
        compiler_params=pltpu.CompilerParams(needs_layout_passes=False),
        scratch_types=[
            pltpu.VMEM((2, SC_TOKENS, K), jnp.int32),
            pltpu.VMEM((2, SC_TOKENS, K), jnp.int32),
            pltpu.VMEM((2, G, DW), jnp.uint32),
            pltpu.VMEM((2, SC_TOKENS, D), jnp.float32),
            pltpu.SemaphoreType.DMA((2,)),
            pltpu.SemaphoreType.DMA((2,)),
            pltpu.SemaphoreType.DMA((2,)),
            pltpu.SemaphoreType.DMA((2,)),
        ],
        name="expert_mix")
    def k(w_hbm, idx_hbm, tab_hbm, out_hbm, idx_v, w_v, rows_v, out_v,
          idx_sems, w_sems, out_sems, row_sems):
        base = _sc_worker_id() * tok_per_w

        def loads(b, slot):
            toks = pl.ds(base + b * SC_TOKENS, SC_TOKENS)
            return [pltpu.make_async_copy(idx_hbm.at[toks], idx_v.at[slot], idx_sems.at[slot]),
                    pltpu.make_async_copy(w_hbm.at[toks], w_v.at[slot], w_sems.at[slot])]

        def store(b, slot):
            toks = pl.ds(base + b * SC_TOKENS, SC_TOKENS)
            return pltpu.make_async_copy(out_v.at[slot], out_hbm.at[toks], out_sems.at[slot])

        def gather(slot, t, q, buf):
            return pltpu.make_async_copy(
                tab_hbm.at[idx_v.at[slot, t, pl.ds(q * G, G)]], rows_v.at[buf], row_sems.at[buf])

        def compute(slot, t, q, buf):
            ssplat = jnp.full((L,), slot, jnp.int32)
            tsplat = jnp.full((L,), t, jnp.int32)
            for c in range(DW // SC_MIX_CHUNK):
                def body(kg, acc):
                    kk = kg * SC_MIX_GROUP
                    wks = [_sc_bf16(plsc.load_gather(
                        w_v, [ssplat, tsplat, jnp.full((L,), q * G + i, jnp.int32) + kk]))
                        for i in range(SC_MIX_GROUP)]
                    out = []
                    for j in range(nj):
                        prods = [wks[i] * _sc_bf16(rows_v[buf, kk + i, pl.ds(c * SC_MIX_CHUNK + j * L, L)])
                                 for i in range(SC_MIX_GROUP)]
                        lo, hi = _sc_halves_f32(_tree_sum(prods))
                        out += [acc[2 * j] + lo, acc[2 * j + 1] + hi]
                    return tuple(out)

                zero = jnp.zeros((L,), jnp.float32)
                acc = plsc.parallel_loop(0, G // SC_MIX_GROUP, carry=(zero,) * (2 * nj))(body)
                for j in range(nj):
                    for half in range(2):
                        dst = out_v.at[slot, t, pl.ds(half * DW + c * SC_MIX_CHUNK + j * L, L)]
                        if q == 0:
                            dst[...] = acc[2 * j + half]
                        else:
                            plsc.addupdate(dst, acc[2 * j + half])

        _sc_block_pipeline(tok_per_w // SC_TOKENS, K // G, loads, store, gather, compute)

    return k(w, idx, table)


def _chunk_sizes(total):
    sizes, size = [], FIRST_CHUNK
    while total > 0:
        sizes.append(min(size, total))
        total -= sizes[-1]
        size = min(2 * size, MAX_CHUNK)
    return sizes


def kernel(x, norm_mix, w_in, pool_w, pool_scale, sgu_ln_g, sgu_ln_b, sgu_w, sgu_b, out_norm_pool,
           out_norm_sgu, w_out, norm_ffn, peer_wq, peer_keys, peer_u, peer_v, norm_final):
    B, S, D = x.shape
    assert norm_mix.shape[0] == 1, "single-layer block"
    T = B * S
    x1 = _mixer(x, norm_mix[0], w_in[0], pool_w[0], pool_scale[0], sgu_ln_g[0], sgu_ln_b[0],
                sgu_w[0], sgu_b[0], out_norm_pool[0], out_norm_sgu[0], w_out[0]).reshape(T, D)
    wq = peer_wq[0].astype(jnp.bfloat16)
    keys = peer_keys[0].astype(jnp.bfloat16)
    u_t = peer_u[0].astype(jnp.bfloat16).T
    v_tab = _pack_table(peer_v[0])
    out = None
    tok0 = 0
    ws = [norm_ffn, v_tab]
    peers = [norm_ffn, norm_ffn]
    for tc in _chunk_sizes(T):
        h2, idx, slots, gate = _router(x1, norm_ffn[0], wq, keys, tok0, tc, ws[-2])
        pre = _pick_scores(_dense_scores(h2, u_t), slots, peers[-2])
        w = _expert_weights(pre, gate)
        ws.append(w)
        peer = _expert_mix(w, idx, v_tab)
        peers.append(peer)
        out = _final(x1, peer, norm_final, out, tok0)
        tok0 += tc
    return out.reshape(B, S, D)
```

```python
import functools
import math

import jax
import jax.numpy as jnp
from jax import lax
from jax.experimental import pallas as pl
from jax.experimental.pallas import tpu as pltpu
from jax.experimental.pallas import tpu_sc as plsc

POOL_WINDOWS = (2, 4, 8, 16)
N_POOL_GROUPS = len(POOL_WINDOWS)
SGU_HEADS = 4
SGU_CHUNK = 128
PEER_HEADS = 8
PEER_N_KEYS = 128
PEER_D_HALF = 128
PEER_TOPK = 16
NORM_EPS = 1e-6
EXPERTS_PER_TOKEN = PEER_HEADS * PEER_TOPK

V7X_LANES = 128
V7X_SUBLANES = 8
V7X_SC_CORES = 2
V7X_SC_SUBCORES = 16
V7X_SC_LANES = 16
V7X_SC_WORKERS = V7X_SC_CORES * V7X_SC_SUBCORES

HALO = max(POOL_WINDOWS)
MIX_TILE = 512
ROUTE_TILE = 512
EW_TILE = 512
SCORE_TOK_TILE = 1024
SCORE_EXP_TILE = 1024
SC_PICK_TOKENS = 32
SC_GATHER = 64
SC_TOKENS = 8
SC_MIX_CHUNK = 128
SC_MIX_GROUP = 4
SC_UNROLL = 2
HI_HALF = 0xFFFF0000
TC_VMEM_LIMIT = 48 * 1024 * 1024
FIRST_CHUNK = 1024
MAX_CHUNK = 2048


def _rms(x, g):
    inv = lax.rsqrt(jnp.mean(x * x, axis=-1, keepdims=True) + NORM_EPS)
    return x * inv * g


def _pack_halves(bits):
    half = bits.shape[1] // 2
    return (bits[:, :half] >> 16) | (bits[:, half:] & jnp.uint32(HI_HALF))


def _pack_table(a):
    return _pack_halves(lax.bitcast_convert_type(a.astype(jnp.bfloat16).astype(jnp.float32), jnp.uint32))


def _gelu(x):
    return 0.5 * x * (1.0 + lax.erf(x * math.sqrt(0.5)))


def _mixer_kernel(x_ref, xh_ref, nmix_ref, win_ref, poolw_ref, pscale_ref, lng_ref, lnb_ref,
                  sguw_ref, sgub_ref, onp_ref, ons_ref, wout_ref, o_ref, pext_ref, mix_ref):
    i = pl.program_id(1)
    ts = x_ref.shape[1]
    pool_w = pscale_ref.shape[1]
    gdim = pool_w // N_POOL_GROUPS
    sgu_w = lng_ref.shape[1]
    hdim = sgu_w // SGU_HEADS

    x = x_ref[0]
    h = _rms(x, nmix_ref[...]).astype(jnp.bfloat16)
    z = jnp.dot(h, win_ref[...], preferred_element_type=jnp.float32)
    p = z[:, :pool_w]

    hh = _rms(xh_ref[0], nmix_ref[...]).astype(jnp.bfloat16)
    ph = jnp.dot(hh, win_ref[:, :pool_w], preferred_element_type=jnp.float32)
    ph = jnp.where(i > 0, ph, 0.0)
    pext_ref[0:HALO, :] = ph
    pext_ref[HALO:HALO + ts, :] = p

    pos = i * ts + lax.broadcasted_iota(jnp.int32, (ts, 1), 0)
    ssq = jnp.zeros((ts, 1), jnp.float32)
    a_parts = []
    for g, win in enumerate(POOL_WINDOWS):
        cols = slice(g * gdim, (g + 1) * gdim)
        s = pext_ref[HALO:HALO + ts, cols]
        for j in range(1, win):
            s = s + pext_ref[HALO - j:HALO - j + ts, cols]
        cnt = jnp.minimum(pos + 1, win).astype(jnp.float32)
        d = (s / cnt - p[:, cols]).astype(jnp.bfloat16)
        a = jnp.dot(d, poolw_ref[g], preferred_element_type=jnp.float32) * pscale_ref[:, cols]
        ssq = ssq + jnp.sum(a * a, axis=-1, keepdims=True)
        a_parts.append(a)
    inv_a = lax.rsqrt(ssq / pool_w + NORM_EPS)
    for g in range(N_POOL_GROUPS):
        cols = slice(g * gdim, (g + 1) * gdim)
        mix_ref[:, cols] = (a_parts[g] * inv_a * onp_ref[:, cols]).astype(jnp.bfloat16)

    gz = _gelu(z[:, pool_w:])
    tril = (lax.broadcasted_iota(jnp.int32, (SGU_CHUNK, SGU_CHUNK), 0)
            >= lax.broadcasted_iota(jnp.int32, (SGU_CHUNK, SGU_CHUNK), 1))
    ssq = jnp.zeros((ts, 1), jnp.float32)
    b_parts = []
    for hd in range(SGU_HEADS):
        cols = slice(hd * hdim, (hd + 1) * hdim)
        u = gz[:, hd * hdim:(hd + 1) * hdim]
        v = gz[:, sgu_w + hd * hdim:sgu_w + (hd + 1) * hdim]
        mu = jnp.mean(v, axis=-1, keepdims=True)
        vc = v - mu
        var = jnp.mean(vc * vc, axis=-1, keepdims=True)
        vn = (vc * lax.rsqrt(var + NORM_EPS) * lng_ref[:, cols] + lnb_ref[:, cols]).astype(jnp.bfloat16)
        w = jnp.where(tril, sguw_ref[hd], jnp.zeros((), sguw_ref.dtype))
        mixed = [jnp.dot(w, vn[n * SGU_CHUNK:(n + 1) * SGU_CHUNK], preferred_element_type=jnp.float32)
                 + sgub_ref[hd] for n in range(ts // SGU_CHUNK)]
        b = u * jnp.concatenate(mixed, axis=0)
        ssq = ssq + jnp.sum(b * b, axis=-1, keepdims=True)
        b_parts.append(b)
    inv_b = lax.rsqrt(ssq / sgu_w + NORM_EPS)
    for hd in range(SGU_HEADS):
        cols = slice(hd * hdim, (hd + 1) * hdim)
        mix_ref[:, pool_w + hd * hdim:pool_w + (hd + 1) * hdim] = (
            b_parts[hd] * inv_b * ons_ref[:, cols]).astype(jnp.bfloat16)

    o_ref[0] = x + jnp.dot(mix_ref[...], wout_ref[...], preferred_element_type=jnp.float32)


def _mixer(x, norm_mix, w_in, pool_w, pool_scale, ln_g, ln_b, sgu_w, sgu_b, on_pool, on_sgu, w_out):
    B, S, D = x.shape
    ts = min(MIX_TILE, S)
    pool_width = pool_scale.size
    sgu_width = ln_g.size
    in_width = w_in.shape[1]
    gdim = pool_width // N_POOL_GROUPS
    halo_blocks = ts // HALO
    full = lambda shape: pl.BlockSpec(shape, lambda b, i: (0,) * len(shape))
    return pl.pallas_call(
        _mixer_kernel,
        grid=(B, S // ts),
        in_specs=[
            pl.BlockSpec((1, ts, D), lambda b, i: (b, i, 0)),
            pl.BlockSpec((1, HALO, D), lambda b, i: (b, jnp.maximum(i * halo_blocks - 1, 0), 0)),
            full((1, D)),
            full((D, in_width)),
            full((N_POOL_GROUPS, gdim, gdim)),
            full((1, pool_width)),
            full((1, sgu_width)),
            full((1, sgu_width)),
            full((SGU_HEADS, SGU_CHUNK, SGU_CHUNK)),
            full((SGU_HEADS, SGU_CHUNK, SGU_CHUNK)),
            full((1, pool_width)),
            full((1, sgu_width)),
            full((pool_width + sgu_width, D)),
        ],
        out_specs=pl.BlockSpec((1, ts, D), lambda b, i: (b, i, 0)),
        out_shape=jax.ShapeDtypeStruct((B, S, D), jnp.float32),
        scratch_shapes=[
            pltpu.VMEM((HALO + ts, pool_width), jnp.float32),
            pltpu.VMEM((ts, pool_width + sgu_width), jnp.bfloat16),
        ],
        compiler_params=pltpu.CompilerParams(
            dimension_semantics=("parallel", "arbitrary"), vmem_limit_bytes=TC_VMEM_LIMIT),
        name="mixer",
    )(x, x, norm_mix.reshape(1, D), w_in.astype(jnp.bfloat16), pool_w.astype(jnp.bfloat16),
      pool_scale.reshape(1, pool_width), ln_g.reshape(1, sgu_width), ln_b.reshape(1, sgu_width),
      sgu_w.astype(jnp.bfloat16),
      jnp.broadcast_to(sgu_b[:, :, None], (SGU_HEADS, SGU_CHUNK, SGU_CHUNK)),
      on_pool.reshape(1, pool_width), on_sgu.reshape(1, sgu_width), w_out.astype(jnp.bfloat16))


def _topk_rows(s, k):
    n = s.shape[0]
    iota = lax.broadcasted_iota(jnp.int32, s.shape, 0)
    vals, idxs = [], []
    for _ in range(k):
        m = jnp.max(s, axis=0, keepdims=True)
        ix = jnp.min(jnp.where(s == m, iota, n), axis=0, keepdims=True)
        vals.append(m)
        idxs.append(ix)
        s = jnp.where(iota == ix, -jnp.inf, s)
    return vals, idxs


def _pair_candidates(v1, i1, v2, i2):
    k = PEER_TOPK
    v2c, i2c = jnp.concatenate(v2, axis=0), jnp.concatenate(i2, axis=0)
    m = v2c.shape[1]
    vals, experts, flats = [], [], []
    a = 0
    while k // (a + 1) > 1:
        rows = -(-(k // (a + 1)) // V7X_SUBLANES) * V7X_SUBLANES
        vals.append(v1[a] + v2c[:rows])
        experts.append(i1[a] * PEER_N_KEYS + i2c[:rows])
        flats.append(a * k + lax.broadcasted_iota(jnp.int32, (rows, m), 0))
        a += 1
    vals.append(jnp.concatenate(v1[a:], axis=0) + v2[0])
    experts.append(jnp.concatenate(i1[a:], axis=0) * PEER_N_KEYS + i2[0])
    flats.append((a + lax.broadcasted_iota(jnp.int32, (k - a, m), 0)) * k)
    return jnp.concatenate(vals, axis=0), jnp.concatenate(experts, axis=0), jnp.concatenate(flats, axis=0)


def _router_kernel(x_ref, nffn_ref, wq_ref, keys_ref, after_ref, h2_ref, idx_ref, slot_ref, gate_ref,
                   q_ref, idxt_ref, gatet_ref):
    del after_ref
    h2 = _rms(x_ref[...], nffn_ref[...]).astype(jnp.bfloat16)
    h2_ref[...] = h2
    q_ref[...] = jnp.dot(h2, wq_ref[...], preferred_element_type=jnp.float32).astype(jnp.bfloat16)
    dq = 2 * PEER_D_HALF
    nt = (((1,), (1,)), ((), ()))

    def head(hd, carry):
        off = pl.multiple_of(hd * dq, dq)
        s1 = lax.dot_general(keys_ref[0], q_ref[:, pl.ds(off, PEER_D_HALF)], nt,
                             preferred_element_type=jnp.float32)
        s2 = lax.dot_general(keys_ref[1], q_ref[:, pl.ds(off + PEER_D_HALF, PEER_D_HALF)], nt,
                             preferred_element_type=jnp.float32)
        v1, i1 = _topk_rows(s1, PEER_TOPK)
        v2, i2 = _topk_rows(s2, PEER_TOPK)
        cand, expert, flat = _pair_candidates(v1, i1, v2, i2)
        cv, ce = [], []
        for _ in range(PEER_TOPK):
            m = jnp.max(cand, axis=0, keepdims=True)
            ix = jnp.min(jnp.where(cand == m, flat, PEER_TOPK * PEER_TOPK), axis=0, keepdims=True)
            hit = flat == ix
            cv.append(m)
            ce.append(jnp.max(jnp.where(hit, expert, -1), axis=0, keepdims=True))
            cand = jnp.where(hit, -jnp.inf, cand)
        cvc = jnp.concatenate(cv, axis=0)
        e = jnp.exp(cvc - cv[0])
        gate = e / jnp.sum(e, axis=0, keepdims=True)
        row = pl.multiple_of(hd * PEER_TOPK, PEER_TOPK)
        idxt_ref[pl.ds(row, PEER_TOPK), :] = jnp.concatenate(ce, axis=0)
        gatet_ref[pl.ds(row, PEER_TOPK), :] = gate
        return carry

    lax.fori_loop(0, PEER_HEADS, head, 0)
    idx = idxt_ref[...].T
    idx_ref[...] = idx
    tok = pl.program_id(0) * idx.shape[0] + lax.broadcasted_iota(jnp.int32, idx.shape, 0)
    slot_ref[...] = _tile_linear_index(tok, idx, PEER_N_KEYS * PEER_N_KEYS)
    gate_ref[...] = gatet_ref[...].T


def _router(x1, norm_ffn, wq, keys, tok0, T, after):
    D = x1.shape[1]
    tr = min(ROUTE_TILE, T)
    qw = wq.shape[1]
    assert tok0 % tr == 0 and T % tr == 0
    first = tok0 // tr
    full = lambda shape: pl.BlockSpec(shape, lambda i: (0,) * len(shape))
    return pl.pallas_call(
        _router_kernel,
        grid=(T // tr,),
        in_specs=[
            pl.BlockSpec((tr, D), lambda i: (first + i, 0)),
            full((1, D)),
            full((D, qw)),
            full((2, PEER_N_KEYS, PEER_D_HALF)),
            pl.BlockSpec(memory_space=pl.ANY),
        ],
        out_specs=[
            pl.BlockSpec((tr, D), lambda i: (i, 0)),
            pl.BlockSpec((tr, EXPERTS_PER_TOKEN), lambda i: (i, 0)),
            pl.BlockSpec((tr, EXPERTS_PER_TOKEN), lambda i: (i, 0)),
            pl.BlockSpec((tr, EXPERTS_PER_TOKEN), lambda i: (i, 0)),
        ],
        out_shape=[
            jax.ShapeDtypeStruct((T, D), jnp.bfloat16),
            jax.ShapeDtypeStruct((T, EXPERTS_PER_TOKEN), jnp.int32),
            jax.ShapeDtypeStruct((T, EXPERTS_PER_TOKEN), jnp.int32),
            jax.ShapeDtypeStruct((T, EXPERTS_PER_TOKEN), jnp.float32),
        ],
        scratch_shapes=[
            pltpu.VMEM((tr, qw), jnp.bfloat16),
            pltpu.VMEM((EXPERTS_PER_TOKEN, tr), jnp.int32),
            pltpu.VMEM((EXPERTS_PER_TOKEN, tr), jnp.float32),
        ],
        compiler_params=pltpu.CompilerParams(
            dimension_semantics=("parallel",), vmem_limit_bytes=TC_VMEM_LIMIT),
        name="router",
    )(x1, norm_ffn.reshape(1, D), wq, keys, after)


def _tile_linear_index(row, col, ncols):
    return ((row >> 3) * (ncols * V7X_SUBLANES) + (col >> 7) * (V7X_SUBLANES * V7X_LANES)
            + (row & (V7X_SUBLANES - 1)) * V7X_LANES + (col & (V7X_LANES - 1)))


def _scores_kernel(h_ref, ut_ref, o_ref):
    acc = jnp.dot(h_ref[...], ut_ref[...], preferred_element_type=jnp.float32)
    tm, tn = acc.shape
    for n in range(tn // V7X_LANES):
        o_ref[:, n * V7X_SUBLANES:(n + 1) * V7X_SUBLANES, :] = acc[:, n * V7X_LANES:(n + 1) * V7X_LANES].reshape(
            tm // V7X_SUBLANES, V7X_SUBLANES, V7X_LANES)


def _dense_scores(h2, u_t):
    T, D = h2.shape
    E = u_t.shape[1]
    tm, tn = min(SCORE_TOK_TILE, T), min(SCORE_EXP_TILE, E)
    assert T % tm == 0 and E % tn == 0 and tm % V7X_SUBLANES == 0 and tn % V7X_LANES == 0
    out = pl.pallas_call(
        _scores_kernel,
        grid=(T // tm, E // tn),
        in_specs=[pl.BlockSpec((tm, D), lambda i, j: (i, 0)), pl.BlockSpec((D, tn), lambda i, j: (0, j))],
        out_specs=pl.BlockSpec((tm // V7X_SUBLANES, tn // V7X_LANES * V7X_SUBLANES, V7X_LANES),
                               lambda i, j: (i, j, 0)),
        out_shape=jax.ShapeDtypeStruct((T // V7X_SUBLANES, E // V7X_LANES * V7X_SUBLANES, V7X_LANES),
                                       jnp.float32),
        compiler_params=pltpu.CompilerParams(
            dimension_semantics=("parallel", "arbitrary"), vmem_limit_bytes=TC_VMEM_LIMIT),
        name="dense_scores",
    )(h2, u_t)
    return out.reshape(-1)


def _act_kernel(pre_ref, gate_ref, w_ref):
    w = (gate_ref[...] * _gelu(pre_ref[...])).astype(jnp.bfloat16).astype(jnp.float32)
    hi = pltpu.bitcast(w, jnp.uint32) & jnp.uint32(HI_HALF)
    w_ref[...] = pltpu.bitcast(hi | (hi >> 16), jnp.int32)


def _expert_weights(pre, gate):
    T, K = pre.shape
    te = min(EW_TILE, T)
    spec = pl.BlockSpec((te, K), lambda i: (i, 0))
    return pl.pallas_call(
        _act_kernel, grid=(T // te,), in_specs=[spec, spec], out_specs=spec,
        out_shape=jax.ShapeDtypeStruct((T, K), jnp.int32),
        compiler_params=pltpu.CompilerParams(dimension_semantics=("parallel",)),
        name="expert_weights",
    )(pre, gate)


def _final_kernel(x_ref, y_ref, g_ref, *rest):
    o_ref = rest[-1]
    o_ref[...] = _rms(x_ref[...] + y_ref[...], g_ref[...])


def _final(x1, peer, norm_final, out_prev, tok0):
    Tc, D = peer.shape
    te = min(EW_TILE, Tc)
    assert tok0 % te == 0 and Tc % te == 0
    nblk = Tc // te
    spec = pl.BlockSpec((te, D), lambda i: (tok0 // te + i, 0))
    in_specs = [spec, pl.BlockSpec((te, D), lambda i: (i, 0)), pl.BlockSpec((1, D), lambda i: (0, 0))]
    args = [x1, peer, norm_final.reshape(1, D)]
    aliases = {}
    if out_prev is not None:
        in_specs.append(pl.BlockSpec(memory_space=pl.ANY))
        args.append(out_prev)
        aliases = {3: 0}
    return pl.pallas_call(
        _final_kernel, grid=(nblk,), in_specs=in_specs, out_specs=spec,
        out_shape=jax.ShapeDtypeStruct(x1.shape, jnp.float32),
        input_output_aliases=aliases,
        compiler_params=pltpu.CompilerParams(dimension_semantics=("parallel",)),
        name="final_norm",
    )(*args)


def _tree_sum(vals):
    while len(vals) > 1:
        nxt = [vals[i] + vals[i + 1] for i in range(0, len(vals) - 1, 2)]
        if len(vals) % 2:
            nxt.append(vals[-1])
        vals = nxt
    return vals[0]


def _sc_block_pipeline(nblk, items_per_token, loads, store, gather, compute):
    assert items_per_token % 2 == 0 and nblk >= 1

    for c in loads(0, 0):
        c.start()
    for c in loads(0, 0):
        c.wait()
    if nblk > 1:
        for c in loads(1, 1):
            c.start()
    gather(0, 0, 0, 0).start()

    @pl.loop(0, nblk)
    def _(b):
        slot = b % 2

        @pl.when(b >= 2)
        def _():
            store(b - 2, slot).wait()

        @pl.loop(0, SC_TOKENS)
        def _(t):
            for q in range(items_per_token):
                buf = q % 2
                if q + 1 < items_per_token:
                    gather(slot, t, q + 1, 1 - buf).start()
                else:
                    @pl.when(t + 1 < SC_TOKENS)
                    def _():
                        gather(slot, t + 1, 0, 1 - buf).start()

                    @pl.when(jnp.logical_and(t + 1 == SC_TOKENS, b + 1 < nblk))
                    def _():
                        for c in loads(b + 1, 1 - slot):
                            c.wait()
                        gather(1 - slot, 0, 0, 1 - buf).start()

                gather(slot, t, q, buf).wait()
                compute(slot, t, q, buf)

        store(b, slot).start()

        @pl.when(b + 2 < nblk)
        def _():
            for c in loads(b + 2, slot):
                c.start()

    if nblk >= 2:
        store(nblk - 2, nblk % 2).wait()
    store(nblk - 1, (nblk - 1) % 2).wait()


def _sc_mesh():
    return plsc.VectorSubcoreMesh(core_axis_name="c", subcore_axis_name="s")


def _sc_worker_id():
    return lax.axis_index("s") * V7X_SC_CORES + lax.axis_index("c")


def _sc_bf16(words):
    return plsc.bitcast(words, jnp.bfloat16)


def _sc_halves_f32(pairs):
    words = plsc.bitcast(pairs, jnp.uint32)
    return (plsc.bitcast(words << 16, jnp.float32),
            plsc.bitcast(words & jnp.uint32(HI_HALF), jnp.float32))


def _pick_scores(flat, slots, after):
    T, K = slots.shape
    nb = SC_PICK_TOKENS
    tok_per_w = T // V7X_SC_WORKERS
    nblk = tok_per_w // nb
    assert T % (V7X_SC_WORKERS * nb) == 0

    @functools.partial(
        pl.kernel, mesh=_sc_mesh(),
        out_type=jax.ShapeDtypeStruct((T, K), jnp.float32),
        compiler_params=pltpu.CompilerParams(needs_layout_passes=False),
        scratch_types=[
            pltpu.VMEM((2, nb, K), jnp.int32),
            pltpu.VMEM((2, nb, K), jnp.float32),
            pltpu.SemaphoreType.DMA((2,)),
            pltpu.SemaphoreType.DMA((2,)),
            pltpu.SemaphoreType.DMA((2,)),
        ],
        name="pick_scores")
    def k(flat_hbm, slots_hbm, after_hbm, out_hbm, idx_v, val_v, idx_sems, out_sems, row_sems):
        del after_hbm
        base = _sc_worker_id() * tok_per_w

        def load(b, slot):
            return pltpu.make_async_copy(slots_hbm.at[pl.ds(base + b * nb, nb)], idx_v.at[slot], idx_sems.at[slot])

        def store(b, slot):
            return pltpu.make_async_copy(val_v.at[slot], out_hbm.at[pl.ds(base + b * nb, nb)], out_sems.at[slot])

        def gathers(slot):
            return [pltpu.make_async_copy(flat_hbm.at[idx_v.at[slot, t]], val_v.at[slot, t], row_sems.at[slot])
                    for t in range(nb)]

        load(0, 0).start()

        @pl.loop(0, nblk)
        def _(b):
            slot = b % 2
            load(b, slot).wait()

            @pl.when(b + 1 < nblk)
            def _():
                load(b + 1, 1 - slot).start()

            @pl.when(b >= 2)
            def _():
                store(b - 2, slot).wait()

            for c in gathers(slot):
                c.start()
            for c in gathers(slot):
                c.wait()
            store(b, slot).start()

        if nblk >= 2:
            store(nblk - 2, nblk % 2).wait()
        store(nblk - 1, (nblk - 1) % 2).wait()

    return k(flat, slots, after)


def _expert_mix(w, idx, table):
    T, K = w.shape
    DW = table.shape[1]
    D = 2 * DW
    L, G = V7X_SC_LANES, SC_GATHER
    nj = SC_MIX_CHUNK // L
    tok_per_w = T // V7X_SC_WORKERS
    assert T % (V7X_SC_WORKERS * SC_TOKENS) == 0 and K % (2 * G) == 0
    assert DW % SC_MIX_CHUNK == 0 and G % SC_MIX_GROUP == 0

    @functools.partial(
        pl.kernel, mesh=_sc_mesh(),
        out_type=jax.ShapeDtypeStruct((T, D), jnp.float32),
        compiler_params=pltpu.CompilerParams(needs_layout_passes=False),
        scratch_types=[
            pltpu.VMEM((2, SC_TOKENS, K), jnp.int32),
            pltpu.VMEM((2, SC_TOKENS, K), jnp.int32),
            pltpu.VMEM((2, G, DW), jnp.uint32),
            pltpu.VMEM((2, SC_TOKENS, D), jnp.float32),
            pltpu.SemaphoreType.DMA((2,)),
            pltpu.SemaphoreType.DMA((2,)),
            pltpu.SemaphoreType.DMA((2,)),
            pltpu.SemaphoreType.DMA((2,)),
        ],
        name="expert_mix")
    def k(w_hbm, idx_hbm, tab_hbm, out_hbm, idx_v, w_v, rows_v, out_v,
          idx_sems, w_sems, out_sems, row_sems):
        base = _sc_worker_id() * tok_per_w

        def loads(b, slot):
            toks = pl.ds(base + b * SC_TOKENS, SC_TOKENS)
            return [pltpu.make_async_copy(idx_hbm.at[toks], idx_v.at[slot], idx_sems.at[slot]),
                    pltpu.make_async_copy(w_hbm.at[toks], w_v.at[slot], w_sems.at[slot])]

        def store(b, slot):
            toks = pl.ds(base + b * SC_TOKENS, SC_TOKENS)
            return pltpu.make_async_copy(out_v.at[slot], out_hbm.at[toks], out_sems.at[slot])

        def gather(slot, t, q, buf):
            return pltpu.make_async_copy(
                tab_hbm.at[idx_v.at[slot, t, pl.ds(q * G, G)]], rows_v.at[buf], row_sems.at[buf])

        def compute(slot, t, q, buf):
            ssplat = jnp.full((L,), slot, jnp.int32)
            tsplat = jnp.full((L,), t, jnp.int32)
            for c in range(DW // SC_MIX_CHUNK):
                def body(kg, acc):
                    kk = kg * SC_MIX_GROUP
                    wks = [_sc_bf16(plsc.load_gather(
                        w_v, [ssplat, tsplat, jnp.full((L,), q * G + i, jnp.int32) + kk]))
                        for i in range(SC_MIX_GROUP)]
                    out = []
                    for j in range(nj):
                        prods = [wks[i] * _sc_bf16(rows_v[buf, kk + i, pl.ds(c * SC_MIX_CHUNK + j * L, L)])
                                 for i in range(SC_MIX_GROUP)]
                        lo, hi = _sc_halves_f32(_tree_sum(prods))
                        out += [acc[2 * j] + lo, acc[2 * j + 1] + hi]
                    return tuple(out)

                zero = jnp.zeros((L,), jnp.float32)
                acc = plsc.parallel_loop(0, G // SC_MIX_GROUP, carry=(zero,) * (2 * nj))(body)
                for j in range(nj):
                    for half in range(2):
                        dst = out_v.at[slot, t, pl.ds(half * DW + c * SC_MIX_CHUNK + j * L, L)]
                        if q == 0:
                            dst[...] = acc[2 * j + half]
                        else:
                            plsc.addupdate(dst, acc[2 * j + half])

        _sc_block_pipeline(tok_per_w // SC_TOKENS, K // G, loads, store, gather, compute)

    return k(w, idx, table)


def _chunk_sizes(total):
    sizes, size = [], FIRST_CHUNK
    while total > 0:
        sizes.append(min(size, total))
        total -= sizes[-1]
        size = min(2 * size, MAX_CHUNK)
    return sizes


def kernel(x, norm_mix, w_in, pool_w, pool_scale, sgu_ln_g, sgu_ln_b, sgu_w, sgu_b, out_norm_pool,
           out_norm_sgu, w_out, norm_ffn, peer_wq, peer_keys, peer_u, peer_v, norm_final):
    B, S, D = x.shape
    assert norm_mix.shape[0] == 1, "single-layer block"
    T = B * S
    x1 = _mixer(x, norm_mix[0], w_in[0], pool_w[0], pool_scale[0], sgu_ln_g[0], sgu_ln_b[0],
                sgu_w[0], sgu_b[0], out_norm_pool[0], out_norm_sgu[0], w_out[0]).reshape(T, D)
    wq = peer_wq[0].astype(jnp.bfloat16)
    keys = peer_keys[0].astype(jnp.bfloat16)
    u_t = peer_u[0].astype(jnp.bfloat16).T
    v_tab = _pack_table(peer_v[0])
    out = None
    tok0 = 0
    ws = [norm_ffn, v_tab]
    peers = [norm_ffn, norm_ffn]
    for tc in _chunk_sizes(T):
        h2, idx, slots, gate = _router(x1, norm_ffn[0], wq, keys, tok0, tc, ws[-2])
        pre = _pick_scores(_dense_scores(h2, u_t), slots, peers[-2])
        w = _expert_weights(pre, gate)
        ws.append(w)
        peer = _expert_mix(w, idx, v_tab)
        peers.append(peer)
        out = _final(x1, peer, norm_final, out, tok0)
        tok0 += tc
    return out.reshape(B, S, D)
```

```python
import functools
import math

import jax
import jax.numpy as jnp
from jax import lax
from jax.experimental import pallas as pl
from jax.experimental.pallas import tpu as pltpu
from jax.experimental.pallas import tpu_sc as plsc

POOL_WINDOWS = (2, 4, 8, 16)
N_POOL_GROUPS = len(POOL_WINDOWS)
SGU_HEADS = 4
SGU_CHUNK = 128
PEER_HEADS = 8
PEER_N_KEYS = 128
PEER_D_HALF = 128
PEER_TOPK = 16
NORM_EPS = 1e-6
EXPERTS_PER_TOKEN = PEER_HEADS * PEER_TOPK

V7X_LANES = 128
V7X_SUBLANES = 8
V7X_SC_CORES = 2
V7X_SC_SUBCORES = 16
V7X_SC_LANES = 16
V7X_SC_WORKERS = V7X_SC_CORES * V7X_SC_SUBCORES

HALO = max(POOL_WINDOWS)
MIX_TILE = 512
ROUTE_TILE = 512
EW_TILE = 512
SCORE_TOK_TILE = 1024
SCORE_EXP_TILE = 1024
SC_PICK_TOKENS = 16
SC_GATHER = 64
SC_TOKENS = 8
SC_MIX_CHUNK = 128
SC_MIX_GROUP = 4
SC_UNROLL = 2
HI_HALF = 0xFFFF0000
TC_VMEM_LIMIT = 48 * 1024 * 1024
EDGE_CHUNK = 512
MAX_CHUNK = 2048


def _rms(x, g):
    inv = lax.rsqrt(jnp.mean(x * x, axis=-1, keepdims=True) + NORM_EPS)
    return x * inv * g


def _pack_halves(bits):
    half = bits.shape[1] // 2
    return (bits[:, :half] >> 16) | (bits[:, half:] & jnp.uint32(HI_HALF))


def _pack_table(a):
    return _pack_halves(lax.bitcast_convert_type(a.astype(jnp.bfloat16).astype(jnp.float32), jnp.uint32))


def _gelu(x):
    return 0.5 * x * (1.0 + lax.erf(x * math.sqrt(0.5)))


def _mixer_kernel(x_ref, xh_ref, nmix_ref, win_ref, poolw_ref, pscale_ref, lng_ref, lnb_ref,
                  sguw_ref, sgub_ref, onp_ref, ons_ref, wout_ref, o_ref, pext_ref, mix_ref):
    i = pl.program_id(1)
    ts = x_ref.shape[1]
    pool_w = pscale_ref.shape[1]
    gdim = pool_w // N_POOL_GROUPS
    sgu_w = lng_ref.shape[1]
    hdim = sgu_w // SGU_HEADS

    x = x_ref[0]
    h = _rms(x, nmix_ref[...]).astype(jnp.bfloat16)
    z = jnp.dot(h, win_ref[...], preferred_element_type=jnp.float32)
    p = z[:, :pool_w]

    hh = _rms(xh_ref[0], nmix_ref[...]).astype(jnp.bfloat16)
    ph = jnp.dot(hh, win_ref[:, :pool_w], preferred_element_type=jnp.float32)
    ph = jnp.where(i > 0, ph, 0.0)
    pext_ref[0:HALO, :] = ph
    pext_ref[HALO:HALO + ts, :] = p

    pos = i * ts + lax.broadcasted_iota(jnp.int32, (ts, 1), 0)
    ssq = jnp.zeros((ts, 1), jnp.float32)
    a_parts = []
    for g, win in enumerate(POOL_WINDOWS):
        cols = slice(g * gdim, (g + 1) * gdim)
        s = pext_ref[HALO:HALO + ts, cols]
        for j in range(1, win):
            s = s + pext_ref[HALO - j:HALO - j + ts, cols]
        cnt = jnp.minimum(pos + 1, win).astype(jnp.float32)
        d = (s / cnt - p[:, cols]).astype(jnp.bfloat16)
        a = jnp.dot(d, poolw_ref[g], preferred_element_type=jnp.float32) * pscale_ref[:, cols]
        ssq = ssq + jnp.sum(a * a, axis=-1, keepdims=True)
        a_parts.append(a)
    inv_a = lax.rsqrt(ssq / pool_w + NORM_EPS)
    for g in range(N_POOL_GROUPS):
        cols = slice(g * gdim, (g + 1) * gdim)
        mix_ref[:, cols] = (a_parts[g] * inv_a * onp_ref[:, cols]).astype(jnp.bfloat16)

    gz = _gelu(z[:, pool_w:])
    tril = (lax.broadcasted_iota(jnp.int32, (SGU_CHUNK, SGU_CHUNK), 0)
            >= lax.broadcasted_iota(jnp.int32, (SGU_CHUNK, SGU_CHUNK), 1))
    ssq = jnp.zeros((ts, 1), jnp.float32)
    b_parts = []
    for hd in range(SGU_HEADS):
        cols = slice(hd * hdim, (hd + 1) * hdim)
        u = gz[:, hd * hdim:(hd + 1) * hdim]
        v = gz[:, sgu_w + hd * hdim:sgu_w + (hd + 1) * hdim]
        mu = jnp.mean(v, axis=-1, keepdims=True)
        vc = v - mu
        var = jnp.mean(vc * vc, axis=-1, keepdims=True)
        vn = (vc * lax.rsqrt(var + NORM_EPS) * lng_ref[:, cols] + lnb_ref[:, cols]).astype(jnp.bfloat16)
        w = jnp.where(tril, sguw_ref[hd], jnp.zeros((), sguw_ref.dtype))
        mixed = [jnp.dot(w, vn[n * SGU_CHUNK:(n + 1) * SGU_CHUNK], preferred_element_type=jnp.float32)
                 + sgub_ref[hd] for n in range(ts // SGU_CHUNK)]
        b = u * jnp.concatenate(mixed, axis=0)
        ssq = ssq + jnp.sum(b * b, axis=-1, keepdims=True)
        b_parts.append(b)
    inv_b = lax.rsqrt(ssq / sgu_w + NORM_EPS)
    for hd in range(SGU_HEADS):
        cols = slice(hd * hdim, (hd + 1) * hdim)
        mix_ref[:, pool_w + hd * hdim:pool_w + (hd + 1) * hdim] = (
            b_parts[hd] * inv_b * ons_ref[:, cols]).astype(jnp.bfloat16)

    o_ref[0] = x + jnp.dot(mix_ref[...], wout_ref[...], preferred_element_type=jnp.float32)


def _mixer(x, b0, nb, norm_mix, w_in, pool_w, pool_scale, ln_g, ln_b, sgu_w, sgu_b, on_pool, on_sgu, w_out):
    _, S, D = x.shape
    ts = min(MIX_TILE, S)
    pool_width = pool_scale.size
    sgu_width = ln_g.size
    in_width = w_in.shape[1]
    gdim = pool_width // N_POOL_GROUPS
    halo_blocks = ts // HALO
    full = lambda shape: pl.BlockSpec(shape, lambda b, i: (0,) * len(shape))
    return pl.pallas_call(
        _mixer_kernel,
        grid=(nb, S // ts),
        in_specs=[
            pl.BlockSpec((1, ts, D), lambda b, i: (b0 + b, i, 0)),
            pl.BlockSpec((1, HALO, D), lambda b, i: (b0 + b, jnp.maximum(i * halo_blocks - 1, 0), 0)),
            full((1, D)),
            full((D, in_width)),
            full((N_POOL_GROUPS, gdim, gdim)),
            full((1, pool_width)),
            full((1, sgu_width)),
            full((1, sgu_width)),
            full((SGU_HEADS, SGU_CHUNK, SGU_CHUNK)),
            full((SGU_HEADS, SGU_CHUNK, SGU_CHUNK)),
            full((1, pool_width)),
            full((1, sgu_width)),
            full((pool_width + sgu_width, D)),
        ],
        out_specs=pl.BlockSpec((1, ts, D), lambda b, i: (b, i, 0)),
        out_shape=jax.ShapeDtypeStruct((nb, S, D), jnp.float32),
        scratch_shapes=[
            pltpu.VMEM((HALO + ts, pool_width), jnp.float32),
            pltpu.VMEM((ts, pool_width + sgu_width), jnp.bfloat16),
        ],
        compiler_params=pltpu.CompilerParams(
            dimension_semantics=("parallel", "arbitrary"), vmem_limit_bytes=TC_VMEM_LIMIT),
        name="mixer",
    )(x, x, norm_mix.reshape(1, D), w_in.astype(jnp.bfloat16), pool_w.astype(jnp.bfloat16),
      pool_scale.reshape(1, pool_width), ln_g.reshape(1, sgu_width), ln_b.reshape(1, sgu_width),
      sgu_w.astype(jnp.bfloat16),
      jnp.broadcast_to(sgu_b[:, :, None], (SGU_HEADS, SGU_CHUNK, SGU_CHUNK)),
      on_pool.reshape(1, pool_width), on_sgu.reshape(1, sgu_width), w_out.astype(jnp.bfloat16))


def _topk_rows(s, k):
    n = s.shape[0]
    iota = lax.broadcasted_iota(jnp.int32, s.shape, 0)
    vals, idxs = [], []
    for _ in range(k):
        m = jnp.max(s, axis=0, keepdims=True)
        ix = jnp.min(jnp.where(s == m, iota, n), axis=0, keepdims=True)
        vals.append(m)
        idxs.append(ix)
        s = jnp.where(iota == ix, -jnp.inf, s)
    return vals, idxs


def _pair_candidates(v1, i1, v2, i2):
    k = PEER_TOPK
    v2c, i2c = jnp.concatenate(v2, axis=0), jnp.concatenate(i2, axis=0)
    m = v2c.shape[1]
    vals, experts, flats = [], [], []
    a = 0
    while k // (a + 1) > 1:
        rows = -(-(k // (a + 1)) // V7X_SUBLANES) * V7X_SUBLANES
        vals.append(v1[a] + v2c[:rows])
        experts.append(i1[a] * PEER_N_KEYS + i2c[:rows])
        flats.append(a * k + lax.broadcasted_iota(jnp.int32, (rows, m), 0))
        a += 1
    vals.append(jnp.concatenate(v1[a:], axis=0) + v2[0])
    experts.append(jnp.concatenate(i1[a:], axis=0) * PEER_N_KEYS + i2[0])
    flats.append((a + lax.broadcasted_iota(jnp.int32, (k - a, m), 0)) * k)
    return jnp.concatenate(vals, axis=0), jnp.concatenate(experts, axis=0), jnp.concatenate(flats, axis=0)


def _router_kernel(x_ref, nffn_ref, wq_ref, keys_ref, after_ref, h2_ref, idx_ref, slot_ref, gate_ref,
                   q_ref, idxt_ref, gatet_ref):
    del after_ref
    h2 = _rms(x_ref[...], nffn_ref[...]).astype(jnp.bfloat16)
    h2_ref[...] = h2
    q_ref[...] = jnp.dot(h2, wq_ref[...], preferred_element_type=jnp.float32).astype(jnp.bfloat16)
    dq = 2 * PEER_D_HALF
    nt = (((1,), (1,)), ((), ()))

    def head(hd, carry):
        off = pl.multiple_of(hd * dq, dq)
        s1 = lax.dot_general(keys_ref[0], q_ref[:, pl.ds(off, PEER_D_HALF)], nt,
                             preferred_element_type=jnp.float32)
        s2 = lax.dot_general(keys_ref[1], q_ref[:, pl.ds(off + PEER_D_HALF, PEER_D_HALF)], nt,
                             preferred_element_type=jnp.float32)
        v1, i1 = _topk_rows(s1, PEER_TOPK)
        v2, i2 = _topk_rows(s2, PEER_TOPK)
        cand, expert, flat = _pair_candidates(v1, i1, v2, i2)
        cv, ce = [], []
        for _ in range(PEER_TOPK):
            m = jnp.max(cand, axis=0, keepdims=True)
            ix = jnp.min(jnp.where(cand == m, flat, PEER_TOPK * PEER_TOPK), axis=0, keepdims=True)
            hit = flat == ix
            cv.append(m)
            ce.append(jnp.max(jnp.where(hit, expert, -1), axis=0, keepdims=True))
            cand = jnp.where(hit, -jnp.inf, cand)
        cvc = jnp.concatenate(cv, axis=0)
        e = jnp.exp(cvc - cv[0])
        gate = e / jnp.sum(e, axis=0, keepdims=True)
        row = pl.multiple_of(hd * PEER_TOPK, PEER_TOPK)
        idxt_ref[pl.ds(row, PEER_TOPK), :] = jnp.concatenate(ce, axis=0)
        gatet_ref[pl.ds(row, PEER_TOPK), :] = gate
        return carry

    lax.fori_loop(0, PEER_HEADS, head, 0)
    idx = idxt_ref[...].T
    idx_ref[...] = idx
    tok = pl.program_id(0) * idx.shape[0] + lax.broadcasted_iota(jnp.int32, idx.shape, 0)
    slot_ref[...] = _tile_linear_index(tok, idx, PEER_N_KEYS * PEER_N_KEYS)
    gate_ref[...] = gatet_ref[...].T


def _router(x1, norm_ffn, wq, keys, tok0, T, after):
    D = x1.shape[1]
    tr = min(ROUTE_TILE, T)
    qw = wq.shape[1]
    assert tok0 % tr == 0 and T % tr == 0
    first = tok0 // tr
    full = lambda shape: pl.BlockSpec(shape, lambda i: (0,) * len(shape))
    return pl.pallas_call(
        _router_kernel,
        grid=(T // tr,),
        in_specs=[
            pl.BlockSpec((tr, D), lambda i: (first + i, 0)),
            full((1, D)),
            full((D, qw)),
            full((2, PEER_N_KEYS, PEER_D_HALF)),
            pl.BlockSpec(memory_space=pl.ANY),
        ],
        out_specs=[
            pl.BlockSpec((tr, D), lambda i: (i, 0)),
            pl.BlockSpec((tr, EXPERTS_PER_TOKEN), lambda i: (i, 0)),
            pl.BlockSpec((tr, EXPERTS_PER_TOKEN), lambda i: (i, 0)),
            pl.BlockSpec((tr, EXPERTS_PER_TOKEN), lambda i: (i, 0)),
        ],
        out_shape=[
            jax.ShapeDtypeStruct((T, D), jnp.bfloat16),
            jax.ShapeDtypeStruct((T, EXPERTS_PER_TOKEN), jnp.int32),
            jax.ShapeDtypeStruct((T, EXPERTS_PER_TOKEN), jnp.int32),
            jax.ShapeDtypeStruct((T, EXPERTS_PER_TOKEN), jnp.float32),
        ],
        scratch_shapes=[
            pltpu.VMEM((tr, qw), jnp.bfloat16),
            pltpu.VMEM((EXPERTS_PER_TOKEN, tr), jnp.int32),
            pltpu.VMEM((EXPERTS_PER_TOKEN, tr), jnp.float32),
        ],
        compiler_params=pltpu.CompilerParams(
            dimension_semantics=("parallel",), vmem_limit_bytes=TC_VMEM_LIMIT),
        name="router",
    )(x1, norm_ffn.reshape(1, D), wq, keys, after)


def _tile_linear_index(row, col, ncols):
    return ((row >> 3) * (ncols * V7X_SUBLANES) + (col >> 7) * (V7X_SUBLANES * V7X_LANES)
            + (row & (V7X_SUBLANES - 1)) * V7X_LANES + (col & (V7X_LANES - 1)))


def _scores_kernel(h_ref, ut_ref, o_ref):
    acc = jnp.dot(h_ref[...], ut_ref[...], preferred_element_type=jnp.float32)
    tm, tn = acc.shape
    for n in range(tn // V7X_LANES):
        o_ref[:, n * V7X_SUBLANES:(n + 1) * V7X_SUBLANES, :] = acc[:, n * V7X_LANES:(n + 1) * V7X_LANES].reshape(
            tm // V7X_SUBLANES, V7X_SUBLANES, V7X_LANES)


def _dense_scores(h2, u_t):
    T, D = h2.shape
    E = u_t.shape[1]
    tm, tn = min(SCORE_TOK_TILE, T), min(SCORE_EXP_TILE, E)
    assert T % tm == 0 and E % tn == 0 and tm % V7X_SUBLANES == 0 and tn % V7X_LANES == 0
    out = pl.pallas_call(
        _scores_kernel,
        grid=(T // tm, E // tn),
        in_specs=[pl.BlockSpec((tm, D), lambda i, j: (i, 0)), pl.BlockSpec((D, tn), lambda i, j: (0, j))],
        out_specs=pl.BlockSpec((tm // V7X_SUBLANES, tn // V7X_LANES * V7X_SUBLANES, V7X_LANES),
                               lambda i, j: (i, j, 0)),
        out_shape=jax.ShapeDtypeStruct((T // V7X_SUBLANES, E // V7X_LANES * V7X_SUBLANES, V7X_LANES),
                                       jnp.float32),
        compiler_params=pltpu.CompilerParams(
            dimension_semantics=("parallel", "arbitrary"), vmem_limit_bytes=TC_VMEM_LIMIT),
        name="dense_scores",
    )(h2, u_t)
    return out.reshape(-1)


def _act_kernel(pre_ref, gate_ref, w_ref):
    w = (gate_ref[...] * _gelu(pre_ref[...])).astype(jnp.bfloat16).astype(jnp.float32)
    hi = pltpu.bitcast(w, jnp.uint32) & jnp.uint32(HI_HALF)
    w_ref[...] = pltpu.bitcast(hi | (hi >> 16), jnp.int32)


def _expert_weights(pre, gate):
    T, K = pre.shape
    te = min(EW_TILE, T)
    spec = pl.BlockSpec((te, K), lambda i: (i, 0))
    return pl.pallas_call(
        _act_kernel, grid=(T // te,), in_specs=[spec, spec], out_specs=spec,
        out_shape=jax.ShapeDtypeStruct((T, K), jnp.int32),
        compiler_params=pltpu.CompilerParams(dimension_semantics=("parallel",)),
        name="expert_weights",
    )(pre, gate)


def _final_kernel(x_ref, y_ref, g_ref, *rest):
    o_ref = rest[-1]
    o_ref[...] = _rms(x_ref[...] + y_ref[...], g_ref[...])


def _final(x1, x_tok0, peer, norm_final, out_prev, out_tok0, total):
    Tc, D = peer.shape
    te = min(EW_TILE, Tc)
    assert x_tok0 % te == 0 and out_tok0 % te == 0 and Tc % te == 0
    nblk = Tc // te
    spec = pl.BlockSpec((te, D), lambda i: (out_tok0 // te + i, 0))
    in_specs = [pl.BlockSpec((te, D), lambda i: (x_tok0 // te + i, 0)),
                pl.BlockSpec((te, D), lambda i: (i, 0)), pl.BlockSpec((1, D), lambda i: (0, 0))]
    args = [x1, peer, norm_final.reshape(1, D)]
    aliases = {}
    if out_prev is not None:
        in_specs.append(pl.BlockSpec(memory_space=pl.ANY))
        args.append(out_prev)
        aliases = {3: 0}
    return pl.pallas_call(
        _final_kernel, grid=(nblk,), in_specs=in_specs, out_specs=spec,
        out_shape=jax.ShapeDtypeStruct((total, D), jnp.float32),
        input_output_aliases=aliases,
        compiler_params=pltpu.CompilerParams(dimension_semantics=("parallel",)),
        name="final_norm",
    )(*args)


def _tree_sum(vals):
    while len(vals) > 1:
        nxt = [vals[i] + vals[i + 1] for i in range(0, len(vals) - 1, 2)]
        if len(vals) % 2:
            nxt.append(vals[-1])
        vals = nxt
    return vals[0]


def _sc_block_pipeline(nblk, items_per_token, loads, store, gather, compute):
    assert items_per_token % 2 == 0 and nblk >= 1

    for c in loads(0, 0):
        c.start()
    for c in loads(0, 0):
        c.wait()
    if nblk > 1:
        for c in loads(1, 1):
            c.start()
    gather(0, 0, 0, 0).start()

    @pl.loop(0, nblk)
    def _(b):
        slot = b % 2

        @pl.when(b >= 2)
        def _():
            store(b - 2, slot).wait()

        @pl.loop(0, SC_TOKENS)
        def _(t):
            for q in range(items_per_token):
                buf = q % 2
                if q + 1 < items_per_token:
                    gather(slot, t, q + 1, 1 - buf).start()
                else:
                    @pl.when(t + 1 < SC_TOKENS)
                    def _():
                        gather(slot, t + 1, 0, 1 - buf).start()

                    @pl.when(jnp.logical_and(t + 1 == SC_TOKENS, b + 1 < nblk))
                    def _():
                        for c in loads(b + 1, 1 - slot):
                            c.wait()
                        gather(1 - slot, 0, 0, 1 - buf).start()

                gather(slot, t, q, buf).wait()
                compute(slot, t, q, buf)

        store(b, slot).start()

        @pl.when(b + 2 < nblk)
        def _():
            for c in loads(b + 2, slot):
                c.start()

    if nblk >= 2:
        store(nblk - 2, nblk % 2).wait()
    store(nblk - 1, (nblk - 1) % 2).wait()


def _sc_mesh():
    return plsc.VectorSubcoreMesh(core_axis_name="c", subcore_axis_name="s")


def _sc_worker_id():
    return lax.axis_index("s") * V7X_SC_CORES + lax.axis_index("c")


def _sc_bf16(words):
    return plsc.bitcast(words, jnp.bfloat16)


def _sc_halves_f32(pairs):
    words = plsc.bitcast(pairs, jnp.uint32)
    return (plsc.bitcast(words << 16, jnp.float32),
            plsc.bitcast(words & jnp.uint32(HI_HALF), jnp.float32))


def _pick_scores(flat, slots, after):
    T, K = slots.shape
    nb = SC_PICK_TOKENS
    tok_per_w = T // V7X_SC_WORKERS
    nblk = tok_per_w // nb
    assert T % (V7X_SC_WORKERS * nb) == 0

    @functools.partial(
        pl.kernel, mesh=_sc_mesh(),
        out_type=jax.ShapeDtypeStruct((T, K), jnp.float32),
        compiler_params=pltpu.CompilerParams(needs_layout_passes=False),
        scratch_types=[
            pltpu.VMEM((2, nb, K), jnp.int32),
            pltpu.VMEM((2, nb, K), jnp.float32),
            pltpu.SemaphoreType.DMA((2,)),
            pltpu.SemaphoreType.DMA((2,)),
            pltpu.SemaphoreType.DMA((2,)),
        ],
        name="pick_scores")
    def k(flat_hbm, slots_hbm, after_hbm, out_hbm, idx_v, val_v, idx_sems, out_sems, row_sems):
        del after_hbm
        base = _sc_worker_id() * tok_per_w

        def load(b, slot):
            return pltpu.make_async_copy(slots_hbm.at[pl.ds(base + b * nb, nb)], idx_v.at[slot], idx_sems.at[slot])

        def store(b, slot):
            return pltpu.make_async_copy(val_v.at[slot], out_hbm.at[pl.ds(base + b * nb, nb)], out_sems.at[slot])

        def gathers(slot):
            return [pltpu.make_async_copy(flat_hbm.at[idx_v.at[slot, t]], val_v.at[slot, t], row_sems.at[slot])
                    for t in range(nb)]

        load(0, 0).start()

        @pl.loop(0, nblk)
        def _(b):
            slot = b % 2
            load(b, slot).wait()

            @pl.when(b + 1 < nblk)
            def _():
                load(b + 1, 1 - slot).start()

            @pl.when(b >= 2)
            def _():
                store(b - 2, slot).wait()

            for c in gathers(slot):
                c.start()
            for c in gathers(slot):
                c.wait()
            store(b, slot).start()

        if nblk >= 2:
            store(nblk - 2, nblk % 2).wait()
        store(nblk - 1, (nblk - 1) % 2).wait()

    return k(flat, slots, after)


def _expert_mix(w, idx, table):
    T, K = w.shape
    DW = table.shape[1]
    D = 2 * DW
    L, G = V7X_SC_LANES, SC_GATHER
    nj = SC_MIX_CHUNK // L
    tok_per_w = T // V7X_SC_WORKERS
    assert T % (V7X_SC_WORKERS * SC_TOKENS) == 0 and K % (2 * G) == 0
    assert DW % SC_MIX_CHUNK == 0 and G % SC_MIX_GROUP == 0

    @functools.partial(
        pl.kernel, mesh=_sc_mesh(),
        out_type=jax.ShapeDtypeStruct((T, D), jnp.float32),
        compiler_params=pltpu.CompilerParams(needs_layout_passes=False),
        scratch_types=[
            pltpu.VMEM((2, SC_TOKENS, K), jnp.int32),
            pltpu.VMEM((2, SC_TOKENS, K), jnp.int32),
            pltpu.VMEM((2, G, DW), jnp.uint32),
            pltpu.VMEM((2, SC_TOKENS, D), jnp.float32),
            pltpu.SemaphoreType.DMA((2,)),
            pltpu.SemaphoreType.DMA((2,)),
            pltpu.SemaphoreType.DMA((2,)),
            pltpu.SemaphoreType.DMA((2,)),
        ],
        name="expert_mix")
    def k(w_hbm, idx_hbm, tab_hbm, out_hbm, idx_v, w_v, rows_v, out_v,
          idx_sems, w_sems, out_sems, row_sems):
        base = _sc_worker_id() * tok_per_w

        def loads(b, slot):
            toks = pl.ds(base + b * SC_TOKENS, SC_TOKENS)
            return [pltpu.make_async_copy(idx_hbm.at[toks], idx_v.at[slot], idx_sems.at[slot]),
                    pltpu.make_async_copy(w_hbm.at[toks], w_v.at[slot], w_sems.at[slot])]

        def store(b, slot):
            toks = pl.ds(base + b * SC_TOKENS, SC_TOKENS)
            return pltpu.make_async_copy(out_v.at[slot], out_hbm.at[toks], out_sems.at[slot])

        def gather(slot, t, q, buf):
            return pltpu.make_async_copy(
                tab_hbm.at[idx_v.at[slot, t, pl.ds(q * G, G)]], rows_v.at[buf], row_sems.at[buf])

        def compute(slot, t, q, buf):
            ssplat = jnp.full((L,), slot, jnp.int32)
            tsplat = jnp.full((L,), t, jnp.int32)
            for c in range(DW // SC_MIX_CHUNK):
                def body(kg, acc):
                    kk = kg * SC_MIX_GROUP
                    wks = [_sc_bf16(plsc.load_gather(
                        w_v, [ssplat, tsplat, jnp.full((L,), q * G + i, jnp.int32) + kk]))
                        for i in range(SC_MIX_GROUP)]
                    out = []
                    for j in range(nj):
                        prods = [wks[i] * _sc_bf16(rows_v[buf, kk + i, pl.ds(c * SC_MIX_CHUNK + j * L, L)])
                                 for i in range(SC_MIX_GROUP)]
                        lo, hi = _sc_halves_f32(_tree_sum(prods))
                        out += [acc[2 * j] + lo, acc[2 * j + 1] + hi]
                    return tuple(out)

                zero = jnp.zeros((L,), jnp.float32)
                acc = plsc.parallel_loop(0, G // SC_MIX_GROUP, carry=(zero,) * (2 * nj))(body)
                for j in range(nj):
                    for half in range(2):
                        dst = out_v.at[slot, t, pl.ds(half * DW + c * SC_MIX_CHUNK + j * L, L)]
                        if q == 0:
                            dst[...] = acc[2 * j + half]
                        else:
                            plsc.addupdate(dst, acc[2 * j + half])

        _sc_block_pipeline(tok_per_w // SC_TOKENS, K // G, loads, store, gather, compute)

    return k(w, idx, table)


def _chunk_sizes(total):
    ramp, size = [EDGE_CHUNK], EDGE_CHUNK
    while size < MAX_CHUNK:
        ramp.append(size)
        size *= 2
    middle = total - 2 * sum(ramp)
    assert middle >= 0 and middle % MAX_CHUNK == 0
    return ramp + [MAX_CHUNK] * (middle // MAX_CHUNK) + ramp[::-1]


def kernel(x, norm_mix, w_in, pool_w, pool_scale, sgu_ln_g, sgu_ln_b, sgu_w, sgu_b, out_norm_pool,
           out_norm_sgu, w_out, norm_ffn, peer_wq, peer_keys, peer_u, peer_v, norm_final):
    B, S, D = x.shape
    assert norm_mix.shape[0] == 1, "single-layer block"
    T = B * S
    mix_args = (norm_mix[0], w_in[0], pool_w[0], pool_scale[0], sgu_ln_g[0], sgu_ln_b[0],
                sgu_w[0], sgu_b[0], out_norm_pool[0], out_norm_sgu[0], w_out[0])
    x1_parts = [(0, _mixer(x, 0, 1, *mix_args).reshape(S, D))]
    if B > 1:
        x1_parts.append((S, _mixer(x, 1, B - 1, *mix_args).reshape((B - 1) * S, D)))
    wq = peer_wq[0].astype(jnp.bfloat16)
    keys = peer_keys[0].astype(jnp.bfloat16)
    u_t = peer_u[0].astype(jnp.bfloat16).T
    v_tab = _pack_table(peer_v[0])
    out = None
    tok0 = 0
    ws = [norm_ffn, v_tab]
    peers = [norm_ffn, norm_ffn]
    for tc in _chunk_sizes(T):
        part0, x1 = [p for p in x1_parts if p[0] <= tok0][-1]
        assert tok0 + tc <= part0 + x1.shape[0], "a token chunk must lie inside one mixer call"
        h2, idx, slots, gate = _router(x1, norm_ffn[0], wq, keys, tok0 - part0, tc, ws[-2])
        pre = _pick_scores(_dense_scores(h2, u_t), slots, peers[-2])
        w = _expert_weights(pre, gate)
        ws.append(w)
        peer = _expert_mix(w, idx, v_tab)
        peers.append(peer)
        out = _final(x1, tok0 - part0, peer, norm_final, out, tok0, T)
        tok0 += tc
    return out.reshape(B, S, D)
```

```python
import functools
import math

import jax
import jax.numpy as jnp
from jax import lax
from jax.experimental import pallas as pl
from jax.experimental.pallas import tpu as pltpu
from jax.experimental.pallas import tpu_sc as plsc

POOL_WINDOWS = (2, 4, 8, 16)
N_POOL_GROUPS = len(POOL_WINDOWS)
SGU_HEADS = 4
SGU_CHUNK = 128
PEER_HEADS = 8
PEER_N_KEYS = 128
PEER_D_HALF = 128
PEER_TOPK = 16
NORM_EPS = 1e-6
EXPERTS_PER_TOKEN = PEER_HEADS * PEER_TOPK

V7X_LANES = 128
V7X_SUBLANES = 8
V7X_SC_CORES = 2
V7X_SC_SUBCORES = 16
V7X_SC_LANES = 16
V7X_SC_WORKERS = V7X_SC_CORES * V7X_SC_SUBCORES

HALO = max(POOL_WINDOWS)
MIX_TILE = 512
ROUTE_TILE = 512
EW_TILE = 512
SCORE_TOK_TILE = 1024
SCORE_EXP_TILE = 1024
SC_PICK_TOKENS = 16
SC_GATHER = 64
SC_TOKENS = 8
SC_MIX_CHUNK = 128
SC_MIX_GROUP = 4
SC_UNROLL = 2
HI_HALF = 0xFFFF0000
TC_VMEM_LIMIT = 48 * 1024 * 1024
EDGE_CHUNK = 512
MAX_CHUNK = 2048


def _rms(x, g):
    inv = lax.rsqrt(jnp.mean(x * x, axis=-1, keepdims=True) + NORM_EPS)
    return x * inv * g


def _pack_halves(bits):
    half = bits.shape[1] // 2
    return (bits[:, :half] >> 16) | (bits[:, half:] & jnp.uint32(HI_HALF))


def _pack_table(a):
    return _pack_halves(lax.bitcast_convert_type(a.astype(jnp.bfloat16).astype(jnp.float32), jnp.uint32))


def _gelu(x):
    return 0.5 * x * (1.0 + lax.erf(x * math.sqrt(0.5)))


def _mixer_kernel(x_ref, xh_ref, nmix_ref, win_ref, poolw_ref, pscale_ref, lng_ref, lnb_ref,
                  sguw_ref, sgub_ref, onp_ref, ons_ref, wout_ref, o_ref, pext_ref, mix_ref):
    i = pl.program_id(1)
    ts = x_ref.shape[1]
    pool_w = pscale_ref.shape[1]
    gdim = pool_w // N_POOL_GROUPS
    sgu_w = lng_ref.shape[1]
    hdim = sgu_w // SGU_HEADS

    x = x_ref[0]
    h = _rms(x, nmix_ref[...]).astype(jnp.bfloat16)
    z = jnp.dot(h, win_ref[...], preferred_element_type=jnp.float32)
    p = z[:, :pool_w]

    hh = _rms(xh_ref[0], nmix_ref[...]).astype(jnp.bfloat16)
    ph = jnp.dot(hh, win_ref[:, :pool_w], preferred_element_type=jnp.float32)
    ph = jnp.where(i > 0, ph, 0.0)
    pext_ref[0:HALO, :] = ph
    pext_ref[HALO:HALO + ts, :] = p

    pos = i * ts + lax.broadcasted_iota(jnp.int32, (ts, 1), 0)
    ssq = jnp.zeros((ts, 1), jnp.float32)
    a_parts = []
    for g, win in enumerate(POOL_WINDOWS):
        cols = slice(g * gdim, (g + 1) * gdim)
        s = pext_ref[HALO:HALO + ts, cols]
        for j in range(1, win):
            s = s + pext_ref[HALO - j:HALO - j + ts, cols]
        cnt = jnp.minimum(pos + 1, win).astype(jnp.float32)
        d = (s / cnt - p[:, cols]).astype(jnp.bfloat16)
        a = jnp.dot(d, poolw_ref[g], preferred_element_type=jnp.float32) * pscale_ref[:, cols]
        ssq = ssq + jnp.sum(a * a, axis=-1, keepdims=True)
        a_parts.append(a)
    inv_a = lax.rsqrt(ssq / pool_w + NORM_EPS)
    for g in range(N_POOL_GROUPS):
        cols = slice(g * gdim, (g + 1) * gdim)
        mix_ref[:, cols] = (a_parts[g] * inv_a * onp_ref[:, cols]).astype(jnp.bfloat16)

    gz = _gelu(z[:, pool_w:])
    tril = (lax.broadcasted_iota(jnp.int32, (SGU_CHUNK, SGU_CHUNK), 0)
            >= lax.broadcasted_iota(jnp.int32, (SGU_CHUNK, SGU_CHUNK), 1))
    ssq = jnp.zeros((ts, 1), jnp.float32)
    b_parts = []
    for hd in range(SGU_HEADS):
        cols = slice(hd * hdim, (hd + 1) * hdim)
        u = gz[:, hd * hdim:(hd + 1) * hdim]
        v = gz[:, sgu_w + hd * hdim:sgu_w + (hd + 1) * hdim]
        mu = jnp.mean(v, axis=-1, keepdims=True)
        vc = v - mu
        var = jnp.mean(vc * vc, axis=-1, keepdims=True)
        vn = (vc * lax.rsqrt(var + NORM_EPS) * lng_ref[:, cols] + lnb_ref[:, cols]).astype(jnp.bfloat16)
        w = jnp.where(tril, sguw_ref[hd], jnp.zeros((), sguw_ref.dtype))
        mixed = [jnp.dot(w, vn[n * SGU_CHUNK:(n + 1) * SGU_CHUNK], preferred_element_type=jnp.float32)
                 + sgub_ref[hd] for n in range(ts // SGU_CHUNK)]
        b = u * jnp.concatenate(mixed, axis=0)
        ssq = ssq + jnp.sum(b * b, axis=-1, keepdims=True)
        b_parts.append(b)
    inv_b = lax.rsqrt(ssq / sgu_w + NORM_EPS)
    for hd in range(SGU_HEADS):
        cols = slice(hd * hdim, (hd + 1) * hdim)
        mix_ref[:, pool_w + hd * hdim:pool_w + (hd + 1) * hdim] = (
            b_parts[hd] * inv_b * ons_ref[:, cols]).astype(jnp.bfloat16)

    o_ref[0] = x + jnp.dot(mix_ref[...], wout_ref[...], preferred_element_type=jnp.float32)


def _mixer(x, b0, nb, norm_mix, w_in, pool_w, pool_scale, ln_g, ln_b, sgu_w, sgu_b, on_pool, on_sgu, w_out):
    _, S, D = x.shape
    ts = min(MIX_TILE, S)
    pool_width = pool_scale.size
    sgu_width = ln_g.size
    in_width = w_in.shape[1]
    gdim = pool_width // N_POOL_GROUPS
    halo_blocks = ts // HALO
    full = lambda shape: pl.BlockSpec(shape, lambda b, i: (0,) * len(shape))
    return pl.pallas_call(
        _mixer_kernel,
        grid=(nb, S // ts),
        in_specs=[
            pl.BlockSpec((1, ts, D), lambda b, i: (b0 + b, i, 0)),
            pl.BlockSpec((1, HALO, D), lambda b, i: (b0 + b, jnp.maximum(i * halo_blocks - 1, 0), 0)),
            full((1, D)),
            full((D, in_width)),
            full((N_POOL_GROUPS, gdim, gdim)),
            full((1, pool_width)),
            full((1, sgu_width)),
            full((1, sgu_width)),
            full((SGU_HEADS, SGU_CHUNK, SGU_CHUNK)),
            full((SGU_HEADS, SGU_CHUNK, SGU_CHUNK)),
            full((1, pool_width)),
            full((1, sgu_width)),
            full((pool_width + sgu_width, D)),
        ],
        out_specs=pl.BlockSpec((1, ts, D), lambda b, i: (b, i, 0)),
        out_shape=jax.ShapeDtypeStruct((nb, S, D), jnp.float32),
        scratch_shapes=[
            pltpu.VMEM((HALO + ts, pool_width), jnp.float32),
            pltpu.VMEM((ts, pool_width + sgu_width), jnp.bfloat16),
        ],
        compiler_params=pltpu.CompilerParams(
            dimension_semantics=("parallel", "arbitrary"), vmem_limit_bytes=TC_VMEM_LIMIT),
        name="mixer",
    )(x, x, norm_mix.reshape(1, D), w_in.astype(jnp.bfloat16), pool_w.astype(jnp.bfloat16),
      pool_scale.reshape(1, pool_width), ln_g.reshape(1, sgu_width), ln_b.reshape(1, sgu_width),
      sgu_w.astype(jnp.bfloat16),
      jnp.broadcast_to(sgu_b[:, :, None], (SGU_HEADS, SGU_CHUNK, SGU_CHUNK)),
      on_pool.reshape(1, pool_width), on_sgu.reshape(1, sgu_width), w_out.astype(jnp.bfloat16))


def _topk_rows(s, k):
    n = s.shape[0]
    iota = lax.broadcasted_iota(jnp.int32, s.shape, 0)
    vals, idxs = [], []
    for _ in range(k):
        m = jnp.max(s, axis=0, keepdims=True)
        ix = jnp.min(jnp.where(s == m, iota, n), axis=0, keepdims=True)
        vals.append(m)
        idxs.append(ix)
        s = jnp.where(iota == ix, -jnp.inf, s)
    return vals, idxs


def _pair_candidates(v1, i1, v2, i2):
    k = PEER_TOPK
    v2c, i2c = jnp.concatenate(v2, axis=0), jnp.concatenate(i2, axis=0)
    m = v2c.shape[1]
    vals, experts, flats = [], [], []
    a = 0
    while k // (a + 1) > 1:
        rows = -(-(k // (a + 1)) // V7X_SUBLANES) * V7X_SUBLANES
        vals.append(v1[a] + v2c[:rows])
        experts.append(i1[a] * PEER_N_KEYS + i2c[:rows])
        flats.append(a * k + lax.broadcasted_iota(jnp.int32, (rows, m), 0))
        a += 1
    vals.append(jnp.concatenate(v1[a:], axis=0) + v2[0])
    experts.append(jnp.concatenate(i1[a:], axis=0) * PEER_N_KEYS + i2[0])
    flats.append((a + lax.broadcasted_iota(jnp.int32, (k - a, m), 0)) * k)
    return jnp.concatenate(vals, axis=0), jnp.concatenate(experts, axis=0), jnp.concatenate(flats, axis=0)


def _router_kernel(x_ref, nffn_ref, wq_ref, keys_ref, after_ref, h2_ref, idx_ref, slot_ref, gate_ref,
                   q_ref, idxt_ref, gatet_ref):
    del after_ref
    h2 = _rms(x_ref[...], nffn_ref[...]).astype(jnp.bfloat16)
    h2_ref[...] = h2
    q_ref[...] = jnp.dot(h2, wq_ref[...], preferred_element_type=jnp.float32).astype(jnp.bfloat16)
    dq = 2 * PEER_D_HALF
    nt = (((1,), (1,)), ((), ()))

    def head(hd, carry):
        off = pl.multiple_of(hd * dq, dq)
        s1 = lax.dot_general(keys_ref[0], q_ref[:, pl.ds(off, PEER_D_HALF)], nt,
                             preferred_element_type=jnp.float32)
        s2 = lax.dot_general(keys_ref[1], q_ref[:, pl.ds(off + PEER_D_HALF, PEER_D_HALF)], nt,
                             preferred_element_type=jnp.float32)
        v1, i1 = _topk_rows(s1, PEER_TOPK)
        v2, i2 = _topk_rows(s2, PEER_TOPK)
        cand, expert, flat = _pair_candidates(v1, i1, v2, i2)
        cv, ce = [], []
        for _ in range(PEER_TOPK):
            m = jnp.max(cand, axis=0, keepdims=True)
            ix = jnp.min(jnp.where(cand == m, flat, PEER_TOPK * PEER_TOPK), axis=0, keepdims=True)
            hit = flat == ix
            cv.append(m)
            ce.append(jnp.max(jnp.where(hit, expert, -1), axis=0, keepdims=True))
            cand = jnp.where(hit, -jnp.inf, cand)
        cvc = jnp.concatenate(cv, axis=0)
        e = jnp.exp(cvc - cv[0])
        gate = e / jnp.sum(e, axis=0, keepdims=True)
        row = pl.multiple_of(hd * PEER_TOPK, PEER_TOPK)
        idxt_ref[pl.ds(row, PEER_TOPK), :] = jnp.concatenate(ce, axis=0)
        gatet_ref[pl.ds(row, PEER_TOPK), :] = gate
        return carry

    lax.fori_loop(0, PEER_HEADS, head, 0)
    idx = idxt_ref[...].T
    idx_ref[...] = idx
    tok = pl.program_id(0) * idx.shape[0] + lax.broadcasted_iota(jnp.int32, idx.shape, 0)
    slot_ref[...] = _tile_linear_index(tok, idx, PEER_N_KEYS * PEER_N_KEYS)
    gate_ref[...] = gatet_ref[...].T


def _router(x1, norm_ffn, wq, keys, tok0, T, after):
    D = x1.shape[1]
    tr = min(ROUTE_TILE, T)
    qw = wq.shape[1]
    assert tok0 % tr == 0 and T % tr == 0
    first = tok0 // tr
    full = lambda shape: pl.BlockSpec(shape, lambda i: (0,) * len(shape))
    return pl.pallas_call(
        _router_kernel,
        grid=(T // tr,),
        in_specs=[
            pl.BlockSpec((tr, D), lambda i: (first + i, 0)),
            full((1, D)),
            full((D, qw)),
            full((2, PEER_N_KEYS, PEER_D_HALF)),
            pl.BlockSpec(memory_space=pl.ANY),
        ],
        out_specs=[
            pl.BlockSpec((tr, D), lambda i: (i, 0)),
            pl.BlockSpec((tr, EXPERTS_PER_TOKEN), lambda i: (i, 0)),
            pl.BlockSpec((tr, EXPERTS_PER_TOKEN), lambda i: (i, 0)),
            pl.BlockSpec((tr, EXPERTS_PER_TOKEN), lambda i: (i, 0)),
        ],
        out_shape=[
            jax.ShapeDtypeStruct((T, D), jnp.bfloat16),
            jax.ShapeDtypeStruct((T, EXPERTS_PER_TOKEN), jnp.int32),
            jax.ShapeDtypeStruct((T, EXPERTS_PER_TOKEN), jnp.int32),
            jax.ShapeDtypeStruct((T, EXPERTS_PER_TOKEN), jnp.float32),
        ],
        scratch_shapes=[
            pltpu.VMEM((tr, qw), jnp.bfloat16),
            pltpu.VMEM((EXPERTS_PER_TOKEN, tr), jnp.int32),
            pltpu.VMEM((EXPERTS_PER_TOKEN, tr), jnp.float32),
        ],
        compiler_params=pltpu.CompilerParams(
            dimension_semantics=("parallel",), vmem_limit_bytes=TC_VMEM_LIMIT),
        name="router",
    )(x1, norm_ffn.reshape(1, D), wq, keys, after)


def _tile_linear_index(row, col, ncols):
    return ((row >> 3) * (ncols * V7X_SUBLANES) + (col >> 7) * (V7X_SUBLANES * V7X_LANES)
            + (row & (V7X_SUBLANES - 1)) * V7X_LANES + (col & (V7X_LANES - 1)))


def _scores_kernel(h_ref, ut_ref, o_ref):
    acc = jnp.dot(h_ref[...], ut_ref[...], preferred_element_type=jnp.float32)
    tm, tn = acc.shape
    for n in range(tn // V7X_LANES):
        o_ref[:, n * V7X_SUBLANES:(n + 1) * V7X_SUBLANES, :] = acc[:, n * V7X_LANES:(n + 1) * V7X_LANES].reshape(
            tm // V7X_SUBLANES, V7X_SUBLANES, V7X_LANES)


def _dense_scores(h2, u_t):
    T, D = h2.shape
    E = u_t.shape[1]
    tm, tn = min(SCORE_TOK_TILE, T), min(SCORE_EXP_TILE, E)
    assert T % tm == 0 and E % tn == 0 and tm % V7X_SUBLANES == 0 and tn % V7X_LANES == 0
    out = pl.pallas_call(
        _scores_kernel,
        grid=(T // tm, E // tn),
        in_specs=[pl.BlockSpec((tm, D), lambda i, j: (i, 0)), pl.BlockSpec((D, tn), lambda i, j: (0, j))],
        out_specs=pl.BlockSpec((tm // V7X_SUBLANES, tn // V7X_LANES * V7X_SUBLANES, V7X_LANES),
                               lambda i, j: (i, j, 0)),
        out_shape=jax.ShapeDtypeStruct((T // V7X_SUBLANES, E // V7X_LANES * V7X_SUBLANES, V7X_LANES),
                                       jnp.float32),
        compiler_params=pltpu.CompilerParams(
            dimension_semantics=("parallel", "arbitrary"), vmem_limit_bytes=TC_VMEM_LIMIT),
        name="dense_scores",
    )(h2, u_t)
    return out.reshape(-1)


def _act_kernel(pre_ref, gate_ref, w_ref):
    w = (gate_ref[...] * _gelu(pre_ref[...])).astype(jnp.bfloat16).astype(jnp.float32)
    hi = pltpu.bitcast(w, jnp.uint32) & jnp.uint32(HI_HALF)
    w_ref[...] = pltpu.bitcast(hi | (hi >> 16), jnp.int32)


def _expert_weights(pre, gate):
    T, K = pre.shape
    te = min(EW_TILE, T)
    spec = pl.BlockSpec((te, K), lambda i: (i, 0))
    return pl.pallas_call(
        _act_kernel, grid=(T // te,), in_specs=[spec, spec], out_specs=spec,
        out_shape=jax.ShapeDtypeStruct((T, K), jnp.int32),
        compiler_params=pltpu.CompilerParams(dimension_semantics=("parallel",)),
        name="expert_weights",
    )(pre, gate)


def _final_kernel(x_ref, y_ref, g_ref, *rest):
    o_ref = rest[-1]
    o_ref[...] = _rms(x_ref[...] + y_ref[...], g_ref[...])


def _final(x1, x_tok0, peer, norm_final, out_prev, out_tok0, total):
    Tc, D = peer.shape
    te = min(EW_TILE, Tc)
    assert x_tok0 % te == 0 and out_tok0 % te == 0 and Tc % te == 0
    nblk = Tc // te
    spec = pl.BlockSpec((te, D), lambda i: (out_tok0 // te + i, 0))
    in_specs = [pl.BlockSpec((te, D), lambda i: (x_tok0 // te + i, 0)),
                pl.BlockSpec((te, D), lambda i: (i, 0)), pl.BlockSpec((1, D), lambda i: (0, 0))]
    args = [x1, peer, norm_final.reshape(1, D)]
    aliases = {}
    if out_prev is not None:
        in_specs.append(pl.BlockSpec(memory_space=pl.ANY))
        args.append(out_prev)
        aliases = {3: 0}
    return pl.pallas_call(
        _final_kernel, grid=(nblk,), in_specs=in_specs, out_specs=spec,
        out_shape=jax.ShapeDtypeStruct((total, D), jnp.float32),
        input_output_aliases=aliases,
        compiler_params=pltpu.CompilerParams(dimension_semantics=("parallel",)),
        name="final_norm",
    )(*args)


def _tree_sum(vals):
    while len(vals) > 1:
        nxt = [vals[i] + vals[i + 1] for i in range(0, len(vals) - 1, 2)]
        if len(vals) % 2:
            nxt.append(vals[-1])
        vals = nxt
    return vals[0]


def _sc_block_pipeline(nblk, items_per_token, loads, store, gather, compute):
    assert items_per_token % 2 == 0 and nblk >= 1

    for c in loads(0, 0):
        c.start()
    for c in loads(0, 0):
        c.wait()
    if nblk > 1:
        for c in loads(1, 1):
            c.start()
    gather(0, 0, 0, 0).start()

    @pl.loop(0, nblk)
    def _(b):
        slot = b % 2

        @pl.when(b >= 2)
        def _():
            store(b - 2, slot).wait()

        @pl.loop(0, SC_TOKENS)
        def _(t):
            for q in range(items_per_token):
                buf = q % 2
                if q + 1 < items_per_token:
                    gather(slot, t, q + 1, 1 - buf).start()
                else:
                    @pl.when(t + 1 < SC_TOKENS)
                    def _():
                        gather(slot, t + 1, 0, 1 - buf).start()

                    @pl.when(jnp.logical_and(t + 1 == SC_TOKENS, b + 1 < nblk))
                    def _():
                        for c in loads(b + 1, 1 - slot):
                            c.wait()
                        gather(1 - slot, 0, 0, 1 - buf).start()

                gather(slot, t, q, buf).wait()
                compute(slot, t, q, buf)

        store(b, slot).start()

        @pl.when(b + 2 < nblk)
        def _():
            for c in loads(b + 2, slot):
                c.start()

    if nblk >= 2:
        store(nblk - 2, nblk % 2).wait()
    store(nblk - 1, (nblk - 1) % 2).wait()


def _sc_mesh():
    return plsc.VectorSubcoreMesh(core_axis_name="c", subcore_axis_name="s")


def _sc_worker_id():
    return lax.axis_index("s") * V7X_SC_CORES + lax.axis_index("c")


def _sc_bf16(words):
    return plsc.bitcast(words, jnp.bfloat16)


def _sc_halves_f32(pairs):
    words = plsc.bitcast(pairs, jnp.uint32)
    return (plsc.bitcast(words << 16, jnp.float32),
            plsc.bitcast(words & jnp.uint32(HI_HALF), jnp.float32))


def _pick_scores(flat, slots, after):
    T, K = slots.shape
    nb = SC_PICK_TOKENS
    tok_per_w = T // V7X_SC_WORKERS
    nblk = tok_per_w // nb
    assert T % (V7X_SC_WORKERS * nb) == 0

    @functools.partial(
        pl.kernel, mesh=_sc_mesh(),
        out_type=jax.ShapeDtypeStruct((T, K), jnp.float32),
        compiler_params=pltpu.CompilerParams(needs_layout_passes=False),
        scratch_types=[
            pltpu.VMEM((2, nb, K), jnp.int32),
            pltpu.VMEM((2, nb, K), jnp.float32),
            pltpu.SemaphoreType.DMA((2,)),
            pltpu.SemaphoreType.DMA((2,)),
            pltpu.SemaphoreType.DMA((2,)),
        ],
        name="pick_scores")
    def k(flat_hbm, slots_hbm, after_hbm, out_hbm, idx_v, val_v, idx_sems, out_sems, row_sems):
        del after_hbm
        base = _sc_worker_id() * tok_per_w

        def load(b, slot):
            return pltpu.make_async_copy(slots_hbm.at[pl.ds(base + b * nb, nb)], idx_v.at[slot], idx_sems.at[slot])

        def store(b, slot):
            return pltpu.make_async_copy(val_v.at[slot], out_hbm.at[pl.ds(base + b * nb, nb)], out_sems.at[slot])

        def gathers(slot):
            return [pltpu.make_async_copy(flat_hbm.at[idx_v.at[slot, t]], val_v.at[slot, t], row_sems.at[slot])
                    for t in range(nb)]

        load(0, 0).start()

        @pl.loop(0, nblk)
        def _(b):
            slot = b % 2
            load(b, slot).wait()

            @pl.when(b + 1 < nblk)
            def _():
                load(b + 1, 1 - slot).start()

            @pl.when(b >= 2)
            def _():
                store(b - 2, slot).wait()

            for c in gathers(slot):
                c.start()
            for c in gathers(slot):
                c.wait()
            store(b, slot).start()

        if nblk >= 2:
            store(nblk - 2, nblk % 2).wait()
        store(nblk - 1, (nblk - 1) % 2).wait()

    return k(flat, slots, after)


def _expert_mix(w, idx, table):
    T, K = w.shape
    DW = table.shape[1]
    D = 2 * DW
    L, G = V7X_SC_LANES, SC_GATHER
    nj = SC_MIX_CHUNK // L
    tok_per_w = T // V7X_SC_WORKERS
    assert T % (V7X_SC_WORKERS * SC_TOKENS) == 0 and K % (2 * G) == 0
    assert DW % SC_MIX_CHUNK == 0 and G % SC_MIX_GROUP == 0

    @functools.partial(
        pl.kernel, mesh=_sc_mesh(),
        out_type=jax.ShapeDtypeStruct((T, D), jnp.float32),
        compiler_params=pltpu.CompilerParams(needs_layout_passes=False),
        scratch_types=[
            pltpu.VMEM((2, SC_TOKENS, K), jnp.int32),
            pltpu.VMEM((2, SC_TOKENS, K), jnp.int32),
            pltpu.VMEM((2, G, DW), jnp.uint32),
            pltpu.VMEM((2, SC_TOKENS, D), jnp.float32),
            pltpu.SemaphoreType.DMA((2,)),
            pltpu.SemaphoreType.DMA((2,)),
            pltpu.SemaphoreType.DMA((2,)),
            pltpu.SemaphoreType.DMA((2,)),
        ],
        name="expert_mix")
    def k(w_hbm, idx_hbm, tab_hbm, out_hbm, idx_v, w_v, rows_v, out_v,
          idx_sems, w_sems, out_sems, row_sems):
        base = _sc_worker_id() * tok_per_w

        def loads(b, slot):
            toks = pl.ds(base + b * SC_TOKENS, SC_TOKENS)
            return [pltpu.make_async_copy(idx_hbm.at[toks], idx_v.at[slot], idx_sems.at[slot]),
                    pltpu.make_async_copy(w_hbm.at[toks], w_v.at[slot], w_sems.at[slot])]

        def store(b, slot):
            toks = pl.ds(base + b * SC_TOKENS, SC_TOKENS)
            return pltpu.make_async_copy(out_v.at[slot], out_hbm.at[toks], out_sems.at[slot])

        def gather(slot, t, q, buf):
            return pltpu.make_async_copy(
                tab_hbm.at[idx_v.at[slot, t, pl.ds(q * G, G)]], rows_v.at[buf], row_sems.at[buf])

        def compute(slot, t, q, buf):
            ssplat = jnp.full((L,), slot, jnp.int32)
            tsplat = jnp.full((L,), t, jnp.int32)
            for c in range(DW // SC_MIX_CHUNK):
                def body(kg, acc):
                    kk = kg * SC_MIX_GROUP
                    wks = [_sc_bf16(plsc.load_gather(
                        w_v, [ssplat, tsplat, jnp.full((L,), q * G + i, jnp.int32) + kk]))
                        for i in range(SC_MIX_GROUP)]
                    out = []
                    for j in range(nj):
                        prods = [wks[i] * _sc_bf16(rows_v[buf, kk + i, pl.ds(c * SC_MIX_CHUNK + j * L, L)])
                                 for i in range(SC_MIX_GROUP)]
                        lo, hi = _sc_halves_f32(_tree_sum(prods))
                        out += [acc[2 * j] + lo, acc[2 * j + 1] + hi]
                    return tuple(out)

                zero = jnp.zeros((L,), jnp.float32)
                acc = plsc.parallel_loop(0, G // SC_MIX_GROUP, carry=(zero,) * (2 * nj))(body)
                for j in range(nj):
                    for half in range(2):
                        dst = out_v.at[slot, t, pl.ds(half * DW + c * SC_MIX_CHUNK + j * L, L)]
                        if q == 0:
                            dst[...] = acc[2 * j + half]
                        else:
                            plsc.addupdate(dst, acc[2 * j + half])

        _sc_block_pipeline(tok_per_w // SC_TOKENS, K // G, loads, store, gather, compute)

    return k(w, idx, table)


def _chunk_sizes(total):
    ramp, size = [EDGE_CHUNK], EDGE_CHUNK
    while size < MAX_CHUNK:
        ramp.append(size)
        size *= 2
    middle = total - 2 * sum(ramp)
    assert middle >= 0 and middle % MAX_CHUNK == 0
    return ramp + [MAX_CHUNK] * (middle // MAX_CHUNK) + ramp[::-1]


def kernel(x, norm_mix, w_in, pool_w, pool_scale, sgu_ln_g, sgu_ln_b, sgu_w, sgu_b, out_norm_pool,
           out_norm_sgu, w_out, norm_ffn, peer_wq, peer_keys, peer_u, peer_v, norm_final):
    B, S, D = x.shape
    assert norm_mix.shape[0] == 1, "single-layer block"
    T = B * S
    mix_args = (norm_mix[0], w_in[0], pool_w[0], pool_scale[0], sgu_ln_g[0], sgu_ln_b[0],
                sgu_w[0], sgu_b[0], out_norm_pool[0], out_norm_sgu[0], w_out[0])
    x1_parts = [(b * S, _mixer(x, b, 1, *mix_args).reshape(S, D)) for b in range(B)]
    wq = peer_wq[0].astype(jnp.bfloat16)
    keys = peer_keys[0].astype(jnp.bfloat16)
    u_t = peer_u[0].astype(jnp.bfloat16).T
    v_tab = _pack_table(peer_v[0])
    out = None
    tok0 = 0
    ws = [norm_ffn, v_tab]
    peers = [norm_ffn, norm_ffn]
    for tc in _chunk_sizes(T):
        part0, x1 = [p for p in x1_parts if p[0] <= tok0][-1]
        assert tok0 + tc <= part0 + x1.shape[0], "a token chunk must lie inside one mixer call"
        h2, idx, slots, gate = _router(x1, norm_ffn[0], wq, keys, tok0 - part0, tc, ws[-2])
        pre = _pick_scores(_dense_scores(h2, u_t), slots, peers[-2])
        w = _expert_weights(pre, gate)
        ws.append(w)
        peer = _expert_mix(w, idx, v_tab)
        peers.append(peer)
        out = _final(x1, tok0 - part0, peer, norm_final, out, tok0, T)
        tok0 += tc
    return out.reshape(B, S, D)
```

```python
import functools
import math

import jax
import jax.numpy as jnp
from jax import lax
from jax.experimental import pallas as pl
from jax.experimental.pallas import tpu as pltpu
from jax.experimental.pallas import tpu_sc as plsc

POOL_WINDOWS = (2, 4, 8, 16)
N_POOL_GROUPS = len(POOL_WINDOWS)
SGU_HEADS = 4
SGU_CHUNK = 128
PEER_HEADS = 8
PEER_N_KEYS = 128
PEER_D_HALF = 128
PEER_TOPK = 16
NORM_EPS = 1e-6
EXPERTS_PER_TOKEN = PEER_HEADS * PEER_TOPK

V7X_LANES = 128
V7X_SUBLANES = 8
V7X_SC_CORES = 2
V7X_SC_SUBCORES = 16
V7X_SC_LANES = 16
V7X_SC_WORKERS = V7X_SC_CORES * V7X_SC_SUBCORES

HALO = max(POOL_WINDOWS)
MIX_TILE = 512
ROUTE_TILE = 512
EW_TILE = 512
SCORE_TOK_TILE = 1024
SCORE_EXP_TILE = 1024
SC_PICK_TOKENS = 16
SC_GATHER = 64
SC_TOKENS = 8
SC_MIX_CHUNK = 128
SC_MIX_GROUP = 4
SC_UNROLL = 2
HI_HALF = 0xFFFF0000
TC_VMEM_LIMIT = 48 * 1024 * 1024
EDGE_CHUNK = 512
MAX_CHUNK = 2048


def _rms(x, g):
    inv = lax.rsqrt(jnp.mean(x * x, axis=-1, keepdims=True) + NORM_EPS)
    return x * inv * g


def _pack_halves(bits):
    half = bits.shape[1] // 2
    return (bits[:, :half] >> 16) | (bits[:, half:] & jnp.uint32(HI_HALF))


def _pack_table(a):
    return _pack_halves(lax.bitcast_convert_type(a.astype(jnp.bfloat16).astype(jnp.float32), jnp.uint32))


def _gelu(x):
    return 0.5 * x * (1.0 + lax.erf(x * math.sqrt(0.5)))


def _mixer_kernel(x_ref, xh_ref, nmix_ref, win_ref, poolw_ref, pscale_ref, lng_ref, lnb_ref,
                  sguw_ref, sgub_ref, onp_ref, ons_ref, wout_ref, o_ref, pext_ref, mix_ref):
    i = pl.program_id(1)
    ts = x_ref.shape[1]
    pool_w = pscale_ref.shape[1]
    gdim = pool_w // N_POOL_GROUPS
    sgu_w = lng_ref.shape[1]
    hdim = sgu_w // SGU_HEADS

    x = x_ref[0]
    h = _rms(x, nmix_ref[...]).astype(jnp.bfloat16)
    z = jnp.dot(h, win_ref[...], preferred_element_type=jnp.float32)
    p = z[:, :pool_w]

    hh = _rms(xh_ref[0], nmix_ref[...]).astype(jnp.bfloat16)
    ph = jnp.dot(hh, win_ref[:, :pool_w], preferred_element_type=jnp.float32)
    ph = jnp.where(i > 0, ph, 0.0)
    pext_ref[0:HALO, :] = ph
    pext_ref[HALO:HALO + ts, :] = p

    pos = i * ts + lax.broadcasted_iota(jnp.int32, (ts, 1), 0)
    ssq = jnp.zeros((ts, 1), jnp.float32)
    a_parts = []
    for g, win in enumerate(POOL_WINDOWS):
        cols = slice(g * gdim, (g + 1) * gdim)
        s = pext_ref[HALO:HALO + ts, cols]
        for j in range(1, win):
            s = s + pext_ref[HALO - j:HALO - j + ts, cols]
        cnt = jnp.minimum(pos + 1, win).astype(jnp.float32)
        d = (s / cnt - p[:, cols]).astype(jnp.bfloat16)
        a = jnp.dot(d, poolw_ref[g], preferred_element_type=jnp.float32) * pscale_ref[:, cols]
        ssq = ssq + jnp.sum(a * a, axis=-1, keepdims=True)
        a_parts.append(a)
    inv_a = lax.rsqrt(ssq / pool_w + NORM_EPS)
    for g in range(N_POOL_GROUPS):
        cols = slice(g * gdim, (g + 1) * gdim)
        mix_ref[:, cols] = (a_parts[g] * inv_a * onp_ref[:, cols]).astype(jnp.bfloat16)

    gz = _gelu(z[:, pool_w:])
    tril = (lax.broadcasted_iota(jnp.int32, (SGU_CHUNK, SGU_CHUNK), 0)
            >= lax.broadcasted_iota(jnp.int32, (SGU_CHUNK, SGU_CHUNK), 1))
    ssq = jnp.zeros((ts, 1), jnp.float32)
    b_parts = []
    for hd in range(SGU_HEADS):
        cols = slice(hd * hdim, (hd + 1) * hdim)
        u = gz[:, hd * hdim:(hd + 1) * hdim]
        v = gz[:, sgu_w + hd * hdim:sgu_w + (hd + 1) * hdim]
        mu = jnp.mean(v, axis=-1, keepdims=True)
        vc = v - mu
        var = jnp.mean(vc * vc, axis=-1, keepdims=True)
        vn = (vc * lax.rsqrt(var + NORM_EPS) * lng_ref[:, cols] + lnb_ref[:, cols]).astype(jnp.bfloat16)
        w = jnp.where(tril, sguw_ref[hd], jnp.zeros((), sguw_ref.dtype))
        mixed = [jnp.dot(w, vn[n * SGU_CHUNK:(n + 1) * SGU_CHUNK], preferred_element_type=jnp.float32)
                 + sgub_ref[hd] for n in range(ts // SGU_CHUNK)]
        b = u * jnp.concatenate(mixed, axis=0)
        ssq = ssq + jnp.sum(b * b, axis=-1, keepdims=True)
        b_parts.append(b)
    inv_b = lax.rsqrt(ssq / sgu_w + NORM_EPS)
    for hd in range(SGU_HEADS):
        cols = slice(hd * hdim, (hd + 1) * hdim)
        mix_ref[:, pool_w + hd * hdim:pool_w + (hd + 1) * hdim] = (
            b_parts[hd] * inv_b * ons_ref[:, cols]).astype(jnp.bfloat16)

    o_ref[0] = x + jnp.dot(mix_ref[...], wout_ref[...], preferred_element_type=jnp.float32)


def _mixer(x, b0, nb, norm_mix, w_in, pool_w, pool_scale, ln_g, ln_b, sgu_w, sgu_b, on_pool, on_sgu, w_out):
    _, S, D = x.shape
    ts = min(MIX_TILE, S)
    pool_width = pool_scale.size
    sgu_width = ln_g.size
    in_width = w_in.shape[1]
    gdim = pool_width // N_POOL_GROUPS
    halo_blocks = ts // HALO
    full = lambda shape: pl.BlockSpec(shape, lambda b, i: (0,) * len(shape))
    return pl.pallas_call(
        _mixer_kernel,
        grid=(nb, S // ts),
        in_specs=[
            pl.BlockSpec((1, ts, D), lambda b, i: (b0 + b, i, 0)),
            pl.BlockSpec((1, HALO, D), lambda b, i: (b0 + b, jnp.maximum(i * halo_blocks - 1, 0), 0)),
            full((1, D)),
            full((D, in_width)),
            full((N_POOL_GROUPS, gdim, gdim)),
            full((1, pool_width)),
            full((1, sgu_width)),
            full((1, sgu_width)),
            full((SGU_HEADS, SGU_CHUNK, SGU_CHUNK)),
            full((SGU_HEADS, SGU_CHUNK, SGU_CHUNK)),
            full((1, pool_width)),
            full((1, sgu_width)),
            full((pool_width + sgu_width, D)),
        ],
        out_specs=pl.BlockSpec((1, ts, D), lambda b, i: (b, i, 0)),
        out_shape=jax.ShapeDtypeStruct((nb, S, D), jnp.float32),
        scratch_shapes=[
            pltpu.VMEM((HALO + ts, pool_width), jnp.float32),
            pltpu.VMEM((ts, pool_width + sgu_width), jnp.bfloat16),
        ],
        compiler_params=pltpu.CompilerParams(
            dimension_semantics=("parallel", "arbitrary"), vmem_limit_bytes=TC_VMEM_LIMIT),
        name="mixer",
    )(x, x, norm_mix.reshape(1, D), w_in.astype(jnp.bfloat16), pool_w.astype(jnp.bfloat16),
      pool_scale.reshape(1, pool_width), ln_g.reshape(1, sgu_width), ln_b.reshape(1, sgu_width),
      sgu_w.astype(jnp.bfloat16),
      jnp.broadcast_to(sgu_b[:, :, None], (SGU_HEADS, SGU_CHUNK, SGU_CHUNK)),
      on_pool.reshape(1, pool_width), on_sgu.reshape(1, sgu_width), w_out.astype(jnp.bfloat16))


def _topk_rows(s, k):
    n = s.shape[0]
    iota = lax.broadcasted_iota(jnp.int32, s.shape, 0)
    vals, idxs = [], []
    for _ in range(k):
        m = jnp.max(s, axis=0, keepdims=True)
        ix = jnp.min(jnp.where(s == m, iota, n), axis=0, keepdims=True)
        vals.append(m)
        idxs.append(ix)
        s = jnp.where(iota == ix, -jnp.inf, s)
    return vals, idxs


def _pair_candidates(v1, i1, v2, i2):
    k = PEER_TOPK
    v2c, i2c = jnp.concatenate(v2, axis=0), jnp.concatenate(i2, axis=0)
    m = v2c.shape[1]
    vals, experts, flats = [], [], []
    a = 0
    while k // (a + 1) > 1:
        rows = -(-(k // (a + 1)) // V7X_SUBLANES) * V7X_SUBLANES
        vals.append(v1[a] + v2c[:rows])
        experts.append(i1[a] * PEER_N_KEYS + i2c[:rows])
        flats.append(a * k + lax.broadcasted_iota(jnp.int32, (rows, m), 0))
        a += 1
    vals.append(jnp.concatenate(v1[a:], axis=0) + v2[0])
    experts.append(jnp.concatenate(i1[a:], axis=0) * PEER_N_KEYS + i2[0])
    flats.append((a + lax.broadcasted_iota(jnp.int32, (k - a, m), 0)) * k)
    return jnp.concatenate(vals, axis=0), jnp.concatenate(experts, axis=0), jnp.concatenate(flats, axis=0)


def _router_kernel(chunk_tokens, x_ref, nffn_ref, wq_ref, keys_ref, after_ref, h2_ref, idx_ref, slot_ref,
                   gate_ref, q_ref, idxt_ref, gatet_ref):
    del after_ref
    h2 = _rms(x_ref[...], nffn_ref[...]).astype(jnp.bfloat16)
    h2_ref[...] = h2
    q_ref[...] = jnp.dot(h2, wq_ref[...], preferred_element_type=jnp.float32).astype(jnp.bfloat16)
    dq = 2 * PEER_D_HALF
    nt = (((1,), (1,)), ((), ()))

    def head(hd, carry):
        off = pl.multiple_of(hd * dq, dq)
        s1 = lax.dot_general(keys_ref[0], q_ref[:, pl.ds(off, PEER_D_HALF)], nt,
                             preferred_element_type=jnp.float32)
        s2 = lax.dot_general(keys_ref[1], q_ref[:, pl.ds(off + PEER_D_HALF, PEER_D_HALF)], nt,
                             preferred_element_type=jnp.float32)
        v1, i1 = _topk_rows(s1, PEER_TOPK)
        v2, i2 = _topk_rows(s2, PEER_TOPK)
        cand, expert, flat = _pair_candidates(v1, i1, v2, i2)
        cv, ce = [], []
        for _ in range(PEER_TOPK):
            m = jnp.max(cand, axis=0, keepdims=True)
            ix = jnp.min(jnp.where(cand == m, flat, PEER_TOPK * PEER_TOPK), axis=0, keepdims=True)
            hit = flat == ix
            cv.append(m)
            ce.append(jnp.max(jnp.where(hit, expert, -1), axis=0, keepdims=True))
            cand = jnp.where(hit, -jnp.inf, cand)
        cvc = jnp.concatenate(cv, axis=0)
        e = jnp.exp(cvc - cv[0])
        gate = e / jnp.sum(e, axis=0, keepdims=True)
        row = pl.multiple_of(hd * PEER_TOPK, PEER_TOPK)
        idxt_ref[pl.ds(row, PEER_TOPK), :] = jnp.concatenate(ce, axis=0)
        gatet_ref[pl.ds(row, PEER_TOPK), :] = gate
        return carry

    lax.fori_loop(0, PEER_HEADS, head, 0)
    idx = idxt_ref[...].T
    idx_ref[...] = idx
    tok = pl.program_id(0) * idx.shape[0] + lax.broadcasted_iota(jnp.int32, idx.shape, 0)
    slot_ref[...] = _tile_linear_index(_score_word_row(tok, chunk_tokens), idx, PEER_N_KEYS * PEER_N_KEYS)
    gate_ref[...] = gatet_ref[...].T


def _router(x1, norm_ffn, wq, keys, tok0, T, after):
    D = x1.shape[1]
    tr = min(ROUTE_TILE, T)
    qw = wq.shape[1]
    assert tok0 % tr == 0 and T % tr == 0
    first = tok0 // tr
    full = lambda shape: pl.BlockSpec(shape, lambda i: (0,) * len(shape))
    return pl.pallas_call(
        functools.partial(_router_kernel, T),
        grid=(T // tr,),
        in_specs=[
            pl.BlockSpec((tr, D), lambda i: (first + i, 0)),
            full((1, D)),
            full((D, qw)),
            full((2, PEER_N_KEYS, PEER_D_HALF)),
            pl.BlockSpec(memory_space=pl.ANY),
        ],
        out_specs=[
            pl.BlockSpec((tr, D), lambda i: (i, 0)),
            pl.BlockSpec((tr, EXPERTS_PER_TOKEN), lambda i: (i, 0)),
            pl.BlockSpec((tr, EXPERTS_PER_TOKEN), lambda i: (i, 0)),
            pl.BlockSpec((tr, EXPERTS_PER_TOKEN), lambda i: (i, 0)),
        ],
        out_shape=[
            jax.ShapeDtypeStruct((T, D), jnp.bfloat16),
            jax.ShapeDtypeStruct((T, EXPERTS_PER_TOKEN), jnp.int32),
            jax.ShapeDtypeStruct((T, EXPERTS_PER_TOKEN), jnp.int32),
            jax.ShapeDtypeStruct((T, EXPERTS_PER_TOKEN), jnp.float32),
        ],
        scratch_shapes=[
            pltpu.VMEM((tr, qw), jnp.bfloat16),
            pltpu.VMEM((EXPERTS_PER_TOKEN, tr), jnp.int32),
            pltpu.VMEM((EXPERTS_PER_TOKEN, tr), jnp.float32),
        ],
        compiler_params=pltpu.CompilerParams(
            dimension_semantics=("parallel",), vmem_limit_bytes=TC_VMEM_LIMIT),
        name="router",
    )(x1, norm_ffn.reshape(1, D), wq, keys, after)


def _tile_linear_index(row, col, ncols):
    return ((row >> 3) * (ncols * V7X_SUBLANES) + (col >> 7) * (V7X_SUBLANES * V7X_LANES)
            + (row & (V7X_SUBLANES - 1)) * V7X_LANES + (col & (V7X_LANES - 1)))


def _score_tile(tokens):
    return min(SCORE_TOK_TILE, tokens)


def _score_word_row(tok, tokens):
    tm = _score_tile(tokens)
    assert tm & (tm - 1) == 0, "power-of-two score tile"
    return (tok >> int(math.log2(tm))) * (tm // 2) + (tok & (tm // 2 - 1))


def _scores_kernel(h_ref, ut_ref, o_ref):
    acc = jnp.dot(h_ref[...], ut_ref[...], preferred_element_type=jnp.float32)
    tm, tn = acc.shape
    bits = pltpu.bitcast(acc.astype(jnp.bfloat16).astype(jnp.float32), jnp.uint32)
    words = (bits[:tm // 2] >> 16) | (bits[tm // 2:] & jnp.uint32(HI_HALF))
    for n in range(tn // V7X_LANES):
        o_ref[:, n * V7X_SUBLANES:(n + 1) * V7X_SUBLANES, :] = words[:, n * V7X_LANES:(n + 1) * V7X_LANES].reshape(
            tm // 2 // V7X_SUBLANES, V7X_SUBLANES, V7X_LANES)


def _dense_scores(h2, u_t):
    T, D = h2.shape
    E = u_t.shape[1]
    tm, tn = _score_tile(T), min(SCORE_EXP_TILE, E)
    assert T % tm == 0 and E % tn == 0 and tm % (2 * V7X_SUBLANES) == 0 and tn % V7X_LANES == 0
    out = pl.pallas_call(
        _scores_kernel,
        grid=(T // tm, E // tn),
        in_specs=[pl.BlockSpec((tm, D), lambda i, j: (i, 0)), pl.BlockSpec((D, tn), lambda i, j: (0, j))],
        out_specs=pl.BlockSpec((tm // 2 // V7X_SUBLANES, tn // V7X_LANES * V7X_SUBLANES, V7X_LANES),
                               lambda i, j: (i, j, 0)),
        out_shape=jax.ShapeDtypeStruct((T // 2 // V7X_SUBLANES, E // V7X_LANES * V7X_SUBLANES, V7X_LANES),
                                       jnp.uint32),
        compiler_params=pltpu.CompilerParams(
            dimension_semantics=("parallel", "arbitrary"), vmem_limit_bytes=TC_VMEM_LIMIT),
        name="dense_scores",
    )(h2, u_t)
    return out.reshape(-1)


def _act_kernel(pre_ref, gate_ref, w_ref):
    w = (gate_ref[...] * _gelu(pre_ref[...])).astype(jnp.bfloat16).astype(jnp.float32)
    hi = pltpu.bitcast(w, jnp.uint32) & jnp.uint32(HI_HALF)
    w_ref[...] = pltpu.bitcast(hi | (hi >> 16), jnp.int32)


def _expert_weights(pre, gate):
    T, K = pre.shape
    te = min(EW_TILE, T)
    spec = pl.BlockSpec((te, K), lambda i: (i, 0))
    return pl.pallas_call(
        _act_kernel, grid=(T // te,), in_specs=[spec, spec], out_specs=spec,
        out_shape=jax.ShapeDtypeStruct((T, K), jnp.int32),
        compiler_params=pltpu.CompilerParams(dimension_semantics=("parallel",)),
        name="expert_weights",
    )(pre, gate)


def _final_kernel(x_ref, y_ref, g_ref, *rest):
    o_ref = rest[-1]
    o_ref[...] = _rms(x_ref[...] + y_ref[...], g_ref[...])


def _final(x1, x_tok0, peer, norm_final, out_prev, out_tok0, total):
    Tc, D = peer.shape
    te = min(EW_TILE, Tc)
    assert x_tok0 % te == 0 and out_tok0 % te == 0 and Tc % te == 0
    nblk = Tc // te
    spec = pl.BlockSpec((te, D), lambda i: (out_tok0 // te + i, 0))
    in_specs = [pl.BlockSpec((te, D), lambda i: (x_tok0 // te + i, 0)),
                pl.BlockSpec((te, D), lambda i: (i, 0)), pl.BlockSpec((1, D), lambda i: (0, 0))]
    args = [x1, peer, norm_final.reshape(1, D)]
    aliases = {}
    if out_prev is not None:
        in_specs.append(pl.BlockSpec(memory_space=pl.ANY))
        args.append(out_prev)
        aliases = {3: 0}
    return pl.pallas_call(
        _final_kernel, grid=(nblk,), in_specs=in_specs, out_specs=spec,
        out_shape=jax.ShapeDtypeStruct((total, D), jnp.float32),
        input_output_aliases=aliases,
        compiler_params=pltpu.CompilerParams(dimension_semantics=("parallel",)),
        name="final_norm",
    )(*args)


def _tree_sum(vals):
    while len(vals) > 1:
        nxt = [vals[i] + vals[i + 1] for i in range(0, len(vals) - 1, 2)]
        if len(vals) % 2:
            nxt.append(vals[-1])
        vals = nxt
    return vals[0]


def _sc_block_pipeline(nblk, items_per_token, loads, store, gather, compute):
    assert items_per_token % 2 == 0 and nblk >= 1

    for c in loads(0, 0):
        c.start()
    for c in loads(0, 0):
        c.wait()
    if nblk > 1:
        for c in loads(1, 1):
            c.start()
    gather(0, 0, 0, 0).start()

    @pl.loop(0, nblk)
    def _(b):
        slot = b % 2

        @pl.when(b >= 2)
        def _():
            store(b - 2, slot).wait()

        @pl.loop(0, SC_TOKENS)
        def _(t):
            for q in range(items_per_token):
                buf = q % 2
                if q + 1 < items_per_token:
                    gather(slot, t, q + 1, 1 - buf).start()
                else:
                    @pl.when(t + 1 < SC_TOKENS)
                    def _():
                        gather(slot, t + 1, 0, 1 - buf).start()

                    @pl.when(jnp.logical_and(t + 1 == SC_TOKENS, b + 1 < nblk))
                    def _():
                        for c in loads(b + 1, 1 - slot):
                            c.wait()
                        gather(1 - slot, 0, 0, 1 - buf).start()

                gather(slot, t, q, buf).wait()
                compute(slot, t, q, buf)

        store(b, slot).start()

        @pl.when(b + 2 < nblk)
        def _():
            for c in loads(b + 2, slot):
                c.start()

    if nblk >= 2:
        store(nblk - 2, nblk % 2).wait()
    store(nblk - 1, (nblk - 1) % 2).wait()


def _sc_mesh():
    return plsc.VectorSubcoreMesh(core_axis_name="c", subcore_axis_name="s")


def _sc_worker_id():
    return lax.axis_index("s") * V7X_SC_CORES + lax.axis_index("c")


def _sc_bf16(words):
    return plsc.bitcast(words, jnp.bfloat16)


def _sc_halves_f32(pairs):
    words = plsc.bitcast(pairs, jnp.uint32)
    return (plsc.bitcast(words << 16, jnp.float32),
            plsc.bitcast(words & jnp.uint32(HI_HALF), jnp.float32))


def _pick_scores(flat, slots, after):
    T, K = slots.shape
    L, nb = V7X_SC_LANES, SC_PICK_TOKENS
    tok_per_w = T // V7X_SC_WORKERS
    nblk = tok_per_w // nb
    half_tile = _score_tile(T) // 2
    assert T % (V7X_SC_WORKERS * nb) == 0 and half_tile % nb == 0 and half_tile & (half_tile - 1) == 0

    @functools.partial(
        pl.kernel, mesh=_sc_mesh(),
        out_type=jax.ShapeDtypeStruct((T, K), jnp.float32),
        compiler_params=pltpu.CompilerParams(needs_layout_passes=False),
        scratch_types=[
            pltpu.VMEM((2, nb, K), jnp.int32),
            pltpu.VMEM((2, nb, K), jnp.uint32),
            pltpu.VMEM((2, nb, K), jnp.float32),
            pltpu.SemaphoreType.DMA((2,)),
            pltpu.SemaphoreType.DMA((2,)),
            pltpu.SemaphoreType.DMA((2,)),
        ],
        name="pick_scores")
    def k(flat_hbm, slots_hbm, after_hbm, out_hbm, idx_v, word_v, val_v, idx_sems, out_sems, row_sems):
        del after_hbm
        base = _sc_worker_id() * tok_per_w

        def load(b, slot):
            return pltpu.make_async_copy(slots_hbm.at[pl.ds(base + b * nb, nb)], idx_v.at[slot], idx_sems.at[slot])

        def store(b, slot):
            return pltpu.make_async_copy(val_v.at[slot], out_hbm.at[pl.ds(base + b * nb, nb)], out_sems.at[slot])

        def gathers(slot):
            return [pltpu.make_async_copy(flat_hbm.at[idx_v.at[slot, t]], word_v.at[slot, t], row_sems.at[slot])
                    for t in range(nb)]

        def unpack(b, slot):
            low = ((base + b * nb) & half_tile) == 0
            shift = jnp.full((L,), jnp.where(low, 16, 0), jnp.uint32)
            for t in range(nb):
                for j in range(K // L):
                    words = word_v[slot, t, pl.ds(j * L, L)]
                    val_v[slot, t, pl.ds(j * L, L)] = plsc.bitcast(
                        (words << shift) & jnp.uint32(HI_HALF), jnp.float32)

        load(0, 0).start()

        @pl.loop(0, nblk)
        def _(b):
            slot = b % 2
            load(b, slot).wait()

            @pl.when(b + 1 < nblk)
            def _():
                load(b + 1, 1 - slot).start()

            @pl.when(b >= 2)
            def _():
                store(b - 2, slot).wait()

            for c in gathers(slot):
                c.start()
            for c in gathers(slot):
                c.wait()
            unpack(b, slot)
            store(b, slot).start()

        if nblk >= 2:
            store(nblk - 2, nblk % 2).wait()
        store(nblk - 1, (nblk - 1) % 2).wait()

    return k(flat, slots, after)


def _expert_mix(w, idx, table):
    T, K = w.shape
    DW = table.shape[1]
    D = 2 * DW
    L, G = V7X_SC_LANES, SC_GATHER
    nj = SC_MIX_CHUNK // L
    tok_per_w = T // V7X_SC_WORKERS
    assert T % (V7X_SC_WORKERS * SC_TOKENS) == 0 and K % (2 * G) == 0
    assert DW % SC_MIX_CHUNK == 0 and G % SC_MIX_GROUP == 0

    @functools.partial(
        pl.kernel, mesh=_sc_mesh(),
        out_type=jax.ShapeDtypeStruct((T, D), jnp.float32),
        compiler_params=pltpu.CompilerParams(needs_layout_passes=False),
        scratch_types=[
            pltpu.VMEM((2, SC_TOKENS, K), jnp.int32),
            pltpu.VMEM((2, SC_TOKENS, K), jnp.int32),
            pltpu.VMEM((2, G, DW), jnp.uint32),
            pltpu.VMEM((2, SC_TOKENS, D), jnp.float32),
            pltpu.SemaphoreType.DMA((2,)),
            pltpu.SemaphoreType.DMA((2,)),
            pltpu.SemaphoreType.DMA((2,)),
            pltpu.SemaphoreType.DMA((2,)),
        ],
        name="expert_mix")
    def k(w_hbm, idx_hbm, tab_hbm, out_hbm, idx_v, w_v, rows_v, out_v,
          idx_sems, w_sems, out_sems, row_sems):
        base = _sc_worker_id() * tok_per_w

        def loads(b, slot):
            toks = pl.ds(base + b * SC_TOKENS, SC_TOKENS)
            return [pltpu.make_async_copy(idx_hbm.at[toks], idx_v.at[slot], idx_sems.at[slot]),
                    pltpu.make_async_copy(w_hbm.at[toks], w_v.at[slot], w_sems.at[slot])]

        def store(b, slot):
            toks = pl.ds(base + b * SC_TOKENS, SC_TOKENS)
            return pltpu.make_async_copy(out_v.at[slot], out_hbm.at[toks], out_sems.at[slot])

        def gather(slot, t, q, buf):
            return pltpu.make_async_copy(
                tab_hbm.at[idx_v.at[slot, t, pl.ds(q * G, G)]], rows_v.at[buf], row_sems.at[buf])

        def compute(slot, t, q, buf):
            ssplat = jnp.full((L,), slot, jnp.int32)
            tsplat = jnp.full((L,), t, jnp.int32)
            for c in range(DW // SC_MIX_CHUNK):
                def body(kg, acc):
                    kk = kg * SC_MIX_GROUP
                    wks = [_sc_bf16(plsc.load_gather(
                        w_v, [ssplat, tsplat, jnp.full((L,), q * G + i, jnp.int32) + kk]))
                        for i in range(SC_MIX_GROUP)]
                    out = []
                    for j in range(nj):
                        prods = [wks[i] * _sc_bf16(rows_v[buf, kk + i, pl.ds(c * SC_MIX_CHUNK + j * L, L)])
                                 for i in range(SC_MIX_GROUP)]
                        lo, hi = _sc_halves_f32(_tree_sum(prods))
                        out += [acc[2 * j] + lo, acc[2 * j + 1] + hi]
                    return tuple(out)

                zero = jnp.zeros((L,), jnp.float32)
                acc = plsc.parallel_loop(0, G // SC_MIX_GROUP, carry=(zero,) * (2 * nj))(body)
                for j in range(nj):
                    for half in range(2):
                        dst = out_v.at[slot, t, pl.ds(half * DW + c * SC_MIX_CHUNK + j * L, L)]
                        if q == 0:
                            dst[...] = acc[2 * j + half]
                        else:
                            plsc.addupdate(dst, acc[2 * j + half])

        _sc_block_pipeline(tok_per_w // SC_TOKENS, K // G, loads, store, gather, compute)

    return k(w, idx, table)


def _chunk_sizes(total):
    ramp, size = [EDGE_CHUNK], EDGE_CHUNK
    while size < MAX_CHUNK:
        ramp.append(size)
        size *= 2
    middle = total - 2 * sum(ramp)
    assert middle >= 0 and middle % MAX_CHUNK == 0
    return ramp + [MAX_CHUNK] * (middle // MAX_CHUNK) + ramp[::-1]


def kernel(x, norm_mix, w_in, pool_w, pool_scale, sgu_ln_g, sgu_ln_b, sgu_w, sgu_b, out_norm_pool,
           out_norm_sgu, w_out, norm_ffn, peer_wq, peer_keys, peer_u, peer_v, norm_final):
    B, S, D = x.shape
    assert norm_mix.shape[0] == 1, "single-layer block"
    T = B * S
    mix_args = (norm_mix[0], w_in[0], pool_w[0], pool_scale[0], sgu_ln_g[0], sgu_ln_b[0],
                sgu_w[0], sgu_b[0], out_norm_pool[0], out_norm_sgu[0], w_out[0])
    x1_parts = [(b * S, _mixer(x, b, 1, *mix_args).reshape(S, D)) for b in range(B)]
    wq = peer_wq[0].astype(jnp.bfloat16)
    keys = peer_keys[0].astype(jnp.bfloat16)
    u_t = peer_u[0].astype(jnp.bfloat16).T
    v_tab = _pack_table(peer_v[0])
    out = None
    tok0 = 0
    ws = [norm_ffn, v_tab]
    peers = [norm_ffn, norm_ffn]
    for tc in _chunk_sizes(T):
        part0, x1 = [p for p in x1_parts if p[0] <= tok0][-1]
        assert tok0 + tc <= part0 + x1.shape[0], "a token chunk must lie inside one mixer call"
        h2, idx, slots, gate = _router(x1, norm_ffn[0], wq, keys, tok0 - part0, tc, ws[-2])
        pre = _pick_scores(_dense_scores(h2, u_t), slots, peers[-2])
        w = _expert_weights(pre, gate)
        ws.append(w)
        peer = _expert_mix(w, idx, v_tab)
        peers.append(peer)
        out = _final(x1, tok0 - part0, peer, norm_final, out, tok0, T)
        tok0 += tc
    return out.reshape(B, S, D)
```

```python
import functools
import math

import jax
import jax.numpy as jnp
from jax import lax
from jax.experimental import pallas as pl
from jax.experimental.pallas import tpu as pltpu
from jax.experimental.pallas import tpu_sc as plsc

POOL_WINDOWS = (2, 4, 8, 16)
N_POOL_GROUPS = len(POOL_WINDOWS)
SGU_HEADS = 4
SGU_CHUNK = 128
PEER_HEADS = 8
PEER_N_KEYS = 128
PEER_D_HALF = 128
PEER_TOPK = 16
NORM_EPS = 1e-6
EXPERTS_PER_TOKEN = PEER_HEADS * PEER_TOPK

V7X_LANES = 128
V7X_SUBLANES = 8
V7X_SC_CORES = 2
V7X_SC_SUBCORES = 16
V7X_SC_LANES = 16
V7X_SC_WORKERS = V7X_SC_CORES * V7X_SC_SUBCORES

HALO = max(POOL_WINDOWS)
MIX_TILE = 512
ROUTE_TILE = 512
EW_TILE = 512
SCORE_TOK_TILE = 2048
SCORE_EXP_TILE = 1024
SC_PICK_TOKENS = 16
SC_GATHER = 64
SC_TOKENS = 8
SC_MIX_CHUNK = 128
SC_MIX_GROUP = 4
SC_UNROLL = 2
HI_HALF = 0xFFFF0000
TC_VMEM_LIMIT = 48 * 1024 * 1024
EDGE_CHUNK = 512
MAX_CHUNK = 2048


def _rms(x, g):
    inv = lax.rsqrt(jnp.mean(x * x, axis=-1, keepdims=True) + NORM_EPS)
    return x * inv * g


def _pack_halves(bits):
    half = bits.shape[1] // 2
    return (bits[:, :half] >> 16) | (bits[:, half:] & jnp.uint32(HI_HALF))


def _pack_table(a):
    return _pack_halves(lax.bitcast_convert_type(a.astype(jnp.bfloat16).astype(jnp.float32), jnp.uint32))


def _gelu(x):
    return 0.5 * x * (1.0 + lax.erf(x * math.sqrt(0.5)))


def _mixer_kernel(x_ref, xh_ref, nmix_ref, win_ref, poolw_ref, pscale_ref, lng_ref, lnb_ref,
                  sguw_ref, sgub_ref, onp_ref, ons_ref, wout_ref, o_ref, pext_ref, mix_ref):
    i = pl.program_id(1)
    ts = x_ref.shape[1]
    pool_w = pscale_ref.shape[1]
    gdim = pool_w // N_POOL_GROUPS
    sgu_w = lng_ref.shape[1]
    hdim = sgu_w // SGU_HEADS

    x = x_ref[0]
    h = _rms(x, nmix_ref[...]).astype(jnp.bfloat16)
    z = jnp.dot(h, win_ref[...], preferred_element_type=jnp.float32)
    p = z[:, :pool_w]

    hh = _rms(xh_ref[0], nmix_ref[...]).astype(jnp.bfloat16)
    ph = jnp.dot(hh, win_ref[:, :pool_w], preferred_element_type=jnp.float32)
    ph = jnp.where(i > 0, ph, 0.0)
    pext_ref[0:HALO, :] = ph
    pext_ref[HALO:HALO + ts, :] = p

    pos = i * ts + lax.broadcasted_iota(jnp.int32, (ts, 1), 0)
    ssq = jnp.zeros((ts, 1), jnp.float32)
    a_parts = []
    for g, win in enumerate(POOL_WINDOWS):
        cols = slice(g * gdim, (g + 1) * gdim)
        s = pext_ref[HALO:HALO + ts, cols]
        for j in range(1, win):
            s = s + pext_ref[HALO - j:HALO - j + ts, cols]
        cnt = jnp.minimum(pos + 1, win).astype(jnp.float32)
        d = (s / cnt - p[:, cols]).astype(jnp.bfloat16)
        a = jnp.dot(d, poolw_ref[g], preferred_element_type=jnp.float32) * pscale_ref[:, cols]
        ssq = ssq + jnp.sum(a * a, axis=-1, keepdims=True)
        a_parts.append(a)
    inv_a = lax.rsqrt(ssq / pool_w + NORM_EPS)
    for g in range(N_POOL_GROUPS):
        cols = slice(g * gdim, (g + 1) * gdim)
        mix_ref[:, cols] = (a_parts[g] * inv_a * onp_ref[:, cols]).astype(jnp.bfloat16)

    gz = _gelu(z[:, pool_w:])
    tril = (lax.broadcasted_iota(jnp.int32, (SGU_CHUNK, SGU_CHUNK), 0)
            >= lax.broadcasted_iota(jnp.int32, (SGU_CHUNK, SGU_CHUNK), 1))
    ssq = jnp.zeros((ts, 1), jnp.float32)
    b_parts = []
    for hd in range(SGU_HEADS):
        cols = slice(hd * hdim, (hd + 1) * hdim)
        u = gz[:, hd * hdim:(hd + 1) * hdim]
        v = gz[:, sgu_w + hd * hdim:sgu_w + (hd + 1) * hdim]
        mu = jnp.mean(v, axis=-1, keepdims=True)
        vc = v - mu
        var = jnp.mean(vc * vc, axis=-1, keepdims=True)
        vn = (vc * lax.rsqrt(var + NORM_EPS) * lng_ref[:, cols] + lnb_ref[:, cols]).astype(jnp.bfloat16)
        w = jnp.where(tril, sguw_ref[hd], jnp.zeros((), sguw_ref.dtype))
        mixed = [jnp.dot(w, vn[n * SGU_CHUNK:(n + 1) * SGU_CHUNK], preferred_element_type=jnp.float32)
                 + sgub_ref[hd] for n in range(ts // SGU_CHUNK)]
        b = u * jnp.concatenate(mixed, axis=0)
        ssq = ssq + jnp.sum(b * b, axis=-1, keepdims=True)
        b_parts.append(b)
    inv_b = lax.rsqrt(ssq / sgu_w + NORM_EPS)
    for hd in range(SGU_HEADS):
        cols = slice(hd * hdim, (hd + 1) * hdim)
        mix_ref[:, pool_w + hd * hdim:pool_w + (hd + 1) * hdim] = (
            b_parts[hd] * inv_b * ons_ref[:, cols]).astype(jnp.bfloat16)

    o_ref[0] = x + jnp.dot(mix_ref[...], wout_ref[...], preferred_element_type=jnp.float32)


def _mixer(x, b0, nb, norm_mix, w_in, pool_w, pool_scale, ln_g, ln_b, sgu_w, sgu_b, on_pool, on_sgu, w_out):
    _, S, D = x.shape
    ts = min(MIX_TILE, S)
    pool_width = pool_scale.size
    sgu_width = ln_g.size
    in_width = w_in.shape[1]
    gdim = pool_width // N_POOL_GROUPS
    halo_blocks = ts // HALO
    full = lambda shape: pl.BlockSpec(shape, lambda b, i: (0,) * len(shape))
    return pl.pallas_call(
        _mixer_kernel,
        grid=(nb, S // ts),
        in_specs=[
            pl.BlockSpec((1, ts, D), lambda b, i: (b0 + b, i, 0)),
            pl.BlockSpec((1, HALO, D), lambda b, i: (b0 + b, jnp.maximum(i * halo_blocks - 1, 0), 0)),
            full((1, D)),
            full((D, in_width)),
            full((N_POOL_GROUPS, gdim, gdim)),
            full((1, pool_width)),
            full((1, sgu_width)),
            full((1, sgu_width)),
            full((SGU_HEADS, SGU_CHUNK, SGU_CHUNK)),
            full((SGU_HEADS, SGU_CHUNK, SGU_CHUNK)),
            full((1, pool_width)),
            full((1, sgu_width)),
            full((pool_width + sgu_width, D)),
        ],
        out_specs=pl.BlockSpec((1, ts, D), lambda b, i: (b, i, 0)),
        out_shape=jax.ShapeDtypeStruct((nb, S, D), jnp.float32),
        scratch_shapes=[
            pltpu.VMEM((HALO + ts, pool_width), jnp.float32),
            pltpu.VMEM((ts, pool_width + sgu_width), jnp.bfloat16),
        ],
        compiler_params=pltpu.CompilerParams(
            dimension_semantics=("parallel", "arbitrary"), vmem_limit_bytes=TC_VMEM_LIMIT),
        name="mixer",
    )(x, x, norm_mix.reshape(1, D), w_in.astype(jnp.bfloat16), pool_w.astype(jnp.bfloat16),
      pool_scale.reshape(1, pool_width), ln_g.reshape(1, sgu_width), ln_b.reshape(1, sgu_width),
      sgu_w.astype(jnp.bfloat16),
      jnp.broadcast_to(sgu_b[:, :, None], (SGU_HEADS, SGU_CHUNK, SGU_CHUNK)),
      on_pool.reshape(1, pool_width), on_sgu.reshape(1, sgu_width), w_out.astype(jnp.bfloat16))


def _topk_rows(s, k):
    n = s.shape[0]
    iota = lax.broadcasted_iota(jnp.int32, s.shape, 0)
    vals, idxs = [], []
    for _ in range(k):
        m = jnp.max(s, axis=0, keepdims=True)
        ix = jnp.min(jnp.where(s == m, iota, n), axis=0, keepdims=True)
        vals.append(m)
        idxs.append(ix)
        s = jnp.where(iota == ix, -jnp.inf, s)
    return vals, idxs


def _pair_candidates(v1, i1, v2, i2):
    k = PEER_TOPK
    v2c, i2c = jnp.concatenate(v2, axis=0), jnp.concatenate(i2, axis=0)
    m = v2c.shape[1]
    vals, experts, flats = [], [], []
    a = 0
    while k // (a + 1) > 1:
        rows = -(-(k // (a + 1)) // V7X_SUBLANES) * V7X_SUBLANES
        vals.append(v1[a] + v2c[:rows])
        experts.append(i1[a] * PEER_N_KEYS + i2c[:rows])
        flats.append(a * k + lax.broadcasted_iota(jnp.int32, (rows, m), 0))
        a += 1
    vals.append(jnp.concatenate(v1[a:], axis=0) + v2[0])
    experts.append(jnp.concatenate(i1[a:], axis=0) * PEER_N_KEYS + i2[0])
    flats.append((a + lax.broadcasted_iota(jnp.int32, (k - a, m), 0)) * k)
    return jnp.concatenate(vals, axis=0), jnp.concatenate(experts, axis=0), jnp.concatenate(flats, axis=0)


def _router_kernel(chunk_tokens, x_ref, nffn_ref, wq_ref, keys_ref, after_ref, h2_ref, idx_ref, slot_ref,
                   gate_ref, q_ref, idxt_ref, gatet_ref):
    del after_ref
    h2 = _rms(x_ref[...], nffn_ref[...]).astype(jnp.bfloat16)
    h2_ref[...] = h2
    q_ref[...] = jnp.dot(h2, wq_ref[...], preferred_element_type=jnp.float32).astype(jnp.bfloat16)
    dq = 2 * PEER_D_HALF
    nt = (((1,), (1,)), ((), ()))

    def head(hd, carry):
        off = pl.multiple_of(hd * dq, dq)
        s1 = lax.dot_general(keys_ref[0], q_ref[:, pl.ds(off, PEER_D_HALF)], nt,
                             preferred_element_type=jnp.float32)
        s2 = lax.dot_general(keys_ref[1], q_ref[:, pl.ds(off + PEER_D_HALF, PEER_D_HALF)], nt,
                             preferred_element_type=jnp.float32)
        v1, i1 = _topk_rows(s1, PEER_TOPK)
        v2, i2 = _topk_rows(s2, PEER_TOPK)
        cand, expert, flat = _pair_candidates(v1, i1, v2, i2)
        cv, ce = [], []
        for _ in range(PEER_TOPK):
            m = jnp.max(cand, axis=0, keepdims=True)
            ix = jnp.min(jnp.where(cand == m, flat, PEER_TOPK * PEER_TOPK), axis=0, keepdims=True)
            hit = flat == ix
            cv.append(m)
            ce.append(jnp.max(jnp.where(hit, expert, -1), axis=0, keepdims=True))
            cand = jnp.where(hit, -jnp.inf, cand)
        cvc = jnp.concatenate(cv, axis=0)
        e = jnp.exp(cvc - cv[0])
        gate = e / jnp.sum(e, axis=0, keepdims=True)
        row = pl.multiple_of(hd * PEER_TOPK, PEER_TOPK)
        idxt_ref[pl.ds(row, PEER_TOPK), :] = jnp.concatenate(ce, axis=0)
        gatet_ref[pl.ds(row, PEER_TOPK), :] = gate
        return carry

    lax.fori_loop(0, PEER_HEADS, head, 0)
    idx = idxt_ref[...].T
    idx_ref[...] = idx
    tok = pl.program_id(0) * idx.shape[0] + lax.broadcasted_iota(jnp.int32, idx.shape, 0)
    slot_ref[...] = _tile_linear_index(_score_word_row(tok, chunk_tokens), idx, PEER_N_KEYS * PEER_N_KEYS)
    gate_ref[...] = gatet_ref[...].T


def _router(x1, norm_ffn, wq, keys, tok0, T, after):
    D = x1.shape[1]
    tr = min(ROUTE_TILE, T)
    qw = wq.shape[1]
    assert tok0 % tr == 0 and T % tr == 0
    first = tok0 // tr
    full = lambda shape: pl.BlockSpec(shape, lambda i: (0,) * len(shape))
    return pl.pallas_call(
        functools.partial(_router_kernel, T),
        grid=(T // tr,),
        in_specs=[
            pl.BlockSpec((tr, D), lambda i: (first + i, 0)),
            full((1, D)),
            full((D, qw)),
            full((2, PEER_N_KEYS, PEER_D_HALF)),
            pl.BlockSpec(memory_space=pl.ANY),
        ],
        out_specs=[
            pl.BlockSpec((tr, D), lambda i: (i, 0)),
            pl.BlockSpec((tr, EXPERTS_PER_TOKEN), lambda i: (i, 0)),
            pl.BlockSpec((tr, EXPERTS_PER_TOKEN), lambda i: (i, 0)),
            pl.BlockSpec((tr, EXPERTS_PER_TOKEN), lambda i: (i, 0)),
        ],
        out_shape=[
            jax.ShapeDtypeStruct((T, D), jnp.bfloat16),
            jax.ShapeDtypeStruct((T, EXPERTS_PER_TOKEN), jnp.int32),
            jax.ShapeDtypeStruct((T, EXPERTS_PER_TOKEN), jnp.int32),
            jax.ShapeDtypeStruct((T, EXPERTS_PER_TOKEN), jnp.float32),
        ],
        scratch_shapes=[
            pltpu.VMEM((tr, qw), jnp.bfloat16),
            pltpu.VMEM((EXPERTS_PER_TOKEN, tr), jnp.int32),
            pltpu.VMEM((EXPERTS_PER_TOKEN, tr), jnp.float32),
        ],
        compiler_params=pltpu.CompilerParams(
            dimension_semantics=("parallel",), vmem_limit_bytes=TC_VMEM_LIMIT),
        name="router",
    )(x1, norm_ffn.reshape(1, D), wq, keys, after)


def _tile_linear_index(row, col, ncols):
    return ((row >> 3) * (ncols * V7X_SUBLANES) + (col >> 7) * (V7X_SUBLANES * V7X_LANES)
            + (row & (V7X_SUBLANES - 1)) * V7X_LANES + (col & (V7X_LANES - 1)))


def _score_tile(tokens):
    return min(SCORE_TOK_TILE, tokens)


def _score_word_row(tok, tokens):
    tm = _score_tile(tokens)
    assert tm & (tm - 1) == 0, "power-of-two score tile"
    return (tok >> int(math.log2(tm))) * (tm // 2) + (tok & (tm // 2 - 1))


def _scores_kernel(h_ref, ut_ref, o_ref):
    acc = jnp.dot(h_ref[...], ut_ref[...], preferred_element_type=jnp.float32)
    tm, tn = acc.shape
    bits = pltpu.bitcast(acc.astype(jnp.bfloat16).astype(jnp.float32), jnp.uint32)
    words = (bits[:tm // 2] >> 16) | (bits[tm // 2:] & jnp.uint32(HI_HALF))
    for n in range(tn // V7X_LANES):
        o_ref[:, n * V7X_SUBLANES:(n + 1) * V7X_SUBLANES, :] = words[:, n * V7X_LANES:(n + 1) * V7X_LANES].reshape(
            tm // 2 // V7X_SUBLANES, V7X_SUBLANES, V7X_LANES)


def _dense_scores(h2, u_t):
    T, D = h2.shape
    E = u_t.shape[1]
    tm, tn = _score_tile(T), min(SCORE_EXP_TILE, E)
    assert T % tm == 0 and E % tn == 0 and tm % (2 * V7X_SUBLANES) == 0 and tn % V7X_LANES == 0
    out = pl.pallas_call(
        _scores_kernel,
        grid=(T // tm, E // tn),
        in_specs=[pl.BlockSpec((tm, D), lambda i, j: (i, 0)), pl.BlockSpec((D, tn), lambda i, j: (0, j))],
        out_specs=pl.BlockSpec((tm // 2 // V7X_SUBLANES, tn // V7X_LANES * V7X_SUBLANES, V7X_LANES),
                               lambda i, j: (i, j, 0)),
        out_shape=jax.ShapeDtypeStruct((T // 2 // V7X_SUBLANES, E // V7X_LANES * V7X_SUBLANES, V7X_LANES),
                                       jnp.uint32),
        compiler_params=pltpu.CompilerParams(
            dimension_semantics=("parallel", "arbitrary"), vmem_limit_bytes=TC_VMEM_LIMIT),
        name="dense_scores",
    )(h2, u_t)
    return out.reshape(-1)


def _act_kernel(pre_ref, gate_ref, w_ref):
    w = (gate_ref[...] * _gelu(pre_ref[...])).astype(jnp.bfloat16).astype(jnp.float32)
    hi = pltpu.bitcast(w, jnp.uint32) & jnp.uint32(HI_HALF)
    w_ref[...] = pltpu.bitcast(hi | (hi >> 16), jnp.int32)


def _expert_weights(pre, gate):
    T, K = pre.shape
    te = min(EW_TILE, T)
    spec = pl.BlockSpec((te, K), lambda i: (i, 0))
    return pl.pallas_call(
        _act_kernel, grid=(T // te,), in_specs=[spec, spec], out_specs=spec,
        out_shape=jax.ShapeDtypeStruct((T, K), jnp.int32),
        compiler_params=pltpu.CompilerParams(dimension_semantics=("parallel",)),
        name="expert_weights",
    )(pre, gate)


def _final_kernel(x_ref, y_ref, g_ref, *rest):
    o_ref = rest[-1]
    o_ref[...] = _rms(x_ref[...] + y_ref[...], g_ref[...])


def _final(x1, x_tok0, peer, norm_final, out_prev, out_tok0, total):
    Tc, D = peer.shape
    te = min(EW_TILE, Tc)
    assert x_tok0 % te == 0 and out_tok0 % te == 0 and Tc % te == 0
    nblk = Tc // te
    spec = pl.BlockSpec((te, D), lambda i: (out_tok0 // te + i, 0))
    in_specs = [pl.BlockSpec((te, D), lambda i: (x_tok0 // te + i, 0)),
                pl.BlockSpec((te, D), lambda i: (i, 0)), pl.BlockSpec((1, D), lambda i: (0, 0))]
    args = [x1, peer, norm_final.reshape(1, D)]
    aliases = {}
    if out_prev is not None:
        in_specs.append(pl.BlockSpec(memory_space=pl.ANY))
        args.append(out_prev)
        aliases = {3: 0}
    return pl.pallas_call(
        _final_kernel, grid=(nblk,), in_specs=in_specs, out_specs=spec,
        out_shape=jax.ShapeDtypeStruct((total, D), jnp.float32),
        input_output_aliases=aliases,
        compiler_params=pltpu.CompilerParams(dimension_semantics=("parallel",)),
        name="final_norm",
    )(*args)


def _tree_sum(vals):
    while len(vals) > 1:
        nxt = [vals[i] + vals[i + 1] for i in range(0, len(vals) - 1, 2)]
        if len(vals) % 2:
            nxt.append(vals[-1])
        vals = nxt
    return vals[0]


def _sc_block_pipeline(nblk, items_per_token, loads, store, gather, compute):
    assert items_per_token % 2 == 0 and nblk >= 1

    for c in loads(0, 0):
        c.start()
    for c in loads(0, 0):
        c.wait()
    if nblk > 1:
        for c in loads(1, 1):
            c.start()
    gather(0, 0, 0, 0).start()

    @pl.loop(0, nblk)
    def _(b):
        slot = b % 2

        @pl.when(b >= 2)
        def _():
            store(b - 2, slot).wait()

        @pl.loop(0, SC_TOKENS)
        def _(t):
            for q in range(items_per_token):
                buf = q % 2
                if q + 1 < items_per_token:
                    gather(slot, t, q + 1, 1 - buf).start()
                else:
                    @pl.when(t + 1 < SC_TOKENS)
                    def _():
                        gather(slot, t + 1, 0, 1 - buf).start()

                    @pl.when(jnp.logical_and(t + 1 == SC_TOKENS, b + 1 < nblk))
                    def _():
                        for c in loads(b + 1, 1 - slot):
                            c.wait()
                        gather(1 - slot, 0, 0, 1 - buf).start()

                gather(slot, t, q, buf).wait()
                compute(slot, t, q, buf)

        store(b, slot).start()

        @pl.when(b + 2 < nblk)
        def _():
            for c in loads(b + 2, slot):
                c.start()

    if nblk >= 2:
        store(nblk - 2, nblk % 2).wait()
    store(nblk - 1, (nblk - 1) % 2).wait()


def _sc_mesh():
    return plsc.VectorSubcoreMesh(core_axis_name="c", subcore_axis_name="s")


def _sc_worker_id():
    return lax.axis_index("s") * V7X_SC_CORES + lax.axis_index("c")


def _sc_bf16(words):
    return plsc.bitcast(words, jnp.bfloat16)


def _sc_halves_f32(pairs):
    words = plsc.bitcast(pairs, jnp.uint32)
    return (plsc.bitcast(words << 16, jnp.float32),
            plsc.bitcast(words & jnp.uint32(HI_HALF), jnp.float32))


def _pick_scores(flat, slots, after):
    T, K = slots.shape
    L, nb = V7X_SC_LANES, SC_PICK_TOKENS
    tok_per_w = T // V7X_SC_WORKERS
    nblk = tok_per_w // nb
    half_tile = _score_tile(T) // 2
    assert T % (V7X_SC_WORKERS * nb) == 0 and half_tile % nb == 0 and half_tile & (half_tile - 1) == 0

    @functools.partial(
        pl.kernel, mesh=_sc_mesh(),
        out_type=jax.ShapeDtypeStruct((T, K), jnp.float32),
        compiler_params=pltpu.CompilerParams(needs_layout_passes=False),
        scratch_types=[
            pltpu.VMEM((2, nb, K), jnp.int32),
            pltpu.VMEM((2, nb, K), jnp.uint32),
            pltpu.VMEM((2, nb, K), jnp.float32),
            pltpu.SemaphoreType.DMA((2,)),
            pltpu.SemaphoreType.DMA((2,)),
            pltpu.SemaphoreType.DMA((2,)),
        ],
        name="pick_scores")
    def k(flat_hbm, slots_hbm, after_hbm, out_hbm, idx_v, word_v, val_v, idx_sems, out_sems, row_sems):
        del after_hbm
        base = _sc_worker_id() * tok_per_w

        def load(b, slot):
            return pltpu.make_async_copy(slots_hbm.at[pl.ds(base + b * nb, nb)], idx_v.at[slot], idx_sems.at[slot])

        def store(b, slot):
            return pltpu.make_async_copy(val_v.at[slot], out_hbm.at[pl.ds(base + b * nb, nb)], out_sems.at[slot])

        def gathers(slot):
            return [pltpu.make_async_copy(flat_hbm.at[idx_v.at[slot, t]], word_v.at[slot, t], row_sems.at[slot])
                    for t in range(nb)]

        def unpack(b, slot):
            low = ((base + b * nb) & half_tile) == 0
            shift = jnp.full((L,), jnp.where(low, 16, 0), jnp.uint32)
            for t in range(nb):
                for j in range(K // L):
                    words = word_v[slot, t, pl.ds(j * L, L)]
                    val_v[slot, t, pl.ds(j * L, L)] = plsc.bitcast(
                        (words << shift) & jnp.uint32(HI_HALF), jnp.float32)

        load(0, 0).start()

        @pl.loop(0, nblk)
        def _(b):
            slot = b % 2
            load(b, slot).wait()

            @pl.when(b + 1 < nblk)
            def _():
                load(b + 1, 1 - slot).start()

            @pl.when(b >= 2)
            def _():
                store(b - 2, slot).wait()

            for c in gathers(slot):
                c.start()
            for c in gathers(slot):
                c.wait()
            unpack(b, slot)
            store(b, slot).start()

        if nblk >= 2:
            store(nblk - 2, nblk % 2).wait()
        store(nblk - 1, (nblk - 1) % 2).wait()

    return k(flat, slots, after)


def _expert_mix(w, idx, table):
    T, K = w.shape
    DW = table.shape[1]
    D = 2 * DW
    L, G = V7X_SC_LANES, SC_GATHER
    nj = SC_MIX_CHUNK // L
    tok_per_w = T // V7X_SC_WORKERS
    assert T % (V7X_SC_WORKERS * SC_TOKENS) == 0 and K % (2 * G) == 0
    assert DW % SC_MIX_CHUNK == 0 and G % SC_MIX_GROUP == 0

    @functools.partial(
        pl.kernel, mesh=_sc_mesh(),
        out_type=jax.ShapeDtypeStruct((T, D), jnp.float32),
        compiler_params=pltpu.CompilerParams(needs_layout_passes=False),
        scratch_types=[
            pltpu.VMEM((2, SC_TOKENS, K), jnp.int32),
            pltpu.VMEM((2, SC_TOKENS, K), jnp.int32),
            pltpu.VMEM((2, G, DW), jnp.uint32),
            pltpu.VMEM((2, SC_TOKENS, D), jnp.float32),
            pltpu.SemaphoreType.DMA((2,)),
            pltpu.SemaphoreType.DMA((2,)),
            pltpu.SemaphoreType.DMA((2,)),
            pltpu.SemaphoreType.DMA((2,)),
        ],
        name="expert_mix")
    def k(w_hbm, idx_hbm, tab_hbm, out_hbm, idx_v, w_v, rows_v, out_v,
          idx_sems, w_sems, out_sems, row_sems):
        base = _sc_worker_id() * tok_per_w

        def loads(b, slot):
            toks = pl.ds(base + b * SC_TOKENS, SC_TOKENS)
            return [pltpu.make_async_copy(idx_hbm.at[toks], idx_v.at[slot], idx_sems.at[slot]),
                    pltpu.make_async_copy(w_hbm.at[toks], w_v.at[slot], w_sems.at[slot])]

        def store(b, slot):
            toks = pl.ds(base + b * SC_TOKENS, SC_TOKENS)
            return pltpu.make_async_copy(out_v.at[slot], out_hbm.at[toks], out_sems.at[slot])

        def gather(slot, t, q, buf):
            return pltpu.make_async_copy(
                tab_hbm.at[idx_v.at[slot, t, pl.ds(q * G, G)]], rows_v.at[buf], row_sems.at[buf])

        def compute(slot, t, q, buf):
            ssplat = jnp.full((L,), slot, jnp.int32)
            tsplat = jnp.full((L,), t, jnp.int32)
            for c in range(DW // SC_MIX_CHUNK):
                def body(kg, acc):
                    kk = kg * SC_MIX_GROUP
                    wks = [_sc_bf16(plsc.load_gather(
                        w_v, [ssplat, tsplat, jnp.full((L,), q * G + i, jnp.int32) + kk]))
                        for i in range(SC_MIX_GROUP)]
                    out = []
                    for j in range(nj):
                        prods = [wks[i] * _sc_bf16(rows_v[buf, kk + i, pl.ds(c * SC_MIX_CHUNK + j * L, L)])
                                 for i in range(SC_MIX_GROUP)]
                        lo, hi = _sc_halves_f32(_tree_sum(prods))
                        out += [acc[2 * j] + lo, acc[2 * j + 1] + hi]
                    return tuple(out)

                zero = jnp.zeros((L,), jnp.float32)
                acc = plsc.parallel_loop(0, G // SC_MIX_GROUP, carry=(zero,) * (2 * nj))(body)
                for j in range(nj):
                    for half in range(2):
                        dst = out_v.at[slot, t, pl.ds(half * DW + c * SC_MIX_CHUNK + j * L, L)]
                        if q == 0:
                            dst[...] = acc[2 * j + half]
                        else:
                            plsc.addupdate(dst, acc[2 * j + half])

        _sc_block_pipeline(tok_per_w // SC_TOKENS, K // G, loads, store, gather, compute)

    return k(w, idx, table)


def _chunk_sizes(total):
    ramp, size = [EDGE_CHUNK], EDGE_CHUNK
    while size < MAX_CHUNK:
        ramp.append(size)
        size *= 2
    middle = total - 2 * sum(ramp)
    assert middle >= 0 and middle % MAX_CHUNK == 0
    return ramp + [MAX_CHUNK] * (middle // MAX_CHUNK) + ramp[::-1]


def kernel(x, norm_mix, w_in, pool_w, pool_scale, sgu_ln_g, sgu_ln_b, sgu_w, sgu_b, out_norm_pool,
           out_norm_sgu, w_out, norm_ffn, peer_wq, peer_keys, peer_u, peer_v, norm_final):
    B, S, D = x.shape
    assert norm_mix.shape[0] == 1, "single-layer block"
    T = B * S
    mix_args = (norm_mix[0], w_in[0], pool_w[0], pool_scale[0], sgu_ln_g[0], sgu_ln_b[0],
                sgu_w[0], sgu_b[0], out_norm_pool[0], out_norm_sgu[0], w_out[0])
    x1_parts = [(b * S, _mixer(x, b, 1, *mix_args).reshape(S, D)) for b in range(B)]
    wq = peer_wq[0].astype(jnp.bfloat16)
    keys = peer_keys[0].astype(jnp.bfloat16)
    u_t = peer_u[0].astype(jnp.bfloat16).T
    v_tab = _pack_table(peer_v[0])
    out = None
    tok0 = 0
    ws = [norm_ffn, v_tab]
    peers = [norm_ffn, norm_ffn]
    for tc in _chunk_sizes(T):
        part0, x1 = [p for p in x1_parts if p[0] <= tok0][-1]
        assert tok0 + tc <= part0 + x1.shape[0], "a token chunk must lie inside one mixer call"
        h2, idx, slots, gate = _router(x1, norm_ffn[0], wq, keys, tok0 - part0, tc, ws[-2])
        pre = _pick_scores(_dense_scores(h2, u_t), slots, peers[-2])
        w = _expert_weights(pre, gate)
        ws.append(w)
        peer = _expert_mix(w, idx, v_tab)
        peers.append(peer)
        out = _final(x1, tok0 - part0, peer, norm_final, out, tok0, T)
        tok0 += tc
    return out.reshape(B, S, D)
```

```python
import functools
import math

import jax
import jax.numpy as jnp
from jax import lax
from jax.experimental import pallas as pl
from jax.experimental.pallas import tpu as pltpu
from jax.experimental.pallas import tpu_sc as plsc

POOL_WINDOWS = (2, 4, 8, 16)
N_POOL_GROUPS = len(POOL_WINDOWS)
SGU_HEADS = 4
SGU_CHUNK = 128
PEER_HEADS = 8
PEER_N_KEYS = 128
PEER_D_HALF = 128
PEER_TOPK = 16
NORM_EPS = 1e-6
EXPERTS_PER_TOKEN = PEER_HEADS * PEER_TOPK

V7X_LANES = 128
V7X_SUBLANES = 8
V7X_SC_CORES = 2
V7X_SC_SUBCORES = 16
V7X_SC_LANES = 16
V7X_SC_WORKERS = V7X_SC_CORES * V7X_SC_SUBCORES

HALO = max(POOL_WINDOWS)
MIX_TILE = 512
ROUTE_TILE = 1024
EW_TILE = 512
SCORE_TOK_TILE = 2048
SCORE_EXP_TILE = 1024
SC_PICK_TOKENS = 16
SC_GATHER = 64
SC_TOKENS = 8
SC_MIX_CHUNK = 128
SC_MIX_GROUP = 4
SC_UNROLL = 2
HI_HALF = 0xFFFF0000
TC_VMEM_LIMIT = 48 * 1024 * 1024
EDGE_CHUNK = 512
MAX_CHUNK = 2048


def _rms(x, g):
    inv = lax.rsqrt(jnp.mean(x * x, axis=-1, keepdims=True) + NORM_EPS)
    return x * inv * g


def _pack_halves(bits):
    half = bits.shape[1] // 2
    return (bits[:, :half] >> 16) | (bits[:, half:] & jnp.uint32(HI_HALF))


def _pack_table(a):
    return _pack_halves(lax.bitcast_convert_type(a.astype(jnp.bfloat16).astype(jnp.float32), jnp.uint32))


def _gelu(x):
    return 0.5 * x * (1.0 + lax.erf(x * math.sqrt(0.5)))


def _mixer_kernel(x_ref, xh_ref, nmix_ref, win_ref, poolw_ref, pscale_ref, lng_ref, lnb_ref,
                  sguw_ref, sgub_ref, onp_ref, ons_ref, wout_ref, o_ref, pext_ref, mix_ref):
    i = pl.program_id(1)
    ts = x_ref.shape[1]
    pool_w = pscale_ref.shape[1]
    gdim = pool_w // N_POOL_GROUPS
    sgu_w = lng_ref.shape[1]
    hdim = sgu_w // SGU_HEADS

    x = x_ref[0]
    h = _rms(x, nmix_ref[...]).astype(jnp.bfloat16)
    z = jnp.dot(h, win_ref[...], preferred_element_type=jnp.float32)
    p = z[:, :pool_w]

    hh = _rms(xh_ref[0], nmix_ref[...]).astype(jnp.bfloat16)
    ph = jnp.dot(hh, win_ref[:, :pool_w], preferred_element_type=jnp.float32)
    ph = jnp.where(i > 0, ph, 0.0)
    pext_ref[0:HALO, :] = ph
    pext_ref[HALO:HALO + ts, :] = p

    pos = i * ts + lax.broadcasted_iota(jnp.int32, (ts, 1), 0)
    ssq = jnp.zeros((ts, 1), jnp.float32)
    a_parts = []
    for g, win in enumerate(POOL_WINDOWS):
        cols = slice(g * gdim, (g + 1) * gdim)
        s = pext_ref[HALO:HALO + ts, cols]
        for j in range(1, win):
            s = s + pext_ref[HALO - j:HALO - j + ts, cols]
        cnt = jnp.minimum(pos + 1, win).astype(jnp.float32)
        d = (s / cnt - p[:, cols]).astype(jnp.bfloat16)
        a = jnp.dot(d, poolw_ref[g], preferred_element_type=jnp.float32) * pscale_ref[:, cols]
        ssq = ssq + jnp.sum(a * a, axis=-1, keepdims=True)
        a_parts.append(a)
    inv_a = lax.rsqrt(ssq / pool_w + NORM_EPS)
    for g in range(N_POOL_GROUPS):
        cols = slice(g * gdim, (g + 1) * gdim)
        mix_ref[:, cols] = (a_parts[g] * inv_a * onp_ref[:, cols]).astype(jnp.bfloat16)

    gz = _gelu(z[:, pool_w:])
    tril = (lax.broadcasted_iota(jnp.int32, (SGU_CHUNK, SGU_CHUNK), 0)
            >= lax.broadcasted_iota(jnp.int32, (SGU_CHUNK, SGU_CHUNK), 1))
    ssq = jnp.zeros((ts, 1), jnp.float32)
    b_parts = []
    for hd in range(SGU_HEADS):
        cols = slice(hd * hdim, (hd + 1) * hdim)
        u = gz[:, hd * hdim:(hd + 1) * hdim]
        v = gz[:, sgu_w + hd * hdim:sgu_w + (hd + 1) * hdim]
        mu = jnp.mean(v, axis=-1, keepdims=True)
        vc = v - mu
        var = jnp.mean(vc * vc, axis=-1, keepdims=True)
        vn = (vc * lax.rsqrt(var + NORM_EPS) * lng_ref[:, cols] + lnb_ref[:, cols]).astype(jnp.bfloat16)
        w = jnp.where(tril, sguw_ref[hd], jnp.zeros((), sguw_ref.dtype))
        mixed = [jnp.dot(w, vn[n * SGU_CHUNK:(n + 1) * SGU_CHUNK], preferred_element_type=jnp.float32)
                 + sgub_ref[hd] for n in range(ts // SGU_CHUNK)]
        b = u * jnp.concatenate(mixed, axis=0)
        ssq = ssq + jnp.sum(b * b, axis=-1, keepdims=True)
        b_parts.append(b)
    inv_b = lax.rsqrt(ssq / sgu_w + NORM_EPS)
    for hd in range(SGU_HEADS):
        cols = slice(hd * hdim, (hd + 1) * hdim)
        mix_ref[:, pool_w + hd * hdim:pool_w + (hd + 1) * hdim] = (
            b_parts[hd] * inv_b * ons_ref[:, cols]).astype(jnp.bfloat16)

    o_ref[0] = x + jnp.dot(mix_ref[...], wout_ref[...], preferred_element_type=jnp.float32)


def _mixer(x, b0, nb, norm_mix, w_in, pool_w, pool_scale, ln_g, ln_b, sgu_w, sgu_b, on_pool, on_sgu, w_out):
    _, S, D = x.shape
    ts = min(MIX_TILE, S)
    pool_width = pool_scale.size
    sgu_width = ln_g.size
    in_width = w_in.shape[1]
    gdim = pool_width // N_POOL_GROUPS
    halo_blocks = ts // HALO
    full = lambda shape: pl.BlockSpec(shape, lambda b, i: (0,) * len(shape))
    return pl.pallas_call(
        _mixer_kernel,
        grid=(nb, S // ts),
        in_specs=[
            pl.BlockSpec((1, ts, D), lambda b, i: (b0 + b, i, 0)),
            pl.BlockSpec((1, HALO, D), lambda b, i: (b0 + b, jnp.maximum(i * halo_blocks - 1, 0), 0)),
            full((1, D)),
            full((D, in_width)),
            full((N_POOL_GROUPS, gdim, gdim)),
            full((1, pool_width)),
            full((1, sgu_width)),
            full((1, sgu_width)),
            full((SGU_HEADS, SGU_CHUNK, SGU_CHUNK)),
            full((SGU_HEADS, SGU_CHUNK, SGU_CHUNK)),
            full((1, pool_width)),
            full((1, sgu_width)),
            full((pool_width + sgu_width, D)),
        ],
        out_specs=pl.BlockSpec((1, ts, D), lambda b, i: (b, i, 0)),
        out_shape=jax.ShapeDtypeStruct((nb, S, D), jnp.float32),
        scratch_shapes=[
            pltpu.VMEM((HALO + ts, pool_width), jnp.float32),
            pltpu.VMEM((ts, pool_width + sgu_width), jnp.bfloat16),
        ],
        compiler_params=pltpu.CompilerParams(
            dimension_semantics=("parallel", "arbitrary"), vmem_limit_bytes=TC_VMEM_LIMIT),
        name="mixer",
    )(x, x, norm_mix.reshape(1, D), w_in.astype(jnp.bfloat16), pool_w.astype(jnp.bfloat16),
      pool_scale.reshape(1, pool_width), ln_g.reshape(1, sgu_width), ln_b.reshape(1, sgu_width),
      sgu_w.astype(jnp.bfloat16),
      jnp.broadcast_to(sgu_b[:, :, None], (SGU_HEADS, SGU_CHUNK, SGU_CHUNK)),
      on_pool.reshape(1, pool_width), on_sgu.reshape(1, sgu_width), w_out.astype(jnp.bfloat16))


def _topk_rows(s, k):
    n = s.shape[0]
    iota = lax.broadcasted_iota(jnp.int32, s.shape, 0)
    vals, idxs = [], []
    for _ in range(k):
        m = jnp.max(s, axis=0, keepdims=True)
        ix = jnp.min(jnp.where(s == m, iota, n), axis=0, keepdims=True)
        vals.append(m)
        idxs.append(ix)
        s = jnp.where(iota == ix, -jnp.inf, s)
    return vals, idxs


def _pair_candidates(v1, i1, v2, i2):
    k = PEER_TOPK
    v2c, i2c = jnp.concatenate(v2, axis=0), jnp.concatenate(i2, axis=0)
    m = v2c.shape[1]
    vals, experts, flats = [], [], []
    a = 0
    while k // (a + 1) > 1:
        rows = -(-(k // (a + 1)) // V7X_SUBLANES) * V7X_SUBLANES
        vals.append(v1[a] + v2c[:rows])
        experts.append(i1[a] * PEER_N_KEYS + i2c[:rows])
        flats.append(a * k + lax.broadcasted_iota(jnp.int32, (rows, m), 0))
        a += 1
    vals.append(jnp.concatenate(v1[a:], axis=0) + v2[0])
    experts.append(jnp.concatenate(i1[a:], axis=0) * PEER_N_KEYS + i2[0])
    flats.append((a + lax.broadcasted_iota(jnp.int32, (k - a, m), 0)) * k)
    return jnp.concatenate(vals, axis=0), jnp.concatenate(experts, axis=0), jnp.concatenate(flats, axis=0)


def _router_kernel(chunk_tokens, x_ref, nffn_ref, wq_ref, keys_ref, after_ref, h2_ref, idx_ref, slot_ref,
                   gate_ref, q_ref, idxt_ref, gatet_ref):
    del after_ref
    h2 = _rms(x_ref[...], nffn_ref[...]).astype(jnp.bfloat16)
    h2_ref[...] = h2
    q_ref[...] = jnp.dot(h2, wq_ref[...], preferred_element_type=jnp.float32).astype(jnp.bfloat16)
    dq = 2 * PEER_D_HALF
    nt = (((1,), (1,)), ((), ()))

    def head(hd, carry):
        off = pl.multiple_of(hd * dq, dq)
        s1 = lax.dot_general(keys_ref[0], q_ref[:, pl.ds(off, PEER_D_HALF)], nt,
                             preferred_element_type=jnp.float32)
        s2 = lax.dot_general(keys_ref[1], q_ref[:, pl.ds(off + PEER_D_HALF, PEER_D_HALF)], nt,
                             preferred_element_type=jnp.float32)
        v1, i1 = _topk_rows(s1, PEER_TOPK)
        v2, i2 = _topk_rows(s2, PEER_TOPK)
        cand, expert, flat = _pair_candidates(v1, i1, v2, i2)
        cv, ce = [], []
        for _ in range(PEER_TOPK):
            m = jnp.max(cand, axis=0, keepdims=True)
            ix = jnp.min(jnp.where(cand == m, flat, PEER_TOPK * PEER_TOPK), axis=0, keepdims=True)
            hit = flat == ix
            cv.append(m)
            ce.append(jnp.max(jnp.where(hit, expert, -1), axis=0, keepdims=True))
            cand = jnp.where(hit, -jnp.inf, cand)
        cvc = jnp.concatenate(cv, axis=0)
        e = jnp.exp(cvc - cv[0])
        gate = e / jnp.sum(e, axis=0, keepdims=True)
        row = pl.multiple_of(hd * PEER_TOPK, PEER_TOPK)
        idxt_ref[pl.ds(row, PEER_TOPK), :] = jnp.concatenate(ce, axis=0)
        gatet_ref[pl.ds(row, PEER_TOPK), :] = gate
        return carry

    lax.fori_loop(0, PEER_HEADS, head, 0)
    idx = idxt_ref[...].T
    idx_ref[...] = idx
    tok = pl.program_id(0) * idx.shape[0] + lax.broadcasted_iota(jnp.int32, idx.shape, 0)
    slot_ref[...] = _tile_linear_index(_score_word_row(tok, chunk_tokens), idx, PEER_N_KEYS * PEER_N_KEYS)
    gate_ref[...] = gatet_ref[...].T


def _router(x1, norm_ffn, wq, keys, tok0, T, after):
    D = x1.shape[1]
    tr = min(ROUTE_TILE, T)
    qw = wq.shape[1]
    assert tok0 % tr == 0 and T % tr == 0
    first = tok0 // tr
    full = lambda shape: pl.BlockSpec(shape, lambda i: (0,) * len(shape))
    return pl.pallas_call(
        functools.partial(_router_kernel, T),
        grid=(T // tr,),
        in_specs=[
            pl.BlockSpec((tr, D), lambda i: (first + i, 0)),
            full((1, D)),
            full((D, qw)),
            full((2, PEER_N_KEYS, PEER_D_HALF)),
            pl.BlockSpec(memory_space=pl.ANY),
        ],
        out_specs=[
            pl.BlockSpec((tr, D), lambda i: (i, 0)),
            pl.BlockSpec((tr, EXPERTS_PER_TOKEN), lambda i: (i, 0)),
            pl.BlockSpec((tr, EXPERTS_PER_TOKEN), lambda i: (i, 0)),
            pl.BlockSpec((tr, EXPERTS_PER_TOKEN), lambda i: (i, 0)),
        ],
        out_shape=[
            jax.ShapeDtypeStruct((T, D), jnp.bfloat16),
            jax.ShapeDtypeStruct((T, EXPERTS_PER_TOKEN), jnp.int32),
            jax.ShapeDtypeStruct((T, EXPERTS_PER_TOKEN), jnp.int32),
            jax.ShapeDtypeStruct((T, EXPERTS_PER_TOKEN), jnp.float32),
        ],
        scratch_shapes=[
            pltpu.VMEM((tr, qw), jnp.bfloat16),
            pltpu.VMEM((EXPERTS_PER_TOKEN, tr), jnp.int32),
            pltpu.VMEM((EXPERTS_PER_TOKEN, tr), jnp.float32),
        ],
        compiler_params=pltpu.CompilerParams(
            dimension_semantics=("parallel",), vmem_limit_bytes=TC_VMEM_LIMIT),
        name="router",
    )(x1, norm_ffn.reshape(1, D), wq, keys, after)


def _tile_linear_index(row, col, ncols):
    return ((row >> 3) * (ncols * V7X_SUBLANES) + (col >> 7) * (V7X_SUBLANES * V7X_LANES)
            + (row & (V7X_SUBLANES - 1)) * V7X_LANES + (col & (V7X_LANES - 1)))


def _score_tile(tokens):
    return min(SCORE_TOK_TILE, tokens)


def _score_word_row(tok, tokens):
    tm = _score_tile(tokens)
    assert tm & (tm - 1) == 0, "power-of-two score tile"
    return (tok >> int(math.log2(tm))) * (tm // 2) + (tok & (tm // 2 - 1))


def _scores_kernel(h_ref, ut_ref, o_ref):
    acc = jnp.dot(h_ref[...], ut_ref[...], preferred_element_type=jnp.float32)
    tm, tn = acc.shape
    bits = pltpu.bitcast(acc.astype(jnp.bfloat16).astype(jnp.float32), jnp.uint32)
    words = (bits[:tm // 2] >> 16) | (bits[tm // 2:] & jnp.uint32(HI_HALF))
    for n in range(tn // V7X_LANES):
        o_ref[:, n * V7X_SUBLANES:(n + 1) * V7X_SUBLANES, :] = words[:, n * V7X_LANES:(n + 1) * V7X_LANES].reshape(
            tm // 2 // V7X_SUBLANES, V7X_SUBLANES, V7X_LANES)


def _dense_scores(h2, u_t):
    T, D = h2.shape
    E = u_t.shape[1]
    tm, tn = _score_tile(T), min(SCORE_EXP_TILE, E)
    assert T % tm == 0 and E % tn == 0 and tm % (2 * V7X_SUBLANES) == 0 and tn % V7X_LANES == 0
    out = pl.pallas_call(
        _scores_kernel,
        grid=(T // tm, E // tn),
        in_specs=[pl.BlockSpec((tm, D), lambda i, j: (i, 0)), pl.BlockSpec((D, tn), lambda i, j: (0, j))],
        out_specs=pl.BlockSpec((tm // 2 // V7X_SUBLANES, tn // V7X_LANES * V7X_SUBLANES, V7X_LANES),
                               lambda i, j: (i, j, 0)),
        out_shape=jax.ShapeDtypeStruct((T // 2 // V7X_SUBLANES, E // V7X_LANES * V7X_SUBLANES, V7X_LANES),
                                       jnp.uint32),
        compiler_params=pltpu.CompilerParams(
            dimension_semantics=("parallel", "arbitrary"), vmem_limit_bytes=TC_VMEM_LIMIT),
        name="dense_scores",
    )(h2, u_t)
    return out.reshape(-1)


def _act_kernel(pre_ref, gate_ref, w_ref):
    w = (gate_ref[...] * _gelu(pre_ref[...])).astype(jnp.bfloat16).astype(jnp.float32)
    hi = pltpu.bitcast(w, jnp.uint32) & jnp.uint32(HI_HALF)
    w_ref[...] = pltpu.bitcast(hi | (hi >> 16), jnp.int32)


def _expert_weights(pre, gate):
    T, K = pre.shape
    te = min(EW_TILE, T)
    spec = pl.BlockSpec((te, K), lambda i: (i, 0))
    return pl.pallas_call(
        _act_kernel, grid=(T // te,), in_specs=[spec, spec], out_specs=spec,
        out_shape=jax.ShapeDtypeStruct((T, K), jnp.int32),
        compiler_params=pltpu.CompilerParams(dimension_semantics=("parallel",)),
        name="expert_weights",
    )(pre, gate)


def _final_kernel(x_ref, y_ref, g_ref, *rest):
    o_ref = rest[-1]
    o_ref[...] = _rms(x_ref[...] + y_ref[...], g_ref[...])


def _final(x1, x_tok0, peer, norm_final, out_prev, out_tok0, total):
    Tc, D = peer.shape
    te = min(EW_TILE, Tc)
    assert x_tok0 % te == 0 and out_tok0 % te == 0 and Tc % te == 0
    nblk = Tc // te
    spec = pl.BlockSpec((te, D), lambda i: (out_tok0 // te + i, 0))
    in_specs = [pl.BlockSpec((te, D), lambda i: (x_tok0 // te + i, 0)),
                pl.BlockSpec((te, D), lambda i: (i, 0)), pl.BlockSpec((1, D), lambda i: (0, 0))]
    args = [x1, peer, norm_final.reshape(1, D)]
    aliases = {}
    if out_prev is not None:
        in_specs.append(pl.BlockSpec(memory_space=pl.ANY))
        args.append(out_prev)
        aliases = {3: 0}
    return pl.pallas_call(
        _final_kernel, grid=(nblk,), in_specs=in_specs, out_specs=spec,
        out_shape=jax.ShapeDtypeStruct((total, D), jnp.float32),
        input_output_aliases=aliases,
        compiler_params=pltpu.CompilerParams(dimension_semantics=("parallel",)),
        name="final_norm",
    )(*args)


def _tree_sum(vals):
    while len(vals) > 1:
        nxt = [vals[i] + vals[i + 1] for i in range(0, len(vals) - 1, 2)]
        if len(vals) % 2:
            nxt.append(vals[-1])
        vals = nxt
    return vals[0]


def _sc_block_pipeline(nblk, items_per_token, loads, store, gather, compute):
    assert items_per_token % 2 == 0 and nblk >= 1

    for c in loads(0, 0):
        c.start()
    for c in loads(0, 0):
        c.wait()
    if nblk > 1:
        for c in loads(1, 1):
            c.start()
    gather(0, 0, 0, 0).start()

    @pl.loop(0, nblk)
    def _(b):
        slot = b % 2

        @pl.when(b >= 2)
        def _():
            store(b - 2, slot).wait()

        @pl.loop(0, SC_TOKENS)
        def _(t):
            for q in range(items_per_token):
                buf = q % 2
                if q + 1 < items_per_token:
                    gather(slot, t, q + 1, 1 - buf).start()
                else:
                    @pl.when(t + 1 < SC_TOKENS)
                    def _():
                        gather(slot, t + 1, 0, 1 - buf).start()

                    @pl.when(jnp.logical_and(t + 1 == SC_TOKENS, b + 1 < nblk))
                    def _():
                        for c in loads(b + 1, 1 - slot):
                            c.wait()
                        gather(1 - slot, 0, 0, 1 - buf).start()

                gather(slot, t, q, buf).wait()
                compute(slot, t, q, buf)

        store(b, slot).start()

        @pl.when(b + 2 < nblk)
        def _():
            for c in loads(b + 2, slot):
                c.start()

    if nblk >= 2:
        store(nblk - 2, nblk % 2).wait()
    store(nblk - 1, (nblk - 1) % 2).wait()


def _sc_mesh():
    return plsc.VectorSubcoreMesh(core_axis_name="c", subcore_axis_name="s")


def _sc_worker_id():
    return lax.axis_index("s") * V7X_SC_CORES + lax.axis_index("c")


def _sc_bf16(words):
    return plsc.bitcast(words, jnp.bfloat16)


def _sc_halves_f32(pairs):
    words = plsc.bitcast(pairs, jnp.uint32)
    return (plsc.bitcast(words << 16, jnp.float32),
            plsc.bitcast(words & jnp.uint32(HI_HALF), jnp.float32))


def _pick_scores(flat, slots, after):
    T, K = slots.shape
    L, nb = V7X_SC_LANES, SC_PICK_TOKENS
    tok_per_w = T // V7X_SC_WORKERS
    nblk = tok_per_w // nb
    half_tile = _score_tile(T) // 2
    assert T % (V7X_SC_WORKERS * nb) == 0 and half_tile % nb == 0 and half_tile & (half_tile - 1) == 0

    @functools.partial(
        pl.kernel, mesh=_sc_mesh(),
        out_type=jax.ShapeDtypeStruct((T, K), jnp.float32),
        compiler_params=pltpu.CompilerParams(needs_layout_passes=False),
        scratch_types=[
            pltpu.VMEM((2, nb, K), jnp.int32),
            pltpu.VMEM((2, nb, K), jnp.uint32),
            pltpu.VMEM((2, nb, K), jnp.float32),
            pltpu.SemaphoreType.DMA((2,)),
            pltpu.SemaphoreType.DMA((2,)),
            pltpu.SemaphoreType.DMA((2,)),
        ],
        name="pick_scores")
    def k(flat_hbm, slots_hbm, after_hbm, out_hbm, idx_v, word_v, val_v, idx_sems, out_sems, row_sems):
        del after_hbm
        base = _sc_worker_id() * tok_per_w

        def load(b, slot):
            return pltpu.make_async_copy(slots_hbm.at[pl.ds(base + b * nb, nb)], idx_v.at[slot], idx_sems.at[slot])

        def store(b, slot):
            return pltpu.make_async_copy(val_v.at[slot], out_hbm.at[pl.ds(base + b * nb, nb)], out_sems.at[slot])

        def gathers(slot):
            return [pltpu.make_async_copy(flat_hbm.at[idx_v.at[slot, t]], word_v.at[slot, t], row_sems.at[slot])
                    for t in range(nb)]

        def unpack(b, slot):
            low = ((base + b * nb) & half_tile) == 0
            shift = jnp.full((L,), jnp.where(low, 16, 0), jnp.uint32)
            for t in range(nb):
                for j in range(K // L):
                    words = word_v[slot, t, pl.ds(j * L, L)]
                    val_v[slot, t, pl.ds(j * L, L)] = plsc.bitcast(
                        (words << shift) & jnp.uint32(HI_HALF), jnp.float32)

        load(0, 0).start()

        @pl.loop(0, nblk)
        def _(b):
            slot = b % 2
            load(b, slot).wait()

            @pl.when(b + 1 < nblk)
            def _():
                load(b + 1, 1 - slot).start()

            @pl.when(b >= 2)
            def _():
                store(b - 2, slot).wait()

            for c in gathers(slot):
                c.start()
            for c in gathers(slot):
                c.wait()
            unpack(b, slot)
            store(b, slot).start()

        if nblk >= 2:
            store(nblk - 2, nblk % 2).wait()
        store(nblk - 1, (nblk - 1) % 2).wait()

    return k(flat, slots, after)


def _expert_mix(w, idx, table):
    T, K = w.shape
    DW = table.shape[1]
    D = 2 * DW
    L, G = V7X_SC_LANES, SC_GATHER
    nj = SC_MIX_CHUNK // L
    tok_per_w = T // V7X_SC_WORKERS
    assert T % (V7X_SC_WORKERS * SC_TOKENS) == 0 and K % (2 * G) == 0
    assert DW % SC_MIX_CHUNK == 0 and G % SC_MIX_GROUP == 0

    @functools.partial(
        pl.kernel, mesh=_sc_mesh(),
        out_type=jax.ShapeDtypeStruct((T, D), jnp.float32),
        compiler_params=pltpu.CompilerParams(needs_layout_passes=False),
        scratch_types=[
            pltpu.VMEM((2, SC_TOKENS, K), jnp.int32),
            pltpu.VMEM((2, SC_TOKENS, K), jnp.int32),
            pltpu.VMEM((2, G, DW), jnp.uint32),
            pltpu.VMEM((2, SC_TOKENS, D), jnp.float32),
            pltpu.SemaphoreType.DMA((2,)),
            pltpu.SemaphoreType.DMA((2,)),
            pltpu.SemaphoreType.DMA((2,)),
            pltpu.SemaphoreType.DMA((2,)),
        ],
        name="expert_mix")
    def k(w_hbm, idx_hbm, tab_hbm, out_hbm, idx_v, w_v, rows_v, out_v,
          idx_sems, w_sems, out_sems, row_sems):
        base = _sc_worker_id() * tok_per_w

        def loads(b, slot):
            toks = pl.ds(base + b * SC_TOKENS, SC_TOKENS)
            return [pltpu.make_async_copy(idx_hbm.at[toks], idx_v.at[slot], idx_sems.at[slot]),
                    pltpu.make_async_copy(w_hbm.at[toks], w_v.at[slot], w_sems.at[slot])]

        def store(b, slot):
            toks = pl.ds(base + b * SC_TOKENS, SC_TOKENS)
            return pltpu.make_async_copy(out_v.at[slot], out_hbm.at[toks], out_sems.at[slot])

        def gather(slot, t, q, buf):
            return pltpu.make_async_copy(
                tab_hbm.at[idx_v.at[slot, t, pl.ds(q * G, G)]], rows_v.at[buf], row_sems.at[buf])

        def compute(slot, t, q, buf):
            ssplat = jnp.full((L,), slot, jnp.int32)
            tsplat = jnp.full((L,), t, jnp.int32)
            for c in range(DW // SC_MIX_CHUNK):
                def body(kg, acc):
                    kk = kg * SC_MIX_GROUP
                    wks = [_sc_bf16(plsc.load_gather(
                        w_v, [ssplat, tsplat, jnp.full((L,), q * G + i, jnp.int32) + kk]))
                        for i in range(SC_MIX_GROUP)]
                    out = []
                    for j in range(nj):
                        prods = [wks[i] * _sc_bf16(rows_v[buf, kk + i, pl.ds(c * SC_MIX_CHUNK + j * L, L)])
                                 for i in range(SC_MIX_GROUP)]
                        lo, hi = _sc_halves_f32(_tree_sum(prods))
                        out += [acc[2 * j] + lo, acc[2 * j + 1] + hi]
                    return tuple(out)

                zero = jnp.zeros((L,), jnp.float32)
                acc = plsc.parallel_loop(0, G // SC_MIX_GROUP, carry=(zero,) * (2 * nj))(body)
                for j in range(nj):
                    for half in range(2):
                        dst = out_v.at[slot, t, pl.ds(half * DW + c * SC_MIX_CHUNK + j * L, L)]
                        if q == 0:
                            dst[...] = acc[2 * j + half]
                        else:
                            plsc.addupdate(dst, acc[2 * j + half])

        _sc_block_pipeline(tok_per_w // SC_TOKENS, K // G, loads, store, gather, compute)

    return k(w, idx, table)


def _chunk_sizes(total):
    ramp, size = [EDGE_CHUNK], EDGE_CHUNK
    while size < MAX_CHUNK:
        ramp.append(size)
        size *= 2
    middle = total - 2 * sum(ramp)
    assert middle >= 0 and middle % MAX_CHUNK == 0
    return ramp + [MAX_CHUNK] * (middle // MAX_CHUNK) + ramp[::-1]


def kernel(x, norm_mix, w_in, pool_w, pool_scale, sgu_ln_g, sgu_ln_b, sgu_w, sgu_b, out_norm_pool,
           out_norm_sgu, w_out, norm_ffn, peer_wq, peer_keys, peer_u, peer_v, norm_final):
    B, S, D = x.shape
    assert norm_mix.shape[0] == 1, "single-layer block"
    T = B * S
    mix_args = (norm_mix[0], w_in[0], pool_w[0], pool_scale[0], sgu_ln_g[0], sgu_ln_b[0],
                sgu_w[0], sgu_b[0], out_norm_pool[0], out_norm_sgu[0], w_out[0])
    x1_parts = [(b * S, _mixer(x, b, 1, *mix_args).reshape(S, D)) for b in range(B)]
    wq = peer_wq[0].astype(jnp.bfloat16)
    keys = peer_keys[0].astype(jnp.bfloat16)
    u_t = peer_u[0].astype(jnp.bfloat16).T
    v_tab = _pack_table(peer_v[0])
    out = None
    tok0 = 0
    ws = [norm_ffn, v_tab]
    peers = [norm_ffn, norm_ffn]
    for tc in _chunk_sizes(T):
        part0, x1 = [p for p in x1_parts if p[0] <= tok0][-1]
        assert tok0 + tc <= part0 + x1.shape[0], "a token chunk must lie inside one mixer call"
        h2, idx, slots, gate = _router(x1, norm_ffn[0], wq, keys, tok0 - part0, tc, ws[-2])
        pre = _pick_scores(_dense_scores(h2, u_t), slots, peers[-2])
        w = _expert_weights(pre, gate)
        ws.append(w)
        peer = _expert_mix(w, idx, v_tab)
        peers.append(peer)
        out = _final(x1, tok0 - part0, peer, norm_final, out, tok0, T)
        tok0 += tc
    return out.reshape(B, S, D)
```

```python
import functools
import math

import jax
import jax.numpy as jnp
from jax import lax
from jax.experimental import pallas as pl
from jax.experimental.pallas import tpu as pltpu
from jax.experimental.pallas import tpu_sc as plsc

POOL_WINDOWS = (2, 4, 8, 16)
N_POOL_GROUPS = len(POOL_WINDOWS)
SGU_HEADS = 4
SGU_CHUNK = 128
PEER_HEADS = 8
PEER_N_KEYS = 128
PEER_D_HALF = 128
PEER_TOPK = 16
NORM_EPS = 1e-6
EXPERTS_PER_TOKEN = PEER_HEADS * PEER_TOPK

V7X_LANES = 128
V7X_SUBLANES = 8
V7X_SC_CORES = 2
V7X_SC_SUBCORES = 16
V7X_SC_LANES = 16
V7X_SC_WORKERS = V7X_SC_CORES * V7X_SC_SUBCORES

HALO = max(POOL_WINDOWS)
MIX_TILE = 512
ROUTE_TILE = 1024
EW_TILE = 512
SCORE_TOK_TILE = 2048
SCORE_EXP_TILE = 1024
SC_PICK_TOKENS = 8
SC_GATHER = 64
SC_TOKENS = 8
SC_MIX_CHUNK = 128
SC_MIX_GROUP = 4
SC_UNROLL = 2
HI_HALF = 0xFFFF0000
TC_VMEM_LIMIT = 48 * 1024 * 1024
EDGE_CHUNK = 256
MAX_CHUNK = 2048


def _rms(x, g):
    inv = lax.rsqrt(jnp.mean(x * x, axis=-1, keepdims=True) + NORM_EPS)
    return x * inv * g


def _pack_halves(bits):
    half = bits.shape[1] // 2
    return (bits[:, :half] >> 16) | (bits[:, half:] & jnp.uint32(HI_HALF))


def _pack_table(a):
    return _pack_halves(lax.bitcast_convert_type(a.astype(jnp.bfloat16).astype(jnp.float32), jnp.uint32))


def _gelu(x):
    return 0.5 * x * (1.0 + lax.erf(x * math.sqrt(0.5)))


def _mixer_kernel(x_ref, xh_ref, nmix_ref, win_ref, poolw_ref, pscale_ref, lng_ref, lnb_ref,
                  sguw_ref, sgub_ref, onp_ref, ons_ref, wout_ref, o_ref, pext_ref, mix_ref):
    i = pl.program_id(1)
    ts = x_ref.shape[1]
    pool_w = pscale_ref.shape[1]
    gdim = pool_w // N_POOL_GROUPS
    sgu_w = lng_ref.shape[1]
    hdim = sgu_w // SGU_HEADS

    x = x_ref[0]
    h = _rms(x, nmix_ref[...]).astype(jnp.bfloat16)
    z = jnp.dot(h, win_ref[...], preferred_element_type=jnp.float32)
    p = z[:, :pool_w]

    hh = _rms(xh_ref[0], nmix_ref[...]).astype(jnp.bfloat16)
    ph = jnp.dot(hh, win_ref[:, :pool_w], preferred_element_type=jnp.float32)
    ph = jnp.where(i > 0, ph, 0.0)
    pext_ref[0:HALO, :] = ph
    pext_ref[HALO:HALO + ts, :] = p

    pos = i * ts + lax.broadcasted_iota(jnp.int32, (ts, 1), 0)
    ssq = jnp.zeros((ts, 1), jnp.float32)
    a_parts = []
    for g, win in enumerate(POOL_WINDOWS):
        cols = slice(g * gdim, (g + 1) * gdim)
        s = pext_ref[HALO:HALO + ts, cols]
        for j in range(1, win):
            s = s + pext_ref[HALO - j:HALO - j + ts, cols]
        cnt = jnp.minimum(pos + 1, win).astype(jnp.float32)
        d = (s / cnt - p[:, cols]).astype(jnp.bfloat16)
        a = jnp.dot(d, poolw_ref[g], preferred_element_type=jnp.float32) * pscale_ref[:, cols]
        ssq = ssq + jnp.sum(a * a, axis=-1, keepdims=True)
        a_parts.append(a)
    inv_a = lax.rsqrt(ssq / pool_w + NORM_EPS)
    for g in range(N_POOL_GROUPS):
        cols = slice(g * gdim, (g + 1) * gdim)
        mix_ref[:, cols] = (a_parts[g] * inv_a * onp_ref[:, cols]).astype(jnp.bfloat16)

    gz = _gelu(z[:, pool_w:])
    tril = (lax.broadcasted_iota(jnp.int32, (SGU_CHUNK, SGU_CHUNK), 0)
            >= lax.broadcasted_iota(jnp.int32, (SGU_CHUNK, SGU_CHUNK), 1))
    ssq = jnp.zeros((ts, 1), jnp.float32)
    b_parts = []
    for hd in range(SGU_HEADS):
        cols = slice(hd * hdim, (hd + 1) * hdim)
        u = gz[:, hd * hdim:(hd + 1) * hdim]
        v = gz[:, sgu_w + hd * hdim:sgu_w + (hd + 1) * hdim]
        mu = jnp.mean(v, axis=-1, keepdims=True)
        vc = v - mu
        var = jnp.mean(vc * vc, axis=-1, keepdims=True)
        vn = (vc * lax.rsqrt(var + NORM_EPS) * lng_ref[:, cols] + lnb_ref[:, cols]).astype(jnp.bfloat16)
        w = jnp.where(tril, sguw_ref[hd], jnp.zeros((), sguw_ref.dtype))
        mixed = [jnp.dot(w, vn[n * SGU_CHUNK:(n + 1) * SGU_CHUNK], preferred_element_type=jnp.float32)
                 + sgub_ref[hd] for n in range(ts // SGU_CHUNK)]
        b = u * jnp.concatenate(mixed, axis=0)
        ssq = ssq + jnp.sum(b * b, axis=-1, keepdims=True)
        b_parts.append(b)
    inv_b = lax.rsqrt(ssq / sgu_w + NORM_EPS)
    for hd in range(SGU_HEADS):
        cols = slice(hd * hdim, (hd + 1) * hdim)
        mix_ref[:, pool_w + hd * hdim:pool_w + (hd + 1) * hdim] = (
            b_parts[hd] * inv_b * ons_ref[:, cols]).astype(jnp.bfloat16)

    o_ref[0] = x + jnp.dot(mix_ref[...], wout_ref[...], preferred_element_type=jnp.float32)


def _mixer(x, b0, nb, norm_mix, w_in, pool_w, pool_scale, ln_g, ln_b, sgu_w, sgu_b, on_pool, on_sgu, w_out):
    _, S, D = x.shape
    ts = min(MIX_TILE, S)
    pool_width = pool_scale.size
    sgu_width = ln_g.size
    in_width = w_in.shape[1]
    gdim = pool_width // N_POOL_GROUPS
    halo_blocks = ts // HALO
    full = lambda shape: pl.BlockSpec(shape, lambda b, i: (0,) * len(shape))
    return pl.pallas_call(
        _mixer_kernel,
        grid=(nb, S // ts),
        in_specs=[
            pl.BlockSpec((1, ts, D), lambda b, i: (b0 + b, i, 0)),
            pl.BlockSpec((1, HALO, D), lambda b, i: (b0 + b, jnp.maximum(i * halo_blocks - 1, 0), 0)),
            full((1, D)),
            full((D, in_width)),
            full((N_POOL_GROUPS, gdim, gdim)),
            full((1, pool_width)),
            full((1, sgu_width)),
            full((1, sgu_width)),
            full((SGU_HEADS, SGU_CHUNK, SGU_CHUNK)),
            full((SGU_HEADS, SGU_CHUNK, SGU_CHUNK)),
            full((1, pool_width)),
            full((1, sgu_width)),
            full((pool_width + sgu_width, D)),
        ],
        out_specs=pl.BlockSpec((1, ts, D), lambda b, i: (b, i, 0)),
        out_shape=jax.ShapeDtypeStruct((nb, S, D), jnp.float32),
        scratch_shapes=[
            pltpu.VMEM((HALO + ts, pool_width), jnp.float32),
            pltpu.VMEM((ts, pool_width + sgu_width), jnp.bfloat16),
        ],
        compiler_params=pltpu.CompilerParams(
            dimension_semantics=("parallel", "arbitrary"), vmem_limit_bytes=TC_VMEM_LIMIT),
        name="mixer",
    )(x, x, norm_mix.reshape(1, D), w_in.astype(jnp.bfloat16), pool_w.astype(jnp.bfloat16),
      pool_scale.reshape(1, pool_width), ln_g.reshape(1, sgu_width), ln_b.reshape(1, sgu_width),
      sgu_w.astype(jnp.bfloat16),
      jnp.broadcast_to(sgu_b[:, :, None], (SGU_HEADS, SGU_CHUNK, SGU_CHUNK)),
      on_pool.reshape(1, pool_width), on_sgu.reshape(1, sgu_width), w_out.astype(jnp.bfloat16))


def _topk_rows(s, k):
    n = s.shape[0]
    iota = lax.broadcasted_iota(jnp.int32, s.shape, 0)
    vals, idxs = [], []
    for _ in range(k):
        m = jnp.max(s, axis=0, keepdims=True)
        ix = jnp.min(jnp.where(s == m, iota, n), axis=0, keepdims=True)
        vals.append(m)
        idxs.append(ix)
        s = jnp.where(iota == ix, -jnp.inf, s)
    return vals, idxs


def _pair_candidates(v1, i1, v2, i2):
    k = PEER_TOPK
    v2c, i2c = jnp.concatenate(v2, axis=0), jnp.concatenate(i2, axis=0)
    m = v2c.shape[1]
    vals, experts, flats = [], [], []
    a = 0
    while k // (a + 1) > 1:
        rows = -(-(k // (a + 1)) // V7X_SUBLANES) * V7X_SUBLANES
        vals.append(v1[a] + v2c[:rows])
        experts.append(i1[a] * PEER_N_KEYS + i2c[:rows])
        flats.append(a * k + lax.broadcasted_iota(jnp.int32, (rows, m), 0))
        a += 1
    vals.append(jnp.concatenate(v1[a:], axis=0) + v2[0])
    experts.append(jnp.concatenate(i1[a:], axis=0) * PEER_N_KEYS + i2[0])
    flats.append((a + lax.broadcasted_iota(jnp.int32, (k - a, m), 0)) * k)
    return jnp.concatenate(vals, axis=0), jnp.concatenate(experts, axis=0), jnp.concatenate(flats, axis=0)


def _router_kernel(chunk_tokens, x_ref, nffn_ref, wq_ref, keys_ref, after_ref, h2_ref, idx_ref, slot_ref,
                   gate_ref, q_ref, idxt_ref, gatet_ref):
    del after_ref
    h2 = _rms(x_ref[...], nffn_ref[...]).astype(jnp.bfloat16)
    h2_ref[...] = h2
    q_ref[...] = jnp.dot(h2, wq_ref[...], preferred_element_type=jnp.float32).astype(jnp.bfloat16)
    dq = 2 * PEER_D_HALF
    nt = (((1,), (1,)), ((), ()))

    def head(hd, carry):
        off = pl.multiple_of(hd * dq, dq)
        s1 = lax.dot_general(keys_ref[0], q_ref[:, pl.ds(off, PEER_D_HALF)], nt,
                             preferred_element_type=jnp.float32)
        s2 = lax.dot_general(keys_ref[1], q_ref[:, pl.ds(off + PEER_D_HALF, PEER_D_HALF)], nt,
                             preferred_element_type=jnp.float32)
        v1, i1 = _topk_rows(s1, PEER_TOPK)
        v2, i2 = _topk_rows(s2, PEER_TOPK)
        cand, expert, flat = _pair_candidates(v1, i1, v2, i2)
        cv, ce = [], []
        for _ in range(PEER_TOPK):
            m = jnp.max(cand, axis=0, keepdims=True)
            ix = jnp.min(jnp.where(cand == m, flat, PEER_TOPK * PEER_TOPK), axis=0, keepdims=True)
            hit = flat == ix
            cv.append(m)
            ce.append(jnp.max(jnp.where(hit, expert, -1), axis=0, keepdims=True))
            cand = jnp.where(hit, -jnp.inf, cand)
        cvc = jnp.concatenate(cv, axis=0)
        e = jnp.exp(cvc - cv[0])
        gate = e / jnp.sum(e, axis=0, keepdims=True)
        row = pl.multiple_of(hd * PEER_TOPK, PEER_TOPK)
        idxt_ref[pl.ds(row, PEER_TOPK), :] = jnp.concatenate(ce, axis=0)
        gatet_ref[pl.ds(row, PEER_TOPK), :] = gate
        return carry

    lax.fori_loop(0, PEER_HEADS, head, 0)
    idx = idxt_ref[...].T
    idx_ref[...] = idx
    tok = pl.program_id(0) * idx.shape[0] + lax.broadcasted_iota(jnp.int32, idx.shape, 0)
    slot_ref[...] = _tile_linear_index(_score_word_row(tok, chunk_tokens), idx, PEER_N_KEYS * PEER_N_KEYS)
    gate_ref[...] = gatet_ref[...].T


def _router(x1, norm_ffn, wq, keys, tok0, T, after):
    D = x1.shape[1]
    tr = min(ROUTE_TILE, T)
    qw = wq.shape[1]
    assert tok0 % tr == 0 and T % tr == 0
    first = tok0 // tr
    full = lambda shape: pl.BlockSpec(shape, lambda i: (0,) * len(shape))
    return pl.pallas_call(
        functools.partial(_router_kernel, T),
        grid=(T // tr,),
        in_specs=[
            pl.BlockSpec((tr, D), lambda i: (first + i, 0)),
            full((1, D)),
            full((D, qw)),
            full((2, PEER_N_KEYS, PEER_D_HALF)),
            pl.BlockSpec(memory_space=pl.ANY),
        ],
        out_specs=[
            pl.BlockSpec((tr, D), lambda i: (i, 0)),
            pl.BlockSpec((tr, EXPERTS_PER_TOKEN), lambda i: (i, 0)),
            pl.BlockSpec((tr, EXPERTS_PER_TOKEN), lambda i: (i, 0)),
            pl.BlockSpec((tr, EXPERTS_PER_TOKEN), lambda i: (i, 0)),
        ],
        out_shape=[
            jax.ShapeDtypeStruct((T, D), jnp.bfloat16),
            jax.ShapeDtypeStruct((T, EXPERTS_PER_TOKEN), jnp.int32),
            jax.ShapeDtypeStruct((T, EXPERTS_PER_TOKEN), jnp.int32),
            jax.ShapeDtypeStruct((T, EXPERTS_PER_TOKEN), jnp.float32),
        ],
        scratch_shapes=[
            pltpu.VMEM((tr, qw), jnp.bfloat16),
            pltpu.VMEM((EXPERTS_PER_TOKEN, tr), jnp.int32),
            pltpu.VMEM((EXPERTS_PER_TOKEN, tr), jnp.float32),
        ],
        compiler_params=pltpu.CompilerParams(
            dimension_semantics=("parallel",), vmem_limit_bytes=TC_VMEM_LIMIT),
        name="router",
    )(x1, norm_ffn.reshape(1, D), wq, keys, after)


def _tile_linear_index(row, col, ncols):
    return ((row >> 3) * (ncols * V7X_SUBLANES) + (col >> 7) * (V7X_SUBLANES * V7X_LANES)
            + (row & (V7X_SUBLANES - 1)) * V7X_LANES + (col & (V7X_LANES - 1)))


def _score_tile(tokens):
    return min(SCORE_TOK_TILE, tokens)


def _score_word_row(tok, tokens):
    tm = _score_tile(tokens)
    assert tm & (tm - 1) == 0, "power-of-two score tile"
    return (tok >> int(math.log2(tm))) * (tm // 2) + (tok & (tm // 2 - 1))


def _scores_kernel(h_ref, ut_ref, o_ref):
    acc = jnp.dot(h_ref[...], ut_ref[...], preferred_element_type=jnp.float32)
    tm, tn = acc.shape
    bits = pltpu.bitcast(acc.astype(jnp.bfloat16).astype(jnp.float32), jnp.uint32)
    words = (bits[:tm // 2] >> 16) | (bits[tm // 2:] & jnp.uint32(HI_HALF))
    for n in range(tn // V7X_LANES):
        o_ref[:, n * V7X_SUBLANES:(n + 1) * V7X_SUBLANES, :] = words[:, n * V7X_LANES:(n + 1) * V7X_LANES].reshape(
            tm // 2 // V7X_SUBLANES, V7X_SUBLANES, V7X_LANES)


def _dense_scores(h2, u_t):
    T, D = h2.shape
    E = u_t.shape[1]
    tm, tn = _score_tile(T), min(SCORE_EXP_TILE, E)
    assert T % tm == 0 and E % tn == 0 and tm % (2 * V7X_SUBLANES) == 0 and tn % V7X_LANES == 0
    out = pl.pallas_call(
        _scores_kernel,
        grid=(T // tm, E // tn),
        in_specs=[pl.BlockSpec((tm, D), lambda i, j: (i, 0)), pl.BlockSpec((D, tn), lambda i, j: (0, j))],
        out_specs=pl.BlockSpec((tm // 2 // V7X_SUBLANES, tn // V7X_LANES * V7X_SUBLANES, V7X_LANES),
                               lambda i, j: (i, j, 0)),
        out_shape=jax.ShapeDtypeStruct((T // 2 // V7X_SUBLANES, E // V7X_LANES * V7X_SUBLANES, V7X_LANES),
                                       jnp.uint32),
        compiler_params=pltpu.CompilerParams(
            dimension_semantics=("parallel", "arbitrary"), vmem_limit_bytes=TC_VMEM_LIMIT),
        name="dense_scores",
    )(h2, u_t)
    return out.reshape(-1)


def _act_kernel(pre_ref, gate_ref, w_ref):
    w = (gate_ref[...] * _gelu(pre_ref[...])).astype(jnp.bfloat16).astype(jnp.float32)
    hi = pltpu.bitcast(w, jnp.uint32) & jnp.uint32(HI_HALF)
    w_ref[...] = pltpu.bitcast(hi | (hi >> 16), jnp.int32)


def _expert_weights(pre, gate):
    T, K = pre.shape
    te = min(EW_TILE, T)
    spec = pl.BlockSpec((te, K), lambda i: (i, 0))
    return pl.pallas_call(
        _act_kernel, grid=(T // te,), in_specs=[spec, spec], out_specs=spec,
        out_shape=jax.ShapeDtypeStruct((T, K), jnp.int32),
        compiler_params=pltpu.CompilerParams(dimension_semantics=("parallel",)),
        name="expert_weights",
    )(pre, gate)


def _final_kernel(x_ref, y_ref, g_ref, *rest):
    o_ref = rest[-1]
    o_ref[...] = _rms(x_ref[...] + y_ref[...], g_ref[...])


def _final(x1, x_tok0, peer, norm_final, out_prev, out_tok0, total):
    Tc, D = peer.shape
    te = min(EW_TILE, Tc)
    assert x_tok0 % te == 0 and out_tok0 % te == 0 and Tc % te == 0
    nblk = Tc // te
    spec = pl.BlockSpec((te, D), lambda i: (out_tok0 // te + i, 0))
    in_specs = [pl.BlockSpec((te, D), lambda i: (x_tok0 // te + i, 0)),
                pl.BlockSpec((te, D), lambda i: (i, 0)), pl.BlockSpec((1, D), lambda i: (0, 0))]
    args = [x1, peer, norm_final.reshape(1, D)]
    aliases = {}
    if out_prev is not None:
        in_specs.append(pl.BlockSpec(memory_space=pl.ANY))
        args.append(out_prev)
        aliases = {3: 0}
    return pl.pallas_call(
        _final_kernel, grid=(nblk,), in_specs=in_specs, out_specs=spec,
        out_shape=jax.ShapeDtypeStruct((total, D), jnp.float32),
        input_output_aliases=aliases,
        compiler_params=pltpu.CompilerParams(dimension_semantics=("parallel",)),
        name="final_norm",
    )(*args)


def _tree_sum(vals):
    while len(vals) > 1:
        nxt = [vals[i] + vals[i + 1] for i in range(0, len(vals) - 1, 2)]
        if len(vals) % 2:
            nxt.append(vals[-1])
        vals = nxt
    return vals[0]


def _sc_block_pipeline(nblk, items_per_token, loads, store, gather, compute):
    assert items_per_token % 2 == 0 and nblk >= 1

    for c in loads(0, 0):
        c.start()
    for c in loads(0, 0):
        c.wait()
    if nblk > 1:
        for c in loads(1, 1):
            c.start()
    gather(0, 0, 0, 0).start()

    @pl.loop(0, nblk)
    def _(b):
        slot = b % 2

        @pl.when(b >= 2)
        def _():
            store(b - 2, slot).wait()

        @pl.loop(0, SC_TOKENS)
        def _(t):
            for q in range(items_per_token):
                buf = q % 2
                if q + 1 < items_per_token:
                    gather(slot, t, q + 1, 1 - buf).start()
                else:
                    @pl.when(t + 1 < SC_TOKENS)
                    def _():
                        gather(slot, t + 1, 0, 1 - buf).start()

                    @pl.when(jnp.logical_and(t + 1 == SC_TOKENS, b + 1 < nblk))
                    def _():
                        for c in loads(b + 1, 1 - slot):
                            c.wait()
                        gather(1 - slot, 0, 0, 1 - buf).start()

                gather(slot, t, q, buf).wait()
                compute(slot, t, q, buf)

        store(b, slot).start()

        @pl.when(b + 2 < nblk)
        def _():
            for c in loads(b + 2, slot):
                c.start()

    if nblk >= 2:
        store(nblk - 2, nblk % 2).wait()
    store(nblk - 1, (nblk - 1) % 2).wait()


def _sc_mesh():
    return plsc.VectorSubcoreMesh(core_axis_name="c", subcore_axis_name="s")


def _sc_worker_id():
    return lax.axis_index("s") * V7X_SC_CORES + lax.axis_index("c")


def _sc_bf16(words):
    return plsc.bitcast(words, jnp.bfloat16)


def _sc_halves_f32(pairs):
    words = plsc.bitcast(pairs, jnp.uint32)
    return (plsc.bitcast(words << 16, jnp.float32),
            plsc.bitcast(words & jnp.uint32(HI_HALF), jnp.float32))


def _pick_scores(flat, slots, after):
    T, K = slots.shape
    L, nb = V7X_SC_LANES, SC_PICK_TOKENS
    tok_per_w = T // V7X_SC_WORKERS
    nblk = tok_per_w // nb
    half_tile = _score_tile(T) // 2
    assert T % (V7X_SC_WORKERS * nb) == 0 and half_tile % nb == 0 and half_tile & (half_tile - 1) == 0

    @functools.partial(
        pl.kernel, mesh=_sc_mesh(),
        out_type=jax.ShapeDtypeStruct((T, K), jnp.float32),
        compiler_params=pltpu.CompilerParams(needs_layout_passes=False),
        scratch_types=[
            pltpu.VMEM((2, nb, K), jnp.int32),
            pltpu.VMEM((2, nb, K), jnp.uint32),
            pltpu.VMEM((2, nb, K), jnp.float32),
            pltpu.SemaphoreType.DMA((2,)),
            pltpu.SemaphoreType.DMA((2,)),
            pltpu.SemaphoreType.DMA((2,)),
        ],
        name="pick_scores")
    def k(flat_hbm, slots_hbm, after_hbm, out_hbm, idx_v, word_v, val_v, idx_sems, out_sems, row_sems):
        del after_hbm
        base = _sc_worker_id() * tok_per_w

        def load(b, slot):
            return pltpu.make_async_copy(slots_hbm.at[pl.ds(base + b * nb, nb)], idx_v.at[slot], idx_sems.at[slot])

        def store(b, slot):
            return pltpu.make_async_copy(val_v.at[slot], out_hbm.at[pl.ds(base + b * nb, nb)], out_sems.at[slot])

        def gathers(slot):
            return [pltpu.make_async_copy(flat_hbm.at[idx_v.at[slot, t]], word_v.at[slot, t], row_sems.at[slot])
                    for t in range(nb)]

        def unpack(b, slot):
            low = ((base + b * nb) & half_tile) == 0
            shift = jnp.full((L,), jnp.where(low, 16, 0), jnp.uint32)
            for t in range(nb):
                for j in range(K // L):
                    words = word_v[slot, t, pl.ds(j * L, L)]
                    val_v[slot, t, pl.ds(j * L, L)] = plsc.bitcast(
                        (words << shift) & jnp.uint32(HI_HALF), jnp.float32)

        load(0, 0).start()

        @pl.loop(0, nblk)
        def _(b):
            slot = b % 2
            load(b, slot).wait()

            @pl.when(b + 1 < nblk)
            def _():
                load(b + 1, 1 - slot).start()

            @pl.when(b >= 2)
            def _():
                store(b - 2, slot).wait()

            for c in gathers(slot):
                c.start()
            for c in gathers(slot):
                c.wait()
            unpack(b, slot)
            store(b, slot).start()

        if nblk >= 2:
            store(nblk - 2, nblk % 2).wait()
        store(nblk - 1, (nblk - 1) % 2).wait()

    return k(flat, slots, after)


def _expert_mix(w, idx, table):
    T, K = w.shape
    DW = table.shape[1]
    D = 2 * DW
    L, G = V7X_SC_LANES, SC_GATHER
    nj = SC_MIX_CHUNK // L
    tok_per_w = T // V7X_SC_WORKERS
    assert T % (V7X_SC_WORKERS * SC_TOKENS) == 0 and K % (2 * G) == 0
    assert DW % SC_MIX_CHUNK == 0 and G % SC_MIX_GROUP == 0

    @functools.partial(
        pl.kernel, mesh=_sc_mesh(),
        out_type=jax.ShapeDtypeStruct((T, D), jnp.float32),
        compiler_params=pltpu.CompilerParams(needs_layout_passes=False),
        scratch_types=[
            pltpu.VMEM((2, SC_TOKENS, K), jnp.int32),
            pltpu.VMEM((2, SC_TOKENS, K), jnp.int32),
            pltpu.VMEM((2, G, DW), jnp.uint32),
            pltpu.VMEM((2, SC_TOKENS, D), jnp.float32),
            pltpu.SemaphoreType.DMA((2,)),
            pltpu.SemaphoreType.DMA((2,)),
            pltpu.SemaphoreType.DMA((2,)),
            pltpu.SemaphoreType.DMA((2,)),
        ],
        name="expert_mix")
    def k(w_hbm, idx_hbm, tab_hbm, out_hbm, idx_v, w_v, rows_v, out_v,
          idx_sems, w_sems, out_sems, row_sems):
        base = _sc_worker_id() * tok_per_w

        def loads(b, slot):
            toks = pl.ds(base + b * SC_TOKENS, SC_TOKENS)
            return [pltpu.make_async_copy(idx_hbm.at[toks], idx_v.at[slot], idx_sems.at[slot]),
                    pltpu.make_async_copy(w_hbm.at[toks], w_v.at[slot], w_sems.at[slot])]

        def store(b, slot):
            toks = pl.ds(base + b * SC_TOKENS, SC_TOKENS)
            return pltpu.make_async_copy(out_v.at[slot], out_hbm.at[toks], out_sems.at[slot])

        def gather(slot, t, q, buf):
            return pltpu.make_async_copy(
                tab_hbm.at[idx_v.at[slot, t, pl.ds(q * G, G)]], rows_v.at[buf], row_sems.at[buf])

        def compute(slot, t, q, buf):
            ssplat = jnp.full((L,), slot, jnp.int32)
            tsplat = jnp.full((L,), t, jnp.int32)
            for c in range(DW // SC_MIX_CHUNK):
                def body(kg, acc):
                    kk = kg * SC_MIX_GROUP
                    wks = [_sc_bf16(plsc.load_gather(
                        w_v, [ssplat, tsplat, jnp.full((L,), q * G + i, jnp.int32) + kk]))
                        for i in range(SC_MIX_GROUP)]
                    out = []
                    for j in range(nj):
                        prods = [wks[i] * _sc_bf16(rows_v[buf, kk + i, pl.ds(c * SC_MIX_CHUNK + j * L, L)])
                                 for i in range(SC_MIX_GROUP)]
                        lo, hi = _sc_halves_f32(_tree_sum(prods))
                        out += [acc[2 * j] + lo, acc[2 * j + 1] + hi]
                    return tuple(out)

                zero = jnp.zeros((L,), jnp.float32)
                acc = plsc.parallel_loop(0, G // SC_MIX_GROUP, carry=(zero,) * (2 * nj))(body)
                for j in range(nj):
                    for half in range(2):
                        dst = out_v.at[slot, t, pl.ds(half * DW + c * SC_MIX_CHUNK + j * L, L)]
                        if q == 0:
                            dst[...] = acc[2 * j + half]
                        else:
                            plsc.addupdate(dst, acc[2 * j + half])

        _sc_block_pipeline(tok_per_w // SC_TOKENS, K // G, loads, store, gather, compute)

    return k(w, idx, table)


def _chunk_sizes(total):
    ramp, size = [EDGE_CHUNK], EDGE_CHUNK
    while size < MAX_CHUNK:
        ramp.append(size)
        size *= 2
    middle = total - 2 * sum(ramp)
    assert middle >= 0 and middle % MAX_CHUNK == 0
    return ramp + [MAX_CHUNK] * (middle // MAX_CHUNK) + ramp[::-1]


def kernel(x, norm_mix, w_in, pool_w, pool_scale, sgu_ln_g, sgu_ln_b, sgu_w, sgu_b, out_norm_pool,
           out_norm_sgu, w_out, norm_ffn, peer_wq, peer_keys, peer_u, peer_v, norm_final):
    B, S, D = x.shape
    assert norm_mix.shape[0] == 1, "single-layer block"
    T = B * S
    mix_args = (norm_mix[0], w_in[0], pool_w[0], pool_scale[0], sgu_ln_g[0], sgu_ln_b[0],
                sgu_w[0], sgu_b[0], out_norm_pool[0], out_norm_sgu[0], w_out[0])
    x1_parts = [(b * S, _mixer(x, b, 1, *mix_args).reshape(S, D)) for b in range(B)]
    wq = peer_wq[0].astype(jnp.bfloat16)
    keys = peer_keys[0].astype(jnp.bfloat16)
    u_t = peer_u[0].astype(jnp.bfloat16).T
    v_tab = _pack_table(peer_v[0])
    out = None
    tok0 = 0
    ws = [norm_ffn, v_tab]
    peers = [norm_ffn, norm_ffn]
    for tc in _chunk_sizes(T):
        part0, x1 = [p for p in x1_parts if p[0] <= tok0][-1]
        assert tok0 + tc <= part0 + x1.shape[0], "a token chunk must lie inside one mixer call"
        h2, idx, slots, gate = _router(x1, norm_ffn[0], wq, keys, tok0 - part0, tc, ws[-2])
        pre = _pick_scores(_dense_scores(h2, u_t), slots, peers[-2])
        w = _expert_weights(pre, gate)
        ws.append(w)
        peer = _expert_mix(w, idx, v_tab)
        peers.append(peer)
        out = _final(x1, tok0 - part0, peer, norm_final, out, tok0, T)
        tok0 += tc
    return out.reshape(B, S, D)
```

```python
import functools
import math

import jax
import jax.numpy as jnp
from jax import lax
from jax.experimental import pallas as pl
from jax.experimental.pallas import tpu as pltpu
from jax.experimental.pallas import tpu_sc as plsc

POOL_WINDOWS = (2, 4, 8, 16)
N_POOL_GROUPS = len(POOL_WINDOWS)
SGU_HEADS = 4
SGU_CHUNK = 128
PEER_HEADS = 8
PEER_N_KEYS = 128
PEER_D_HALF = 128
PEER_TOPK = 16
NORM_EPS = 1e-6
EXPERTS_PER_TOKEN = PEER_HEADS * PEER_TOPK

V7X_LANES = 128
V7X_SUBLANES = 8
V7X_SC_CORES = 2
V7X_SC_SUBCORES = 16
V7X_SC_LANES = 16
V7X_SC_WORKERS = V7X_SC_CORES * V7X_SC_SUBCORES

HALO = max(POOL_WINDOWS)
MIX_TILE = 512
ROUTE_TILE = 1024
EW_TILE = 512
SCORE_TOK_TILE = 2048
SCORE_EXP_TILE = 1024
SC_PICK_TOKENS = 16
SC_GATHER = 64
SC_TOKENS = 8
SC_MIX_CHUNK = 128
SC_MIX_GROUP = 4
SC_UNROLL = 2
HI_HALF = 0xFFFF0000
TC_VMEM_LIMIT = 48 * 1024 * 1024
EDGE_CHUNK = 512
MAX_CHUNK = 4096


def _rms(x, g):
    inv = lax.rsqrt(jnp.mean(x * x, axis=-1, keepdims=True) + NORM_EPS)
    return x * inv * g


def _pack_halves(bits):
    half = bits.shape[1] // 2
    return (bits[:, :half] >> 16) | (bits[:, half:] & jnp.uint32(HI_HALF))


def _pack_table(a):
    return _pack_halves(lax.bitcast_convert_type(a.astype(jnp.bfloat16).astype(jnp.float32), jnp.uint32))


def _gelu(x):
    return 0.5 * x * (1.0 + lax.erf(x * math.sqrt(0.5)))


def _mixer_kernel(x_ref, xh_ref, nmix_ref, win_ref, poolw_ref, pscale_ref, lng_ref, lnb_ref,
                  sguw_ref, sgub_ref, onp_ref, ons_ref, wout_ref, o_ref, pext_ref, mix_ref):
    i = pl.program_id(1)
    ts = x_ref.shape[1]
    pool_w = pscale_ref.shape[1]
    gdim = pool_w // N_POOL_GROUPS
    sgu_w = lng_ref.shape[1]
    hdim = sgu_w // SGU_HEADS

    x = x_ref[0]
    h = _rms(x, nmix_ref[...]).astype(jnp.bfloat16)
    z = jnp.dot(h, win_ref[...], preferred_element_type=jnp.float32)
    p = z[:, :pool_w]

    hh = _rms(xh_ref[0], nmix_ref[...]).astype(jnp.bfloat16)
    ph = jnp.dot(hh, win_ref[:, :pool_w], preferred_element_type=jnp.float32)
    ph = jnp.where(i > 0, ph, 0.0)
    pext_ref[0:HALO, :] = ph
    pext_ref[HALO:HALO + ts, :] = p

    pos = i * ts + lax.broadcasted_iota(jnp.int32, (ts, 1), 0)
    ssq = jnp.zeros((ts, 1), jnp.float32)
    a_parts = []
    for g, win in enumerate(POOL_WINDOWS):
        cols = slice(g * gdim, (g + 1) * gdim)
        s = pext_ref[HALO:HALO + ts, cols]
        for j in range(1, win):
            s = s + pext_ref[HALO - j:HALO - j + ts, cols]
        cnt = jnp.minimum(pos + 1, win).astype(jnp.float32)
        d = (s / cnt - p[:, cols]).astype(jnp.bfloat16)
        a = jnp.dot(d, poolw_ref[g], preferred_element_type=jnp.float32) * pscale_ref[:, cols]
        ssq = ssq + jnp.sum(a * a, axis=-1, keepdims=True)
        a_parts.append(a)
    inv_a = lax.rsqrt(ssq / pool_w + NORM_EPS)
    for g in range(N_POOL_GROUPS):
        cols = slice(g * gdim, (g + 1) * gdim)
        mix_ref[:, cols] = (a_parts[g] * inv_a * onp_ref[:, cols]).astype(jnp.bfloat16)

    gz = _gelu(z[:, pool_w:])
    tril = (lax.broadcasted_iota(jnp.int32, (SGU_CHUNK, SGU_CHUNK), 0)
            >= lax.broadcasted_iota(jnp.int32, (SGU_CHUNK, SGU_CHUNK), 1))
    ssq = jnp.zeros((ts, 1), jnp.float32)
    b_parts = []
    for hd in range(SGU_HEADS):
        cols = slice(hd * hdim, (hd + 1) * hdim)
        u = gz[:, hd * hdim:(hd + 1) * hdim]
        v = gz[:, sgu_w + hd * hdim:sgu_w + (hd + 1) * hdim]
        mu = jnp.mean(v, axis=-1, keepdims=True)
        vc = v - mu
        var = jnp.mean(vc * vc, axis=-1, keepdims=True)
        vn = (vc * lax.rsqrt(var + NORM_EPS) * lng_ref[:, cols] + lnb_ref[:, cols]).astype(jnp.bfloat16)
        w = jnp.where(tril, sguw_ref[hd], jnp.zeros((), sguw_ref.dtype))
        mixed = [jnp.dot(w, vn[n * SGU_CHUNK:(n + 1) * SGU_CHUNK], preferred_element_type=jnp.float32)
                 + sgub_ref[hd] for n in range(ts // SGU_CHUNK)]
        b = u * jnp.concatenate(mixed, axis=0)
        ssq = ssq + jnp.sum(b * b, axis=-1, keepdims=True)
        b_parts.append(b)
    inv_b = lax.rsqrt(ssq / sgu_w + NORM_EPS)
    for hd in range(SGU_HEADS):
        cols = slice(hd * hdim, (hd + 1) * hdim)
        mix_ref[:, pool_w + hd * hdim:pool_w + (hd + 1) * hdim] = (
            b_parts[hd] * inv_b * ons_ref[:, cols]).astype(jnp.bfloat16)

    o_ref[0] = x + jnp.dot(mix_ref[...], wout_ref[...], preferred_element_type=jnp.float32)


def _mixer(x, b0, nb, norm_mix, w_in, pool_w, pool_scale, ln_g, ln_b, sgu_w, sgu_b, on_pool, on_sgu, w_out):
    _, S, D = x.shape
    ts = min(MIX_TILE, S)
    pool_width = pool_scale.size
    sgu_width = ln_g.size
    in_width = w_in.shape[1]
    gdim = pool_width // N_POOL_GROUPS
    halo_blocks = ts // HALO
    full = lambda shape: pl.BlockSpec(shape, lambda b, i: (0,) * len(shape))
    return pl.pallas_call(
        _mixer_kernel,
        grid=(nb, S // ts),
        in_specs=[
            pl.BlockSpec((1, ts, D), lambda b, i: (b0 + b, i, 0)),
            pl.BlockSpec((1, HALO, D), lambda b, i: (b0 + b, jnp.maximum(i * halo_blocks - 1, 0), 0)),
            full((1, D)),
            full((D, in_width)),
            full((N_POOL_GROUPS, gdim, gdim)),
            full((1, pool_width)),
            full((1, sgu_width)),
            full((1, sgu_width)),
            full((SGU_HEADS, SGU_CHUNK, SGU_CHUNK)),
            full((SGU_HEADS, SGU_CHUNK, SGU_CHUNK)),
            full((1, pool_width)),
            full((1, sgu_width)),
            full((pool_width + sgu_width, D)),
        ],
        out_specs=pl.BlockSpec((1, ts, D), lambda b, i: (b, i, 0)),
        out_shape=jax.ShapeDtypeStruct((nb, S, D), jnp.float32),
        scratch_shapes=[
            pltpu.VMEM((HALO + ts, pool_width), jnp.float32),
            pltpu.VMEM((ts, pool_width + sgu_width), jnp.bfloat16),
        ],
        compiler_params=pltpu.CompilerParams(
            dimension_semantics=("parallel", "arbitrary"), vmem_limit_bytes=TC_VMEM_LIMIT),
        name="mixer",
    )(x, x, norm_mix.reshape(1, D), w_in.astype(jnp.bfloat16), pool_w.astype(jnp.bfloat16),
      pool_scale.reshape(1, pool_width), ln_g.reshape(1, sgu_width), ln_b.reshape(1, sgu_width),
      sgu_w.astype(jnp.bfloat16),
      jnp.broadcast_to(sgu_b[:, :, None], (SGU_HEADS, SGU_CHUNK, SGU_CHUNK)),
      on_pool.reshape(1, pool_width), on_sgu.reshape(1, sgu_width), w_out.astype(jnp.bfloat16))


def _topk_rows(s, k):
    n = s.shape[0]
    iota = lax.broadcasted_iota(jnp.int32, s.shape, 0)
    vals, idxs = [], []
    for _ in range(k):
        m = jnp.max(s, axis=0, keepdims=True)
        ix = jnp.min(jnp.where(s == m, iota, n), axis=0, keepdims=True)
        vals.append(m)
        idxs.append(ix)
        s = jnp.where(iota == ix, -jnp.inf, s)
    return vals, idxs


def _pair_candidates(v1, i1, v2, i2):
    k = PEER_TOPK
    v2c, i2c = jnp.concatenate(v2, axis=0), jnp.concatenate(i2, axis=0)
    m = v2c.shape[1]
    vals, experts, flats = [], [], []
    a = 0
    while k // (a + 1) > 1:
        rows = -(-(k // (a + 1)) // V7X_SUBLANES) * V7X_SUBLANES
        vals.append(v1[a] + v2c[:rows])
        experts.append(i1[a] * PEER_N_KEYS + i2c[:rows])
        flats.append(a * k + lax.broadcasted_iota(jnp.int32, (rows, m), 0))
        a += 1
    vals.append(jnp.concatenate(v1[a:], axis=0) + v2[0])
    experts.append(jnp.concatenate(i1[a:], axis=0) * PEER_N_KEYS + i2[0])
    flats.append((a + lax.broadcasted_iota(jnp.int32, (k - a, m), 0)) * k)
    return jnp.concatenate(vals, axis=0), jnp.concatenate(experts, axis=0), jnp.concatenate(flats, axis=0)


def _router_kernel(chunk_tokens, x_ref, nffn_ref, wq_ref, keys_ref, after_ref, h2_ref, idx_ref, slot_ref,
                   gate_ref, q_ref, idxt_ref, gatet_ref):
    del after_ref
    h2 = _rms(x_ref[...], nffn_ref[...]).astype(jnp.bfloat16)
    h2_ref[...] = h2
    q_ref[...] = jnp.dot(h2, wq_ref[...], preferred_element_type=jnp.float32).astype(jnp.bfloat16)
    dq = 2 * PEER_D_HALF
    nt = (((1,), (1,)), ((), ()))

    def head(hd, carry):
        off = pl.multiple_of(hd * dq, dq)
        s1 = lax.dot_general(keys_ref[0], q_ref[:, pl.ds(off, PEER_D_HALF)], nt,
                             preferred_element_type=jnp.float32)
        s2 = lax.dot_general(keys_ref[1], q_ref[:, pl.ds(off + PEER_D_HALF, PEER_D_HALF)], nt,
                             preferred_element_type=jnp.float32)
        v1, i1 = _topk_rows(s1, PEER_TOPK)
        v2, i2 = _topk_rows(s2, PEER_TOPK)
        cand, expert, flat = _pair_candidates(v1, i1, v2, i2)
        cv, ce = [], []
        for _ in range(PEER_TOPK):
            m = jnp.max(cand, axis=0, keepdims=True)
            ix = jnp.min(jnp.where(cand == m, flat, PEER_TOPK * PEER_TOPK), axis=0, keepdims=True)
            hit = flat == ix
            cv.append(m)
            ce.append(jnp.max(jnp.where(hit, expert, -1), axis=0, keepdims=True))
            cand = jnp.where(hit, -jnp.inf, cand)
        cvc = jnp.concatenate(cv, axis=0)
        e = jnp.exp(cvc - cv[0])
        gate = e / jnp.sum(e, axis=0, keepdims=True)
        row = pl.multiple_of(hd * PEER_TOPK, PEER_TOPK)
        idxt_ref[pl.ds(row, PEER_TOPK), :] = jnp.concatenate(ce, axis=0)
        gatet_ref[pl.ds(row, PEER_TOPK), :] = gate
        return carry

    lax.fori_loop(0, PEER_HEADS, head, 0)
    idx = idxt_ref[...].T
    idx_ref[...] = idx
    tok = pl.program_id(0) * idx.shape[0] + lax.broadcasted_iota(jnp.int32, idx.shape, 0)
    slot_ref[...] = _tile_linear_index(_score_word_row(tok, chunk_tokens), idx, PEER_N_KEYS * PEER_N_KEYS)
    gate_ref[...] = gatet_ref[...].T


def _router(x1, norm_ffn, wq, keys, tok0, T, after):
    D = x1.shape[1]
    tr = min(ROUTE_TILE, T)
    qw = wq.shape[1]
    assert tok0 % tr == 0 and T % tr == 0
    first = tok0 // tr
    full = lambda shape: pl.BlockSpec(shape, lambda i: (0,) * len(shape))
    return pl.pallas_call(
        functools.partial(_router_kernel, T),
        grid=(T // tr,),
        in_specs=[
            pl.BlockSpec((tr, D), lambda i: (first + i, 0)),
            full((1, D)),
            full((D, qw)),
            full((2, PEER_N_KEYS, PEER_D_HALF)),
            pl.BlockSpec(memory_space=pl.ANY),
        ],
        out_specs=[
            pl.BlockSpec((tr, D), lambda i: (i, 0)),
            pl.BlockSpec((tr, EXPERTS_PER_TOKEN), lambda i: (i, 0)),
            pl.BlockSpec((tr, EXPERTS_PER_TOKEN), lambda i: (i, 0)),
            pl.BlockSpec((tr, EXPERTS_PER_TOKEN), lambda i: (i, 0)),
        ],
        out_shape=[
            jax.ShapeDtypeStruct((T, D), jnp.bfloat16),
            jax.ShapeDtypeStruct((T, EXPERTS_PER_TOKEN), jnp.int32),
            jax.ShapeDtypeStruct((T, EXPERTS_PER_TOKEN), jnp.int32),
            jax.ShapeDtypeStruct((T, EXPERTS_PER_TOKEN), jnp.float32),
        ],
        scratch_shapes=[
            pltpu.VMEM((tr, qw), jnp.bfloat16),
            pltpu.VMEM((EXPERTS_PER_TOKEN, tr), jnp.int32),
            pltpu.VMEM((EXPERTS_PER_TOKEN, tr), jnp.float32),
        ],
        compiler_params=pltpu.CompilerParams(
            dimension_semantics=("parallel",), vmem_limit_bytes=TC_VMEM_LIMIT),
        name="router",
    )(x1, norm_ffn.reshape(1, D), wq, keys, after)


def _tile_linear_index(row, col, ncols):
    return ((row >> 3) * (ncols * V7X_SUBLANES) + (col >> 7) * (V7X_SUBLANES * V7X_LANES)
            + (row & (V7X_SUBLANES - 1)) * V7X_LANES + (col & (V7X_LANES - 1)))


def _score_tile(tokens):
    return min(SCORE_TOK_TILE, tokens)


def _score_word_row(tok, tokens):
    tm = _score_tile(tokens)
    assert tm & (tm - 1) == 0, "power-of-two score tile"
    return (tok >> int(math.log2(tm))) * (tm // 2) + (tok & (tm // 2 - 1))


def _scores_kernel(h_ref, ut_ref, o_ref):
    acc = jnp.dot(h_ref[...], ut_ref[...], preferred_element_type=jnp.float32)
    tm, tn = acc.shape
    bits = pltpu.bitcast(acc.astype(jnp.bfloat16).astype(jnp.float32), jnp.uint32)
    words = (bits[:tm // 2] >> 16) | (bits[tm // 2:] & jnp.uint32(HI_HALF))
    for n in range(tn // V7X_LANES):
        o_ref[:, n * V7X_SUBLANES:(n + 1) * V7X_SUBLANES, :] = words[:, n * V7X_LANES:(n + 1) * V7X_LANES].reshape(
            tm // 2 // V7X_SUBLANES, V7X_SUBLANES, V7X_LANES)


def _dense_scores(h2, u_t):
    T, D = h2.shape
    E = u_t.shape[1]
    tm, tn = _score_tile(T), min(SCORE_EXP_TILE, E)
    assert T % tm == 0 and E % tn == 0 and tm % (2 * V7X_SUBLANES) == 0 and tn % V7X_LANES == 0
    out = pl.pallas_call(
        _scores_kernel,
        grid=(T // tm, E // tn),
        in_specs=[pl.BlockSpec((tm, D), lambda i, j: (i, 0)), pl.BlockSpec((D, tn), lambda i, j: (0, j))],
        out_specs=pl.BlockSpec((tm // 2 // V7X_SUBLANES, tn // V7X_LANES * V7X_SUBLANES, V7X_LANES),
                               lambda i, j: (i, j, 0)),
        out_shape=jax.ShapeDtypeStruct((T // 2 // V7X_SUBLANES, E // V7X_LANES * V7X_SUBLANES, V7X_LANES),
                                       jnp.uint32),
        compiler_params=pltpu.CompilerParams(
            dimension_semantics=("parallel", "arbitrary"), vmem_limit_bytes=TC_VMEM_LIMIT),
        name="dense_scores",
    )(h2, u_t)
    return out.reshape(-1)


def _act_kernel(pre_ref, gate_ref, w_ref):
    w = (gate_ref[...] * _gelu(pre_ref[...])).astype(jnp.bfloat16).astype(jnp.float32)
    hi = pltpu.bitcast(w, jnp.uint32) & jnp.uint32(HI_HALF)
    w_ref[...] = pltpu.bitcast(hi | (hi >> 16), jnp.int32)


def _expert_weights(pre, gate):
    T, K = pre.shape
    te = min(EW_TILE, T)
    spec = pl.BlockSpec((te, K), lambda i: (i, 0))
    return pl.pallas_call(
        _act_kernel, grid=(T // te,), in_specs=[spec, spec], out_specs=spec,
        out_shape=jax.ShapeDtypeStruct((T, K), jnp.int32),
        compiler_params=pltpu.CompilerParams(dimension_semantics=("parallel",)),
        name="expert_weights",
    )(pre, gate)


def _final_kernel(x_ref, y_ref, g_ref, *rest):
    o_ref = rest[-1]
    o_ref[...] = _rms(x_ref[...] + y_ref[...], g_ref[...])


def _final(x1, x_tok0, peer, norm_final, out_prev, out_tok0, total):
    Tc, D = peer.shape
    te = min(EW_TILE, Tc)
    assert x_tok0 % te == 0 and out_tok0 % te == 0 and Tc % te == 0
    nblk = Tc // te
    spec = pl.BlockSpec((te, D), lambda i: (out_tok0 // te + i, 0))
    in_specs = [pl.BlockSpec((te, D), lambda i: (x_tok0 // te + i, 0)),
                pl.BlockSpec((te, D), lambda i: (i, 0)), pl.BlockSpec((1, D), lambda i: (0, 0))]
    args = [x1, peer, norm_final.reshape(1, D)]
    aliases = {}
    if out_prev is not None:
        in_specs.append(pl.BlockSpec(memory_space=pl.ANY))
        args.append(out_prev)
        aliases = {3: 0}
    return pl.pallas_call(
        _final_kernel, grid=(nblk,), in_specs=in_specs, out_specs=spec,
        out_shape=jax.ShapeDtypeStruct((total, D), jnp.float32),
        input_output_aliases=aliases,
        compiler_params=pltpu.CompilerParams(dimension_semantics=("parallel",)),
        name="final_norm",
    )(*args)


def _tree_sum(vals):
    while len(vals) > 1:
        nxt = [vals[i] + vals[i + 1] for i in range(0, len(vals) - 1, 2)]
        if len(vals) % 2:
            nxt.append(vals[-1])
        vals = nxt
    return vals[0]


def _sc_block_pipeline(nblk, items_per_token, loads, store, gather, compute):
    assert items_per_token % 2 == 0 and nblk >= 1

    for c in loads(0, 0):
        c.start()
    for c in loads(0, 0):
        c.wait()
    if nblk > 1:
        for c in loads(1, 1):
            c.start()
    gather(0, 0, 0, 0).start()

    @pl.loop(0, nblk)
    def _(b):
        slot = b % 2

        @pl.when(b >= 2)
        def _():
            store(b - 2, slot).wait()

        @pl.loop(0, SC_TOKENS)
        def _(t):
            for q in range(items_per_token):
                buf = q % 2
                if q + 1 < items_per_token:
                    gather(slot, t, q + 1, 1 - buf).start()
                else:
                    @pl.when(t + 1 < SC_TOKENS)
                    def _():
                        gather(slot, t + 1, 0, 1 - buf).start()

                    @pl.when(jnp.logical_and(t + 1 == SC_TOKENS, b + 1 < nblk))
                    def _():
                        for c in loads(b + 1, 1 - slot):
                            c.wait()
                        gather(1 - slot, 0, 0, 1 - buf).start()

                gather(slot, t, q, buf).wait()
                compute(slot, t, q, buf)

        store(b, slot).start()

        @pl.when(b + 2 < nblk)
        def _():
            for c in loads(b + 2, slot):
                c.start()

    if nblk >= 2:
        store(nblk - 2, nblk % 2).wait()
    store(nblk - 1, (nblk - 1) % 2).wait()


def _sc_mesh():
    return plsc.VectorSubcoreMesh(core_axis_name="c", subcore_axis_name="s")


def _sc_worker_id():
    return lax.axis_index("s") * V7X_SC_CORES + lax.axis_index("c")


def _sc_bf16(words):
    return plsc.bitcast(words, jnp.bfloat16)


def _sc_halves_f32(pairs):
    words = plsc.bitcast(pairs, jnp.uint32)
    return (plsc.bitcast(words << 16, jnp.float32),
            plsc.bitcast(words & jnp.uint32(HI_HALF), jnp.float32))


def _pick_scores(flat, slots, after):
    T, K = slots.shape
    L, nb = V7X_SC_LANES, SC_PICK_TOKENS
    tok_per_w = T // V7X_SC_WORKERS
    nblk = tok_per_w // nb
    half_tile = _score_tile(T) // 2
    assert T % (V7X_SC_WORKERS * nb) == 0 and half_tile % nb == 0 and half_tile & (half_tile - 1) == 0

    @functools.partial(
        pl.kernel, mesh=_sc_mesh(),
        out_type=jax.ShapeDtypeStruct((T, K), jnp.float32),
        compiler_params=pltpu.CompilerParams(needs_layout_passes=False),
        scratch_types=[
            pltpu.VMEM((2, nb, K), jnp.int32),
            pltpu.VMEM((2, nb, K), jnp.uint32),
            pltpu.VMEM((2, nb, K), jnp.float32),
            pltpu.SemaphoreType.DMA((2,)),
            pltpu.SemaphoreType.DMA((2,)),
            pltpu.SemaphoreType.DMA((2,)),
        ],
        name="pick_scores")
    def k(flat_hbm, slots_hbm, after_hbm, out_hbm, idx_v, word_v, val_v, idx_sems, out_sems, row_sems):
        del after_hbm
        base = _sc_worker_id() * tok_per_w

        def load(b, slot):
            return pltpu.make_async_copy(slots_hbm.at[pl.ds(base + b * nb, nb)], idx_v.at[slot], idx_sems.at[slot])

        def store(b, slot):
            return pltpu.make_async_copy(val_v.at[slot], out_hbm.at[pl.ds(base + b * nb, nb)], out_sems.at[slot])

        def gathers(slot):
            return [pltpu.make_async_copy(flat_hbm.at[idx_v.at[slot, t]], word_v.at[slot, t], row_sems.at[slot])
                    for t in range(nb)]

        def unpack(b, slot):
            low = ((base + b * nb) & half_tile) == 0
            shift = jnp.full((L,), jnp.where(low, 16, 0), jnp.uint32)
            for t in range(nb):
                for j in range(K // L):
                    words = word_v[slot, t, pl.ds(j * L, L)]
                    val_v[slot, t, pl.ds(j * L, L)] = plsc.bitcast(
                        (words << shift) & jnp.uint32(HI_HALF), jnp.float32)

        load(0, 0).start()

        @pl.loop(0, nblk)
        def _(b):
            slot = b % 2
            load(b, slot).wait()

            @pl.when(b + 1 < nblk)
            def _():
                load(b + 1, 1 - slot).start()

            @pl.when(b >= 2)
            def _():
                store(b - 2, slot).wait()

            for c in gathers(slot):
                c.start()
            for c in gathers(slot):
                c.wait()
            unpack(b, slot)
            store(b, slot).start()

        if nblk >= 2:
            store(nblk - 2, nblk % 2).wait()
        store(nblk - 1, (nblk - 1) % 2).wait()

    return k(flat, slots, after)


def _expert_mix(w, idx, table):
    T, K = w.shape
    DW = table.shape[1]
    D = 2 * DW
    L, G = V7X_SC_LANES, SC_GATHER
    nj = SC_MIX_CHUNK // L
    tok_per_w = T // V7X_SC_WORKERS
    assert T % (V7X_SC_WORKERS * SC_TOKENS) == 0 and K % (2 * G) == 0
    assert DW % SC_MIX_CHUNK == 0 and G % SC_MIX_GROUP == 0

    @functools.partial(
        pl.kernel, mesh=_sc_mesh(),
        out_type=jax.ShapeDtypeStruct((T, D), jnp.float32),
        compiler_params=pltpu.CompilerParams(needs_layout_passes=False),
        scratch_types=[
            pltpu.VMEM((2, SC_TOKENS, K), jnp.int32),
            pltpu.VMEM((2, SC_TOKENS, K), jnp.int32),
            pltpu.VMEM((2, G, DW), jnp.uint32),
            pltpu.VMEM((2, SC_TOKENS, D), jnp.float32),
            pltpu.SemaphoreType.DMA((2,)),
            pltpu.SemaphoreType.DMA((2,)),
            pltpu.SemaphoreType.DMA((2,)),
            pltpu.SemaphoreType.DMA((2,)),
        ],
        name="expert_mix")
    def k(w_hbm, idx_hbm, tab_hbm, out_hbm, idx_v, w_v, rows_v, out_v,
          idx_sems, w_sems, out_sems, row_sems):
        base = _sc_worker_id() * tok_per_w

        def loads(b, slot):
            toks = pl.ds(base + b * SC_TOKENS, SC_TOKENS)
            return [pltpu.make_async_copy(idx_hbm.at[toks], idx_v.at[slot], idx_sems.at[slot]),
                    pltpu.make_async_copy(w_hbm.at[toks], w_v.at[slot], w_sems.at[slot])]

        def store(b, slot):
            toks = pl.ds(base + b * SC_TOKENS, SC_TOKENS)
            return pltpu.make_async_copy(out_v.at[slot], out_hbm.at[toks], out_sems.at[slot])

        def gather(slot, t, q, buf):
            return pltpu.make_async_copy(
                tab_hbm.at[idx_v.at[slot, t, pl.ds(q * G, G)]], rows_v.at[buf], row_sems.at[buf])

        def compute(slot, t, q, buf):
            ssplat = jnp.full((L,), slot, jnp.int32)
            tsplat = jnp.full((L,), t, jnp.int32)
            for c in range(DW // SC_MIX_CHUNK):
                def body(kg, acc):
                    kk = kg * SC_MIX_GROUP
                    wks = [_sc_bf16(plsc.load_gather(
                        w_v, [ssplat, tsplat, jnp.full((L,), q * G + i, jnp.int32) + kk]))
                        for i in range(SC_MIX_GROUP)]
                    out = []
                    for j in range(nj):
                        prods = [wks[i] * _sc_bf16(rows_v[buf, kk + i, pl.ds(c * SC_MIX_CHUNK + j * L, L)])
                                 for i in range(SC_MIX_GROUP)]
                        lo, hi = _sc_halves_f32(_tree_sum(prods))
                        out += [acc[2 * j] + lo, acc[2 * j + 1] + hi]
                    return tuple(out)

                zero = jnp.zeros((L,), jnp.float32)
                acc = plsc.parallel_loop(0, G // SC_MIX_GROUP, carry=(zero,) * (2 * nj))(body)
                for j in range(nj):
                    for half in range(2):
                        dst = out_v.at[slot, t, pl.ds(half * DW + c * SC_MIX_CHUNK + j * L, L)]
                        if q == 0:
                            dst[...] = acc[2 * j + half]
                        else:
                            plsc.addupdate(dst, acc[2 * j + half])

        _sc_block_pipeline(tok_per_w // SC_TOKENS, K // G, loads, store, gather, compute)

    return k(w, idx, table)


def _chunk_sizes(total):
    ramp, size = [EDGE_CHUNK], EDGE_CHUNK
    while size < MAX_CHUNK:
        ramp.append(size)
        size *= 2
    middle = total - 2 * sum(ramp)
    assert middle >= 0 and middle % MAX_CHUNK == 0
    return ramp + [MAX_CHUNK] * (middle // MAX_CHUNK) + ramp[::-1]


def kernel(x, norm_mix, w_in, pool_w, pool_scale, sgu_ln_g, sgu_ln_b, sgu_w, sgu_b, out_norm_pool,
           out_norm_sgu, w_out, norm_ffn, peer_wq, peer_keys, peer_u, peer_v, norm_final):
    B, S, D = x.shape
    assert norm_mix.shape[0] == 1, "single-layer block"
    T = B * S
    mix_args = (norm_mix[0], w_in[0], pool_w[0], pool_scale[0], sgu_ln_g[0], sgu_ln_b[0],
                sgu_w[0], sgu_b[0], out_norm_pool[0], out_norm_sgu[0], w_out[0])
    x1_parts = [(b * S, _mixer(x, b, 1, *mix_args).reshape(S, D)) for b in range(B)]
    wq = peer_wq[0].astype(jnp.bfloat16)
    keys = peer_keys[0].astype(jnp.bfloat16)
    u_t = peer_u[0].astype(jnp.bfloat16).T
    v_tab = _pack_table(peer_v[0])
    out = None
    tok0 = 0
    ws = [norm_ffn, v_tab]
    peers = [norm_ffn, norm_ffn]
    for tc in _chunk_sizes(T):
        part0, x1 = [p for p in x1_parts if p[0] <= tok0][-1]
        assert tok0 + tc <= part0 + x1.shape[0], "a token chunk must lie inside one mixer call"
        h2, idx, slots, gate = _router(x1, norm_ffn[0], wq, keys, tok0 - part0, tc, ws[-2])
        pre = _pick_scores(_dense_scores(h2, u_t), slots, peers[-2])
        w = _expert_weights(pre, gate)
        ws.append(w)
        peer = _expert_mix(w, idx, v_tab)
        peers.append(peer)
        out = _final(x1, tok0 - part0, peer, norm_final, out, tok0, T)
        tok0 += tc
    return out.reshape(B, S, D)
```

```python
import functools
import math

import jax
import jax.numpy as jnp
from jax import lax
from jax.experimental import pallas as pl
from jax.experimental.pallas import tpu as pltpu
from jax.experimental.pallas import tpu_sc as plsc

POOL_WINDOWS = (2, 4, 8, 16)
N_POOL_GROUPS = len(POOL_WINDOWS)
SGU_HEADS = 4
SGU_CHUNK = 128
PEER_HEADS = 8
PEER_N_KEYS = 128
PEER_D_HALF = 128
PEER_TOPK = 16
NORM_EPS = 1e-6
EXPERTS_PER_TOKEN = PEER_HEADS * PEER_TOPK

V7X_LANES = 128
V7X_SUBLANES = 8
V7X_SC_CORES = 2
V7X_SC_SUBCORES = 16
V7X_SC_LANES = 16
V7X_SC_WORKERS = V7X_SC_CORES * V7X_SC_SUBCORES

HALO = max(POOL_WINDOWS)
MIX_TILE = 512
ROUTE_TILE = 1024
EW_TILE = 512
SCORE_TOK_TILE = 2048
SCORE_EXP_TILE = 1024
SC_PICK_TOKENS = 16
SC_GATHER = 64
SC_TOKENS = 8
SC_MIX_CHUNK = 128
SC_MIX_GROUP = 4
SC_UNROLL = 2
HI_HALF = 0xFFFF0000
TC_VMEM_LIMIT = 48 * 1024 * 1024
EDGE_CHUNK = 512
MAX_CHUNK = 1024


def _rms(x, g):
    inv = lax.rsqrt(jnp.mean(x * x, axis=-1, keepdims=True) + NORM_EPS)
    return x * inv * g


def _pack_halves(bits):
    half = bits.shape[1] // 2
    return (bits[:, :half] >> 16) | (bits[:, half:] & jnp.uint32(HI_HALF))


def _pack_table(a):
    return _pack_halves(lax.bitcast_convert_type(a.astype(jnp.bfloat16).astype(jnp.float32), jnp.uint32))


def _gelu(x):
    return 0.5 * x * (1.0 + lax.erf(x * math.sqrt(0.5)))


def _mixer_kernel(x_ref, xh_ref, nmix_ref, win_ref, poolw_ref, pscale_ref, lng_ref, lnb_ref,
                  sguw_ref, sgub_ref, onp_ref, ons_ref, wout_ref, o_ref, pext_ref, mix_ref):
    i = pl.program_id(1)
    ts = x_ref.shape[1]
    pool_w = pscale_ref.shape[1]
    gdim = pool_w // N_POOL_GROUPS
    sgu_w = lng_ref.shape[1]
    hdim = sgu_w // SGU_HEADS

    x = x_ref[0]
    h = _rms(x, nmix_ref[...]).astype(jnp.bfloat16)
    z = jnp.dot(h, win_ref[...], preferred_element_type=jnp.float32)
    p = z[:, :pool_w]

    hh = _rms(xh_ref[0], nmix_ref[...]).astype(jnp.bfloat16)
    ph = jnp.dot(hh, win_ref[:, :pool_w], preferred_element_type=jnp.float32)
    ph = jnp.where(i > 0, ph, 0.0)
    pext_ref[0:HALO, :] = ph
    pext_ref[HALO:HALO + ts, :] = p

    pos = i * ts + lax.broadcasted_iota(jnp.int32, (ts, 1), 0)
    ssq = jnp.zeros((ts, 1), jnp.float32)
    a_parts = []
    for g, win in enumerate(POOL_WINDOWS):
        cols = slice(g * gdim, (g + 1) * gdim)
        s = pext_ref[HALO:HALO + ts, cols]
        for j in range(1, win):
            s = s + pext_ref[HALO - j:HALO - j + ts, cols]
        cnt = jnp.minimum(pos + 1, win).astype(jnp.float32)
        d = (s / cnt - p[:, cols]).astype(jnp.bfloat16)
        a = jnp.dot(d, poolw_ref[g], preferred_element_type=jnp.float32) * pscale_ref[:, cols]
        ssq = ssq + jnp.sum(a * a, axis=-1, keepdims=True)
        a_parts.append(a)
    inv_a = lax.rsqrt(ssq / pool_w + NORM_EPS)
    for g in range(N_POOL_GROUPS):
        cols = slice(g * gdim, (g + 1) * gdim)
        mix_ref[:, cols] = (a_parts[g] * inv_a * onp_ref[:, cols]).astype(jnp.bfloat16)

    gz = _gelu(z[:, pool_w:])
    tril = (lax.broadcasted_iota(jnp.int32, (SGU_CHUNK, SGU_CHUNK), 0)
            >= lax.broadcasted_iota(jnp.int32, (SGU_CHUNK, SGU_CHUNK), 1))
    ssq = jnp.zeros((ts, 1), jnp.float32)
    b_parts = []
    for hd in range(SGU_HEADS):
        cols = slice(hd * hdim, (hd + 1) * hdim)
        u = gz[:, hd * hdim:(hd + 1) * hdim]
        v = gz[:, sgu_w + hd * hdim:sgu_w + (hd + 1) * hdim]
        mu = jnp.mean(v, axis=-1, keepdims=True)
        vc = v - mu
        var = jnp.mean(vc * vc, axis=-1, keepdims=True)
        vn = (vc * lax.rsqrt(var + NORM_EPS) * lng_ref[:, cols] + lnb_ref[:, cols]).astype(jnp.bfloat16)
        w = jnp.where(tril, sguw_ref[hd], jnp.zeros((), sguw_ref.dtype))
        mixed = [jnp.dot(w, vn[n * SGU_CHUNK:(n + 1) * SGU_CHUNK], preferred_element_type=jnp.float32)
                 + sgub_ref[hd] for n in range(ts // SGU_CHUNK)]
        b = u * jnp.concatenate(mixed, axis=0)
        ssq = ssq + jnp.sum(b * b, axis=-1, keepdims=True)
        b_parts.append(b)
    inv_b = lax.rsqrt(ssq / sgu_w + NORM_EPS)
    for hd in range(SGU_HEADS):
        cols = slice(hd * hdim, (hd + 1) * hdim)
        mix_ref[:, pool_w + hd * hdim:pool_w + (hd + 1) * hdim] = (
            b_parts[hd] * inv_b * ons_ref[:, cols]).astype(jnp.bfloat16)

    o_ref[0] = x + jnp.dot(mix_ref[...], wout_ref[...], preferred_element_type=jnp.float32)


def _mixer(x, b0, nb, norm_mix, w_in, pool_w, pool_scale, ln_g, ln_b, sgu_w, sgu_b, on_pool, on_sgu, w_out):
    _, S, D = x.shape
    ts = min(MIX_TILE, S)
    pool_width = pool_scale.size
    sgu_width = ln_g.size
    in_width = w_in.shape[1]
    gdim = pool_width // N_POOL_GROUPS
    halo_blocks = ts // HALO
    full = lambda shape: pl.BlockSpec(shape, lambda b, i: (0,) * len(shape))
    return pl.pallas_call(
        _mixer_kernel,
        grid=(nb, S // ts),
        in_specs=[
            pl.BlockSpec((1, ts, D), lambda b, i: (b0 + b, i, 0)),
            pl.BlockSpec((1, HALO, D), lambda b, i: (b0 + b, jnp.maximum(i * halo_blocks - 1, 0), 0)),
            full((1, D)),
            full((D, in_width)),
            full((N_POOL_GROUPS, gdim, gdim)),
            full((1, pool_width)),
            full((1, sgu_width)),
            full((1, sgu_width)),
            full((SGU_HEADS, SGU_CHUNK, SGU_CHUNK)),
            full((SGU_HEADS, SGU_CHUNK, SGU_CHUNK)),
            full((1, pool_width)),
            full((1, sgu_width)),
            full((pool_width + sgu_width, D)),
        ],
        out_specs=pl.BlockSpec((1, ts, D), lambda b, i: (b, i, 0)),
        out_shape=jax.ShapeDtypeStruct((nb, S, D), jnp.float32),
        scratch_shapes=[
            pltpu.VMEM((HALO + ts, pool_width), jnp.float32),
            pltpu.VMEM((ts, pool_width + sgu_width), jnp.bfloat16),
        ],
        compiler_params=pltpu.CompilerParams(
            dimension_semantics=("parallel", "arbitrary"), vmem_limit_bytes=TC_VMEM_LIMIT),
        name="mixer",
    )(x, x, norm_mix.reshape(1, D), w_in.astype(jnp.bfloat16), pool_w.astype(jnp.bfloat16),
      pool_scale.reshape(1, pool_width), ln_g.reshape(1, sgu_width), ln_b.reshape(1, sgu_width),
      sgu_w.astype(jnp.bfloat16),
      jnp.broadcast_to(sgu_b[:, :, None], (SGU_HEADS, SGU_CHUNK, SGU_CHUNK)),
      on_pool.reshape(1, pool_width), on_sgu.reshape(1, sgu_width), w_out.astype(jnp.bfloat16))


def _topk_rows(s, k):
    n = s.shape[0]
    iota = lax.broadcasted_iota(jnp.int32, s.shape, 0)
    vals, idxs = [], []
    for _ in range(k):
        m = jnp.max(s, axis=0, keepdims=True)
        ix = jnp.min(jnp.where(s == m, iota, n), axis=0, keepdims=True)
        vals.append(m)
        idxs.append(ix)
        s = jnp.where(iota == ix, -jnp.inf, s)
    return vals, idxs


def _pair_candidates(v1, i1, v2, i2):
    k = PEER_TOPK
    v2c, i2c = jnp.concatenate(v2, axis=0), jnp.concatenate(i2, axis=0)
    m = v2c.shape[1]
    vals, experts, flats = [], [], []
    a = 0
    while k // (a + 1) > 1:
        rows = -(-(k // (a + 1)) // V7X_SUBLANES) * V7X_SUBLANES
        vals.append(v1[a] + v2c[:rows])
        experts.append(i1[a] * PEER_N_KEYS + i2c[:rows])
        flats.append(a * k + lax.broadcasted_iota(jnp.int32, (rows, m), 0))
        a += 1
    vals.append(jnp.concatenate(v1[a:], axis=0) + v2[0])
    experts.append(jnp.concatenate(i1[a:], axis=0) * PEER_N_KEYS + i2[0])
    flats.append((a + lax.broadcasted_iota(jnp.int32, (k - a, m), 0)) * k)
    return jnp.concatenate(vals, axis=0), jnp.concatenate(experts, axis=0), jnp.concatenate(flats, axis=0)


def _router_kernel(chunk_tokens, x_ref, nffn_ref, wq_ref, keys_ref, after_ref, h2_ref, idx_ref, slot_ref,
                   gate_ref, q_ref, idxt_ref, gatet_ref):
    del after_ref
    h2 = _rms(x_ref[...], nffn_ref[...]).astype(jnp.bfloat16)
    h2_ref[...] = h2
    q_ref[...] = jnp.dot(h2, wq_ref[...], preferred_element_type=jnp.float32).astype(jnp.bfloat16)
    dq = 2 * PEER_D_HALF
    nt = (((1,), (1,)), ((), ()))

    def head(hd, carry):
        off = pl.multiple_of(hd * dq, dq)
        s1 = lax.dot_general(keys_ref[0], q_ref[:, pl.ds(off, PEER_D_HALF)], nt,
                             preferred_element_type=jnp.float32)
        s2 = lax.dot_general(keys_ref[1], q_ref[:, pl.ds(off + PEER_D_HALF, PEER_D_HALF)], nt,
                             preferred_element_type=jnp.float32)
        v1, i1 = _topk_rows(s1, PEER_TOPK)
        v2, i2 = _topk_rows(s2, PEER_TOPK)
        cand, expert, flat = _pair_candidates(v1, i1, v2, i2)
        cv, ce = [], []
        for _ in range(PEER_TOPK):
            m = jnp.max(cand, axis=0, keepdims=True)
            ix = jnp.min(jnp.where(cand == m, flat, PEER_TOPK * PEER_TOPK), axis=0, keepdims=True)
            hit = flat == ix
            cv.append(m)
            ce.append(jnp.max(jnp.where(hit, expert, -1), axis=0, keepdims=True))
            cand = jnp.where(hit, -jnp.inf, cand)
        cvc = jnp.concatenate(cv, axis=0)
        e = jnp.exp(cvc - cv[0])
        gate = e / jnp.sum(e, axis=0, keepdims=True)
        row = pl.multiple_of(hd * PEER_TOPK, PEER_TOPK)
        idxt_ref[pl.ds(row, PEER_TOPK), :] = jnp.concatenate(ce, axis=0)
        gatet_ref[pl.ds(row, PEER_TOPK), :] = gate
        return carry

    lax.fori_loop(0, PEER_HEADS, head, 0)
    idx = idxt_ref[...].T
    idx_ref[...] = idx
    tok = pl.program_id(0) * idx.shape[0] + lax.broadcasted_iota(jnp.int32, idx.shape, 0)
    slot_ref[...] = _tile_linear_index(_score_word_row(tok, chunk_tokens), idx, PEER_N_KEYS * PEER_N_KEYS)
    gate_ref[...] = gatet_ref[...].T


def _router(x1, norm_ffn, wq, keys, tok0, T, after):
    D = x1.shape[1]
    tr = min(ROUTE_TILE, T)
    qw = wq.shape[1]
    assert tok0 % tr == 0 and T % tr == 0
    first = tok0 // tr
    full = lambda shape: pl.BlockSpec(shape, lambda i: (0,) * len(shape))
    return pl.pallas_call(
        functools.partial(_router_kernel, T),
        grid=(T // tr,),
        in_specs=[
            pl.BlockSpec((tr, D), lambda i: (first + i, 0)),
            full((1, D)),
            full((D, qw)),
            full((2, PEER_N_KEYS, PEER_D_HALF)),
            pl.BlockSpec(memory_space=pl.ANY),
        ],
        out_specs=[
            pl.BlockSpec((tr, D), lambda i: (i, 0)),
            pl.BlockSpec((tr, EXPERTS_PER_TOKEN), lambda i: (i, 0)),
            pl.BlockSpec((tr, EXPERTS_PER_TOKEN), lambda i: (i, 0)),
            pl.BlockSpec((tr, EXPERTS_PER_TOKEN), lambda i: (i, 0)),
        ],
        out_shape=[
            jax.ShapeDtypeStruct((T, D), jnp.bfloat16),
            jax.ShapeDtypeStruct((T, EXPERTS_PER_TOKEN), jnp.int32),
            jax.ShapeDtypeStruct((T, EXPERTS_PER_TOKEN), jnp.int32),
            jax.ShapeDtypeStruct((T, EXPERTS_PER_TOKEN), jnp.float32),
        ],
        scratch_shapes=[
            pltpu.VMEM((tr, qw), jnp.bfloat16),
            pltpu.VMEM((EXPERTS_PER_TOKEN, tr), jnp.int32),
            pltpu.VMEM((EXPERTS_PER_TOKEN, tr), jnp.float32),
        ],
        compiler_params=pltpu.CompilerParams(
            dimension_semantics=("parallel",), vmem_limit_bytes=TC_VMEM_LIMIT),
        name="router",
    )(x1, norm_ffn.reshape(1, D), wq, keys, after)


def _tile_linear_index(row, col, ncols):
    return ((row >> 3) * (ncols * V7X_SUBLANES) + (col >> 7) * (V7X_SUBLANES * V7X_LANES)
            + (row & (V7X_SUBLANES - 1)) * V7X_LANES + (col & (V7X_LANES - 1)))


def _score_tile(tokens):
    return min(SCORE_TOK_TILE, tokens)


def _score_word_row(tok, tokens):
    tm = _score_tile(tokens)
    assert tm & (tm - 1) == 0, "power-of-two score tile"
    return (tok >> int(math.log2(tm))) * (tm // 2) + (tok & (tm // 2 - 1))


def _scores_kernel(h_ref, ut_ref, o_ref):
    acc = jnp.dot(h_ref[...], ut_ref[...], preferred_element_type=jnp.float32)
    tm, tn = acc.shape
    bits = pltpu.bitcast(acc.astype(jnp.bfloat16).astype(jnp.float32), jnp.uint32)
    words = (bits[:tm // 2] >> 16) | (bits[tm // 2:] & jnp.uint32(HI_HALF))
    for n in range(tn // V7X_LANES):
        o_ref[:, n * V7X_SUBLANES:(n + 1) * V7X_SUBLANES, :] = words[:, n * V7X_LANES:(n + 1) * V7X_LANES].reshape(
            tm // 2 // V7X_SUBLANES, V7X_SUBLANES, V7X_LANES)


def _dense_scores(h2, u_t):
    T, D = h2.shape
    E = u_t.shape[1]
    tm, tn = _score_tile(T), min(SCORE_EXP_TILE, E)
    assert T % tm == 0 and E % tn == 0 and tm % (2 * V7X_SUBLANES) == 0 and tn % V7X_LANES == 0
    out = pl.pallas_call(
        _scores_kernel,
        grid=(T // tm, E // tn),
        in_specs=[pl.BlockSpec((tm, D), lambda i, j: (i, 0)), pl.BlockSpec((D, tn), lambda i, j: (0, j))],
        out_specs=pl.BlockSpec((tm // 2 // V7X_SUBLANES, tn // V7X_LANES * V7X_SUBLANES, V7X_LANES),
                               lambda i, j: (i, j, 0)),
        out_shape=jax.ShapeDtypeStruct((T // 2 // V7X_SUBLANES, E // V7X_LANES * V7X_SUBLANES, V7X_LANES),
                                       jnp.uint32),
        compiler_params=pltpu.CompilerParams(
            dimension_semantics=("parallel", "arbitrary"), vmem_limit_bytes=TC_VMEM_LIMIT),
        name="dense_scores",
    )(h2, u_t)
    return out.reshape(-1)


def _act_kernel(pre_ref, gate_ref, w_ref):
    w = (gate_ref[...] * _gelu(pre_ref[...])).astype(jnp.bfloat16).astype(jnp.float32)
    hi = pltpu.bitcast(w, jnp.uint32) & jnp.uint32(HI_HALF)
    w_ref[...] = pltpu.bitcast(hi | (hi >> 16), jnp.int32)


def _expert_weights(pre, gate):
    T, K = pre.shape
    te = min(EW_TILE, T)
    spec = pl.BlockSpec((te, K), lambda i: (i, 0))
    return pl.pallas_call(
        _act_kernel, grid=(T // te,), in_specs=[spec, spec], out_specs=spec,
        out_shape=jax.ShapeDtypeStruct((T, K), jnp.int32),
        compiler_params=pltpu.CompilerParams(dimension_semantics=("parallel",)),
        name="expert_weights",
    )(pre, gate)


def _final_kernel(x_ref, y_ref, g_ref, *rest):
    o_ref = rest[-1]
    o_ref[...] = _rms(x_ref[...] + y_ref[...], g_ref[...])


def _final(x1, x_tok0, peer, norm_final, out_prev, out_tok0, total):
    Tc, D = peer.shape
    te = min(EW_TILE, Tc)
    assert x_tok0 % te == 0 and out_tok0 % te == 0 and Tc % te == 0
    nblk = Tc // te
    spec = pl.BlockSpec((te, D), lambda i: (out_tok0 // te + i, 0))
    in_specs = [pl.BlockSpec((te, D), lambda i: (x_tok0 // te + i, 0)),
                pl.BlockSpec((te, D), lambda i: (i, 0)), pl.BlockSpec((1, D), lambda i: (0, 0))]
    args = [x1, peer, norm_final.reshape(1, D)]
    aliases = {}
    if out_prev is not None:
        in_specs.append(pl.BlockSpec(memory_space=pl.ANY))
        args.append(out_prev)
        aliases = {3: 0}
    return pl.pallas_call(
        _final_kernel, grid=(nblk,), in_specs=in_specs, out_specs=spec,
        out_shape=jax.ShapeDtypeStruct((total, D), jnp.float32),
        input_output_aliases=aliases,
        compiler_params=pltpu.CompilerParams(dimension_semantics=("parallel",)),
        name="final_norm",
    )(*args)


def _tree_sum(vals):
    while len(vals) > 1:
        nxt = [vals[i] + vals[i + 1] for i in range(0, len(vals) - 1, 2)]
        if len(vals) % 2:
            nxt.append(vals[-1])
        vals = nxt
    return vals[0]


def _sc_block_pipeline(nblk, items_per_token, loads, store, gather, compute):
    assert items_per_token % 2 == 0 and nblk >= 1

    for c in loads(0, 0):
        c.start()
    for c in loads(0, 0):
        c.wait()
    if nblk > 1:
        for c in loads(1, 1):
            c.start()
    gather(0, 0, 0, 0).start()

    @pl.loop(0, nblk)
    def _(b):
        slot = b % 2

        @pl.when(b >= 2)
        def _():
            store(b - 2, slot).wait()

        @pl.loop(0, SC_TOKENS)
        def _(t):
            for q in range(items_per_token):
                buf = q % 2
                if q + 1 < items_per_token:
                    gather(slot, t, q + 1, 1 - buf).start()
                else:
                    @pl.when(t + 1 < SC_TOKENS)
                    def _():
                        gather(slot, t + 1, 0, 1 - buf).start()

                    @pl.when(jnp.logical_and(t + 1 == SC_TOKENS, b + 1 < nblk))
                    def _():
                        for c in loads(b + 1, 1 - slot):
                            c.wait()
                        gather(1 - slot, 0, 0, 1 - buf).start()

                gather(slot, t, q, buf).wait()
                compute(slot, t, q, buf)

        store(b, slot).start()

        @pl.when(b + 2 < nblk)
        def _():
            for c in loads(b + 2, slot):
                c.start()

    if nblk >= 2:
        store(nblk - 2, nblk % 2).wait()
    store(nblk - 1, (nblk - 1) % 2).wait()


def _sc_mesh():
    return plsc.VectorSubcoreMesh(core_axis_name="c", subcore_axis_name="s")


def _sc_worker_id():
    return lax.axis_index("s") * V7X_SC_CORES + lax.axis_index("c")


def _sc_bf16(words):
    return plsc.bitcast(words, jnp.bfloat16)


def _sc_halves_f32(pairs):
    words = plsc.bitcast(pairs, jnp.uint32)
    return (plsc.bitcast(words << 16, jnp.float32),
            plsc.bitcast(words & jnp.uint32(HI_HALF), jnp.float32))


def _pick_scores(flat, slots, after):
    T, K = slots.shape
    L, nb = V7X_SC_LANES, SC_PICK_TOKENS
    tok_per_w = T // V7X_SC_WORKERS
    nblk = tok_per_w // nb
    half_tile = _score_tile(T) // 2
    assert T % (V7X_SC_WORKERS * nb) == 0 and half_tile % nb == 0 and half_tile & (half_tile - 1) == 0

    @functools.partial(
        pl.kernel, mesh=_sc_mesh(),
        out_type=jax.ShapeDtypeStruct((T, K), jnp.float32),
        compiler_params=pltpu.CompilerParams(needs_layout_passes=False),
        scratch_types=[
            pltpu.VMEM((2, nb, K), jnp.int32),
            pltpu.VMEM((2, nb, K), jnp.uint32),
            pltpu.VMEM((2, nb, K), jnp.float32),
            pltpu.SemaphoreType.DMA((2,)),
            pltpu.SemaphoreType.DMA((2,)),
            pltpu.SemaphoreType.DMA((2,)),
        ],
        name="pick_scores")
    def k(flat_hbm, slots_hbm, after_hbm, out_hbm, idx_v, word_v, val_v, idx_sems, out_sems, row_sems):
        del after_hbm
        base = _sc_worker_id() * tok_per_w

        def load(b, slot):
            return pltpu.make_async_copy(slots_hbm.at[pl.ds(base + b * nb, nb)], idx_v.at[slot], idx_sems.at[slot])

        def store(b, slot):
            return pltpu.make_async_copy(val_v.at[slot], out_hbm.at[pl.ds(base + b * nb, nb)], out_sems.at[slot])

        def gathers(slot):
            return [pltpu.make_async_copy(flat_hbm.at[idx_v.at[slot, t]], word_v.at[slot, t], row_sems.at[slot])
                    for t in range(nb)]

        def unpack(b, slot):
            low = ((base + b * nb) & half_tile) == 0
            shift = jnp.full((L,), jnp.where(low, 16, 0), jnp.uint32)
            for t in range(nb):
                for j in range(K // L):
                    words = word_v[slot, t, pl.ds(j * L, L)]
                    val_v[slot, t, pl.ds(j * L, L)] = plsc.bitcast(
                        (words << shift) & jnp.uint32(HI_HALF), jnp.float32)

        load(0, 0).start()

        @pl.loop(0, nblk)
        def _(b):
            slot = b % 2
            load(b, slot).wait()

            @pl.when(b + 1 < nblk)
            def _():
                load(b + 1, 1 - slot).start()

            @pl.when(b >= 2)
            def _():
                store(b - 2, slot).wait()

            for c in gathers(slot):
                c.start()
            for c in gathers(slot):
                c.wait()
            unpack(b, slot)
            store(b, slot).start()

        if nblk >= 2:
            store(nblk - 2, nblk % 2).wait()
        store(nblk - 1, (nblk - 1) % 2).wait()

    return k(flat, slots, after)


def _expert_mix(w, idx, table):
    T, K = w.shape
    DW = table.shape[1]
    D = 2 * DW
    L, G = V7X_SC_LANES, SC_GATHER
    nj = SC_MIX_CHUNK // L
    tok_per_w = T // V7X_SC_WORKERS
    assert T % (V7X_SC_WORKERS * SC_TOKENS) == 0 and K % (2 * G) == 0
    assert DW % SC_MIX_CHUNK == 0 and G % SC_MIX_GROUP == 0

    @functools.partial(
        pl.kernel, mesh=_sc_mesh(),
        out_type=jax.ShapeDtypeStruct((T, D), jnp.float32),
        compiler_params=pltpu.CompilerParams(needs_layout_passes=False),
        scratch_types=[
            pltpu.VMEM((2, SC_TOKENS, K), jnp.int32),
            pltpu.VMEM((2, SC_TOKENS, K), jnp.int32),
            pltpu.VMEM((2, G, DW), jnp.uint32),
            pltpu.VMEM((2, SC_TOKENS, D), jnp.float32),
            pltpu.SemaphoreType.DMA((2,)),
            pltpu.SemaphoreType.DMA((2,)),
            pltpu.SemaphoreType.DMA((2,)),
            pltpu.SemaphoreType.DMA((2,)),
        ],
        name="expert_mix")
    def k(w_hbm, idx_hbm, tab_hbm, out_hbm, idx_v, w_v, rows_v, out_v,
          idx_sems, w_sems, out_sems, row_sems):
        base = _sc_worker_id() * tok_per_w

        def loads(b, slot):
            toks = pl.ds(base + b * SC_TOKENS, SC_TOKENS)
            return [pltpu.make_async_copy(idx_hbm.at[toks], idx_v.at[slot], idx_sems.at[slot]),
                    pltpu.make_async_copy(w_hbm.at[toks], w_v.at[slot], w_sems.at[slot])]

        def store(b, slot):
            toks = pl.ds(base + b * SC_TOKENS, SC_TOKENS)
            return pltpu.make_async_copy(out_v.at[slot], out_hbm.at[toks], out_sems.at[slot])

        def gather(slot, t, q, buf):
            return pltpu.make_async_copy(
                tab_hbm.at[idx_v.at[slot, t, pl.ds(q * G, G)]], rows_v.at[buf], row_sems.at[buf])

        def compute(slot, t, q, buf):
            ssplat = jnp.full((L,), slot, jnp.int32)
            tsplat = jnp.full((L,), t, jnp.int32)
            for c in range(DW // SC_MIX_CHUNK):
                def body(kg, acc):
                    kk = kg * SC_MIX_GROUP
                    wks = [_sc_bf16(plsc.load_gather(
                        w_v, [ssplat, tsplat, jnp.full((L,), q * G + i, jnp.int32) + kk]))
                        for i in range(SC_MIX_GROUP)]
                    out = []
                    for j in range(nj):
                        prods = [wks[i] * _sc_bf16(rows_v[buf, kk + i, pl.ds(c * SC_MIX_CHUNK + j * L, L)])
                                 for i in range(SC_MIX_GROUP)]
                        lo, hi = _sc_halves_f32(_tree_sum(prods))
                        out += [acc[2 * j] + lo, acc[2 * j + 1] + hi]
                    return tuple(out)

                zero = jnp.zeros((L,), jnp.float32)
                acc = plsc.parallel_loop(0, G // SC_MIX_GROUP, carry=(zero,) * (2 * nj))(body)
                for j in range(nj):
                    for half in range(2):
                        dst = out_v.at[slot, t, pl.ds(half * DW + c * SC_MIX_CHUNK + j * L, L)]
                        if q == 0:
                            dst[...] = acc[2 * j + half]
                        else:
                            plsc.addupdate(dst, acc[2 * j + half])

        _sc_block_pipeline(tok_per_w // SC_TOKENS, K // G, loads, store, gather, compute)

    return k(w, idx, table)


def _chunk_sizes(total):
    ramp, size = [EDGE_CHUNK], EDGE_CHUNK
    while size < MAX_CHUNK:
        ramp.append(size)
        size *= 2
    middle = total - 2 * sum(ramp)
    assert middle >= 0 and middle % MAX_CHUNK == 0
    return ramp + [MAX_CHUNK] * (middle // MAX_CHUNK) + ramp[::-1]


def kernel(x, norm_mix, w_in, pool_w, pool_scale, sgu_ln_g, sgu_ln_b, sgu_w, sgu_b, out_norm_pool,
           out_norm_sgu, w_out, norm_ffn, peer_wq, peer_keys, peer_u, peer_v, norm_final):
    B, S, D = x.shape
    assert norm_mix.shape[0] == 1, "single-layer block"
    T = B * S
    mix_args = (norm_mix[0], w_in[0], pool_w[0], pool_scale[0], sgu_ln_g[0], sgu_ln_b[0],
                sgu_w[0], sgu_b[0], out_norm_pool[0], out_norm_sgu[0], w_out[0])
    x1_parts = [(b * S, _mixer(x, b, 1, *mix_args).reshape(S, D)) for b in range(B)]
    wq = peer_wq[0].astype(jnp.bfloat16)
    keys = peer_keys[0].astype(jnp.bfloat16)
    u_t = peer_u[0].astype(jnp.bfloat16).T
    v_tab = _pack_table(peer_v[0])
    out = None
    tok0 = 0
    ws = [norm_ffn, v_tab]
    peers = [norm_ffn, norm_ffn]
    for tc in _chunk_sizes(T):
        part0, x1 = [p for p in x1_parts if p[0] <= tok0][-1]
        assert tok0 + tc <= part0 + x1.shape[0], "a token chunk must lie inside one mixer call"
        h2, idx, slots, gate = _router(x1, norm_ffn[0], wq, keys, tok0 - part0, tc, ws[-2])
        pre = _pick_scores(_dense_scores(h2, u_t), slots, peers[-2])
        w = _expert_weights(pre, gate)
        ws.append(w)
        peer = _expert_mix(w, idx, v_tab)
        peers.append(peer)
        out = _final(x1, tok0 - part0, peer, norm_final, out, tok0, T)
        tok0 += tc
    return out.reshape(B, S, D)
```

```python
import functools
import math

import jax
import jax.numpy as jnp
from jax import lax
from jax.experimental import pallas as pl
from jax.experimental.pallas import tpu as pltpu
from jax.experimental.pallas import tpu_sc as plsc

POOL_WINDOWS = (2, 4, 8, 16)
N_POOL_GROUPS = len(POOL_WINDOWS)
SGU_HEADS = 4
SGU_CHUNK = 128
PEER_HEADS = 8
PEER_N_KEYS = 128
PEER_D_HALF = 128
PEER_TOPK = 16
NORM_EPS = 1e-6
EXPERTS_PER_TOKEN = PEER_HEADS * PEER_TOPK

V7X_LANES = 128
V7X_SUBLANES = 8
V7X_SC_CORES = 2
V7X_SC_SUBCORES = 16
V7X_SC_LANES = 16
V7X_SC_WORKERS = V7X_SC_CORES * V7X_SC_SUBCORES

HALO = max(POOL_WINDOWS)
MIX_TILE = 512
ROUTE_TILE = 1024
EW_TILE = 512
SCORE_TOK_TILE = 2048
SCORE_EXP_TILE = 1024
SC_PICK_TOKENS = 64
SC_GATHER = 64
SC_TOKENS = 8
SC_MIX_CHUNK = 128
SC_MIX_GROUP = 4
SC_UNROLL = 2
HI_HALF = 0xFFFF0000
TC_VMEM_LIMIT = 48 * 1024 * 1024
EDGE_CHUNK = 512
MAX_CHUNK = 2048


def _rms(x, g):
    inv = lax.rsqrt(jnp.mean(x * x, axis=-1, keepdims=True) + NORM_EPS)
    return x * inv * g


def _pack_halves(bits):
    half = bits.shape[1] // 2
    return (bits[:, :half] >> 16) | (bits[:, half:] & jnp.uint32(HI_HALF))


def _pack_table(a):
    return _pack_halves(lax.bitcast_convert_type(a.astype(jnp.bfloat16).astype(jnp.float32), jnp.uint32))


def _gelu(x):
    return 0.5 * x * (1.0 + lax.erf(x * math.sqrt(0.5)))


def _mixer_kernel(x_ref, xh_ref, nmix_ref, win_ref, poolw_ref, pscale_ref, lng_ref, lnb_ref,
                  sguw_ref, sgub_ref, onp_ref, ons_ref, wout_ref, o_ref, pext_ref, mix_ref):
    i = pl.program_id(1)
    ts = x_ref.shape[1]
    pool_w = pscale_ref.shape[1]
    gdim = pool_w // N_POOL_GROUPS
    sgu_w = lng_ref.shape[1]
    hdim = sgu_w // SGU_HEADS

    x = x_ref[0]
    h = _rms(x, nmix_ref[...]).astype(jnp.bfloat16)
    z = jnp.dot(h, win_ref[...], preferred_element_type=jnp.float32)
    p = z[:, :pool_w]

    hh = _rms(xh_ref[0], nmix_ref[...]).astype(jnp.bfloat16)
    ph = jnp.dot(hh, win_ref[:, :pool_w], preferred_element_type=jnp.float32)
    ph = jnp.where(i > 0, ph, 0.0)
    pext_ref[0:HALO, :] = ph
    pext_ref[HALO:HALO + ts, :] = p

    pos = i * ts + lax.broadcasted_iota(jnp.int32, (ts, 1), 0)
    ssq = jnp.zeros((ts, 1), jnp.float32)
    a_parts = []
    for g, win in enumerate(POOL_WINDOWS):
        cols = slice(g * gdim, (g + 1) * gdim)
        s = pext_ref[HALO:HALO + ts, cols]
        for j in range(1, win):
            s = s + pext_ref[HALO - j:HALO - j + ts, cols]
        cnt = jnp.minimum(pos + 1, win).astype(jnp.float32)
        d = (s / cnt - p[:, cols]).astype(jnp.bfloat16)
        a = jnp.dot(d, poolw_ref[g], preferred_element_type=jnp.float32) * pscale_ref[:, cols]
        ssq = ssq + jnp.sum(a * a, axis=-1, keepdims=True)
        a_parts.append(a)
    inv_a = lax.rsqrt(ssq / pool_w + NORM_EPS)
    for g in range(N_POOL_GROUPS):
        cols = slice(g * gdim, (g + 1) * gdim)
        mix_ref[:, cols] = (a_parts[g] * inv_a * onp_ref[:, cols]).astype(jnp.bfloat16)

    gz = _gelu(z[:, pool_w:])
    tril = (lax.broadcasted_iota(jnp.int32, (SGU_CHUNK, SGU_CHUNK), 0)
            >= lax.broadcasted_iota(jnp.int32, (SGU_CHUNK, SGU_CHUNK), 1))
    ssq = jnp.zeros((ts, 1), jnp.float32)
    b_parts = []
    for hd in range(SGU_HEADS):
        cols = slice(hd * hdim, (hd + 1) * hdim)
        u = gz[:, hd * hdim:(hd + 1) * hdim]
        v = gz[:, sgu_w + hd * hdim:sgu_w + (hd + 1) * hdim]
        mu = jnp.mean(v, axis=-1, keepdims=True)
        vc = v - mu
        var = jnp.mean(vc * vc, axis=-1, keepdims=True)
        vn = (vc * lax.rsqrt(var + NORM_EPS) * lng_ref[:, cols] + lnb_ref[:, cols]).astype(jnp.bfloat16)
        w = jnp.where(tril, sguw_ref[hd], jnp.zeros((), sguw_ref.dtype))
        mixed = [jnp.dot(w, vn[n * SGU_CHUNK:(n + 1) * SGU_CHUNK], preferred_element_type=jnp.float32)
                 + sgub_ref[hd] for n in range(ts // SGU_CHUNK)]
        b = u * jnp.concatenate(mixed, axis=0)
        ssq = ssq + jnp.sum(b * b, axis=-1, keepdims=True)
        b_parts.append(b)
    inv_b = lax.rsqrt(ssq / sgu_w + NORM_EPS)
    for hd in range(SGU_HEADS):
        cols = slice(hd * hdim, (hd + 1) * hdim)
        mix_ref[:, pool_w + hd * hdim:pool_w + (hd + 1) * hdim] = (
            b_parts[hd] * inv_b * ons_ref[:, cols]).astype(jnp.bfloat16)

    o_ref[0] = x + jnp.dot(mix_ref[...], wout_ref[...], preferred_element_type=jnp.float32)


def _mixer(x, b0, nb, norm_mix, w_in, pool_w, pool_scale, ln_g, ln_b, sgu_w, sgu_b, on_pool, on_sgu, w_out):
    _, S, D = x.shape
    ts = min(MIX_TILE, S)
    pool_width = pool_scale.size
    sgu_width = ln_g.size
    in_width = w_in.shape[1]
    gdim = pool_width // N_POOL_GROUPS
    halo_blocks = ts // HALO
    full = lambda shape: pl.BlockSpec(shape, lambda b, i: (0,) * len(shape))
    return pl.pallas_call(
        _mixer_kernel,
        grid=(nb, S // ts),
        in_specs=[
            pl.BlockSpec((1, ts, D), lambda b, i: (b0 + b, i, 0)),
            pl.BlockSpec((1, HALO, D), lambda b, i: (b0 + b, jnp.maximum(i * halo_blocks - 1, 0), 0)),
            full((1, D)),
            full((D, in_width)),
            full((N_POOL_GROUPS, gdim, gdim)),
            full((1, pool_width)),
            full((1, sgu_width)),
            full((1, sgu_width)),
            full((SGU_HEADS, SGU_CHUNK, SGU_CHUNK)),
            full((SGU_HEADS, SGU_CHUNK, SGU_CHUNK)),
            full((1, pool_width)),
            full((1, sgu_width)),
            full((pool_width + sgu_width, D)),
        ],
        out_specs=pl.BlockSpec((1, ts, D), lambda b, i: (b, i, 0)),
        out_shape=jax.ShapeDtypeStruct((nb, S, D), jnp.float32),
        scratch_shapes=[
            pltpu.VMEM((HALO + ts, pool_width), jnp.float32),
            pltpu.VMEM((ts, pool_width + sgu_width), jnp.bfloat16),
        ],
        compiler_params=pltpu.CompilerParams(
            dimension_semantics=("parallel", "arbitrary"), vmem_limit_bytes=TC_VMEM_LIMIT),
        name="mixer",
    )(x, x, norm_mix.reshape(1, D), w_in.astype(jnp.bfloat16), pool_w.astype(jnp.bfloat16),
      pool_scale.reshape(1, pool_width), ln_g.reshape(1, sgu_width), ln_b.reshape(1, sgu_width),
      sgu_w.astype(jnp.bfloat16),
      jnp.broadcast_to(sgu_b[:, :, None], (SGU_HEADS, SGU_CHUNK, SGU_CHUNK)),
      on_pool.reshape(1, pool_width), on_sgu.reshape(1, sgu_width), w_out.astype(jnp.bfloat16))


def _topk_rows(s, k):
    n = s.shape[0]
    iota = lax.broadcasted_iota(jnp.int32, s.shape, 0)
    vals, idxs = [], []
    for _ in range(k):
        m = jnp.max(s, axis=0, keepdims=True)
        ix = jnp.min(jnp.where(s == m, iota, n), axis=0, keepdims=True)
        vals.append(m)
        idxs.append(ix)
        s = jnp.where(iota == ix, -jnp.inf, s)
    return vals, idxs


def _pair_candidates(v1, i1, v2, i2):
    k = PEER_TOPK
    v2c, i2c = jnp.concatenate(v2, axis=0), jnp.concatenate(i2, axis=0)
    m = v2c.shape[1]
    vals, experts, flats = [], [], []
    a = 0
    while k // (a + 1) > 1:
        rows = -(-(k // (a + 1)) // V7X_SUBLANES) * V7X_SUBLANES
        vals.append(v1[a] + v2c[:rows])
        experts.append(i1[a] * PEER_N_KEYS + i2c[:rows])
        flats.append(a * k + lax.broadcasted_iota(jnp.int32, (rows, m), 0))
        a += 1
    vals.append(jnp.concatenate(v1[a:], axis=0) + v2[0])
    experts.append(jnp.concatenate(i1[a:], axis=0) * PEER_N_KEYS + i2[0])
    flats.append((a + lax.broadcasted_iota(jnp.int32, (k - a, m), 0)) * k)
    return jnp.concatenate(vals, axis=0), jnp.concatenate(experts, axis=0), jnp.concatenate(flats, axis=0)


def _router_kernel(chunk_tokens, x_ref, nffn_ref, wq_ref, keys_ref, after_ref, h2_ref, idx_ref, slot_ref,
                   gate_ref, q_ref, idxt_ref, gatet_ref):
    del after_ref
    h2 = _rms(x_ref[...], nffn_ref[...]).astype(jnp.bfloat16)
    h2_ref[...] = h2
    q_ref[...] = jnp.dot(h2, wq_ref[...], preferred_element_type=jnp.float32).astype(jnp.bfloat16)
    dq = 2 * PEER_D_HALF
    nt = (((1,), (1,)), ((), ()))

    def head(hd, carry):
        off = pl.multiple_of(hd * dq, dq)
        s1 = lax.dot_general(keys_ref[0], q_ref[:, pl.ds(off, PEER_D_HALF)], nt,
                             preferred_element_type=jnp.float32)
        s2 = lax.dot_general(keys_ref[1], q_ref[:, pl.ds(off + PEER_D_HALF, PEER_D_HALF)], nt,
                             preferred_element_type=jnp.float32)
        v1, i1 = _topk_rows(s1, PEER_TOPK)
        v2, i2 = _topk_rows(s2, PEER_TOPK)
        cand, expert, flat = _pair_candidates(v1, i1, v2, i2)
        cv, ce = [], []
        for _ in range(PEER_TOPK):
            m = jnp.max(cand, axis=0, keepdims=True)
            ix = jnp.min(jnp.where(cand == m, flat, PEER_TOPK * PEER_TOPK), axis=0, keepdims=True)
            hit = flat == ix
            cv.append(m)
            ce.append(jnp.max(jnp.where(hit, expert, -1), axis=0, keepdims=True))
            cand = jnp.where(hit, -jnp.inf, cand)
        cvc = jnp.concatenate(cv, axis=0)
        e = jnp.exp(cvc - cv[0])
        gate = e / jnp.sum(e, axis=0, keepdims=True)
        row = pl.multiple_of(hd * PEER_TOPK, PEER_TOPK)
        idxt_ref[pl.ds(row, PEER_TOPK), :] = jnp.concatenate(ce, axis=0)
        gatet_ref[pl.ds(row, PEER_TOPK), :] = gate
        return carry

    lax.fori_loop(0, PEER_HEADS, head, 0)
    idx = idxt_ref[...].T
    idx_ref[...] = idx
    tok = pl.program_id(0) * idx.shape[0] + lax.broadcasted_iota(jnp.int32, idx.shape, 0)
    slot_ref[...] = _tile_linear_index(_score_word_row(tok, chunk_tokens), idx, PEER_N_KEYS * PEER_N_KEYS)
    gate_ref[...] = gatet_ref[...].T


def _router(x1, norm_ffn, wq, keys, tok0, T, after):
    D = x1.shape[1]
    tr = min(ROUTE_TILE, T)
    qw = wq.shape[1]
    assert tok0 % tr == 0 and T % tr == 0
    first = tok0 // tr
    full = lambda shape: pl.BlockSpec(shape, lambda i: (0,) * len(shape))
    return pl.pallas_call(
        functools.partial(_router_kernel, T),
        grid=(T // tr,),
        in_specs=[
            pl.BlockSpec((tr, D), lambda i: (first + i, 0)),
            full((1, D)),
            full((D, qw)),
            full((2, PEER_N_KEYS, PEER_D_HALF)),
            pl.BlockSpec(memory_space=pl.ANY),
        ],
        out_specs=[
            pl.BlockSpec((tr, D), lambda i: (i, 0)),
            pl.BlockSpec((tr, EXPERTS_PER_TOKEN), lambda i: (i, 0)),
            pl.BlockSpec((tr, EXPERTS_PER_TOKEN), lambda i: (i, 0)),
            pl.BlockSpec((tr, EXPERTS_PER_TOKEN), lambda i: (i, 0)),
        ],
        out_shape=[
            jax.ShapeDtypeStruct((T, D), jnp.bfloat16),
            jax.ShapeDtypeStruct((T, EXPERTS_PER_TOKEN), jnp.int32),
            jax.ShapeDtypeStruct((T, EXPERTS_PER_TOKEN), jnp.int32),
            jax.ShapeDtypeStruct((T, EXPERTS_PER_TOKEN), jnp.float32),
        ],
        scratch_shapes=[
            pltpu.VMEM((tr, qw), jnp.bfloat16),
            pltpu.VMEM((EXPERTS_PER_TOKEN, tr), jnp.int32),
            pltpu.VMEM((EXPERTS_PER_TOKEN, tr), jnp.float32),
        ],
        compiler_params=pltpu.CompilerParams(
            dimension_semantics=("parallel",), vmem_limit_bytes=TC_VMEM_LIMIT),
        name="router",
    )(x1, norm_ffn.reshape(1, D), wq, keys, after)


def _tile_linear_index(row, col, ncols):
    return ((row >> 3) * (ncols * V7X_SUBLANES) + (col >> 7) * (V7X_SUBLANES * V7X_LANES)
            + (row & (V7X_SUBLANES - 1)) * V7X_LANES + (col & (V7X_LANES - 1)))


def _score_tile(tokens):
    return min(SCORE_TOK_TILE, tokens)


def _score_word_row(tok, tokens):
    tm = _score_tile(tokens)
    assert tm & (tm - 1) == 0, "power-of-two score tile"
    return (tok >> int(math.log2(tm))) * (tm // 2) + (tok & (tm // 2 - 1))


def _scores_kernel(h_ref, ut_ref, o_ref):
    acc = jnp.dot(h_ref[...], ut_ref[...], preferred_element_type=jnp.float32)
    tm, tn = acc.shape
    bits = pltpu.bitcast(acc.astype(jnp.bfloat16).astype(jnp.float32), jnp.uint32)
    words = (bits[:tm // 2] >> 16) | (bits[tm // 2:] & jnp.uint32(HI_HALF))
    for n in range(tn // V7X_LANES):
        o_ref[:, n * V7X_SUBLANES:(n + 1) * V7X_SUBLANES, :] = words[:, n * V7X_LANES:(n + 1) * V7X_LANES].reshape(
            tm // 2 // V7X_SUBLANES, V7X_SUBLANES, V7X_LANES)


def _dense_scores(h2, u_t):
    T, D = h2.shape
    E = u_t.shape[1]
    tm, tn = _score_tile(T), min(SCORE_EXP_TILE, E)
    assert T % tm == 0 and E % tn == 0 and tm % (2 * V7X_SUBLANES) == 0 and tn % V7X_LANES == 0
    out = pl.pallas_call(
        _scores_kernel,
        grid=(T // tm, E // tn),
        in_specs=[pl.BlockSpec((tm, D), lambda i, j: (i, 0)), pl.BlockSpec((D, tn), lambda i, j: (0, j))],
        out_specs=pl.BlockSpec((tm // 2 // V7X_SUBLANES, tn // V7X_LANES * V7X_SUBLANES, V7X_LANES),
                               lambda i, j: (i, j, 0)),
        out_shape=jax.ShapeDtypeStruct((T // 2 // V7X_SUBLANES, E // V7X_LANES * V7X_SUBLANES, V7X_LANES),
                                       jnp.uint32),
        compiler_params=pltpu.CompilerParams(
            dimension_semantics=("parallel", "arbitrary"), vmem_limit_bytes=TC_VMEM_LIMIT),
        name="dense_scores",
    )(h2, u_t)
    return out.reshape(-1)


def _act_kernel(pre_ref, gate_ref, w_ref):
    w = (gate_ref[...] * _gelu(pre_ref[...])).astype(jnp.bfloat16).astype(jnp.float32)
    hi = pltpu.bitcast(w, jnp.uint32) & jnp.uint32(HI_HALF)
    w_ref[...] = pltpu.bitcast(hi | (hi >> 16), jnp.int32)


def _expert_weights(pre, gate):
    T, K = pre.shape
    te = min(EW_TILE, T)
    spec = pl.BlockSpec((te, K), lambda i: (i, 0))
    return pl.pallas_call(
        _act_kernel, grid=(T // te,), in_specs=[spec, spec], out_specs=spec,
        out_shape=jax.ShapeDtypeStruct((T, K), jnp.int32),
        compiler_params=pltpu.CompilerParams(dimension_semantics=("parallel",)),
        name="expert_weights",
    )(pre, gate)


def _final_kernel(x_ref, y_ref, g_ref, *rest):
    o_ref = rest[-1]
    o_ref[...] = _rms(x_ref[...] + y_ref[...], g_ref[...])


def _final(x1, x_tok0, peer, norm_final, out_prev, out_tok0, total):
    Tc, D = peer.shape
    te = min(EW_TILE, Tc)
    assert x_tok0 % te == 0 and out_tok0 % te == 0 and Tc % te == 0
    nblk = Tc // te
    spec = pl.BlockSpec((te, D), lambda i: (out_tok0 // te + i, 0))
    in_specs = [pl.BlockSpec((te, D), lambda i: (x_tok0 // te + i, 0)),
                pl.BlockSpec((te, D), lambda i: (i, 0)), pl.BlockSpec((1, D), lambda i: (0, 0))]
    args = [x1, peer, norm_final.reshape(1, D)]
    aliases = {}
    if out_prev is not None:
        in_specs.append(pl.BlockSpec(memory_space=pl.ANY))
        args.append(out_prev)
        aliases = {3: 0}
    return pl.pallas_call(
        _final_kernel, grid=(nblk,), in_specs=in_specs, out_specs=spec,
        out_shape=jax.ShapeDtypeStruct((total, D), jnp.float32),
        input_output_aliases=aliases,
        compiler_params=pltpu.CompilerParams(dimension_semantics=("parallel",)),
        name="final_norm",
    )(*args)


def _tree_sum(vals):
    while len(vals) > 1:
        nxt = [vals[i] + vals[i + 1] for i in range(0, len(vals) - 1, 2)]
        if len(vals) % 2:
            nxt.append(vals[-1])
        vals = nxt
    return vals[0]


def _sc_block_pipeline(nblk, items_per_token, loads, store, gather, compute):
    assert items_per_token % 2 == 0 and nblk >= 1

    for c in loads(0, 0):
        c.start()
    for c in loads(0, 0):
        c.wait()
    if nblk > 1:
        for c in loads(1, 1):
            c.start()
    gather(0, 0, 0, 0).start()

    @pl.loop(0, nblk)
    def _(b):
        slot = b % 2

        @pl.when(b >= 2)
        def _():
            store(b - 2, slot).wait()

        @pl.loop(0, SC_TOKENS)
        def _(t):
            for q in range(items_per_token):
                buf = q % 2
                if q + 1 < items_per_token:
                    gather(slot, t, q + 1, 1 - buf).start()
                else:
                    @pl.when(t + 1 < SC_TOKENS)
                    def _():
                        gather(slot, t + 1, 0, 1 - buf).start()

                    @pl.when(jnp.logical_and(t + 1 == SC_TOKENS, b + 1 < nblk))
                    def _():
                        for c in loads(b + 1, 1 - slot):
                            c.wait()
                        gather(1 - slot, 0, 0, 1 - buf).start()

                gather(slot, t, q, buf).wait()
                compute(slot, t, q, buf)

        store(b, slot).start()

        @pl.when(b + 2 < nblk)
        def _():
            for c in loads(b + 2, slot):
                c.start()

    if nblk >= 2:
        store(nblk - 2, nblk % 2).wait()
    store(nblk - 1, (nblk - 1) % 2).wait()


def _sc_mesh():
    return plsc.VectorSubcoreMesh(core_axis_name="c", subcore_axis_name="s")


def _sc_worker_id():
    return lax.axis_index("s") * V7X_SC_CORES + lax.axis_index("c")


def _sc_bf16(words):
    return plsc.bitcast(words, jnp.bfloat16)


def _sc_halves_f32(pairs):
    words = plsc.bitcast(pairs, jnp.uint32)
    return (plsc.bitcast(words << 16, jnp.float32),
            plsc.bitcast(words & jnp.uint32(HI_HALF), jnp.float32))


def _pick_scores(flat, slots, after):
    T, K = slots.shape
    L = V7X_SC_LANES
    tok_per_w = T // V7X_SC_WORKERS
    nb = min(SC_PICK_TOKENS, tok_per_w)
    nblk = tok_per_w // nb
    half_tile = _score_tile(T) // 2
    assert T % (V7X_SC_WORKERS * nb) == 0 and half_tile % nb == 0 and half_tile & (half_tile - 1) == 0

    @functools.partial(
        pl.kernel, mesh=_sc_mesh(),
        out_type=jax.ShapeDtypeStruct((T, K), jnp.float32),
        compiler_params=pltpu.CompilerParams(needs_layout_passes=False),
        scratch_types=[
            pltpu.VMEM((2, nb, K), jnp.int32),
            pltpu.VMEM((2, nb, K), jnp.uint32),
            pltpu.VMEM((2, nb, K), jnp.float32),
            pltpu.SemaphoreType.DMA((2,)),
            pltpu.SemaphoreType.DMA((2,)),
            pltpu.SemaphoreType.DMA((2,)),
        ],
        name="pick_scores")
    def k(flat_hbm, slots_hbm, after_hbm, out_hbm, idx_v, word_v, val_v, idx_sems, out_sems, row_sems):
        del after_hbm
        base = _sc_worker_id() * tok_per_w

        def load(b, slot):
            return pltpu.make_async_copy(slots_hbm.at[pl.ds(base + b * nb, nb)], idx_v.at[slot], idx_sems.at[slot])

        def store(b, slot):
            return pltpu.make_async_copy(val_v.at[slot], out_hbm.at[pl.ds(base + b * nb, nb)], out_sems.at[slot])

        def gathers(slot):
            return [pltpu.make_async_copy(flat_hbm.at[idx_v.at[slot, t]], word_v.at[slot, t], row_sems.at[slot])
                    for t in range(nb)]

        def unpack(b, slot):
            low = ((base + b * nb) & half_tile) == 0
            shift = jnp.full((L,), jnp.where(low, 16, 0), jnp.uint32)
            @pl.loop(0, nb)
            def _(t):
                for j in range(K // L):
                    words = word_v[slot, t, pl.ds(j * L, L)]
                    val_v[slot, t, pl.ds(j * L, L)] = plsc.bitcast(
                        (words << shift) & jnp.uint32(HI_HALF), jnp.float32)

        load(0, 0).start()

        @pl.loop(0, nblk)
        def _(b):
            slot = b % 2
            load(b, slot).wait()

            @pl.when(b + 1 < nblk)
            def _():
                load(b + 1, 1 - slot).start()

            @pl.when(b >= 2)
            def _():
                store(b - 2, slot).wait()

            for c in gathers(slot):
                c.start()
            for c in gathers(slot):
                c.wait()
            unpack(b, slot)
            store(b, slot).start()

        if nblk >= 2:
            store(nblk - 2, nblk % 2).wait()
        store(nblk - 1, (nblk - 1) % 2).wait()

    return k(flat, slots, after)


def _expert_mix(w, idx, table):
    T, K = w.shape
    DW = table.shape[1]
    D = 2 * DW
    L, G = V7X_SC_LANES, SC_GATHER
    nj = SC_MIX_CHUNK // L
    tok_per_w = T // V7X_SC_WORKERS
    assert T % (V7X_SC_WORKERS * SC_TOKENS) == 0 and K % (2 * G) == 0
    assert DW % SC_MIX_CHUNK == 0 and G % SC_MIX_GROUP == 0

    @functools.partial(
        pl.kernel, mesh=_sc_mesh(),
        out_type=jax.ShapeDtypeStruct((T, D), jnp.float32),
        compiler_params=pltpu.CompilerParams(needs_layout_passes=False),
        scratch_types=[
            pltpu.VMEM((2, SC_TOKENS, K), jnp.int32),
            pltpu.VMEM((2, SC_TOKENS, K), jnp.int32),
            pltpu.VMEM((2, G, DW), jnp.uint32),
            pltpu.VMEM((2, SC_TOKENS, D), jnp.float32),
            pltpu.SemaphoreType.DMA((2,)),
            pltpu.SemaphoreType.DMA((2,)),
            pltpu.SemaphoreType.DMA((2,)),
            pltpu.SemaphoreType.DMA((2,)),
        ],
        name="expert_mix")
    def k(w_hbm, idx_hbm, tab_hbm, out_hbm, idx_v, w_v, rows_v, out_v,
          idx_sems, w_sems, out_sems, row_sems):
        base = _sc_worker_id() * tok_per_w

        def loads(b, slot):
            toks = pl.ds(base + b * SC_TOKENS, SC_TOKENS)
            return [pltpu.make_async_copy(idx_hbm.at[toks], idx_v.at[slot], idx_sems.at[slot]),
                    pltpu.make_async_copy(w_hbm.at[toks], w_v.at[slot], w_sems.at[slot])]

        def store(b, slot):
            toks = pl.ds(base + b * SC_TOKENS, SC_TOKENS)
            return pltpu.make_async_copy(out_v.at[slot], out_hbm.at[toks], out_sems.at[slot])

        def gather(slot, t, q, buf):
            return pltpu.make_async_copy(
                tab_hbm.at[idx_v.at[slot, t, pl.ds(q * G, G)]], rows_v.at[buf], row_sems.at[buf])

        def compute(slot, t, q, buf):
            ssplat = jnp.full((L,), slot, jnp.int32)
            tsplat = jnp.full((L,), t, jnp.int32)
            for c in range(DW // SC_MIX_CHUNK):
                def body(kg, acc):
                    kk = kg * SC_MIX_GROUP
                    wks = [_sc_bf16(plsc.load_gather(
                        w_v, [ssplat, tsplat, jnp.full((L,), q * G + i, jnp.int32) + kk]))
                        for i in range(SC_MIX_GROUP)]
                    out = []
                    for j in range(nj):
                        prods = [wks[i] * _sc_bf16(rows_v[buf, kk + i, pl.ds(c * SC_MIX_CHUNK + j * L, L)])
                                 for i in range(SC_MIX_GROUP)]
                        lo, hi = _sc_halves_f32(_tree_sum(prods))
                        out += [acc[2 * j] + lo, acc[2 * j + 1] + hi]
                    return tuple(out)

                zero = jnp.zeros((L,), jnp.float32)
                acc = plsc.parallel_loop(0, G // SC_MIX_GROUP, carry=(zero,) * (2 * nj))(body)
                for j in range(nj):
                    for half in range(2):
                        dst = out_v.at[slot, t, pl.ds(half * DW + c * SC_MIX_CHUNK + j * L, L)]
                        if q == 0:
                            dst[...] = acc[2 * j + half]
                        else:
                            plsc.addupdate(dst, acc[2 * j + half])

        _sc_block_pipeline(tok_per_w // SC_TOKENS, K // G, loads, store, gather, compute)

    return k(w, idx, table)


def _chunk_sizes(total):
    ramp, size = [EDGE_CHUNK], EDGE_CHUNK
    while size < MAX_CHUNK:
        ramp.append(size)
        size *= 2
    middle = total - 2 * sum(ramp)
    assert middle >= 0 and middle % MAX_CHUNK == 0
    return ramp + [MAX_CHUNK] * (middle // MAX_CHUNK) + ramp[::-1]


def kernel(x, norm_mix, w_in, pool_w, pool_scale, sgu_ln_g, sgu_ln_b, sgu_w, sgu_b, out_norm_pool,
           out_norm_sgu, w_out, norm_ffn, peer_wq, peer_keys, peer_u, peer_v, norm_final):
    B, S, D = x.shape
    assert norm_mix.shape[0] == 1, "single-layer block"
    T = B * S
    mix_args = (norm_mix[0], w_in[0], pool_w[0], pool_scale[0], sgu_ln_g[0], sgu_ln_b[0],
                sgu_w[0], sgu_b[0], out_norm_pool[0], out_norm_sgu[0], w_out[0])
    x1_parts = [(b * S, _mixer(x, b, 1, *mix_args).reshape(S, D)) for b in range(B)]
    wq = peer_wq[0].astype(jnp.bfloat16)
    keys = peer_keys[0].astype(jnp.bfloat16)
    u_t = peer_u[0].astype(jnp.bfloat16).T
    v_tab = _pack_table(peer_v[0])
    out = None
    tok0 = 0
    ws = [norm_ffn, v_tab]
    peers = [norm_ffn, norm_ffn]
    for tc in _chunk_sizes(T):
        part0, x1 = [p for p in x1_parts if p[0] <= tok0][-1]
        assert tok0 + tc <= part0 + x1.shape[0], "a token chunk must lie inside one mixer call"
        h2, idx, slots, gate = _router(x1, norm_ffn[0], wq, keys, tok0 - part0, tc, ws[-2])
        pre = _pick_scores(_dense_scores(h2, u_t), slots, peers[-2])
        w = _expert_weights(pre, gate)
        ws.append(w)
        peer = _expert_mix(w, idx, v_tab)
        peers.append(peer)
        out = _final(x1, tok0 - part0, peer, norm_final, out, tok0, T)
        tok0 += tc
    return out.reshape(B, S, D)
```

```python
import functools
import math

import jax
import jax.numpy as jnp
from jax import lax
from jax.experimental import pallas as pl
from jax.experimental.pallas import tpu as pltpu
from jax.experimental.pallas import tpu_sc as plsc

POOL_WINDOWS = (2, 4, 8, 16)
N_POOL_GROUPS = len(POOL_WINDOWS)
SGU_HEADS = 4
SGU_CHUNK = 128
PEER_HEADS = 8
PEER_N_KEYS = 128
PEER_D_HALF = 128
PEER_TOPK = 16
NORM_EPS = 1e-6
EXPERTS_PER_TOKEN = PEER_HEADS * PEER_TOPK

V7X_LANES = 128
V7X_SUBLANES = 8
V7X_SC_CORES = 2
V7X_SC_SUBCORES = 16
V7X_SC_LANES = 16
V7X_SC_WORKERS = V7X_SC_CORES * V7X_SC_SUBCORES

HALO = max(POOL_WINDOWS)
MIX_TILE = 512
ROUTE_TILE = 1024
EW_TILE = 512
SCORE_TOK_TILE = 2048
SCORE_EXP_TILE = 1024
SC_PICK_TOKENS = 64
SC_GATHER = 64
SC_TOKENS = 8
SC_MIX_CHUNK = 128
SC_MIX_GROUP = 4
HI_HALF = 0xFFFF0000
TC_VMEM_LIMIT = 48 * 1024 * 1024
EDGE_CHUNK = 512
MAX_CHUNK = 2048


def _rms(x, g):
    inv = lax.rsqrt(jnp.mean(x * x, axis=-1, keepdims=True) + NORM_EPS)
    return x * inv * g


def _pack_halves(bits):
    half = bits.shape[1] // 2
    return (bits[:, :half] >> 16) | (bits[:, half:] & jnp.uint32(HI_HALF))


def _pack_table(a):
    return _pack_halves(lax.bitcast_convert_type(a.astype(jnp.bfloat16).astype(jnp.float32), jnp.uint32))


def _gelu(x):
    return 0.5 * x * (1.0 + lax.erf(x * math.sqrt(0.5)))


def _mixer_kernel(x_ref, xh_ref, nmix_ref, win_ref, poolw_ref, pscale_ref, lng_ref, lnb_ref,
                  sguw_ref, sgub_ref, onp_ref, ons_ref, wout_ref, o_ref, pext_ref, mix_ref):
    i = pl.program_id(1)
    ts = x_ref.shape[1]
    pool_w = pscale_ref.shape[1]
    gdim = pool_w // N_POOL_GROUPS
    sgu_w = lng_ref.shape[1]
    hdim = sgu_w // SGU_HEADS

    x = x_ref[0]
    h = _rms(x, nmix_ref[...]).astype(jnp.bfloat16)
    z = jnp.dot(h, win_ref[...], preferred_element_type=jnp.float32)
    p = z[:, :pool_w]

    hh = _rms(xh_ref[0], nmix_ref[...]).astype(jnp.bfloat16)
    ph = jnp.dot(hh, win_ref[:, :pool_w], preferred_element_type=jnp.float32)
    ph = jnp.where(i > 0, ph, 0.0)
    pext_ref[0:HALO, :] = ph
    pext_ref[HALO:HALO + ts, :] = p

    pos = i * ts + lax.broadcasted_iota(jnp.int32, (ts, 1), 0)
    ssq = jnp.zeros((ts, 1), jnp.float32)
    a_parts = []
    for g, win in enumerate(POOL_WINDOWS):
        cols = slice(g * gdim, (g + 1) * gdim)
        s = pext_ref[HALO:HALO + ts, cols]
        for j in range(1, win):
            s = s + pext_ref[HALO - j:HALO - j + ts, cols]
        cnt = jnp.minimum(pos + 1, win).astype(jnp.float32)
        d = (s / cnt - p[:, cols]).astype(jnp.bfloat16)
        a = jnp.dot(d, poolw_ref[g], preferred_element_type=jnp.float32) * pscale_ref[:, cols]
        ssq = ssq + jnp.sum(a * a, axis=-1, keepdims=True)
        a_parts.append(a)
    inv_a = lax.rsqrt(ssq / pool_w + NORM_EPS)
    for g in range(N_POOL_GROUPS):
        cols = slice(g * gdim, (g + 1) * gdim)
        mix_ref[:, cols] = (a_parts[g] * inv_a * onp_ref[:, cols]).astype(jnp.bfloat16)

    gz = _gelu(z[:, pool_w:])
    tril = (lax.broadcasted_iota(jnp.int32, (SGU_CHUNK, SGU_CHUNK), 0)
            >= lax.broadcasted_iota(jnp.int32, (SGU_CHUNK, SGU_CHUNK), 1))
    ssq = jnp.zeros((ts, 1), jnp.float32)
    b_parts = []
    for hd in range(SGU_HEADS):
        cols = slice(hd * hdim, (hd + 1) * hdim)
        u = gz[:, hd * hdim:(hd + 1) * hdim]
        v = gz[:, sgu_w + hd * hdim:sgu_w + (hd + 1) * hdim]
        mu = jnp.mean(v, axis=-1, keepdims=True)
        vc = v - mu
        var = jnp.mean(vc * vc, axis=-1, keepdims=True)
        vn = (vc * lax.rsqrt(var + NORM_EPS) * lng_ref[:, cols] + lnb_ref[:, cols]).astype(jnp.bfloat16)
        w = jnp.where(tril, sguw_ref[hd], jnp.zeros((), sguw_ref.dtype))
        mixed = [jnp.dot(w, vn[n * SGU_CHUNK:(n + 1) * SGU_CHUNK], preferred_element_type=jnp.float32)
                 + sgub_ref[hd] for n in range(ts // SGU_CHUNK)]
        b = u * jnp.concatenate(mixed, axis=0)
        ssq = ssq + jnp.sum(b * b, axis=-1, keepdims=True)
        b_parts.append(b)
    inv_b = lax.rsqrt(ssq / sgu_w + NORM_EPS)
    for hd in range(SGU_HEADS):
        cols = slice(hd * hdim, (hd + 1) * hdim)
        mix_ref[:, pool_w + hd * hdim:pool_w + (hd + 1) * hdim] = (
            b_parts[hd] * inv_b * ons_ref[:, cols]).astype(jnp.bfloat16)

    o_ref[0] = x + jnp.dot(mix_ref[...], wout_ref[...], preferred_element_type=jnp.float32)


def _mixer(x, b0, nb, norm_mix, w_in, pool_w, pool_scale, ln_g, ln_b, sgu_w, sgu_b, on_pool, on_sgu, w_out):
    _, S, D = x.shape
    ts = min(MIX_TILE, S)
    pool_width = pool_scale.size
    sgu_width = ln_g.size
    in_width = w_in.shape[1]
    gdim = pool_width // N_POOL_GROUPS
    halo_blocks = ts // HALO
    full = lambda shape: pl.BlockSpec(shape, lambda b, i: (0,) * len(shape))
    return pl.pallas_call(
        _mixer_kernel,
        grid=(nb, S // ts),
        in_specs=[
            pl.BlockSpec((1, ts, D), lambda b, i: (b0 + b, i, 0)),
            pl.BlockSpec((1, HALO, D), lambda b, i: (b0 + b, jnp.maximum(i * halo_blocks - 1, 0), 0)),
            full((1, D)),
            full((D, in_width)),
            full((N_POOL_GROUPS, gdim, gdim)),
            full((1, pool_width)),
            full((1, sgu_width)),
            full((1, sgu_width)),
            full((SGU_HEADS, SGU_CHUNK, SGU_CHUNK)),
            full((SGU_HEADS, SGU_CHUNK, SGU_CHUNK)),
            full((1, pool_width)),
            full((1, sgu_width)),
            full((pool_width + sgu_width, D)),
        ],
        out_specs=pl.BlockSpec((1, ts, D), lambda b, i: (b, i, 0)),
        out_shape=jax.ShapeDtypeStruct((nb, S, D), jnp.float32),
        scratch_shapes=[
            pltpu.VMEM((HALO + ts, pool_width), jnp.float32),
            pltpu.VMEM((ts, pool_width + sgu_width), jnp.bfloat16),
        ],
        compiler_params=pltpu.CompilerParams(
            dimension_semantics=("parallel", "arbitrary"), vmem_limit_bytes=TC_VMEM_LIMIT),
        name="mixer",
    )(x, x, norm_mix.reshape(1, D), w_in.astype(jnp.bfloat16), pool_w.astype(jnp.bfloat16),
      pool_scale.reshape(1, pool_width), ln_g.reshape(1, sgu_width), ln_b.reshape(1, sgu_width),
      sgu_w.astype(jnp.bfloat16),
      jnp.broadcast_to(sgu_b[:, :, None], (SGU_HEADS, SGU_CHUNK, SGU_CHUNK)),
      on_pool.reshape(1, pool_width), on_sgu.reshape(1, sgu_width), w_out.astype(jnp.bfloat16))


def _topk_rows(s, k):
    n = s.shape[0]
    iota = lax.broadcasted_iota(jnp.int32, s.shape, 0)
    vals, idxs = [], []
    for _ in range(k):
        m = jnp.max(s, axis=0, keepdims=True)
        ix = jnp.min(jnp.where(s == m, iota, n), axis=0, keepdims=True)
        vals.append(m)
        idxs.append(ix)
        s = jnp.where(iota == ix, -jnp.inf, s)
    return vals, idxs


def _pair_candidates(v1, i1, v2, i2):
    k = PEER_TOPK
    v2c, i2c = jnp.concatenate(v2, axis=0), jnp.concatenate(i2, axis=0)
    m = v2c.shape[1]
    vals, experts, flats = [], [], []
    a = 0
    while k // (a + 1) > 1:
        rows = -(-(k // (a + 1)) // V7X_SUBLANES) * V7X_SUBLANES
        vals.append(v1[a] + v2c[:rows])
        experts.append(i1[a] * PEER_N_KEYS + i2c[:rows])
        flats.append(a * k + lax.broadcasted_iota(jnp.int32, (rows, m), 0))
        a += 1
    vals.append(jnp.concatenate(v1[a:], axis=0) + v2[0])
    experts.append(jnp.concatenate(i1[a:], axis=0) * PEER_N_KEYS + i2[0])
    flats.append((a + lax.broadcasted_iota(jnp.int32, (k - a, m), 0)) * k)
    return jnp.concatenate(vals, axis=0), jnp.concatenate(experts, axis=0), jnp.concatenate(flats, axis=0)


def _router_kernel(chunk_tokens, x_ref, nffn_ref, wq_ref, keys_ref, after_ref, h2_ref, idx_ref, slot_ref,
                   gate_ref, q_ref, idxt_ref, gatet_ref):
    del after_ref
    h2 = _rms(x_ref[...], nffn_ref[...]).astype(jnp.bfloat16)
    h2_ref[...] = h2
    q_ref[...] = jnp.dot(h2, wq_ref[...], preferred_element_type=jnp.float32).astype(jnp.bfloat16)
    dq = 2 * PEER_D_HALF
    nt = (((1,), (1,)), ((), ()))

    def head(hd, carry):
        off = pl.multiple_of(hd * dq, dq)
        s1 = lax.dot_general(keys_ref[0], q_ref[:, pl.ds(off, PEER_D_HALF)], nt,
                             preferred_element_type=jnp.float32)
        s2 = lax.dot_general(keys_ref[1], q_ref[:, pl.ds(off + PEER_D_HALF, PEER_D_HALF)], nt,
                             preferred_element_type=jnp.float32)
        v1, i1 = _topk_rows(s1, PEER_TOPK)
        v2, i2 = _topk_rows(s2, PEER_TOPK)
        cand, expert, flat = _pair_candidates(v1, i1, v2, i2)
        cv, ce = [], []
        for _ in range(PEER_TOPK):
            m = jnp.max(cand, axis=0, keepdims=True)
            ix = jnp.min(jnp.where(cand == m, flat, PEER_TOPK * PEER_TOPK), axis=0, keepdims=True)
            hit = flat == ix
            cv.append(m)
            ce.append(jnp.max(jnp.where(hit, expert, -1), axis=0, keepdims=True))
            cand = jnp.where(hit, -jnp.inf, cand)
        cvc = jnp.concatenate(cv, axis=0)
        e = jnp.exp(cvc - cv[0])
        gate = e / jnp.sum(e, axis=0, keepdims=True)
        row = pl.multiple_of(hd * PEER_TOPK, PEER_TOPK)
        idxt_ref[pl.ds(row, PEER_TOPK), :] = jnp.concatenate(ce, axis=0)
        gatet_ref[pl.ds(row, PEER_TOPK), :] = gate
        return carry

    lax.fori_loop(0, PEER_HEADS, head, 0)
    idx = idxt_ref[...].T
    idx_ref[...] = idx
    tok = pl.program_id(0) * idx.shape[0] + lax.broadcasted_iota(jnp.int32, idx.shape, 0)
    slot_ref[...] = _tile_linear_index(_score_word_row(tok, chunk_tokens), idx, PEER_N_KEYS * PEER_N_KEYS)
    gate_ref[...] = gatet_ref[...].T


def _router(x1, norm_ffn, wq, keys, tok0, T, after):
    D = x1.shape[1]
    tr = min(ROUTE_TILE, T)
    qw = wq.shape[1]
    assert tok0 % tr == 0 and T % tr == 0
    first = tok0 // tr
    full = lambda shape: pl.BlockSpec(shape, lambda i: (0,) * len(shape))
    return pl.pallas_call(
        functools.partial(_router_kernel, T),
        grid=(T // tr,),
        in_specs=[
            pl.BlockSpec((tr, D), lambda i: (first + i, 0)),
            full((1, D)),
            full((D, qw)),
            full((2, PEER_N_KEYS, PEER_D_HALF)),
            pl.BlockSpec(memory_space=pl.ANY),
        ],
        out_specs=[
            pl.BlockSpec((tr, D), lambda i: (i, 0)),
            pl.BlockSpec((tr, EXPERTS_PER_TOKEN), lambda i: (i, 0)),
            pl.BlockSpec((tr, EXPERTS_PER_TOKEN), lambda i: (i, 0)),
            pl.BlockSpec((tr, EXPERTS_PER_TOKEN), lambda i: (i, 0)),
        ],
        out_shape=[
            jax.ShapeDtypeStruct((T, D), jnp.bfloat16),
            jax.ShapeDtypeStruct((T, EXPERTS_PER_TOKEN), jnp.int32),
            jax.ShapeDtypeStruct((T, EXPERTS_PER_TOKEN), jnp.int32),
            jax.ShapeDtypeStruct((T, EXPERTS_PER_TOKEN), jnp.float32),
        ],
        scratch_shapes=[
            pltpu.VMEM((tr, qw), jnp.bfloat16),
            pltpu.VMEM((EXPERTS_PER_TOKEN, tr), jnp.int32),
            pltpu.VMEM((EXPERTS_PER_TOKEN, tr), jnp.float32),
        ],
        compiler_params=pltpu.CompilerParams(
            dimension_semantics=("parallel",), vmem_limit_bytes=TC_VMEM_LIMIT),
        name="router",
    )(x1, norm_ffn.reshape(1, D), wq, keys, after)


def _tile_linear_index(row, col, ncols):
    return ((row >> 3) * (ncols * V7X_SUBLANES) + (col >> 7) * (V7X_SUBLANES * V7X_LANES)
            + (row & (V7X_SUBLANES - 1)) * V7X_LANES + (col & (V7X_LANES - 1)))


def _score_tile(tokens):
    return min(SCORE_TOK_TILE, tokens)


def _score_word_row(tok, tokens):
    tm = _score_tile(tokens)
    assert tm & (tm - 1) == 0, "power-of-two score tile"
    return (tok >> int(math.log2(tm))) * (tm // 2) + (tok & (tm // 2 - 1))


def _scores_kernel(h_ref, ut_ref, o_ref):
    acc = jnp.dot(h_ref[...], ut_ref[...], preferred_element_type=jnp.float32)
    tm, tn = acc.shape
    bits = pltpu.bitcast(acc.astype(jnp.bfloat16).astype(jnp.float32), jnp.uint32)
    words = (bits[:tm // 2] >> 16) | (bits[tm // 2:] & jnp.uint32(HI_HALF))
    for n in range(tn // V7X_LANES):
        o_ref[:, n * V7X_SUBLANES:(n + 1) * V7X_SUBLANES, :] = words[:, n * V7X_LANES:(n + 1) * V7X_LANES].reshape(
            tm // 2 // V7X_SUBLANES, V7X_SUBLANES, V7X_LANES)


def _dense_scores(h2, u_t):
    T, D = h2.shape
    E = u_t.shape[1]
    tm, tn = _score_tile(T), min(SCORE_EXP_TILE, E)
    assert T % tm == 0 and E % tn == 0 and tm % (2 * V7X_SUBLANES) == 0 and tn % V7X_LANES == 0
    out = pl.pallas_call(
        _scores_kernel,
        grid=(T // tm, E // tn),
        in_specs=[pl.BlockSpec((tm, D), lambda i, j: (i, 0)), pl.BlockSpec((D, tn), lambda i, j: (0, j))],
        out_specs=pl.BlockSpec((tm // 2 // V7X_SUBLANES, tn // V7X_LANES * V7X_SUBLANES, V7X_LANES),
                               lambda i, j: (i, j, 0)),
        out_shape=jax.ShapeDtypeStruct((T // 2 // V7X_SUBLANES, E // V7X_LANES * V7X_SUBLANES, V7X_LANES),
                                       jnp.uint32),
        compiler_params=pltpu.CompilerParams(
            dimension_semantics=("parallel", "arbitrary"), vmem_limit_bytes=TC_VMEM_LIMIT),
        name="dense_scores",
    )(h2, u_t)
    return out.reshape(-1)


def _act_kernel(pre_ref, gate_ref, w_ref):
    w = (gate_ref[...] * _gelu(pre_ref[...])).astype(jnp.bfloat16).astype(jnp.float32)
    hi = pltpu.bitcast(w, jnp.uint32) & jnp.uint32(HI_HALF)
    w_ref[...] = pltpu.bitcast(hi | (hi >> 16), jnp.int32)


def _expert_weights(pre, gate):
    T, K = pre.shape
    te = min(EW_TILE, T)
    spec = pl.BlockSpec((te, K), lambda i: (i, 0))
    return pl.pallas_call(
        _act_kernel, grid=(T // te,), in_specs=[spec, spec], out_specs=spec,
        out_shape=jax.ShapeDtypeStruct((T, K), jnp.int32),
        compiler_params=pltpu.CompilerParams(dimension_semantics=("parallel",)),
        name="expert_weights",
    )(pre, gate)


def _final_kernel(x_ref, y_ref, g_ref, *rest):
    o_ref = rest[-1]
    o_ref[...] = _rms(x_ref[...] + y_ref[...], g_ref[...])


def _final(x1, x_tok0, peer, norm_final, out_prev, out_tok0, total):
    Tc, D = peer.shape
    te = min(EW_TILE, Tc)
    assert x_tok0 % te == 0 and out_tok0 % te == 0 and Tc % te == 0
    nblk = Tc // te
    spec = pl.BlockSpec((te, D), lambda i: (out_tok0 // te + i, 0))
    in_specs = [pl.BlockSpec((te, D), lambda i: (x_tok0 // te + i, 0)),
                pl.BlockSpec((te, D), lambda i: (i, 0)), pl.BlockSpec((1, D), lambda i: (0, 0))]
    args = [x1, peer, norm_final.reshape(1, D)]
    aliases = {}
    if out_prev is not None:
        in_specs.append(pl.BlockSpec(memory_space=pl.ANY))
        args.append(out_prev)
        aliases = {3: 0}
    return pl.pallas_call(
        _final_kernel, grid=(nblk,), in_specs=in_specs, out_specs=spec,
        out_shape=jax.ShapeDtypeStruct((total, D), jnp.float32),
        input_output_aliases=aliases,
        compiler_params=pltpu.CompilerParams(dimension_semantics=("parallel",)),
        name="final_norm",
    )(*args)


def _tree_sum(vals):
    while len(vals) > 1:
        nxt = [vals[i] + vals[i + 1] for i in range(0, len(vals) - 1, 2)]
        if len(vals) % 2:
            nxt.append(vals[-1])
        vals = nxt
    return vals[0]


def _sc_block_pipeline(nblk, items_per_token, loads, store, gather, compute):
    assert items_per_token % 2 == 0 and nblk >= 1

    for c in loads(0, 0):
        c.start()
    for c in loads(0, 0):
        c.wait()
    if nblk > 1:
        for c in loads(1, 1):
            c.start()
    gather(0, 0, 0, 0).start()

    @pl.loop(0, nblk)
    def _(b):
        slot = b % 2

        @pl.when(b >= 2)
        def _():
            store(b - 2, slot).wait()

        @pl.loop(0, SC_TOKENS)
        def _(t):
            for q in range(items_per_token):
                buf = q % 2
                if q + 1 < items_per_token:
                    gather(slot, t, q + 1, 1 - buf).start()
                else:
                    @pl.when(t + 1 < SC_TOKENS)
                    def _():
                        gather(slot, t + 1, 0, 1 - buf).start()

                    @pl.when(jnp.logical_and(t + 1 == SC_TOKENS, b + 1 < nblk))
                    def _():
                        for c in loads(b + 1, 1 - slot):
                            c.wait()
                        gather(1 - slot, 0, 0, 1 - buf).start()

                gather(slot, t, q, buf).wait()
                compute(slot, t, q, buf)

        store(b, slot).start()

        @pl.when(b + 2 < nblk)
        def _():
            for c in loads(b + 2, slot):
                c.start()

    if nblk >= 2:
        store(nblk - 2, nblk % 2).wait()
    store(nblk - 1, (nblk - 1) % 2).wait()


def _sc_mesh():
    return plsc.VectorSubcoreMesh(core_axis_name="c", subcore_axis_name="s")


def _sc_worker_id():
    return lax.axis_index("s") * V7X_SC_CORES + lax.axis_index("c")


def _sc_bf16(words):
    return plsc.bitcast(words, jnp.bfloat16)


def _sc_halves_f32(pairs):
    words = plsc.bitcast(pairs, jnp.uint32)
    return (plsc.bitcast(words << 16, jnp.float32),
            plsc.bitcast(words & jnp.uint32(HI_HALF), jnp.float32))


def _pick_parts(T, K):
    L = V7X_SC_LANES
    nb = T // V7X_SC_WORKERS
    half_tile = _score_tile(T) // 2
    assert T % V7X_SC_WORKERS == 0 and nb <= SC_PICK_TOKENS
    assert half_tile % nb == 0 and half_tile & (half_tile - 1) == 0
    scratch = [pltpu.VMEM((nb, K), jnp.int32), pltpu.VMEM((nb, K), jnp.uint32),
               pltpu.VMEM((nb, K), jnp.float32), pltpu.SemaphoreType.DMA(())]

    def gather(flat_hbm, idx_v, word_v, sem, t):
        return pltpu.make_async_copy(flat_hbm.at[idx_v.at[t]], word_v.at[t], sem)

    def start(flat_hbm, slots_hbm, idx_v, word_v, val_v, sem):
        pltpu.sync_copy(slots_hbm.at[pl.ds(_sc_worker_id() * nb, nb)], idx_v)

        @pl.loop(0, nb)
        def _(t):
            gather(flat_hbm, idx_v, word_v, sem, t).start()

    def finish(flat_hbm, out_hbm, idx_v, word_v, val_v, sem):
        base = _sc_worker_id() * nb
        shift = jnp.full((L,), jnp.where((base & half_tile) == 0, 16, 0), jnp.uint32)

        @pl.loop(0, nb)
        def _(t):
            gather(flat_hbm, idx_v, word_v, sem, t).wait()
            for j in range(K // L):
                words = word_v[t, pl.ds(j * L, L)]
                val_v[t, pl.ds(j * L, L)] = plsc.bitcast((words << shift) & jnp.uint32(HI_HALF), jnp.float32)

        pltpu.sync_copy(val_v, out_hbm.at[pl.ds(base, nb)])

    return scratch, start, finish


def _mix_parts(T, K, DW):
    D = 2 * DW
    L, G = V7X_SC_LANES, SC_GATHER
    nj = SC_MIX_CHUNK // L
    tok_per_w = T // V7X_SC_WORKERS
    assert T % (V7X_SC_WORKERS * SC_TOKENS) == 0 and K % (2 * G) == 0
    assert DW % SC_MIX_CHUNK == 0 and G % SC_MIX_GROUP == 0
    scratch = [
        pltpu.VMEM((2, SC_TOKENS, K), jnp.int32),
        pltpu.VMEM((2, SC_TOKENS, K), jnp.int32),
        pltpu.VMEM((2, G, DW), jnp.uint32),
        pltpu.VMEM((2, SC_TOKENS, D), jnp.float32),
        pltpu.SemaphoreType.DMA((2,)),
        pltpu.SemaphoreType.DMA((2,)),
        pltpu.SemaphoreType.DMA((2,)),
        pltpu.SemaphoreType.DMA((2,)),
    ]

    def run(w_hbm, idx_hbm, tab_hbm, out_hbm, idx_v, w_v, rows_v, out_v, idx_sems, w_sems, out_sems, row_sems):
        base = _sc_worker_id() * tok_per_w

        def loads(b, slot):
            toks = pl.ds(base + b * SC_TOKENS, SC_TOKENS)
            return [pltpu.make_async_copy(idx_hbm.at[toks], idx_v.at[slot], idx_sems.at[slot]),
                    pltpu.make_async_copy(w_hbm.at[toks], w_v.at[slot], w_sems.at[slot])]

        def store(b, slot):
            toks = pl.ds(base + b * SC_TOKENS, SC_TOKENS)
            return pltpu.make_async_copy(out_v.at[slot], out_hbm.at[toks], out_sems.at[slot])

        def gather(slot, t, q, buf):
            return pltpu.make_async_copy(
                tab_hbm.at[idx_v.at[slot, t, pl.ds(q * G, G)]], rows_v.at[buf], row_sems.at[buf])

        def compute(slot, t, q, buf):
            ssplat = jnp.full((L,), slot, jnp.int32)
            tsplat = jnp.full((L,), t, jnp.int32)
            for c in range(DW // SC_MIX_CHUNK):
                def body(kg, acc):
                    kk = kg * SC_MIX_GROUP
                    wks = [_sc_bf16(plsc.load_gather(
                        w_v, [ssplat, tsplat, jnp.full((L,), q * G + i, jnp.int32) + kk]))
                        for i in range(SC_MIX_GROUP)]
                    out = []
                    for j in range(nj):
                        prods = [wks[i] * _sc_bf16(rows_v[buf, kk + i, pl.ds(c * SC_MIX_CHUNK + j * L, L)])
                                 for i in range(SC_MIX_GROUP)]
                        lo, hi = _sc_halves_f32(_tree_sum(prods))
                        out += [acc[2 * j] + lo, acc[2 * j + 1] + hi]
                    return tuple(out)

                zero = jnp.zeros((L,), jnp.float32)
                acc = plsc.parallel_loop(0, G // SC_MIX_GROUP, carry=(zero,) * (2 * nj))(body)
                for j in range(nj):
                    for half in range(2):
                        dst = out_v.at[slot, t, pl.ds(half * DW + c * SC_MIX_CHUNK + j * L, L)]
                        if q == 0:
                            dst[...] = acc[2 * j + half]
                        else:
                            plsc.addupdate(dst, acc[2 * j + half])

        _sc_block_pipeline(tok_per_w // SC_TOKENS, K // G, loads, store, gather, compute)

    return scratch, run


def _sc_kernel(out_type, scratch_types, name):
    return functools.partial(
        pl.kernel, mesh=_sc_mesh(), out_type=out_type, scratch_types=scratch_types,
        compiler_params=pltpu.CompilerParams(needs_layout_passes=False), name=name)


def _pick_scores(flat, slots):
    T, K = slots.shape
    scratch, start, finish = _pick_parts(T, K)

    @_sc_kernel(jax.ShapeDtypeStruct((T, K), jnp.float32), scratch, "pick_scores")
    def k(flat_hbm, slots_hbm, pre_hbm, *pick_scratch):
        start(flat_hbm, slots_hbm, *pick_scratch)
        finish(flat_hbm, pre_hbm, *pick_scratch)

    return k(flat, slots)


def _expert_mix(w, idx, table):
    T, K = w.shape
    DW = table.shape[1]
    scratch, run = _mix_parts(T, K, DW)

    @_sc_kernel(jax.ShapeDtypeStruct((T, 2 * DW), jnp.float32), scratch, "expert_mix")
    def k(w_hbm, idx_hbm, tab_hbm, out_hbm, *mix_scratch):
        run(w_hbm, idx_hbm, tab_hbm, out_hbm, *mix_scratch)

    return k(w, idx, table)


def _mix_and_pick(w, idx, table, flat, slots):
    Tm, K = w.shape
    Tp = slots.shape[0]
    DW = table.shape[1]
    mix_scratch, run = _mix_parts(Tm, K, DW)
    pick_scratch, start, finish = _pick_parts(Tp, K)
    n_mix = len(mix_scratch)

    @_sc_kernel((jax.ShapeDtypeStruct((Tm, 2 * DW), jnp.float32), jax.ShapeDtypeStruct((Tp, K), jnp.float32)),
                mix_scratch + pick_scratch, "mix_and_pick")
    def k(w_hbm, idx_hbm, tab_hbm, flat_hbm, slots_hbm, out_hbm, pre_hbm, *scratch):
        start(flat_hbm, slots_hbm, *scratch[n_mix:])
        run(w_hbm, idx_hbm, tab_hbm, out_hbm, *scratch[:n_mix])
        finish(flat_hbm, pre_hbm, *scratch[n_mix:])

    return k(w, idx, table, flat, slots)


def _chunk_sizes(total):
    ramp, size = [EDGE_CHUNK], EDGE_CHUNK
    while size < MAX_CHUNK:
        ramp.append(size)
        size *= 2
    middle = total - 2 * sum(ramp)
    assert middle >= 0 and middle % MAX_CHUNK == 0
    return ramp + [MAX_CHUNK] * (middle // MAX_CHUNK) + ramp[::-1]


def kernel(x, norm_mix, w_in, pool_w, pool_scale, sgu_ln_g, sgu_ln_b, sgu_w, sgu_b, out_norm_pool,
           out_norm_sgu, w_out, norm_ffn, peer_wq, peer_keys, peer_u, peer_v, norm_final):
    B, S, D = x.shape
    assert norm_mix.shape[0] == 1, "single-layer block"
    T = B * S
    mix_args = (norm_mix[0], w_in[0], pool_w[0], pool_scale[0], sgu_ln_g[0], sgu_ln_b[0],
                sgu_w[0], sgu_b[0], out_norm_pool[0], out_norm_sgu[0], w_out[0])
    x1_parts = [(b * S, _mixer(x, b, 1, *mix_args).reshape(S, D)) for b in range(B)]
    wq = peer_wq[0].astype(jnp.bfloat16)
    keys = peer_keys[0].astype(jnp.bfloat16)
    u_t = peer_u[0].astype(jnp.bfloat16).T
    v_tab = _pack_table(peer_v[0])
    chunks = []
    ws = [norm_ffn, v_tab]
    out = None
    tok0 = 0

    def finish_chunk(out, chunk, peer):
        x1, x_tok0, out_tok0 = chunk[:3]
        return _final(x1, x_tok0, peer, norm_final, out, out_tok0, T)

    for tc in _chunk_sizes(T):
        part0, x1 = [p for p in x1_parts if p[0] <= tok0][-1]
        assert tok0 + tc <= part0 + x1.shape[0], "a token chunk must lie inside one mixer call"
        h2, idx, slots, gate = _router(x1, norm_ffn[0], wq, keys, tok0 - part0, tc, ws[-2])
        flat = _dense_scores(h2, u_t)
        if len(chunks) >= 2:
            mixed = chunks[-2]
            peer, pre = _mix_and_pick(mixed[4], mixed[3], v_tab, flat, slots)
            out = finish_chunk(out, mixed, peer)
        else:
            pre = _pick_scores(flat, slots)
        w = _expert_weights(pre, gate)
        ws.append(w)
        chunks.append((x1, tok0 - part0, tok0, idx, w))
        tok0 += tc
    for mixed in chunks[-2:]:
        out = finish_chunk(out, mixed, _expert_mix(mixed[4], mixed[3], v_tab))
    return out.reshape(B, S, D)
```

```python
import functools
import math

import jax
import jax.numpy as jnp
from jax import lax
from jax.experimental import pallas as pl
from jax.experimental.pallas import tpu as pltpu
from jax.experimental.pallas import tpu_sc as plsc

POOL_WINDOWS = (2, 4, 8, 16)
N_POOL_GROUPS = len(POOL_WINDOWS)
SGU_HEADS = 4
SGU_CHUNK = 128
PEER_HEADS = 8
PEER_N_KEYS = 128
PEER_D_HALF = 128
PEER_TOPK = 16
NORM_EPS = 1e-6
EXPERTS_PER_TOKEN = PEER_HEADS * PEER_TOPK

V7X_LANES = 128
V7X_SUBLANES = 8
V7X_SC_CORES = 2
V7X_SC_SUBCORES = 16
V7X_SC_LANES = 16
V7X_SC_WORKERS = V7X_SC_CORES * V7X_SC_SUBCORES

HALO = max(POOL_WINDOWS)
MIX_TILE = 512
ROUTE_TILE = 1024
EW_TILE = 512
SCORE_TOK_TILE = 2048
SCORE_EXP_TILE = 1024
SC_PICK_TOKENS = 64
SC_GATHER = 64
SC_TOKENS = 8
SC_MIX_CHUNK = 128
SC_MIX_GROUP = 4
HI_HALF = 0xFFFF0000
TC_VMEM_LIMIT = 48 * 1024 * 1024
EDGE_CHUNK = 512
MAX_CHUNK = 2048


def _rms(x, g):
    inv = lax.rsqrt(jnp.mean(x * x, axis=-1, keepdims=True) + NORM_EPS)
    return x * inv * g


def _pack_halves(bits):
    half = bits.shape[1] // 2
    return (bits[:, :half] >> 16) | (bits[:, half:] & jnp.uint32(HI_HALF))


def _pack_table(a):
    return _pack_halves(lax.bitcast_convert_type(a.astype(jnp.bfloat16).astype(jnp.float32), jnp.uint32))


def _gelu(x):
    return 0.5 * x * (1.0 + lax.erf(x * math.sqrt(0.5)))


def _mixer_kernel(x_ref, xh_ref, nmix_ref, win_ref, poolw_ref, pscale_ref, lng_ref, lnb_ref,
                  sguw_ref, sgub_ref, onp_ref, ons_ref, wout_ref, o_ref, pext_ref, mix_ref):
    i = pl.program_id(1)
    ts = x_ref.shape[1]
    pool_w = pscale_ref.shape[1]
    gdim = pool_w // N_POOL_GROUPS
    sgu_w = lng_ref.shape[1]
    hdim = sgu_w // SGU_HEADS

    x = x_ref[0]
    h = _rms(x, nmix_ref[...]).astype(jnp.bfloat16)
    z = jnp.dot(h, win_ref[...], preferred_element_type=jnp.float32)
    p = z[:, :pool_w]

    hh = _rms(xh_ref[0], nmix_ref[...]).astype(jnp.bfloat16)
    ph = jnp.dot(hh, win_ref[:, :pool_w], preferred_element_type=jnp.float32)
    ph = jnp.where(i > 0, ph, 0.0)
    pext_ref[0:HALO, :] = ph
    pext_ref[HALO:HALO + ts, :] = p

    pos = i * ts + lax.broadcasted_iota(jnp.int32, (ts, 1), 0)
    ssq = jnp.zeros((ts, 1), jnp.float32)
    a_parts = []
    for g, win in enumerate(POOL_WINDOWS):
        cols = slice(g * gdim, (g + 1) * gdim)
        s = pext_ref[HALO:HALO + ts, cols]
        for j in range(1, win):
            s = s + pext_ref[HALO - j:HALO - j + ts, cols]
        cnt = jnp.minimum(pos + 1, win).astype(jnp.float32)
        d = (s / cnt - p[:, cols]).astype(jnp.bfloat16)
        a = jnp.dot(d, poolw_ref[g], preferred_element_type=jnp.float32) * pscale_ref[:, cols]
        ssq = ssq + jnp.sum(a * a, axis=-1, keepdims=True)
        a_parts.append(a)
    inv_a = lax.rsqrt(ssq / pool_w + NORM_EPS)
    for g in range(N_POOL_GROUPS):
        cols = slice(g * gdim, (g + 1) * gdim)
        mix_ref[:, cols] = (a_parts[g] * inv_a * onp_ref[:, cols]).astype(jnp.bfloat16)

    gz = _gelu(z[:, pool_w:])
    tril = (lax.broadcasted_iota(jnp.int32, (SGU_CHUNK, SGU_CHUNK), 0)
            >= lax.broadcasted_iota(jnp.int32, (SGU_CHUNK, SGU_CHUNK), 1))
    ssq = jnp.zeros((ts, 1), jnp.float32)
    b_parts = []
    for hd in range(SGU_HEADS):
        cols = slice(hd * hdim, (hd + 1) * hdim)
        u = gz[:, hd * hdim:(hd + 1) * hdim]
        v = gz[:, sgu_w + hd * hdim:sgu_w + (hd + 1) * hdim]
        mu = jnp.mean(v, axis=-1, keepdims=True)
        vc = v - mu
        var = jnp.mean(vc * vc, axis=-1, keepdims=True)
        vn = (vc * lax.rsqrt(var + NORM_EPS) * lng_ref[:, cols] + lnb_ref[:, cols]).astype(jnp.bfloat16)
        w = jnp.where(tril, sguw_ref[hd], jnp.zeros((), sguw_ref.dtype))
        mixed = [jnp.dot(w, vn[n * SGU_CHUNK:(n + 1) * SGU_CHUNK], preferred_element_type=jnp.float32)
                 + sgub_ref[hd] for n in range(ts // SGU_CHUNK)]
        b = u * jnp.concatenate(mixed, axis=0)
        ssq = ssq + jnp.sum(b * b, axis=-1, keepdims=True)
        b_parts.append(b)
    inv_b = lax.rsqrt(ssq / sgu_w + NORM_EPS)
    for hd in range(SGU_HEADS):
        cols = slice(hd * hdim, (hd + 1) * hdim)
        mix_ref[:, pool_w + hd * hdim:pool_w + (hd + 1) * hdim] = (
            b_parts[hd] * inv_b * ons_ref[:, cols]).astype(jnp.bfloat16)

    o_ref[0] = x + jnp.dot(mix_ref[...], wout_ref[...], preferred_element_type=jnp.float32)


def _mixer(x, b0, nb, norm_mix, w_in, pool_w, pool_scale, ln_g, ln_b, sgu_w, sgu_b, on_pool, on_sgu, w_out):
    _, S, D = x.shape
    ts = min(MIX_TILE, S)
    pool_width = pool_scale.size
    sgu_width = ln_g.size
    in_width = w_in.shape[1]
    gdim = pool_width // N_POOL_GROUPS
    halo_blocks = ts // HALO
    full = lambda shape: pl.BlockSpec(shape, lambda b, i: (0,) * len(shape))
    return pl.pallas_call(
        _mixer_kernel,
        grid=(nb, S // ts),
        in_specs=[
            pl.BlockSpec((1, ts, D), lambda b, i: (b0 + b, i, 0)),
            pl.BlockSpec((1, HALO, D), lambda b, i: (b0 + b, jnp.maximum(i * halo_blocks - 1, 0), 0)),
            full((1, D)),
            full((D, in_width)),
            full((N_POOL_GROUPS, gdim, gdim)),
            full((1, pool_width)),
            full((1, sgu_width)),
            full((1, sgu_width)),
            full((SGU_HEADS, SGU_CHUNK, SGU_CHUNK)),
            full((SGU_HEADS, SGU_CHUNK, SGU_CHUNK)),
            full((1, pool_width)),
            full((1, sgu_width)),
            full((pool_width + sgu_width, D)),
        ],
        out_specs=pl.BlockSpec((1, ts, D), lambda b, i: (b, i, 0)),
        out_shape=jax.ShapeDtypeStruct((nb, S, D), jnp.float32),
        scratch_shapes=[
            pltpu.VMEM((HALO + ts, pool_width), jnp.float32),
            pltpu.VMEM((ts, pool_width + sgu_width), jnp.bfloat16),
        ],
        compiler_params=pltpu.CompilerParams(
            dimension_semantics=("parallel", "arbitrary"), vmem_limit_bytes=TC_VMEM_LIMIT),
        name="mixer",
    )(x, x, norm_mix.reshape(1, D), w_in.astype(jnp.bfloat16), pool_w.astype(jnp.bfloat16),
      pool_scale.reshape(1, pool_width), ln_g.reshape(1, sgu_width), ln_b.reshape(1, sgu_width),
      sgu_w.astype(jnp.bfloat16),
      jnp.broadcast_to(sgu_b[:, :, None], (SGU_HEADS, SGU_CHUNK, SGU_CHUNK)),
      on_pool.reshape(1, pool_width), on_sgu.reshape(1, sgu_width), w_out.astype(jnp.bfloat16))


def _topk_rows(s, k):
    n = s.shape[0]
    iota = lax.broadcasted_iota(jnp.int32, s.shape, 0)
    vals, idxs = [], []
    for _ in range(k):
        m = jnp.max(s, axis=0, keepdims=True)
        ix = jnp.min(jnp.where(s == m, iota, n), axis=0, keepdims=True)
        vals.append(m)
        idxs.append(ix)
        s = jnp.where(iota == ix, -jnp.inf, s)
    return vals, idxs


def _pair_candidates(v1, i1, v2, i2):
    k = PEER_TOPK
    v2c, i2c = jnp.concatenate(v2, axis=0), jnp.concatenate(i2, axis=0)
    m = v2c.shape[1]
    vals, experts, flats = [], [], []
    a = 0
    while k // (a + 1) > 1:
        rows = -(-(k // (a + 1)) // V7X_SUBLANES) * V7X_SUBLANES
        vals.append(v1[a] + v2c[:rows])
        experts.append(i1[a] * PEER_N_KEYS + i2c[:rows])
        flats.append(a * k + lax.broadcasted_iota(jnp.int32, (rows, m), 0))
        a += 1
    vals.append(jnp.concatenate(v1[a:], axis=0) + v2[0])
    experts.append(jnp.concatenate(i1[a:], axis=0) * PEER_N_KEYS + i2[0])
    flats.append((a + lax.broadcasted_iota(jnp.int32, (k - a, m), 0)) * k)
    return jnp.concatenate(vals, axis=0), jnp.concatenate(experts, axis=0), jnp.concatenate(flats, axis=0)


def _router_kernel(chunk_tokens, x_ref, nffn_ref, wq_ref, keys_ref, after_ref, h2_ref, idx_ref, slot_ref,
                   gate_ref, q_ref, idxt_ref, gatet_ref):
    del after_ref
    h2 = _rms(x_ref[...], nffn_ref[...]).astype(jnp.bfloat16)
    h2_ref[...] = h2
    q_ref[...] = jnp.dot(h2, wq_ref[...], preferred_element_type=jnp.float32).astype(jnp.bfloat16)
    dq = 2 * PEER_D_HALF
    nt = (((1,), (1,)), ((), ()))

    def head(hd, carry):
        off = pl.multiple_of(hd * dq, dq)
        s1 = lax.dot_general(keys_ref[0], q_ref[:, pl.ds(off, PEER_D_HALF)], nt,
                             preferred_element_type=jnp.float32)
        s2 = lax.dot_general(keys_ref[1], q_ref[:, pl.ds(off + PEER_D_HALF, PEER_D_HALF)], nt,
                             preferred_element_type=jnp.float32)
        v1, i1 = _topk_rows(s1, PEER_TOPK)
        v2, i2 = _topk_rows(s2, PEER_TOPK)
        cand, expert, flat = _pair_candidates(v1, i1, v2, i2)
        cv, ce = [], []
        for _ in range(PEER_TOPK):
            m = jnp.max(cand, axis=0, keepdims=True)
            ix = jnp.min(jnp.where(cand == m, flat, PEER_TOPK * PEER_TOPK), axis=0, keepdims=True)
            hit = flat == ix
            cv.append(m)
            ce.append(jnp.max(jnp.where(hit, expert, -1), axis=0, keepdims=True))
            cand = jnp.where(hit, -jnp.inf, cand)
        cvc = jnp.concatenate(cv, axis=0)
        e = jnp.exp(cvc - cv[0])
        gate = e / jnp.sum(e, axis=0, keepdims=True)
        row = pl.multiple_of(hd * PEER_TOPK, PEER_TOPK)
        idxt_ref[pl.ds(row, PEER_TOPK), :] = jnp.concatenate(ce, axis=0)
        gatet_ref[pl.ds(row, PEER_TOPK), :] = gate
        return carry

    lax.fori_loop(0, PEER_HEADS, head, 0)
    idx = idxt_ref[...].T
    idx_ref[...] = idx
    tok = pl.program_id(0) * idx.shape[0] + lax.broadcasted_iota(jnp.int32, idx.shape, 0)
    slot_ref[...] = _tile_linear_index(_score_word_row(tok, chunk_tokens), idx, PEER_N_KEYS * PEER_N_KEYS)
    gate_ref[...] = gatet_ref[...].T


def _router(x1, norm_ffn, wq, keys, tok0, T, after):
    D = x1.shape[1]
    tr = min(ROUTE_TILE, T)
    qw = wq.shape[1]
    assert tok0 % tr == 0 and T % tr == 0
    first = tok0 // tr
    full = lambda shape: pl.BlockSpec(shape, lambda i: (0,) * len(shape))
    return pl.pallas_call(
        functools.partial(_router_kernel, T),
        grid=(T // tr,),
        in_specs=[
            pl.BlockSpec((tr, D), lambda i: (first + i, 0)),
            full((1, D)),
            full((D, qw)),
            full((2, PEER_N_KEYS, PEER_D_HALF)),
            pl.BlockSpec(memory_space=pl.ANY),
        ],
        out_specs=[
            pl.BlockSpec((tr, D), lambda i: (i, 0)),
            pl.BlockSpec((tr, EXPERTS_PER_TOKEN), lambda i: (i, 0)),
            pl.BlockSpec((tr, EXPERTS_PER_TOKEN), lambda i: (i, 0)),
            pl.BlockSpec((tr, EXPERTS_PER_TOKEN), lambda i: (i, 0)),
        ],
        out_shape=[
            jax.ShapeDtypeStruct((T, D), jnp.bfloat16),
            jax.ShapeDtypeStruct((T, EXPERTS_PER_TOKEN), jnp.int32),
            jax.ShapeDtypeStruct((T, EXPERTS_PER_TOKEN), jnp.int32),
            jax.ShapeDtypeStruct((T, EXPERTS_PER_TOKEN), jnp.float32),
        ],
        scratch_shapes=[
            pltpu.VMEM((tr, qw), jnp.bfloat16),
            pltpu.VMEM((EXPERTS_PER_TOKEN, tr), jnp.int32),
            pltpu.VMEM((EXPERTS_PER_TOKEN, tr), jnp.float32),
        ],
        compiler_params=pltpu.CompilerParams(
            dimension_semantics=("parallel",), vmem_limit_bytes=TC_VMEM_LIMIT),
        name="router",
    )(x1, norm_ffn.reshape(1, D), wq, keys, after)


def _tile_linear_index(row, col, ncols):
    return ((row >> 3) * (ncols * V7X_SUBLANES) + (col >> 7) * (V7X_SUBLANES * V7X_LANES)
            + (row & (V7X_SUBLANES - 1)) * V7X_LANES + (col & (V7X_LANES - 1)))


def _score_tile(tokens):
    return min(SCORE_TOK_TILE, tokens)


def _score_word_row(tok, tokens):
    tm = _score_tile(tokens)
    assert tm & (tm - 1) == 0, "power-of-two score tile"
    return (tok >> int(math.log2(tm))) * (tm // 2) + (tok & (tm // 2 - 1))


def _scores_kernel(h_ref, ut_ref, o_ref):
    acc = jnp.dot(h_ref[...], ut_ref[...], preferred_element_type=jnp.float32)
    tm, tn = acc.shape
    bits = pltpu.bitcast(acc.astype(jnp.bfloat16).astype(jnp.float32), jnp.uint32)
    words = (bits[:tm // 2] >> 16) | (bits[tm // 2:] & jnp.uint32(HI_HALF))
    for n in range(tn // V7X_LANES):
        o_ref[:, n * V7X_SUBLANES:(n + 1) * V7X_SUBLANES, :] = words[:, n * V7X_LANES:(n + 1) * V7X_LANES].reshape(
            tm // 2 // V7X_SUBLANES, V7X_SUBLANES, V7X_LANES)


def _dense_scores(h2, u_t):
    T, D = h2.shape
    E = u_t.shape[1]
    tm, tn = _score_tile(T), min(SCORE_EXP_TILE, E)
    assert T % tm == 0 and E % tn == 0 and tm % (2 * V7X_SUBLANES) == 0 and tn % V7X_LANES == 0
    out = pl.pallas_call(
        _scores_kernel,
        grid=(T // tm, E // tn),
        in_specs=[pl.BlockSpec((tm, D), lambda i, j: (i, 0)), pl.BlockSpec((D, tn), lambda i, j: (0, j))],
        out_specs=pl.BlockSpec((tm // 2 // V7X_SUBLANES, tn // V7X_LANES * V7X_SUBLANES, V7X_LANES),
                               lambda i, j: (i, j, 0)),
        out_shape=jax.ShapeDtypeStruct((T // 2 // V7X_SUBLANES, E // V7X_LANES * V7X_SUBLANES, V7X_LANES),
                                       jnp.uint32),
        compiler_params=pltpu.CompilerParams(
            dimension_semantics=("parallel", "arbitrary"), vmem_limit_bytes=TC_VMEM_LIMIT),
        name="dense_scores",
    )(h2, u_t)
    return out.reshape(-1)


def _act_kernel(pre_ref, gate_ref, w_ref):
    w = (gate_ref[...] * _gelu(pre_ref[...])).astype(jnp.bfloat16).astype(jnp.float32)
    hi = pltpu.bitcast(w, jnp.uint32) & jnp.uint32(HI_HALF)
    w_ref[...] = pltpu.bitcast(hi | (hi >> 16), jnp.int32)


def _expert_weights(pre, gate):
    T, K = pre.shape
    te = min(EW_TILE, T)
    spec = pl.BlockSpec((te, K), lambda i: (i, 0))
    return pl.pallas_call(
        _act_kernel, grid=(T // te,), in_specs=[spec, spec], out_specs=spec,
        out_shape=jax.ShapeDtypeStruct((T, K), jnp.int32),
        compiler_params=pltpu.CompilerParams(dimension_semantics=("parallel",)),
        name="expert_weights",
    )(pre, gate)


def _final_kernel(x_ref, y_ref, g_ref, *rest):
    o_ref = rest[-1]
    o_ref[...] = _rms(x_ref[...] + y_ref[...], g_ref[...])


def _final(x1, x_tok0, peer, norm_final, out_prev, out_tok0, total):
    Tc, D = peer.shape
    te = min(EW_TILE, Tc)
    assert x_tok0 % te == 0 and out_tok0 % te == 0 and Tc % te == 0
    nblk = Tc // te
    spec = pl.BlockSpec((te, D), lambda i: (out_tok0 // te + i, 0))
    in_specs = [pl.BlockSpec((te, D), lambda i: (x_tok0 // te + i, 0)),
                pl.BlockSpec((te, D), lambda i: (i, 0)), pl.BlockSpec((1, D), lambda i: (0, 0))]
    args = [x1, peer, norm_final.reshape(1, D)]
    aliases = {}
    if out_prev is not None:
        in_specs.append(pl.BlockSpec(memory_space=pl.ANY))
        args.append(out_prev)
        aliases = {3: 0}
    return pl.pallas_call(
        _final_kernel, grid=(nblk,), in_specs=in_specs, out_specs=spec,
        out_shape=jax.ShapeDtypeStruct((total, D), jnp.float32),
        input_output_aliases=aliases,
        compiler_params=pltpu.CompilerParams(dimension_semantics=("parallel",)),
        name="final_norm",
    )(*args)


def _tree_sum(vals):
    while len(vals) > 1:
        nxt = [vals[i] + vals[i + 1] for i in range(0, len(vals) - 1, 2)]
        if len(vals) % 2:
            nxt.append(vals[-1])
        vals = nxt
    return vals[0]


def _sc_block_pipeline(nblk, items_per_token, loads, store, gather, compute):
    assert items_per_token % 2 == 0 and nblk >= 1

    for c in loads(0, 0):
        c.start()
    for c in loads(0, 0):
        c.wait()
    if nblk > 1:
        for c in loads(1, 1):
            c.start()
    gather(0, 0, 0, 0).start()

    @pl.loop(0, nblk)
    def _(b):
        slot = b % 2

        @pl.when(b >= 2)
        def _():
            store(b - 2, slot).wait()

        @pl.loop(0, SC_TOKENS)
        def _(t):
            for q in range(items_per_token):
                buf = q % 2
                if q + 1 < items_per_token:
                    gather(slot, t, q + 1, 1 - buf).start()
                else:
                    @pl.when(t + 1 < SC_TOKENS)
                    def _():
                        gather(slot, t + 1, 0, 1 - buf).start()

                    @pl.when(jnp.logical_and(t + 1 == SC_TOKENS, b + 1 < nblk))
                    def _():
                        for c in loads(b + 1, 1 - slot):
                            c.wait()
                        gather(1 - slot, 0, 0, 1 - buf).start()

                gather(slot, t, q, buf).wait()
                compute(slot, t, q, buf)

        store(b, slot).start()

        @pl.when(b + 2 < nblk)
        def _():
            for c in loads(b + 2, slot):
                c.start()

    if nblk >= 2:
        store(nblk - 2, nblk % 2).wait()
    store(nblk - 1, (nblk - 1) % 2).wait()


def _sc_mesh():
    return plsc.VectorSubcoreMesh(core_axis_name="c", subcore_axis_name="s")


def _sc_worker_id():
    return lax.axis_index("s") * V7X_SC_CORES + lax.axis_index("c")


def _sc_bf16(words):
    return plsc.bitcast(words, jnp.bfloat16)


def _sc_halves_f32(pairs):
    words = plsc.bitcast(pairs, jnp.uint32)
    return (plsc.bitcast(words << 16, jnp.float32),
            plsc.bitcast(words & jnp.uint32(HI_HALF), jnp.float32))


def _pick_parts(T, K):
    L = V7X_SC_LANES
    nb = T // V7X_SC_WORKERS
    half_tile = _score_tile(T) // 2
    assert T % V7X_SC_WORKERS == 0 and nb <= SC_PICK_TOKENS
    assert half_tile % nb == 0 and half_tile & (half_tile - 1) == 0
    scratch = [pltpu.VMEM((nb, K), jnp.int32), pltpu.VMEM((nb, K), jnp.uint32),
               pltpu.VMEM((nb, K), jnp.float32), pltpu.SemaphoreType.DMA(())]

    def gather(flat_hbm, idx_v, word_v, sem, t):
        return pltpu.make_async_copy(flat_hbm.at[idx_v.at[t]], word_v.at[t], sem)

    def start(flat_hbm, slots_hbm, idx_v, word_v, val_v, sem):
        pltpu.sync_copy(slots_hbm.at[pl.ds(_sc_worker_id() * nb, nb)], idx_v)

        @pl.loop(0, nb)
        def _(t):
            gather(flat_hbm, idx_v, word_v, sem, t).start()

    def finish(flat_hbm, out_hbm, idx_v, word_v, val_v, sem):
        base = _sc_worker_id() * nb
        shift = jnp.full((L,), jnp.where((base & half_tile) == 0, 16, 0), jnp.uint32)

        @pl.loop(0, nb)
        def _(t):
            gather(flat_hbm, idx_v, word_v, sem, t).wait()
            for j in range(K // L):
                words = word_v[t, pl.ds(j * L, L)]
                val_v[t, pl.ds(j * L, L)] = plsc.bitcast((words << shift) & jnp.uint32(HI_HALF), jnp.float32)

        pltpu.sync_copy(val_v, out_hbm.at[pl.ds(base, nb)])

    return scratch, start, finish


def _mix_parts(T, K, DW):
    D = 2 * DW
    L, G = V7X_SC_LANES, SC_GATHER
    nj = SC_MIX_CHUNK // L
    tok_per_w = T // V7X_SC_WORKERS
    assert T % (V7X_SC_WORKERS * SC_TOKENS) == 0 and K % (2 * G) == 0
    assert DW % SC_MIX_CHUNK == 0 and G % SC_MIX_GROUP == 0
    scratch = [
        pltpu.VMEM((2, SC_TOKENS, K), jnp.int32),
        pltpu.VMEM((2, SC_TOKENS, K), jnp.int32),
        pltpu.VMEM((2, G, DW), jnp.uint32),
        pltpu.VMEM((2, SC_TOKENS, D), jnp.float32),
        pltpu.SemaphoreType.DMA((2,)),
        pltpu.SemaphoreType.DMA((2,)),
        pltpu.SemaphoreType.DMA((2,)),
        pltpu.SemaphoreType.DMA((2,)),
    ]

    def run(w_hbm, idx_hbm, tab_hbm, out_hbm, idx_v, w_v, rows_v, out_v, idx_sems, w_sems, out_sems, row_sems):
        base = _sc_worker_id() * tok_per_w

        def loads(b, slot):
            toks = pl.ds(base + b * SC_TOKENS, SC_TOKENS)
            return [pltpu.make_async_copy(idx_hbm.at[toks], idx_v.at[slot], idx_sems.at[slot]),
                    pltpu.make_async_copy(w_hbm.at[toks], w_v.at[slot], w_sems.at[slot])]

        def store(b, slot):
            toks = pl.ds(base + b * SC_TOKENS, SC_TOKENS)
            return pltpu.make_async_copy(out_v.at[slot], out_hbm.at[toks], out_sems.at[slot])

        def gather(slot, t, q, buf):
            return pltpu.make_async_copy(
                tab_hbm.at[idx_v.at[slot, t, pl.ds(q * G, G)]], rows_v.at[buf], row_sems.at[buf])

        def compute(slot, t, q, buf):
            ssplat = jnp.full((L,), slot, jnp.int32)
            tsplat = jnp.full((L,), t, jnp.int32)
            for c in range(DW // SC_MIX_CHUNK):
                def body(kg, acc):
                    kk = kg * SC_MIX_GROUP
                    wks = [_sc_bf16(plsc.load_gather(
                        w_v, [ssplat, tsplat, jnp.full((L,), q * G + i, jnp.int32) + kk]))
                        for i in range(SC_MIX_GROUP)]
                    out = []
                    for j in range(nj):
                        prods = [wks[i] * _sc_bf16(rows_v[buf, kk + i, pl.ds(c * SC_MIX_CHUNK + j * L, L)])
                                 for i in range(SC_MIX_GROUP)]
                        lo, hi = _sc_halves_f32(_tree_sum(prods))
                        out += [acc[2 * j] + lo, acc[2 * j + 1] + hi]
                    return tuple(out)

                zero = jnp.zeros((L,), jnp.float32)
                acc = plsc.parallel_loop(0, G // SC_MIX_GROUP, carry=(zero,) * (2 * nj))(body)
                for j in range(nj):
                    for half in range(2):
                        dst = out_v.at[slot, t, pl.ds(half * DW + c * SC_MIX_CHUNK + j * L, L)]
                        if q == 0:
                            dst[...] = acc[2 * j + half]
                        else:
                            plsc.addupdate(dst, acc[2 * j + half])

        _sc_block_pipeline(tok_per_w // SC_TOKENS, K // G, loads, store, gather, compute)

    return scratch, run


def _sc_kernel(out_type, scratch_types, name):
    return functools.partial(
        pl.kernel, mesh=_sc_mesh(), out_type=out_type, scratch_types=scratch_types,
        compiler_params=pltpu.CompilerParams(needs_layout_passes=False), name=name)


def _pick_scores(flat, slots, after):
    T, K = slots.shape
    scratch, start, finish = _pick_parts(T, K)

    @_sc_kernel(jax.ShapeDtypeStruct((T, K), jnp.float32), scratch, "pick_scores")
    def k(flat_hbm, slots_hbm, after_hbm, pre_hbm, *pick_scratch):
        del after_hbm
        start(flat_hbm, slots_hbm, *pick_scratch)
        finish(flat_hbm, pre_hbm, *pick_scratch)

    return k(flat, slots, after)


def _expert_mix(w, idx, table):
    T, K = w.shape
    DW = table.shape[1]
    scratch, run = _mix_parts(T, K, DW)

    @_sc_kernel(jax.ShapeDtypeStruct((T, 2 * DW), jnp.float32), scratch, "expert_mix")
    def k(w_hbm, idx_hbm, tab_hbm, out_hbm, *mix_scratch):
        run(w_hbm, idx_hbm, tab_hbm, out_hbm, *mix_scratch)

    return k(w, idx, table)


def _chunk_sizes(total):
    ramp, size = [EDGE_CHUNK], EDGE_CHUNK
    while size < MAX_CHUNK:
        ramp.append(size)
        size *= 2
    middle = total - 2 * sum(ramp)
    assert middle >= 0 and middle % MAX_CHUNK == 0
    return ramp + [MAX_CHUNK] * (middle // MAX_CHUNK) + ramp[::-1]


def kernel(x, norm_mix, w_in, pool_w, pool_scale, sgu_ln_g, sgu_ln_b, sgu_w, sgu_b, out_norm_pool,
           out_norm_sgu, w_out, norm_ffn, peer_wq, peer_keys, peer_u, peer_v, norm_final):
    B, S, D = x.shape
    assert norm_mix.shape[0] == 1, "single-layer block"
    T = B * S
    mix_args = (norm_mix[0], w_in[0], pool_w[0], pool_scale[0], sgu_ln_g[0], sgu_ln_b[0],
                sgu_w[0], sgu_b[0], out_norm_pool[0], out_norm_sgu[0], w_out[0])
    x1_parts = [(b * S, _mixer(x, b, 1, *mix_args).reshape(S, D)) for b in range(B)]
    wq = peer_wq[0].astype(jnp.bfloat16)
    keys = peer_keys[0].astype(jnp.bfloat16)
    u_t = peer_u[0].astype(jnp.bfloat16).T
    v_tab = _pack_table(peer_v[0])
    out = None
    tok0 = 0
    ws = [norm_ffn, v_tab]
    peers = [norm_ffn, norm_ffn]
    for tc in _chunk_sizes(T):
        part0, x1 = [p for p in x1_parts if p[0] <= tok0][-1]
        assert tok0 + tc <= part0 + x1.shape[0], "a token chunk must lie inside one mixer call"
        h2, idx, slots, gate = _router(x1, norm_ffn[0], wq, keys, tok0 - part0, tc, ws[-2])
        pre = _pick_scores(_dense_scores(h2, u_t), slots, peers[-2])
        w = _expert_weights(pre, gate)
        ws.append(w)
        peer = _expert_mix(w, idx, v_tab)
        peers.append(peer)
        out = _final(x1, tok0 - part0, peer, norm_final, out, tok0, T)
        tok0 += tc
    return out.reshape(B, S, D)
```

```python
import functools
import math

import jax
import jax.numpy as jnp
from jax import lax
from jax.experimental import pallas as pl
from jax.experimental.pallas import tpu as pltpu
from jax.experimental.pallas import tpu_sc as plsc

POOL_WINDOWS = (2, 4, 8, 16)
N_POOL_GROUPS = len(POOL_WINDOWS)
SGU_HEADS = 4
SGU_CHUNK = 128
PEER_HEADS = 8
PEER_N_KEYS = 128
PEER_D_HALF = 128
PEER_TOPK = 16
NORM_EPS = 1e-6
EXPERTS_PER_TOKEN = PEER_HEADS * PEER_TOPK

V7X_LANES = 128
V7X_SUBLANES = 8
V7X_SC_CORES = 2
V7X_SC_SUBCORES = 16
V7X_SC_LANES = 16
V7X_SC_WORKERS = V7X_SC_CORES * V7X_SC_SUBCORES

HALO = max(POOL_WINDOWS)
MIX_TILE = 512
ROUTE_TILE = 1024
EW_TILE = 512
SCORE_TOK_TILE = 2048
SCORE_EXP_TILE = 1024
SC_PICK_TOKENS = 64
SC_GATHER = 64
SC_TOKENS = 16
SC_MIX_CHUNK = 128
SC_MIX_GROUP = 4
HI_HALF = 0xFFFF0000
TC_VMEM_LIMIT = 48 * 1024 * 1024
EDGE_CHUNK = 512
MAX_CHUNK = 2048


def _rms(x, g):
    inv = lax.rsqrt(jnp.mean(x * x, axis=-1, keepdims=True) + NORM_EPS)
    return x * inv * g


def _pack_halves(bits):
    half = bits.shape[1] // 2
    return (bits[:, :half] >> 16) | (bits[:, half:] & jnp.uint32(HI_HALF))


def _pack_table(a):
    return _pack_halves(lax.bitcast_convert_type(a.astype(jnp.bfloat16).astype(jnp.float32), jnp.uint32))


def _gelu(x):
    return 0.5 * x * (1.0 + lax.erf(x * math.sqrt(0.5)))


def _mixer_kernel(x_ref, xh_ref, nmix_ref, win_ref, poolw_ref, pscale_ref, lng_ref, lnb_ref,
                  sguw_ref, sgub_ref, onp_ref, ons_ref, wout_ref, o_ref, pext_ref, mix_ref):
    i = pl.program_id(1)
    ts = x_ref.shape[1]
    pool_w = pscale_ref.shape[1]
    gdim = pool_w // N_POOL_GROUPS
    sgu_w = lng_ref.shape[1]
    hdim = sgu_w // SGU_HEADS

    x = x_ref[0]
    h = _rms(x, nmix_ref[...]).astype(jnp.bfloat16)
    z = jnp.dot(h, win_ref[...], preferred_element_type=jnp.float32)
    p = z[:, :pool_w]

    hh = _rms(xh_ref[0], nmix_ref[...]).astype(jnp.bfloat16)
    ph = jnp.dot(hh, win_ref[:, :pool_w], preferred_element_type=jnp.float32)
    ph = jnp.where(i > 0, ph, 0.0)
    pext_ref[0:HALO, :] = ph
    pext_ref[HALO:HALO + ts, :] = p

    pos = i * ts + lax.broadcasted_iota(jnp.int32, (ts, 1), 0)
    ssq = jnp.zeros((ts, 1), jnp.float32)
    a_parts = []
    for g, win in enumerate(POOL_WINDOWS):
        cols = slice(g * gdim, (g + 1) * gdim)
        s = pext_ref[HALO:HALO + ts, cols]
        for j in range(1, win):
            s = s + pext_ref[HALO - j:HALO - j + ts, cols]
        cnt = jnp.minimum(pos + 1, win).astype(jnp.float32)
        d = (s / cnt - p[:, cols]).astype(jnp.bfloat16)
        a = jnp.dot(d, poolw_ref[g], preferred_element_type=jnp.float32) * pscale_ref[:, cols]
        ssq = ssq + jnp.sum(a * a, axis=-1, keepdims=True)
        a_parts.append(a)
    inv_a = lax.rsqrt(ssq / pool_w + NORM_EPS)
    for g in range(N_POOL_GROUPS):
        cols = slice(g * gdim, (g + 1) * gdim)
        mix_ref[:, cols] = (a_parts[g] * inv_a * onp_ref[:, cols]).astype(jnp.bfloat16)

    gz = _gelu(z[:, pool_w:])
    tril = (lax.broadcasted_iota(jnp.int32, (SGU_CHUNK, SGU_CHUNK), 0)
            >= lax.broadcasted_iota(jnp.int32, (SGU_CHUNK, SGU_CHUNK), 1))
    ssq = jnp.zeros((ts, 1), jnp.float32)
    b_parts = []
    for hd in range(SGU_HEADS):
        cols = slice(hd * hdim, (hd + 1) * hdim)
        u = gz[:, hd * hdim:(hd + 1) * hdim]
        v = gz[:, sgu_w + hd * hdim:sgu_w + (hd + 1) * hdim]
        mu = jnp.mean(v, axis=-1, keepdims=True)
        vc = v - mu
        var = jnp.mean(vc * vc, axis=-1, keepdims=True)
        vn = (vc * lax.rsqrt(var + NORM_EPS) * lng_ref[:, cols] + lnb_ref[:, cols]).astype(jnp.bfloat16)
        w = jnp.where(tril, sguw_ref[hd], jnp.zeros((), sguw_ref.dtype))
        mixed = [jnp.dot(w, vn[n * SGU_CHUNK:(n + 1) * SGU_CHUNK], preferred_element_type=jnp.float32)
                 + sgub_ref[hd] for n in range(ts // SGU_CHUNK)]
        b = u * jnp.concatenate(mixed, axis=0)
        ssq = ssq + jnp.sum(b * b, axis=-1, keepdims=True)
        b_parts.append(b)
    inv_b = lax.rsqrt(ssq / sgu_w + NORM_EPS)
    for hd in range(SGU_HEADS):
        cols = slice(hd * hdim, (hd + 1) * hdim)
        mix_ref[:, pool_w + hd * hdim:pool_w + (hd + 1) * hdim] = (
            b_parts[hd] * inv_b * ons_ref[:, cols]).astype(jnp.bfloat16)

    o_ref[0] = x + jnp.dot(mix_ref[...], wout_ref[...], preferred_element_type=jnp.float32)


def _mixer(x, b0, nb, norm_mix, w_in, pool_w, pool_scale, ln_g, ln_b, sgu_w, sgu_b, on_pool, on_sgu, w_out):
    _, S, D = x.shape
    ts = min(MIX_TILE, S)
    pool_width = pool_scale.size
    sgu_width = ln_g.size
    in_width = w_in.shape[1]
    gdim = pool_width // N_POOL_GROUPS
    halo_blocks = ts // HALO
    full = lambda shape: pl.BlockSpec(shape, lambda b, i: (0,) * len(shape))
    return pl.pallas_call(
        _mixer_kernel,
        grid=(nb, S // ts),
        in_specs=[
            pl.BlockSpec((1, ts, D), lambda b, i: (b0 + b, i, 0)),
            pl.BlockSpec((1, HALO, D), lambda b, i: (b0 + b, jnp.maximum(i * halo_blocks - 1, 0), 0)),
            full((1, D)),
            full((D, in_width)),
            full((N_POOL_GROUPS, gdim, gdim)),
            full((1, pool_width)),
            full((1, sgu_width)),
            full((1, sgu_width)),
            full((SGU_HEADS, SGU_CHUNK, SGU_CHUNK)),
            full((SGU_HEADS, SGU_CHUNK, SGU_CHUNK)),
            full((1, pool_width)),
            full((1, sgu_width)),
            full((pool_width + sgu_width, D)),
        ],
        out_specs=pl.BlockSpec((1, ts, D), lambda b, i: (b, i, 0)),
        out_shape=jax.ShapeDtypeStruct((nb, S, D), jnp.float32),
        scratch_shapes=[
            pltpu.VMEM((HALO + ts, pool_width), jnp.float32),
            pltpu.VMEM((ts, pool_width + sgu_width), jnp.bfloat16),
        ],
        compiler_params=pltpu.CompilerParams(
            dimension_semantics=("parallel", "arbitrary"), vmem_limit_bytes=TC_VMEM_LIMIT),
        name="mixer",
    )(x, x, norm_mix.reshape(1, D), w_in.astype(jnp.bfloat16), pool_w.astype(jnp.bfloat16),
      pool_scale.reshape(1, pool_width), ln_g.reshape(1, sgu_width), ln_b.reshape(1, sgu_width),
      sgu_w.astype(jnp.bfloat16),
      jnp.broadcast_to(sgu_b[:, :, None], (SGU_HEADS, SGU_CHUNK, SGU_CHUNK)),
      on_pool.reshape(1, pool_width), on_sgu.reshape(1, sgu_width), w_out.astype(jnp.bfloat16))


def _topk_rows(s, k):
    n = s.shape[0]
    iota = lax.broadcasted_iota(jnp.int32, s.shape, 0)
    vals, idxs = [], []
    for _ in range(k):
        m = jnp.max(s, axis=0, keepdims=True)
        ix = jnp.min(jnp.where(s == m, iota, n), axis=0, keepdims=True)
        vals.append(m)
        idxs.append(ix)
        s = jnp.where(iota == ix, -jnp.inf, s)
    return vals, idxs


def _pair_candidates(v1, i1, v2, i2):
    k = PEER_TOPK
    v2c, i2c = jnp.concatenate(v2, axis=0), jnp.concatenate(i2, axis=0)
    m = v2c.shape[1]
    vals, experts, flats = [], [], []
    a = 0
    while k // (a + 1) > 1:
        rows = -(-(k // (a + 1)) // V7X_SUBLANES) * V7X_SUBLANES
        vals.append(v1[a] + v2c[:rows])
        experts.append(i1[a] * PEER_N_KEYS + i2c[:rows])
        flats.append(a * k + lax.broadcasted_iota(jnp.int32, (rows, m), 0))
        a += 1
    vals.append(jnp.concatenate(v1[a:], axis=0) + v2[0])
    experts.append(jnp.concatenate(i1[a:], axis=0) * PEER_N_KEYS + i2[0])
    flats.append((a + lax.broadcasted_iota(jnp.int32, (k - a, m), 0)) * k)
    return jnp.concatenate(vals, axis=0), jnp.concatenate(experts, axis=0), jnp.concatenate(flats, axis=0)


def _router_kernel(chunk_tokens, x_ref, nffn_ref, wq_ref, keys_ref, after_ref, h2_ref, idx_ref, slot_ref,
                   gate_ref, q_ref, idxt_ref, gatet_ref):
    del after_ref
    h2 = _rms(x_ref[...], nffn_ref[...]).astype(jnp.bfloat16)
    h2_ref[...] = h2
    q_ref[...] = jnp.dot(h2, wq_ref[...], preferred_element_type=jnp.float32).astype(jnp.bfloat16)
    dq = 2 * PEER_D_HALF
    nt = (((1,), (1,)), ((), ()))

    def head(hd, carry):
        off = pl.multiple_of(hd * dq, dq)
        s1 = lax.dot_general(keys_ref[0], q_ref[:, pl.ds(off, PEER_D_HALF)], nt,
                             preferred_element_type=jnp.float32)
        s2 = lax.dot_general(keys_ref[1], q_ref[:, pl.ds(off + PEER_D_HALF, PEER_D_HALF)], nt,
                             preferred_element_type=jnp.float32)
        v1, i1 = _topk_rows(s1, PEER_TOPK)
        v2, i2 = _topk_rows(s2, PEER_TOPK)
        cand, expert, flat = _pair_candidates(v1, i1, v2, i2)
        cv, ce = [], []
        for _ in range(PEER_TOPK):
            m = jnp.max(cand, axis=0, keepdims=True)
            ix = jnp.min(jnp.where(cand == m, flat, PEER_TOPK * PEER_TOPK), axis=0, keepdims=True)
            hit = flat == ix
            cv.append(m)
            ce.append(jnp.max(jnp.where(hit, expert, -1), axis=0, keepdims=True))
            cand = jnp.where(hit, -jnp.inf, cand)
        cvc = jnp.concatenate(cv, axis=0)
        e = jnp.exp(cvc - cv[0])
        gate = e / jnp.sum(e, axis=0, keepdims=True)
        row = pl.multiple_of(hd * PEER_TOPK, PEER_TOPK)
        idxt_ref[pl.ds(row, PEER_TOPK), :] = jnp.concatenate(ce, axis=0)
        gatet_ref[pl.ds(row, PEER_TOPK), :] = gate
        return carry

    lax.fori_loop(0, PEER_HEADS, head, 0)
    idx = idxt_ref[...].T
    idx_ref[...] = idx
    tok = pl.program_id(0) * idx.shape[0] + lax.broadcasted_iota(jnp.int32, idx.shape, 0)
    slot_ref[...] = _tile_linear_index(_score_word_row(tok, chunk_tokens), idx, PEER_N_KEYS * PEER_N_KEYS)
    gate_ref[...] = gatet_ref[...].T


def _router(x1, norm_ffn, wq, keys, tok0, T, after):
    D = x1.shape[1]
    tr = min(ROUTE_TILE, T)
    qw = wq.shape[1]
    assert tok0 % tr == 0 and T % tr == 0
    first = tok0 // tr
    full = lambda shape: pl.BlockSpec(shape, lambda i: (0,) * len(shape))
    return pl.pallas_call(
        functools.partial(_router_kernel, T),
        grid=(T // tr,),
        in_specs=[
            pl.BlockSpec((tr, D), lambda i: (first + i, 0)),
            full((1, D)),
            full((D, qw)),
            full((2, PEER_N_KEYS, PEER_D_HALF)),
            pl.BlockSpec(memory_space=pl.ANY),
        ],
        out_specs=[
            pl.BlockSpec((tr, D), lambda i: (i, 0)),
            pl.BlockSpec((tr, EXPERTS_PER_TOKEN), lambda i: (i, 0)),
            pl.BlockSpec((tr, EXPERTS_PER_TOKEN), lambda i: (i, 0)),
            pl.BlockSpec((tr, EXPERTS_PER_TOKEN), lambda i: (i, 0)),
        ],
        out_shape=[
            jax.ShapeDtypeStruct((T, D), jnp.bfloat16),
            jax.ShapeDtypeStruct((T, EXPERTS_PER_TOKEN), jnp.int32),
            jax.ShapeDtypeStruct((T, EXPERTS_PER_TOKEN), jnp.int32),
            jax.ShapeDtypeStruct((T, EXPERTS_PER_TOKEN), jnp.float32),
        ],
        scratch_shapes=[
            pltpu.VMEM((tr, qw), jnp.bfloat16),
            pltpu.VMEM((EXPERTS_PER_TOKEN, tr), jnp.int32),
            pltpu.VMEM((EXPERTS_PER_TOKEN, tr), jnp.float32),
        ],
        compiler_params=pltpu.CompilerParams(
            dimension_semantics=("parallel",), vmem_limit_bytes=TC_VMEM_LIMIT),
        name="router",
    )(x1, norm_ffn.reshape(1, D), wq, keys, after)


def _tile_linear_index(row, col, ncols):
    return ((row >> 3) * (ncols * V7X_SUBLANES) + (col >> 7) * (V7X_SUBLANES * V7X_LANES)
            + (row & (V7X_SUBLANES - 1)) * V7X_LANES + (col & (V7X_LANES - 1)))


def _score_tile(tokens):
    return min(SCORE_TOK_TILE, tokens)


def _score_word_row(tok, tokens):
    tm = _score_tile(tokens)
    assert tm & (tm - 1) == 0, "power-of-two score tile"
    return (tok >> int(math.log2(tm))) * (tm // 2) + (tok & (tm // 2 - 1))


def _scores_kernel(h_ref, ut_ref, o_ref):
    acc = jnp.dot(h_ref[...], ut_ref[...], preferred_element_type=jnp.float32)
    tm, tn = acc.shape
    bits = pltpu.bitcast(acc.astype(jnp.bfloat16).astype(jnp.float32), jnp.uint32)
    words = (bits[:tm // 2] >> 16) | (bits[tm // 2:] & jnp.uint32(HI_HALF))
    for n in range(tn // V7X_LANES):
        o_ref[:, n * V7X_SUBLANES:(n + 1) * V7X_SUBLANES, :] = words[:, n * V7X_LANES:(n + 1) * V7X_LANES].reshape(
            tm // 2 // V7X_SUBLANES, V7X_SUBLANES, V7X_LANES)


def _dense_scores(h2, u_t):
    T, D = h2.shape
    E = u_t.shape[1]
    tm, tn = _score_tile(T), min(SCORE_EXP_TILE, E)
    assert T % tm == 0 and E % tn == 0 and tm % (2 * V7X_SUBLANES) == 0 and tn % V7X_LANES == 0
    out = pl.pallas_call(
        _scores_kernel,
        grid=(T // tm, E // tn),
        in_specs=[pl.BlockSpec((tm, D), lambda i, j: (i, 0)), pl.BlockSpec((D, tn), lambda i, j: (0, j))],
        out_specs=pl.BlockSpec((tm // 2 // V7X_SUBLANES, tn // V7X_LANES * V7X_SUBLANES, V7X_LANES),
                               lambda i, j: (i, j, 0)),
        out_shape=jax.ShapeDtypeStruct((T // 2 // V7X_SUBLANES, E // V7X_LANES * V7X_SUBLANES, V7X_LANES),
                                       jnp.uint32),
        compiler_params=pltpu.CompilerParams(
            dimension_semantics=("parallel", "arbitrary"), vmem_limit_bytes=TC_VMEM_LIMIT),
        name="dense_scores",
    )(h2, u_t)
    return out.reshape(-1)


def _act_kernel(pre_ref, gate_ref, w_ref):
    w = (gate_ref[...] * _gelu(pre_ref[...])).astype(jnp.bfloat16).astype(jnp.float32)
    hi = pltpu.bitcast(w, jnp.uint32) & jnp.uint32(HI_HALF)
    w_ref[...] = pltpu.bitcast(hi | (hi >> 16), jnp.int32)


def _expert_weights(pre, gate):
    T, K = pre.shape
    te = min(EW_TILE, T)
    spec = pl.BlockSpec((te, K), lambda i: (i, 0))
    return pl.pallas_call(
        _act_kernel, grid=(T // te,), in_specs=[spec, spec], out_specs=spec,
        out_shape=jax.ShapeDtypeStruct((T, K), jnp.int32),
        compiler_params=pltpu.CompilerParams(dimension_semantics=("parallel",)),
        name="expert_weights",
    )(pre, gate)


def _final_kernel(x_ref, y_ref, g_ref, *rest):
    o_ref = rest[-1]
    o_ref[...] = _rms(x_ref[...] + y_ref[...], g_ref[...])


def _final(x1, x_tok0, peer, norm_final, out_prev, out_tok0, total):
    Tc, D = peer.shape
    te = min(EW_TILE, Tc)
    assert x_tok0 % te == 0 and out_tok0 % te == 0 and Tc % te == 0
    nblk = Tc // te
    spec = pl.BlockSpec((te, D), lambda i: (out_tok0 // te + i, 0))
    in_specs = [pl.BlockSpec((te, D), lambda i: (x_tok0 // te + i, 0)),
                pl.BlockSpec((te, D), lambda i: (i, 0)), pl.BlockSpec((1, D), lambda i: (0, 0))]
    args = [x1, peer, norm_final.reshape(1, D)]
    aliases = {}
    if out_prev is not None:
        in_specs.append(pl.BlockSpec(memory_space=pl.ANY))
        args.append(out_prev)
        aliases = {3: 0}
    return pl.pallas_call(
        _final_kernel, grid=(nblk,), in_specs=in_specs, out_specs=spec,
        out_shape=jax.ShapeDtypeStruct((total, D), jnp.float32),
        input_output_aliases=aliases,
        compiler_params=pltpu.CompilerParams(dimension_semantics=("parallel",)),
        name="final_norm",
    )(*args)


def _tree_sum(vals):
    while len(vals) > 1:
        nxt = [vals[i] + vals[i + 1] for i in range(0, len(vals) - 1, 2)]
        if len(vals) % 2:
            nxt.append(vals[-1])
        vals = nxt
    return vals[0]


def _sc_block_pipeline(nblk, items_per_token, loads, store, gather, compute):
    assert items_per_token % 2 == 0 and nblk >= 1

    for c in loads(0, 0):
        c.start()
    for c in loads(0, 0):
        c.wait()
    if nblk > 1:
        for c in loads(1, 1):
            c.start()
    gather(0, 0, 0, 0).start()

    @pl.loop(0, nblk)
    def _(b):
        slot = b % 2

        @pl.when(b >= 2)
        def _():
            store(b - 2, slot).wait()

        @pl.loop(0, SC_TOKENS)
        def _(t):
            for q in range(items_per_token):
                buf = q % 2
                if q + 1 < items_per_token:
                    gather(slot, t, q + 1, 1 - buf).start()
                else:
                    @pl.when(t + 1 < SC_TOKENS)
                    def _():
                        gather(slot, t + 1, 0, 1 - buf).start()

                    @pl.when(jnp.logical_and(t + 1 == SC_TOKENS, b + 1 < nblk))
                    def _():
                        for c in loads(b + 1, 1 - slot):
                            c.wait()
                        gather(1 - slot, 0, 0, 1 - buf).start()

                gather(slot, t, q, buf).wait()
                compute(slot, t, q, buf)

        store(b, slot).start()

        @pl.when(b + 2 < nblk)
        def _():
            for c in loads(b + 2, slot):
                c.start()

    if nblk >= 2:
        store(nblk - 2, nblk % 2).wait()
    store(nblk - 1, (nblk - 1) % 2).wait()


def _sc_mesh():
    return plsc.VectorSubcoreMesh(core_axis_name="c", subcore_axis_name="s")


def _sc_worker_id():
    return lax.axis_index("s") * V7X_SC_CORES + lax.axis_index("c")


def _sc_bf16(words):
    return plsc.bitcast(words, jnp.bfloat16)


def _sc_halves_f32(pairs):
    words = plsc.bitcast(pairs, jnp.uint32)
    return (plsc.bitcast(words << 16, jnp.float32),
            plsc.bitcast(words & jnp.uint32(HI_HALF), jnp.float32))


def _pick_parts(T, K):
    L = V7X_SC_LANES
    nb = T // V7X_SC_WORKERS
    half_tile = _score_tile(T) // 2
    assert T % V7X_SC_WORKERS == 0 and nb <= SC_PICK_TOKENS
    assert half_tile % nb == 0 and half_tile & (half_tile - 1) == 0
    scratch = [pltpu.VMEM((nb, K), jnp.int32), pltpu.VMEM((nb, K), jnp.uint32),
               pltpu.VMEM((nb, K), jnp.float32), pltpu.SemaphoreType.DMA(())]

    def gather(flat_hbm, idx_v, word_v, sem, t):
        return pltpu.make_async_copy(flat_hbm.at[idx_v.at[t]], word_v.at[t], sem)

    def start(flat_hbm, slots_hbm, idx_v, word_v, val_v, sem):
        pltpu.sync_copy(slots_hbm.at[pl.ds(_sc_worker_id() * nb, nb)], idx_v)

        @pl.loop(0, nb)
        def _(t):
            gather(flat_hbm, idx_v, word_v, sem, t).start()

    def finish(flat_hbm, out_hbm, idx_v, word_v, val_v, sem):
        base = _sc_worker_id() * nb
        shift = jnp.full((L,), jnp.where((base & half_tile) == 0, 16, 0), jnp.uint32)

        @pl.loop(0, nb)
        def _(t):
            gather(flat_hbm, idx_v, word_v, sem, t).wait()
            for j in range(K // L):
                words = word_v[t, pl.ds(j * L, L)]
                val_v[t, pl.ds(j * L, L)] = plsc.bitcast((words << shift) & jnp.uint32(HI_HALF), jnp.float32)

        pltpu.sync_copy(val_v, out_hbm.at[pl.ds(base, nb)])

    return scratch, start, finish


def _mix_parts(T, K, DW):
    D = 2 * DW
    L, G = V7X_SC_LANES, SC_GATHER
    nj = SC_MIX_CHUNK // L
    tok_per_w = T // V7X_SC_WORKERS
    assert T % (V7X_SC_WORKERS * SC_TOKENS) == 0 and K % (2 * G) == 0
    assert DW % SC_MIX_CHUNK == 0 and G % SC_MIX_GROUP == 0
    scratch = [
        pltpu.VMEM((2, SC_TOKENS, K), jnp.int32),
        pltpu.VMEM((2, SC_TOKENS, K), jnp.int32),
        pltpu.VMEM((2, G, DW), jnp.uint32),
        pltpu.VMEM((2, SC_TOKENS, D), jnp.float32),
        pltpu.SemaphoreType.DMA((2,)),
        pltpu.SemaphoreType.DMA((2,)),
        pltpu.SemaphoreType.DMA((2,)),
        pltpu.SemaphoreType.DMA((2,)),
    ]

    def run(w_hbm, idx_hbm, tab_hbm, out_hbm, idx_v, w_v, rows_v, out_v, idx_sems, w_sems, out_sems, row_sems):
        base = _sc_worker_id() * tok_per_w

        def loads(b, slot):
            toks = pl.ds(base + b * SC_TOKENS, SC_TOKENS)
            return [pltpu.make_async_copy(idx_hbm.at[toks], idx_v.at[slot], idx_sems.at[slot]),
                    pltpu.make_async_copy(w_hbm.at[toks], w_v.at[slot], w_sems.at[slot])]

        def store(b, slot):
            toks = pl.ds(base + b * SC_TOKENS, SC_TOKENS)
            return pltpu.make_async_copy(out_v.at[slot], out_hbm.at[toks], out_sems.at[slot])

        def gather(slot, t, q, buf):
            return pltpu.make_async_copy(
                tab_hbm.at[idx_v.at[slot, t, pl.ds(q * G, G)]], rows_v.at[buf], row_sems.at[buf])

        def compute(slot, t, q, buf):
            ssplat = jnp.full((L,), slot, jnp.int32)
            tsplat = jnp.full((L,), t, jnp.int32)
            for c in range(DW // SC_MIX_CHUNK):
                def body(kg, acc):
                    kk = kg * SC_MIX_GROUP
                    wks = [_sc_bf16(plsc.load_gather(
                        w_v, [ssplat, tsplat, jnp.full((L,), q * G + i, jnp.int32) + kk]))
                        for i in range(SC_MIX_GROUP)]
                    out = []
                    for j in range(nj):
                        prods = [wks[i] * _sc_bf16(rows_v[buf, kk + i, pl.ds(c * SC_MIX_CHUNK + j * L, L)])
                                 for i in range(SC_MIX_GROUP)]
                        lo, hi = _sc_halves_f32(_tree_sum(prods))
                        out += [acc[2 * j] + lo, acc[2 * j + 1] + hi]
                    return tuple(out)

                zero = jnp.zeros((L,), jnp.float32)
                acc = plsc.parallel_loop(0, G // SC_MIX_GROUP, carry=(zero,) * (2 * nj))(body)
                for j in range(nj):
                    for half in range(2):
                        dst = out_v.at[slot, t, pl.ds(half * DW + c * SC_MIX_CHUNK + j * L, L)]
                        if q == 0:
                            dst[...] = acc[2 * j + half]
                        else:
                            plsc.addupdate(dst, acc[2 * j + half])

        _sc_block_pipeline(tok_per_w // SC_TOKENS, K // G, loads, store, gather, compute)

    return scratch, run


def _sc_kernel(out_type, scratch_types, name):
    return functools.partial(
        pl.kernel, mesh=_sc_mesh(), out_type=out_type, scratch_types=scratch_types,
        compiler_params=pltpu.CompilerParams(needs_layout_passes=False), name=name)


def _pick_scores(flat, slots, after):
    T, K = slots.shape
    scratch, start, finish = _pick_parts(T, K)

    @_sc_kernel(jax.ShapeDtypeStruct((T, K), jnp.float32), scratch, "pick_scores")
    def k(flat_hbm, slots_hbm, after_hbm, pre_hbm, *pick_scratch):
        del after_hbm
        start(flat_hbm, slots_hbm, *pick_scratch)
        finish(flat_hbm, pre_hbm, *pick_scratch)

    return k(flat, slots, after)


def _expert_mix(w, idx, table):
    T, K = w.shape
    DW = table.shape[1]
    scratch, run = _mix_parts(T, K, DW)

    @_sc_kernel(jax.ShapeDtypeStruct((T, 2 * DW), jnp.float32), scratch, "expert_mix")
    def k(w_hbm, idx_hbm, tab_hbm, out_hbm, *mix_scratch):
        run(w_hbm, idx_hbm, tab_hbm, out_hbm, *mix_scratch)

    return k(w, idx, table)


def _chunk_sizes(total):
    ramp, size = [EDGE_CHUNK], EDGE_CHUNK
    while size < MAX_CHUNK:
        ramp.append(size)
        size *= 2
    middle = total - 2 * sum(ramp)
    assert middle >= 0 and middle % MAX_CHUNK == 0
    return ramp + [MAX_CHUNK] * (middle // MAX_CHUNK) + ramp[::-1]


def kernel(x, norm_mix, w_in, pool_w, pool_scale, sgu_ln_g, sgu_ln_b, sgu_w, sgu_b, out_norm_pool,
           out_norm_sgu, w_out, norm_ffn, peer_wq, peer_keys, peer_u, peer_v, norm_final):
    B, S, D = x.shape
    assert norm_mix.shape[0] == 1, "single-layer block"
    T = B * S
    mix_args = (norm_mix[0], w_in[0], pool_w[0], pool_scale[0], sgu_ln_g[0], sgu_ln_b[0],
                sgu_w[0], sgu_b[0], out_norm_pool[0], out_norm_sgu[0], w_out[0])
    x1_parts = [(b * S, _mixer(x, b, 1, *mix_args).reshape(S, D)) for b in range(B)]
    wq = peer_wq[0].astype(jnp.bfloat16)
    keys = peer_keys[0].astype(jnp.bfloat16)
    u_t = peer_u[0].astype(jnp.bfloat16).T
    v_tab = _pack_table(peer_v[0])
    out = None
    tok0 = 0
    ws = [norm_ffn, v_tab]
    peers = [norm_ffn, norm_ffn]
    for tc in _chunk_sizes(T):
        part0, x1 = [p for p in x1_parts if p[0] <= tok0][-1]
        assert tok0 + tc <= part0 + x1.shape[0], "a token chunk must lie inside one mixer call"
        h2, idx, slots, gate = _router(x1, norm_ffn[0], wq, keys, tok0 - part0, tc, ws[-2])
        pre = _pick_scores(_dense_scores(h2, u_t), slots, peers[-2])
        w = _expert_weights(pre, gate)
        ws.append(w)
        peer = _expert_mix(w, idx, v_tab)
        peers.append(peer)
        out = _final(x1, tok0 - part0, peer, norm_final, out, tok0, T)
        tok0 += tc
    return out.reshape(B, S, D)
```

```python
import functools
import math

import jax
import jax.numpy as jnp
from jax import lax
from jax.experimental import pallas as pl
from jax.experimental.pallas import tpu as pltpu
from jax.experimental.pallas import tpu_sc as plsc

POOL_WINDOWS = (2, 4, 8, 16)
N_POOL_GROUPS = len(POOL_WINDOWS)
SGU_HEADS = 4
SGU_CHUNK = 128
PEER_HEADS = 8
PEER_N_KEYS = 128
PEER_D_HALF = 128
PEER_TOPK = 16
NORM_EPS = 1e-6
EXPERTS_PER_TOKEN = PEER_HEADS * PEER_TOPK

V7X_LANES = 128
V7X_SUBLANES = 8
V7X_SC_CORES = 2
V7X_SC_SUBCORES = 16
V7X_SC_LANES = 16
V7X_SC_WORKERS = V7X_SC_CORES * V7X_SC_SUBCORES

HALO = max(POOL_WINDOWS)
MIX_TILE = 512
ROUTE_TILE = 1024
EW_TILE = 512
SCORE_TOK_TILE = 2048
SCORE_EXP_TILE = 1024
SC_PICK_TOKENS = 64
SC_GATHER = 64
SC_TOKENS = 8
SC_MIX_CHUNK = 128
SC_MIX_GROUP = 4
HI_HALF = 0xFFFF0000
V7X_VMEM_BYTES = 64 * 1024 * 1024
TC_VMEM_LIMIT = V7X_VMEM_BYTES * 3 // 4
EDGE_CHUNK = 512
MAX_CHUNK = 2048


def _rms(x, g):
    inv = lax.rsqrt(jnp.mean(x * x, axis=-1, keepdims=True) + NORM_EPS)
    return x * inv * g


def _pack_halves(bits):
    half = bits.shape[1] // 2
    return (bits[:, :half] >> 16) | (bits[:, half:] & jnp.uint32(HI_HALF))


def _pack_table(a):
    return _pack_halves(lax.bitcast_convert_type(a.astype(jnp.bfloat16).astype(jnp.float32), jnp.uint32))


def _gelu(x):
    return 0.5 * x * (1.0 + lax.erf(x * math.sqrt(0.5)))


def _mixer_kernel(x_ref, xh_ref, nmix_ref, win_ref, poolw_ref, pscale_ref, lng_ref, lnb_ref,
                  sguw_ref, sgub_ref, onp_ref, ons_ref, wout_ref, o_ref, pext_ref, mix_ref):
    i = pl.program_id(1)
    ts = x_ref.shape[1]
    pool_w = pscale_ref.shape[1]
    gdim = pool_w // N_POOL_GROUPS
    sgu_w = lng_ref.shape[1]
    hdim = sgu_w // SGU_HEADS

    x = x_ref[0]
    h = _rms(x, nmix_ref[...]).astype(jnp.bfloat16)
    z = jnp.dot(h, win_ref[...], preferred_element_type=jnp.float32)
    p = z[:, :pool_w]

    hh = _rms(xh_ref[0], nmix_ref[...]).astype(jnp.bfloat16)
    ph = jnp.dot(hh, win_ref[:, :pool_w], preferred_element_type=jnp.float32)
    ph = jnp.where(i > 0, ph, 0.0)
    pext_ref[0:HALO, :] = ph
    pext_ref[HALO:HALO + ts, :] = p

    pos = i * ts + lax.broadcasted_iota(jnp.int32, (ts, 1), 0)
    ssq = jnp.zeros((ts, 1), jnp.float32)
    a_parts = []
    for g, win in enumerate(POOL_WINDOWS):
        cols = slice(g * gdim, (g + 1) * gdim)
        s = pext_ref[HALO:HALO + ts, cols]
        for j in range(1, win):
            s = s + pext_ref[HALO - j:HALO - j + ts, cols]
        cnt = jnp.minimum(pos + 1, win).astype(jnp.float32)
        d = (s / cnt - p[:, cols]).astype(jnp.bfloat16)
        a = jnp.dot(d, poolw_ref[g], preferred_element_type=jnp.float32) * pscale_ref[:, cols]
        ssq = ssq + jnp.sum(a * a, axis=-1, keepdims=True)
        a_parts.append(a)
    inv_a = lax.rsqrt(ssq / pool_w + NORM_EPS)
    for g in range(N_POOL_GROUPS):
        cols = slice(g * gdim, (g + 1) * gdim)
        mix_ref[:, cols] = (a_parts[g] * inv_a * onp_ref[:, cols]).astype(jnp.bfloat16)

    gz = _gelu(z[:, pool_w:])
    tril = (lax.broadcasted_iota(jnp.int32, (SGU_CHUNK, SGU_CHUNK), 0)
            >= lax.broadcasted_iota(jnp.int32, (SGU_CHUNK, SGU_CHUNK), 1))
    ssq = jnp.zeros((ts, 1), jnp.float32)
    b_parts = []
    for hd in range(SGU_HEADS):
        cols = slice(hd * hdim, (hd + 1) * hdim)
        u = gz[:, hd * hdim:(hd + 1) * hdim]
        v = gz[:, sgu_w + hd * hdim:sgu_w + (hd + 1) * hdim]
        mu = jnp.mean(v, axis=-1, keepdims=True)
        vc = v - mu
        var = jnp.mean(vc * vc, axis=-1, keepdims=True)
        vn = (vc * lax.rsqrt(var + NORM_EPS) * lng_ref[:, cols] + lnb_ref[:, cols]).astype(jnp.bfloat16)
        w = jnp.where(tril, sguw_ref[hd], jnp.zeros((), sguw_ref.dtype))
        mixed = [jnp.dot(w, vn[n * SGU_CHUNK:(n + 1) * SGU_CHUNK], preferred_element_type=jnp.float32)
                 + sgub_ref[hd] for n in range(ts // SGU_CHUNK)]
        b = u * jnp.concatenate(mixed, axis=0)
        ssq = ssq + jnp.sum(b * b, axis=-1, keepdims=True)
        b_parts.append(b)
    inv_b = lax.rsqrt(ssq / sgu_w + NORM_EPS)
    for hd in range(SGU_HEADS):
        cols = slice(hd * hdim, (hd + 1) * hdim)
        mix_ref[:, pool_w + hd * hdim:pool_w + (hd + 1) * hdim] = (
            b_parts[hd] * inv_b * ons_ref[:, cols]).astype(jnp.bfloat16)

    o_ref[0] = x + jnp.dot(mix_ref[...], wout_ref[...], preferred_element_type=jnp.float32)


def _mixer(x, b0, nb, norm_mix, w_in, pool_w, pool_scale, ln_g, ln_b, sgu_w, sgu_b, on_pool, on_sgu, w_out):
    _, S, D = x.shape
    ts = min(MIX_TILE, S)
    pool_width = pool_scale.size
    sgu_width = ln_g.size
    in_width = w_in.shape[1]
    gdim = pool_width // N_POOL_GROUPS
    halo_blocks = ts // HALO
    full = lambda shape: pl.BlockSpec(shape, lambda b, i: (0,) * len(shape))
    return pl.pallas_call(
        _mixer_kernel,
        grid=(nb, S // ts),
        in_specs=[
            pl.BlockSpec((1, ts, D), lambda b, i: (b0 + b, i, 0)),
            pl.BlockSpec((1, HALO, D), lambda b, i: (b0 + b, jnp.maximum(i * halo_blocks - 1, 0), 0)),
            full((1, D)),
            full((D, in_width)),
            full((N_POOL_GROUPS, gdim, gdim)),
            full((1, pool_width)),
            full((1, sgu_width)),
            full((1, sgu_width)),
            full((SGU_HEADS, SGU_CHUNK, SGU_CHUNK)),
            full((SGU_HEADS, SGU_CHUNK, SGU_CHUNK)),
            full((1, pool_width)),
            full((1, sgu_width)),
            full((pool_width + sgu_width, D)),
        ],
        out_specs=pl.BlockSpec((1, ts, D), lambda b, i: (b, i, 0)),
        out_shape=jax.ShapeDtypeStruct((nb, S, D), jnp.float32),
        scratch_shapes=[
            pltpu.VMEM((HALO + ts, pool_width), jnp.float32),
            pltpu.VMEM((ts, pool_width + sgu_width), jnp.bfloat16),
        ],
        compiler_params=pltpu.CompilerParams(
            dimension_semantics=("parallel", "arbitrary"), vmem_limit_bytes=TC_VMEM_LIMIT),
        name="mixer",
    )(x, x, norm_mix.reshape(1, D), w_in.astype(jnp.bfloat16), pool_w.astype(jnp.bfloat16),
      pool_scale.reshape(1, pool_width), ln_g.reshape(1, sgu_width), ln_b.reshape(1, sgu_width),
      sgu_w.astype(jnp.bfloat16),
      jnp.broadcast_to(sgu_b[:, :, None], (SGU_HEADS, SGU_CHUNK, SGU_CHUNK)),
      on_pool.reshape(1, pool_width), on_sgu.reshape(1, sgu_width), w_out.astype(jnp.bfloat16))


def _topk_rows(s, k):
    n = s.shape[0]
    iota = lax.broadcasted_iota(jnp.int32, s.shape, 0)
    vals, idxs = [], []
    for _ in range(k):
        m = jnp.max(s, axis=0, keepdims=True)
        ix = jnp.min(jnp.where(s == m, iota, n), axis=0, keepdims=True)
        vals.append(m)
        idxs.append(ix)
        s = jnp.where(iota == ix, -jnp.inf, s)
    return vals, idxs


def _pair_candidates(v1, i1, v2, i2):
    k = PEER_TOPK
    v2c, i2c = jnp.concatenate(v2, axis=0), jnp.concatenate(i2, axis=0)
    m = v2c.shape[1]
    vals, experts, flats = [], [], []
    a = 0
    while k // (a + 1) > 1:
        rows = -(-(k // (a + 1)) // V7X_SUBLANES) * V7X_SUBLANES
        vals.append(v1[a] + v2c[:rows])
        experts.append(i1[a] * PEER_N_KEYS + i2c[:rows])
        flats.append(a * k + lax.broadcasted_iota(jnp.int32, (rows, m), 0))
        a += 1
    vals.append(jnp.concatenate(v1[a:], axis=0) + v2[0])
    experts.append(jnp.concatenate(i1[a:], axis=0) * PEER_N_KEYS + i2[0])
    flats.append((a + lax.broadcasted_iota(jnp.int32, (k - a, m), 0)) * k)
    return jnp.concatenate(vals, axis=0), jnp.concatenate(experts, axis=0), jnp.concatenate(flats, axis=0)


def _router_kernel(chunk_tokens, x_ref, nffn_ref, wq_ref, keys_ref, after_a_ref, after_b_ref, h2_ref, idx_ref,
                   slot_ref, gate_ref, q_ref, idxt_ref, gatet_ref):
    del after_a_ref, after_b_ref
    h2 = _rms(x_ref[...], nffn_ref[...]).astype(jnp.bfloat16)
    h2_ref[...] = h2
    q_ref[...] = jnp.dot(h2, wq_ref[...], preferred_element_type=jnp.float32).astype(jnp.bfloat16)
    dq = 2 * PEER_D_HALF
    nt = (((1,), (1,)), ((), ()))

    def head(hd, carry):
        off = pl.multiple_of(hd * dq, dq)
        s1 = lax.dot_general(keys_ref[0], q_ref[:, pl.ds(off, PEER_D_HALF)], nt,
                             preferred_element_type=jnp.float32)
        s2 = lax.dot_general(keys_ref[1], q_ref[:, pl.ds(off + PEER_D_HALF, PEER_D_HALF)], nt,
                             preferred_element_type=jnp.float32)
        v1, i1 = _topk_rows(s1, PEER_TOPK)
        v2, i2 = _topk_rows(s2, PEER_TOPK)
        cand, expert, flat = _pair_candidates(v1, i1, v2, i2)
        cv, ce = [], []
        for _ in range(PEER_TOPK):
            m = jnp.max(cand, axis=0, keepdims=True)
            ix = jnp.min(jnp.where(cand == m, flat, PEER_TOPK * PEER_TOPK), axis=0, keepdims=True)
            hit = flat == ix
            cv.append(m)
            ce.append(jnp.max(jnp.where(hit, expert, -1), axis=0, keepdims=True))
            cand = jnp.where(hit, -jnp.inf, cand)
        cvc = jnp.concatenate(cv, axis=0)
        e = jnp.exp(cvc - cv[0])
        gate = e / jnp.sum(e, axis=0, keepdims=True)
        row = pl.multiple_of(hd * PEER_TOPK, PEER_TOPK)
        idxt_ref[pl.ds(row, PEER_TOPK), :] = jnp.concatenate(ce, axis=0)
        gatet_ref[pl.ds(row, PEER_TOPK), :] = gate
        return carry

    lax.fori_loop(0, PEER_HEADS, head, 0)
    idx = idxt_ref[...].T
    idx_ref[...] = idx
    tok = pl.program_id(0) * idx.shape[0] + lax.broadcasted_iota(jnp.int32, idx.shape, 0)
    slot_ref[...] = _tile_linear_index(_score_word_row(tok, chunk_tokens), idx, PEER_N_KEYS * PEER_N_KEYS)
    gate_ref[...] = gatet_ref[...].T


def _router(x1, norm_ffn, wq, keys, tok0, T, after):
    D = x1.shape[1]
    tr = min(ROUTE_TILE, T)
    qw = wq.shape[1]
    assert tok0 % tr == 0 and T % tr == 0
    first = tok0 // tr
    full = lambda shape: pl.BlockSpec(shape, lambda i: (0,) * len(shape))
    return pl.pallas_call(
        functools.partial(_router_kernel, T),
        grid=(T // tr,),
        in_specs=[
            pl.BlockSpec((tr, D), lambda i: (first + i, 0)),
            full((1, D)),
            full((D, qw)),
            full((2, PEER_N_KEYS, PEER_D_HALF)),
            pl.BlockSpec(memory_space=pl.ANY),
            pl.BlockSpec(memory_space=pl.ANY),
        ],
        out_specs=[
            pl.BlockSpec((tr, D), lambda i: (i, 0)),
            pl.BlockSpec((tr, EXPERTS_PER_TOKEN), lambda i: (i, 0)),
            pl.BlockSpec((tr, EXPERTS_PER_TOKEN), lambda i: (i, 0)),
            pl.BlockSpec((tr, EXPERTS_PER_TOKEN), lambda i: (i, 0)),
        ],
        out_shape=[
            jax.ShapeDtypeStruct((T, D), jnp.bfloat16),
            jax.ShapeDtypeStruct((T, EXPERTS_PER_TOKEN), jnp.int32),
            jax.ShapeDtypeStruct((T, EXPERTS_PER_TOKEN), jnp.int32),
            jax.ShapeDtypeStruct((T, EXPERTS_PER_TOKEN), jnp.float32),
        ],
        scratch_shapes=[
            pltpu.VMEM((tr, qw), jnp.bfloat16),
            pltpu.VMEM((EXPERTS_PER_TOKEN, tr), jnp.int32),
            pltpu.VMEM((EXPERTS_PER_TOKEN, tr), jnp.float32),
        ],
        compiler_params=pltpu.CompilerParams(
            dimension_semantics=("parallel",), vmem_limit_bytes=TC_VMEM_LIMIT),
        name="router",
    )(x1, norm_ffn.reshape(1, D), wq, keys, *after)


def _tile_linear_index(row, col, ncols):
    sub_bits, lane_bits = int(math.log2(V7X_SUBLANES)), int(math.log2(V7X_LANES))
    return ((row >> sub_bits) * (ncols * V7X_SUBLANES) + (col >> lane_bits) * (V7X_SUBLANES * V7X_LANES)
            + (row & (V7X_SUBLANES - 1)) * V7X_LANES + (col & (V7X_LANES - 1)))


def _score_tile(tokens):
    return min(SCORE_TOK_TILE, tokens)


def _score_word_row(tok, tokens):
    tm = _score_tile(tokens)
    assert tm & (tm - 1) == 0, "power-of-two score tile"
    return (tok >> int(math.log2(tm))) * (tm // 2) + (tok & (tm // 2 - 1))


def _scores_kernel(h_ref, ut_ref, o_ref):
    acc = jnp.dot(h_ref[...], ut_ref[...], preferred_element_type=jnp.float32)
    tm, tn = acc.shape
    bits = pltpu.bitcast(acc.astype(jnp.bfloat16).astype(jnp.float32), jnp.uint32)
    words = (bits[:tm // 2] >> 16) | (bits[tm // 2:] & jnp.uint32(HI_HALF))
    for n in range(tn // V7X_LANES):
        o_ref[:, n * V7X_SUBLANES:(n + 1) * V7X_SUBLANES, :] = words[:, n * V7X_LANES:(n + 1) * V7X_LANES].reshape(
            tm // 2 // V7X_SUBLANES, V7X_SUBLANES, V7X_LANES)


def _dense_scores(h2, u_t):
    T, D = h2.shape
    E = u_t.shape[1]
    tm, tn = _score_tile(T), min(SCORE_EXP_TILE, E)
    assert T % tm == 0 and E % tn == 0 and tm % (2 * V7X_SUBLANES) == 0 and tn % V7X_LANES == 0
    out = pl.pallas_call(
        _scores_kernel,
        grid=(T // tm, E // tn),
        in_specs=[pl.BlockSpec((tm, D), lambda i, j: (i, 0)), pl.BlockSpec((D, tn), lambda i, j: (0, j))],
        out_specs=pl.BlockSpec((tm // 2 // V7X_SUBLANES, tn // V7X_LANES * V7X_SUBLANES, V7X_LANES),
                               lambda i, j: (i, j, 0)),
        out_shape=jax.ShapeDtypeStruct((T // 2 // V7X_SUBLANES, E // V7X_LANES * V7X_SUBLANES, V7X_LANES),
                                       jnp.uint32),
        compiler_params=pltpu.CompilerParams(
            dimension_semantics=("parallel", "arbitrary"), vmem_limit_bytes=TC_VMEM_LIMIT),
        name="dense_scores",
    )(h2, u_t)
    return out.reshape(-1)


def _act_kernel(pre_ref, gate_ref, w_ref):
    w = (gate_ref[...] * _gelu(pre_ref[...])).astype(jnp.bfloat16).astype(jnp.float32)
    hi = pltpu.bitcast(w, jnp.uint32) & jnp.uint32(HI_HALF)
    w_ref[...] = pltpu.bitcast(hi | (hi >> 16), jnp.int32)


def _expert_weights(pre, gate):
    T, K = pre.shape
    te = min(EW_TILE, T)
    spec = pl.BlockSpec((te, K), lambda i: (i, 0))
    return pl.pallas_call(
        _act_kernel, grid=(T // te,), in_specs=[spec, spec], out_specs=spec,
        out_shape=jax.ShapeDtypeStruct((T, K), jnp.int32),
        compiler_params=pltpu.CompilerParams(dimension_semantics=("parallel",)),
        name="expert_weights",
    )(pre, gate)


def _final_kernel(x_ref, y_ref, g_ref, *rest):
    o_ref, done_ref = rest[-2:]
    o_ref[...] = _rms(x_ref[...] + y_ref[...], g_ref[...])
    done_ref[...] = jnp.zeros(done_ref.shape, done_ref.dtype)


def _final(x1, x_tok0, peer, norm_final, out_prev, out_tok0, total):
    Tc, D = peer.shape
    te = min(EW_TILE, Tc)
    assert x_tok0 % te == 0 and out_tok0 % te == 0 and Tc % te == 0
    nblk = Tc // te
    spec = pl.BlockSpec((te, D), lambda i: (out_tok0 // te + i, 0))
    in_specs = [pl.BlockSpec((te, D), lambda i: (x_tok0 // te + i, 0)),
                pl.BlockSpec((te, D), lambda i: (i, 0)), pl.BlockSpec((1, D), lambda i: (0, 0))]
    args = [x1, peer, norm_final.reshape(1, D)]
    aliases = {}
    if out_prev is not None:
        in_specs.append(pl.BlockSpec(memory_space=pl.ANY))
        args.append(out_prev)
        aliases = {3: 0}
    return pl.pallas_call(
        _final_kernel, grid=(nblk,), in_specs=in_specs,
        out_specs=[spec, pl.BlockSpec((V7X_SUBLANES, V7X_LANES), lambda i: (i, 0))],
        out_shape=[jax.ShapeDtypeStruct((total, D), jnp.float32),
                   jax.ShapeDtypeStruct((nblk * V7X_SUBLANES, V7X_LANES), jnp.float32)],
        input_output_aliases=aliases,
        compiler_params=pltpu.CompilerParams(dimension_semantics=("parallel",)),
        name="final_norm",
    )(*args)


def _tree_sum(vals):
    while len(vals) > 1:
        nxt = [vals[i] + vals[i + 1] for i in range(0, len(vals) - 1, 2)]
        if len(vals) % 2:
            nxt.append(vals[-1])
        vals = nxt
    return vals[0]


def _sc_block_pipeline(nblk, items_per_token, loads, store, gather, compute):
    assert items_per_token % 2 == 0 and nblk >= 1

    for c in loads(0, 0):
        c.start()
    for c in loads(0, 0):
        c.wait()
    if nblk > 1:
        for c in loads(1, 1):
            c.start()
    gather(0, 0, 0, 0).start()

    @pl.loop(0, nblk)
    def _(b):
        slot = b % 2

        @pl.when(b >= 2)
        def _():
            store(b - 2, slot).wait()

        @pl.loop(0, SC_TOKENS)
        def _(t):
            for q in range(items_per_token):
                buf = q % 2
                if q + 1 < items_per_token:
                    gather(slot, t, q + 1, 1 - buf).start()
                else:
                    @pl.when(t + 1 < SC_TOKENS)
                    def _():
                        gather(slot, t + 1, 0, 1 - buf).start()

                    @pl.when(jnp.logical_and(t + 1 == SC_TOKENS, b + 1 < nblk))
                    def _():
                        for c in loads(b + 1, 1 - slot):
                            c.wait()
                        gather(1 - slot, 0, 0, 1 - buf).start()

                gather(slot, t, q, buf).wait()
                compute(slot, t, q, buf)

        store(b, slot).start()

        @pl.when(b + 2 < nblk)
        def _():
            for c in loads(b + 2, slot):
                c.start()

    if nblk >= 2:
        store(nblk - 2, nblk % 2).wait()
    store(nblk - 1, (nblk - 1) % 2).wait()


def _sc_mesh():
    return plsc.VectorSubcoreMesh(core_axis_name="c", subcore_axis_name="s")


def _sc_worker_id():
    return lax.axis_index("s") * V7X_SC_CORES + lax.axis_index("c")


def _sc_bf16(words):
    return plsc.bitcast(words, jnp.bfloat16)


def _sc_halves_f32(pairs):
    words = plsc.bitcast(pairs, jnp.uint32)
    return (plsc.bitcast(words << 16, jnp.float32),
            plsc.bitcast(words & jnp.uint32(HI_HALF), jnp.float32))


def _pick_parts(T, K):
    L = V7X_SC_LANES
    nb = T // V7X_SC_WORKERS
    half_tile = _score_tile(T) // 2
    assert T % V7X_SC_WORKERS == 0 and nb <= SC_PICK_TOKENS
    assert half_tile % nb == 0 and half_tile & (half_tile - 1) == 0
    scratch = [pltpu.VMEM((nb, K), jnp.int32), pltpu.VMEM((nb, K), jnp.uint32),
               pltpu.VMEM((nb, K), jnp.float32), pltpu.SemaphoreType.DMA(())]

    def gather(flat_hbm, idx_v, word_v, sem, t):
        return pltpu.make_async_copy(flat_hbm.at[idx_v.at[t]], word_v.at[t], sem)

    def start(flat_hbm, slots_hbm, idx_v, word_v, val_v, sem):
        pltpu.sync_copy(slots_hbm.at[pl.ds(_sc_worker_id() * nb, nb)], idx_v)

        @pl.loop(0, nb)
        def _(t):
            gather(flat_hbm, idx_v, word_v, sem, t).start()

    def finish(flat_hbm, out_hbm, idx_v, word_v, val_v, sem):
        base = _sc_worker_id() * nb
        shift = jnp.full((L,), jnp.where((base & half_tile) == 0, 16, 0), jnp.uint32)

        @pl.loop(0, nb)
        def _(t):
            gather(flat_hbm, idx_v, word_v, sem, t).wait()
            for j in range(K // L):
                words = word_v[t, pl.ds(j * L, L)]
                val_v[t, pl.ds(j * L, L)] = plsc.bitcast((words << shift) & jnp.uint32(HI_HALF), jnp.float32)

        pltpu.sync_copy(val_v, out_hbm.at[pl.ds(base, nb)])

    return scratch, start, finish


def _mix_parts(T, K, DW):
    D = 2 * DW
    L, G = V7X_SC_LANES, SC_GATHER
    nj = SC_MIX_CHUNK // L
    tok_per_w = T // V7X_SC_WORKERS
    assert T % (V7X_SC_WORKERS * SC_TOKENS) == 0 and K % (2 * G) == 0
    assert DW % SC_MIX_CHUNK == 0 and G % SC_MIX_GROUP == 0
    scratch = [
        pltpu.VMEM((2, SC_TOKENS, K), jnp.int32),
        pltpu.VMEM((2, SC_TOKENS, K), jnp.int32),
        pltpu.VMEM((2, G, DW), jnp.uint32),
        pltpu.VMEM((2, SC_TOKENS, D), jnp.float32),
        pltpu.SemaphoreType.DMA((2,)),
        pltpu.SemaphoreType.DMA((2,)),
        pltpu.SemaphoreType.DMA((2,)),
        pltpu.SemaphoreType.DMA((2,)),
    ]

    def run(w_hbm, idx_hbm, tab_hbm, out_hbm, idx_v, w_v, rows_v, out_v, idx_sems, w_sems, out_sems, row_sems):
        base = _sc_worker_id() * tok_per_w

        def loads(b, slot):
            toks = pl.ds(base + b * SC_TOKENS, SC_TOKENS)
            return [pltpu.make_async_copy(idx_hbm.at[toks], idx_v.at[slot], idx_sems.at[slot]),
                    pltpu.make_async_copy(w_hbm.at[toks], w_v.at[slot], w_sems.at[slot])]

        def store(b, slot):
            toks = pl.ds(base + b * SC_TOKENS, SC_TOKENS)
            return pltpu.make_async_copy(out_v.at[slot], out_hbm.at[toks], out_sems.at[slot])

        def gather(slot, t, q, buf):
            return pltpu.make_async_copy(
                tab_hbm.at[idx_v.at[slot, t, pl.ds(q * G, G)]], rows_v.at[buf], row_sems.at[buf])

        def compute(slot, t, q, buf):
            ssplat = jnp.full((L,), slot, jnp.int32)
            tsplat = jnp.full((L,), t, jnp.int32)
            for c in range(DW // SC_MIX_CHUNK):
                def body(kg, acc):
                    kk = kg * SC_MIX_GROUP
                    wks = [_sc_bf16(plsc.load_gather(
                        w_v, [ssplat, tsplat, jnp.full((L,), q * G + i, jnp.int32) + kk]))
                        for i in range(SC_MIX_GROUP)]
                    out = []
                    for j in range(nj):
                        prods = [wks[i] * _sc_bf16(rows_v[buf, kk + i, pl.ds(c * SC_MIX_CHUNK + j * L, L)])
                                 for i in range(SC_MIX_GROUP)]
                        lo, hi = _sc_halves_f32(_tree_sum(prods))
                        out += [acc[2 * j] + lo, acc[2 * j + 1] + hi]
                    return tuple(out)

                zero = jnp.zeros((L,), jnp.float32)
                acc = plsc.parallel_loop(0, G // SC_MIX_GROUP, carry=(zero,) * (2 * nj))(body)
                for j in range(nj):
                    for half in range(2):
                        dst = out_v.at[slot, t, pl.ds(half * DW + c * SC_MIX_CHUNK + j * L, L)]
                        if q == 0:
                            dst[...] = acc[2 * j + half]
                        else:
                            plsc.addupdate(dst, acc[2 * j + half])

        _sc_block_pipeline(tok_per_w // SC_TOKENS, K // G, loads, store, gather, compute)

    return scratch, run


def _sc_kernel(out_type, scratch_types, name):
    return functools.partial(
        pl.kernel, mesh=_sc_mesh(), out_type=out_type, scratch_types=scratch_types,
        compiler_params=pltpu.CompilerParams(needs_layout_passes=False), name=name)


def _pick_scores(flat, slots, after):
    T, K = slots.shape
    scratch, start, finish = _pick_parts(T, K)

    @_sc_kernel(jax.ShapeDtypeStruct((T, K), jnp.float32), scratch, "pick_scores")
    def k(flat_hbm, slots_hbm, after_hbm, pre_hbm, *pick_scratch):
        del after_hbm
        start(flat_hbm, slots_hbm, *pick_scratch)
        finish(flat_hbm, pre_hbm, *pick_scratch)

    return k(flat, slots, after)


def _expert_mix(w, idx, table):
    T, K = w.shape
    DW = table.shape[1]
    scratch, run = _mix_parts(T, K, DW)

    @_sc_kernel(jax.ShapeDtypeStruct((T, 2 * DW), jnp.float32), scratch, "expert_mix")
    def k(w_hbm, idx_hbm, tab_hbm, out_hbm, *mix_scratch):
        run(w_hbm, idx_hbm, tab_hbm, out_hbm, *mix_scratch)

    return k(w, idx, table)


def _chunk_sizes(total):
    ramp, size = [EDGE_CHUNK], EDGE_CHUNK
    while size < MAX_CHUNK:
        ramp.append(size)
        size *= 2
    middle = total - 2 * sum(ramp)
    assert middle >= 0 and middle % MAX_CHUNK == 0
    return ramp + [MAX_CHUNK] * (middle // MAX_CHUNK) + ramp[::-1]


def kernel(x, norm_mix, w_in, pool_w, pool_scale, sgu_ln_g, sgu_ln_b, sgu_w, sgu_b, out_norm_pool,
           out_norm_sgu, w_out, norm_ffn, peer_wq, peer_keys, peer_u, peer_v, norm_final):
    B, S, D = x.shape
    assert norm_mix.shape[0] == 1, "single-layer block"
    T = B * S
    mix_args = (norm_mix[0], w_in[0], pool_w[0], pool_scale[0], sgu_ln_g[0], sgu_ln_b[0],
                sgu_w[0], sgu_b[0], out_norm_pool[0], out_norm_sgu[0], w_out[0])
    x1_parts = [(b * S, _mixer(x, b, 1, *mix_args).reshape(S, D)) for b in range(B)]
    wq = peer_wq[0].astype(jnp.bfloat16)
    keys = peer_keys[0].astype(jnp.bfloat16)
    u_t = peer_u[0].astype(jnp.bfloat16).T
    v_tab = _pack_table(peer_v[0])
    out = None
    tok0 = 0
    ws = [norm_ffn, v_tab]
    peers = [norm_ffn, norm_ffn]
    dones = [norm_ffn, norm_ffn, norm_ffn]
    for tc in _chunk_sizes(T):
        part0, x1 = [p for p in x1_parts if p[0] <= tok0][-1]
        assert tok0 + tc <= part0 + x1.shape[0], "a token chunk must lie inside one mixer call"
        h2, idx, slots, gate = _router(x1, norm_ffn[0], wq, keys, tok0 - part0, tc, (ws[-2], dones[-3]))
        pre = _pick_scores(_dense_scores(h2, u_t), slots, peers[-2])
        w = _expert_weights(pre, gate)
        ws.append(w)
        peer = _expert_mix(w, idx, v_tab)
        peers.append(peer)
        out, done = _final(x1, tok0 - part0, peer, norm_final, out, tok0, T)
        dones.append(done)
        tok0 += tc
    return out.reshape(B, S, D)
```

```python
import functools
import math

import jax
import jax.numpy as jnp
from jax import lax
from jax.experimental import pallas as pl
from jax.experimental.pallas import tpu as pltpu
from jax.experimental.pallas import tpu_sc as plsc

POOL_WINDOWS = (2, 4, 8, 16)
N_POOL_GROUPS = len(POOL_WINDOWS)
SGU_HEADS = 4
SGU_CHUNK = 128
PEER_HEADS = 8
PEER_N_KEYS = 128
PEER_D_HALF = 128
PEER_TOPK = 16
NORM_EPS = 1e-6
EXPERTS_PER_TOKEN = PEER_HEADS * PEER_TOPK

V7X_LANES = 128
V7X_SUBLANES = 8
V7X_SC_CORES = 2
V7X_SC_SUBCORES = 16
V7X_SC_LANES = 16
V7X_SC_WORKERS = V7X_SC_CORES * V7X_SC_SUBCORES

HALO = max(POOL_WINDOWS)
MIX_TILE = 512
ROUTE_TILE = 1024
EW_TILE = 512
SCORE_TOK_TILE = 2048
SCORE_EXP_TILE = 1024
SC_PICK_TOKENS = 64
SC_GATHER = 64
SC_TOKENS = 8
SC_MIX_CHUNK = 128
SC_MIX_GROUP = 4
HI_HALF = 0xFFFF0000
V7X_VMEM_BYTES = 64 * 1024 * 1024
TC_VMEM_LIMIT = V7X_VMEM_BYTES * 3 // 4
EDGE_CHUNK = 512
MAX_CHUNK = 2048


def _rms(x, g):
    inv = lax.rsqrt(jnp.mean(x * x, axis=-1, keepdims=True) + NORM_EPS)
    return x * inv * g


def _pack_halves(bits):
    half = bits.shape[1] // 2
    return (bits[:, :half] >> 16) | (bits[:, half:] & jnp.uint32(HI_HALF))


def _pack_table_kernel(a_ref, o_ref):
    o_ref[...] = _pack_halves(pltpu.bitcast(a_ref[...].astype(jnp.bfloat16).astype(jnp.float32), jnp.uint32))


def _pack_table(a):
    n, c = a.shape
    rows = min(EW_TILE, n)
    return pl.pallas_call(
        _pack_table_kernel, grid=(n // rows,),
        in_specs=[pl.BlockSpec((rows, c), lambda i: (i, 0))],
        out_specs=pl.BlockSpec((rows, c // 2), lambda i: (i, 0)),
        out_shape=jax.ShapeDtypeStruct((n, c // 2), jnp.uint32),
        compiler_params=pltpu.CompilerParams(dimension_semantics=("parallel",)),
        name="pack_table",
    )(a)


def _transpose_cast_kernel(a_ref, o_ref):
    o_ref[...] = a_ref[...].T.astype(o_ref.dtype)


def _transposed_bf16(a):
    n, c = a.shape
    rows = min(SCORE_EXP_TILE, n)
    return pl.pallas_call(
        _transpose_cast_kernel, grid=(n // rows,),
        in_specs=[pl.BlockSpec((rows, c), lambda i: (i, 0))],
        out_specs=pl.BlockSpec((c, rows), lambda i: (0, i)),
        out_shape=jax.ShapeDtypeStruct((c, n), jnp.bfloat16),
        compiler_params=pltpu.CompilerParams(dimension_semantics=("parallel",)),
        name="transpose_table",
    )(a)


def _gelu(x):
    return 0.5 * x * (1.0 + lax.erf(x * math.sqrt(0.5)))


def _mixer_kernel(x_ref, xh_ref, nmix_ref, win_ref, poolw_ref, pscale_ref, lng_ref, lnb_ref,
                  sguw_ref, sgub_ref, onp_ref, ons_ref, wout_ref, o_ref, pext_ref, mix_ref):
    i = pl.program_id(1)
    ts = x_ref.shape[1]
    pool_w = pscale_ref.shape[1]
    gdim = pool_w // N_POOL_GROUPS
    sgu_w = lng_ref.shape[1]
    hdim = sgu_w // SGU_HEADS

    x = x_ref[0]
    h = _rms(x, nmix_ref[...]).astype(jnp.bfloat16)
    z = jnp.dot(h, win_ref[...], preferred_element_type=jnp.float32)
    p = z[:, :pool_w]

    hh = _rms(xh_ref[0], nmix_ref[...]).astype(jnp.bfloat16)
    ph = jnp.dot(hh, win_ref[:, :pool_w], preferred_element_type=jnp.float32)
    ph = jnp.where(i > 0, ph, 0.0)
    pext_ref[0:HALO, :] = ph
    pext_ref[HALO:HALO + ts, :] = p

    pos = i * ts + lax.broadcasted_iota(jnp.int32, (ts, 1), 0)
    ssq = jnp.zeros((ts, 1), jnp.float32)
    a_parts = []
    for g, win in enumerate(POOL_WINDOWS):
        cols = slice(g * gdim, (g + 1) * gdim)
        s = pext_ref[HALO:HALO + ts, cols]
        for j in range(1, win):
            s = s + pext_ref[HALO - j:HALO - j + ts, cols]
        cnt = jnp.minimum(pos + 1, win).astype(jnp.float32)
        d = (s / cnt - p[:, cols]).astype(jnp.bfloat16)
        a = jnp.dot(d, poolw_ref[g], preferred_element_type=jnp.float32) * pscale_ref[:, cols]
        ssq = ssq + jnp.sum(a * a, axis=-1, keepdims=True)
        a_parts.append(a)
    inv_a = lax.rsqrt(ssq / pool_w + NORM_EPS)
    for g in range(N_POOL_GROUPS):
        cols = slice(g * gdim, (g + 1) * gdim)
        mix_ref[:, cols] = (a_parts[g] * inv_a * onp_ref[:, cols]).astype(jnp.bfloat16)

    gz = _gelu(z[:, pool_w:])
    tril = (lax.broadcasted_iota(jnp.int32, (SGU_CHUNK, SGU_CHUNK), 0)
            >= lax.broadcasted_iota(jnp.int32, (SGU_CHUNK, SGU_CHUNK), 1))
    ssq = jnp.zeros((ts, 1), jnp.float32)
    b_parts = []
    for hd in range(SGU_HEADS):
        cols = slice(hd * hdim, (hd + 1) * hdim)
        u = gz[:, hd * hdim:(hd + 1) * hdim]
        v = gz[:, sgu_w + hd * hdim:sgu_w + (hd + 1) * hdim]
        mu = jnp.mean(v, axis=-1, keepdims=True)
        vc = v - mu
        var = jnp.mean(vc * vc, axis=-1, keepdims=True)
        vn = (vc * lax.rsqrt(var + NORM_EPS) * lng_ref[:, cols] + lnb_ref[:, cols]).astype(jnp.bfloat16)
        w = jnp.where(tril, sguw_ref[hd], jnp.zeros((), sguw_ref.dtype))
        mixed = [jnp.dot(w, vn[n * SGU_CHUNK:(n + 1) * SGU_CHUNK], preferred_element_type=jnp.float32)
                 + sgub_ref[hd] for n in range(ts // SGU_CHUNK)]
        b = u * jnp.concatenate(mixed, axis=0)
        ssq = ssq + jnp.sum(b * b, axis=-1, keepdims=True)
        b_parts.append(b)
    inv_b = lax.rsqrt(ssq / sgu_w + NORM_EPS)
    for hd in range(SGU_HEADS):
        cols = slice(hd * hdim, (hd + 1) * hdim)
        mix_ref[:, pool_w + hd * hdim:pool_w + (hd + 1) * hdim] = (
            b_parts[hd] * inv_b * ons_ref[:, cols]).astype(jnp.bfloat16)

    o_ref[0] = x + jnp.dot(mix_ref[...], wout_ref[...], preferred_element_type=jnp.float32)


def _mixer(x, b0, nb, norm_mix, w_in, pool_w, pool_scale, ln_g, ln_b, sgu_w, sgu_b, on_pool, on_sgu, w_out):
    _, S, D = x.shape
    ts = min(MIX_TILE, S)
    pool_width = pool_scale.size
    sgu_width = ln_g.size
    in_width = w_in.shape[1]
    gdim = pool_width // N_POOL_GROUPS
    halo_blocks = ts // HALO
    full = lambda shape: pl.BlockSpec(shape, lambda b, i: (0,) * len(shape))
    return pl.pallas_call(
        _mixer_kernel,
        grid=(nb, S // ts),
        in_specs=[
            pl.BlockSpec((1, ts, D), lambda b, i: (b0 + b, i, 0)),
            pl.BlockSpec((1, HALO, D), lambda b, i: (b0 + b, jnp.maximum(i * halo_blocks - 1, 0), 0)),
            full((1, D)),
            full((D, in_width)),
            full((N_POOL_GROUPS, gdim, gdim)),
            full((1, pool_width)),
            full((1, sgu_width)),
            full((1, sgu_width)),
            full((SGU_HEADS, SGU_CHUNK, SGU_CHUNK)),
            full((SGU_HEADS, SGU_CHUNK, SGU_CHUNK)),
            full((1, pool_width)),
            full((1, sgu_width)),
            full((pool_width + sgu_width, D)),
        ],
        out_specs=pl.BlockSpec((1, ts, D), lambda b, i: (b, i, 0)),
        out_shape=jax.ShapeDtypeStruct((nb, S, D), jnp.float32),
        scratch_shapes=[
            pltpu.VMEM((HALO + ts, pool_width), jnp.float32),
            pltpu.VMEM((ts, pool_width + sgu_width), jnp.bfloat16),
        ],
        compiler_params=pltpu.CompilerParams(
            dimension_semantics=("parallel", "arbitrary"), vmem_limit_bytes=TC_VMEM_LIMIT),
        name="mixer",
    )(x, x, norm_mix.reshape(1, D), w_in.astype(jnp.bfloat16), pool_w.astype(jnp.bfloat16),
      pool_scale.reshape(1, pool_width), ln_g.reshape(1, sgu_width), ln_b.reshape(1, sgu_width),
      sgu_w.astype(jnp.bfloat16),
      jnp.broadcast_to(sgu_b[:, :, None], (SGU_HEADS, SGU_CHUNK, SGU_CHUNK)),
      on_pool.reshape(1, pool_width), on_sgu.reshape(1, sgu_width), w_out.astype(jnp.bfloat16))


def _topk_rows(s, k):
    n = s.shape[0]
    iota = lax.broadcasted_iota(jnp.int32, s.shape, 0)
    vals, idxs = [], []
    for _ in range(k):
        m = jnp.max(s, axis=0, keepdims=True)
        ix = jnp.min(jnp.where(s == m, iota, n), axis=0, keepdims=True)
        vals.append(m)
        idxs.append(ix)
        s = jnp.where(iota == ix, -jnp.inf, s)
    return vals, idxs


def _pair_candidates(v1, i1, v2, i2):
    k = PEER_TOPK
    v2c, i2c = jnp.concatenate(v2, axis=0), jnp.concatenate(i2, axis=0)
    m = v2c.shape[1]
    vals, experts, flats = [], [], []
    a = 0
    while k // (a + 1) > 1:
        rows = -(-(k // (a + 1)) // V7X_SUBLANES) * V7X_SUBLANES
        vals.append(v1[a] + v2c[:rows])
        experts.append(i1[a] * PEER_N_KEYS + i2c[:rows])
        flats.append(a * k + lax.broadcasted_iota(jnp.int32, (rows, m), 0))
        a += 1
    vals.append(jnp.concatenate(v1[a:], axis=0) + v2[0])
    experts.append(jnp.concatenate(i1[a:], axis=0) * PEER_N_KEYS + i2[0])
    flats.append((a + lax.broadcasted_iota(jnp.int32, (k - a, m), 0)) * k)
    return jnp.concatenate(vals, axis=0), jnp.concatenate(experts, axis=0), jnp.concatenate(flats, axis=0)


def _router_kernel(chunk_tokens, x_ref, nffn_ref, wq_ref, keys_ref, after_ref, h2_ref, idx_ref, slot_ref,
                   gate_ref, q_ref, idxt_ref, gatet_ref):
    del after_ref
    h2 = _rms(x_ref[...], nffn_ref[...]).astype(jnp.bfloat16)
    h2_ref[...] = h2
    q_ref[...] = jnp.dot(h2, wq_ref[...], preferred_element_type=jnp.float32).astype(jnp.bfloat16)
    dq = 2 * PEER_D_HALF
    nt = (((1,), (1,)), ((), ()))

    def head(hd, carry):
        off = pl.multiple_of(hd * dq, dq)
        s1 = lax.dot_general(keys_ref[0], q_ref[:, pl.ds(off, PEER_D_HALF)], nt,
                             preferred_element_type=jnp.float32)
        s2 = lax.dot_general(keys_ref[1], q_ref[:, pl.ds(off + PEER_D_HALF, PEER_D_HALF)], nt,
                             preferred_element_type=jnp.float32)
        v1, i1 = _topk_rows(s1, PEER_TOPK)
        v2, i2 = _topk_rows(s2, PEER_TOPK)
        cand, expert, flat = _pair_candidates(v1, i1, v2, i2)
        cv, ce = [], []
        for _ in range(PEER_TOPK):
            m = jnp.max(cand, axis=0, keepdims=True)
            ix = jnp.min(jnp.where(cand == m, flat, PEER_TOPK * PEER_TOPK), axis=0, keepdims=True)
            hit = flat == ix
            cv.append(m)
            ce.append(jnp.max(jnp.where(hit, expert, -1), axis=0, keepdims=True))
            cand = jnp.where(hit, -jnp.inf, cand)
        cvc = jnp.concatenate(cv, axis=0)
        e = jnp.exp(cvc - cv[0])
        gate = e / jnp.sum(e, axis=0, keepdims=True)
        row = pl.multiple_of(hd * PEER_TOPK, PEER_TOPK)
        idxt_ref[pl.ds(row, PEER_TOPK), :] = jnp.concatenate(ce, axis=0)
        gatet_ref[pl.ds(row, PEER_TOPK), :] = gate
        return carry

    lax.fori_loop(0, PEER_HEADS, head, 0)
    idx = idxt_ref[...].T
    idx_ref[...] = idx
    tok = pl.program_id(0) * idx.shape[0] + lax.broadcasted_iota(jnp.int32, idx.shape, 0)
    slot_ref[...] = _tile_linear_index(_score_word_row(tok, chunk_tokens), idx, PEER_N_KEYS * PEER_N_KEYS)
    gate_ref[...] = gatet_ref[...].T


def _router(x1, norm_ffn, wq, keys, tok0, T, after):
    D = x1.shape[1]
    tr = min(ROUTE_TILE, T)
    qw = wq.shape[1]
    assert tok0 % tr == 0 and T % tr == 0
    first = tok0 // tr
    full = lambda shape: pl.BlockSpec(shape, lambda i: (0,) * len(shape))
    return pl.pallas_call(
        functools.partial(_router_kernel, T),
        grid=(T // tr,),
        in_specs=[
            pl.BlockSpec((tr, D), lambda i: (first + i, 0)),
            full((1, D)),
            full((D, qw)),
            full((2, PEER_N_KEYS, PEER_D_HALF)),
            pl.BlockSpec(memory_space=pl.ANY),
        ],
        out_specs=[
            pl.BlockSpec((tr, D), lambda i: (i, 0)),
            pl.BlockSpec((tr, EXPERTS_PER_TOKEN), lambda i: (i, 0)),
            pl.BlockSpec((tr, EXPERTS_PER_TOKEN), lambda i: (i, 0)),
            pl.BlockSpec((tr, EXPERTS_PER_TOKEN), lambda i: (i, 0)),
        ],
        out_shape=[
            jax.ShapeDtypeStruct((T, D), jnp.bfloat16),
            jax.ShapeDtypeStruct((T, EXPERTS_PER_TOKEN), jnp.int32),
            jax.ShapeDtypeStruct((T, EXPERTS_PER_TOKEN), jnp.int32),
            jax.ShapeDtypeStruct((T, EXPERTS_PER_TOKEN), jnp.float32),
        ],
        scratch_shapes=[
            pltpu.VMEM((tr, qw), jnp.bfloat16),
            pltpu.VMEM((EXPERTS_PER_TOKEN, tr), jnp.int32),
            pltpu.VMEM((EXPERTS_PER_TOKEN, tr), jnp.float32),
        ],
        compiler_params=pltpu.CompilerParams(
            dimension_semantics=("parallel",), vmem_limit_bytes=TC_VMEM_LIMIT),
        name="router",
    )(x1, norm_ffn.reshape(1, D), wq, keys, after)


def _tile_linear_index(row, col, ncols):
    sub_bits, lane_bits = int(math.log2(V7X_SUBLANES)), int(math.log2(V7X_LANES))
    return ((row >> sub_bits) * (ncols * V7X_SUBLANES) + (col >> lane_bits) * (V7X_SUBLANES * V7X_LANES)
            + (row & (V7X_SUBLANES - 1)) * V7X_LANES + (col & (V7X_LANES - 1)))


def _score_tile(tokens):
    return min(SCORE_TOK_TILE, tokens)


def _score_word_row(tok, tokens):
    tm = _score_tile(tokens)
    assert tm & (tm - 1) == 0, "power-of-two score tile"
    return (tok >> int(math.log2(tm))) * (tm // 2) + (tok & (tm // 2 - 1))


def _scores_kernel(h_ref, ut_ref, o_ref):
    acc = jnp.dot(h_ref[...], ut_ref[...], preferred_element_type=jnp.float32)
    tm, tn = acc.shape
    bits = pltpu.bitcast(acc.astype(jnp.bfloat16).astype(jnp.float32), jnp.uint32)
    words = (bits[:tm // 2] >> 16) | (bits[tm // 2:] & jnp.uint32(HI_HALF))
    for n in range(tn // V7X_LANES):
        o_ref[:, n * V7X_SUBLANES:(n + 1) * V7X_SUBLANES, :] = words[:, n * V7X_LANES:(n + 1) * V7X_LANES].reshape(
            tm // 2 // V7X_SUBLANES, V7X_SUBLANES, V7X_LANES)


def _dense_scores(h2, u_t):
    T, D = h2.shape
    E = u_t.shape[1]
    tm, tn = _score_tile(T), min(SCORE_EXP_TILE, E)
    assert T % tm == 0 and E % tn == 0 and tm % (2 * V7X_SUBLANES) == 0 and tn % V7X_LANES == 0
    out = pl.pallas_call(
        _scores_kernel,
        grid=(T // tm, E // tn),
        in_specs=[pl.BlockSpec((tm, D), lambda i, j: (i, 0)), pl.BlockSpec((D, tn), lambda i, j: (0, j))],
        out_specs=pl.BlockSpec((tm // 2 // V7X_SUBLANES, tn // V7X_LANES * V7X_SUBLANES, V7X_LANES),
                               lambda i, j: (i, j, 0)),
        out_shape=jax.ShapeDtypeStruct((T // 2 // V7X_SUBLANES, E // V7X_LANES * V7X_SUBLANES, V7X_LANES),
                                       jnp.uint32),
        compiler_params=pltpu.CompilerParams(
            dimension_semantics=("parallel", "arbitrary"), vmem_limit_bytes=TC_VMEM_LIMIT),
        name="dense_scores",
    )(h2, u_t)
    return out.reshape(-1)


def _act_kernel(pre_ref, gate_ref, w_ref):
    w = (gate_ref[...] * _gelu(pre_ref[...])).astype(jnp.bfloat16).astype(jnp.float32)
    hi = pltpu.bitcast(w, jnp.uint32) & jnp.uint32(HI_HALF)
    w_ref[...] = pltpu.bitcast(hi | (hi >> 16), jnp.int32)


def _expert_weights(pre, gate):
    T, K = pre.shape
    te = min(EW_TILE, T)
    spec = pl.BlockSpec((te, K), lambda i: (i, 0))
    return pl.pallas_call(
        _act_kernel, grid=(T // te,), in_specs=[spec, spec], out_specs=spec,
        out_shape=jax.ShapeDtypeStruct((T, K), jnp.int32),
        compiler_params=pltpu.CompilerParams(dimension_semantics=("parallel",)),
        name="expert_weights",
    )(pre, gate)


def _final_kernel(x_ref, y_ref, g_ref, *rest):
    o_ref = rest[-1]
    o_ref[...] = _rms(x_ref[...] + y_ref[...], g_ref[...])


def _final(x1, x_tok0, peer, norm_final, out_prev, out_tok0, total):
    Tc, D = peer.shape
    te = min(EW_TILE, Tc)
    assert x_tok0 % te == 0 and out_tok0 % te == 0 and Tc % te == 0
    nblk = Tc // te
    spec = pl.BlockSpec((te, D), lambda i: (out_tok0 // te + i, 0))
    in_specs = [pl.BlockSpec((te, D), lambda i: (x_tok0 // te + i, 0)),
                pl.BlockSpec((te, D), lambda i: (i, 0)), pl.BlockSpec((1, D), lambda i: (0, 0))]
    args = [x1, peer, norm_final.reshape(1, D)]
    aliases = {}
    if out_prev is not None:
        in_specs.append(pl.BlockSpec(memory_space=pl.ANY))
        args.append(out_prev)
        aliases = {3: 0}
    return pl.pallas_call(
        _final_kernel, grid=(nblk,), in_specs=in_specs, out_specs=spec,
        out_shape=jax.ShapeDtypeStruct((total, D), jnp.float32),
        input_output_aliases=aliases,
        compiler_params=pltpu.CompilerParams(dimension_semantics=("parallel",)),
        name="final_norm",
    )(*args)


def _tree_sum(vals):
    while len(vals) > 1:
        nxt = [vals[i] + vals[i + 1] for i in range(0, len(vals) - 1, 2)]
        if len(vals) % 2:
            nxt.append(vals[-1])
        vals = nxt
    return vals[0]


def _sc_block_pipeline(nblk, items_per_token, loads, store, gather, compute):
    assert items_per_token % 2 == 0 and nblk >= 1

    for c in loads(0, 0):
        c.start()
    for c in loads(0, 0):
        c.wait()
    if nblk > 1:
        for c in loads(1, 1):
            c.start()
    gather(0, 0, 0, 0).start()

    @pl.loop(0, nblk)
    def _(b):
        slot = b % 2

        @pl.when(b >= 2)
        def _():
            store(b - 2, slot).wait()

        @pl.loop(0, SC_TOKENS)
        def _(t):
            for q in range(items_per_token):
                buf = q % 2
                if q + 1 < items_per_token:
                    gather(slot, t, q + 1, 1 - buf).start()
                else:
                    @pl.when(t + 1 < SC_TOKENS)
                    def _():
                        gather(slot, t + 1, 0, 1 - buf).start()

                    @pl.when(jnp.logical_and(t + 1 == SC_TOKENS, b + 1 < nblk))
                    def _():
                        for c in loads(b + 1, 1 - slot):
                            c.wait()
                        gather(1 - slot, 0, 0, 1 - buf).start()

                gather(slot, t, q, buf).wait()
                compute(slot, t, q, buf)

        store(b, slot).start()

        @pl.when(b + 2 < nblk)
        def _():
            for c in loads(b + 2, slot):
                c.start()

    if nblk >= 2:
        store(nblk - 2, nblk % 2).wait()
    store(nblk - 1, (nblk - 1) % 2).wait()


def _sc_mesh():
    return plsc.VectorSubcoreMesh(core_axis_name="c", subcore_axis_name="s")


def _sc_worker_id():
    return lax.axis_index("s") * V7X_SC_CORES + lax.axis_index("c")


def _sc_bf16(words):
    return plsc.bitcast(words, jnp.bfloat16)


def _sc_halves_f32(pairs):
    words = plsc.bitcast(pairs, jnp.uint32)
    return (plsc.bitcast(words << 16, jnp.float32),
            plsc.bitcast(words & jnp.uint32(HI_HALF), jnp.float32))


def _pick_parts(T, K):
    L = V7X_SC_LANES
    nb = T // V7X_SC_WORKERS
    half_tile = _score_tile(T) // 2
    assert T % V7X_SC_WORKERS == 0 and nb <= SC_PICK_TOKENS
    assert half_tile % nb == 0 and half_tile & (half_tile - 1) == 0
    scratch = [pltpu.VMEM((nb, K), jnp.int32), pltpu.VMEM((nb, K), jnp.uint32),
               pltpu.VMEM((nb, K), jnp.float32), pltpu.SemaphoreType.DMA(())]

    def gather(flat_hbm, idx_v, word_v, sem, t):
        return pltpu.make_async_copy(flat_hbm.at[idx_v.at[t]], word_v.at[t], sem)

    def start(flat_hbm, slots_hbm, idx_v, word_v, val_v, sem):
        pltpu.sync_copy(slots_hbm.at[pl.ds(_sc_worker_id() * nb, nb)], idx_v)

        @pl.loop(0, nb)
        def _(t):
            gather(flat_hbm, idx_v, word_v, sem, t).start()

    def finish(flat_hbm, out_hbm, idx_v, word_v, val_v, sem):
        base = _sc_worker_id() * nb
        shift = jnp.full((L,), jnp.where((base & half_tile) == 0, 16, 0), jnp.uint32)

        @pl.loop(0, nb)
        def _(t):
            gather(flat_hbm, idx_v, word_v, sem, t).wait()
            for j in range(K // L):
                words = word_v[t, pl.ds(j * L, L)]
                val_v[t, pl.ds(j * L, L)] = plsc.bitcast((words << shift) & jnp.uint32(HI_HALF), jnp.float32)

        pltpu.sync_copy(val_v, out_hbm.at[pl.ds(base, nb)])

    return scratch, start, finish


def _mix_parts(T, K, DW):
    D = 2 * DW
    L, G = V7X_SC_LANES, SC_GATHER
    nj = SC_MIX_CHUNK // L
    tok_per_w = T // V7X_SC_WORKERS
    assert T % (V7X_SC_WORKERS * SC_TOKENS) == 0 and K % (2 * G) == 0
    assert DW % SC_MIX_CHUNK == 0 and G % SC_MIX_GROUP == 0
    scratch = [
        pltpu.VMEM((2, SC_TOKENS, K), jnp.int32),
        pltpu.VMEM((2, SC_TOKENS, K), jnp.int32),
        pltpu.VMEM((2, G, DW), jnp.uint32),
        pltpu.VMEM((2, SC_TOKENS, D), jnp.float32),
        pltpu.SemaphoreType.DMA((2,)),
        pltpu.SemaphoreType.DMA((2,)),
        pltpu.SemaphoreType.DMA((2,)),
        pltpu.SemaphoreType.DMA((2,)),
    ]

    def run(w_hbm, idx_hbm, tab_hbm, out_hbm, idx_v, w_v, rows_v, out_v, idx_sems, w_sems, out_sems, row_sems):
        base = _sc_worker_id() * tok_per_w

        def loads(b, slot):
            toks = pl.ds(base + b * SC_TOKENS, SC_TOKENS)
            return [pltpu.make_async_copy(idx_hbm.at[toks], idx_v.at[slot], idx_sems.at[slot]),
                    pltpu.make_async_copy(w_hbm.at[toks], w_v.at[slot], w_sems.at[slot])]

        def store(b, slot):
            toks = pl.ds(base + b * SC_TOKENS, SC_TOKENS)
            return pltpu.make_async_copy(out_v.at[slot], out_hbm.at[toks], out_sems.at[slot])

        def gather(slot, t, q, buf):
            return pltpu.make_async_copy(
                tab_hbm.at[idx_v.at[slot, t, pl.ds(q * G, G)]], rows_v.at[buf], row_sems.at[buf])

        def compute(slot, t, q, buf):
            ssplat = jnp.full((L,), slot, jnp.int32)
            tsplat = jnp.full((L,), t, jnp.int32)
            for c in range(DW // SC_MIX_CHUNK):
                def body(kg, acc):
                    kk = kg * SC_MIX_GROUP
                    wks = [_sc_bf16(plsc.load_gather(
                        w_v, [ssplat, tsplat, jnp.full((L,), q * G + i, jnp.int32) + kk]))
                        for i in range(SC_MIX_GROUP)]
                    out = []
                    for j in range(nj):
                        prods = [wks[i] * _sc_bf16(rows_v[buf, kk + i, pl.ds(c * SC_MIX_CHUNK + j * L, L)])
                                 for i in range(SC_MIX_GROUP)]
                        lo, hi = _sc_halves_f32(_tree_sum(prods))
                        out += [acc[2 * j] + lo, acc[2 * j + 1] + hi]
                    return tuple(out)

                zero = jnp.zeros((L,), jnp.float32)
                acc = plsc.parallel_loop(0, G // SC_MIX_GROUP, carry=(zero,) * (2 * nj))(body)
                for j in range(nj):
                    for half in range(2):
                        dst = out_v.at[slot, t, pl.ds(half * DW + c * SC_MIX_CHUNK + j * L, L)]
                        if q == 0:
                            dst[...] = acc[2 * j + half]
                        else:
                            plsc.addupdate(dst, acc[2 * j + half])

        _sc_block_pipeline(tok_per_w // SC_TOKENS, K // G, loads, store, gather, compute)

    return scratch, run


def _sc_kernel(out_type, scratch_types, name):
    return functools.partial(
        pl.kernel, mesh=_sc_mesh(), out_type=out_type, scratch_types=scratch_types,
        compiler_params=pltpu.CompilerParams(needs_layout_passes=False), name=name)


def _pick_scores(flat, slots, after):
    T, K = slots.shape
    scratch, start, finish = _pick_parts(T, K)

    @_sc_kernel(jax.ShapeDtypeStruct((T, K), jnp.float32), scratch, "pick_scores")
    def k(flat_hbm, slots_hbm, after_hbm, pre_hbm, *pick_scratch):
        del after_hbm
        start(flat_hbm, slots_hbm, *pick_scratch)
        finish(flat_hbm, pre_hbm, *pick_scratch)

    return k(flat, slots, after)


def _expert_mix(w, idx, table):
    T, K = w.shape
    DW = table.shape[1]
    scratch, run = _mix_parts(T, K, DW)

    @_sc_kernel(jax.ShapeDtypeStruct((T, 2 * DW), jnp.float32), scratch, "expert_mix")
    def k(w_hbm, idx_hbm, tab_hbm, out_hbm, *mix_scratch):
        run(w_hbm, idx_hbm, tab_hbm, out_hbm, *mix_scratch)

    return k(w, idx, table)


def _chunk_sizes(total):
    ramp, size = [EDGE_CHUNK], EDGE_CHUNK
    while size < MAX_CHUNK:
        ramp.append(size)
        size *= 2
    middle = total - 2 * sum(ramp)
    assert middle >= 0 and middle % MAX_CHUNK == 0
    return ramp + [MAX_CHUNK] * (middle // MAX_CHUNK) + ramp[::-1]


def kernel(x, norm_mix, w_in, pool_w, pool_scale, sgu_ln_g, sgu_ln_b, sgu_w, sgu_b, out_norm_pool,
           out_norm_sgu, w_out, norm_ffn, peer_wq, peer_keys, peer_u, peer_v, norm_final):
    B, S, D = x.shape
    assert norm_mix.shape[0] == 1, "single-layer block"
    T = B * S
    mix_args = (norm_mix[0], w_in[0], pool_w[0], pool_scale[0], sgu_ln_g[0], sgu_ln_b[0],
                sgu_w[0], sgu_b[0], out_norm_pool[0], out_norm_sgu[0], w_out[0])
    x1_parts = [(b * S, _mixer(x, b, 1, *mix_args).reshape(S, D)) for b in range(B)]
    wq = peer_wq[0].astype(jnp.bfloat16)
    keys = peer_keys[0].astype(jnp.bfloat16)
    u_t = _transposed_bf16(peer_u[0])
    v_tab = _pack_table(peer_v[0])
    out = None
    tok0 = 0
    ws = [norm_ffn, v_tab]
    peers = [norm_ffn, norm_ffn]
    for tc in _chunk_sizes(T):
        part0, x1 = [p for p in x1_parts if p[0] <= tok0][-1]
        assert tok0 + tc <= part0 + x1.shape[0], "a token chunk must lie inside one mixer call"
        h2, idx, slots, gate = _router(x1, norm_ffn[0], wq, keys, tok0 - part0, tc, ws[-2])
        pre = _pick_scores(_dense_scores(h2, u_t), slots, peers[-2])
        w = _expert_weights(pre, gate)
        ws.append(w)
        peer = _expert_mix(w, idx, v_tab)
        peers.append(peer)
        out = _final(x1, tok0 - part0, peer, norm_final, out, tok0, T)
        tok0 += tc
    return out.reshape(B, S, D)
```

```python
import functools
import math

import jax
import jax.numpy as jnp
from jax import lax
from jax.experimental import pallas as pl
from jax.experimental.pallas import tpu as pltpu
from jax.experimental.pallas import tpu_sc as plsc

POOL_WINDOWS = (2, 4, 8, 16)
N_POOL_GROUPS = len(POOL_WINDOWS)
SGU_HEADS = 4
SGU_CHUNK = 128
PEER_HEADS = 8
PEER_N_KEYS = 128
PEER_D_HALF = 128
PEER_TOPK = 16
NORM_EPS = 1e-6
EXPERTS_PER_TOKEN = PEER_HEADS * PEER_TOPK

V7X_LANES = 128
V7X_SUBLANES = 8
V7X_SC_CORES = 2
V7X_SC_SUBCORES = 16
V7X_SC_LANES = 16
V7X_SC_WORKERS = V7X_SC_CORES * V7X_SC_SUBCORES

HALO = max(POOL_WINDOWS)
MIX_TILE = 512
ROUTE_TILE = 1024
EW_TILE = 512
SCORE_TOK_TILE = 2048
SCORE_EXP_TILE = 1024
SC_PICK_TOKENS = 64
SC_PACK_ROWS = 16
SC_GATHER = 64
SC_TOKENS = 8
SC_MIX_CHUNK = 128
SC_MIX_GROUP = 4
HI_HALF = 0xFFFF0000
V7X_VMEM_BYTES = 64 * 1024 * 1024
TC_VMEM_LIMIT = V7X_VMEM_BYTES * 3 // 4
EDGE_CHUNK = 512
MAX_CHUNK = 2048


def _rms(x, g):
    inv = lax.rsqrt(jnp.mean(x * x, axis=-1, keepdims=True) + NORM_EPS)
    return x * inv * g


def _transpose_cast_kernel(a_ref, o_ref):
    o_ref[...] = a_ref[...].T.astype(o_ref.dtype)


def _transposed_bf16(a):
    n, c = a.shape
    rows = min(SCORE_EXP_TILE, n)
    return pl.pallas_call(
        _transpose_cast_kernel, grid=(n // rows,),
        in_specs=[pl.BlockSpec((rows, c), lambda i: (i, 0))],
        out_specs=pl.BlockSpec((c, rows), lambda i: (0, i)),
        out_shape=jax.ShapeDtypeStruct((c, n), jnp.bfloat16),
        compiler_params=pltpu.CompilerParams(dimension_semantics=("parallel",)),
        name="transpose_table",
    )(a)


def _gelu(x):
    return 0.5 * x * (1.0 + lax.erf(x * math.sqrt(0.5)))


def _mixer_kernel(x_ref, xh_ref, nmix_ref, win_ref, poolw_ref, pscale_ref, lng_ref, lnb_ref,
                  sguw_ref, sgub_ref, onp_ref, ons_ref, wout_ref, o_ref, pext_ref, mix_ref):
    i = pl.program_id(1)
    ts = x_ref.shape[1]
    pool_w = pscale_ref.shape[1]
    gdim = pool_w // N_POOL_GROUPS
    sgu_w = lng_ref.shape[1]
    hdim = sgu_w // SGU_HEADS

    x = x_ref[0]
    h = _rms(x, nmix_ref[...]).astype(jnp.bfloat16)
    z = jnp.dot(h, win_ref[...], preferred_element_type=jnp.float32)
    p = z[:, :pool_w]

    hh = _rms(xh_ref[0], nmix_ref[...]).astype(jnp.bfloat16)
    ph = jnp.dot(hh, win_ref[:, :pool_w], preferred_element_type=jnp.float32)
    ph = jnp.where(i > 0, ph, 0.0)
    pext_ref[0:HALO, :] = ph
    pext_ref[HALO:HALO + ts, :] = p

    pos = i * ts + lax.broadcasted_iota(jnp.int32, (ts, 1), 0)
    ssq = jnp.zeros((ts, 1), jnp.float32)
    a_parts = []
    for g, win in enumerate(POOL_WINDOWS):
        cols = slice(g * gdim, (g + 1) * gdim)
        s = pext_ref[HALO:HALO + ts, cols]
        for j in range(1, win):
            s = s + pext_ref[HALO - j:HALO - j + ts, cols]
        cnt = jnp.minimum(pos + 1, win).astype(jnp.float32)
        d = (s / cnt - p[:, cols]).astype(jnp.bfloat16)
        a = jnp.dot(d, poolw_ref[g], preferred_element_type=jnp.float32) * pscale_ref[:, cols]
        ssq = ssq + jnp.sum(a * a, axis=-1, keepdims=True)
        a_parts.append(a)
    inv_a = lax.rsqrt(ssq / pool_w + NORM_EPS)
    for g in range(N_POOL_GROUPS):
        cols = slice(g * gdim, (g + 1) * gdim)
        mix_ref[:, cols] = (a_parts[g] * inv_a * onp_ref[:, cols]).astype(jnp.bfloat16)

    gz = _gelu(z[:, pool_w:])
    tril = (lax.broadcasted_iota(jnp.int32, (SGU_CHUNK, SGU_CHUNK), 0)
            >= lax.broadcasted_iota(jnp.int32, (SGU_CHUNK, SGU_CHUNK), 1))
    ssq = jnp.zeros((ts, 1), jnp.float32)
    b_parts = []
    for hd in range(SGU_HEADS):
        cols = slice(hd * hdim, (hd + 1) * hdim)
        u = gz[:, hd * hdim:(hd + 1) * hdim]
        v = gz[:, sgu_w + hd * hdim:sgu_w + (hd + 1) * hdim]
        mu = jnp.mean(v, axis=-1, keepdims=True)
        vc = v - mu
        var = jnp.mean(vc * vc, axis=-1, keepdims=True)
        vn = (vc * lax.rsqrt(var + NORM_EPS) * lng_ref[:, cols] + lnb_ref[:, cols]).astype(jnp.bfloat16)
        w = jnp.where(tril, sguw_ref[hd], jnp.zeros((), sguw_ref.dtype))
        mixed = [jnp.dot(w, vn[n * SGU_CHUNK:(n + 1) * SGU_CHUNK], preferred_element_type=jnp.float32)
                 + sgub_ref[hd] for n in range(ts // SGU_CHUNK)]
        b = u * jnp.concatenate(mixed, axis=0)
        ssq = ssq + jnp.sum(b * b, axis=-1, keepdims=True)
        b_parts.append(b)
    inv_b = lax.rsqrt(ssq / sgu_w + NORM_EPS)
    for hd in range(SGU_HEADS):
        cols = slice(hd * hdim, (hd + 1) * hdim)
        mix_ref[:, pool_w + hd * hdim:pool_w + (hd + 1) * hdim] = (
            b_parts[hd] * inv_b * ons_ref[:, cols]).astype(jnp.bfloat16)

    o_ref[0] = x + jnp.dot(mix_ref[...], wout_ref[...], preferred_element_type=jnp.float32)


def _mixer(x, b0, nb, norm_mix, w_in, pool_w, pool_scale, ln_g, ln_b, sgu_w, sgu_b, on_pool, on_sgu, w_out):
    _, S, D = x.shape
    ts = min(MIX_TILE, S)
    pool_width = pool_scale.size
    sgu_width = ln_g.size
    in_width = w_in.shape[1]
    gdim = pool_width // N_POOL_GROUPS
    halo_blocks = ts // HALO
    full = lambda shape: pl.BlockSpec(shape, lambda b, i: (0,) * len(shape))
    return pl.pallas_call(
        _mixer_kernel,
        grid=(nb, S // ts),
        in_specs=[
            pl.BlockSpec((1, ts, D), lambda b, i: (b0 + b, i, 0)),
            pl.BlockSpec((1, HALO, D), lambda b, i: (b0 + b, jnp.maximum(i * halo_blocks - 1, 0), 0)),
            full((1, D)),
            full((D, in_width)),
            full((N_POOL_GROUPS, gdim, gdim)),
            full((1, pool_width)),
            full((1, sgu_width)),
            full((1, sgu_width)),
            full((SGU_HEADS, SGU_CHUNK, SGU_CHUNK)),
            full((SGU_HEADS, SGU_CHUNK, SGU_CHUNK)),
            full((1, pool_width)),
            full((1, sgu_width)),
            full((pool_width + sgu_width, D)),
        ],
        out_specs=pl.BlockSpec((1, ts, D), lambda b, i: (b, i, 0)),
        out_shape=jax.ShapeDtypeStruct((nb, S, D), jnp.float32),
        scratch_shapes=[
            pltpu.VMEM((HALO + ts, pool_width), jnp.float32),
            pltpu.VMEM((ts, pool_width + sgu_width), jnp.bfloat16),
        ],
        compiler_params=pltpu.CompilerParams(
            dimension_semantics=("parallel", "arbitrary"), vmem_limit_bytes=TC_VMEM_LIMIT),
        name="mixer",
    )(x, x, norm_mix.reshape(1, D), w_in.astype(jnp.bfloat16), pool_w.astype(jnp.bfloat16),
      pool_scale.reshape(1, pool_width), ln_g.reshape(1, sgu_width), ln_b.reshape(1, sgu_width),
      sgu_w.astype(jnp.bfloat16),
      jnp.broadcast_to(sgu_b[:, :, None], (SGU_HEADS, SGU_CHUNK, SGU_CHUNK)),
      on_pool.reshape(1, pool_width), on_sgu.reshape(1, sgu_width), w_out.astype(jnp.bfloat16))


def _topk_rows(s, k):
    n = s.shape[0]
    iota = lax.broadcasted_iota(jnp.int32, s.shape, 0)
    vals, idxs = [], []
    for _ in range(k):
        m = jnp.max(s, axis=0, keepdims=True)
        ix = jnp.min(jnp.where(s == m, iota, n), axis=0, keepdims=True)
        vals.append(m)
        idxs.append(ix)
        s = jnp.where(iota == ix, -jnp.inf, s)
    return vals, idxs


def _pair_candidates(v1, i1, v2, i2):
    k = PEER_TOPK
    v2c, i2c = jnp.concatenate(v2, axis=0), jnp.concatenate(i2, axis=0)
    m = v2c.shape[1]
    vals, experts, flats = [], [], []
    a = 0
    while k // (a + 1) > 1:
        rows = -(-(k // (a + 1)) // V7X_SUBLANES) * V7X_SUBLANES
        vals.append(v1[a] + v2c[:rows])
        experts.append(i1[a] * PEER_N_KEYS + i2c[:rows])
        flats.append(a * k + lax.broadcasted_iota(jnp.int32, (rows, m), 0))
        a += 1
    vals.append(jnp.concatenate(v1[a:], axis=0) + v2[0])
    experts.append(jnp.concatenate(i1[a:], axis=0) * PEER_N_KEYS + i2[0])
    flats.append((a + lax.broadcasted_iota(jnp.int32, (k - a, m), 0)) * k)
    return jnp.concatenate(vals, axis=0), jnp.concatenate(experts, axis=0), jnp.concatenate(flats, axis=0)


def _router_kernel(chunk_tokens, x_ref, nffn_ref, wq_ref, keys_ref, after_ref, h2_ref, idx_ref, slot_ref,
                   gate_ref, q_ref, idxt_ref, gatet_ref):
    del after_ref
    h2 = _rms(x_ref[...], nffn_ref[...]).astype(jnp.bfloat16)
    h2_ref[...] = h2
    q_ref[...] = jnp.dot(h2, wq_ref[...], preferred_element_type=jnp.float32).astype(jnp.bfloat16)
    dq = 2 * PEER_D_HALF
    nt = (((1,), (1,)), ((), ()))

    def head(hd, carry):
        off = pl.multiple_of(hd * dq, dq)
        s1 = lax.dot_general(keys_ref[0], q_ref[:, pl.ds(off, PEER_D_HALF)], nt,
                             preferred_element_type=jnp.float32)
        s2 = lax.dot_general(keys_ref[1], q_ref[:, pl.ds(off + PEER_D_HALF, PEER_D_HALF)], nt,
                             preferred_element_type=jnp.float32)
        v1, i1 = _topk_rows(s1, PEER_TOPK)
        v2, i2 = _topk_rows(s2, PEER_TOPK)
        cand, expert, flat = _pair_candidates(v1, i1, v2, i2)
        cv, ce = [], []
        for _ in range(PEER_TOPK):
            m = jnp.max(cand, axis=0, keepdims=True)
            ix = jnp.min(jnp.where(cand == m, flat, PEER_TOPK * PEER_TOPK), axis=0, keepdims=True)
            hit = flat == ix
            cv.append(m)
            ce.append(jnp.max(jnp.where(hit, expert, -1), axis=0, keepdims=True))
            cand = jnp.where(hit, -jnp.inf, cand)
        cvc = jnp.concatenate(cv, axis=0)
        e = jnp.exp(cvc - cv[0])
        gate = e / jnp.sum(e, axis=0, keepdims=True)
        row = pl.multiple_of(hd * PEER_TOPK, PEER_TOPK)
        idxt_ref[pl.ds(row, PEER_TOPK), :] = jnp.concatenate(ce, axis=0)
        gatet_ref[pl.ds(row, PEER_TOPK), :] = gate
        return carry

    lax.fori_loop(0, PEER_HEADS, head, 0)
    idx = idxt_ref[...].T
    idx_ref[...] = idx
    tok = pl.program_id(0) * idx.shape[0] + lax.broadcasted_iota(jnp.int32, idx.shape, 0)
    slot_ref[...] = _tile_linear_index(_score_word_row(tok, chunk_tokens), idx, PEER_N_KEYS * PEER_N_KEYS)
    gate_ref[...] = gatet_ref[...].T


def _router(x1, norm_ffn, wq, keys, tok0, T, after):
    D = x1.shape[1]
    tr = min(ROUTE_TILE, T)
    qw = wq.shape[1]
    assert tok0 % tr == 0 and T % tr == 0
    first = tok0 // tr
    full = lambda shape: pl.BlockSpec(shape, lambda i: (0,) * len(shape))
    return pl.pallas_call(
        functools.partial(_router_kernel, T),
        grid=(T // tr,),
        in_specs=[
            pl.BlockSpec((tr, D), lambda i: (first + i, 0)),
            full((1, D)),
            full((D, qw)),
            full((2, PEER_N_KEYS, PEER_D_HALF)),
            pl.BlockSpec(memory_space=pl.ANY),
        ],
        out_specs=[
            pl.BlockSpec((tr, D), lambda i: (i, 0)),
            pl.BlockSpec((tr, EXPERTS_PER_TOKEN), lambda i: (i, 0)),
            pl.BlockSpec((tr, EXPERTS_PER_TOKEN), lambda i: (i, 0)),
            pl.BlockSpec((tr, EXPERTS_PER_TOKEN), lambda i: (i, 0)),
        ],
        out_shape=[
            jax.ShapeDtypeStruct((T, D), jnp.bfloat16),
            jax.ShapeDtypeStruct((T, EXPERTS_PER_TOKEN), jnp.int32),
            jax.ShapeDtypeStruct((T, EXPERTS_PER_TOKEN), jnp.int32),
            jax.ShapeDtypeStruct((T, EXPERTS_PER_TOKEN), jnp.float32),
        ],
        scratch_shapes=[
            pltpu.VMEM((tr, qw), jnp.bfloat16),
            pltpu.VMEM((EXPERTS_PER_TOKEN, tr), jnp.int32),
            pltpu.VMEM((EXPERTS_PER_TOKEN, tr), jnp.float32),
        ],
        compiler_params=pltpu.CompilerParams(
            dimension_semantics=("parallel",), vmem_limit_bytes=TC_VMEM_LIMIT),
        name="router",
    )(x1, norm_ffn.reshape(1, D), wq, keys, after)


def _tile_linear_index(row, col, ncols):
    sub_bits, lane_bits = int(math.log2(V7X_SUBLANES)), int(math.log2(V7X_LANES))
    return ((row >> sub_bits) * (ncols * V7X_SUBLANES) + (col >> lane_bits) * (V7X_SUBLANES * V7X_LANES)
            + (row & (V7X_SUBLANES - 1)) * V7X_LANES + (col & (V7X_LANES - 1)))


def _score_tile(tokens):
    return min(SCORE_TOK_TILE, tokens)


def _score_word_row(tok, tokens):
    tm = _score_tile(tokens)
    assert tm & (tm - 1) == 0, "power-of-two score tile"
    return (tok >> int(math.log2(tm))) * (tm // 2) + (tok & (tm // 2 - 1))


def _scores_kernel(h_ref, ut_ref, o_ref):
    acc = jnp.dot(h_ref[...], ut_ref[...], preferred_element_type=jnp.float32)
    tm, tn = acc.shape
    bits = pltpu.bitcast(acc.astype(jnp.bfloat16).astype(jnp.float32), jnp.uint32)
    words = (bits[:tm // 2] >> 16) | (bits[tm // 2:] & jnp.uint32(HI_HALF))
    for n in range(tn // V7X_LANES):
        o_ref[:, n * V7X_SUBLANES:(n + 1) * V7X_SUBLANES, :] = words[:, n * V7X_LANES:(n + 1) * V7X_LANES].reshape(
            tm // 2 // V7X_SUBLANES, V7X_SUBLANES, V7X_LANES)


def _dense_scores(h2, u_t):
    T, D = h2.shape
    E = u_t.shape[1]
    tm, tn = _score_tile(T), min(SCORE_EXP_TILE, E)
    assert T % tm == 0 and E % tn == 0 and tm % (2 * V7X_SUBLANES) == 0 and tn % V7X_LANES == 0
    out = pl.pallas_call(
        _scores_kernel,
        grid=(T // tm, E // tn),
        in_specs=[pl.BlockSpec((tm, D), lambda i, j: (i, 0)), pl.BlockSpec((D, tn), lambda i, j: (0, j))],
        out_specs=pl.BlockSpec((tm // 2 // V7X_SUBLANES, tn // V7X_LANES * V7X_SUBLANES, V7X_LANES),
                               lambda i, j: (i, j, 0)),
        out_shape=jax.ShapeDtypeStruct((T // 2 // V7X_SUBLANES, E // V7X_LANES * V7X_SUBLANES, V7X_LANES),
                                       jnp.uint32),
        compiler_params=pltpu.CompilerParams(
            dimension_semantics=("parallel", "arbitrary"), vmem_limit_bytes=TC_VMEM_LIMIT),
        name="dense_scores",
    )(h2, u_t)
    return out.reshape(-1)


def _act_kernel(pre_ref, gate_ref, w_ref):
    w = (gate_ref[...] * _gelu(pre_ref[...])).astype(jnp.bfloat16).astype(jnp.float32)
    hi = pltpu.bitcast(w, jnp.uint32) & jnp.uint32(HI_HALF)
    w_ref[...] = pltpu.bitcast(hi | (hi >> 16), jnp.int32)


def _expert_weights(pre, gate):
    T, K = pre.shape
    te = min(EW_TILE, T)
    spec = pl.BlockSpec((te, K), lambda i: (i, 0))
    return pl.pallas_call(
        _act_kernel, grid=(T // te,), in_specs=[spec, spec], out_specs=spec,
        out_shape=jax.ShapeDtypeStruct((T, K), jnp.int32),
        compiler_params=pltpu.CompilerParams(dimension_semantics=("parallel",)),
        name="expert_weights",
    )(pre, gate)


def _final_kernel(x_ref, y_ref, g_ref, *rest):
    o_ref = rest[-1]
    o_ref[...] = _rms(x_ref[...] + y_ref[...], g_ref[...])


def _final(x1, x_tok0, peer, norm_final, out_prev, out_tok0, total):
    Tc, D = peer.shape
    te = min(EW_TILE, Tc)
    assert x_tok0 % te == 0 and out_tok0 % te == 0 and Tc % te == 0
    nblk = Tc // te
    spec = pl.BlockSpec((te, D), lambda i: (out_tok0 // te + i, 0))
    in_specs = [pl.BlockSpec((te, D), lambda i: (x_tok0 // te + i, 0)),
                pl.BlockSpec((te, D), lambda i: (i, 0)), pl.BlockSpec((1, D), lambda i: (0, 0))]
    args = [x1, peer, norm_final.reshape(1, D)]
    aliases = {}
    if out_prev is not None:
        in_specs.append(pl.BlockSpec(memory_space=pl.ANY))
        args.append(out_prev)
        aliases = {3: 0}
    return pl.pallas_call(
        _final_kernel, grid=(nblk,), in_specs=in_specs, out_specs=spec,
        out_shape=jax.ShapeDtypeStruct((total, D), jnp.float32),
        input_output_aliases=aliases,
        compiler_params=pltpu.CompilerParams(dimension_semantics=("parallel",)),
        name="final_norm",
    )(*args)


def _tree_sum(vals):
    while len(vals) > 1:
        nxt = [vals[i] + vals[i + 1] for i in range(0, len(vals) - 1, 2)]
        if len(vals) % 2:
            nxt.append(vals[-1])
        vals = nxt
    return vals[0]


def _sc_block_pipeline(nblk, items_per_token, loads, store, gather, compute):
    assert items_per_token % 2 == 0 and nblk >= 1

    for c in loads(0, 0):
        c.start()
    for c in loads(0, 0):
        c.wait()
    if nblk > 1:
        for c in loads(1, 1):
            c.start()
    gather(0, 0, 0, 0).start()

    @pl.loop(0, nblk)
    def _(b):
        slot = b % 2

        @pl.when(b >= 2)
        def _():
            store(b - 2, slot).wait()

        @pl.loop(0, SC_TOKENS)
        def _(t):
            for q in range(items_per_token):
                buf = q % 2
                if q + 1 < items_per_token:
                    gather(slot, t, q + 1, 1 - buf).start()
                else:
                    @pl.when(t + 1 < SC_TOKENS)
                    def _():
                        gather(slot, t + 1, 0, 1 - buf).start()

                    @pl.when(jnp.logical_and(t + 1 == SC_TOKENS, b + 1 < nblk))
                    def _():
                        for c in loads(b + 1, 1 - slot):
                            c.wait()
                        gather(1 - slot, 0, 0, 1 - buf).start()

                gather(slot, t, q, buf).wait()
                compute(slot, t, q, buf)

        store(b, slot).start()

        @pl.when(b + 2 < nblk)
        def _():
            for c in loads(b + 2, slot):
                c.start()

    if nblk >= 2:
        store(nblk - 2, nblk % 2).wait()
    store(nblk - 1, (nblk - 1) % 2).wait()


def _sc_mesh():
    return plsc.VectorSubcoreMesh(core_axis_name="c", subcore_axis_name="s")


def _sc_worker_id():
    return lax.axis_index("s") * V7X_SC_CORES + lax.axis_index("c")


def _sc_bf16(words):
    return plsc.bitcast(words, jnp.bfloat16)


def _sc_halves_f32(pairs):
    words = plsc.bitcast(pairs, jnp.uint32)
    return (plsc.bitcast(words << 16, jnp.float32),
            plsc.bitcast(words & jnp.uint32(HI_HALF), jnp.float32))


def _pick_parts(T, K):
    L = V7X_SC_LANES
    nb = T // V7X_SC_WORKERS
    half_tile = _score_tile(T) // 2
    assert T % V7X_SC_WORKERS == 0 and nb <= SC_PICK_TOKENS
    assert half_tile % nb == 0 and half_tile & (half_tile - 1) == 0
    scratch = [pltpu.VMEM((nb, K), jnp.int32), pltpu.VMEM((nb, K), jnp.uint32),
               pltpu.VMEM((nb, K), jnp.float32), pltpu.SemaphoreType.DMA(())]

    def gather(flat_hbm, idx_v, word_v, sem, t):
        return pltpu.make_async_copy(flat_hbm.at[idx_v.at[t]], word_v.at[t], sem)

    def start(flat_hbm, slots_hbm, idx_v, word_v, val_v, sem):
        pltpu.sync_copy(slots_hbm.at[pl.ds(_sc_worker_id() * nb, nb)], idx_v)

        @pl.loop(0, nb)
        def _(t):
            gather(flat_hbm, idx_v, word_v, sem, t).start()

    def finish(flat_hbm, out_hbm, idx_v, word_v, val_v, sem):
        base = _sc_worker_id() * nb
        shift = jnp.full((L,), jnp.where((base & half_tile) == 0, 16, 0), jnp.uint32)

        @pl.loop(0, nb)
        def _(t):
            gather(flat_hbm, idx_v, word_v, sem, t).wait()
            for j in range(K // L):
                words = word_v[t, pl.ds(j * L, L)]
                val_v[t, pl.ds(j * L, L)] = plsc.bitcast((words << shift) & jnp.uint32(HI_HALF), jnp.float32)

        pltpu.sync_copy(val_v, out_hbm.at[pl.ds(base, nb)])

    return scratch, start, finish


def _mix_parts(T, K, DW):
    D = 2 * DW
    L, G = V7X_SC_LANES, SC_GATHER
    nj = SC_MIX_CHUNK // L
    tok_per_w = T // V7X_SC_WORKERS
    assert T % (V7X_SC_WORKERS * SC_TOKENS) == 0 and K % (2 * G) == 0
    assert DW % SC_MIX_CHUNK == 0 and G % SC_MIX_GROUP == 0
    scratch = [
        pltpu.VMEM((2, SC_TOKENS, K), jnp.int32),
        pltpu.VMEM((2, SC_TOKENS, K), jnp.int32),
        pltpu.VMEM((2, G, DW), jnp.uint32),
        pltpu.VMEM((2, SC_TOKENS, D), jnp.float32),
        pltpu.SemaphoreType.DMA((2,)),
        pltpu.SemaphoreType.DMA((2,)),
        pltpu.SemaphoreType.DMA((2,)),
        pltpu.SemaphoreType.DMA((2,)),
    ]

    def run(w_hbm, idx_hbm, tab_hbm, out_hbm, idx_v, w_v, rows_v, out_v, idx_sems, w_sems, out_sems, row_sems):
        base = _sc_worker_id() * tok_per_w

        def loads(b, slot):
            toks = pl.ds(base + b * SC_TOKENS, SC_TOKENS)
            return [pltpu.make_async_copy(idx_hbm.at[toks], idx_v.at[slot], idx_sems.at[slot]),
                    pltpu.make_async_copy(w_hbm.at[toks], w_v.at[slot], w_sems.at[slot])]

        def store(b, slot):
            toks = pl.ds(base + b * SC_TOKENS, SC_TOKENS)
            return pltpu.make_async_copy(out_v.at[slot], out_hbm.at[toks], out_sems.at[slot])

        def gather(slot, t, q, buf):
            return pltpu.make_async_copy(
                tab_hbm.at[idx_v.at[slot, t, pl.ds(q * G, G)]], rows_v.at[buf], row_sems.at[buf])

        def compute(slot, t, q, buf):
            ssplat = jnp.full((L,), slot, jnp.int32)
            tsplat = jnp.full((L,), t, jnp.int32)
            for c in range(DW // SC_MIX_CHUNK):
                def body(kg, acc):
                    kk = kg * SC_MIX_GROUP
                    wks = [_sc_bf16(plsc.load_gather(
                        w_v, [ssplat, tsplat, jnp.full((L,), q * G + i, jnp.int32) + kk]))
                        for i in range(SC_MIX_GROUP)]
                    out = []
                    for j in range(nj):
                        prods = [wks[i] * _sc_bf16(rows_v[buf, kk + i, pl.ds(c * SC_MIX_CHUNK + j * L, L)])
                                 for i in range(SC_MIX_GROUP)]
                        lo, hi = _sc_halves_f32(_tree_sum(prods))
                        out += [acc[2 * j] + lo, acc[2 * j + 1] + hi]
                    return tuple(out)

                zero = jnp.zeros((L,), jnp.float32)
                acc = plsc.parallel_loop(0, G // SC_MIX_GROUP, carry=(zero,) * (2 * nj))(body)
                for j in range(nj):
                    for half in range(2):
                        dst = out_v.at[slot, t, pl.ds(half * DW + c * SC_MIX_CHUNK + j * L, L)]
                        if q == 0:
                            dst[...] = acc[2 * j + half]
                        else:
                            plsc.addupdate(dst, acc[2 * j + half])

        _sc_block_pipeline(tok_per_w // SC_TOKENS, K // G, loads, store, gather, compute)

    return scratch, run


def _sc_kernel(out_type, scratch_types, name):
    return functools.partial(
        pl.kernel, mesh=_sc_mesh(), out_type=out_type, scratch_types=scratch_types,
        compiler_params=pltpu.CompilerParams(needs_layout_passes=False), name=name)


def _pack_table(a):
    n, c = a.shape
    half = c // 2
    L, rb = V7X_SC_LANES, SC_PACK_ROWS
    rows_per_w = n // V7X_SC_WORKERS
    nblk = rows_per_w // rb
    assert n % (V7X_SC_WORKERS * rb) == 0 and half % L == 0

    @_sc_kernel(jax.ShapeDtypeStruct((n, half), jnp.uint32),
                [pltpu.VMEM((2, rb, c), jnp.float32), pltpu.VMEM((2, rb, half), jnp.uint32),
                 pltpu.SemaphoreType.DMA((2,)), pltpu.SemaphoreType.DMA((2,))], "pack_table")
    def k(a_hbm, o_hbm, a_v, o_v, in_sems, out_sems):
        base = _sc_worker_id() * rows_per_w

        def load(b, slot):
            return pltpu.make_async_copy(a_hbm.at[pl.ds(base + b * rb, rb)], a_v.at[slot], in_sems.at[slot])

        def store(b, slot):
            return pltpu.make_async_copy(o_v.at[slot], o_hbm.at[pl.ds(base + b * rb, rb)], out_sems.at[slot])

        load(0, 0).start()

        @pl.loop(0, nblk)
        def _(b):
            slot = b % 2
            load(b, slot).wait()

            @pl.when(b + 1 < nblk)
            def _():
                load(b + 1, 1 - slot).start()

            @pl.when(b >= 2)
            def _():
                store(b - 2, slot).wait()

            @pl.loop(0, rb)
            def _(r):
                for j in range(half // L):
                    pairs = plsc.pack(a_v[slot, r, pl.ds(j * L, L)], a_v[slot, r, pl.ds(half + j * L, L)],
                                      format=plsc.PackFormat.INTERLEAVED)
                    o_v[slot, r, pl.ds(j * L, L)] = plsc.bitcast(pairs, jnp.uint32)

            store(b, slot).start()

        if nblk >= 2:
            store(nblk - 2, nblk % 2).wait()
        store(nblk - 1, (nblk - 1) % 2).wait()

    return k(a)


def _pick_scores(flat, slots, after):
    T, K = slots.shape
    scratch, start, finish = _pick_parts(T, K)

    @_sc_kernel(jax.ShapeDtypeStruct((T, K), jnp.float32), scratch, "pick_scores")
    def k(flat_hbm, slots_hbm, after_hbm, pre_hbm, *pick_scratch):
        del after_hbm
        start(flat_hbm, slots_hbm, *pick_scratch)
        finish(flat_hbm, pre_hbm, *pick_scratch)

    return k(flat, slots, after)


def _expert_mix(w, idx, table):
    T, K = w.shape
    DW = table.shape[1]
    scratch, run = _mix_parts(T, K, DW)

    @_sc_kernel(jax.ShapeDtypeStruct((T, 2 * DW), jnp.float32), scratch, "expert_mix")
    def k(w_hbm, idx_hbm, tab_hbm, out_hbm, *mix_scratch):
        run(w_hbm, idx_hbm, tab_hbm, out_hbm, *mix_scratch)

    return k(w, idx, table)


def _chunk_sizes(total):
    ramp, size = [EDGE_CHUNK], EDGE_CHUNK
    while size < MAX_CHUNK:
        ramp.append(size)
        size *= 2
    middle = total - 2 * sum(ramp)
    assert middle >= 0 and middle % MAX_CHUNK == 0
    return ramp + [MAX_CHUNK] * (middle // MAX_CHUNK) + ramp[::-1]


def kernel(x, norm_mix, w_in, pool_w, pool_scale, sgu_ln_g, sgu_ln_b, sgu_w, sgu_b, out_norm_pool,
           out_norm_sgu, w_out, norm_ffn, peer_wq, peer_keys, peer_u, peer_v, norm_final):
    B, S, D = x.shape
    assert norm_mix.shape[0] == 1, "single-layer block"
    T = B * S
    mix_args = (norm_mix[0], w_in[0], pool_w[0], pool_scale[0], sgu_ln_g[0], sgu_ln_b[0],
                sgu_w[0], sgu_b[0], out_norm_pool[0], out_norm_sgu[0], w_out[0])
    x1_parts = [(b * S, _mixer(x, b, 1, *mix_args).reshape(S, D)) for b in range(B)]
    wq = peer_wq[0].astype(jnp.bfloat16)
    keys = peer_keys[0].astype(jnp.bfloat16)
    u_t = _transposed_bf16(peer_u[0])
    v_tab = _pack_table(peer_v[0])
    out = None
    tok0 = 0
    ws = [norm_ffn, v_tab]
    peers = [norm_ffn, norm_ffn]
    for tc in _chunk_sizes(T):
        part0, x1 = [p for p in x1_parts if p[0] <= tok0][-1]
        assert tok0 + tc <= part0 + x1.shape[0], "a token chunk must lie inside one mixer call"
        h2, idx, slots, gate = _router(x1, norm_ffn[0], wq, keys, tok0 - part0, tc, ws[-2])
        pre = _pick_scores(_dense_scores(h2, u_t), slots, peers[-2])
        w = _expert_weights(pre, gate)
        ws.append(w)
        peer = _expert_mix(w, idx, v_tab)
        peers.append(peer)
        out = _final(x1, tok0 - part0, peer, norm_final, out, tok0, T)
        tok0 += tc
    return out.reshape(B, S, D)
```

```python
import functools
import math

import jax
import jax.numpy as jnp
from jax import lax
from jax.experimental import pallas as pl
from jax.experimental.pallas import tpu as pltpu
from jax.experimental.pallas import tpu_sc as plsc

POOL_WINDOWS = (2, 4, 8, 16)
N_POOL_GROUPS = len(POOL_WINDOWS)
SGU_HEADS = 4
SGU_CHUNK = 128
PEER_HEADS = 8
PEER_N_KEYS = 128
PEER_D_HALF = 128
PEER_TOPK = 16
NORM_EPS = 1e-6
EXPERTS_PER_TOKEN = PEER_HEADS * PEER_TOPK

V7X_LANES = 128
V7X_SUBLANES = 8
V7X_SC_CORES = 2
V7X_SC_SUBCORES = 16
V7X_SC_LANES = 16
V7X_SC_WORKERS = V7X_SC_CORES * V7X_SC_SUBCORES

HALO = max(POOL_WINDOWS)
MIX_TILE = 512
ROUTE_TILE = 1024
EW_TILE = 512
SCORE_TOK_TILE = 2048
SCORE_EXP_TILE = 1024
SC_GATHER = 64
SC_TOKENS = 8
SC_MIX_CHUNK = 128
SC_MIX_GROUP = 4
HI_HALF = 0xFFFF0000
V7X_VMEM_BYTES = 64 * 1024 * 1024
TC_VMEM_LIMIT = V7X_VMEM_BYTES * 3 // 4
EDGE_CHUNK = 512
MAX_CHUNK = 2048


def _rms(x, g):
    inv = lax.rsqrt(jnp.mean(x * x, axis=-1, keepdims=True) + NORM_EPS)
    return x * inv * g


def _pack_halves(bits):
    half = bits.shape[1] // 2
    return (bits[:, :half] >> 16) | (bits[:, half:] & jnp.uint32(HI_HALF))


def _pack_table_kernel(a_ref, o_ref):
    o_ref[...] = _pack_halves(pltpu.bitcast(a_ref[...].astype(jnp.bfloat16).astype(jnp.float32), jnp.uint32))


def _pack_table(a):
    n, c = a.shape
    rows = min(EW_TILE, n)
    return pl.pallas_call(
        _pack_table_kernel, grid=(n // rows,),
        in_specs=[pl.BlockSpec((rows, c), lambda i: (i, 0))],
        out_specs=pl.BlockSpec((rows, c // 2), lambda i: (i, 0)),
        out_shape=jax.ShapeDtypeStruct((n, c // 2), jnp.uint32),
        compiler_params=pltpu.CompilerParams(dimension_semantics=("parallel",)),
        name="pack_table",
    )(a)


def _transpose_cast_kernel(a_ref, o_ref):
    o_ref[...] = a_ref[...].T.astype(o_ref.dtype)


def _transposed_bf16(a):
    n, c = a.shape
    rows = min(SCORE_EXP_TILE, n)
    return pl.pallas_call(
        _transpose_cast_kernel, grid=(n // rows,),
        in_specs=[pl.BlockSpec((rows, c), lambda i: (i, 0))],
        out_specs=pl.BlockSpec((c, rows), lambda i: (0, i)),
        out_shape=jax.ShapeDtypeStruct((c, n), jnp.bfloat16),
        compiler_params=pltpu.CompilerParams(dimension_semantics=("parallel",)),
        name="transpose_table",
    )(a)


def _gelu(x):
    return 0.5 * x * (1.0 + lax.erf(x * math.sqrt(0.5)))


def _mixer_kernel(x_ref, xh_ref, nmix_ref, win_ref, poolw_ref, pscale_ref, lng_ref, lnb_ref,
                  sguw_ref, sgub_ref, onp_ref, ons_ref, wout_ref, o_ref, pext_ref, mix_ref):
    i = pl.program_id(1)
    ts = x_ref.shape[1]
    pool_w = pscale_ref.shape[1]
    gdim = pool_w // N_POOL_GROUPS
    sgu_w = lng_ref.shape[1]
    hdim = sgu_w // SGU_HEADS

    x = x_ref[0]
    h = _rms(x, nmix_ref[...]).astype(jnp.bfloat16)
    z = jnp.dot(h, win_ref[...], preferred_element_type=jnp.float32)
    p = z[:, :pool_w]

    hh = _rms(xh_ref[0], nmix_ref[...]).astype(jnp.bfloat16)
    ph = jnp.dot(hh, win_ref[:, :pool_w], preferred_element_type=jnp.float32)
    ph = jnp.where(i > 0, ph, 0.0)
    pext_ref[0:HALO, :] = ph
    pext_ref[HALO:HALO + ts, :] = p

    pos = i * ts + lax.broadcasted_iota(jnp.int32, (ts, 1), 0)
    ssq = jnp.zeros((ts, 1), jnp.float32)
    a_parts = []
    for g, win in enumerate(POOL_WINDOWS):
        cols = slice(g * gdim, (g + 1) * gdim)
        s = pext_ref[HALO:HALO + ts, cols]
        for j in range(1, win):
            s = s + pext_ref[HALO - j:HALO - j + ts, cols]
        cnt = jnp.minimum(pos + 1, win).astype(jnp.float32)
        d = (s / cnt - p[:, cols]).astype(jnp.bfloat16)
        a = jnp.dot(d, poolw_ref[g], preferred_element_type=jnp.float32) * pscale_ref[:, cols]
        ssq = ssq + jnp.sum(a * a, axis=-1, keepdims=True)
        a_parts.append(a)
    inv_a = lax.rsqrt(ssq / pool_w + NORM_EPS)
    for g in range(N_POOL_GROUPS):
        cols = slice(g * gdim, (g + 1) * gdim)
        mix_ref[:, cols] = (a_parts[g] * inv_a * onp_ref[:, cols]).astype(jnp.bfloat16)

    gz = _gelu(z[:, pool_w:])
    tril = (lax.broadcasted_iota(jnp.int32, (SGU_CHUNK, SGU_CHUNK), 0)
            >= lax.broadcasted_iota(jnp.int32, (SGU_CHUNK, SGU_CHUNK), 1))
    ssq = jnp.zeros((ts, 1), jnp.float32)
    b_parts = []
    for hd in range(SGU_HEADS):
        cols = slice(hd * hdim, (hd + 1) * hdim)
        u = gz[:, hd * hdim:(hd + 1) * hdim]
        v = gz[:, sgu_w + hd * hdim:sgu_w + (hd + 1) * hdim]
        mu = jnp.mean(v, axis=-1, keepdims=True)
        vc = v - mu
        var = jnp.mean(vc * vc, axis=-1, keepdims=True)
        vn = (vc * lax.rsqrt(var + NORM_EPS) * lng_ref[:, cols] + lnb_ref[:, cols]).astype(jnp.bfloat16)
        w = jnp.where(tril, sguw_ref[hd], jnp.zeros((), sguw_ref.dtype))
        mixed = [jnp.dot(w, vn[n * SGU_CHUNK:(n + 1) * SGU_CHUNK], preferred_element_type=jnp.float32)
                 + sgub_ref[hd] for n in range(ts // SGU_CHUNK)]
        b = u * jnp.concatenate(mixed, axis=0)
        ssq = ssq + jnp.sum(b * b, axis=-1, keepdims=True)
        b_parts.append(b)
    inv_b = lax.rsqrt(ssq / sgu_w + NORM_EPS)
    for hd in range(SGU_HEADS):
        cols = slice(hd * hdim, (hd + 1) * hdim)
        mix_ref[:, pool_w + hd * hdim:pool_w + (hd + 1) * hdim] = (
            b_parts[hd] * inv_b * ons_ref[:, cols]).astype(jnp.bfloat16)

    o_ref[0] = x + jnp.dot(mix_ref[...], wout_ref[...], preferred_element_type=jnp.float32)


def _mixer(x, b0, nb, norm_mix, w_in, pool_w, pool_scale, ln_g, ln_b, sgu_w, sgu_b, on_pool, on_sgu, w_out):
    _, S, D = x.shape
    ts = min(MIX_TILE, S)
    pool_width = pool_scale.size
    sgu_width = ln_g.size
    in_width = w_in.shape[1]
    gdim = pool_width // N_POOL_GROUPS
    halo_blocks = ts // HALO
    full = lambda shape: pl.BlockSpec(shape, lambda b, i: (0,) * len(shape))
    return pl.pallas_call(
        _mixer_kernel,
        grid=(nb, S // ts),
        in_specs=[
            pl.BlockSpec((1, ts, D), lambda b, i: (b0 + b, i, 0)),
            pl.BlockSpec((1, HALO, D), lambda b, i: (b0 + b, jnp.maximum(i * halo_blocks - 1, 0), 0)),
            full((1, D)),
            full((D, in_width)),
            full((N_POOL_GROUPS, gdim, gdim)),
            full((1, pool_width)),
            full((1, sgu_width)),
            full((1, sgu_width)),
            full((SGU_HEADS, SGU_CHUNK, SGU_CHUNK)),
            full((SGU_HEADS, SGU_CHUNK, SGU_CHUNK)),
            full((1, pool_width)),
            full((1, sgu_width)),
            full((pool_width + sgu_width, D)),
        ],
        out_specs=pl.BlockSpec((1, ts, D), lambda b, i: (b, i, 0)),
        out_shape=jax.ShapeDtypeStruct((nb, S, D), jnp.float32),
        scratch_shapes=[
            pltpu.VMEM((HALO + ts, pool_width), jnp.float32),
            pltpu.VMEM((ts, pool_width + sgu_width), jnp.bfloat16),
        ],
        compiler_params=pltpu.CompilerParams(
            dimension_semantics=("parallel", "arbitrary"), vmem_limit_bytes=TC_VMEM_LIMIT),
        name="mixer",
    )(x, x, norm_mix.reshape(1, D), w_in.astype(jnp.bfloat16), pool_w.astype(jnp.bfloat16),
      pool_scale.reshape(1, pool_width), ln_g.reshape(1, sgu_width), ln_b.reshape(1, sgu_width),
      sgu_w.astype(jnp.bfloat16),
      jnp.broadcast_to(sgu_b[:, :, None], (SGU_HEADS, SGU_CHUNK, SGU_CHUNK)),
      on_pool.reshape(1, pool_width), on_sgu.reshape(1, sgu_width), w_out.astype(jnp.bfloat16))


def _topk_rows(s, k):
    n = s.shape[0]
    iota = lax.broadcasted_iota(jnp.int32, s.shape, 0)
    vals, idxs = [], []
    for _ in range(k):
        m = jnp.max(s, axis=0, keepdims=True)
        ix = jnp.min(jnp.where(s == m, iota, n), axis=0, keepdims=True)
        vals.append(m)
        idxs.append(ix)
        s = jnp.where(iota == ix, -jnp.inf, s)
    return vals, idxs


def _pair_candidates(v1, i1, v2, i2):
    k = PEER_TOPK
    v2c, i2c = jnp.concatenate(v2, axis=0), jnp.concatenate(i2, axis=0)
    m = v2c.shape[1]
    vals, experts, flats = [], [], []
    a = 0
    while k // (a + 1) > 1:
        rows = -(-(k // (a + 1)) // V7X_SUBLANES) * V7X_SUBLANES
        vals.append(v1[a] + v2c[:rows])
        experts.append(i1[a] * PEER_N_KEYS + i2c[:rows])
        flats.append(a * k + lax.broadcasted_iota(jnp.int32, (rows, m), 0))
        a += 1
    vals.append(jnp.concatenate(v1[a:], axis=0) + v2[0])
    experts.append(jnp.concatenate(i1[a:], axis=0) * PEER_N_KEYS + i2[0])
    flats.append((a + lax.broadcasted_iota(jnp.int32, (k - a, m), 0)) * k)
    return jnp.concatenate(vals, axis=0), jnp.concatenate(experts, axis=0), jnp.concatenate(flats, axis=0)


def _router_kernel(x_ref, nffn_ref, wq_ref, keys_ref, after_ref, h2_ref, idx_ref, gate_ref,
                   q_ref, idxt_ref, gatet_ref):
    del after_ref
    h2 = _rms(x_ref[...], nffn_ref[...]).astype(jnp.bfloat16)
    h2_ref[...] = h2
    q_ref[...] = jnp.dot(h2, wq_ref[...], preferred_element_type=jnp.float32).astype(jnp.bfloat16)
    dq = 2 * PEER_D_HALF
    nt = (((1,), (1,)), ((), ()))

    def head(hd, carry):
        off = pl.multiple_of(hd * dq, dq)
        s1 = lax.dot_general(keys_ref[0], q_ref[:, pl.ds(off, PEER_D_HALF)], nt,
                             preferred_element_type=jnp.float32)
        s2 = lax.dot_general(keys_ref[1], q_ref[:, pl.ds(off + PEER_D_HALF, PEER_D_HALF)], nt,
                             preferred_element_type=jnp.float32)
        v1, i1 = _topk_rows(s1, PEER_TOPK)
        v2, i2 = _topk_rows(s2, PEER_TOPK)
        cand, expert, flat = _pair_candidates(v1, i1, v2, i2)
        cv, ce = [], []
        for _ in range(PEER_TOPK):
            m = jnp.max(cand, axis=0, keepdims=True)
            ix = jnp.min(jnp.where(cand == m, flat, PEER_TOPK * PEER_TOPK), axis=0, keepdims=True)
            hit = flat == ix
            cv.append(m)
            ce.append(jnp.max(jnp.where(hit, expert, -1), axis=0, keepdims=True))
            cand = jnp.where(hit, -jnp.inf, cand)
        cvc = jnp.concatenate(cv, axis=0)
        e = jnp.exp(cvc - cv[0])
        gate = e / jnp.sum(e, axis=0, keepdims=True)
        row = pl.multiple_of(hd * PEER_TOPK, PEER_TOPK)
        idxt_ref[pl.ds(row, PEER_TOPK), :] = jnp.concatenate(ce, axis=0)
        gatet_ref[pl.ds(row, PEER_TOPK), :] = gate
        return carry

    lax.fori_loop(0, PEER_HEADS, head, 0)
    idx_ref[...] = idxt_ref[...].T
    gate_ref[...] = gatet_ref[...].T


def _router(x1, norm_ffn, wq, keys, tok0, T, after):
    D = x1.shape[1]
    tr = min(ROUTE_TILE, T)
    qw = wq.shape[1]
    assert tok0 % tr == 0 and T % tr == 0
    first = tok0 // tr
    full = lambda shape: pl.BlockSpec(shape, lambda i: (0,) * len(shape))
    return pl.pallas_call(
        _router_kernel,
        grid=(T // tr,),
        in_specs=[
            pl.BlockSpec((tr, D), lambda i: (first + i, 0)),
            full((1, D)),
            full((D, qw)),
            full((2, PEER_N_KEYS, PEER_D_HALF)),
            pl.BlockSpec(memory_space=pl.ANY),
        ],
        out_specs=[
            pl.BlockSpec((tr, D), lambda i: (i, 0)),
            pl.BlockSpec((tr, EXPERTS_PER_TOKEN), lambda i: (i, 0)),
            pl.BlockSpec((tr, EXPERTS_PER_TOKEN), lambda i: (i, 0)),
        ],
        out_shape=[
            jax.ShapeDtypeStruct((T, D), jnp.bfloat16),
            jax.ShapeDtypeStruct((T, EXPERTS_PER_TOKEN), jnp.int32),
            jax.ShapeDtypeStruct((T, EXPERTS_PER_TOKEN), jnp.float32),
        ],
        scratch_shapes=[
            pltpu.VMEM((tr, qw), jnp.bfloat16),
            pltpu.VMEM((EXPERTS_PER_TOKEN, tr), jnp.int32),
            pltpu.VMEM((EXPERTS_PER_TOKEN, tr), jnp.float32),
        ],
        compiler_params=pltpu.CompilerParams(
            dimension_semantics=("parallel",), vmem_limit_bytes=TC_VMEM_LIMIT),
        name="router",
    )(x1, norm_ffn.reshape(1, D), wq, keys, after)


def _scores_kernel(h_ref, ut_ref, idx_ref, pre_ref):
    j = pl.program_id(1)
    acc = jnp.dot(h_ref[...], ut_ref[...], preferred_element_type=jnp.float32)
    groups = acc.shape[1] // V7X_LANES
    lane_bits = int(math.log2(V7X_LANES))
    idx = idx_ref[...]
    group, lane = idx >> lane_bits, idx & (V7X_LANES - 1)
    picked = jnp.zeros(idx.shape, jnp.float32)
    for n in range(groups):
        vals = jnp.take_along_axis(acc[:, n * V7X_LANES:(n + 1) * V7X_LANES], lane, axis=1)
        picked = jnp.where(group == j * groups + n, vals, picked)

    @pl.when(j == 0)
    def _():
        pre_ref[...] = picked

    @pl.when(j > 0)
    def _():
        pre_ref[...] += picked


def _expert_scores(h2, u_t, idx):
    T, D = h2.shape
    E = u_t.shape[1]
    K = idx.shape[1]
    tm, tn = min(SCORE_TOK_TILE, T), min(SCORE_EXP_TILE, E)
    assert T % tm == 0 and E % tn == 0 and tn % V7X_LANES == 0 and K == V7X_LANES
    return pl.pallas_call(
        _scores_kernel,
        grid=(T // tm, E // tn),
        in_specs=[pl.BlockSpec((tm, D), lambda i, j: (i, 0)), pl.BlockSpec((D, tn), lambda i, j: (0, j)),
                  pl.BlockSpec((tm, K), lambda i, j: (i, 0))],
        out_specs=pl.BlockSpec((tm, K), lambda i, j: (i, 0)),
        out_shape=jax.ShapeDtypeStruct((T, K), jnp.float32),
        compiler_params=pltpu.CompilerParams(
            dimension_semantics=("parallel", "arbitrary"), vmem_limit_bytes=TC_VMEM_LIMIT),
        name="expert_scores",
    )(h2, u_t, idx)


def _act_kernel(pre_ref, gate_ref, w_ref):
    w = (gate_ref[...] * _gelu(pre_ref[...])).astype(jnp.bfloat16).astype(jnp.float32)
    hi = pltpu.bitcast(w, jnp.uint32) & jnp.uint32(HI_HALF)
    w_ref[...] = pltpu.bitcast(hi | (hi >> 16), jnp.int32)


def _expert_weights(pre, gate):
    T, K = pre.shape
    te = min(EW_TILE, T)
    spec = pl.BlockSpec((te, K), lambda i: (i, 0))
    return pl.pallas_call(
        _act_kernel, grid=(T // te,), in_specs=[spec, spec], out_specs=spec,
        out_shape=jax.ShapeDtypeStruct((T, K), jnp.int32),
        compiler_params=pltpu.CompilerParams(dimension_semantics=("parallel",)),
        name="expert_weights",
    )(pre, gate)


def _final_kernel(x_ref, y_ref, g_ref, *rest):
    o_ref = rest[-1]
    o_ref[...] = _rms(x_ref[...] + y_ref[...], g_ref[...])


def _final(x1, x_tok0, peer, norm_final, out_prev, out_tok0, total):
    Tc, D = peer.shape
    te = min(EW_TILE, Tc)
    assert x_tok0 % te == 0 and out_tok0 % te == 0 and Tc % te == 0
    nblk = Tc // te
    spec = pl.BlockSpec((te, D), lambda i: (out_tok0 // te + i, 0))
    in_specs = [pl.BlockSpec((te, D), lambda i: (x_tok0 // te + i, 0)),
                pl.BlockSpec((te, D), lambda i: (i, 0)), pl.BlockSpec((1, D), lambda i: (0, 0))]
    args = [x1, peer, norm_final.reshape(1, D)]
    aliases = {}
    if out_prev is not None:
        in_specs.append(pl.BlockSpec(memory_space=pl.ANY))
        args.append(out_prev)
        aliases = {3: 0}
    return pl.pallas_call(
        _final_kernel, grid=(nblk,), in_specs=in_specs, out_specs=spec,
        out_shape=jax.ShapeDtypeStruct((total, D), jnp.float32),
        input_output_aliases=aliases,
        compiler_params=pltpu.CompilerParams(dimension_semantics=("parallel",)),
        name="final_norm",
    )(*args)


def _tree_sum(vals):
    while len(vals) > 1:
        nxt = [vals[i] + vals[i + 1] for i in range(0, len(vals) - 1, 2)]
        if len(vals) % 2:
            nxt.append(vals[-1])
        vals = nxt
    return vals[0]


def _sc_block_pipeline(nblk, items_per_token, loads, store, gather, compute):
    assert items_per_token % 2 == 0 and nblk >= 1

    for c in loads(0, 0):
        c.start()
    for c in loads(0, 0):
        c.wait()
    if nblk > 1:
        for c in loads(1, 1):
            c.start()
    gather(0, 0, 0, 0).start()

    @pl.loop(0, nblk)
    def _(b):
        slot = b % 2

        @pl.when(b >= 2)
        def _():
            store(b - 2, slot).wait()

        @pl.loop(0, SC_TOKENS)
        def _(t):
            for q in range(items_per_token):
                buf = q % 2
                if q + 1 < items_per_token:
                    gather(slot, t, q + 1, 1 - buf).start()
                else:
                    @pl.when(t + 1 < SC_TOKENS)
                    def _():
                        gather(slot, t + 1, 0, 1 - buf).start()

                    @pl.when(jnp.logical_and(t + 1 == SC_TOKENS, b + 1 < nblk))
                    def _():
                        for c in loads(b + 1, 1 - slot):
                            c.wait()
                        gather(1 - slot, 0, 0, 1 - buf).start()

                gather(slot, t, q, buf).wait()
                compute(slot, t, q, buf)

        store(b, slot).start()

        @pl.when(b + 2 < nblk)
        def _():
            for c in loads(b + 2, slot):
                c.start()

    if nblk >= 2:
        store(nblk - 2, nblk % 2).wait()
    store(nblk - 1, (nblk - 1) % 2).wait()


def _sc_mesh():
    return plsc.VectorSubcoreMesh(core_axis_name="c", subcore_axis_name="s")


def _sc_worker_id():
    return lax.axis_index("s") * V7X_SC_CORES + lax.axis_index("c")


def _sc_bf16(words):
    return plsc.bitcast(words, jnp.bfloat16)


def _sc_halves_f32(pairs):
    words = plsc.bitcast(pairs, jnp.uint32)
    return (plsc.bitcast(words << 16, jnp.float32),
            plsc.bitcast(words & jnp.uint32(HI_HALF), jnp.float32))


def _mix_parts(T, K, DW):
    D = 2 * DW
    L, G = V7X_SC_LANES, SC_GATHER
    nj = SC_MIX_CHUNK // L
    tok_per_w = T // V7X_SC_WORKERS
    assert T % (V7X_SC_WORKERS * SC_TOKENS) == 0 and K % (2 * G) == 0
    assert DW % SC_MIX_CHUNK == 0 and G % SC_MIX_GROUP == 0
    scratch = [
        pltpu.VMEM((2, SC_TOKENS, K), jnp.int32),
        pltpu.VMEM((2, SC_TOKENS, K), jnp.int32),
        pltpu.VMEM((2, G, DW), jnp.uint32),
        pltpu.VMEM((2, SC_TOKENS, D), jnp.float32),
        pltpu.SemaphoreType.DMA((2,)),
        pltpu.SemaphoreType.DMA((2,)),
        pltpu.SemaphoreType.DMA((2,)),
        pltpu.SemaphoreType.DMA((2,)),
    ]

    def run(w_hbm, idx_hbm, tab_hbm, out_hbm, idx_v, w_v, rows_v, out_v, idx_sems, w_sems, out_sems, row_sems):
        base = _sc_worker_id() * tok_per_w

        def loads(b, slot):
            toks = pl.ds(base + b * SC_TOKENS, SC_TOKENS)
            return [pltpu.make_async_copy(idx_hbm.at[toks], idx_v.at[slot], idx_sems.at[slot]),
                    pltpu.make_async_copy(w_hbm.at[toks], w_v.at[slot], w_sems.at[slot])]

        def store(b, slot):
            toks = pl.ds(base + b * SC_TOKENS, SC_TOKENS)
            return pltpu.make_async_copy(out_v.at[slot], out_hbm.at[toks], out_sems.at[slot])

        def gather(slot, t, q, buf):
            return pltpu.make_async_copy(
                tab_hbm.at[idx_v.at[slot, t, pl.ds(q * G, G)]], rows_v.at[buf], row_sems.at[buf])

        def compute(slot, t, q, buf):
            ssplat = jnp.full((L,), slot, jnp.int32)
            tsplat = jnp.full((L,), t, jnp.int32)
            for c in range(DW // SC_MIX_CHUNK):
                def body(kg, acc):
                    kk = kg * SC_MIX_GROUP
                    wks = [_sc_bf16(plsc.load_gather(
                        w_v, [ssplat, tsplat, jnp.full((L,), q * G + i, jnp.int32) + kk]))
                        for i in range(SC_MIX_GROUP)]
                    out = []
                    for j in range(nj):
                        prods = [wks[i] * _sc_bf16(rows_v[buf, kk + i, pl.ds(c * SC_MIX_CHUNK + j * L, L)])
                                 for i in range(SC_MIX_GROUP)]
                        lo, hi = _sc_halves_f32(_tree_sum(prods))
                        out += [acc[2 * j] + lo, acc[2 * j + 1] + hi]
                    return tuple(out)

                zero = jnp.zeros((L,), jnp.float32)
                acc = plsc.parallel_loop(0, G // SC_MIX_GROUP, carry=(zero,) * (2 * nj))(body)
                for j in range(nj):
                    for half in range(2):
                        dst = out_v.at[slot, t, pl.ds(half * DW + c * SC_MIX_CHUNK + j * L, L)]
                        if q == 0:
                            dst[...] = acc[2 * j + half]
                        else:
                            plsc.addupdate(dst, acc[2 * j + half])

        _sc_block_pipeline(tok_per_w // SC_TOKENS, K // G, loads, store, gather, compute)

    return scratch, run


def _sc_kernel(out_type, scratch_types, name):
    return functools.partial(
        pl.kernel, mesh=_sc_mesh(), out_type=out_type, scratch_types=scratch_types,
        compiler_params=pltpu.CompilerParams(needs_layout_passes=False), name=name)


def _expert_mix(w, idx, table):
    T, K = w.shape
    DW = table.shape[1]
    scratch, run = _mix_parts(T, K, DW)

    @_sc_kernel(jax.ShapeDtypeStruct((T, 2 * DW), jnp.float32), scratch, "expert_mix")
    def k(w_hbm, idx_hbm, tab_hbm, out_hbm, *mix_scratch):
        run(w_hbm, idx_hbm, tab_hbm, out_hbm, *mix_scratch)

    return k(w, idx, table)


def _chunk_sizes(total):
    ramp, size = [EDGE_CHUNK], EDGE_CHUNK
    while size < MAX_CHUNK:
        ramp.append(size)
        size *= 2
    middle = total - 2 * sum(ramp)
    assert middle >= 0 and middle % MAX_CHUNK == 0
    return ramp + [MAX_CHUNK] * (middle // MAX_CHUNK) + ramp[::-1]


def kernel(x, norm_mix, w_in, pool_w, pool_scale, sgu_ln_g, sgu_ln_b, sgu_w, sgu_b, out_norm_pool,
           out_norm_sgu, w_out, norm_ffn, peer_wq, peer_keys, peer_u, peer_v, norm_final):
    B, S, D = x.shape
    assert norm_mix.shape[0] == 1, "single-layer block"
    T = B * S
    mix_args = (norm_mix[0], w_in[0], pool_w[0], pool_scale[0], sgu_ln_g[0], sgu_ln_b[0],
                sgu_w[0], sgu_b[0], out_norm_pool[0], out_norm_sgu[0], w_out[0])
    x1_parts = [(b * S, _mixer(x, b, 1, *mix_args).reshape(S, D)) for b in range(B)]
    wq = peer_wq[0].astype(jnp.bfloat16)
    keys = peer_keys[0].astype(jnp.bfloat16)
    u_t = _transposed_bf16(peer_u[0])
    v_tab = _pack_table(peer_v[0])
    out = None
    tok0 = 0
    peers = [norm_ffn, v_tab]
    for tc in _chunk_sizes(T):
        part0, x1 = [p for p in x1_parts if p[0] <= tok0][-1]
        assert tok0 + tc <= part0 + x1.shape[0], "a token chunk must lie inside one mixer call"
        h2, idx, gate = _router(x1, norm_ffn[0], wq, keys, tok0 - part0, tc, peers[-2])
        w = _expert_weights(_expert_scores(h2, u_t, idx), gate)
        peer = _expert_mix(w, idx, v_tab)
        peers.append(peer)
        out = _final(x1, tok0 - part0, peer, norm_final, out, tok0, T)
        tok0 += tc
    return out.reshape(B, S, D)
```

```python
import functools
import math

import jax
import jax.numpy as jnp
from jax import lax
from jax.experimental import pallas as pl
from jax.experimental.pallas import tpu as pltpu
from jax.experimental.pallas import tpu_sc as plsc

POOL_WINDOWS = (2, 4, 8, 16)
N_POOL_GROUPS = len(POOL_WINDOWS)
SGU_HEADS = 4
SGU_CHUNK = 128
PEER_HEADS = 8
PEER_N_KEYS = 128
PEER_D_HALF = 128
PEER_TOPK = 16
NORM_EPS = 1e-6
EXPERTS_PER_TOKEN = PEER_HEADS * PEER_TOPK

V7X_LANES = 128
V7X_SUBLANES = 8
V7X_SC_CORES = 2
V7X_SC_SUBCORES = 16
V7X_SC_LANES = 16
V7X_SC_WORKERS = V7X_SC_CORES * V7X_SC_SUBCORES

HALO = max(POOL_WINDOWS)
MIX_TILE = 512
ROUTE_TILE = 1024
EW_TILE = 512
SCORE_TOK_TILE = 2048
SCORE_EXP_TILE = 1024
SC_GATHER = 64
SC_TOKENS = 8
SC_MIX_CHUNK = 128
SC_MIX_GROUP = 4
HI_HALF = 0xFFFF0000
V7X_VMEM_BYTES = 64 * 1024 * 1024
TC_VMEM_LIMIT = V7X_VMEM_BYTES * 3 // 4
EDGE_CHUNK = 512
MAX_CHUNK = 2048


def _rms(x, g):
    inv = lax.rsqrt(jnp.mean(x * x, axis=-1, keepdims=True) + NORM_EPS)
    return x * inv * g


def _pack_halves(bits):
    half = bits.shape[1] // 2
    return (bits[:, :half] >> 16) | (bits[:, half:] & jnp.uint32(HI_HALF))


def _pack_table_kernel(a_ref, o_ref):
    o_ref[...] = _pack_halves(pltpu.bitcast(a_ref[...].astype(jnp.bfloat16).astype(jnp.float32), jnp.uint32))


def _pack_table(a):
    n, c = a.shape
    rows = min(EW_TILE, n)
    return pl.pallas_call(
        _pack_table_kernel, grid=(n // rows,),
        in_specs=[pl.BlockSpec((rows, c), lambda i: (i, 0))],
        out_specs=pl.BlockSpec((rows, c // 2), lambda i: (i, 0)),
        out_shape=jax.ShapeDtypeStruct((n, c // 2), jnp.uint32),
        compiler_params=pltpu.CompilerParams(dimension_semantics=("parallel",)),
        name="pack_table",
    )(a)


def _transpose_cast_kernel(a_ref, o_ref):
    o_ref[...] = a_ref[...].T.astype(o_ref.dtype)


def _transposed_bf16(a):
    n, c = a.shape
    rows = min(SCORE_EXP_TILE, n)
    return pl.pallas_call(
        _transpose_cast_kernel, grid=(n // rows,),
        in_specs=[pl.BlockSpec((rows, c), lambda i: (i, 0))],
        out_specs=pl.BlockSpec((c, rows), lambda i: (0, i)),
        out_shape=jax.ShapeDtypeStruct((c, n), jnp.bfloat16),
        compiler_params=pltpu.CompilerParams(dimension_semantics=("parallel",)),
        name="transpose_table",
    )(a)


def _gelu(x):
    return 0.5 * x * (1.0 + lax.erf(x * math.sqrt(0.5)))


def _mixer_kernel(x_ref, xh_ref, nmix_ref, win_ref, poolw_ref, pscale_ref, lng_ref, lnb_ref,
                  sguw_ref, sgub_ref, onp_ref, ons_ref, wout_ref, o_ref, pext_ref, mix_ref):
    i = pl.program_id(1)
    ts = x_ref.shape[1]
    pool_w = pscale_ref.shape[1]
    gdim = pool_w // N_POOL_GROUPS
    sgu_w = lng_ref.shape[1]
    hdim = sgu_w // SGU_HEADS

    x = x_ref[0]
    h = _rms(x, nmix_ref[...]).astype(jnp.bfloat16)
    z = jnp.dot(h, win_ref[...], preferred_element_type=jnp.float32)
    p = z[:, :pool_w]

    hh = _rms(xh_ref[0], nmix_ref[...]).astype(jnp.bfloat16)
    ph = jnp.dot(hh, win_ref[:, :pool_w], preferred_element_type=jnp.float32)
    ph = jnp.where(i > 0, ph, 0.0)
    pext_ref[0:HALO, :] = ph
    pext_ref[HALO:HALO + ts, :] = p

    pos = i * ts + lax.broadcasted_iota(jnp.int32, (ts, 1), 0)
    ssq = jnp.zeros((ts, 1), jnp.float32)
    a_parts = []
    for g, win in enumerate(POOL_WINDOWS):
        cols = slice(g * gdim, (g + 1) * gdim)
        s = pext_ref[HALO:HALO + ts, cols]
        for j in range(1, win):
            s = s + pext_ref[HALO - j:HALO - j + ts, cols]
        cnt = jnp.minimum(pos + 1, win).astype(jnp.float32)
        d = (s / cnt - p[:, cols]).astype(jnp.bfloat16)
        a = jnp.dot(d, poolw_ref[g], preferred_element_type=jnp.float32) * pscale_ref[:, cols]
        ssq = ssq + jnp.sum(a * a, axis=-1, keepdims=True)
        a_parts.append(a)
    inv_a = lax.rsqrt(ssq / pool_w + NORM_EPS)
    for g in range(N_POOL_GROUPS):
        cols = slice(g * gdim, (g + 1) * gdim)
        mix_ref[:, cols] = (a_parts[g] * inv_a * onp_ref[:, cols]).astype(jnp.bfloat16)

    gz = _gelu(z[:, pool_w:])
    tril = (lax.broadcasted_iota(jnp.int32, (SGU_CHUNK, SGU_CHUNK), 0)
            >= lax.broadcasted_iota(jnp.int32, (SGU_CHUNK, SGU_CHUNK), 1))
    ssq = jnp.zeros((ts, 1), jnp.float32)
    b_parts = []
    for hd in range(SGU_HEADS):
        cols = slice(hd * hdim, (hd + 1) * hdim)
        u = gz[:, hd * hdim:(hd + 1) * hdim]
        v = gz[:, sgu_w + hd * hdim:sgu_w + (hd + 1) * hdim]
        mu = jnp.mean(v, axis=-1, keepdims=True)
        vc = v - mu
        var = jnp.mean(vc * vc, axis=-1, keepdims=True)
        vn = (vc * lax.rsqrt(var + NORM_EPS) * lng_ref[:, cols] + lnb_ref[:, cols]).astype(jnp.bfloat16)
        w = jnp.where(tril, sguw_ref[hd], jnp.zeros((), sguw_ref.dtype))
        mixed = [jnp.dot(w, vn[n * SGU_CHUNK:(n + 1) * SGU_CHUNK], preferred_element_type=jnp.float32)
                 + sgub_ref[hd] for n in range(ts // SGU_CHUNK)]
        b = u * jnp.concatenate(mixed, axis=0)
        ssq = ssq + jnp.sum(b * b, axis=-1, keepdims=True)
        b_parts.append(b)
    inv_b = lax.rsqrt(ssq / sgu_w + NORM_EPS)
    for hd in range(SGU_HEADS):
        cols = slice(hd * hdim, (hd + 1) * hdim)
        mix_ref[:, pool_w + hd * hdim:pool_w + (hd + 1) * hdim] = (
            b_parts[hd] * inv_b * ons_ref[:, cols]).astype(jnp.bfloat16)

    o_ref[0] = x + jnp.dot(mix_ref[...], wout_ref[...], preferred_element_type=jnp.float32)


def _mixer(x, b0, nb, norm_mix, w_in, pool_w, pool_scale, ln_g, ln_b, sgu_w, sgu_b, on_pool, on_sgu, w_out):
    _, S, D = x.shape
    ts = min(MIX_TILE, S)
    pool_width = pool_scale.size
    sgu_width = ln_g.size
    in_width = w_in.shape[1]
    gdim = pool_width // N_POOL_GROUPS
    halo_blocks = ts // HALO
    full = lambda shape: pl.BlockSpec(shape, lambda b, i: (0,) * len(shape))
    return pl.pallas_call(
        _mixer_kernel,
        grid=(nb, S // ts),
        in_specs=[
            pl.BlockSpec((1, ts, D), lambda b, i: (b0 + b, i, 0)),
            pl.BlockSpec((1, HALO, D), lambda b, i: (b0 + b, jnp.maximum(i * halo_blocks - 1, 0), 0)),
            full((1, D)),
            full((D, in_width)),
            full((N_POOL_GROUPS, gdim, gdim)),
            full((1, pool_width)),
            full((1, sgu_width)),
            full((1, sgu_width)),
            full((SGU_HEADS, SGU_CHUNK, SGU_CHUNK)),
            full((SGU_HEADS, SGU_CHUNK, SGU_CHUNK)),
            full((1, pool_width)),
            full((1, sgu_width)),
            full((pool_width + sgu_width, D)),
        ],
        out_specs=pl.BlockSpec((1, ts, D), lambda b, i: (b, i, 0)),
        out_shape=jax.ShapeDtypeStruct((nb, S, D), jnp.float32),
        scratch_shapes=[
            pltpu.VMEM((HALO + ts, pool_width), jnp.float32),
            pltpu.VMEM((ts, pool_width + sgu_width), jnp.bfloat16),
        ],
        compiler_params=pltpu.CompilerParams(
            dimension_semantics=("parallel", "arbitrary"), vmem_limit_bytes=TC_VMEM_LIMIT),
        name="mixer",
    )(x, x, norm_mix.reshape(1, D), w_in.astype(jnp.bfloat16), pool_w.astype(jnp.bfloat16),
      pool_scale.reshape(1, pool_width), ln_g.reshape(1, sgu_width), ln_b.reshape(1, sgu_width),
      sgu_w.astype(jnp.bfloat16),
      jnp.broadcast_to(sgu_b[:, :, None], (SGU_HEADS, SGU_CHUNK, SGU_CHUNK)),
      on_pool.reshape(1, pool_width), on_sgu.reshape(1, sgu_width), w_out.astype(jnp.bfloat16))


def _topk_rows(s, k):
    n = s.shape[0]
    iota = lax.broadcasted_iota(jnp.int32, s.shape, 0)
    vals, idxs = [], []
    for _ in range(k):
        m = jnp.max(s, axis=0, keepdims=True)
        ix = jnp.min(jnp.where(s == m, iota, n), axis=0, keepdims=True)
        vals.append(m)
        idxs.append(ix)
        s = jnp.where(iota == ix, -jnp.inf, s)
    return vals, idxs


def _pair_candidates(v1, i1, v2, i2):
    k = PEER_TOPK
    v2c, i2c = jnp.concatenate(v2, axis=0), jnp.concatenate(i2, axis=0)
    m = v2c.shape[1]
    vals, experts, flats = [], [], []
    a = 0
    while k // (a + 1) > 1:
        rows = -(-(k // (a + 1)) // V7X_SUBLANES) * V7X_SUBLANES
        vals.append(v1[a] + v2c[:rows])
        experts.append(i1[a] * PEER_N_KEYS + i2c[:rows])
        flats.append(a * k + lax.broadcasted_iota(jnp.int32, (rows, m), 0))
        a += 1
    vals.append(jnp.concatenate(v1[a:], axis=0) + v2[0])
    experts.append(jnp.concatenate(i1[a:], axis=0) * PEER_N_KEYS + i2[0])
    flats.append((a + lax.broadcasted_iota(jnp.int32, (k - a, m), 0)) * k)
    return jnp.concatenate(vals, axis=0), jnp.concatenate(experts, axis=0), jnp.concatenate(flats, axis=0)


def _router_kernel(x_ref, nffn_ref, wq_ref, keys_ref, after_ref, h2_ref, idx_ref, gate_ref,
                   q_ref, idxt_ref, gatet_ref):
    del after_ref
    h2 = _rms(x_ref[...], nffn_ref[...]).astype(jnp.bfloat16)
    h2_ref[...] = h2
    q_ref[...] = jnp.dot(h2, wq_ref[...], preferred_element_type=jnp.float32).astype(jnp.bfloat16)
    dq = 2 * PEER_D_HALF
    nt = (((1,), (1,)), ((), ()))

    def head(hd, carry):
        off = pl.multiple_of(hd * dq, dq)
        s1 = lax.dot_general(keys_ref[0], q_ref[:, pl.ds(off, PEER_D_HALF)], nt,
                             preferred_element_type=jnp.float32)
        s2 = lax.dot_general(keys_ref[1], q_ref[:, pl.ds(off + PEER_D_HALF, PEER_D_HALF)], nt,
                             preferred_element_type=jnp.float32)
        v1, i1 = _topk_rows(s1, PEER_TOPK)
        v2, i2 = _topk_rows(s2, PEER_TOPK)
        cand, expert, flat = _pair_candidates(v1, i1, v2, i2)
        cv, ce = [], []
        for _ in range(PEER_TOPK):
            m = jnp.max(cand, axis=0, keepdims=True)
            ix = jnp.min(jnp.where(cand == m, flat, PEER_TOPK * PEER_TOPK), axis=0, keepdims=True)
            hit = flat == ix
            cv.append(m)
            ce.append(jnp.max(jnp.where(hit, expert, -1), axis=0, keepdims=True))
            cand = jnp.where(hit, -jnp.inf, cand)
        cvc = jnp.concatenate(cv, axis=0)
        e = jnp.exp(cvc - cv[0])
        gate = e / jnp.sum(e, axis=0, keepdims=True)
        row = pl.multiple_of(hd * PEER_TOPK, PEER_TOPK)
        idxt_ref[pl.ds(row, PEER_TOPK), :] = jnp.concatenate(ce, axis=0)
        gatet_ref[pl.ds(row, PEER_TOPK), :] = gate
        return carry

    lax.fori_loop(0, PEER_HEADS, head, 0)
    idx_ref[...] = idxt_ref[...].T
    gate_ref[...] = gatet_ref[...].T


def _router(x1, norm_ffn, wq, keys, tok0, T, after):
    D = x1.shape[1]
    tr = min(ROUTE_TILE, T)
    qw = wq.shape[1]
    assert tok0 % tr == 0 and T % tr == 0
    first = tok0 // tr
    full = lambda shape: pl.BlockSpec(shape, lambda i: (0,) * len(shape))
    return pl.pallas_call(
        _router_kernel,
        grid=(T // tr,),
        in_specs=[
            pl.BlockSpec((tr, D), lambda i: (first + i, 0)),
            full((1, D)),
            full((D, qw)),
            full((2, PEER_N_KEYS, PEER_D_HALF)),
            pl.BlockSpec(memory_space=pl.ANY),
        ],
        out_specs=[
            pl.BlockSpec((tr, D), lambda i: (i, 0)),
            pl.BlockSpec((tr, EXPERTS_PER_TOKEN), lambda i: (i, 0)),
            pl.BlockSpec((tr, EXPERTS_PER_TOKEN), lambda i: (i, 0)),
        ],
        out_shape=[
            jax.ShapeDtypeStruct((T, D), jnp.bfloat16),
            jax.ShapeDtypeStruct((T, EXPERTS_PER_TOKEN), jnp.int32),
            jax.ShapeDtypeStruct((T, EXPERTS_PER_TOKEN), jnp.float32),
        ],
        scratch_shapes=[
            pltpu.VMEM((tr, qw), jnp.bfloat16),
            pltpu.VMEM((EXPERTS_PER_TOKEN, tr), jnp.int32),
            pltpu.VMEM((EXPERTS_PER_TOKEN, tr), jnp.float32),
        ],
        compiler_params=pltpu.CompilerParams(
            dimension_semantics=("parallel",), vmem_limit_bytes=TC_VMEM_LIMIT),
        name="router",
    )(x1, norm_ffn.reshape(1, D), wq, keys, after)


def _scores_kernel(h_ref, ut_ref, idx_ref, gate_ref, w_ref, pre_ref):
    j = pl.program_id(1)
    acc = jnp.dot(h_ref[...], ut_ref[...], preferred_element_type=jnp.float32)
    groups = acc.shape[1] // V7X_LANES
    lane_bits = int(math.log2(V7X_LANES))
    idx = idx_ref[...]
    group, lane = idx >> lane_bits, idx & (V7X_LANES - 1)
    picked = jnp.zeros(idx.shape, jnp.float32)
    for n in range(groups):
        vals = jnp.take_along_axis(acc[:, n * V7X_LANES:(n + 1) * V7X_LANES], lane, axis=1)
        picked = jnp.where(group == j * groups + n, vals, picked)

    @pl.when(j == 0)
    def _():
        pre_ref[...] = picked

    @pl.when(j > 0)
    def _():
        pre_ref[...] += picked

    @pl.when(j == pl.num_programs(1) - 1)
    def _():
        w = (gate_ref[...] * _gelu(pre_ref[...])).astype(jnp.bfloat16).astype(jnp.float32)
        hi = pltpu.bitcast(w, jnp.uint32) & jnp.uint32(HI_HALF)
        w_ref[...] = pltpu.bitcast(hi | (hi >> 16), jnp.int32)


def _expert_weights(h2, u_t, idx, gate):
    T, D = h2.shape
    E = u_t.shape[1]
    K = idx.shape[1]
    tm, tn = min(SCORE_TOK_TILE, T), min(SCORE_EXP_TILE, E)
    assert T % tm == 0 and E % tn == 0 and tn % V7X_LANES == 0 and K == V7X_LANES
    per_token = pl.BlockSpec((tm, K), lambda i, j: (i, 0))
    return pl.pallas_call(
        _scores_kernel,
        grid=(T // tm, E // tn),
        in_specs=[pl.BlockSpec((tm, D), lambda i, j: (i, 0)), pl.BlockSpec((D, tn), lambda i, j: (0, j)),
                  per_token, per_token],
        out_specs=per_token,
        out_shape=jax.ShapeDtypeStruct((T, K), jnp.int32),
        scratch_shapes=[pltpu.VMEM((tm, K), jnp.float32)],
        compiler_params=pltpu.CompilerParams(
            dimension_semantics=("parallel", "arbitrary"), vmem_limit_bytes=TC_VMEM_LIMIT),
        name="expert_weights",
    )(h2, u_t, idx, gate)


def _final_kernel(x_ref, y_ref, g_ref, *rest):
    o_ref = rest[-1]
    o_ref[...] = _rms(x_ref[...] + y_ref[...], g_ref[...])


def _final(x1, x_tok0, peer, norm_final, out_prev, out_tok0, total):
    Tc, D = peer.shape
    te = min(EW_TILE, Tc)
    assert x_tok0 % te == 0 and out_tok0 % te == 0 and Tc % te == 0
    nblk = Tc // te
    spec = pl.BlockSpec((te, D), lambda i: (out_tok0 // te + i, 0))
    in_specs = [pl.BlockSpec((te, D), lambda i: (x_tok0 // te + i, 0)),
                pl.BlockSpec((te, D), lambda i: (i, 0)), pl.BlockSpec((1, D), lambda i: (0, 0))]
    args = [x1, peer, norm_final.reshape(1, D)]
    aliases = {}
    if out_prev is not None:
        in_specs.append(pl.BlockSpec(memory_space=pl.ANY))
        args.append(out_prev)
        aliases = {3: 0}
    return pl.pallas_call(
        _final_kernel, grid=(nblk,), in_specs=in_specs, out_specs=spec,
        out_shape=jax.ShapeDtypeStruct((total, D), jnp.float32),
        input_output_aliases=aliases,
        compiler_params=pltpu.CompilerParams(dimension_semantics=("parallel",)),
        name="final_norm",
    )(*args)


def _tree_sum(vals):
    while len(vals) > 1:
        nxt = [vals[i] + vals[i + 1] for i in range(0, len(vals) - 1, 2)]
        if len(vals) % 2:
            nxt.append(vals[-1])
        vals = nxt
    return vals[0]


def _sc_block_pipeline(nblk, items_per_token, loads, store, gather, compute):
    assert items_per_token % 2 == 0 and nblk >= 1

    for c in loads(0, 0):
        c.start()
    for c in loads(0, 0):
        c.wait()
    if nblk > 1:
        for c in loads(1, 1):
            c.start()
    gather(0, 0, 0, 0).start()

    @pl.loop(0, nblk)
    def _(b):
        slot = b % 2

        @pl.when(b >= 2)
        def _():
            store(b - 2, slot).wait()

        @pl.loop(0, SC_TOKENS)
        def _(t):
            for q in range(items_per_token):
                buf = q % 2
                if q + 1 < items_per_token:
                    gather(slot, t, q + 1, 1 - buf).start()
                else:
                    @pl.when(t + 1 < SC_TOKENS)
                    def _():
                        gather(slot, t + 1, 0, 1 - buf).start()

                    @pl.when(jnp.logical_and(t + 1 == SC_TOKENS, b + 1 < nblk))
                    def _():
                        for c in loads(b + 1, 1 - slot):
                            c.wait()
                        gather(1 - slot, 0, 0, 1 - buf).start()

                gather(slot, t, q, buf).wait()
                compute(slot, t, q, buf)

        store(b, slot).start()

        @pl.when(b + 2 < nblk)
        def _():
            for c in loads(b + 2, slot):
                c.start()

    if nblk >= 2:
        store(nblk - 2, nblk % 2).wait()
    store(nblk - 1, (nblk - 1) % 2).wait()


def _sc_mesh():
    return plsc.VectorSubcoreMesh(core_axis_name="c", subcore_axis_name="s")


def _sc_worker_id():
    return lax.axis_index("s") * V7X_SC_CORES + lax.axis_index("c")


def _sc_bf16(words):
    return plsc.bitcast(words, jnp.bfloat16)


def _sc_halves_f32(pairs):
    words = plsc.bitcast(pairs, jnp.uint32)
    return (plsc.bitcast(words << 16, jnp.float32),
            plsc.bitcast(words & jnp.uint32(HI_HALF), jnp.float32))


def _mix_parts(T, K, DW):
    D = 2 * DW
    L, G = V7X_SC_LANES, SC_GATHER
    nj = SC_MIX_CHUNK // L
    tok_per_w = T // V7X_SC_WORKERS
    assert T % (V7X_SC_WORKERS * SC_TOKENS) == 0 and K % (2 * G) == 0
    assert DW % SC_MIX_CHUNK == 0 and G % SC_MIX_GROUP == 0
    scratch = [
        pltpu.VMEM((2, SC_TOKENS, K), jnp.int32),
        pltpu.VMEM((2, SC_TOKENS, K), jnp.int32),
        pltpu.VMEM((2, G, DW), jnp.uint32),
        pltpu.VMEM((2, SC_TOKENS, D), jnp.float32),
        pltpu.SemaphoreType.DMA((2,)),
        pltpu.SemaphoreType.DMA((2,)),
        pltpu.SemaphoreType.DMA((2,)),
        pltpu.SemaphoreType.DMA((2,)),
    ]

    def run(w_hbm, idx_hbm, tab_hbm, out_hbm, idx_v, w_v, rows_v, out_v, idx_sems, w_sems, out_sems, row_sems):
        base = _sc_worker_id() * tok_per_w

        def loads(b, slot):
            toks = pl.ds(base + b * SC_TOKENS, SC_TOKENS)
            return [pltpu.make_async_copy(idx_hbm.at[toks], idx_v.at[slot], idx_sems.at[slot]),
                    pltpu.make_async_copy(w_hbm.at[toks], w_v.at[slot], w_sems.at[slot])]

        def store(b, slot):
            toks = pl.ds(base + b * SC_TOKENS, SC_TOKENS)
            return pltpu.make_async_copy(out_v.at[slot], out_hbm.at[toks], out_sems.at[slot])

        def gather(slot, t, q, buf):
            return pltpu.make_async_copy(
                tab_hbm.at[idx_v.at[slot, t, pl.ds(q * G, G)]], rows_v.at[buf], row_sems.at[buf])

        def compute(slot, t, q, buf):
            ssplat = jnp.full((L,), slot, jnp.int32)
            tsplat = jnp.full((L,), t, jnp.int32)
            for c in range(DW // SC_MIX_CHUNK):
                def body(kg, acc):
                    kk = kg * SC_MIX_GROUP
                    wks = [_sc_bf16(plsc.load_gather(
                        w_v, [ssplat, tsplat, jnp.full((L,), q * G + i, jnp.int32) + kk]))
                        for i in range(SC_MIX_GROUP)]
                    out = []
                    for j in range(nj):
                        prods = [wks[i] * _sc_bf16(rows_v[buf, kk + i, pl.ds(c * SC_MIX_CHUNK + j * L, L)])
                                 for i in range(SC_MIX_GROUP)]
                        lo, hi = _sc_halves_f32(_tree_sum(prods))
                        out += [acc[2 * j] + lo, acc[2 * j + 1] + hi]
                    return tuple(out)

                zero = jnp.zeros((L,), jnp.float32)
                acc = plsc.parallel_loop(0, G // SC_MIX_GROUP, carry=(zero,) * (2 * nj))(body)
                for j in range(nj):
                    for half in range(2):
                        dst = out_v.at[slot, t, pl.ds(half * DW + c * SC_MIX_CHUNK + j * L, L)]
                        if q == 0:
                            dst[...] = acc[2 * j + half]
                        else:
                            plsc.addupdate(dst, acc[2 * j + half])

        _sc_block_pipeline(tok_per_w // SC_TOKENS, K // G, loads, store, gather, compute)

    return scratch, run


def _sc_kernel(out_type, scratch_types, name):
    return functools.partial(
        pl.kernel, mesh=_sc_mesh(), out_type=out_type, scratch_types=scratch_types,
        compiler_params=pltpu.CompilerParams(needs_layout_passes=False), name=name)


def _expert_mix(w, idx, table):
    T, K = w.shape
    DW = table.shape[1]
    scratch, run = _mix_parts(T, K, DW)

    @_sc_kernel(jax.ShapeDtypeStruct((T, 2 * DW), jnp.float32), scratch, "expert_mix")
    def k(w_hbm, idx_hbm, tab_hbm, out_hbm, *mix_scratch):
        run(w_hbm, idx_hbm, tab_hbm, out_hbm, *mix_scratch)

    return k(w, idx, table)


def _chunk_sizes(total):
    ramp, size = [EDGE_CHUNK], EDGE_CHUNK
    while size < MAX_CHUNK:
        ramp.append(size)
        size *= 2
    middle = total - 2 * sum(ramp)
    assert middle >= 0 and middle % MAX_CHUNK == 0
    return ramp + [MAX_CHUNK] * (middle // MAX_CHUNK) + ramp[::-1]


def kernel(x, norm_mix, w_in, pool_w, pool_scale, sgu_ln_g, sgu_ln_b, sgu_w, sgu_b, out_norm_pool,
           out_norm_sgu, w_out, norm_ffn, peer_wq, peer_keys, peer_u, peer_v, norm_final):
    B, S, D = x.shape
    assert norm_mix.shape[0] == 1, "single-layer block"
    T = B * S
    mix_args = (norm_mix[0], w_in[0], pool_w[0], pool_scale[0], sgu_ln_g[0], sgu_ln_b[0],
                sgu_w[0], sgu_b[0], out_norm_pool[0], out_norm_sgu[0], w_out[0])
    x1_parts = [(b * S, _mixer(x, b, 1, *mix_args).reshape(S, D)) for b in range(B)]
    wq = peer_wq[0].astype(jnp.bfloat16)
    keys = peer_keys[0].astype(jnp.bfloat16)
    u_t = _transposed_bf16(peer_u[0])
    v_tab = _pack_table(peer_v[0])
    out = None
    tok0 = 0
    peers = [norm_ffn, v_tab]
    for tc in _chunk_sizes(T):
        part0, x1 = [p for p in x1_parts if p[0] <= tok0][-1]
        assert tok0 + tc <= part0 + x1.shape[0], "a token chunk must lie inside one mixer call"
        h2, idx, gate = _router(x1, norm_ffn[0], wq, keys, tok0 - part0, tc, peers[-2])
        w = _expert_weights(h2, u_t, idx, gate)
        peer = _expert_mix(w, idx, v_tab)
        peers.append(peer)
        out = _final(x1, tok0 - part0, peer, norm_final, out, tok0, T)
        tok0 += tc
    return out.reshape(B, S, D)
```

```python
import functools
import math

import jax
import jax.numpy as jnp
from jax import lax
from jax.experimental import pallas as pl
from jax.experimental.pallas import tpu as pltpu
from jax.experimental.pallas import tpu_sc as plsc

POOL_WINDOWS = (2, 4, 8, 16)
N_POOL_GROUPS = len(POOL_WINDOWS)
SGU_HEADS = 4
SGU_CHUNK = 128
PEER_HEADS = 8
PEER_N_KEYS = 128
PEER_D_HALF = 128
PEER_TOPK = 16
NORM_EPS = 1e-6
EXPERTS_PER_TOKEN = PEER_HEADS * PEER_TOPK

V7X_LANES = 128
V7X_SUBLANES = 8
V7X_SC_CORES = 2
V7X_SC_SUBCORES = 16
V7X_SC_LANES = 16
V7X_SC_WORKERS = V7X_SC_CORES * V7X_SC_SUBCORES

HALO = max(POOL_WINDOWS)
MIX_TILE = 512
ROUTE_TILE = 1024
EW_TILE = 512
SCORE_TOK_TILE = 2048
SCORE_EXP_TILE = 2048
SC_GATHER = 64
SC_TOKENS = 8
SC_MIX_CHUNK = 128
SC_MIX_GROUP = 4
HI_HALF = 0xFFFF0000
V7X_VMEM_BYTES = 64 * 1024 * 1024
TC_VMEM_LIMIT = V7X_VMEM_BYTES * 3 // 4
EDGE_CHUNK = 512
MAX_CHUNK = 2048


def _rms(x, g):
    inv = lax.rsqrt(jnp.mean(x * x, axis=-1, keepdims=True) + NORM_EPS)
    return x * inv * g


def _pack_halves(bits):
    half = bits.shape[1] // 2
    return (bits[:, :half] >> 16) | (bits[:, half:] & jnp.uint32(HI_HALF))


def _pack_table_kernel(a_ref, o_ref):
    o_ref[...] = _pack_halves(pltpu.bitcast(a_ref[...].astype(jnp.bfloat16).astype(jnp.float32), jnp.uint32))


def _pack_table(a):
    n, c = a.shape
    rows = min(EW_TILE, n)
    return pl.pallas_call(
        _pack_table_kernel, grid=(n // rows,),
        in_specs=[pl.BlockSpec((rows, c), lambda i: (i, 0))],
        out_specs=pl.BlockSpec((rows, c // 2), lambda i: (i, 0)),
        out_shape=jax.ShapeDtypeStruct((n, c // 2), jnp.uint32),
        compiler_params=pltpu.CompilerParams(dimension_semantics=("parallel",)),
        name="pack_table",
    )(a)


def _transpose_cast_kernel(a_ref, o_ref):
    o_ref[...] = a_ref[...].T.astype(o_ref.dtype)


def _transposed_bf16(a):
    n, c = a.shape
    rows = min(SCORE_EXP_TILE, n)
    return pl.pallas_call(
        _transpose_cast_kernel, grid=(n // rows,),
        in_specs=[pl.BlockSpec((rows, c), lambda i: (i, 0))],
        out_specs=pl.BlockSpec((c, rows), lambda i: (0, i)),
        out_shape=jax.ShapeDtypeStruct((c, n), jnp.bfloat16),
        compiler_params=pltpu.CompilerParams(dimension_semantics=("parallel",)),
        name="transpose_table",
    )(a)


def _gelu(x):
    return 0.5 * x * (1.0 + lax.erf(x * math.sqrt(0.5)))


def _mixer_kernel(x_ref, xh_ref, nmix_ref, win_ref, poolw_ref, pscale_ref, lng_ref, lnb_ref,
                  sguw_ref, sgub_ref, onp_ref, ons_ref, wout_ref, o_ref, pext_ref, mix_ref):
    i = pl.program_id(1)
    ts = x_ref.shape[1]
    pool_w = pscale_ref.shape[1]
    gdim = pool_w // N_POOL_GROUPS
    sgu_w = lng_ref.shape[1]
    hdim = sgu_w // SGU_HEADS

    x = x_ref[0]
    h = _rms(x, nmix_ref[...]).astype(jnp.bfloat16)
    z = jnp.dot(h, win_ref[...], preferred_element_type=jnp.float32)
    p = z[:, :pool_w]

    hh = _rms(xh_ref[0], nmix_ref[...]).astype(jnp.bfloat16)
    ph = jnp.dot(hh, win_ref[:, :pool_w], preferred_element_type=jnp.float32)
    ph = jnp.where(i > 0, ph, 0.0)
    pext_ref[0:HALO, :] = ph
    pext_ref[HALO:HALO + ts, :] = p

    pos = i * ts + lax.broadcasted_iota(jnp.int32, (ts, 1), 0)
    ssq = jnp.zeros((ts, 1), jnp.float32)
    a_parts = []
    for g, win in enumerate(POOL_WINDOWS):
        cols = slice(g * gdim, (g + 1) * gdim)
        s = pext_ref[HALO:HALO + ts, cols]
        for j in range(1, win):
            s = s + pext_ref[HALO - j:HALO - j + ts, cols]
        cnt = jnp.minimum(pos + 1, win).astype(jnp.float32)
        d = (s / cnt - p[:, cols]).astype(jnp.bfloat16)
        a = jnp.dot(d, poolw_ref[g], preferred_element_type=jnp.float32) * pscale_ref[:, cols]
        ssq = ssq + jnp.sum(a * a, axis=-1, keepdims=True)
        a_parts.append(a)
    inv_a = lax.rsqrt(ssq / pool_w + NORM_EPS)
    for g in range(N_POOL_GROUPS):
        cols = slice(g * gdim, (g + 1) * gdim)
        mix_ref[:, cols] = (a_parts[g] * inv_a * onp_ref[:, cols]).astype(jnp.bfloat16)

    gz = _gelu(z[:, pool_w:])
    tril = (lax.broadcasted_iota(jnp.int32, (SGU_CHUNK, SGU_CHUNK), 0)
            >= lax.broadcasted_iota(jnp.int32, (SGU_CHUNK, SGU_CHUNK), 1))
    ssq = jnp.zeros((ts, 1), jnp.float32)
    b_parts = []
    for hd in range(SGU_HEADS):
        cols = slice(hd * hdim, (hd + 1) * hdim)
        u = gz[:, hd * hdim:(hd + 1) * hdim]
        v = gz[:, sgu_w + hd * hdim:sgu_w + (hd + 1) * hdim]
        mu = jnp.mean(v, axis=-1, keepdims=True)
        vc = v - mu
        var = jnp.mean(vc * vc, axis=-1, keepdims=True)
        vn = (vc * lax.rsqrt(var + NORM_EPS) * lng_ref[:, cols] + lnb_ref[:, cols]).astype(jnp.bfloat16)
        w = jnp.where(tril, sguw_ref[hd], jnp.zeros((), sguw_ref.dtype))
        mixed = [jnp.dot(w, vn[n * SGU_CHUNK:(n + 1) * SGU_CHUNK], preferred_element_type=jnp.float32)
                 + sgub_ref[hd] for n in range(ts // SGU_CHUNK)]
        b = u * jnp.concatenate(mixed, axis=0)
        ssq = ssq + jnp.sum(b * b, axis=-1, keepdims=True)
        b_parts.append(b)
    inv_b = lax.rsqrt(ssq / sgu_w + NORM_EPS)
    for hd in range(SGU_HEADS):
        cols = slice(hd * hdim, (hd + 1) * hdim)
        mix_ref[:, pool_w + hd * hdim:pool_w + (hd + 1) * hdim] = (
            b_parts[hd] * inv_b * ons_ref[:, cols]).astype(jnp.bfloat16)

    o_ref[0] = x + jnp.dot(mix_ref[...], wout_ref[...], preferred_element_type=jnp.float32)


def _mixer(x, b0, nb, norm_mix, w_in, pool_w, pool_scale, ln_g, ln_b, sgu_w, sgu_b, on_pool, on_sgu, w_out):
    _, S, D = x.shape
    ts = min(MIX_TILE, S)
    pool_width = pool_scale.size
    sgu_width = ln_g.size
    in_width = w_in.shape[1]
    gdim = pool_width // N_POOL_GROUPS
    halo_blocks = ts // HALO
    full = lambda shape: pl.BlockSpec(shape, lambda b, i: (0,) * len(shape))
    return pl.pallas_call(
        _mixer_kernel,
        grid=(nb, S // ts),
        in_specs=[
            pl.BlockSpec((1, ts, D), lambda b, i: (b0 + b, i, 0)),
            pl.BlockSpec((1, HALO, D), lambda b, i: (b0 + b, jnp.maximum(i * halo_blocks - 1, 0), 0)),
            full((1, D)),
            full((D, in_width)),
            full((N_POOL_GROUPS, gdim, gdim)),
            full((1, pool_width)),
            full((1, sgu_width)),
            full((1, sgu_width)),
            full((SGU_HEADS, SGU_CHUNK, SGU_CHUNK)),
            full((SGU_HEADS, SGU_CHUNK, SGU_CHUNK)),
            full((1, pool_width)),
            full((1, sgu_width)),
            full((pool_width + sgu_width, D)),
        ],
        out_specs=pl.BlockSpec((1, ts, D), lambda b, i: (b, i, 0)),
        out_shape=jax.ShapeDtypeStruct((nb, S, D), jnp.float32),
        scratch_shapes=[
            pltpu.VMEM((HALO + ts, pool_width), jnp.float32),
            pltpu.VMEM((ts, pool_width + sgu_width), jnp.bfloat16),
        ],
        compiler_params=pltpu.CompilerParams(
            dimension_semantics=("parallel", "arbitrary"), vmem_limit_bytes=TC_VMEM_LIMIT),
        name="mixer",
    )(x, x, norm_mix.reshape(1, D), w_in.astype(jnp.bfloat16), pool_w.astype(jnp.bfloat16),
      pool_scale.reshape(1, pool_width), ln_g.reshape(1, sgu_width), ln_b.reshape(1, sgu_width),
      sgu_w.astype(jnp.bfloat16),
      jnp.broadcast_to(sgu_b[:, :, None], (SGU_HEADS, SGU_CHUNK, SGU_CHUNK)),
      on_pool.reshape(1, pool_width), on_sgu.reshape(1, sgu_width), w_out.astype(jnp.bfloat16))


def _topk_rows(s, k):
    n = s.shape[0]
    iota = lax.broadcasted_iota(jnp.int32, s.shape, 0)
    vals, idxs = [], []
    for _ in range(k):
        m = jnp.max(s, axis=0, keepdims=True)
        ix = jnp.min(jnp.where(s == m, iota, n), axis=0, keepdims=True)
        vals.append(m)
        idxs.append(ix)
        s = jnp.where(iota == ix, -jnp.inf, s)
    return vals, idxs


def _pair_candidates(v1, i1, v2, i2):
    k = PEER_TOPK
    v2c, i2c = jnp.concatenate(v2, axis=0), jnp.concatenate(i2, axis=0)
    m = v2c.shape[1]
    vals, experts, flats = [], [], []
    a = 0
    while k // (a + 1) > 1:
        rows = -(-(k // (a + 1)) // V7X_SUBLANES) * V7X_SUBLANES
        vals.append(v1[a] + v2c[:rows])
        experts.append(i1[a] * PEER_N_KEYS + i2c[:rows])
        flats.append(a * k + lax.broadcasted_iota(jnp.int32, (rows, m), 0))
        a += 1
    vals.append(jnp.concatenate(v1[a:], axis=0) + v2[0])
    experts.append(jnp.concatenate(i1[a:], axis=0) * PEER_N_KEYS + i2[0])
    flats.append((a + lax.broadcasted_iota(jnp.int32, (k - a, m), 0)) * k)
    return jnp.concatenate(vals, axis=0), jnp.concatenate(experts, axis=0), jnp.concatenate(flats, axis=0)


def _router_kernel(x_ref, nffn_ref, wq_ref, keys_ref, after_ref, h2_ref, idx_ref, gate_ref,
                   q_ref, idxt_ref, gatet_ref):
    del after_ref
    h2 = _rms(x_ref[...], nffn_ref[...]).astype(jnp.bfloat16)
    h2_ref[...] = h2
    q_ref[...] = jnp.dot(h2, wq_ref[...], preferred_element_type=jnp.float32).astype(jnp.bfloat16)
    dq = 2 * PEER_D_HALF
    nt = (((1,), (1,)), ((), ()))

    def head(hd, carry):
        off = pl.multiple_of(hd * dq, dq)
        s1 = lax.dot_general(keys_ref[0], q_ref[:, pl.ds(off, PEER_D_HALF)], nt,
                             preferred_element_type=jnp.float32)
        s2 = lax.dot_general(keys_ref[1], q_ref[:, pl.ds(off + PEER_D_HALF, PEER_D_HALF)], nt,
                             preferred_element_type=jnp.float32)
        v1, i1 = _topk_rows(s1, PEER_TOPK)
        v2, i2 = _topk_rows(s2, PEER_TOPK)
        cand, expert, flat = _pair_candidates(v1, i1, v2, i2)
        cv, ce = [], []
        for _ in range(PEER_TOPK):
            m = jnp.max(cand, axis=0, keepdims=True)
            ix = jnp.min(jnp.where(cand == m, flat, PEER_TOPK * PEER_TOPK), axis=0, keepdims=True)
            hit = flat == ix
            cv.append(m)
            ce.append(jnp.max(jnp.where(hit, expert, -1), axis=0, keepdims=True))
            cand = jnp.where(hit, -jnp.inf, cand)
        cvc = jnp.concatenate(cv, axis=0)
        e = jnp.exp(cvc - cv[0])
        gate = e / jnp.sum(e, axis=0, keepdims=True)
        row = pl.multiple_of(hd * PEER_TOPK, PEER_TOPK)
        idxt_ref[pl.ds(row, PEER_TOPK), :] = jnp.concatenate(ce, axis=0)
        gatet_ref[pl.ds(row, PEER_TOPK), :] = gate
        return carry

    lax.fori_loop(0, PEER_HEADS, head, 0)
    idx_ref[...] = idxt_ref[...].T
    gate_ref[...] = gatet_ref[...].T


def _router(x1, norm_ffn, wq, keys, tok0, T, after):
    D = x1.shape[1]
    tr = min(ROUTE_TILE, T)
    qw = wq.shape[1]
    assert tok0 % tr == 0 and T % tr == 0
    first = tok0 // tr
    full = lambda shape: pl.BlockSpec(shape, lambda i: (0,) * len(shape))
    return pl.pallas_call(
        _router_kernel,
        grid=(T // tr,),
        in_specs=[
            pl.BlockSpec((tr, D), lambda i: (first + i, 0)),
            full((1, D)),
            full((D, qw)),
            full((2, PEER_N_KEYS, PEER_D_HALF)),
            pl.BlockSpec(memory_space=pl.ANY),
        ],
        out_specs=[
            pl.BlockSpec((tr, D), lambda i: (i, 0)),
            pl.BlockSpec((tr, EXPERTS_PER_TOKEN), lambda i: (i, 0)),
            pl.BlockSpec((tr, EXPERTS_PER_TOKEN), lambda i: (i, 0)),
        ],
        out_shape=[
            jax.ShapeDtypeStruct((T, D), jnp.bfloat16),
            jax.ShapeDtypeStruct((T, EXPERTS_PER_TOKEN), jnp.int32),
            jax.ShapeDtypeStruct((T, EXPERTS_PER_TOKEN), jnp.float32),
        ],
        scratch_shapes=[
            pltpu.VMEM((tr, qw), jnp.bfloat16),
            pltpu.VMEM((EXPERTS_PER_TOKEN, tr), jnp.int32),
            pltpu.VMEM((EXPERTS_PER_TOKEN, tr), jnp.float32),
        ],
        compiler_params=pltpu.CompilerParams(
            dimension_semantics=("parallel",), vmem_limit_bytes=TC_VMEM_LIMIT),
        name="router",
    )(x1, norm_ffn.reshape(1, D), wq, keys, after)


def _scores_kernel(h_ref, ut_ref, idx_ref, gate_ref, w_ref, pre_ref):
    j = pl.program_id(1)
    acc = jnp.dot(h_ref[...], ut_ref[...], preferred_element_type=jnp.float32)
    groups = acc.shape[1] // V7X_LANES
    lane_bits = int(math.log2(V7X_LANES))
    idx = idx_ref[...]
    group, lane = idx >> lane_bits, idx & (V7X_LANES - 1)
    picked = jnp.zeros(idx.shape, jnp.float32)
    for n in range(groups):
        vals = jnp.take_along_axis(acc[:, n * V7X_LANES:(n + 1) * V7X_LANES], lane, axis=1)
        picked = jnp.where(group == j * groups + n, vals, picked)

    @pl.when(j == 0)
    def _():
        pre_ref[...] = picked

    @pl.when(j > 0)
    def _():
        pre_ref[...] += picked

    @pl.when(j == pl.num_programs(1) - 1)
    def _():
        w = (gate_ref[...] * _gelu(pre_ref[...])).astype(jnp.bfloat16).astype(jnp.float32)
        hi = pltpu.bitcast(w, jnp.uint32) & jnp.uint32(HI_HALF)
        w_ref[...] = pltpu.bitcast(hi | (hi >> 16), jnp.int32)


def _expert_weights(h2, u_t, idx, gate):
    T, D = h2.shape
    E = u_t.shape[1]
    K = idx.shape[1]
    tm, tn = min(SCORE_TOK_TILE, T), min(SCORE_EXP_TILE, E)
    assert T % tm == 0 and E % tn == 0 and tn % V7X_LANES == 0 and K == V7X_LANES
    per_token = pl.BlockSpec((tm, K), lambda i, j: (i, 0))
    return pl.pallas_call(
        _scores_kernel,
        grid=(T // tm, E // tn),
        in_specs=[pl.BlockSpec((tm, D), lambda i, j: (i, 0)), pl.BlockSpec((D, tn), lambda i, j: (0, j)),
                  per_token, per_token],
        out_specs=per_token,
        out_shape=jax.ShapeDtypeStruct((T, K), jnp.int32),
        scratch_shapes=[pltpu.VMEM((tm, K), jnp.float32)],
        compiler_params=pltpu.CompilerParams(
            dimension_semantics=("parallel", "arbitrary"), vmem_limit_bytes=TC_VMEM_LIMIT),
        name="expert_weights",
    )(h2, u_t, idx, gate)


def _final_kernel(x_ref, y_ref, g_ref, *rest):
    o_ref = rest[-1]
    o_ref[...] = _rms(x_ref[...] + y_ref[...], g_ref[...])


def _final(x1, x_tok0, peer, norm_final, out_prev, out_tok0, total):
    Tc, D = peer.shape
    te = min(EW_TILE, Tc)
    assert x_tok0 % te == 0 and out_tok0 % te == 0 and Tc % te == 0
    nblk = Tc // te
    spec = pl.BlockSpec((te, D), lambda i: (out_tok0 // te + i, 0))
    in_specs = [pl.BlockSpec((te, D), lambda i: (x_tok0 // te + i, 0)),
                pl.BlockSpec((te, D), lambda i: (i, 0)), pl.BlockSpec((1, D), lambda i: (0, 0))]
    args = [x1, peer, norm_final.reshape(1, D)]
    aliases = {}
    if out_prev is not None:
        in_specs.append(pl.BlockSpec(memory_space=pl.ANY))
        args.append(out_prev)
        aliases = {3: 0}
    return pl.pallas_call(
        _final_kernel, grid=(nblk,), in_specs=in_specs, out_specs=spec,
        out_shape=jax.ShapeDtypeStruct((total, D), jnp.float32),
        input_output_aliases=aliases,
        compiler_params=pltpu.CompilerParams(dimension_semantics=("parallel",)),
        name="final_norm",
    )(*args)


def _tree_sum(vals):
    while len(vals) > 1:
        nxt = [vals[i] + vals[i + 1] for i in range(0, len(vals) - 1, 2)]
        if len(vals) % 2:
            nxt.append(vals[-1])
        vals = nxt
    return vals[0]


def _sc_block_pipeline(nblk, items_per_token, loads, store, gather, compute):
    assert items_per_token % 2 == 0 and nblk >= 1

    for c in loads(0, 0):
        c.start()
    for c in loads(0, 0):
        c.wait()
    if nblk > 1:
        for c in loads(1, 1):
            c.start()
    gather(0, 0, 0, 0).start()

    @pl.loop(0, nblk)
    def _(b):
        slot = b % 2

        @pl.when(b >= 2)
        def _():
            store(b - 2, slot).wait()

        @pl.loop(0, SC_TOKENS)
        def _(t):
            for q in range(items_per_token):
                buf = q % 2
                if q + 1 < items_per_token:
                    gather(slot, t, q + 1, 1 - buf).start()
                else:
                    @pl.when(t + 1 < SC_TOKENS)
                    def _():
                        gather(slot, t + 1, 0, 1 - buf).start()

                    @pl.when(jnp.logical_and(t + 1 == SC_TOKENS, b + 1 < nblk))
                    def _():
                        for c in loads(b + 1, 1 - slot):
                            c.wait()
                        gather(1 - slot, 0, 0, 1 - buf).start()

                gather(slot, t, q, buf).wait()
                compute(slot, t, q, buf)

        store(b, slot).start()

        @pl.when(b + 2 < nblk)
        def _():
            for c in loads(b + 2, slot):
                c.start()

    if nblk >= 2:
        store(nblk - 2, nblk % 2).wait()
    store(nblk - 1, (nblk - 1) % 2).wait()


def _sc_mesh():
    return plsc.VectorSubcoreMesh(core_axis_name="c", subcore_axis_name="s")


def _sc_worker_id():
    return lax.axis_index("s") * V7X_SC_CORES + lax.axis_index("c")


def _sc_bf16(words):
    return plsc.bitcast(words, jnp.bfloat16)


def _sc_halves_f32(pairs):
    words = plsc.bitcast(pairs, jnp.uint32)
    return (plsc.bitcast(words << 16, jnp.float32),
            plsc.bitcast(words & jnp.uint32(HI_HALF), jnp.float32))


def _mix_parts(T, K, DW):
    D = 2 * DW
    L, G = V7X_SC_LANES, SC_GATHER
    nj = SC_MIX_CHUNK // L
    tok_per_w = T // V7X_SC_WORKERS
    assert T % (V7X_SC_WORKERS * SC_TOKENS) == 0 and K % (2 * G) == 0
    assert DW % SC_MIX_CHUNK == 0 and G % SC_MIX_GROUP == 0
    scratch = [
        pltpu.VMEM((2, SC_TOKENS, K), jnp.int32),
        pltpu.VMEM((2, SC_TOKENS, K), jnp.int32),
        pltpu.VMEM((2, G, DW), jnp.uint32),
        pltpu.VMEM((2, SC_TOKENS, D), jnp.float32),
        pltpu.SemaphoreType.DMA((2,)),
        pltpu.SemaphoreType.DMA((2,)),
        pltpu.SemaphoreType.DMA((2,)),
        pltpu.SemaphoreType.DMA((2,)),
    ]

    def run(w_hbm, idx_hbm, tab_hbm, out_hbm, idx_v, w_v, rows_v, out_v, idx_sems, w_sems, out_sems, row_sems):
        base = _sc_worker_id() * tok_per_w

        def loads(b, slot):
            toks = pl.ds(base + b * SC_TOKENS, SC_TOKENS)
            return [pltpu.make_async_copy(idx_hbm.at[toks], idx_v.at[slot], idx_sems.at[slot]),
                    pltpu.make_async_copy(w_hbm.at[toks], w_v.at[slot], w_sems.at[slot])]

        def store(b, slot):
            toks = pl.ds(base + b * SC_TOKENS, SC_TOKENS)
            return pltpu.make_async_copy(out_v.at[slot], out_hbm.at[toks], out_sems.at[slot])

        def gather(slot, t, q, buf):
            return pltpu.make_async_copy(
                tab_hbm.at[idx_v.at[slot, t, pl.ds(q * G, G)]], rows_v.at[buf], row_sems.at[buf])

        def compute(slot, t, q, buf):
            ssplat = jnp.full((L,), slot, jnp.int32)
            tsplat = jnp.full((L,), t, jnp.int32)
            for c in range(DW // SC_MIX_CHUNK):
                def body(kg, acc):
                    kk = kg * SC_MIX_GROUP
                    wks = [_sc_bf16(plsc.load_gather(
                        w_v, [ssplat, tsplat, jnp.full((L,), q * G + i, jnp.int32) + kk]))
                        for i in range(SC_MIX_GROUP)]
                    out = []
                    for j in range(nj):
                        prods = [wks[i] * _sc_bf16(rows_v[buf, kk + i, pl.ds(c * SC_MIX_CHUNK + j * L, L)])
                                 for i in range(SC_MIX_GROUP)]
                        lo, hi = _sc_halves_f32(_tree_sum(prods))
                        out += [acc[2 * j] + lo, acc[2 * j + 1] + hi]
                    return tuple(out)

                zero = jnp.zeros((L,), jnp.float32)
                acc = plsc.parallel_loop(0, G // SC_MIX_GROUP, carry=(zero,) * (2 * nj))(body)
                for j in range(nj):
                    for half in range(2):
                        dst = out_v.at[slot, t, pl.ds(half * DW + c * SC_MIX_CHUNK + j * L, L)]
                        if q == 0:
                            dst[...] = acc[2 * j + half]
                        else:
                            plsc.addupdate(dst, acc[2 * j + half])

        _sc_block_pipeline(tok_per_w // SC_TOKENS, K // G, loads, store, gather, compute)

    return scratch, run


def _sc_kernel(out_type, scratch_types, name):
    return functools.partial(
        pl.kernel, mesh=_sc_mesh(), out_type=out_type, scratch_types=scratch_types,
        compiler_params=pltpu.CompilerParams(needs_layout_passes=False), name=name)


def _expert_mix(w, idx, table):
    T, K = w.shape
    DW = table.shape[1]
    scratch, run = _mix_parts(T, K, DW)

    @_sc_kernel(jax.ShapeDtypeStruct((T, 2 * DW), jnp.float32), scratch, "expert_mix")
    def k(w_hbm, idx_hbm, tab_hbm, out_hbm, *mix_scratch):
        run(w_hbm, idx_hbm, tab_hbm, out_hbm, *mix_scratch)

    return k(w, idx, table)


def _chunk_sizes(total):
    ramp, size = [EDGE_CHUNK], EDGE_CHUNK
    while size < MAX_CHUNK:
        ramp.append(size)
        size *= 2
    middle = total - 2 * sum(ramp)
    assert middle >= 0 and middle % MAX_CHUNK == 0
    return ramp + [MAX_CHUNK] * (middle // MAX_CHUNK) + ramp[::-1]


def kernel(x, norm_mix, w_in, pool_w, pool_scale, sgu_ln_g, sgu_ln_b, sgu_w, sgu_b, out_norm_pool,
           out_norm_sgu, w_out, norm_ffn, peer_wq, peer_keys, peer_u, peer_v, norm_final):
    B, S, D = x.shape
    assert norm_mix.shape[0] == 1, "single-layer block"
    T = B * S
    mix_args = (norm_mix[0], w_in[0], pool_w[0], pool_scale[0], sgu_ln_g[0], sgu_ln_b[0],
                sgu_w[0], sgu_b[0], out_norm_pool[0], out_norm_sgu[0], w_out[0])
    x1_parts = [(b * S, _mixer(x, b, 1, *mix_args).reshape(S, D)) for b in range(B)]
    wq = peer_wq[0].astype(jnp.bfloat16)
    keys = peer_keys[0].astype(jnp.bfloat16)
    u_t = _transposed_bf16(peer_u[0])
    v_tab = _pack_table(peer_v[0])
    out = None
    tok0 = 0
    peers = [norm_ffn, v_tab]
    for tc in _chunk_sizes(T):
        part0, x1 = [p for p in x1_parts if p[0] <= tok0][-1]
        assert tok0 + tc <= part0 + x1.shape[0], "a token chunk must lie inside one mixer call"
        h2, idx, gate = _router(x1, norm_ffn[0], wq, keys, tok0 - part0, tc, peers[-2])
        w = _expert_weights(h2, u_t, idx, gate)
        peer = _expert_mix(w, idx, v_tab)
        peers.append(peer)
        out = _final(x1, tok0 - part0, peer, norm_final, out, tok0, T)
        tok0 += tc
    return out.reshape(B, S, D)
```

```python
import functools
import math

import jax
import jax.numpy as jnp
from jax import lax
from jax.experimental import pallas as pl
from jax.experimental.pallas import tpu as pltpu
from jax.experimental.pallas import tpu_sc as plsc

POOL_WINDOWS = (2, 4, 8, 16)
N_POOL_GROUPS = len(POOL_WINDOWS)
SGU_HEADS = 4
SGU_CHUNK = 128
PEER_HEADS = 8
PEER_N_KEYS = 128
PEER_D_HALF = 128
PEER_TOPK = 16
NORM_EPS = 1e-6
EXPERTS_PER_TOKEN = PEER_HEADS * PEER_TOPK

V7X_LANES = 128
V7X_SUBLANES = 8
V7X_SC_CORES = 2
V7X_SC_SUBCORES = 16
V7X_SC_LANES = 16
V7X_SC_WORKERS = V7X_SC_CORES * V7X_SC_SUBCORES

HALO = max(POOL_WINDOWS)
MIX_TILE = 512
ROUTE_TILE = 1024
EW_TILE = 512
SCORE_TOK_TILE = 2048
SCORE_EXP_TILE = 2048
SC_GATHER = 64
SC_TOKENS = 8
SC_MIX_CHUNK = 128
SC_MIX_GROUP = 4
HI_HALF = 0xFFFF0000
V7X_VMEM_BYTES = 64 * 1024 * 1024
TC_VMEM_LIMIT = V7X_VMEM_BYTES * 3 // 4
EDGE_CHUNK = 256
MAX_CHUNK = 2048


def _rms(x, g):
    inv = lax.rsqrt(jnp.mean(x * x, axis=-1, keepdims=True) + NORM_EPS)
    return x * inv * g


def _pack_halves(bits):
    half = bits.shape[1] // 2
    return (bits[:, :half] >> 16) | (bits[:, half:] & jnp.uint32(HI_HALF))


def _pack_table_kernel(a_ref, o_ref):
    o_ref[...] = _pack_halves(pltpu.bitcast(a_ref[...].astype(jnp.bfloat16).astype(jnp.float32), jnp.uint32))


def _pack_table(a):
    n, c = a.shape
    rows = min(EW_TILE, n)
    return pl.pallas_call(
        _pack_table_kernel, grid=(n // rows,),
        in_specs=[pl.BlockSpec((rows, c), lambda i: (i, 0))],
        out_specs=pl.BlockSpec((rows, c // 2), lambda i: (i, 0)),
        out_shape=jax.ShapeDtypeStruct((n, c // 2), jnp.uint32),
        compiler_params=pltpu.CompilerParams(dimension_semantics=("parallel",)),
        name="pack_table",
    )(a)


def _transpose_cast_kernel(a_ref, o_ref):
    o_ref[...] = a_ref[...].T.astype(o_ref.dtype)


def _transposed_bf16(a):
    n, c = a.shape
    rows = min(SCORE_EXP_TILE, n)
    return pl.pallas_call(
        _transpose_cast_kernel, grid=(n // rows,),
        in_specs=[pl.BlockSpec((rows, c), lambda i: (i, 0))],
        out_specs=pl.BlockSpec((c, rows), lambda i: (0, i)),
        out_shape=jax.ShapeDtypeStruct((c, n), jnp.bfloat16),
        compiler_params=pltpu.CompilerParams(dimension_semantics=("parallel",)),
        name="transpose_table",
    )(a)


def _gelu(x):
    return 0.5 * x * (1.0 + lax.erf(x * math.sqrt(0.5)))


def _mixer_kernel(x_ref, xh_ref, nmix_ref, win_ref, poolw_ref, pscale_ref, lng_ref, lnb_ref,
                  sguw_ref, sgub_ref, onp_ref, ons_ref, wout_ref, o_ref, pext_ref, mix_ref):
    i = pl.program_id(1)
    ts = x_ref.shape[1]
    pool_w = pscale_ref.shape[1]
    gdim = pool_w // N_POOL_GROUPS
    sgu_w = lng_ref.shape[1]
    hdim = sgu_w // SGU_HEADS

    x = x_ref[0]
    h = _rms(x, nmix_ref[...]).astype(jnp.bfloat16)
    z = jnp.dot(h, win_ref[...], preferred_element_type=jnp.float32)
    p = z[:, :pool_w]

    hh = _rms(xh_ref[0], nmix_ref[...]).astype(jnp.bfloat16)
    ph = jnp.dot(hh, win_ref[:, :pool_w], preferred_element_type=jnp.float32)
    ph = jnp.where(i > 0, ph, 0.0)
    pext_ref[0:HALO, :] = ph
    pext_ref[HALO:HALO + ts, :] = p

    pos = i * ts + lax.broadcasted_iota(jnp.int32, (ts, 1), 0)
    ssq = jnp.zeros((ts, 1), jnp.float32)
    a_parts = []
    for g, win in enumerate(POOL_WINDOWS):
        cols = slice(g * gdim, (g + 1) * gdim)
        s = pext_ref[HALO:HALO + ts, cols]
        for j in range(1, win):
            s = s + pext_ref[HALO - j:HALO - j + ts, cols]
        cnt = jnp.minimum(pos + 1, win).astype(jnp.float32)
        d = (s / cnt - p[:, cols]).astype(jnp.bfloat16)
        a = jnp.dot(d, poolw_ref[g], preferred_element_type=jnp.float32) * pscale_ref[:, cols]
        ssq = ssq + jnp.sum(a * a, axis=-1, keepdims=True)
        a_parts.append(a)
    inv_a = lax.rsqrt(ssq / pool_w + NORM_EPS)
    for g in range(N_POOL_GROUPS):
        cols = slice(g * gdim, (g + 1) * gdim)
        mix_ref[:, cols] = (a_parts[g] * inv_a * onp_ref[:, cols]).astype(jnp.bfloat16)

    gz = _gelu(z[:, pool_w:])
    tril = (lax.broadcasted_iota(jnp.int32, (SGU_CHUNK, SGU_CHUNK), 0)
            >= lax.broadcasted_iota(jnp.int32, (SGU_CHUNK, SGU_CHUNK), 1))
    ssq = jnp.zeros((ts, 1), jnp.float32)
    b_parts = []
    for hd in range(SGU_HEADS):
        cols = slice(hd * hdim, (hd + 1) * hdim)
        u = gz[:, hd * hdim:(hd + 1) * hdim]
        v = gz[:, sgu_w + hd * hdim:sgu_w + (hd + 1) * hdim]
        mu = jnp.mean(v, axis=-1, keepdims=True)
        vc = v - mu
        var = jnp.mean(vc * vc, axis=-1, keepdims=True)
        vn = (vc * lax.rsqrt(var + NORM_EPS) * lng_ref[:, cols] + lnb_ref[:, cols]).astype(jnp.bfloat16)
        w = jnp.where(tril, sguw_ref[hd], jnp.zeros((), sguw_ref.dtype))
        mixed = [jnp.dot(w, vn[n * SGU_CHUNK:(n + 1) * SGU_CHUNK], preferred_element_type=jnp.float32)
                 + sgub_ref[hd] for n in range(ts // SGU_CHUNK)]
        b = u * jnp.concatenate(mixed, axis=0)
        ssq = ssq + jnp.sum(b * b, axis=-1, keepdims=True)
        b_parts.append(b)
    inv_b = lax.rsqrt(ssq / sgu_w + NORM_EPS)
    for hd in range(SGU_HEADS):
        cols = slice(hd * hdim, (hd + 1) * hdim)
        mix_ref[:, pool_w + hd * hdim:pool_w + (hd + 1) * hdim] = (
            b_parts[hd] * inv_b * ons_ref[:, cols]).astype(jnp.bfloat16)

    o_ref[0] = x + jnp.dot(mix_ref[...], wout_ref[...], preferred_element_type=jnp.float32)


def _mixer(x, b0, nb, norm_mix, w_in, pool_w, pool_scale, ln_g, ln_b, sgu_w, sgu_b, on_pool, on_sgu, w_out):
    _, S, D = x.shape
    ts = min(MIX_TILE, S)
    pool_width = pool_scale.size
    sgu_width = ln_g.size
    in_width = w_in.shape[1]
    gdim = pool_width // N_POOL_GROUPS
    halo_blocks = ts // HALO
    full = lambda shape: pl.BlockSpec(shape, lambda b, i: (0,) * len(shape))
    return pl.pallas_call(
        _mixer_kernel,
        grid=(nb, S // ts),
        in_specs=[
            pl.BlockSpec((1, ts, D), lambda b, i: (b0 + b, i, 0)),
            pl.BlockSpec((1, HALO, D), lambda b, i: (b0 + b, jnp.maximum(i * halo_blocks - 1, 0), 0)),
            full((1, D)),
            full((D, in_width)),
            full((N_POOL_GROUPS, gdim, gdim)),
            full((1, pool_width)),
            full((1, sgu_width)),
            full((1, sgu_width)),
            full((SGU_HEADS, SGU_CHUNK, SGU_CHUNK)),
            full((SGU_HEADS, SGU_CHUNK, SGU_CHUNK)),
            full((1, pool_width)),
            full((1, sgu_width)),
            full((pool_width + sgu_width, D)),
        ],
        out_specs=pl.BlockSpec((1, ts, D), lambda b, i: (b, i, 0)),
        out_shape=jax.ShapeDtypeStruct((nb, S, D), jnp.float32),
        scratch_shapes=[
            pltpu.VMEM((HALO + ts, pool_width), jnp.float32),
            pltpu.VMEM((ts, pool_width + sgu_width), jnp.bfloat16),
        ],
        compiler_params=pltpu.CompilerParams(
            dimension_semantics=("parallel", "arbitrary"), vmem_limit_bytes=TC_VMEM_LIMIT),
        name="mixer",
    )(x, x, norm_mix.reshape(1, D), w_in.astype(jnp.bfloat16), pool_w.astype(jnp.bfloat16),
      pool_scale.reshape(1, pool_width), ln_g.reshape(1, sgu_width), ln_b.reshape(1, sgu_width),
      sgu_w.astype(jnp.bfloat16),
      jnp.broadcast_to(sgu_b[:, :, None], (SGU_HEADS, SGU_CHUNK, SGU_CHUNK)),
      on_pool.reshape(1, pool_width), on_sgu.reshape(1, sgu_width), w_out.astype(jnp.bfloat16))


def _topk_rows(s, k):
    n = s.shape[0]
    iota = lax.broadcasted_iota(jnp.int32, s.shape, 0)
    vals, idxs = [], []
    for _ in range(k):
        m = jnp.max(s, axis=0, keepdims=True)
        ix = jnp.min(jnp.where(s == m, iota, n), axis=0, keepdims=True)
        vals.append(m)
        idxs.append(ix)
        s = jnp.where(iota == ix, -jnp.inf, s)
    return vals, idxs


def _pair_candidates(v1, i1, v2, i2):
    k = PEER_TOPK
    v2c, i2c = jnp.concatenate(v2, axis=0), jnp.concatenate(i2, axis=0)
    m = v2c.shape[1]
    vals, experts, flats = [], [], []
    a = 0
    while k // (a + 1) > 1:
        rows = -(-(k // (a + 1)) // V7X_SUBLANES) * V7X_SUBLANES
        vals.append(v1[a] + v2c[:rows])
        experts.append(i1[a] * PEER_N_KEYS + i2c[:rows])
        flats.append(a * k + lax.broadcasted_iota(jnp.int32, (rows, m), 0))
        a += 1
    vals.append(jnp.concatenate(v1[a:], axis=0) + v2[0])
    experts.append(jnp.concatenate(i1[a:], axis=0) * PEER_N_KEYS + i2[0])
    flats.append((a + lax.broadcasted_iota(jnp.int32, (k - a, m), 0)) * k)
    return jnp.concatenate(vals, axis=0), jnp.concatenate(experts, axis=0), jnp.concatenate(flats, axis=0)


def _router_kernel(x_ref, nffn_ref, wq_ref, keys_ref, after_ref, h2_ref, idx_ref, gate_ref,
                   q_ref, idxt_ref, gatet_ref):
    del after_ref
    h2 = _rms(x_ref[...], nffn_ref[...]).astype(jnp.bfloat16)
    h2_ref[...] = h2
    q_ref[...] = jnp.dot(h2, wq_ref[...], preferred_element_type=jnp.float32).astype(jnp.bfloat16)
    dq = 2 * PEER_D_HALF
    nt = (((1,), (1,)), ((), ()))

    def head(hd, carry):
        off = pl.multiple_of(hd * dq, dq)
        s1 = lax.dot_general(keys_ref[0], q_ref[:, pl.ds(off, PEER_D_HALF)], nt,
                             preferred_element_type=jnp.float32)
        s2 = lax.dot_general(keys_ref[1], q_ref[:, pl.ds(off + PEER_D_HALF, PEER_D_HALF)], nt,
                             preferred_element_type=jnp.float32)
        v1, i1 = _topk_rows(s1, PEER_TOPK)
        v2, i2 = _topk_rows(s2, PEER_TOPK)
        cand, expert, flat = _pair_candidates(v1, i1, v2, i2)
        cv, ce = [], []
        for _ in range(PEER_TOPK):
            m = jnp.max(cand, axis=0, keepdims=True)
            ix = jnp.min(jnp.where(cand == m, flat, PEER_TOPK * PEER_TOPK), axis=0, keepdims=True)
            hit = flat == ix
            cv.append(m)
            ce.append(jnp.max(jnp.where(hit, expert, -1), axis=0, keepdims=True))
            cand = jnp.where(hit, -jnp.inf, cand)
        cvc = jnp.concatenate(cv, axis=0)
        e = jnp.exp(cvc - cv[0])
        gate = e / jnp.sum(e, axis=0, keepdims=True)
        row = pl.multiple_of(hd * PEER_TOPK, PEER_TOPK)
        idxt_ref[pl.ds(row, PEER_TOPK), :] = jnp.concatenate(ce, axis=0)
        gatet_ref[pl.ds(row, PEER_TOPK), :] = gate
        return carry

    lax.fori_loop(0, PEER_HEADS, head, 0)
    idx_ref[...] = idxt_ref[...].T
    gate_ref[...] = gatet_ref[...].T


def _router(x1, norm_ffn, wq, keys, tok0, T, after):
    D = x1.shape[1]
    tr = min(ROUTE_TILE, T)
    qw = wq.shape[1]
    assert tok0 % tr == 0 and T % tr == 0
    first = tok0 // tr
    full = lambda shape: pl.BlockSpec(shape, lambda i: (0,) * len(shape))
    return pl.pallas_call(
        _router_kernel,
        grid=(T // tr,),
        in_specs=[
            pl.BlockSpec((tr, D), lambda i: (first + i, 0)),
            full((1, D)),
            full((D, qw)),
            full((2, PEER_N_KEYS, PEER_D_HALF)),
            pl.BlockSpec(memory_space=pl.ANY),
        ],
        out_specs=[
            pl.BlockSpec((tr, D), lambda i: (i, 0)),
            pl.BlockSpec((tr, EXPERTS_PER_TOKEN), lambda i: (i, 0)),
            pl.BlockSpec((tr, EXPERTS_PER_TOKEN), lambda i: (i, 0)),
        ],
        out_shape=[
            jax.ShapeDtypeStruct((T, D), jnp.bfloat16),
            jax.ShapeDtypeStruct((T, EXPERTS_PER_TOKEN), jnp.int32),
            jax.ShapeDtypeStruct((T, EXPERTS_PER_TOKEN), jnp.float32),
        ],
        scratch_shapes=[
            pltpu.VMEM((tr, qw), jnp.bfloat16),
            pltpu.VMEM((EXPERTS_PER_TOKEN, tr), jnp.int32),
            pltpu.VMEM((EXPERTS_PER_TOKEN, tr), jnp.float32),
        ],
        compiler_params=pltpu.CompilerParams(
            dimension_semantics=("parallel",), vmem_limit_bytes=TC_VMEM_LIMIT),
        name="router",
    )(x1, norm_ffn.reshape(1, D), wq, keys, after)


def _scores_kernel(h_ref, ut_ref, idx_ref, gate_ref, w_ref, pre_ref):
    j = pl.program_id(1)
    acc = jnp.dot(h_ref[...], ut_ref[...], preferred_element_type=jnp.float32)
    groups = acc.shape[1] // V7X_LANES
    lane_bits = int(math.log2(V7X_LANES))
    idx = idx_ref[...]
    group, lane = idx >> lane_bits, idx & (V7X_LANES - 1)
    picked = jnp.zeros(idx.shape, jnp.float32)
    for n in range(groups):
        vals = jnp.take_along_axis(acc[:, n * V7X_LANES:(n + 1) * V7X_LANES], lane, axis=1)
        picked = jnp.where(group == j * groups + n, vals, picked)

    @pl.when(j == 0)
    def _():
        pre_ref[...] = picked

    @pl.when(j > 0)
    def _():
        pre_ref[...] += picked

    @pl.when(j == pl.num_programs(1) - 1)
    def _():
        w = (gate_ref[...] * _gelu(pre_ref[...])).astype(jnp.bfloat16).astype(jnp.float32)
        hi = pltpu.bitcast(w, jnp.uint32) & jnp.uint32(HI_HALF)
        w_ref[...] = pltpu.bitcast(hi | (hi >> 16), jnp.int32)


def _expert_weights(h2, u_t, idx, gate):
    T, D = h2.shape
    E = u_t.shape[1]
    K = idx.shape[1]
    tm, tn = min(SCORE_TOK_TILE, T), min(SCORE_EXP_TILE, E)
    assert T % tm == 0 and E % tn == 0 and tn % V7X_LANES == 0 and K == V7X_LANES
    per_token = pl.BlockSpec((tm, K), lambda i, j: (i, 0))
    return pl.pallas_call(
        _scores_kernel,
        grid=(T // tm, E // tn),
        in_specs=[pl.BlockSpec((tm, D), lambda i, j: (i, 0)), pl.BlockSpec((D, tn), lambda i, j: (0, j)),
                  per_token, per_token],
        out_specs=per_token,
        out_shape=jax.ShapeDtypeStruct((T, K), jnp.int32),
        scratch_shapes=[pltpu.VMEM((tm, K), jnp.float32)],
        compiler_params=pltpu.CompilerParams(
            dimension_semantics=("parallel", "arbitrary"), vmem_limit_bytes=TC_VMEM_LIMIT),
        name="expert_weights",
    )(h2, u_t, idx, gate)


def _final_kernel(x_ref, y_ref, g_ref, *rest):
    o_ref = rest[-1]
    o_ref[...] = _rms(x_ref[...] + y_ref[...], g_ref[...])


def _final(x1, x_tok0, peer, norm_final, out_prev, out_tok0, total):
    Tc, D = peer.shape
    te = min(EW_TILE, Tc)
    assert x_tok0 % te == 0 and out_tok0 % te == 0 and Tc % te == 0
    nblk = Tc // te
    spec = pl.BlockSpec((te, D), lambda i: (out_tok0 // te + i, 0))
    in_specs = [pl.BlockSpec((te, D), lambda i: (x_tok0 // te + i, 0)),
                pl.BlockSpec((te, D), lambda i: (i, 0)), pl.BlockSpec((1, D), lambda i: (0, 0))]
    args = [x1, peer, norm_final.reshape(1, D)]
    aliases = {}
    if out_prev is not None:
        in_specs.append(pl.BlockSpec(memory_space=pl.ANY))
        args.append(out_prev)
        aliases = {3: 0}
    return pl.pallas_call(
        _final_kernel, grid=(nblk,), in_specs=in_specs, out_specs=spec,
        out_shape=jax.ShapeDtypeStruct((total, D), jnp.float32),
        input_output_aliases=aliases,
        compiler_params=pltpu.CompilerParams(dimension_semantics=("parallel",)),
        name="final_norm",
    )(*args)


def _tree_sum(vals):
    while len(vals) > 1:
        nxt = [vals[i] + vals[i + 1] for i in range(0, len(vals) - 1, 2)]
        if len(vals) % 2:
            nxt.append(vals[-1])
        vals = nxt
    return vals[0]


def _sc_block_pipeline(nblk, items_per_token, loads, store, gather, compute):
    assert items_per_token % 2 == 0 and nblk >= 1

    for c in loads(0, 0):
        c.start()
    for c in loads(0, 0):
        c.wait()
    if nblk > 1:
        for c in loads(1, 1):
            c.start()
    gather(0, 0, 0, 0).start()

    @pl.loop(0, nblk)
    def _(b):
        slot = b % 2

        @pl.when(b >= 2)
        def _():
            store(b - 2, slot).wait()

        @pl.loop(0, SC_TOKENS)
        def _(t):
            for q in range(items_per_token):
                buf = q % 2
                if q + 1 < items_per_token:
                    gather(slot, t, q + 1, 1 - buf).start()
                else:
                    @pl.when(t + 1 < SC_TOKENS)
                    def _():
                        gather(slot, t + 1, 0, 1 - buf).start()

                    @pl.when(jnp.logical_and(t + 1 == SC_TOKENS, b + 1 < nblk))
                    def _():
                        for c in loads(b + 1, 1 - slot):
                            c.wait()
                        gather(1 - slot, 0, 0, 1 - buf).start()

                gather(slot, t, q, buf).wait()
                compute(slot, t, q, buf)

        store(b, slot).start()

        @pl.when(b + 2 < nblk)
        def _():
            for c in loads(b + 2, slot):
                c.start()

    if nblk >= 2:
        store(nblk - 2, nblk % 2).wait()
    store(nblk - 1, (nblk - 1) % 2).wait()


def _sc_mesh():
    return plsc.VectorSubcoreMesh(core_axis_name="c", subcore_axis_name="s")


def _sc_worker_id():
    return lax.axis_index("s") * V7X_SC_CORES + lax.axis_index("c")


def _sc_bf16(words):
    return plsc.bitcast(words, jnp.bfloat16)


def _sc_halves_f32(pairs):
    words = plsc.bitcast(pairs, jnp.uint32)
    return (plsc.bitcast(words << 16, jnp.float32),
            plsc.bitcast(words & jnp.uint32(HI_HALF), jnp.float32))


def _mix_parts(T, K, DW):
    D = 2 * DW
    L, G = V7X_SC_LANES, SC_GATHER
    nj = SC_MIX_CHUNK // L
    tok_per_w = T // V7X_SC_WORKERS
    assert T % (V7X_SC_WORKERS * SC_TOKENS) == 0 and K % (2 * G) == 0
    assert DW % SC_MIX_CHUNK == 0 and G % SC_MIX_GROUP == 0
    scratch = [
        pltpu.VMEM((2, SC_TOKENS, K), jnp.int32),
        pltpu.VMEM((2, SC_TOKENS, K), jnp.int32),
        pltpu.VMEM((2, G, DW), jnp.uint32),
        pltpu.VMEM((2, SC_TOKENS, D), jnp.float32),
        pltpu.SemaphoreType.DMA((2,)),
        pltpu.SemaphoreType.DMA((2,)),
        pltpu.SemaphoreType.DMA((2,)),
        pltpu.SemaphoreType.DMA((2,)),
    ]

    def run(w_hbm, idx_hbm, tab_hbm, out_hbm, idx_v, w_v, rows_v, out_v, idx_sems, w_sems, out_sems, row_sems):
        base = _sc_worker_id() * tok_per_w

        def loads(b, slot):
            toks = pl.ds(base + b * SC_TOKENS, SC_TOKENS)
            return [pltpu.make_async_copy(idx_hbm.at[toks], idx_v.at[slot], idx_sems.at[slot]),
                    pltpu.make_async_copy(w_hbm.at[toks], w_v.at[slot], w_sems.at[slot])]

        def store(b, slot):
            toks = pl.ds(base + b * SC_TOKENS, SC_TOKENS)
            return pltpu.make_async_copy(out_v.at[slot], out_hbm.at[toks], out_sems.at[slot])

        def gather(slot, t, q, buf):
            return pltpu.make_async_copy(
                tab_hbm.at[idx_v.at[slot, t, pl.ds(q * G, G)]], rows_v.at[buf], row_sems.at[buf])

        def compute(slot, t, q, buf):
            ssplat = jnp.full((L,), slot, jnp.int32)
            tsplat = jnp.full((L,), t, jnp.int32)
            for c in range(DW // SC_MIX_CHUNK):
                def body(kg, acc):
                    kk = kg * SC_MIX_GROUP
                    wks = [_sc_bf16(plsc.load_gather(
                        w_v, [ssplat, tsplat, jnp.full((L,), q * G + i, jnp.int32) + kk]))
                        for i in range(SC_MIX_GROUP)]
                    out = []
                    for j in range(nj):
                        prods = [wks[i] * _sc_bf16(rows_v[buf, kk + i, pl.ds(c * SC_MIX_CHUNK + j * L, L)])
                                 for i in range(SC_MIX_GROUP)]
                        lo, hi = _sc_halves_f32(_tree_sum(prods))
                        out += [acc[2 * j] + lo, acc[2 * j + 1] + hi]
                    return tuple(out)

                zero = jnp.zeros((L,), jnp.float32)
                acc = plsc.parallel_loop(0, G // SC_MIX_GROUP, carry=(zero,) * (2 * nj))(body)
                for j in range(nj):
                    for half in range(2):
                        dst = out_v.at[slot, t, pl.ds(half * DW + c * SC_MIX_CHUNK + j * L, L)]
                        if q == 0:
                            dst[...] = acc[2 * j + half]
                        else:
                            plsc.addupdate(dst, acc[2 * j + half])

        _sc_block_pipeline(tok_per_w // SC_TOKENS, K // G, loads, store, gather, compute)

    return scratch, run


def _sc_kernel(out_type, scratch_types, name):
    return functools.partial(
        pl.kernel, mesh=_sc_mesh(), out_type=out_type, scratch_types=scratch_types,
        compiler_params=pltpu.CompilerParams(needs_layout_passes=False), name=name)


def _expert_mix(w, idx, table):
    T, K = w.shape
    DW = table.shape[1]
    scratch, run = _mix_parts(T, K, DW)

    @_sc_kernel(jax.ShapeDtypeStruct((T, 2 * DW), jnp.float32), scratch, "expert_mix")
    def k(w_hbm, idx_hbm, tab_hbm, out_hbm, *mix_scratch):
        run(w_hbm, idx_hbm, tab_hbm, out_hbm, *mix_scratch)

    return k(w, idx, table)


def _chunk_sizes(total):
    ramp, size = [EDGE_CHUNK], EDGE_CHUNK
    while size < MAX_CHUNK:
        ramp.append(size)
        size *= 2
    middle = total - 2 * sum(ramp)
    assert middle >= 0 and middle % MAX_CHUNK == 0
    return ramp + [MAX_CHUNK] * (middle // MAX_CHUNK) + ramp[::-1]


def kernel(x, norm_mix, w_in, pool_w, pool_scale, sgu_ln_g, sgu_ln_b, sgu_w, sgu_b, out_norm_pool,
           out_norm_sgu, w_out, norm_ffn, peer_wq, peer_keys, peer_u, peer_v, norm_final):
    B, S, D = x.shape
    assert norm_mix.shape[0] == 1, "single-layer block"
    T = B * S
    mix_args = (norm_mix[0], w_in[0], pool_w[0], pool_scale[0], sgu_ln_g[0], sgu_ln_b[0],
                sgu_w[0], sgu_b[0], out_norm_pool[0], out_norm_sgu[0], w_out[0])
    x1_parts = [(b * S, _mixer(x, b, 1, *mix_args).reshape(S, D)) for b in range(B)]
    wq = peer_wq[0].astype(jnp.bfloat16)
    keys = peer_keys[0].astype(jnp.bfloat16)
    u_t = _transposed_bf16(peer_u[0])
    v_tab = _pack_table(peer_v[0])
    out = None
    tok0 = 0
    peers = [norm_ffn, v_tab]
    for tc in _chunk_sizes(T):
        part0, x1 = [p for p in x1_parts if p[0] <= tok0][-1]
        assert tok0 + tc <= part0 + x1.shape[0], "a token chunk must lie inside one mixer call"
        h2, idx, gate = _router(x1, norm_ffn[0], wq, keys, tok0 - part0, tc, peers[-2])
        w = _expert_weights(h2, u_t, idx, gate)
        peer = _expert_mix(w, idx, v_tab)
        peers.append(peer)
        out = _final(x1, tok0 - part0, peer, norm_final, out, tok0, T)
        tok0 += tc
    return out.reshape(B, S, D)
```

```python
import functools
import math

import jax
import jax.numpy as jnp
from jax import lax
from jax.experimental import pallas as pl
from jax.experimental.pallas import tpu as pltpu
from jax.experimental.pallas import tpu_sc as plsc

POOL_WINDOWS = (2, 4, 8, 16)
N_POOL_GROUPS = len(POOL_WINDOWS)
SGU_HEADS = 4
SGU_CHUNK = 128
PEER_HEADS = 8
PEER_N_KEYS = 128
PEER_D_HALF = 128
PEER_TOPK = 16
NORM_EPS = 1e-6
EXPERTS_PER_TOKEN = PEER_HEADS * PEER_TOPK

V7X_LANES = 128
V7X_SUBLANES = 8
V7X_SC_CORES = 2
V7X_SC_SUBCORES = 16
V7X_SC_LANES = 16
V7X_SC_WORKERS = V7X_SC_CORES * V7X_SC_SUBCORES

HALO = max(POOL_WINDOWS)
MIX_TILE = 512
ROUTE_TILE = 1024
EW_TILE = 512
SCORE_TOK_TILE = 2048
SCORE_EXP_TILE = 2048
SC_PACK_ROWS = 16
SC_GATHER = 64
SC_TOKENS = 8
SC_MIX_CHUNK = 128
SC_MIX_GROUP = 4
HI_HALF = 0xFFFF0000
V7X_VMEM_BYTES = 64 * 1024 * 1024
TC_VMEM_LIMIT = V7X_VMEM_BYTES * 3 // 4
EDGE_CHUNK = 512
MAX_CHUNK = 2048


def _rms(x, g):
    inv = lax.rsqrt(jnp.mean(x * x, axis=-1, keepdims=True) + NORM_EPS)
    return x * inv * g


def _transpose_cast_kernel(a_ref, o_ref):
    o_ref[...] = a_ref[...].T.astype(o_ref.dtype)


def _transposed_bf16(a):
    n, c = a.shape
    rows = min(SCORE_EXP_TILE, n)
    return pl.pallas_call(
        _transpose_cast_kernel, grid=(n // rows,),
        in_specs=[pl.BlockSpec((rows, c), lambda i: (i, 0))],
        out_specs=pl.BlockSpec((c, rows), lambda i: (0, i)),
        out_shape=jax.ShapeDtypeStruct((c, n), jnp.bfloat16),
        compiler_params=pltpu.CompilerParams(dimension_semantics=("parallel",)),
        name="transpose_table",
    )(a)


def _gelu(x):
    return 0.5 * x * (1.0 + lax.erf(x * math.sqrt(0.5)))


def _mixer_kernel(x_ref, xh_ref, nmix_ref, win_ref, poolw_ref, pscale_ref, lng_ref, lnb_ref,
                  sguw_ref, sgub_ref, onp_ref, ons_ref, wout_ref, o_ref, pext_ref, mix_ref):
    i = pl.program_id(1)
    ts = x_ref.shape[1]
    pool_w = pscale_ref.shape[1]
    gdim = pool_w // N_POOL_GROUPS
    sgu_w = lng_ref.shape[1]
    hdim = sgu_w // SGU_HEADS

    x = x_ref[0]
    h = _rms(x, nmix_ref[...]).astype(jnp.bfloat16)
    z = jnp.dot(h, win_ref[...], preferred_element_type=jnp.float32)
    p = z[:, :pool_w]

    hh = _rms(xh_ref[0], nmix_ref[...]).astype(jnp.bfloat16)
    ph = jnp.dot(hh, win_ref[:, :pool_w], preferred_element_type=jnp.float32)
    ph = jnp.where(i > 0, ph, 0.0)
    pext_ref[0:HALO, :] = ph
    pext_ref[HALO:HALO + ts, :] = p

    pos = i * ts + lax.broadcasted_iota(jnp.int32, (ts, 1), 0)
    ssq = jnp.zeros((ts, 1), jnp.float32)
    a_parts = []
    for g, win in enumerate(POOL_WINDOWS):
        cols = slice(g * gdim, (g + 1) * gdim)
        s = pext_ref[HALO:HALO + ts, cols]
        for j in range(1, win):
            s = s + pext_ref[HALO - j:HALO - j + ts, cols]
        cnt = jnp.minimum(pos + 1, win).astype(jnp.float32)
        d = (s / cnt - p[:, cols]).astype(jnp.bfloat16)
        a = jnp.dot(d, poolw_ref[g], preferred_element_type=jnp.float32) * pscale_ref[:, cols]
        ssq = ssq + jnp.sum(a * a, axis=-1, keepdims=True)
        a_parts.append(a)
    inv_a = lax.rsqrt(ssq / pool_w + NORM_EPS)
    for g in range(N_POOL_GROUPS):
        cols = slice(g * gdim, (g + 1) * gdim)
        mix_ref[:, cols] = (a_parts[g] * inv_a * onp_ref[:, cols]).astype(jnp.bfloat16)

    gz = _gelu(z[:, pool_w:])
    tril = (lax.broadcasted_iota(jnp.int32, (SGU_CHUNK, SGU_CHUNK), 0)
            >= lax.broadcasted_iota(jnp.int32, (SGU_CHUNK, SGU_CHUNK), 1))
    ssq = jnp.zeros((ts, 1), jnp.float32)
    b_parts = []
    for hd in range(SGU_HEADS):
        cols = slice(hd * hdim, (hd + 1) * hdim)
        u = gz[:, hd * hdim:(hd + 1) * hdim]
        v = gz[:, sgu_w + hd * hdim:sgu_w + (hd + 1) * hdim]
        mu = jnp.mean(v, axis=-1, keepdims=True)
        vc = v - mu
        var = jnp.mean(vc * vc, axis=-1, keepdims=True)
        vn = (vc * lax.rsqrt(var + NORM_EPS) * lng_ref[:, cols] + lnb_ref[:, cols]).astype(jnp.bfloat16)
        w = jnp.where(tril, sguw_ref[hd], jnp.zeros((), sguw_ref.dtype))
        mixed = [jnp.dot(w, vn[n * SGU_CHUNK:(n + 1) * SGU_CHUNK], preferred_element_type=jnp.float32)
                 + sgub_ref[hd] for n in range(ts // SGU_CHUNK)]
        b = u * jnp.concatenate(mixed, axis=0)
        ssq = ssq + jnp.sum(b * b, axis=-1, keepdims=True)
        b_parts.append(b)
    inv_b = lax.rsqrt(ssq / sgu_w + NORM_EPS)
    for hd in range(SGU_HEADS):
        cols = slice(hd * hdim, (hd + 1) * hdim)
        mix_ref[:, pool_w + hd * hdim:pool_w + (hd + 1) * hdim] = (
            b_parts[hd] * inv_b * ons_ref[:, cols]).astype(jnp.bfloat16)

    o_ref[0] = x + jnp.dot(mix_ref[...], wout_ref[...], preferred_element_type=jnp.float32)


def _mixer(x, b0, nb, norm_mix, w_in, pool_w, pool_scale, ln_g, ln_b, sgu_w, sgu_b, on_pool, on_sgu, w_out):
    _, S, D = x.shape
    ts = min(MIX_TILE, S)
    pool_width = pool_scale.size
    sgu_width = ln_g.size
    in_width = w_in.shape[1]
    gdim = pool_width // N_POOL_GROUPS
    halo_blocks = ts // HALO
    full = lambda shape: pl.BlockSpec(shape, lambda b, i: (0,) * len(shape))
    return pl.pallas_call(
        _mixer_kernel,
        grid=(nb, S // ts),
        in_specs=[
            pl.BlockSpec((1, ts, D), lambda b, i: (b0 + b, i, 0)),
            pl.BlockSpec((1, HALO, D), lambda b, i: (b0 + b, jnp.maximum(i * halo_blocks - 1, 0), 0)),
            full((1, D)),
            full((D, in_width)),
            full((N_POOL_GROUPS, gdim, gdim)),
            full((1, pool_width)),
            full((1, sgu_width)),
            full((1, sgu_width)),
            full((SGU_HEADS, SGU_CHUNK, SGU_CHUNK)),
            full((SGU_HEADS, SGU_CHUNK, SGU_CHUNK)),
            full((1, pool_width)),
            full((1, sgu_width)),
            full((pool_width + sgu_width, D)),
        ],
        out_specs=pl.BlockSpec((1, ts, D), lambda b, i: (b, i, 0)),
        out_shape=jax.ShapeDtypeStruct((nb, S, D), jnp.float32),
        scratch_shapes=[
            pltpu.VMEM((HALO + ts, pool_width), jnp.float32),
            pltpu.VMEM((ts, pool_width + sgu_width), jnp.bfloat16),
        ],
        compiler_params=pltpu.CompilerParams(
            dimension_semantics=("parallel", "arbitrary"), vmem_limit_bytes=TC_VMEM_LIMIT),
        name="mixer",
    )(x, x, norm_mix.reshape(1, D), w_in.astype(jnp.bfloat16), pool_w.astype(jnp.bfloat16),
      pool_scale.reshape(1, pool_width), ln_g.reshape(1, sgu_width), ln_b.reshape(1, sgu_width),
      sgu_w.astype(jnp.bfloat16),
      jnp.broadcast_to(sgu_b[:, :, None], (SGU_HEADS, SGU_CHUNK, SGU_CHUNK)),
      on_pool.reshape(1, pool_width), on_sgu.reshape(1, sgu_width), w_out.astype(jnp.bfloat16))


def _topk_rows(s, k):
    n = s.shape[0]
    iota = lax.broadcasted_iota(jnp.int32, s.shape, 0)
    vals, idxs = [], []
    for _ in range(k):
        m = jnp.max(s, axis=0, keepdims=True)
        ix = jnp.min(jnp.where(s == m, iota, n), axis=0, keepdims=True)
        vals.append(m)
        idxs.append(ix)
        s = jnp.where(iota == ix, -jnp.inf, s)
    return vals, idxs


def _pair_candidates(v1, i1, v2, i2):
    k = PEER_TOPK
    v2c, i2c = jnp.concatenate(v2, axis=0), jnp.concatenate(i2, axis=0)
    m = v2c.shape[1]
    vals, experts, flats = [], [], []
    a = 0
    while k // (a + 1) > 1:
        rows = -(-(k // (a + 1)) // V7X_SUBLANES) * V7X_SUBLANES
        vals.append(v1[a] + v2c[:rows])
        experts.append(i1[a] * PEER_N_KEYS + i2c[:rows])
        flats.append(a * k + lax.broadcasted_iota(jnp.int32, (rows, m), 0))
        a += 1
    vals.append(jnp.concatenate(v1[a:], axis=0) + v2[0])
    experts.append(jnp.concatenate(i1[a:], axis=0) * PEER_N_KEYS + i2[0])
    flats.append((a + lax.broadcasted_iota(jnp.int32, (k - a, m), 0)) * k)
    return jnp.concatenate(vals, axis=0), jnp.concatenate(experts, axis=0), jnp.concatenate(flats, axis=0)


def _router_kernel(x_ref, nffn_ref, wq_ref, keys_ref, after_ref, h2_ref, idx_ref, gate_ref,
                   q_ref, idxt_ref, gatet_ref):
    del after_ref
    h2 = _rms(x_ref[...], nffn_ref[...]).astype(jnp.bfloat16)
    h2_ref[...] = h2
    q_ref[...] = jnp.dot(h2, wq_ref[...], preferred_element_type=jnp.float32).astype(jnp.bfloat16)
    dq = 2 * PEER_D_HALF
    nt = (((1,), (1,)), ((), ()))

    def head(hd, carry):
        off = pl.multiple_of(hd * dq, dq)
        s1 = lax.dot_general(keys_ref[0], q_ref[:, pl.ds(off, PEER_D_HALF)], nt,
                             preferred_element_type=jnp.float32)
        s2 = lax.dot_general(keys_ref[1], q_ref[:, pl.ds(off + PEER_D_HALF, PEER_D_HALF)], nt,
                             preferred_element_type=jnp.float32)
        v1, i1 = _topk_rows(s1, PEER_TOPK)
        v2, i2 = _topk_rows(s2, PEER_TOPK)
        cand, expert, flat = _pair_candidates(v1, i1, v2, i2)
        cv, ce = [], []
        for _ in range(PEER_TOPK):
            m = jnp.max(cand, axis=0, keepdims=True)
            ix = jnp.min(jnp.where(cand == m, flat, PEER_TOPK * PEER_TOPK), axis=0, keepdims=True)
            hit = flat == ix
            cv.append(m)
            ce.append(jnp.max(jnp.where(hit, expert, -1), axis=0, keepdims=True))
            cand = jnp.where(hit, -jnp.inf, cand)
        cvc = jnp.concatenate(cv, axis=0)
        e = jnp.exp(cvc - cv[0])
        gate = e / jnp.sum(e, axis=0, keepdims=True)
        row = pl.multiple_of(hd * PEER_TOPK, PEER_TOPK)
        idxt_ref[pl.ds(row, PEER_TOPK), :] = jnp.concatenate(ce, axis=0)
        gatet_ref[pl.ds(row, PEER_TOPK), :] = gate
        return carry

    lax.fori_loop(0, PEER_HEADS, head, 0)
    idx_ref[...] = idxt_ref[...].T
    gate_ref[...] = gatet_ref[...].T


def _router(x1, norm_ffn, wq, keys, tok0, T, after):
    D = x1.shape[1]
    tr = min(ROUTE_TILE, T)
    qw = wq.shape[1]
    assert tok0 % tr == 0 and T % tr == 0
    first = tok0 // tr
    full = lambda shape: pl.BlockSpec(shape, lambda i: (0,) * len(shape))
    return pl.pallas_call(
        _router_kernel,
        grid=(T // tr,),
        in_specs=[
            pl.BlockSpec((tr, D), lambda i: (first + i, 0)),
            full((1, D)),
            full((D, qw)),
            full((2, PEER_N_KEYS, PEER_D_HALF)),
            pl.BlockSpec(memory_space=pl.ANY),
        ],
        out_specs=[
            pl.BlockSpec((tr, D), lambda i: (i, 0)),
            pl.BlockSpec((tr, EXPERTS_PER_TOKEN), lambda i: (i, 0)),
            pl.BlockSpec((tr, EXPERTS_PER_TOKEN), lambda i: (i, 0)),
        ],
        out_shape=[
            jax.ShapeDtypeStruct((T, D), jnp.bfloat16),
            jax.ShapeDtypeStruct((T, EXPERTS_PER_TOKEN), jnp.int32),
            jax.ShapeDtypeStruct((T, EXPERTS_PER_TOKEN), jnp.float32),
        ],
        scratch_shapes=[
            pltpu.VMEM((tr, qw), jnp.bfloat16),
            pltpu.VMEM((EXPERTS_PER_TOKEN, tr), jnp.int32),
            pltpu.VMEM((EXPERTS_PER_TOKEN, tr), jnp.float32),
        ],
        compiler_params=pltpu.CompilerParams(
            dimension_semantics=("parallel",), vmem_limit_bytes=TC_VMEM_LIMIT),
        name="router",
    )(x1, norm_ffn.reshape(1, D), wq, keys, after)


def _scores_kernel(h_ref, ut_ref, idx_ref, gate_ref, w_ref, pre_ref):
    j = pl.program_id(1)
    acc = jnp.dot(h_ref[...], ut_ref[...], preferred_element_type=jnp.float32)
    groups = acc.shape[1] // V7X_LANES
    lane_bits = int(math.log2(V7X_LANES))
    idx = idx_ref[...]
    group, lane = idx >> lane_bits, idx & (V7X_LANES - 1)
    picked = jnp.zeros(idx.shape, jnp.float32)
    for n in range(groups):
        vals = jnp.take_along_axis(acc[:, n * V7X_LANES:(n + 1) * V7X_LANES], lane, axis=1)
        picked = jnp.where(group == j * groups + n, vals, picked)

    @pl.when(j == 0)
    def _():
        pre_ref[...] = picked

    @pl.when(j > 0)
    def _():
        pre_ref[...] += picked

    @pl.when(j == pl.num_programs(1) - 1)
    def _():
        w = (gate_ref[...] * _gelu(pre_ref[...])).astype(jnp.bfloat16).astype(jnp.float32)
        hi = pltpu.bitcast(w, jnp.uint32) & jnp.uint32(HI_HALF)
        w_ref[...] = pltpu.bitcast(hi | (hi >> 16), jnp.int32)


def _expert_weights(h2, u_t, idx, gate):
    T, D = h2.shape
    E = u_t.shape[1]
    K = idx.shape[1]
    tm, tn = min(SCORE_TOK_TILE, T), min(SCORE_EXP_TILE, E)
    assert T % tm == 0 and E % tn == 0 and tn % V7X_LANES == 0 and K == V7X_LANES
    per_token = pl.BlockSpec((tm, K), lambda i, j: (i, 0))
    return pl.pallas_call(
        _scores_kernel,
        grid=(T // tm, E // tn),
        in_specs=[pl.BlockSpec((tm, D), lambda i, j: (i, 0)), pl.BlockSpec((D, tn), lambda i, j: (0, j)),
                  per_token, per_token],
        out_specs=per_token,
        out_shape=jax.ShapeDtypeStruct((T, K), jnp.int32),
        scratch_shapes=[pltpu.VMEM((tm, K), jnp.float32)],
        compiler_params=pltpu.CompilerParams(
            dimension_semantics=("parallel", "arbitrary"), vmem_limit_bytes=TC_VMEM_LIMIT),
        name="expert_weights",
    )(h2, u_t, idx, gate)


def _final_kernel(x_ref, y_ref, g_ref, *rest):
    o_ref = rest[-1]
    o_ref[...] = _rms(x_ref[...] + y_ref[...], g_ref[...])


def _final(x1, x_tok0, peer, norm_final, out_prev, out_tok0, total):
    Tc, D = peer.shape
    te = min(EW_TILE, Tc)
    assert x_tok0 % te == 0 and out_tok0 % te == 0 and Tc % te == 0
    nblk = Tc // te
    spec = pl.BlockSpec((te, D), lambda i: (out_tok0 // te + i, 0))
    in_specs = [pl.BlockSpec((te, D), lambda i: (x_tok0 // te + i, 0)),
                pl.BlockSpec((te, D), lambda i: (i, 0)), pl.BlockSpec((1, D), lambda i: (0, 0))]
    args = [x1, peer, norm_final.reshape(1, D)]
    aliases = {}
    if out_prev is not None:
        in_specs.append(pl.BlockSpec(memory_space=pl.ANY))
        args.append(out_prev)
        aliases = {3: 0}
    return pl.pallas_call(
        _final_kernel, grid=(nblk,), in_specs=in_specs, out_specs=spec,
        out_shape=jax.ShapeDtypeStruct((total, D), jnp.float32),
        input_output_aliases=aliases,
        compiler_params=pltpu.CompilerParams(dimension_semantics=("parallel",)),
        name="final_norm",
    )(*args)


def _tree_sum(vals):
    while len(vals) > 1:
        nxt = [vals[i] + vals[i + 1] for i in range(0, len(vals) - 1, 2)]
        if len(vals) % 2:
            nxt.append(vals[-1])
        vals = nxt
    return vals[0]


def _sc_block_pipeline(nblk, items_per_token, loads, store, gather, compute):
    assert items_per_token % 2 == 0 and nblk >= 1

    for c in loads(0, 0):
        c.start()
    for c in loads(0, 0):
        c.wait()
    if nblk > 1:
        for c in loads(1, 1):
            c.start()
    gather(0, 0, 0, 0).start()

    @pl.loop(0, nblk)
    def _(b):
        slot = b % 2

        @pl.when(b >= 2)
        def _():
            store(b - 2, slot).wait()

        @pl.loop(0, SC_TOKENS)
        def _(t):
            for q in range(items_per_token):
                buf = q % 2
                if q + 1 < items_per_token:
                    gather(slot, t, q + 1, 1 - buf).start()
                else:
                    @pl.when(t + 1 < SC_TOKENS)
                    def _():
                        gather(slot, t + 1, 0, 1 - buf).start()

                    @pl.when(jnp.logical_and(t + 1 == SC_TOKENS, b + 1 < nblk))
                    def _():
                        for c in loads(b + 1, 1 - slot):
                            c.wait()
                        gather(1 - slot, 0, 0, 1 - buf).start()

                gather(slot, t, q, buf).wait()
                compute(slot, t, q, buf)

        store(b, slot).start()

        @pl.when(b + 2 < nblk)
        def _():
            for c in loads(b + 2, slot):
                c.start()

    if nblk >= 2:
        store(nblk - 2, nblk % 2).wait()
    store(nblk - 1, (nblk - 1) % 2).wait()


def _sc_mesh():
    return plsc.VectorSubcoreMesh(core_axis_name="c", subcore_axis_name="s")


def _sc_worker_id():
    return lax.axis_index("s") * V7X_SC_CORES + lax.axis_index("c")


def _sc_bf16(words):
    return plsc.bitcast(words, jnp.bfloat16)


def _sc_halves_f32(pairs):
    words = plsc.bitcast(pairs, jnp.uint32)
    return (plsc.bitcast(words << 16, jnp.float32),
            plsc.bitcast(words & jnp.uint32(HI_HALF), jnp.float32))


def _mix_parts(T, K, DW):
    D = 2 * DW
    L, G = V7X_SC_LANES, SC_GATHER
    nj = SC_MIX_CHUNK // L
    tok_per_w = T // V7X_SC_WORKERS
    assert T % (V7X_SC_WORKERS * SC_TOKENS) == 0 and K % (2 * G) == 0
    assert DW % SC_MIX_CHUNK == 0 and G % SC_MIX_GROUP == 0
    scratch = [
        pltpu.VMEM((2, SC_TOKENS, K), jnp.int32),
        pltpu.VMEM((2, SC_TOKENS, K), jnp.int32),
        pltpu.VMEM((2, G, DW), jnp.uint32),
        pltpu.VMEM((2, SC_TOKENS, D), jnp.float32),
        pltpu.SemaphoreType.DMA((2,)),
        pltpu.SemaphoreType.DMA((2,)),
        pltpu.SemaphoreType.DMA((2,)),
        pltpu.SemaphoreType.DMA((2,)),
    ]

    def run(w_hbm, idx_hbm, tab_hbm, out_hbm, idx_v, w_v, rows_v, out_v, idx_sems, w_sems, out_sems, row_sems):
        base = _sc_worker_id() * tok_per_w

        def loads(b, slot):
            toks = pl.ds(base + b * SC_TOKENS, SC_TOKENS)
            return [pltpu.make_async_copy(idx_hbm.at[toks], idx_v.at[slot], idx_sems.at[slot]),
                    pltpu.make_async_copy(w_hbm.at[toks], w_v.at[slot], w_sems.at[slot])]

        def store(b, slot):
            toks = pl.ds(base + b * SC_TOKENS, SC_TOKENS)
            return pltpu.make_async_copy(out_v.at[slot], out_hbm.at[toks], out_sems.at[slot])

        def gather(slot, t, q, buf):
            return pltpu.make_async_copy(
                tab_hbm.at[idx_v.at[slot, t, pl.ds(q * G, G)]], rows_v.at[buf], row_sems.at[buf])

        def compute(slot, t, q, buf):
            ssplat = jnp.full((L,), slot, jnp.int32)
            tsplat = jnp.full((L,), t, jnp.int32)
            for c in range(DW // SC_MIX_CHUNK):
                def body(kg, acc):
                    kk = kg * SC_MIX_GROUP
                    wks = [_sc_bf16(plsc.load_gather(
                        w_v, [ssplat, tsplat, jnp.full((L,), q * G + i, jnp.int32) + kk]))
                        for i in range(SC_MIX_GROUP)]
                    out = []
                    for j in range(nj):
                        prods = [wks[i] * _sc_bf16(rows_v[buf, kk + i, pl.ds(c * SC_MIX_CHUNK + j * L, L)])
                                 for i in range(SC_MIX_GROUP)]
                        lo, hi = _sc_halves_f32(_tree_sum(prods))
                        out += [acc[2 * j] + lo, acc[2 * j + 1] + hi]
                    return tuple(out)

                zero = jnp.zeros((L,), jnp.float32)
                acc = plsc.parallel_loop(0, G // SC_MIX_GROUP, carry=(zero,) * (2 * nj))(body)
                for j in range(nj):
                    for half in range(2):
                        dst = out_v.at[slot, t, pl.ds(half * DW + c * SC_MIX_CHUNK + j * L, L)]
                        if q == 0:
                            dst[...] = acc[2 * j + half]
                        else:
                            plsc.addupdate(dst, acc[2 * j + half])

        _sc_block_pipeline(tok_per_w // SC_TOKENS, K // G, loads, store, gather, compute)

    return scratch, run


def _sc_kernel(out_type, scratch_types, name):
    return functools.partial(
        pl.kernel, mesh=_sc_mesh(), out_type=out_type, scratch_types=scratch_types,
        compiler_params=pltpu.CompilerParams(needs_layout_passes=False), name=name)


def _pack_table(a):
    n, c = a.shape
    half = c // 2
    L, rb = V7X_SC_LANES, SC_PACK_ROWS
    rows_per_w = n // V7X_SC_WORKERS
    nblk = rows_per_w // rb
    assert n % (V7X_SC_WORKERS * rb) == 0 and half % L == 0

    @_sc_kernel(jax.ShapeDtypeStruct((n, half), jnp.uint32),
                [pltpu.VMEM((2, rb, c), jnp.float32), pltpu.VMEM((2, rb, half), jnp.uint32),
                 pltpu.SemaphoreType.DMA((2,)), pltpu.SemaphoreType.DMA((2,))], "pack_table")
    def k(a_hbm, o_hbm, a_v, o_v, in_sems, out_sems):
        base = _sc_worker_id() * rows_per_w

        def load(b, slot):
            return pltpu.make_async_copy(a_hbm.at[pl.ds(base + b * rb, rb)], a_v.at[slot], in_sems.at[slot])

        def store(b, slot):
            return pltpu.make_async_copy(o_v.at[slot], o_hbm.at[pl.ds(base + b * rb, rb)], out_sems.at[slot])

        load(0, 0).start()

        @pl.loop(0, nblk)
        def _(b):
            slot = b % 2
            load(b, slot).wait()

            @pl.when(b + 1 < nblk)
            def _():
                load(b + 1, 1 - slot).start()

            @pl.when(b >= 2)
            def _():
                store(b - 2, slot).wait()

            @pl.loop(0, rb)
            def _(r):
                for j in range(half // L):
                    pairs = plsc.pack(a_v[slot, r, pl.ds(j * L, L)], a_v[slot, r, pl.ds(half + j * L, L)],
                                      format=plsc.PackFormat.INTERLEAVED)
                    o_v[slot, r, pl.ds(j * L, L)] = plsc.bitcast(pairs, jnp.uint32)

            store(b, slot).start()

        if nblk >= 2:
            store(nblk - 2, nblk % 2).wait()
        store(nblk - 1, (nblk - 1) % 2).wait()

    return k(a)


def _expert_mix(w, idx, table):
    T, K = w.shape
    DW = table.shape[1]
    scratch, run = _mix_parts(T, K, DW)

    @_sc_kernel(jax.ShapeDtypeStruct((T, 2 * DW), jnp.float32), scratch, "expert_mix")
    def k(w_hbm, idx_hbm, tab_hbm, out_hbm, *mix_scratch):
        run(w_hbm, idx_hbm, tab_hbm, out_hbm, *mix_scratch)

    return k(w, idx, table)


def _chunk_sizes(total):
    ramp, size = [EDGE_CHUNK], EDGE_CHUNK
    while size < MAX_CHUNK:
        ramp.append(size)
        size *= 2
    middle = total - 2 * sum(ramp)
    assert middle >= 0 and middle % MAX_CHUNK == 0
    return ramp + [MAX_CHUNK] * (middle // MAX_CHUNK) + ramp[::-1]


def kernel(x, norm_mix, w_in, pool_w, pool_scale, sgu_ln_g, sgu_ln_b, sgu_w, sgu_b, out_norm_pool,
           out_norm_sgu, w_out, norm_ffn, peer_wq, peer_keys, peer_u, peer_v, norm_final):
    B, S, D = x.shape
    assert norm_mix.shape[0] == 1, "single-layer block"
    T = B * S
    mix_args = (norm_mix[0], w_in[0], pool_w[0], pool_scale[0], sgu_ln_g[0], sgu_ln_b[0],
                sgu_w[0], sgu_b[0], out_norm_pool[0], out_norm_sgu[0], w_out[0])
    x1_parts = [(b * S, _mixer(x, b, 1, *mix_args).reshape(S, D)) for b in range(B)]
    wq = peer_wq[0].astype(jnp.bfloat16)
    keys = peer_keys[0].astype(jnp.bfloat16)
    u_t = _transposed_bf16(peer_u[0])
    v_tab = _pack_table(peer_v[0])
    out = None
    tok0 = 0
    peers = [norm_ffn, v_tab]
    for tc in _chunk_sizes(T):
        part0, x1 = [p for p in x1_parts if p[0] <= tok0][-1]
        assert tok0 + tc <= part0 + x1.shape[0], "a token chunk must lie inside one mixer call"
        h2, idx, gate = _router(x1, norm_ffn[0], wq, keys, tok0 - part0, tc, peers[-2])
        w = _expert_weights(h2, u_t, idx, gate)
        peer = _expert_mix(w, idx, v_tab)
        peers.append(peer)
        out = _final(x1, tok0 - part0, peer, norm_final, out, tok0, T)
        tok0 += tc
    return out.reshape(B, S, D)
```

```python
import functools
import math

import jax
import jax.numpy as jnp
from jax import lax
from jax.experimental import pallas as pl
from jax.experimental.pallas import tpu as pltpu
from jax.experimental.pallas import tpu_sc as plsc

POOL_WINDOWS = (2, 4, 8, 16)
N_POOL_GROUPS = len(POOL_WINDOWS)
SGU_HEADS = 4
SGU_CHUNK = 128
PEER_HEADS = 8
PEER_N_KEYS = 128
PEER_D_HALF = 128
PEER_TOPK = 16
NORM_EPS = 1e-6
EXPERTS_PER_TOKEN = PEER_HEADS * PEER_TOPK

V7X_LANES = 128
V7X_SUBLANES = 8
V7X_SC_CORES = 2
V7X_SC_SUBCORES = 16
V7X_SC_LANES = 16
V7X_SC_WORKERS = V7X_SC_CORES * V7X_SC_SUBCORES

HALO = max(POOL_WINDOWS)
MIX_TILE = 512
FIRST_MIX_TOKENS = 1024
ROUTE_TILE = 1024
EW_TILE = 512
SCORE_TOK_TILE = 2048
SCORE_EXP_TILE = 2048
SC_PACK_ROWS = 16
SC_GATHER = 64
SC_TOKENS = 8
SC_MIX_CHUNK = 128
SC_MIX_GROUP = 4
HI_HALF = 0xFFFF0000
V7X_VMEM_BYTES = 64 * 1024 * 1024
TC_VMEM_LIMIT = V7X_VMEM_BYTES * 3 // 4
EDGE_CHUNK = 512
MAX_CHUNK = 2048


def _rms(x, g):
    inv = lax.rsqrt(jnp.mean(x * x, axis=-1, keepdims=True) + NORM_EPS)
    return x * inv * g


def _transpose_cast_kernel(a_ref, o_ref):
    o_ref[...] = a_ref[...].T.astype(o_ref.dtype)


def _transposed_bf16(a):
    n, c = a.shape
    rows = min(SCORE_EXP_TILE, n)
    return pl.pallas_call(
        _transpose_cast_kernel, grid=(n // rows,),
        in_specs=[pl.BlockSpec((rows, c), lambda i: (i, 0))],
        out_specs=pl.BlockSpec((c, rows), lambda i: (0, i)),
        out_shape=jax.ShapeDtypeStruct((c, n), jnp.bfloat16),
        compiler_params=pltpu.CompilerParams(dimension_semantics=("parallel",)),
        name="transpose_table",
    )(a)


def _gelu(x):
    return 0.5 * x * (1.0 + lax.erf(x * math.sqrt(0.5)))


def _mixer_kernel(first_tile, x_ref, xh_ref, nmix_ref, win_ref, poolw_ref, pscale_ref, lng_ref, lnb_ref,
                  sguw_ref, sgub_ref, onp_ref, ons_ref, wout_ref, o_ref, pext_ref, mix_ref):
    i = first_tile + pl.program_id(1)
    ts = x_ref.shape[1]
    pool_w = pscale_ref.shape[1]
    gdim = pool_w // N_POOL_GROUPS
    sgu_w = lng_ref.shape[1]
    hdim = sgu_w // SGU_HEADS

    x = x_ref[0]
    h = _rms(x, nmix_ref[...]).astype(jnp.bfloat16)
    z = jnp.dot(h, win_ref[...], preferred_element_type=jnp.float32)
    p = z[:, :pool_w]

    hh = _rms(xh_ref[0], nmix_ref[...]).astype(jnp.bfloat16)
    ph = jnp.dot(hh, win_ref[:, :pool_w], preferred_element_type=jnp.float32)
    ph = jnp.where(i > 0, ph, 0.0)
    pext_ref[0:HALO, :] = ph
    pext_ref[HALO:HALO + ts, :] = p

    pos = i * ts + lax.broadcasted_iota(jnp.int32, (ts, 1), 0)
    ssq = jnp.zeros((ts, 1), jnp.float32)
    a_parts = []
    for g, win in enumerate(POOL_WINDOWS):
        cols = slice(g * gdim, (g + 1) * gdim)
        s = pext_ref[HALO:HALO + ts, cols]
        for j in range(1, win):
            s = s + pext_ref[HALO - j:HALO - j + ts, cols]
        cnt = jnp.minimum(pos + 1, win).astype(jnp.float32)
        d = (s / cnt - p[:, cols]).astype(jnp.bfloat16)
        a = jnp.dot(d, poolw_ref[g], preferred_element_type=jnp.float32) * pscale_ref[:, cols]
        ssq = ssq + jnp.sum(a * a, axis=-1, keepdims=True)
        a_parts.append(a)
    inv_a = lax.rsqrt(ssq / pool_w + NORM_EPS)
    for g in range(N_POOL_GROUPS):
        cols = slice(g * gdim, (g + 1) * gdim)
        mix_ref[:, cols] = (a_parts[g] * inv_a * onp_ref[:, cols]).astype(jnp.bfloat16)

    gz = _gelu(z[:, pool_w:])
    tril = (lax.broadcasted_iota(jnp.int32, (SGU_CHUNK, SGU_CHUNK), 0)
            >= lax.broadcasted_iota(jnp.int32, (SGU_CHUNK, SGU_CHUNK), 1))
    ssq = jnp.zeros((ts, 1), jnp.float32)
    b_parts = []
    for hd in range(SGU_HEADS):
        cols = slice(hd * hdim, (hd + 1) * hdim)
        u = gz[:, hd * hdim:(hd + 1) * hdim]
        v = gz[:, sgu_w + hd * hdim:sgu_w + (hd + 1) * hdim]
        mu = jnp.mean(v, axis=-1, keepdims=True)
        vc = v - mu
        var = jnp.mean(vc * vc, axis=-1, keepdims=True)
        vn = (vc * lax.rsqrt(var + NORM_EPS) * lng_ref[:, cols] + lnb_ref[:, cols]).astype(jnp.bfloat16)
        w = jnp.where(tril, sguw_ref[hd], jnp.zeros((), sguw_ref.dtype))
        mixed = [jnp.dot(w, vn[n * SGU_CHUNK:(n + 1) * SGU_CHUNK], preferred_element_type=jnp.float32)
                 + sgub_ref[hd] for n in range(ts // SGU_CHUNK)]
        b = u * jnp.concatenate(mixed, axis=0)
        ssq = ssq + jnp.sum(b * b, axis=-1, keepdims=True)
        b_parts.append(b)
    inv_b = lax.rsqrt(ssq / sgu_w + NORM_EPS)
    for hd in range(SGU_HEADS):
        cols = slice(hd * hdim, (hd + 1) * hdim)
        mix_ref[:, pool_w + hd * hdim:pool_w + (hd + 1) * hdim] = (
            b_parts[hd] * inv_b * ons_ref[:, cols]).astype(jnp.bfloat16)

    o_ref[0] = x + jnp.dot(mix_ref[...], wout_ref[...], preferred_element_type=jnp.float32)


def _mixer(x, b0, s0, ns, norm_mix, w_in, pool_w, pool_scale, ln_g, ln_b, sgu_w, sgu_b, on_pool, on_sgu, w_out):
    _, S, D = x.shape
    ts = min(MIX_TILE, ns)
    nb = 1
    assert s0 % ts == 0 and ns % ts == 0 and s0 + ns <= S
    tile0 = s0 // ts
    pool_width = pool_scale.size
    sgu_width = ln_g.size
    in_width = w_in.shape[1]
    gdim = pool_width // N_POOL_GROUPS
    halo_blocks = ts // HALO
    full = lambda shape: pl.BlockSpec(shape, lambda b, i: (0,) * len(shape))
    return pl.pallas_call(
        functools.partial(_mixer_kernel, tile0),
        grid=(nb, ns // ts),
        in_specs=[
            pl.BlockSpec((1, ts, D), lambda b, i: (b0 + b, tile0 + i, 0)),
            pl.BlockSpec((1, HALO, D),
                         lambda b, i: (b0 + b, jnp.maximum((tile0 + i) * halo_blocks - 1, 0), 0)),
            full((1, D)),
            full((D, in_width)),
            full((N_POOL_GROUPS, gdim, gdim)),
            full((1, pool_width)),
            full((1, sgu_width)),
            full((1, sgu_width)),
            full((SGU_HEADS, SGU_CHUNK, SGU_CHUNK)),
            full((SGU_HEADS, SGU_CHUNK, SGU_CHUNK)),
            full((1, pool_width)),
            full((1, sgu_width)),
            full((pool_width + sgu_width, D)),
        ],
        out_specs=pl.BlockSpec((1, ts, D), lambda b, i: (b, i, 0)),
        out_shape=jax.ShapeDtypeStruct((nb, ns, D), jnp.float32),
        scratch_shapes=[
            pltpu.VMEM((HALO + ts, pool_width), jnp.float32),
            pltpu.VMEM((ts, pool_width + sgu_width), jnp.bfloat16),
        ],
        compiler_params=pltpu.CompilerParams(
            dimension_semantics=("parallel", "arbitrary"), vmem_limit_bytes=TC_VMEM_LIMIT),
        name="mixer",
    )(x, x, norm_mix.reshape(1, D), w_in.astype(jnp.bfloat16), pool_w.astype(jnp.bfloat16),
      pool_scale.reshape(1, pool_width), ln_g.reshape(1, sgu_width), ln_b.reshape(1, sgu_width),
      sgu_w.astype(jnp.bfloat16),
      jnp.broadcast_to(sgu_b[:, :, None], (SGU_HEADS, SGU_CHUNK, SGU_CHUNK)),
      on_pool.reshape(1, pool_width), on_sgu.reshape(1, sgu_width), w_out.astype(jnp.bfloat16))


def _topk_rows(s, k):
    n = s.shape[0]
    iota = lax.broadcasted_iota(jnp.int32, s.shape, 0)
    vals, idxs = [], []
    for _ in range(k):
        m = jnp.max(s, axis=0, keepdims=True)
        ix = jnp.min(jnp.where(s == m, iota, n), axis=0, keepdims=True)
        vals.append(m)
        idxs.append(ix)
        s = jnp.where(iota == ix, -jnp.inf, s)
    return vals, idxs


def _pair_candidates(v1, i1, v2, i2):
    k = PEER_TOPK
    v2c, i2c = jnp.concatenate(v2, axis=0), jnp.concatenate(i2, axis=0)
    m = v2c.shape[1]
    vals, experts, flats = [], [], []
    a = 0
    while k // (a + 1) > 1:
        rows = -(-(k // (a + 1)) // V7X_SUBLANES) * V7X_SUBLANES
        vals.append(v1[a] + v2c[:rows])
        experts.append(i1[a] * PEER_N_KEYS + i2c[:rows])
        flats.append(a * k + lax.broadcasted_iota(jnp.int32, (rows, m), 0))
        a += 1
    vals.append(jnp.concatenate(v1[a:], axis=0) + v2[0])
    experts.append(jnp.concatenate(i1[a:], axis=0) * PEER_N_KEYS + i2[0])
    flats.append((a + lax.broadcasted_iota(jnp.int32, (k - a, m), 0)) * k)
    return jnp.concatenate(vals, axis=0), jnp.concatenate(experts, axis=0), jnp.concatenate(flats, axis=0)


def _router_kernel(x_ref, nffn_ref, wq_ref, keys_ref, after_ref, h2_ref, idx_ref, gate_ref,
                   q_ref, idxt_ref, gatet_ref):
    del after_ref
    h2 = _rms(x_ref[...], nffn_ref[...]).astype(jnp.bfloat16)
    h2_ref[...] = h2
    q_ref[...] = jnp.dot(h2, wq_ref[...], preferred_element_type=jnp.float32).astype(jnp.bfloat16)
    dq = 2 * PEER_D_HALF
    nt = (((1,), (1,)), ((), ()))

    def head(hd, carry):
        off = pl.multiple_of(hd * dq, dq)
        s1 = lax.dot_general(keys_ref[0], q_ref[:, pl.ds(off, PEER_D_HALF)], nt,
                             preferred_element_type=jnp.float32)
        s2 = lax.dot_general(keys_ref[1], q_ref[:, pl.ds(off + PEER_D_HALF, PEER_D_HALF)], nt,
                             preferred_element_type=jnp.float32)
        v1, i1 = _topk_rows(s1, PEER_TOPK)
        v2, i2 = _topk_rows(s2, PEER_TOPK)
        cand, expert, flat = _pair_candidates(v1, i1, v2, i2)
        cv, ce = [], []
        for _ in range(PEER_TOPK):
            m = jnp.max(cand, axis=0, keepdims=True)
            ix = jnp.min(jnp.where(cand == m, flat, PEER_TOPK * PEER_TOPK), axis=0, keepdims=True)
            hit = flat == ix
            cv.append(m)
            ce.append(jnp.max(jnp.where(hit, expert, -1), axis=0, keepdims=True))
            cand = jnp.where(hit, -jnp.inf, cand)
        cvc = jnp.concatenate(cv, axis=0)
        e = jnp.exp(cvc - cv[0])
        gate = e / jnp.sum(e, axis=0, keepdims=True)
        row = pl.multiple_of(hd * PEER_TOPK, PEER_TOPK)
        idxt_ref[pl.ds(row, PEER_TOPK), :] = jnp.concatenate(ce, axis=0)
        gatet_ref[pl.ds(row, PEER_TOPK), :] = gate
        return carry

    lax.fori_loop(0, PEER_HEADS, head, 0)
    idx_ref[...] = idxt_ref[...].T
    gate_ref[...] = gatet_ref[...].T


def _router(x1, norm_ffn, wq, keys, tok0, T, after):
    D = x1.shape[1]
    tr = min(ROUTE_TILE, T)
    qw = wq.shape[1]
    assert tok0 % tr == 0 and T % tr == 0
    first = tok0 // tr
    full = lambda shape: pl.BlockSpec(shape, lambda i: (0,) * len(shape))
    return pl.pallas_call(
        _router_kernel,
        grid=(T // tr,),
        in_specs=[
            pl.BlockSpec((tr, D), lambda i: (first + i, 0)),
            full((1, D)),
            full((D, qw)),
            full((2, PEER_N_KEYS, PEER_D_HALF)),
            pl.BlockSpec(memory_space=pl.ANY),
        ],
        out_specs=[
            pl.BlockSpec((tr, D), lambda i: (i, 0)),
            pl.BlockSpec((tr, EXPERTS_PER_TOKEN), lambda i: (i, 0)),
            pl.BlockSpec((tr, EXPERTS_PER_TOKEN), lambda i: (i, 0)),
        ],
        out_shape=[
            jax.ShapeDtypeStruct((T, D), jnp.bfloat16),
            jax.ShapeDtypeStruct((T, EXPERTS_PER_TOKEN), jnp.int32),
            jax.ShapeDtypeStruct((T, EXPERTS_PER_TOKEN), jnp.float32),
        ],
        scratch_shapes=[
            pltpu.VMEM((tr, qw), jnp.bfloat16),
            pltpu.VMEM((EXPERTS_PER_TOKEN, tr), jnp.int32),
            pltpu.VMEM((EXPERTS_PER_TOKEN, tr), jnp.float32),
        ],
        compiler_params=pltpu.CompilerParams(
            dimension_semantics=("parallel",), vmem_limit_bytes=TC_VMEM_LIMIT),
        name="router",
    )(x1, norm_ffn.reshape(1, D), wq, keys, after)


def _scores_kernel(h_ref, ut_ref, idx_ref, gate_ref, w_ref, pre_ref):
    j = pl.program_id(1)
    acc = jnp.dot(h_ref[...], ut_ref[...], preferred_element_type=jnp.float32)
    groups = acc.shape[1] // V7X_LANES
    lane_bits = int(math.log2(V7X_LANES))
    idx = idx_ref[...]
    group, lane = idx >> lane_bits, idx & (V7X_LANES - 1)
    picked = jnp.zeros(idx.shape, jnp.float32)
    for n in range(groups):
        vals = jnp.take_along_axis(acc[:, n * V7X_LANES:(n + 1) * V7X_LANES], lane, axis=1)
        picked = jnp.where(group == j * groups + n, vals, picked)

    @pl.when(j == 0)
    def _():
        pre_ref[...] = picked

    @pl.when(j > 0)
    def _():
        pre_ref[...] += picked

    @pl.when(j == pl.num_programs(1) - 1)
    def _():
        w = (gate_ref[...] * _gelu(pre_ref[...])).astype(jnp.bfloat16).astype(jnp.float32)
        hi = pltpu.bitcast(w, jnp.uint32) & jnp.uint32(HI_HALF)
        w_ref[...] = pltpu.bitcast(hi | (hi >> 16), jnp.int32)


def _expert_weights(h2, u_t, idx, gate):
    T, D = h2.shape
    E = u_t.shape[1]
    K = idx.shape[1]
    tm, tn = min(SCORE_TOK_TILE, T), min(SCORE_EXP_TILE, E)
    assert T % tm == 0 and E % tn == 0 and tn % V7X_LANES == 0 and K == V7X_LANES
    per_token = pl.BlockSpec((tm, K), lambda i, j: (i, 0))
    return pl.pallas_call(
        _scores_kernel,
        grid=(T // tm, E // tn),
        in_specs=[pl.BlockSpec((tm, D), lambda i, j: (i, 0)), pl.BlockSpec((D, tn), lambda i, j: (0, j)),
                  per_token, per_token],
        out_specs=per_token,
        out_shape=jax.ShapeDtypeStruct((T, K), jnp.int32),
        scratch_shapes=[pltpu.VMEM((tm, K), jnp.float32)],
        compiler_params=pltpu.CompilerParams(
            dimension_semantics=("parallel", "arbitrary"), vmem_limit_bytes=TC_VMEM_LIMIT),
        name="expert_weights",
    )(h2, u_t, idx, gate)


def _final_kernel(x_ref, y_ref, g_ref, *rest):
    o_ref = rest[-1]
    o_ref[...] = _rms(x_ref[...] + y_ref[...], g_ref[...])


def _final(x1, x_tok0, peer, norm_final, out_prev, out_tok0, total):
    Tc, D = peer.shape
    te = min(EW_TILE, Tc)
    assert x_tok0 % te == 0 and out_tok0 % te == 0 and Tc % te == 0
    nblk = Tc // te
    spec = pl.BlockSpec((te, D), lambda i: (out_tok0 // te + i, 0))
    in_specs = [pl.BlockSpec((te, D), lambda i: (x_tok0 // te + i, 0)),
                pl.BlockSpec((te, D), lambda i: (i, 0)), pl.BlockSpec((1, D), lambda i: (0, 0))]
    args = [x1, peer, norm_final.reshape(1, D)]
    aliases = {}
    if out_prev is not None:
        in_specs.append(pl.BlockSpec(memory_space=pl.ANY))
        args.append(out_prev)
        aliases = {3: 0}
    return pl.pallas_call(
        _final_kernel, grid=(nblk,), in_specs=in_specs, out_specs=spec,
        out_shape=jax.ShapeDtypeStruct((total, D), jnp.float32),
        input_output_aliases=aliases,
        compiler_params=pltpu.CompilerParams(dimension_semantics=("parallel",)),
        name="final_norm",
    )(*args)


def _tree_sum(vals):
    while len(vals) > 1:
        nxt = [vals[i] + vals[i + 1] for i in range(0, len(vals) - 1, 2)]
        if len(vals) % 2:
            nxt.append(vals[-1])
        vals = nxt
    return vals[0]


def _sc_block_pipeline(nblk, items_per_token, loads, store, gather, compute):
    assert items_per_token % 2 == 0 and nblk >= 1

    for c in loads(0, 0):
        c.start()
    for c in loads(0, 0):
        c.wait()
    if nblk > 1:
        for c in loads(1, 1):
            c.start()
    gather(0, 0, 0, 0).start()

    @pl.loop(0, nblk)
    def _(b):
        slot = b % 2

        @pl.when(b >= 2)
        def _():
            store(b - 2, slot).wait()

        @pl.loop(0, SC_TOKENS)
        def _(t):
            for q in range(items_per_token):
                buf = q % 2
                if q + 1 < items_per_token:
                    gather(slot, t, q + 1, 1 - buf).start()
                else:
                    @pl.when(t + 1 < SC_TOKENS)
                    def _():
                        gather(slot, t + 1, 0, 1 - buf).start()

                    @pl.when(jnp.logical_and(t + 1 == SC_TOKENS, b + 1 < nblk))
                    def _():
                        for c in loads(b + 1, 1 - slot):
                            c.wait()
                        gather(1 - slot, 0, 0, 1 - buf).start()

                gather(slot, t, q, buf).wait()
                compute(slot, t, q, buf)

        store(b, slot).start()

        @pl.when(b + 2 < nblk)
        def _():
            for c in loads(b + 2, slot):
                c.start()

    if nblk >= 2:
        store(nblk - 2, nblk % 2).wait()
    store(nblk - 1, (nblk - 1) % 2).wait()


def _sc_mesh():
    return plsc.VectorSubcoreMesh(core_axis_name="c", subcore_axis_name="s")


def _sc_worker_id():
    return lax.axis_index("s") * V7X_SC_CORES + lax.axis_index("c")


def _sc_bf16(words):
    return plsc.bitcast(words, jnp.bfloat16)


def _sc_halves_f32(pairs):
    words = plsc.bitcast(pairs, jnp.uint32)
    return (plsc.bitcast(words << 16, jnp.float32),
            plsc.bitcast(words & jnp.uint32(HI_HALF), jnp.float32))


def _mix_parts(T, K, DW):
    D = 2 * DW
    L, G = V7X_SC_LANES, SC_GATHER
    nj = SC_MIX_CHUNK // L
    tok_per_w = T // V7X_SC_WORKERS
    assert T % (V7X_SC_WORKERS * SC_TOKENS) == 0 and K % (2 * G) == 0
    assert DW % SC_MIX_CHUNK == 0 and G % SC_MIX_GROUP == 0
    scratch = [
        pltpu.VMEM((2, SC_TOKENS, K), jnp.int32),
        pltpu.VMEM((2, SC_TOKENS, K), jnp.int32),
        pltpu.VMEM((2, G, DW), jnp.uint32),
        pltpu.VMEM((2, SC_TOKENS, D), jnp.float32),
        pltpu.SemaphoreType.DMA((2,)),
        pltpu.SemaphoreType.DMA((2,)),
        pltpu.SemaphoreType.DMA((2,)),
        pltpu.SemaphoreType.DMA((2,)),
    ]

    def run(w_hbm, idx_hbm, tab_hbm, out_hbm, idx_v, w_v, rows_v, out_v, idx_sems, w_sems, out_sems, row_sems):
        base = _sc_worker_id() * tok_per_w

        def loads(b, slot):
            toks = pl.ds(base + b * SC_TOKENS, SC_TOKENS)
            return [pltpu.make_async_copy(idx_hbm.at[toks], idx_v.at[slot], idx_sems.at[slot]),
                    pltpu.make_async_copy(w_hbm.at[toks], w_v.at[slot], w_sems.at[slot])]

        def store(b, slot):
            toks = pl.ds(base + b * SC_TOKENS, SC_TOKENS)
            return pltpu.make_async_copy(out_v.at[slot], out_hbm.at[toks], out_sems.at[slot])

        def gather(slot, t, q, buf):
            return pltpu.make_async_copy(
                tab_hbm.at[idx_v.at[slot, t, pl.ds(q * G, G)]], rows_v.at[buf], row_sems.at[buf])

        def compute(slot, t, q, buf):
            ssplat = jnp.full((L,), slot, jnp.int32)
            tsplat = jnp.full((L,), t, jnp.int32)
            for c in range(DW // SC_MIX_CHUNK):
                def body(kg, acc):
                    kk = kg * SC_MIX_GROUP
                    wks = [_sc_bf16(plsc.load_gather(
                        w_v, [ssplat, tsplat, jnp.full((L,), q * G + i, jnp.int32) + kk]))
                        for i in range(SC_MIX_GROUP)]
                    out = []
                    for j in range(nj):
                        prods = [wks[i] * _sc_bf16(rows_v[buf, kk + i, pl.ds(c * SC_MIX_CHUNK + j * L, L)])
                                 for i in range(SC_MIX_GROUP)]
                        lo, hi = _sc_halves_f32(_tree_sum(prods))
                        out += [acc[2 * j] + lo, acc[2 * j + 1] + hi]
                    return tuple(out)

                zero = jnp.zeros((L,), jnp.float32)
                acc = plsc.parallel_loop(0, G // SC_MIX_GROUP, carry=(zero,) * (2 * nj))(body)
                for j in range(nj):
                    for half in range(2):
                        dst = out_v.at[slot, t, pl.ds(half * DW + c * SC_MIX_CHUNK + j * L, L)]
                        if q == 0:
                            dst[...] = acc[2 * j + half]
                        else:
                            plsc.addupdate(dst, acc[2 * j + half])

        _sc_block_pipeline(tok_per_w // SC_TOKENS, K // G, loads, store, gather, compute)

    return scratch, run


def _sc_kernel(out_type, scratch_types, name):
    return functools.partial(
        pl.kernel, mesh=_sc_mesh(), out_type=out_type, scratch_types=scratch_types,
        compiler_params=pltpu.CompilerParams(needs_layout_passes=False), name=name)


def _pack_table(a):
    n, c = a.shape
    half = c // 2
    L, rb = V7X_SC_LANES, SC_PACK_ROWS
    rows_per_w = n // V7X_SC_WORKERS
    nblk = rows_per_w // rb
    assert n % (V7X_SC_WORKERS * rb) == 0 and half % L == 0

    @_sc_kernel(jax.ShapeDtypeStruct((n, half), jnp.uint32),
                [pltpu.VMEM((2, rb, c), jnp.float32), pltpu.VMEM((2, rb, half), jnp.uint32),
                 pltpu.SemaphoreType.DMA((2,)), pltpu.SemaphoreType.DMA((2,))], "pack_table")
    def k(a_hbm, o_hbm, a_v, o_v, in_sems, out_sems):
        base = _sc_worker_id() * rows_per_w

        def load(b, slot):
            return pltpu.make_async_copy(a_hbm.at[pl.ds(base + b * rb, rb)], a_v.at[slot], in_sems.at[slot])

        def store(b, slot):
            return pltpu.make_async_copy(o_v.at[slot], o_hbm.at[pl.ds(base + b * rb, rb)], out_sems.at[slot])

        load(0, 0).start()

        @pl.loop(0, nblk)
        def _(b):
            slot = b % 2
            load(b, slot).wait()

            @pl.when(b + 1 < nblk)
            def _():
                load(b + 1, 1 - slot).start()

            @pl.when(b >= 2)
            def _():
                store(b - 2, slot).wait()

            @pl.loop(0, rb)
            def _(r):
                for j in range(half // L):
                    pairs = plsc.pack(a_v[slot, r, pl.ds(j * L, L)], a_v[slot, r, pl.ds(half + j * L, L)],
                                      format=plsc.PackFormat.INTERLEAVED)
                    o_v[slot, r, pl.ds(j * L, L)] = plsc.bitcast(pairs, jnp.uint32)

            store(b, slot).start()

        if nblk >= 2:
            store(nblk - 2, nblk % 2).wait()
        store(nblk - 1, (nblk - 1) % 2).wait()

    return k(a)


def _expert_mix(w, idx, table):
    T, K = w.shape
    DW = table.shape[1]
    scratch, run = _mix_parts(T, K, DW)

    @_sc_kernel(jax.ShapeDtypeStruct((T, 2 * DW), jnp.float32), scratch, "expert_mix")
    def k(w_hbm, idx_hbm, tab_hbm, out_hbm, *mix_scratch):
        run(w_hbm, idx_hbm, tab_hbm, out_hbm, *mix_scratch)

    return k(w, idx, table)


def _chunk_sizes(total):
    ramp, size = [EDGE_CHUNK], EDGE_CHUNK
    while size < MAX_CHUNK:
        ramp.append(size)
        size *= 2
    middle = total - 2 * sum(ramp)
    assert middle >= 0 and middle % MAX_CHUNK == 0
    return ramp + [MAX_CHUNK] * (middle // MAX_CHUNK) + ramp[::-1]


def kernel(x, norm_mix, w_in, pool_w, pool_scale, sgu_ln_g, sgu_ln_b, sgu_w, sgu_b, out_norm_pool,
           out_norm_sgu, w_out, norm_ffn, peer_wq, peer_keys, peer_u, peer_v, norm_final):
    B, S, D = x.shape
    assert norm_mix.shape[0] == 1, "single-layer block"
    T = B * S
    mix_args = (norm_mix[0], w_in[0], pool_w[0], pool_scale[0], sgu_ln_g[0], sgu_ln_b[0],
                sgu_w[0], sgu_b[0], out_norm_pool[0], out_norm_sgu[0], w_out[0])
    head = min(FIRST_MIX_TOKENS, S)
    pieces = [(0, 0, head)] + ([(0, head, S - head)] if head < S else []) + [(b, 0, S) for b in range(1, B)]
    x1_parts = [(b * S + s0, _mixer(x, b, s0, ns, *mix_args).reshape(ns, D)) for b, s0, ns in pieces]
    wq = peer_wq[0].astype(jnp.bfloat16)
    keys = peer_keys[0].astype(jnp.bfloat16)
    u_t = _transposed_bf16(peer_u[0])
    v_tab = _pack_table(peer_v[0])
    out = None
    tok0 = 0
    peers = [norm_ffn, v_tab]
    for tc in _chunk_sizes(T):
        part0, x1 = [p for p in x1_parts if p[0] <= tok0][-1]
        assert tok0 + tc <= part0 + x1.shape[0], "a token chunk must lie inside one mixer call"
        h2, idx, gate = _router(x1, norm_ffn[0], wq, keys, tok0 - part0, tc, peers[-2])
        w = _expert_weights(h2, u_t, idx, gate)
        peer = _expert_mix(w, idx, v_tab)
        peers.append(peer)
        out = _final(x1, tok0 - part0, peer, norm_final, out, tok0, T)
        tok0 += tc
    return out.reshape(B, S, D)
```

```python
import functools
import math

import jax
import jax.numpy as jnp
from jax import lax
from jax.experimental import pallas as pl
from jax.experimental.pallas import tpu as pltpu
from jax.experimental.pallas import tpu_sc as plsc

POOL_WINDOWS = (2, 4, 8, 16)
N_POOL_GROUPS = len(POOL_WINDOWS)
SGU_HEADS = 4
SGU_CHUNK = 128
PEER_HEADS = 8
PEER_N_KEYS = 128
PEER_D_HALF = 128
PEER_TOPK = 16
NORM_EPS = 1e-6
EXPERTS_PER_TOKEN = PEER_HEADS * PEER_TOPK

V7X_LANES = 128
V7X_SUBLANES = 8
V7X_SC_CORES = 2
V7X_SC_SUBCORES = 16
V7X_SC_LANES = 16
V7X_SC_WORKERS = V7X_SC_CORES * V7X_SC_SUBCORES

HALO = max(POOL_WINDOWS)
MIX_TILE = 512
ROUTE_TILE = 1024
EW_TILE = 512
SCORE_TOK_TILE = 2048
SCORE_EXP_TILE = 2048
SC_PACK_ROWS = 16
SC_GATHER = 64
SC_TOKENS = 8
SC_MIX_CHUNK = 128
SC_MIX_GROUP = 4
HI_HALF = 0xFFFF0000
V7X_VMEM_BYTES = 64 * 1024 * 1024
TC_VMEM_LIMIT = V7X_VMEM_BYTES * 3 // 4
EDGE_CHUNK = 512
MAX_CHUNK = 2048


def _rms(x, g):
    inv = lax.rsqrt(jnp.mean(x * x, axis=-1, keepdims=True) + NORM_EPS)
    return x * inv * g


def _transpose_cast_kernel(a_ref, o_ref):
    o_ref[...] = a_ref[...].T.astype(o_ref.dtype)


def _transposed_bf16(a):
    n, c = a.shape
    rows = min(SCORE_EXP_TILE, n)
    return pl.pallas_call(
        _transpose_cast_kernel, grid=(n // rows,),
        in_specs=[pl.BlockSpec((rows, c), lambda i: (i, 0))],
        out_specs=pl.BlockSpec((c, rows), lambda i: (0, i)),
        out_shape=jax.ShapeDtypeStruct((c, n), jnp.bfloat16),
        compiler_params=pltpu.CompilerParams(dimension_semantics=("parallel",)),
        name="transpose_table",
    )(a)


def _gelu(x):
    return 0.5 * x * (1.0 + lax.erf(x * math.sqrt(0.5)))


def _mixer_kernel(x_ref, xh_ref, nmix_ref, win_ref, poolw_ref, pscale_ref, lng_ref, lnb_ref,
                  sguw_ref, sgub_ref, onp_ref, ons_ref, wout_ref, o_ref, pext_ref, mix_ref):
    i = pl.program_id(1)
    ts = x_ref.shape[1]
    pool_w = pscale_ref.shape[1]
    gdim = pool_w // N_POOL_GROUPS
    sgu_w = lng_ref.shape[1]
    hdim = sgu_w // SGU_HEADS

    x = x_ref[0]
    h = _rms(x, nmix_ref[...]).astype(jnp.bfloat16)
    z = jnp.dot(h, win_ref[...], preferred_element_type=jnp.float32)
    p = z[:, :pool_w]

    hh = _rms(xh_ref[0], nmix_ref[...]).astype(jnp.bfloat16)
    ph = jnp.dot(hh, win_ref[:, :pool_w], preferred_element_type=jnp.float32)
    ph = jnp.where(i > 0, ph, 0.0)
    pext_ref[0:HALO, :] = ph
    pext_ref[HALO:HALO + ts, :] = p

    pos = i * ts + lax.broadcasted_iota(jnp.int32, (ts, 1), 0)
    ssq = jnp.zeros((ts, 1), jnp.float32)
    a_parts = []
    for g, win in enumerate(POOL_WINDOWS):
        cols = slice(g * gdim, (g + 1) * gdim)
        s = pext_ref[HALO:HALO + ts, cols]
        for j in range(1, win):
            s = s + pext_ref[HALO - j:HALO - j + ts, cols]
        cnt = jnp.minimum(pos + 1, win).astype(jnp.float32)
        d = (s / cnt - p[:, cols]).astype(jnp.bfloat16)
        a = jnp.dot(d, poolw_ref[g], preferred_element_type=jnp.float32) * pscale_ref[:, cols]
        ssq = ssq + jnp.sum(a * a, axis=-1, keepdims=True)
        a_parts.append(a)
    inv_a = lax.rsqrt(ssq / pool_w + NORM_EPS)
    for g in range(N_POOL_GROUPS):
        cols = slice(g * gdim, (g + 1) * gdim)
        mix_ref[:, cols] = (a_parts[g] * inv_a * onp_ref[:, cols]).astype(jnp.bfloat16)

    gz = _gelu(z[:, pool_w:])
    tril = (lax.broadcasted_iota(jnp.int32, (SGU_CHUNK, SGU_CHUNK), 0)
            >= lax.broadcasted_iota(jnp.int32, (SGU_CHUNK, SGU_CHUNK), 1))
    ssq = jnp.zeros((ts, 1), jnp.float32)
    b_parts = []
    for hd in range(SGU_HEADS):
        cols = slice(hd * hdim, (hd + 1) * hdim)
        u = gz[:, hd * hdim:(hd + 1) * hdim]
        v = gz[:, sgu_w + hd * hdim:sgu_w + (hd + 1) * hdim]
        mu = jnp.mean(v, axis=-1, keepdims=True)
        vc = v - mu
        var = jnp.mean(vc * vc, axis=-1, keepdims=True)
        vn = (vc * lax.rsqrt(var + NORM_EPS) * lng_ref[:, cols] + lnb_ref[:, cols]).astype(jnp.bfloat16)
        w = jnp.where(tril, sguw_ref[hd], jnp.zeros((), sguw_ref.dtype))
        mixed = [jnp.dot(w, vn[n * SGU_CHUNK:(n + 1) * SGU_CHUNK], preferred_element_type=jnp.float32)
                 + sgub_ref[hd] for n in range(ts // SGU_CHUNK)]
        b = u * jnp.concatenate(mixed, axis=0)
        ssq = ssq + jnp.sum(b * b, axis=-1, keepdims=True)
        b_parts.append(b)
    inv_b = lax.rsqrt(ssq / sgu_w + NORM_EPS)
    for hd in range(SGU_HEADS):
        cols = slice(hd * hdim, (hd + 1) * hdim)
        mix_ref[:, pool_w + hd * hdim:pool_w + (hd + 1) * hdim] = (
            b_parts[hd] * inv_b * ons_ref[:, cols]).astype(jnp.bfloat16)

    o_ref[0] = x + jnp.dot(mix_ref[...], wout_ref[...], preferred_element_type=jnp.float32)


def _mixer(x, b0, nb, norm_mix, w_in, pool_w, pool_scale, ln_g, ln_b, sgu_w, sgu_b, on_pool, on_sgu, w_out):
    _, S, D = x.shape
    ts = min(MIX_TILE, S)
    pool_width = pool_scale.size
    sgu_width = ln_g.size
    in_width = w_in.shape[1]
    gdim = pool_width // N_POOL_GROUPS
    halo_blocks = ts // HALO
    full = lambda shape: pl.BlockSpec(shape, lambda b, i: (0,) * len(shape))
    return pl.pallas_call(
        _mixer_kernel,
        grid=(nb, S // ts),
        in_specs=[
            pl.BlockSpec((1, ts, D), lambda b, i: (b0 + b, i, 0)),
            pl.BlockSpec((1, HALO, D), lambda b, i: (b0 + b, jnp.maximum(i * halo_blocks - 1, 0), 0)),
            full((1, D)),
            full((D, in_width)),
            full((N_POOL_GROUPS, gdim, gdim)),
            full((1, pool_width)),
            full((1, sgu_width)),
            full((1, sgu_width)),
            full((SGU_HEADS, SGU_CHUNK, SGU_CHUNK)),
            full((SGU_HEADS, SGU_CHUNK, SGU_CHUNK)),
            full((1, pool_width)),
            full((1, sgu_width)),
            full((pool_width + sgu_width, D)),
        ],
        out_specs=pl.BlockSpec((1, ts, D), lambda b, i: (b, i, 0)),
        out_shape=jax.ShapeDtypeStruct((nb, S, D), jnp.float32),
        scratch_shapes=[
            pltpu.VMEM((HALO + ts, pool_width), jnp.float32),
            pltpu.VMEM((ts, pool_width + sgu_width), jnp.bfloat16),
        ],
        compiler_params=pltpu.CompilerParams(
            dimension_semantics=("parallel", "arbitrary"), vmem_limit_bytes=TC_VMEM_LIMIT),
        name="mixer",
    )(x, x, norm_mix.reshape(1, D), w_in.astype(jnp.bfloat16), pool_w.astype(jnp.bfloat16),
      pool_scale.reshape(1, pool_width), ln_g.reshape(1, sgu_width), ln_b.reshape(1, sgu_width),
      sgu_w.astype(jnp.bfloat16),
      jnp.broadcast_to(sgu_b[:, :, None], (SGU_HEADS, SGU_CHUNK, SGU_CHUNK)),
      on_pool.reshape(1, pool_width), on_sgu.reshape(1, sgu_width), w_out.astype(jnp.bfloat16))


def _topk_rows(s, k):
    n = s.shape[0]
    iota = lax.broadcasted_iota(jnp.int32, s.shape, 0)
    vals, idxs = [], []
    for _ in range(k):
        m = jnp.max(s, axis=0, keepdims=True)
        ix = jnp.min(jnp.where(s == m, iota, n), axis=0, keepdims=True)
        vals.append(m)
        idxs.append(ix)
        s = jnp.where(iota == ix, -jnp.inf, s)
    return vals, idxs


def _pair_candidates(v1, i1, v2, i2):
    k = PEER_TOPK
    v2c, i2c = jnp.concatenate(v2, axis=0), jnp.concatenate(i2, axis=0)
    m = v2c.shape[1]
    vals, experts, flats = [], [], []
    a = 0
    while k // (a + 1) > 1:
        rows = -(-(k // (a + 1)) // V7X_SUBLANES) * V7X_SUBLANES
        vals.append(v1[a] + v2c[:rows])
        experts.append(i1[a] * PEER_N_KEYS + i2c[:rows])
        flats.append(a * k + lax.broadcasted_iota(jnp.int32, (rows, m), 0))
        a += 1
    vals.append(jnp.concatenate(v1[a:], axis=0) + v2[0])
    experts.append(jnp.concatenate(i1[a:], axis=0) * PEER_N_KEYS + i2[0])
    flats.append((a + lax.broadcasted_iota(jnp.int32, (k - a, m), 0)) * k)
    return jnp.concatenate(vals, axis=0), jnp.concatenate(experts, axis=0), jnp.concatenate(flats, axis=0)


def _router_kernel(x_ref, nffn_ref, wq_ref, keys_ref, after_a_ref, after_b_ref, h2_ref, idx_ref, gate_ref,
                   q_ref, idxt_ref, gatet_ref):
    del after_a_ref, after_b_ref
    h2 = _rms(x_ref[...], nffn_ref[...]).astype(jnp.bfloat16)
    h2_ref[...] = h2
    q_ref[...] = jnp.dot(h2, wq_ref[...], preferred_element_type=jnp.float32).astype(jnp.bfloat16)
    dq = 2 * PEER_D_HALF
    nt = (((1,), (1,)), ((), ()))

    def head(hd, carry):
        off = pl.multiple_of(hd * dq, dq)
        s1 = lax.dot_general(keys_ref[0], q_ref[:, pl.ds(off, PEER_D_HALF)], nt,
                             preferred_element_type=jnp.float32)
        s2 = lax.dot_general(keys_ref[1], q_ref[:, pl.ds(off + PEER_D_HALF, PEER_D_HALF)], nt,
                             preferred_element_type=jnp.float32)
        v1, i1 = _topk_rows(s1, PEER_TOPK)
        v2, i2 = _topk_rows(s2, PEER_TOPK)
        cand, expert, flat = _pair_candidates(v1, i1, v2, i2)
        cv, ce = [], []
        for _ in range(PEER_TOPK):
            m = jnp.max(cand, axis=0, keepdims=True)
            ix = jnp.min(jnp.where(cand == m, flat, PEER_TOPK * PEER_TOPK), axis=0, keepdims=True)
            hit = flat == ix
            cv.append(m)
            ce.append(jnp.max(jnp.where(hit, expert, -1), axis=0, keepdims=True))
            cand = jnp.where(hit, -jnp.inf, cand)
        cvc = jnp.concatenate(cv, axis=0)
        e = jnp.exp(cvc - cv[0])
        gate = e / jnp.sum(e, axis=0, keepdims=True)
        row = pl.multiple_of(hd * PEER_TOPK, PEER_TOPK)
        idxt_ref[pl.ds(row, PEER_TOPK), :] = jnp.concatenate(ce, axis=0)
        gatet_ref[pl.ds(row, PEER_TOPK), :] = gate
        return carry

    lax.fori_loop(0, PEER_HEADS, head, 0)
    idx_ref[...] = idxt_ref[...].T
    gate_ref[...] = gatet_ref[...].T


def _router(x1, norm_ffn, wq, keys, tok0, T, after):
    D = x1.shape[1]
    tr = min(ROUTE_TILE, T)
    qw = wq.shape[1]
    assert tok0 % tr == 0 and T % tr == 0
    first = tok0 // tr
    full = lambda shape: pl.BlockSpec(shape, lambda i: (0,) * len(shape))
    return pl.pallas_call(
        _router_kernel,
        grid=(T // tr,),
        in_specs=[
            pl.BlockSpec((tr, D), lambda i: (first + i, 0)),
            full((1, D)),
            full((D, qw)),
            full((2, PEER_N_KEYS, PEER_D_HALF)),
            pl.BlockSpec(memory_space=pl.ANY),
            pl.BlockSpec(memory_space=pl.ANY),
        ],
        out_specs=[
            pl.BlockSpec((tr, D), lambda i: (i, 0)),
            pl.BlockSpec((tr, EXPERTS_PER_TOKEN), lambda i: (i, 0)),
            pl.BlockSpec((tr, EXPERTS_PER_TOKEN), lambda i: (i, 0)),
        ],
        out_shape=[
            jax.ShapeDtypeStruct((T, D), jnp.bfloat16),
            jax.ShapeDtypeStruct((T, EXPERTS_PER_TOKEN), jnp.int32),
            jax.ShapeDtypeStruct((T, EXPERTS_PER_TOKEN), jnp.float32),
        ],
        scratch_shapes=[
            pltpu.VMEM((tr, qw), jnp.bfloat16),
            pltpu.VMEM((EXPERTS_PER_TOKEN, tr), jnp.int32),
            pltpu.VMEM((EXPERTS_PER_TOKEN, tr), jnp.float32),
        ],
        compiler_params=pltpu.CompilerParams(
            dimension_semantics=("parallel",), vmem_limit_bytes=TC_VMEM_LIMIT),
        name="router",
    )(x1, norm_ffn.reshape(1, D), wq, keys, *after)


def _scores_kernel(h_ref, ut_ref, idx_ref, gate_ref, w_ref, pre_ref):
    j = pl.program_id(1)
    acc = jnp.dot(h_ref[...], ut_ref[...], preferred_element_type=jnp.float32)
    groups = acc.shape[1] // V7X_LANES
    lane_bits = int(math.log2(V7X_LANES))
    idx = idx_ref[...]
    group, lane = idx >> lane_bits, idx & (V7X_LANES - 1)
    picked = jnp.zeros(idx.shape, jnp.float32)
    for n in range(groups):
        vals = jnp.take_along_axis(acc[:, n * V7X_LANES:(n + 1) * V7X_LANES], lane, axis=1)
        picked = jnp.where(group == j * groups + n, vals, picked)

    @pl.when(j == 0)
    def _():
        pre_ref[...] = picked

    @pl.when(j > 0)
    def _():
        pre_ref[...] += picked

    @pl.when(j == pl.num_programs(1) - 1)
    def _():
        w = (gate_ref[...] * _gelu(pre_ref[...])).astype(jnp.bfloat16).astype(jnp.float32)
        hi = pltpu.bitcast(w, jnp.uint32) & jnp.uint32(HI_HALF)
        w_ref[...] = pltpu.bitcast(hi | (hi >> 16), jnp.int32)


def _expert_weights(h2, u_t, idx, gate):
    T, D = h2.shape
    E = u_t.shape[1]
    K = idx.shape[1]
    tm, tn = min(SCORE_TOK_TILE, T), min(SCORE_EXP_TILE, E)
    assert T % tm == 0 and E % tn == 0 and tn % V7X_LANES == 0 and K == V7X_LANES
    per_token = pl.BlockSpec((tm, K), lambda i, j: (i, 0))
    return pl.pallas_call(
        _scores_kernel,
        grid=(T // tm, E // tn),
        in_specs=[pl.BlockSpec((tm, D), lambda i, j: (i, 0)), pl.BlockSpec((D, tn), lambda i, j: (0, j)),
                  per_token, per_token],
        out_specs=per_token,
        out_shape=jax.ShapeDtypeStruct((T, K), jnp.int32),
        scratch_shapes=[pltpu.VMEM((tm, K), jnp.float32)],
        compiler_params=pltpu.CompilerParams(
            dimension_semantics=("parallel", "arbitrary"), vmem_limit_bytes=TC_VMEM_LIMIT),
        name="expert_weights",
    )(h2, u_t, idx, gate)


def _final_kernel(x_ref, y_ref, g_ref, *rest):
    o_ref, done_ref = rest[-2:]
    o_ref[...] = _rms(x_ref[...] + y_ref[...], g_ref[...])
    done_ref[...] = jnp.zeros(done_ref.shape, done_ref.dtype)


def _final(x1, x_tok0, peer, norm_final, out_prev, out_tok0, total):
    Tc, D = peer.shape
    te = min(EW_TILE, Tc)
    assert x_tok0 % te == 0 and out_tok0 % te == 0 and Tc % te == 0
    nblk = Tc // te
    spec = pl.BlockSpec((te, D), lambda i: (out_tok0 // te + i, 0))
    in_specs = [pl.BlockSpec((te, D), lambda i: (x_tok0 // te + i, 0)),
                pl.BlockSpec((te, D), lambda i: (i, 0)), pl.BlockSpec((1, D), lambda i: (0, 0))]
    args = [x1, peer, norm_final.reshape(1, D)]
    aliases = {}
    if out_prev is not None:
        in_specs.append(pl.BlockSpec(memory_space=pl.ANY))
        args.append(out_prev)
        aliases = {3: 0}
    return pl.pallas_call(
        _final_kernel, grid=(nblk,), in_specs=in_specs,
        out_specs=[spec, pl.BlockSpec((V7X_SUBLANES, V7X_LANES), lambda i: (i, 0))],
        out_shape=[jax.ShapeDtypeStruct((total, D), jnp.float32),
                   jax.ShapeDtypeStruct((nblk * V7X_SUBLANES, V7X_LANES), jnp.float32)],
        input_output_aliases=aliases,
        compiler_params=pltpu.CompilerParams(dimension_semantics=("parallel",)),
        name="final_norm",
    )(*args)


def _tree_sum(vals):
    while len(vals) > 1:
        nxt = [vals[i] + vals[i + 1] for i in range(0, len(vals) - 1, 2)]
        if len(vals) % 2:
            nxt.append(vals[-1])
        vals = nxt
    return vals[0]


def _sc_block_pipeline(nblk, items_per_token, loads, store, gather, compute):
    assert items_per_token % 2 == 0 and nblk >= 1

    for c in loads(0, 0):
        c.start()
    for c in loads(0, 0):
        c.wait()
    if nblk > 1:
        for c in loads(1, 1):
            c.start()
    gather(0, 0, 0, 0).start()

    @pl.loop(0, nblk)
    def _(b):
        slot = b % 2

        @pl.when(b >= 2)
        def _():
            store(b - 2, slot).wait()

        @pl.loop(0, SC_TOKENS)
        def _(t):
            for q in range(items_per_token):
                buf = q % 2
                if q + 1 < items_per_token:
                    gather(slot, t, q + 1, 1 - buf).start()
                else:
                    @pl.when(t + 1 < SC_TOKENS)
                    def _():
                        gather(slot, t + 1, 0, 1 - buf).start()

                    @pl.when(jnp.logical_and(t + 1 == SC_TOKENS, b + 1 < nblk))
                    def _():
                        for c in loads(b + 1, 1 - slot):
                            c.wait()
                        gather(1 - slot, 0, 0, 1 - buf).start()

                gather(slot, t, q, buf).wait()
                compute(slot, t, q, buf)

        store(b, slot).start()

        @pl.when(b + 2 < nblk)
        def _():
            for c in loads(b + 2, slot):
                c.start()

    if nblk >= 2:
        store(nblk - 2, nblk % 2).wait()
    store(nblk - 1, (nblk - 1) % 2).wait()


def _sc_mesh():
    return plsc.VectorSubcoreMesh(core_axis_name="c", subcore_axis_name="s")


def _sc_worker_id():
    return lax.axis_index("s") * V7X_SC_CORES + lax.axis_index("c")


def _sc_bf16(words):
    return plsc.bitcast(words, jnp.bfloat16)


def _sc_halves_f32(pairs):
    words = plsc.bitcast(pairs, jnp.uint32)
    return (plsc.bitcast(words << 16, jnp.float32),
            plsc.bitcast(words & jnp.uint32(HI_HALF), jnp.float32))


def _mix_parts(T, K, DW):
    D = 2 * DW
    L, G = V7X_SC_LANES, SC_GATHER
    nj = SC_MIX_CHUNK // L
    tok_per_w = T // V7X_SC_WORKERS
    assert T % (V7X_SC_WORKERS * SC_TOKENS) == 0 and K % (2 * G) == 0
    assert DW % SC_MIX_CHUNK == 0 and G % SC_MIX_GROUP == 0
    scratch = [
        pltpu.VMEM((2, SC_TOKENS, K), jnp.int32),
        pltpu.VMEM((2, SC_TOKENS, K), jnp.int32),
        pltpu.VMEM((2, G, DW), jnp.uint32),
        pltpu.VMEM((2, SC_TOKENS, D), jnp.float32),
        pltpu.SemaphoreType.DMA((2,)),
        pltpu.SemaphoreType.DMA((2,)),
        pltpu.SemaphoreType.DMA((2,)),
        pltpu.SemaphoreType.DMA((2,)),
    ]

    def run(w_hbm, idx_hbm, tab_hbm, out_hbm, idx_v, w_v, rows_v, out_v, idx_sems, w_sems, out_sems, row_sems):
        base = _sc_worker_id() * tok_per_w

        def loads(b, slot):
            toks = pl.ds(base + b * SC_TOKENS, SC_TOKENS)
            return [pltpu.make_async_copy(idx_hbm.at[toks], idx_v.at[slot], idx_sems.at[slot]),
                    pltpu.make_async_copy(w_hbm.at[toks], w_v.at[slot], w_sems.at[slot])]

        def store(b, slot):
            toks = pl.ds(base + b * SC_TOKENS, SC_TOKENS)
            return pltpu.make_async_copy(out_v.at[slot], out_hbm.at[toks], out_sems.at[slot])

        def gather(slot, t, q, buf):
            return pltpu.make_async_copy(
                tab_hbm.at[idx_v.at[slot, t, pl.ds(q * G, G)]], rows_v.at[buf], row_sems.at[buf])

        def compute(slot, t, q, buf):
            ssplat = jnp.full((L,), slot, jnp.int32)
            tsplat = jnp.full((L,), t, jnp.int32)
            for c in range(DW // SC_MIX_CHUNK):
                def body(kg, acc):
                    kk = kg * SC_MIX_GROUP
                    wks = [_sc_bf16(plsc.load_gather(
                        w_v, [ssplat, tsplat, jnp.full((L,), q * G + i, jnp.int32) + kk]))
                        for i in range(SC_MIX_GROUP)]
                    out = []
                    for j in range(nj):
                        prods = [wks[i] * _sc_bf16(rows_v[buf, kk + i, pl.ds(c * SC_MIX_CHUNK + j * L, L)])
                                 for i in range(SC_MIX_GROUP)]
                        lo, hi = _sc_halves_f32(_tree_sum(prods))
                        out += [acc[2 * j] + lo, acc[2 * j + 1] + hi]
                    return tuple(out)

                zero = jnp.zeros((L,), jnp.float32)
                acc = plsc.parallel_loop(0, G // SC_MIX_GROUP, carry=(zero,) * (2 * nj))(body)
                for j in range(nj):
                    for half in range(2):
                        dst = out_v.at[slot, t, pl.ds(half * DW + c * SC_MIX_CHUNK + j * L, L)]
                        if q == 0:
                            dst[...] = acc[2 * j + half]
                        else:
                            plsc.addupdate(dst, acc[2 * j + half])

        _sc_block_pipeline(tok_per_w // SC_TOKENS, K // G, loads, store, gather, compute)

    return scratch, run


def _sc_kernel(out_type, scratch_types, name):
    return functools.partial(
        pl.kernel, mesh=_sc_mesh(), out_type=out_type, scratch_types=scratch_types,
        compiler_params=pltpu.CompilerParams(needs_layout_passes=False), name=name)


def _pack_table(a):
    n, c = a.shape
    half = c // 2
    L, rb = V7X_SC_LANES, SC_PACK_ROWS
    rows_per_w = n // V7X_SC_WORKERS
    nblk = rows_per_w // rb
    assert n % (V7X_SC_WORKERS * rb) == 0 and half % L == 0

    @_sc_kernel(jax.ShapeDtypeStruct((n, half), jnp.uint32),
                [pltpu.VMEM((2, rb, c), jnp.float32), pltpu.VMEM((2, rb, half), jnp.uint32),
                 pltpu.SemaphoreType.DMA((2,)), pltpu.SemaphoreType.DMA((2,))], "pack_table")
    def k(a_hbm, o_hbm, a_v, o_v, in_sems, out_sems):
        base = _sc_worker_id() * rows_per_w

        def load(b, slot):
            return pltpu.make_async_copy(a_hbm.at[pl.ds(base + b * rb, rb)], a_v.at[slot], in_sems.at[slot])

        def store(b, slot):
            return pltpu.make_async_copy(o_v.at[slot], o_hbm.at[pl.ds(base + b * rb, rb)], out_sems.at[slot])

        load(0, 0).start()

        @pl.loop(0, nblk)
        def _(b):
            slot = b % 2
            load(b, slot).wait()

            @pl.when(b + 1 < nblk)
            def _():
                load(b + 1, 1 - slot).start()

            @pl.when(b >= 2)
            def _():
                store(b - 2, slot).wait()

            @pl.loop(0, rb)
            def _(r):
                for j in range(half // L):
                    pairs = plsc.pack(a_v[slot, r, pl.ds(j * L, L)], a_v[slot, r, pl.ds(half + j * L, L)],
                                      format=plsc.PackFormat.INTERLEAVED)
                    o_v[slot, r, pl.ds(j * L, L)] = plsc.bitcast(pairs, jnp.uint32)

            store(b, slot).start()

        if nblk >= 2:
            store(nblk - 2, nblk % 2).wait()
        store(nblk - 1, (nblk - 1) % 2).wait()

    return k(a)


def _expert_mix(w, idx, table):
    T, K = w.shape
    DW = table.shape[1]
    scratch, run = _mix_parts(T, K, DW)

    @_sc_kernel(jax.ShapeDtypeStruct((T, 2 * DW), jnp.float32), scratch, "expert_mix")
    def k(w_hbm, idx_hbm, tab_hbm, out_hbm, *mix_scratch):
        run(w_hbm, idx_hbm, tab_hbm, out_hbm, *mix_scratch)

    return k(w, idx, table)


def _chunk_sizes(total):
    ramp, size = [EDGE_CHUNK], EDGE_CHUNK
    while size < MAX_CHUNK:
        ramp.append(size)
        size *= 2
    middle = total - 2 * sum(ramp)
    assert middle >= 0 and middle % MAX_CHUNK == 0
    return ramp + [MAX_CHUNK] * (middle // MAX_CHUNK) + ramp[::-1]


def kernel(x, norm_mix, w_in, pool_w, pool_scale, sgu_ln_g, sgu_ln_b, sgu_w, sgu_b, out_norm_pool,
           out_norm_sgu, w_out, norm_ffn, peer_wq, peer_keys, peer_u, peer_v, norm_final):
    B, S, D = x.shape
    assert norm_mix.shape[0] == 1, "single-layer block"
    T = B * S
    mix_args = (norm_mix[0], w_in[0], pool_w[0], pool_scale[0], sgu_ln_g[0], sgu_ln_b[0],
                sgu_w[0], sgu_b[0], out_norm_pool[0], out_norm_sgu[0], w_out[0])
    x1_parts = [(b * S, _mixer(x, b, 1, *mix_args).reshape(S, D)) for b in range(B)]
    wq = peer_wq[0].astype(jnp.bfloat16)
    keys = peer_keys[0].astype(jnp.bfloat16)
    u_t = _transposed_bf16(peer_u[0])
    v_tab = _pack_table(peer_v[0])
    out = None
    tok0 = 0
    peers = [norm_ffn, v_tab]
    dones = [norm_ffn, norm_ffn]
    for tc in _chunk_sizes(T):
        part0, x1 = [p for p in x1_parts if p[0] <= tok0][-1]
        assert tok0 + tc <= part0 + x1.shape[0], "a token chunk must lie inside one mixer call"
        h2, idx, gate = _router(x1, norm_ffn[0], wq, keys, tok0 - part0, tc, (peers[-2], dones[-2]))
        w = _expert_weights(h2, u_t, idx, gate)
        peer = _expert_mix(w, idx, v_tab)
        peers.append(peer)
        out, done = _final(x1, tok0 - part0, peer, norm_final, out, tok0, T)
        dones.append(done)
        tok0 += tc
    return out.reshape(B, S, D)
```

```python
import functools
import math

import jax
import jax.numpy as jnp
from jax import lax
from jax.experimental import pallas as pl
from jax.experimental.pallas import tpu as pltpu
from jax.experimental.pallas import tpu_sc as plsc

POOL_WINDOWS = (2, 4, 8, 16)
N_POOL_GROUPS = len(POOL_WINDOWS)
SGU_HEADS = 4
SGU_CHUNK = 128
PEER_HEADS = 8
PEER_N_KEYS = 128
PEER_D_HALF = 128
PEER_TOPK = 16
NORM_EPS = 1e-6
EXPERTS_PER_TOKEN = PEER_HEADS * PEER_TOPK

V7X_LANES = 128
V7X_SUBLANES = 8
V7X_SC_CORES = 2
V7X_SC_SUBCORES = 16
V7X_SC_LANES = 16
V7X_SC_WORKERS = V7X_SC_CORES * V7X_SC_SUBCORES

HALO = max(POOL_WINDOWS)
MIX_TILE = 512
ROUTE_TILE = 1024
EW_TILE = 512
SCORE_TOK_TILE = 2048
SCORE_EXP_TILE = 2048
SC_PACK_ROWS = 16
SC_GATHER = 64
SC_TOKENS = 8
SC_ROW_BUFS = 3
SC_MIX_CHUNK = 128
SC_MIX_GROUP = 4
HI_HALF = 0xFFFF0000
V7X_VMEM_BYTES = 64 * 1024 * 1024
TC_VMEM_LIMIT = V7X_VMEM_BYTES * 3 // 4
EDGE_CHUNK = 512
MAX_CHUNK = 2048


def _rms(x, g):
    inv = lax.rsqrt(jnp.mean(x * x, axis=-1, keepdims=True) + NORM_EPS)
    return x * inv * g


def _transpose_cast_kernel(a_ref, o_ref):
    o_ref[...] = a_ref[...].T.astype(o_ref.dtype)


def _transposed_bf16(a):
    n, c = a.shape
    rows = min(SCORE_EXP_TILE, n)
    return pl.pallas_call(
        _transpose_cast_kernel, grid=(n // rows,),
        in_specs=[pl.BlockSpec((rows, c), lambda i: (i, 0))],
        out_specs=pl.BlockSpec((c, rows), lambda i: (0, i)),
        out_shape=jax.ShapeDtypeStruct((c, n), jnp.bfloat16),
        compiler_params=pltpu.CompilerParams(dimension_semantics=("parallel",)),
        name="transpose_table",
    )(a)


def _gelu(x):
    return 0.5 * x * (1.0 + lax.erf(x * math.sqrt(0.5)))


def _mixer_kernel(x_ref, xh_ref, nmix_ref, win_ref, poolw_ref, pscale_ref, lng_ref, lnb_ref,
                  sguw_ref, sgub_ref, onp_ref, ons_ref, wout_ref, o_ref, pext_ref, mix_ref):
    i = pl.program_id(1)
    ts = x_ref.shape[1]
    pool_w = pscale_ref.shape[1]
    gdim = pool_w // N_POOL_GROUPS
    sgu_w = lng_ref.shape[1]
    hdim = sgu_w // SGU_HEADS

    x = x_ref[0]
    h = _rms(x, nmix_ref[...]).astype(jnp.bfloat16)
    z = jnp.dot(h, win_ref[...], preferred_element_type=jnp.float32)
    p = z[:, :pool_w]

    hh = _rms(xh_ref[0], nmix_ref[...]).astype(jnp.bfloat16)
    ph = jnp.dot(hh, win_ref[:, :pool_w], preferred_element_type=jnp.float32)
    ph = jnp.where(i > 0, ph, 0.0)
    pext_ref[0:HALO, :] = ph
    pext_ref[HALO:HALO + ts, :] = p

    pos = i * ts + lax.broadcasted_iota(jnp.int32, (ts, 1), 0)
    ssq = jnp.zeros((ts, 1), jnp.float32)
    a_parts = []
    for g, win in enumerate(POOL_WINDOWS):
        cols = slice(g * gdim, (g + 1) * gdim)
        s = pext_ref[HALO:HALO + ts, cols]
        for j in range(1, win):
            s = s + pext_ref[HALO - j:HALO - j + ts, cols]
        cnt = jnp.minimum(pos + 1, win).astype(jnp.float32)
        d = (s / cnt - p[:, cols]).astype(jnp.bfloat16)
        a = jnp.dot(d, poolw_ref[g], preferred_element_type=jnp.float32) * pscale_ref[:, cols]
        ssq = ssq + jnp.sum(a * a, axis=-1, keepdims=True)
        a_parts.append(a)
    inv_a = lax.rsqrt(ssq / pool_w + NORM_EPS)
    for g in range(N_POOL_GROUPS):
        cols = slice(g * gdim, (g + 1) * gdim)
        mix_ref[:, cols] = (a_parts[g] * inv_a * onp_ref[:, cols]).astype(jnp.bfloat16)

    gz = _gelu(z[:, pool_w:])
    tril = (lax.broadcasted_iota(jnp.int32, (SGU_CHUNK, SGU_CHUNK), 0)
            >= lax.broadcasted_iota(jnp.int32, (SGU_CHUNK, SGU_CHUNK), 1))
    ssq = jnp.zeros((ts, 1), jnp.float32)
    b_parts = []
    for hd in range(SGU_HEADS):
        cols = slice(hd * hdim, (hd + 1) * hdim)
        u = gz[:, hd * hdim:(hd + 1) * hdim]
        v = gz[:, sgu_w + hd * hdim:sgu_w + (hd + 1) * hdim]
        mu = jnp.mean(v, axis=-1, keepdims=True)
        vc = v - mu
        var = jnp.mean(vc * vc, axis=-1, keepdims=True)
        vn = (vc * lax.rsqrt(var + NORM_EPS) * lng_ref[:, cols] + lnb_ref[:, cols]).astype(jnp.bfloat16)
        w = jnp.where(tril, sguw_ref[hd], jnp.zeros((), sguw_ref.dtype))
        mixed = [jnp.dot(w, vn[n * SGU_CHUNK:(n + 1) * SGU_CHUNK], preferred_element_type=jnp.float32)
                 + sgub_ref[hd] for n in range(ts // SGU_CHUNK)]
        b = u * jnp.concatenate(mixed, axis=0)
        ssq = ssq + jnp.sum(b * b, axis=-1, keepdims=True)
        b_parts.append(b)
    inv_b = lax.rsqrt(ssq / sgu_w + NORM_EPS)
    for hd in range(SGU_HEADS):
        cols = slice(hd * hdim, (hd + 1) * hdim)
        mix_ref[:, pool_w + hd * hdim:pool_w + (hd + 1) * hdim] = (
            b_parts[hd] * inv_b * ons_ref[:, cols]).astype(jnp.bfloat16)

    o_ref[0] = x + jnp.dot(mix_ref[...], wout_ref[...], preferred_element_type=jnp.float32)


def _mixer(x, b0, nb, norm_mix, w_in, pool_w, pool_scale, ln_g, ln_b, sgu_w, sgu_b, on_pool, on_sgu, w_out):
    _, S, D = x.shape
    ts = min(MIX_TILE, S)
    pool_width = pool_scale.size
    sgu_width = ln_g.size
    in_width = w_in.shape[1]
    gdim = pool_width // N_POOL_GROUPS
    halo_blocks = ts // HALO
    full = lambda shape: pl.BlockSpec(shape, lambda b, i: (0,) * len(shape))
    return pl.pallas_call(
        _mixer_kernel,
        grid=(nb, S // ts),
        in_specs=[
            pl.BlockSpec((1, ts, D), lambda b, i: (b0 + b, i, 0)),
            pl.BlockSpec((1, HALO, D), lambda b, i: (b0 + b, jnp.maximum(i * halo_blocks - 1, 0), 0)),
            full((1, D)),
            full((D, in_width)),
            full((N_POOL_GROUPS, gdim, gdim)),
            full((1, pool_width)),
            full((1, sgu_width)),
            full((1, sgu_width)),
            full((SGU_HEADS, SGU_CHUNK, SGU_CHUNK)),
            full((SGU_HEADS, SGU_CHUNK, SGU_CHUNK)),
            full((1, pool_width)),
            full((1, sgu_width)),
            full((pool_width + sgu_width, D)),
        ],
        out_specs=pl.BlockSpec((1, ts, D), lambda b, i: (b, i, 0)),
        out_shape=jax.ShapeDtypeStruct((nb, S, D), jnp.float32),
        scratch_shapes=[
            pltpu.VMEM((HALO + ts, pool_width), jnp.float32),
            pltpu.VMEM((ts, pool_width + sgu_width), jnp.bfloat16),
        ],
        compiler_params=pltpu.CompilerParams(
            dimension_semantics=("parallel", "arbitrary"), vmem_limit_bytes=TC_VMEM_LIMIT),
        name="mixer",
    )(x, x, norm_mix.reshape(1, D), w_in.astype(jnp.bfloat16), pool_w.astype(jnp.bfloat16),
      pool_scale.reshape(1, pool_width), ln_g.reshape(1, sgu_width), ln_b.reshape(1, sgu_width),
      sgu_w.astype(jnp.bfloat16),
      jnp.broadcast_to(sgu_b[:, :, None], (SGU_HEADS, SGU_CHUNK, SGU_CHUNK)),
      on_pool.reshape(1, pool_width), on_sgu.reshape(1, sgu_width), w_out.astype(jnp.bfloat16))


def _topk_rows(s, k):
    n = s.shape[0]
    iota = lax.broadcasted_iota(jnp.int32, s.shape, 0)
    vals, idxs = [], []
    for _ in range(k):
        m = jnp.max(s, axis=0, keepdims=True)
        ix = jnp.min(jnp.where(s == m, iota, n), axis=0, keepdims=True)
        vals.append(m)
        idxs.append(ix)
        s = jnp.where(iota == ix, -jnp.inf, s)
    return vals, idxs


def _pair_candidates(v1, i1, v2, i2):
    k = PEER_TOPK
    v2c, i2c = jnp.concatenate(v2, axis=0), jnp.concatenate(i2, axis=0)
    m = v2c.shape[1]
    vals, experts, flats = [], [], []
    a = 0
    while k // (a + 1) > 1:
        rows = -(-(k // (a + 1)) // V7X_SUBLANES) * V7X_SUBLANES
        vals.append(v1[a] + v2c[:rows])
        experts.append(i1[a] * PEER_N_KEYS + i2c[:rows])
        flats.append(a * k + lax.broadcasted_iota(jnp.int32, (rows, m), 0))
        a += 1
    vals.append(jnp.concatenate(v1[a:], axis=0) + v2[0])
    experts.append(jnp.concatenate(i1[a:], axis=0) * PEER_N_KEYS + i2[0])
    flats.append((a + lax.broadcasted_iota(jnp.int32, (k - a, m), 0)) * k)
    return jnp.concatenate(vals, axis=0), jnp.concatenate(experts, axis=0), jnp.concatenate(flats, axis=0)


def _router_kernel(x_ref, nffn_ref, wq_ref, keys_ref, after_ref, h2_ref, idx_ref, gate_ref,
                   q_ref, idxt_ref, gatet_ref):
    del after_ref
    h2 = _rms(x_ref[...], nffn_ref[...]).astype(jnp.bfloat16)
    h2_ref[...] = h2
    q_ref[...] = jnp.dot(h2, wq_ref[...], preferred_element_type=jnp.float32).astype(jnp.bfloat16)
    dq = 2 * PEER_D_HALF
    nt = (((1,), (1,)), ((), ()))

    def head(hd, carry):
        off = pl.multiple_of(hd * dq, dq)
        s1 = lax.dot_general(keys_ref[0], q_ref[:, pl.ds(off, PEER_D_HALF)], nt,
                             preferred_element_type=jnp.float32)
        s2 = lax.dot_general(keys_ref[1], q_ref[:, pl.ds(off + PEER_D_HALF, PEER_D_HALF)], nt,
                             preferred_element_type=jnp.float32)
        v1, i1 = _topk_rows(s1, PEER_TOPK)
        v2, i2 = _topk_rows(s2, PEER_TOPK)
        cand, expert, flat = _pair_candidates(v1, i1, v2, i2)
        cv, ce = [], []
        for _ in range(PEER_TOPK):
            m = jnp.max(cand, axis=0, keepdims=True)
            ix = jnp.min(jnp.where(cand == m, flat, PEER_TOPK * PEER_TOPK), axis=0, keepdims=True)
            hit = flat == ix
            cv.append(m)
            ce.append(jnp.max(jnp.where(hit, expert, -1), axis=0, keepdims=True))
            cand = jnp.where(hit, -jnp.inf, cand)
        cvc = jnp.concatenate(cv, axis=0)
        e = jnp.exp(cvc - cv[0])
        gate = e / jnp.sum(e, axis=0, keepdims=True)
        row = pl.multiple_of(hd * PEER_TOPK, PEER_TOPK)
        idxt_ref[pl.ds(row, PEER_TOPK), :] = jnp.concatenate(ce, axis=0)
        gatet_ref[pl.ds(row, PEER_TOPK), :] = gate
        return carry

    lax.fori_loop(0, PEER_HEADS, head, 0)
    idx_ref[...] = idxt_ref[...].T
    gate_ref[...] = gatet_ref[...].T


def _router(x1, norm_ffn, wq, keys, tok0, T, after):
    D = x1.shape[1]
    tr = min(ROUTE_TILE, T)
    qw = wq.shape[1]
    assert tok0 % tr == 0 and T % tr == 0
    first = tok0 // tr
    full = lambda shape: pl.BlockSpec(shape, lambda i: (0,) * len(shape))
    return pl.pallas_call(
        _router_kernel,
        grid=(T // tr,),
        in_specs=[
            pl.BlockSpec((tr, D), lambda i: (first + i, 0)),
            full((1, D)),
            full((D, qw)),
            full((2, PEER_N_KEYS, PEER_D_HALF)),
            pl.BlockSpec(memory_space=pl.ANY),
        ],
        out_specs=[
            pl.BlockSpec((tr, D), lambda i: (i, 0)),
            pl.BlockSpec((tr, EXPERTS_PER_TOKEN), lambda i: (i, 0)),
            pl.BlockSpec((tr, EXPERTS_PER_TOKEN), lambda i: (i, 0)),
        ],
        out_shape=[
            jax.ShapeDtypeStruct((T, D), jnp.bfloat16),
            jax.ShapeDtypeStruct((T, EXPERTS_PER_TOKEN), jnp.int32),
            jax.ShapeDtypeStruct((T, EXPERTS_PER_TOKEN), jnp.float32),
        ],
        scratch_shapes=[
            pltpu.VMEM((tr, qw), jnp.bfloat16),
            pltpu.VMEM((EXPERTS_PER_TOKEN, tr), jnp.int32),
            pltpu.VMEM((EXPERTS_PER_TOKEN, tr), jnp.float32),
        ],
        compiler_params=pltpu.CompilerParams(
            dimension_semantics=("parallel",), vmem_limit_bytes=TC_VMEM_LIMIT),
        name="router",
    )(x1, norm_ffn.reshape(1, D), wq, keys, after)


def _scores_kernel(h_ref, ut_ref, idx_ref, gate_ref, w_ref, pre_ref):
    j = pl.program_id(1)
    acc = jnp.dot(h_ref[...], ut_ref[...], preferred_element_type=jnp.float32)
    groups = acc.shape[1] // V7X_LANES
    lane_bits = int(math.log2(V7X_LANES))
    idx = idx_ref[...]
    group, lane = idx >> lane_bits, idx & (V7X_LANES - 1)
    picked = jnp.zeros(idx.shape, jnp.float32)
    for n in range(groups):
        vals = jnp.take_along_axis(acc[:, n * V7X_LANES:(n + 1) * V7X_LANES], lane, axis=1)
        picked = jnp.where(group == j * groups + n, vals, picked)

    @pl.when(j == 0)
    def _():
        pre_ref[...] = picked

    @pl.when(j > 0)
    def _():
        pre_ref[...] += picked

    @pl.when(j == pl.num_programs(1) - 1)
    def _():
        w = (gate_ref[...] * _gelu(pre_ref[...])).astype(jnp.bfloat16).astype(jnp.float32)
        hi = pltpu.bitcast(w, jnp.uint32) & jnp.uint32(HI_HALF)
        w_ref[...] = pltpu.bitcast(hi | (hi >> 16), jnp.int32)


def _expert_weights(h2, u_t, idx, gate):
    T, D = h2.shape
    E = u_t.shape[1]
    K = idx.shape[1]
    tm, tn = min(SCORE_TOK_TILE, T), min(SCORE_EXP_TILE, E)
    assert T % tm == 0 and E % tn == 0 and tn % V7X_LANES == 0 and K == V7X_LANES
    per_token = pl.BlockSpec((tm, K), lambda i, j: (i, 0))
    return pl.pallas_call(
        _scores_kernel,
        grid=(T // tm, E // tn),
        in_specs=[pl.BlockSpec((tm, D), lambda i, j: (i, 0)), pl.BlockSpec((D, tn), lambda i, j: (0, j)),
                  per_token, per_token],
        out_specs=per_token,
        out_shape=jax.ShapeDtypeStruct((T, K), jnp.int32),
        scratch_shapes=[pltpu.VMEM((tm, K), jnp.float32)],
        compiler_params=pltpu.CompilerParams(
            dimension_semantics=("parallel", "arbitrary"), vmem_limit_bytes=TC_VMEM_LIMIT),
        name="expert_weights",
    )(h2, u_t, idx, gate)


def _final_kernel(x_ref, y_ref, g_ref, *rest):
    o_ref = rest[-1]
    o_ref[...] = _rms(x_ref[...] + y_ref[...], g_ref[...])


def _final(x1, x_tok0, peer, norm_final, out_prev, out_tok0, total):
    Tc, D = peer.shape
    te = min(EW_TILE, Tc)
    assert x_tok0 % te == 0 and out_tok0 % te == 0 and Tc % te == 0
    nblk = Tc // te
    spec = pl.BlockSpec((te, D), lambda i: (out_tok0 // te + i, 0))
    in_specs = [pl.BlockSpec((te, D), lambda i: (x_tok0 // te + i, 0)),
                pl.BlockSpec((te, D), lambda i: (i, 0)), pl.BlockSpec((1, D), lambda i: (0, 0))]
    args = [x1, peer, norm_final.reshape(1, D)]
    aliases = {}
    if out_prev is not None:
        in_specs.append(pl.BlockSpec(memory_space=pl.ANY))
        args.append(out_prev)
        aliases = {3: 0}
    return pl.pallas_call(
        _final_kernel, grid=(nblk,), in_specs=in_specs, out_specs=spec,
        out_shape=jax.ShapeDtypeStruct((total, D), jnp.float32),
        input_output_aliases=aliases,
        compiler_params=pltpu.CompilerParams(dimension_semantics=("parallel",)),
        name="final_norm",
    )(*args)


def _tree_sum(vals):
    while len(vals) > 1:
        nxt = [vals[i] + vals[i + 1] for i in range(0, len(vals) - 1, 2)]
        if len(vals) % 2:
            nxt.append(vals[-1])
        vals = nxt
    return vals[0]


def _sc_block_pipeline(nblk, items_per_token, loads, store, gather, compute):
    ipt = items_per_token
    assert SC_ROW_BUFS > ipt and nblk >= 1

    for c in loads(0, 0):
        c.start()
    for c in loads(0, 0):
        c.wait()
    if nblk > 1:
        for c in loads(1, 1):
            c.start()
    for q in range(ipt):
        gather(0, 0, q, q).start()

    @pl.loop(0, nblk)
    def _(b):
        slot = b % 2

        @pl.when(b >= 2)
        def _():
            store(b - 2, slot).wait()

        @pl.loop(0, SC_TOKENS)
        def _(t):
            item0 = (b * SC_TOKENS + t) * ipt
            for q in range(ipt):
                nxt_buf = (item0 + q + ipt) % SC_ROW_BUFS

                @pl.when(t + 1 < SC_TOKENS)
                def _():
                    gather(slot, t + 1, q, nxt_buf).start()

                @pl.when(jnp.logical_and(t + 1 == SC_TOKENS, b + 1 < nblk))
                def _():
                    if q == 0:
                        for c in loads(b + 1, 1 - slot):
                            c.wait()
                    gather(1 - slot, 0, q, nxt_buf).start()

                buf = (item0 + q) % SC_ROW_BUFS
                gather(slot, t, q, buf).wait()
                compute(slot, t, q, buf)

        store(b, slot).start()

        @pl.when(b + 2 < nblk)
        def _():
            for c in loads(b + 2, slot):
                c.start()

    if nblk >= 2:
        store(nblk - 2, nblk % 2).wait()
    store(nblk - 1, (nblk - 1) % 2).wait()


def _sc_mesh():
    return plsc.VectorSubcoreMesh(core_axis_name="c", subcore_axis_name="s")


def _sc_worker_id():
    return lax.axis_index("s") * V7X_SC_CORES + lax.axis_index("c")


def _sc_bf16(words):
    return plsc.bitcast(words, jnp.bfloat16)


def _sc_halves_f32(pairs):
    words = plsc.bitcast(pairs, jnp.uint32)
    return (plsc.bitcast(words << 16, jnp.float32),
            plsc.bitcast(words & jnp.uint32(HI_HALF), jnp.float32))


def _mix_parts(T, K, DW):
    D = 2 * DW
    L, G = V7X_SC_LANES, SC_GATHER
    nj = SC_MIX_CHUNK // L
    tok_per_w = T // V7X_SC_WORKERS
    assert T % (V7X_SC_WORKERS * SC_TOKENS) == 0 and K % (2 * G) == 0
    assert DW % SC_MIX_CHUNK == 0 and G % SC_MIX_GROUP == 0
    scratch = [
        pltpu.VMEM((2, SC_TOKENS, K), jnp.int32),
        pltpu.VMEM((2, SC_TOKENS, K), jnp.int32),
        pltpu.VMEM((SC_ROW_BUFS, G, DW), jnp.uint32),
        pltpu.VMEM((2, SC_TOKENS, D), jnp.float32),
        pltpu.SemaphoreType.DMA((2,)),
        pltpu.SemaphoreType.DMA((2,)),
        pltpu.SemaphoreType.DMA((2,)),
        pltpu.SemaphoreType.DMA((SC_ROW_BUFS,)),
    ]

    def run(w_hbm, idx_hbm, tab_hbm, out_hbm, idx_v, w_v, rows_v, out_v, idx_sems, w_sems, out_sems, row_sems):
        base = _sc_worker_id() * tok_per_w

        def loads(b, slot):
            toks = pl.ds(base + b * SC_TOKENS, SC_TOKENS)
            return [pltpu.make_async_copy(idx_hbm.at[toks], idx_v.at[slot], idx_sems.at[slot]),
                    pltpu.make_async_copy(w_hbm.at[toks], w_v.at[slot], w_sems.at[slot])]

        def store(b, slot):
            toks = pl.ds(base + b * SC_TOKENS, SC_TOKENS)
            return pltpu.make_async_copy(out_v.at[slot], out_hbm.at[toks], out_sems.at[slot])

        def gather(slot, t, q, buf):
            return pltpu.make_async_copy(
                tab_hbm.at[idx_v.at[slot, t, pl.ds(q * G, G)]], rows_v.at[buf], row_sems.at[buf])

        def compute(slot, t, q, buf):
            ssplat = jnp.full((L,), slot, jnp.int32)
            tsplat = jnp.full((L,), t, jnp.int32)
            for c in range(DW // SC_MIX_CHUNK):
                def body(kg, acc):
                    kk = kg * SC_MIX_GROUP
                    wks = [_sc_bf16(plsc.load_gather(
                        w_v, [ssplat, tsplat, jnp.full((L,), q * G + i, jnp.int32) + kk]))
                        for i in range(SC_MIX_GROUP)]
                    out = []
                    for j in range(nj):
                        prods = [wks[i] * _sc_bf16(rows_v[buf, kk + i, pl.ds(c * SC_MIX_CHUNK + j * L, L)])
                                 for i in range(SC_MIX_GROUP)]
                        lo, hi = _sc_halves_f32(_tree_sum(prods))
                        out += [acc[2 * j] + lo, acc[2 * j + 1] + hi]
                    return tuple(out)

                zero = jnp.zeros((L,), jnp.float32)
                acc = plsc.parallel_loop(0, G // SC_MIX_GROUP, carry=(zero,) * (2 * nj))(body)
                for j in range(nj):
                    for half in range(2):
                        dst = out_v.at[slot, t, pl.ds(half * DW + c * SC_MIX_CHUNK + j * L, L)]
                        if q == 0:
                            dst[...] = acc[2 * j + half]
                        else:
                            plsc.addupdate(dst, acc[2 * j + half])

        _sc_block_pipeline(tok_per_w // SC_TOKENS, K // G, loads, store, gather, compute)

    return scratch, run


def _sc_kernel(out_type, scratch_types, name):
    return functools.partial(
        pl.kernel, mesh=_sc_mesh(), out_type=out_type, scratch_types=scratch_types,
        compiler_params=pltpu.CompilerParams(needs_layout_passes=False), name=name)


def _pack_table(a):
    n, c = a.shape
    half = c // 2
    L, rb = V7X_SC_LANES, SC_PACK_ROWS
    rows_per_w = n // V7X_SC_WORKERS
    nblk = rows_per_w // rb
    assert n % (V7X_SC_WORKERS * rb) == 0 and half % L == 0

    @_sc_kernel(jax.ShapeDtypeStruct((n, half), jnp.uint32),
                [pltpu.VMEM((2, rb, c), jnp.float32), pltpu.VMEM((2, rb, half), jnp.uint32),
                 pltpu.SemaphoreType.DMA((2,)), pltpu.SemaphoreType.DMA((2,))], "pack_table")
    def k(a_hbm, o_hbm, a_v, o_v, in_sems, out_sems):
        base = _sc_worker_id() * rows_per_w

        def load(b, slot):
            return pltpu.make_async_copy(a_hbm.at[pl.ds(base + b * rb, rb)], a_v.at[slot], in_sems.at[slot])

        def store(b, slot):
            return pltpu.make_async_copy(o_v.at[slot], o_hbm.at[pl.ds(base + b * rb, rb)], out_sems.at[slot])

        load(0, 0).start()

        @pl.loop(0, nblk)
        def _(b):
            slot = b % 2
            load(b, slot).wait()

            @pl.when(b + 1 < nblk)
            def _():
                load(b + 1, 1 - slot).start()

            @pl.when(b >= 2)
            def _():
                store(b - 2, slot).wait()

            @pl.loop(0, rb)
            def _(r):
                for j in range(half // L):
                    pairs = plsc.pack(a_v[slot, r, pl.ds(j * L, L)], a_v[slot, r, pl.ds(half + j * L, L)],
                                      format=plsc.PackFormat.INTERLEAVED)
                    o_v[slot, r, pl.ds(j * L, L)] = plsc.bitcast(pairs, jnp.uint32)

            store(b, slot).start()

        if nblk >= 2:
            store(nblk - 2, nblk % 2).wait()
        store(nblk - 1, (nblk - 1) % 2).wait()

    return k(a)


def _expert_mix(w, idx, table):
    T, K = w.shape
    DW = table.shape[1]
    scratch, run = _mix_parts(T, K, DW)

    @_sc_kernel(jax.ShapeDtypeStruct((T, 2 * DW), jnp.float32), scratch, "expert_mix")
    def k(w_hbm, idx_hbm, tab_hbm, out_hbm, *mix_scratch):
        run(w_hbm, idx_hbm, tab_hbm, out_hbm, *mix_scratch)

    return k(w, idx, table)


def _chunk_sizes(total):
    ramp, size = [EDGE_CHUNK], EDGE_CHUNK
    while size < MAX_CHUNK:
        ramp.append(size)
        size *= 2
    middle = total - 2 * sum(ramp)
    assert middle >= 0 and middle % MAX_CHUNK == 0
    return ramp + [MAX_CHUNK] * (middle // MAX_CHUNK) + ramp[::-1]


def kernel(x, norm_mix, w_in, pool_w, pool_scale, sgu_ln_g, sgu_ln_b, sgu_w, sgu_b, out_norm_pool,
           out_norm_sgu, w_out, norm_ffn, peer_wq, peer_keys, peer_u, peer_v, norm_final):
    B, S, D = x.shape
    assert norm_mix.shape[0] == 1, "single-layer block"
    T = B * S
    mix_args = (norm_mix[0], w_in[0], pool_w[0], pool_scale[0], sgu_ln_g[0], sgu_ln_b[0],
                sgu_w[0], sgu_b[0], out_norm_pool[0], out_norm_sgu[0], w_out[0])
    x1_parts = [(b * S, _mixer(x, b, 1, *mix_args).reshape(S, D)) for b in range(B)]
    wq = peer_wq[0].astype(jnp.bfloat16)
    keys = peer_keys[0].astype(jnp.bfloat16)
    u_t = _transposed_bf16(peer_u[0])
    v_tab = _pack_table(peer_v[0])
    out = None
    tok0 = 0
    peers = [norm_ffn, v_tab]
    for tc in _chunk_sizes(T):
        part0, x1 = [p for p in x1_parts if p[0] <= tok0][-1]
        assert tok0 + tc <= part0 + x1.shape[0], "a token chunk must lie inside one mixer call"
        h2, idx, gate = _router(x1, norm_ffn[0], wq, keys, tok0 - part0, tc, peers[-2])
        w = _expert_weights(h2, u_t, idx, gate)
        peer = _expert_mix(w, idx, v_tab)
        peers.append(peer)
        out = _final(x1, tok0 - part0, peer, norm_final, out, tok0, T)
        tok0 += tc
    return out.reshape(B, S, D)
```

```python
import functools
import math

import jax
import jax.numpy as jnp
from jax import lax
from jax.experimental import pallas as pl
from jax.experimental.pallas import tpu as pltpu
from jax.experimental.pallas import tpu_sc as plsc

POOL_WINDOWS = (2, 4, 8, 16)
N_POOL_GROUPS = len(POOL_WINDOWS)
SGU_HEADS = 4
SGU_CHUNK = 128
PEER_HEADS = 8
PEER_N_KEYS = 128
PEER_D_HALF = 128
PEER_TOPK = 16
NORM_EPS = 1e-6
EXPERTS_PER_TOKEN = PEER_HEADS * PEER_TOPK

V7X_LANES = 128
V7X_SUBLANES = 8
V7X_SC_CORES = 2
V7X_SC_SUBCORES = 16
V7X_SC_LANES = 16
V7X_SC_WORKERS = V7X_SC_CORES * V7X_SC_SUBCORES

HALO = max(POOL_WINDOWS)
MIX_TILE = 512
ROUTE_TILE = 1024
EW_TILE = 512
SCORE_TOK_TILE = 2048
SCORE_EXP_TILE = 2048
SC_PACK_ROWS = 16
SC_GATHER = 64
SC_TOKENS = 8
SC_ROW_BUFS = 3
SC_MIX_CHUNK = 128
SC_MIX_GROUP = 4
HI_HALF = 0xFFFF0000
V7X_VMEM_BYTES = 64 * 1024 * 1024
TC_VMEM_LIMIT = V7X_VMEM_BYTES * 3 // 4
EDGE_CHUNK = 512
MAX_CHUNK = 2048


def _rms(x, g):
    inv = lax.rsqrt(jnp.mean(x * x, axis=-1, keepdims=True) + NORM_EPS)
    return x * inv * g


def _transpose_cast_kernel(a_ref, o_ref):
    o_ref[...] = a_ref[...].T.astype(o_ref.dtype)


def _transposed_bf16(a):
    n, c = a.shape
    rows = min(SCORE_EXP_TILE, n)
    return pl.pallas_call(
        _transpose_cast_kernel, grid=(n // rows,),
        in_specs=[pl.BlockSpec((rows, c), lambda i: (i, 0))],
        out_specs=pl.BlockSpec((c, rows), lambda i: (0, i)),
        out_shape=jax.ShapeDtypeStruct((c, n), jnp.bfloat16),
        compiler_params=pltpu.CompilerParams(dimension_semantics=("parallel",)),
        name="transpose_table",
    )(a)


def _gelu(x):
    return 0.5 * x * (1.0 + lax.erf(x * math.sqrt(0.5)))


def _mixer_kernel(x_ref, xh_ref, nmix_ref, win_ref, poolw_ref, pscale_ref, lng_ref, lnb_ref,
                  sguw_ref, sgub_ref, onp_ref, ons_ref, wout_ref, o_ref, pext_ref, mix_ref):
    i = pl.program_id(1)
    ts = x_ref.shape[1]
    pool_w = pscale_ref.shape[1]
    gdim = pool_w // N_POOL_GROUPS
    sgu_w = lng_ref.shape[1]
    hdim = sgu_w // SGU_HEADS

    x = x_ref[0]
    h = _rms(x, nmix_ref[...]).astype(jnp.bfloat16)
    z = jnp.dot(h, win_ref[...], preferred_element_type=jnp.float32)
    p = z[:, :pool_w]

    hh = _rms(xh_ref[0], nmix_ref[...]).astype(jnp.bfloat16)
    ph = jnp.dot(hh, win_ref[:, :pool_w], preferred_element_type=jnp.float32)
    ph = jnp.where(i > 0, ph, 0.0)
    pext_ref[0:HALO, :] = ph
    pext_ref[HALO:HALO + ts, :] = p

    pos = i * ts + lax.broadcasted_iota(jnp.int32, (ts, 1), 0)
    ssq = jnp.zeros((ts, 1), jnp.float32)
    a_parts = []
    for g, win in enumerate(POOL_WINDOWS):
        cols = slice(g * gdim, (g + 1) * gdim)
        s = pext_ref[HALO:HALO + ts, cols]
        for j in range(1, win):
            s = s + pext_ref[HALO - j:HALO - j + ts, cols]
        cnt = jnp.minimum(pos + 1, win).astype(jnp.float32)
        d = (s / cnt - p[:, cols]).astype(jnp.bfloat16)
        a = jnp.dot(d, poolw_ref[g], preferred_element_type=jnp.float32) * pscale_ref[:, cols]
        ssq = ssq + jnp.sum(a * a, axis=-1, keepdims=True)
        a_parts.append(a)
    inv_a = lax.rsqrt(ssq / pool_w + NORM_EPS)
    for g in range(N_POOL_GROUPS):
        cols = slice(g * gdim, (g + 1) * gdim)
        mix_ref[:, cols] = (a_parts[g] * inv_a * onp_ref[:, cols]).astype(jnp.bfloat16)

    gz = _gelu(z[:, pool_w:])
    tril = (lax.broadcasted_iota(jnp.int32, (SGU_CHUNK, SGU_CHUNK), 0)
            >= lax.broadcasted_iota(jnp.int32, (SGU_CHUNK, SGU_CHUNK), 1))
    ssq = jnp.zeros((ts, 1), jnp.float32)
    b_parts = []
    for hd in range(SGU_HEADS):
        cols = slice(hd * hdim, (hd + 1) * hdim)
        u = gz[:, hd * hdim:(hd + 1) * hdim]
        v = gz[:, sgu_w + hd * hdim:sgu_w + (hd + 1) * hdim]
        mu = jnp.mean(v, axis=-1, keepdims=True)
        vc = v - mu
        var = jnp.mean(vc * vc, axis=-1, keepdims=True)
        vn = (vc * lax.rsqrt(var + NORM_EPS) * lng_ref[:, cols] + lnb_ref[:, cols]).astype(jnp.bfloat16)
        w = jnp.where(tril, sguw_ref[hd], jnp.zeros((), sguw_ref.dtype))
        mixed = [jnp.dot(w, vn[n * SGU_CHUNK:(n + 1) * SGU_CHUNK], preferred_element_type=jnp.float32)
                 + sgub_ref[hd] for n in range(ts // SGU_CHUNK)]
        b = u * jnp.concatenate(mixed, axis=0)
        ssq = ssq + jnp.sum(b * b, axis=-1, keepdims=True)
        b_parts.append(b)
    inv_b = lax.rsqrt(ssq / sgu_w + NORM_EPS)
    for hd in range(SGU_HEADS):
        cols = slice(hd * hdim, (hd + 1) * hdim)
        mix_ref[:, pool_w + hd * hdim:pool_w + (hd + 1) * hdim] = (
            b_parts[hd] * inv_b * ons_ref[:, cols]).astype(jnp.bfloat16)

    o_ref[0] = x + jnp.dot(mix_ref[...], wout_ref[...], preferred_element_type=jnp.float32)


def _mixer(x, b0, nb, norm_mix, w_in, pool_w, pool_scale, ln_g, ln_b, sgu_w, sgu_b, on_pool, on_sgu, w_out):
    _, S, D = x.shape
    ts = min(MIX_TILE, S)
    pool_width = pool_scale.size
    sgu_width = ln_g.size
    in_width = w_in.shape[1]
    gdim = pool_width // N_POOL_GROUPS
    halo_blocks = ts // HALO
    full = lambda shape: pl.BlockSpec(shape, lambda b, i: (0,) * len(shape))
    return pl.pallas_call(
        _mixer_kernel,
        grid=(nb, S // ts),
        in_specs=[
            pl.BlockSpec((1, ts, D), lambda b, i: (b0 + b, i, 0)),
            pl.BlockSpec((1, HALO, D), lambda b, i: (b0 + b, jnp.maximum(i * halo_blocks - 1, 0), 0)),
            full((1, D)),
            full((D, in_width)),
            full((N_POOL_GROUPS, gdim, gdim)),
            full((1, pool_width)),
            full((1, sgu_width)),
            full((1, sgu_width)),
            full((SGU_HEADS, SGU_CHUNK, SGU_CHUNK)),
            full((SGU_HEADS, SGU_CHUNK, SGU_CHUNK)),
            full((1, pool_width)),
            full((1, sgu_width)),
            full((pool_width + sgu_width, D)),
        ],
        out_specs=pl.BlockSpec((1, ts, D), lambda b, i: (b, i, 0)),
        out_shape=jax.ShapeDtypeStruct((nb, S, D), jnp.float32),
        scratch_shapes=[
            pltpu.VMEM((HALO + ts, pool_width), jnp.float32),
            pltpu.VMEM((ts, pool_width + sgu_width), jnp.bfloat16),
        ],
        compiler_params=pltpu.CompilerParams(
            dimension_semantics=("parallel", "arbitrary"), vmem_limit_bytes=TC_VMEM_LIMIT),
        name="mixer",
    )(x, x, norm_mix.reshape(1, D), w_in.astype(jnp.bfloat16), pool_w.astype(jnp.bfloat16),
      pool_scale.reshape(1, pool_width), ln_g.reshape(1, sgu_width), ln_b.reshape(1, sgu_width),
      sgu_w.astype(jnp.bfloat16),
      jnp.broadcast_to(sgu_b[:, :, None], (SGU_HEADS, SGU_CHUNK, SGU_CHUNK)),
      on_pool.reshape(1, pool_width), on_sgu.reshape(1, sgu_width), w_out.astype(jnp.bfloat16))


def _topk_rows(s, k):
    n = s.shape[0]
    iota = lax.broadcasted_iota(jnp.int32, s.shape, 0)
    vals, idxs = [], []
    for _ in range(k):
        m = jnp.max(s, axis=0, keepdims=True)
        ix = jnp.min(jnp.where(s == m, iota, n), axis=0, keepdims=True)
        vals.append(m)
        idxs.append(ix)
        s = jnp.where(iota == ix, -jnp.inf, s)
    return vals, idxs


def _pair_candidates(v1, i1, v2, i2):
    k, sub = PEER_TOPK, V7X_SUBLANES
    assert k == 16 and sub == 8, "the tile plan below is written for 16 candidates per side"
    v1c, i1c = jnp.concatenate(v1, axis=0), jnp.concatenate(i1, axis=0)
    v2c, i2c = jnp.concatenate(v2, axis=0), jnp.concatenate(i2, axis=0)
    m = v2c.shape[1]
    r = lax.broadcasted_iota(jnp.int32, (sub, m), 0)

    def with_b(a, b0, row0):
        shift = lambda x: x if row0 == 0 else pltpu.roll(x, row0, 0)
        return (v1[a] + shift(v2c[b0:b0 + sub]), i1[a] * PEER_N_KEYS + shift(i2c[b0:b0 + sub]),
                a * k + b0 + (r - row0))

    def with_a(a0, row0):
        s = (row0 - (a0 - sub)) % sub
        shift = lambda x: x if s == 0 else pltpu.roll(x, s, 0)
        return (shift(v1c[sub:]) + v2[0], shift(i1c[sub:]) * PEER_N_KEYS + i2[0], (a0 + (r - row0)) * k)

    def tile(*parts):
        out = parts[0][1]
        for row0, seg in parts[1:]:
            out = tuple(jnp.where(r < row0, o, s) for o, s in zip(out, seg))
        return out

    tiles = [
        with_b(0, 0, 0), with_b(0, sub, 0),
        with_b(1, 0, 0),
        tile((0, with_b(2, 0, 0)), (5, with_b(4, 0, 5))),
        tile((0, with_b(3, 0, 0)), (4, with_b(5, 0, 4)), (6, with_b(6, 0, 6))),
        tile((0, with_b(7, 0, 0)), (2, with_a(8, 2))),
        tile((0, with_a(14, 0)), (2, with_b(1, sub, 2))),
    ]
    return tuple(jnp.concatenate([t[n] for t in tiles], axis=0) for n in range(3))


def _router_kernel(x_ref, nffn_ref, wq_ref, keys_ref, after_ref, h2_ref, idx_ref, gate_ref,
                   q_ref, idxt_ref, gatet_ref):
    del after_ref
    h2 = _rms(x_ref[...], nffn_ref[...]).astype(jnp.bfloat16)
    h2_ref[...] = h2
    q_ref[...] = jnp.dot(h2, wq_ref[...], preferred_element_type=jnp.float32).astype(jnp.bfloat16)
    dq = 2 * PEER_D_HALF
    nt = (((1,), (1,)), ((), ()))

    def head(hd, carry):
        off = pl.multiple_of(hd * dq, dq)
        s1 = lax.dot_general(keys_ref[0], q_ref[:, pl.ds(off, PEER_D_HALF)], nt,
                             preferred_element_type=jnp.float32)
        s2 = lax.dot_general(keys_ref[1], q_ref[:, pl.ds(off + PEER_D_HALF, PEER_D_HALF)], nt,
                             preferred_element_type=jnp.float32)
        v1, i1 = _topk_rows(s1, PEER_TOPK)
        v2, i2 = _topk_rows(s2, PEER_TOPK)
        cand, expert, flat = _pair_candidates(v1, i1, v2, i2)
        cv, ce = [], []
        for _ in range(PEER_TOPK):
            m = jnp.max(cand, axis=0, keepdims=True)
            ix = jnp.min(jnp.where(cand == m, flat, PEER_TOPK * PEER_TOPK), axis=0, keepdims=True)
            hit = flat == ix
            cv.append(m)
            ce.append(jnp.max(jnp.where(hit, expert, -1), axis=0, keepdims=True))
            cand = jnp.where(hit, -jnp.inf, cand)
        cvc = jnp.concatenate(cv, axis=0)
        e = jnp.exp(cvc - cv[0])
        gate = e / jnp.sum(e, axis=0, keepdims=True)
        row = pl.multiple_of(hd * PEER_TOPK, PEER_TOPK)
        idxt_ref[pl.ds(row, PEER_TOPK), :] = jnp.concatenate(ce, axis=0)
        gatet_ref[pl.ds(row, PEER_TOPK), :] = gate
        return carry

    lax.fori_loop(0, PEER_HEADS, head, 0)
    idx_ref[...] = idxt_ref[...].T
    gate_ref[...] = gatet_ref[...].T


def _router(x1, norm_ffn, wq, keys, tok0, T, after):
    D = x1.shape[1]
    tr = min(ROUTE_TILE, T)
    qw = wq.shape[1]
    assert tok0 % tr == 0 and T % tr == 0
    first = tok0 // tr
    full = lambda shape: pl.BlockSpec(shape, lambda i: (0,) * len(shape))
    return pl.pallas_call(
        _router_kernel,
        grid=(T // tr,),
        in_specs=[
            pl.BlockSpec((tr, D), lambda i: (first + i, 0)),
            full((1, D)),
            full((D, qw)),
            full((2, PEER_N_KEYS, PEER_D_HALF)),
            pl.BlockSpec(memory_space=pl.ANY),
        ],
        out_specs=[
            pl.BlockSpec((tr, D), lambda i: (i, 0)),
            pl.BlockSpec((tr, EXPERTS_PER_TOKEN), lambda i: (i, 0)),
            pl.BlockSpec((tr, EXPERTS_PER_TOKEN), lambda i: (i, 0)),
        ],
        out_shape=[
            jax.ShapeDtypeStruct((T, D), jnp.bfloat16),
            jax.ShapeDtypeStruct((T, EXPERTS_PER_TOKEN), jnp.int32),
            jax.ShapeDtypeStruct((T, EXPERTS_PER_TOKEN), jnp.float32),
        ],
        scratch_shapes=[
            pltpu.VMEM((tr, qw), jnp.bfloat16),
            pltpu.VMEM((EXPERTS_PER_TOKEN, tr), jnp.int32),
            pltpu.VMEM((EXPERTS_PER_TOKEN, tr), jnp.float32),
        ],
        compiler_params=pltpu.CompilerParams(
            dimension_semantics=("parallel",), vmem_limit_bytes=TC_VMEM_LIMIT),
        name="router",
    )(x1, norm_ffn.reshape(1, D), wq, keys, after)


def _scores_kernel(h_ref, ut_ref, idx_ref, gate_ref, w_ref, pre_ref):
    j = pl.program_id(1)
    acc = jnp.dot(h_ref[...], ut_ref[...], preferred_element_type=jnp.float32)
    groups = acc.shape[1] // V7X_LANES
    lane_bits = int(math.log2(V7X_LANES))
    idx = idx_ref[...]
    group, lane = idx >> lane_bits, idx & (V7X_LANES - 1)
    picked = jnp.zeros(idx.shape, jnp.float32)
    for n in range(groups):
        vals = jnp.take_along_axis(acc[:, n * V7X_LANES:(n + 1) * V7X_LANES], lane, axis=1)
        picked = jnp.where(group == j * groups + n, vals, picked)

    @pl.when(j == 0)
    def _():
        pre_ref[...] = picked

    @pl.when(j > 0)
    def _():
        pre_ref[...] += picked

    @pl.when(j == pl.num_programs(1) - 1)
    def _():
        w = (gate_ref[...] * _gelu(pre_ref[...])).astype(jnp.bfloat16).astype(jnp.float32)
        hi = pltpu.bitcast(w, jnp.uint32) & jnp.uint32(HI_HALF)
        w_ref[...] = pltpu.bitcast(hi | (hi >> 16), jnp.int32)


def _expert_weights(h2, u_t, idx, gate):
    T, D = h2.shape
    E = u_t.shape[1]
    K = idx.shape[1]
    tm, tn = min(SCORE_TOK_TILE, T), min(SCORE_EXP_TILE, E)
    assert T % tm == 0 and E % tn == 0 and tn % V7X_LANES == 0 and K == V7X_LANES
    per_token = pl.BlockSpec((tm, K), lambda i, j: (i, 0))
    return pl.pallas_call(
        _scores_kernel,
        grid=(T // tm, E // tn),
        in_specs=[pl.BlockSpec((tm, D), lambda i, j: (i, 0)), pl.BlockSpec((D, tn), lambda i, j: (0, j)),
                  per_token, per_token],
        out_specs=per_token,
        out_shape=jax.ShapeDtypeStruct((T, K), jnp.int32),
        scratch_shapes=[pltpu.VMEM((tm, K), jnp.float32)],
        compiler_params=pltpu.CompilerParams(
            dimension_semantics=("parallel", "arbitrary"), vmem_limit_bytes=TC_VMEM_LIMIT),
        name="expert_weights",
    )(h2, u_t, idx, gate)


def _final_kernel(x_ref, y_ref, g_ref, *rest):
    o_ref = rest[-1]
    o_ref[...] = _rms(x_ref[...] + y_ref[...], g_ref[...])


def _final(x1, x_tok0, peer, norm_final, out_prev, out_tok0, total):
    Tc, D = peer.shape
    te = min(EW_TILE, Tc)
    assert x_tok0 % te == 0 and out_tok0 % te == 0 and Tc % te == 0
    nblk = Tc // te
    spec = pl.BlockSpec((te, D), lambda i: (out_tok0 // te + i, 0))
    in_specs = [pl.BlockSpec((te, D), lambda i: (x_tok0 // te + i, 0)),
                pl.BlockSpec((te, D), lambda i: (i, 0)), pl.BlockSpec((1, D), lambda i: (0, 0))]
    args = [x1, peer, norm_final.reshape(1, D)]
    aliases = {}
    if out_prev is not None:
        in_specs.append(pl.BlockSpec(memory_space=pl.ANY))
        args.append(out_prev)
        aliases = {3: 0}
    return pl.pallas_call(
        _final_kernel, grid=(nblk,), in_specs=in_specs, out_specs=spec,
        out_shape=jax.ShapeDtypeStruct((total, D), jnp.float32),
        input_output_aliases=aliases,
        compiler_params=pltpu.CompilerParams(dimension_semantics=("parallel",)),
        name="final_norm",
    )(*args)


def _tree_sum(vals):
    while len(vals) > 1:
        nxt = [vals[i] + vals[i + 1] for i in range(0, len(vals) - 1, 2)]
        if len(vals) % 2:
            nxt.append(vals[-1])
        vals = nxt
    return vals[0]


def _sc_block_pipeline(nblk, items_per_token, loads, store, gather, compute):
    ipt = items_per_token
    assert SC_ROW_BUFS > ipt and nblk >= 1

    for c in loads(0, 0):
        c.start()
    for c in loads(0, 0):
        c.wait()
    if nblk > 1:
        for c in loads(1, 1):
            c.start()
    for q in range(ipt):
        gather(0, 0, q, q).start()

    @pl.loop(0, nblk)
    def _(b):
        slot = b % 2

        @pl.when(b >= 2)
        def _():
            store(b - 2, slot).wait()

        @pl.loop(0, SC_TOKENS)
        def _(t):
            item0 = (b * SC_TOKENS + t) * ipt
            for q in range(ipt):
                nxt_buf = (item0 + q + ipt) % SC_ROW_BUFS

                @pl.when(t + 1 < SC_TOKENS)
                def _():
                    gather(slot, t + 1, q, nxt_buf).start()

                @pl.when(jnp.logical_and(t + 1 == SC_TOKENS, b + 1 < nblk))
                def _():
                    if q == 0:
                        for c in loads(b + 1, 1 - slot):
                            c.wait()
                    gather(1 - slot, 0, q, nxt_buf).start()

                buf = (item0 + q) % SC_ROW_BUFS
                gather(slot, t, q, buf).wait()
                compute(slot, t, q, buf)

        store(b, slot).start()

        @pl.when(b + 2 < nblk)
        def _():
            for c in loads(b + 2, slot):
                c.start()

    if nblk >= 2:
        store(nblk - 2, nblk % 2).wait()
    store(nblk - 1, (nblk - 1) % 2).wait()


def _sc_mesh():
    return plsc.VectorSubcoreMesh(core_axis_name="c", subcore_axis_name="s")


def _sc_worker_id():
    return lax.axis_index("s") * V7X_SC_CORES + lax.axis_index("c")


def _sc_bf16(words):
    return plsc.bitcast(words, jnp.bfloat16)


def _sc_halves_f32(pairs):
    words = plsc.bitcast(pairs, jnp.uint32)
    return (plsc.bitcast(words << 16, jnp.float32),
            plsc.bitcast(words & jnp.uint32(HI_HALF), jnp.float32))


def _mix_parts(T, K, DW):
    D = 2 * DW
    L, G = V7X_SC_LANES, SC_GATHER
    nj = SC_MIX_CHUNK // L
    tok_per_w = T // V7X_SC_WORKERS
    assert T % (V7X_SC_WORKERS * SC_TOKENS) == 0 and K % (2 * G) == 0
    assert DW % SC_MIX_CHUNK == 0 and G % SC_MIX_GROUP == 0
    scratch = [
        pltpu.VMEM((2, SC_TOKENS, K), jnp.int32),
        pltpu.VMEM((2, SC_TOKENS, K), jnp.int32),
        pltpu.VMEM((SC_ROW_BUFS, G, DW), jnp.uint32),
        pltpu.VMEM((2, SC_TOKENS, D), jnp.float32),
        pltpu.SemaphoreType.DMA((2,)),
        pltpu.SemaphoreType.DMA((2,)),
        pltpu.SemaphoreType.DMA((2,)),
        pltpu.SemaphoreType.DMA((SC_ROW_BUFS,)),
    ]

    def run(w_hbm, idx_hbm, tab_hbm, out_hbm, idx_v, w_v, rows_v, out_v, idx_sems, w_sems, out_sems, row_sems):
        base = _sc_worker_id() * tok_per_w

        def loads(b, slot):
            toks = pl.ds(base + b * SC_TOKENS, SC_TOKENS)
            return [pltpu.make_async_copy(idx_hbm.at[toks], idx_v.at[slot], idx_sems.at[slot]),
                    pltpu.make_async_copy(w_hbm.at[toks], w_v.at[slot], w_sems.at[slot])]

        def store(b, slot):
            toks = pl.ds(base + b * SC_TOKENS, SC_TOKENS)
            return pltpu.make_async_copy(out_v.at[slot], out_hbm.at[toks], out_sems.at[slot])

        def gather(slot, t, q, buf):
            return pltpu.make_async_copy(
                tab_hbm.at[idx_v.at[slot, t, pl.ds(q * G, G)]], rows_v.at[buf], row_sems.at[buf])

        def compute(slot, t, q, buf):
            ssplat = jnp.full((L,), slot, jnp.int32)
            tsplat = jnp.full((L,), t, jnp.int32)
            for c in range(DW // SC_MIX_CHUNK):
                def body(kg, acc):
                    kk = kg * SC_MIX_GROUP
                    wks = [_sc_bf16(plsc.load_gather(
                        w_v, [ssplat, tsplat, jnp.full((L,), q * G + i, jnp.int32) + kk]))
                        for i in range(SC_MIX_GROUP)]
                    out = []
                    for j in range(nj):
                        prods = [wks[i] * _sc_bf16(rows_v[buf, kk + i, pl.ds(c * SC_MIX_CHUNK + j * L, L)])
                                 for i in range(SC_MIX_GROUP)]
                        lo, hi = _sc_halves_f32(_tree_sum(prods))
                        out += [acc[2 * j] + lo, acc[2 * j + 1] + hi]
                    return tuple(out)

                zero = jnp.zeros((L,), jnp.float32)
                acc = plsc.parallel_loop(0, G // SC_MIX_GROUP, carry=(zero,) * (2 * nj))(body)
                for j in range(nj):
                    for half in range(2):
                        dst = out_v.at[slot, t, pl.ds(half * DW + c * SC_MIX_CHUNK + j * L, L)]
                        if q == 0:
                            dst[...] = acc[2 * j + half]
                        else:
                            plsc.addupdate(dst, acc[2 * j + half])

        _sc_block_pipeline(tok_per_w // SC_TOKENS, K // G, loads, store, gather, compute)

    return scratch, run


def _sc_kernel(out_type, scratch_types, name):
    return functools.partial(
        pl.kernel, mesh=_sc_mesh(), out_type=out_type, scratch_types=scratch_types,
        compiler_params=pltpu.CompilerParams(needs_layout_passes=False), name=name)


def _pack_table(a):
    n, c = a.shape
    half = c // 2
    L, rb = V7X_SC_LANES, SC_PACK_ROWS
    rows_per_w = n // V7X_SC_WORKERS
    nblk = rows_per_w // rb
    assert n % (V7X_SC_WORKERS * rb) == 0 and half % L == 0

    @_sc_kernel(jax.ShapeDtypeStruct((n, half), jnp.uint32),
                [pltpu.VMEM((2, rb, c), jnp.float32), pltpu.VMEM((2, rb, half), jnp.uint32),
                 pltpu.SemaphoreType.DMA((2,)), pltpu.SemaphoreType.DMA((2,))], "pack_table")
    def k(a_hbm, o_hbm, a_v, o_v, in_sems, out_sems):
        base = _sc_worker_id() * rows_per_w

        def load(b, slot):
            return pltpu.make_async_copy(a_hbm.at[pl.ds(base + b * rb, rb)], a_v.at[slot], in_sems.at[slot])

        def store(b, slot):
            return pltpu.make_async_copy(o_v.at[slot], o_hbm.at[pl.ds(base + b * rb, rb)], out_sems.at[slot])

        load(0, 0).start()

        @pl.loop(0, nblk)
        def _(b):
            slot = b % 2
            load(b, slot).wait()

            @pl.when(b + 1 < nblk)
            def _():
                load(b + 1, 1 - slot).start()

            @pl.when(b >= 2)
            def _():
                store(b - 2, slot).wait()

            @pl.loop(0, rb)
            def _(r):
                for j in range(half // L):
                    pairs = plsc.pack(a_v[slot, r, pl.ds(j * L, L)], a_v[slot, r, pl.ds(half + j * L, L)],
                                      format=plsc.PackFormat.INTERLEAVED)
                    o_v[slot, r, pl.ds(j * L, L)] = plsc.bitcast(pairs, jnp.uint32)

            store(b, slot).start()

        if nblk >= 2:
            store(nblk - 2, nblk % 2).wait()
        store(nblk - 1, (nblk - 1) % 2).wait()

    return k(a)


def _expert_mix(w, idx, table):
    T, K = w.shape
    DW = table.shape[1]
    scratch, run = _mix_parts(T, K, DW)

    @_sc_kernel(jax.ShapeDtypeStruct((T, 2 * DW), jnp.float32), scratch, "expert_mix")
    def k(w_hbm, idx_hbm, tab_hbm, out_hbm, *mix_scratch):
        run(w_hbm, idx_hbm, tab_hbm, out_hbm, *mix_scratch)

    return k(w, idx, table)


def _chunk_sizes(total):
    ramp, size = [EDGE_CHUNK], EDGE_CHUNK
    while size < MAX_CHUNK:
        ramp.append(size)
        size *= 2
    middle = total - 2 * sum(ramp)
    assert middle >= 0 and middle % MAX_CHUNK == 0
    return ramp + [MAX_CHUNK] * (middle // MAX_CHUNK) + ramp[::-1]


def kernel(x, norm_mix, w_in, pool_w, pool_scale, sgu_ln_g, sgu_ln_b, sgu_w, sgu_b, out_norm_pool,
           out_norm_sgu, w_out, norm_ffn, peer_wq, peer_keys, peer_u, peer_v, norm_final):
    B, S, D = x.shape
    assert norm_mix.shape[0] == 1, "single-layer block"
    T = B * S
    mix_args = (norm_mix[0], w_in[0], pool_w[0], pool_scale[0], sgu_ln_g[0], sgu_ln_b[0],
                sgu_w[0], sgu_b[0], out_norm_pool[0], out_norm_sgu[0], w_out[0])
    x1_parts = [(b * S, _mixer(x, b, 1, *mix_args).reshape(S, D)) for b in range(B)]
    wq = peer_wq[0].astype(jnp.bfloat16)
    keys = peer_keys[0].astype(jnp.bfloat16)
    u_t = _transposed_bf16(peer_u[0])
    v_tab = _pack_table(peer_v[0])
    out = None
    tok0 = 0
    peers = [norm_ffn, v_tab]
    for tc in _chunk_sizes(T):
        part0, x1 = [p for p in x1_parts if p[0] <= tok0][-1]
        assert tok0 + tc <= part0 + x1.shape[0], "a token chunk must lie inside one mixer call"
        h2, idx, gate = _router(x1, norm_ffn[0], wq, keys, tok0 - part0, tc, peers[-2])
        w = _expert_weights(h2, u_t, idx, gate)
        peer = _expert_mix(w, idx, v_tab)
        peers.append(peer)
        out = _final(x1, tok0 - part0, peer, norm_final, out, tok0, T)
        tok0 += tc
    return out.reshape(B, S, D)
```

```python
import functools
import math

import jax
import jax.numpy as jnp
from jax import lax
from jax.experimental import pallas as pl
from jax.experimental.pallas import tpu as pltpu
from jax.experimental.pallas import tpu_sc as plsc

POOL_WINDOWS = (2, 4, 8, 16)
N_POOL_GROUPS = len(POOL_WINDOWS)
SGU_HEADS = 4
SGU_CHUNK = 128
PEER_HEADS = 8
PEER_N_KEYS = 128
PEER_D_HALF = 128
PEER_TOPK = 16
NORM_EPS = 1e-6
EXPERTS_PER_TOKEN = PEER_HEADS * PEER_TOPK

V7X_LANES = 128
V7X_SUBLANES = 8
V7X_SC_CORES = 2
V7X_SC_SUBCORES = 16
V7X_SC_LANES = 16
V7X_SC_WORKERS = V7X_SC_CORES * V7X_SC_SUBCORES

HALO = max(POOL_WINDOWS)
MIX_TILE = 512
ROUTE_TILE = 1024
EW_TILE = 512
SCORE_TOK_TILE = 2048
SCORE_EXP_TILE = 2048
SC_PACK_ROWS = 16
SC_GATHER = 64
SC_TOKENS = 8
SC_ROW_BUFS = 3
SC_MIX_CHUNK = 128
SC_MIX_GROUP = 4
HI_HALF = 0xFFFF0000
V7X_VMEM_BYTES = 64 * 1024 * 1024
TC_VMEM_LIMIT = V7X_VMEM_BYTES * 3 // 4
EDGE_CHUNK = 512
MAX_CHUNK = 2048


def _rms(x, g):
    inv = lax.rsqrt(jnp.mean(x * x, axis=-1, keepdims=True) + NORM_EPS)
    return x * inv * g


def _transpose_cast_kernel(a_ref, o_ref):
    o_ref[...] = a_ref[...].T.astype(o_ref.dtype)


def _transposed_bf16(a):
    n, c = a.shape
    rows = min(SCORE_EXP_TILE, n)
    return pl.pallas_call(
        _transpose_cast_kernel, grid=(n // rows,),
        in_specs=[pl.BlockSpec((rows, c), lambda i: (i, 0))],
        out_specs=pl.BlockSpec((c, rows), lambda i: (0, i)),
        out_shape=jax.ShapeDtypeStruct((c, n), jnp.bfloat16),
        compiler_params=pltpu.CompilerParams(dimension_semantics=("parallel",)),
        name="transpose_table",
    )(a)


def _gelu(x):
    return 0.5 * x * (1.0 + lax.erf(x * math.sqrt(0.5)))


def _mixer_kernel(x_ref, xh_ref, nmix_ref, win_ref, poolw_ref, pscale_ref, lng_ref, lnb_ref,
                  sguw_ref, sgub_ref, onp_ref, ons_ref, wout_ref, o_ref, pext_ref, mix_ref):
    i = pl.program_id(1)
    ts = x_ref.shape[1]
    pool_w = pscale_ref.shape[1]
    gdim = pool_w // N_POOL_GROUPS
    sgu_w = lng_ref.shape[1]
    hdim = sgu_w // SGU_HEADS

    x = x_ref[0]
    h = _rms(x, nmix_ref[...]).astype(jnp.bfloat16)
    z = jnp.dot(h, win_ref[...], preferred_element_type=jnp.float32)
    p = z[:, :pool_w]

    hh = _rms(xh_ref[0], nmix_ref[...]).astype(jnp.bfloat16)
    ph = jnp.dot(hh, win_ref[:, :pool_w], preferred_element_type=jnp.float32)
    ph = jnp.where(i > 0, ph, 0.0)
    pext_ref[0:HALO, :] = ph
    pext_ref[HALO:HALO + ts, :] = p

    pos = i * ts + lax.broadcasted_iota(jnp.int32, (ts, 1), 0)
    ssq = jnp.zeros((ts, 1), jnp.float32)
    a_parts = []
    for g, win in enumerate(POOL_WINDOWS):
        cols = slice(g * gdim, (g + 1) * gdim)
        s = pext_ref[HALO:HALO + ts, cols]
        for j in range(1, win):
            s = s + pext_ref[HALO - j:HALO - j + ts, cols]
        cnt = jnp.minimum(pos + 1, win).astype(jnp.float32)
        d = (s / cnt - p[:, cols]).astype(jnp.bfloat16)
        a = jnp.dot(d, poolw_ref[g], preferred_element_type=jnp.float32) * pscale_ref[:, cols]
        ssq = ssq + jnp.sum(a * a, axis=-1, keepdims=True)
        a_parts.append(a)
    inv_a = lax.rsqrt(ssq / pool_w + NORM_EPS)
    for g in range(N_POOL_GROUPS):
        cols = slice(g * gdim, (g + 1) * gdim)
        mix_ref[:, cols] = (a_parts[g] * inv_a * onp_ref[:, cols]).astype(jnp.bfloat16)

    gz = _gelu(z[:, pool_w:])
    tril = (lax.broadcasted_iota(jnp.int32, (SGU_CHUNK, SGU_CHUNK), 0)
            >= lax.broadcasted_iota(jnp.int32, (SGU_CHUNK, SGU_CHUNK), 1))
    ssq = jnp.zeros((ts, 1), jnp.float32)
    b_parts = []
    for hd in range(SGU_HEADS):
        cols = slice(hd * hdim, (hd + 1) * hdim)
        u = gz[:, hd * hdim:(hd + 1) * hdim]
        v = gz[:, sgu_w + hd * hdim:sgu_w + (hd + 1) * hdim]
        mu = jnp.mean(v, axis=-1, keepdims=True)
        vc = v - mu
        var = jnp.mean(vc * vc, axis=-1, keepdims=True)
        vn = (vc * lax.rsqrt(var + NORM_EPS) * lng_ref[:, cols] + lnb_ref[:, cols]).astype(jnp.bfloat16)
        w = jnp.where(tril, sguw_ref[hd], jnp.zeros((), sguw_ref.dtype))
        mixed = [jnp.dot(w, vn[n * SGU_CHUNK:(n + 1) * SGU_CHUNK], preferred_element_type=jnp.float32)
                 + sgub_ref[hd] for n in range(ts // SGU_CHUNK)]
        b = u * jnp.concatenate(mixed, axis=0)
        ssq = ssq + jnp.sum(b * b, axis=-1, keepdims=True)
        b_parts.append(b)
    inv_b = lax.rsqrt(ssq / sgu_w + NORM_EPS)
    for hd in range(SGU_HEADS):
        cols = slice(hd * hdim, (hd + 1) * hdim)
        mix_ref[:, pool_w + hd * hdim:pool_w + (hd + 1) * hdim] = (
            b_parts[hd] * inv_b * ons_ref[:, cols]).astype(jnp.bfloat16)

    o_ref[0] = x + jnp.dot(mix_ref[...], wout_ref[...], preferred_element_type=jnp.float32)


def _mixer(x, b0, nb, norm_mix, w_in, pool_w, pool_scale, ln_g, ln_b, sgu_w, sgu_b, on_pool, on_sgu, w_out):
    _, S, D = x.shape
    ts = min(MIX_TILE, S)
    pool_width = pool_scale.size
    sgu_width = ln_g.size
    in_width = w_in.shape[1]
    gdim = pool_width // N_POOL_GROUPS
    halo_blocks = ts // HALO
    full = lambda shape: pl.BlockSpec(shape, lambda b, i: (0,) * len(shape))
    return pl.pallas_call(
        _mixer_kernel,
        grid=(nb, S // ts),
        in_specs=[
            pl.BlockSpec((1, ts, D), lambda b, i: (b0 + b, i, 0)),
            pl.BlockSpec((1, HALO, D), lambda b, i: (b0 + b, jnp.maximum(i * halo_blocks - 1, 0), 0)),
            full((1, D)),
            full((D, in_width)),
            full((N_POOL_GROUPS, gdim, gdim)),
            full((1, pool_width)),
            full((1, sgu_width)),
            full((1, sgu_width)),
            full((SGU_HEADS, SGU_CHUNK, SGU_CHUNK)),
            full((SGU_HEADS, SGU_CHUNK, SGU_CHUNK)),
            full((1, pool_width)),
            full((1, sgu_width)),
            full((pool_width + sgu_width, D)),
        ],
        out_specs=pl.BlockSpec((1, ts, D), lambda b, i: (b, i, 0)),
        out_shape=jax.ShapeDtypeStruct((nb, S, D), jnp.float32),
        scratch_shapes=[
            pltpu.VMEM((HALO + ts, pool_width), jnp.float32),
            pltpu.VMEM((ts, pool_width + sgu_width), jnp.bfloat16),
        ],
        compiler_params=pltpu.CompilerParams(
            dimension_semantics=("parallel", "arbitrary"), vmem_limit_bytes=TC_VMEM_LIMIT),
        name="mixer",
    )(x, x, norm_mix.reshape(1, D), w_in.astype(jnp.bfloat16), pool_w.astype(jnp.bfloat16),
      pool_scale.reshape(1, pool_width), ln_g.reshape(1, sgu_width), ln_b.reshape(1, sgu_width),
      sgu_w.astype(jnp.bfloat16),
      jnp.broadcast_to(sgu_b[:, :, None], (SGU_HEADS, SGU_CHUNK, SGU_CHUNK)),
      on_pool.reshape(1, pool_width), on_sgu.reshape(1, sgu_width), w_out.astype(jnp.bfloat16))


def _topk_rows(s, k):
    sub = V7X_SUBLANES
    n, m = s.shape
    pieces = n // sub
    assert n % sub == 0 and pieces >= k, "lists shorter than k would need padding"
    row = lax.broadcasted_iota(jnp.int32, (sub, m), 0)
    v = [s[p * sub:(p + 1) * sub] for p in range(pieces)]
    ix = [row + p * sub for p in range(pieces)]
    for rnd in range(pieces):
        for p in range(rnd % 2, pieces - 1, 2):
            swap = v[p + 1] > v[p]
            v[p], v[p + 1] = jnp.where(swap, v[p + 1], v[p]), jnp.where(swap, v[p], v[p + 1])
            ix[p], ix[p + 1] = jnp.where(swap, ix[p + 1], ix[p]), jnp.where(swap, ix[p], ix[p + 1])
    vals, idxs = [], []
    for r in range(k):
        best = jnp.max(v[0], axis=0, keepdims=True)
        best_ix = jnp.min(jnp.where(v[0] == best, ix[0], n), axis=0, keepdims=True)
        vals.append(best)
        idxs.append(best_ix)
        taken = ix[0] == best_ix
        for p in range(k - 1 - r):
            v[p] = jnp.where(taken, v[p + 1], v[p])
            ix[p] = jnp.where(taken, ix[p + 1], ix[p])
    return vals, idxs


def _pair_candidates(v1, i1, v2, i2):
    k, sub = PEER_TOPK, V7X_SUBLANES
    assert k == 16 and sub == 8, "the tile plan below is written for 16 candidates per side"
    v1c, i1c = jnp.concatenate(v1, axis=0), jnp.concatenate(i1, axis=0)
    v2c, i2c = jnp.concatenate(v2, axis=0), jnp.concatenate(i2, axis=0)
    m = v2c.shape[1]
    r = lax.broadcasted_iota(jnp.int32, (sub, m), 0)

    def with_b(a, b0, row0):
        shift = lambda x: x if row0 == 0 else pltpu.roll(x, row0, 0)
        return (v1[a] + shift(v2c[b0:b0 + sub]), i1[a] * PEER_N_KEYS + shift(i2c[b0:b0 + sub]),
                a * k + b0 + (r - row0))

    def with_a(a0, row0):
        s = (row0 - (a0 - sub)) % sub
        shift = lambda x: x if s == 0 else pltpu.roll(x, s, 0)
        return (shift(v1c[sub:]) + v2[0], shift(i1c[sub:]) * PEER_N_KEYS + i2[0], (a0 + (r - row0)) * k)

    def tile(*parts):
        out = parts[0][1]
        for row0, seg in parts[1:]:
            out = tuple(jnp.where(r < row0, o, s) for o, s in zip(out, seg))
        return out

    tiles = [
        with_b(0, 0, 0), with_b(0, sub, 0),
        with_b(1, 0, 0),
        tile((0, with_b(2, 0, 0)), (5, with_b(4, 0, 5))),
        tile((0, with_b(3, 0, 0)), (4, with_b(5, 0, 4)), (6, with_b(6, 0, 6))),
        tile((0, with_b(7, 0, 0)), (2, with_a(8, 2))),
        tile((0, with_a(14, 0)), (2, with_b(1, sub, 2))),
    ]
    return tuple(jnp.concatenate([t[n] for t in tiles], axis=0) for n in range(3))


def _router_kernel(x_ref, nffn_ref, wq_ref, keys_ref, after_ref, h2_ref, idx_ref, gate_ref,
                   q_ref, idxt_ref, gatet_ref):
    del after_ref
    h2 = _rms(x_ref[...], nffn_ref[...]).astype(jnp.bfloat16)
    h2_ref[...] = h2
    q_ref[...] = jnp.dot(h2, wq_ref[...], preferred_element_type=jnp.float32).astype(jnp.bfloat16)
    dq = 2 * PEER_D_HALF
    nt = (((1,), (1,)), ((), ()))

    def head(hd, carry):
        off = pl.multiple_of(hd * dq, dq)
        s1 = lax.dot_general(keys_ref[0], q_ref[:, pl.ds(off, PEER_D_HALF)], nt,
                             preferred_element_type=jnp.float32)
        s2 = lax.dot_general(keys_ref[1], q_ref[:, pl.ds(off + PEER_D_HALF, PEER_D_HALF)], nt,
                             preferred_element_type=jnp.float32)
        v1, i1 = _topk_rows(s1, PEER_TOPK)
        v2, i2 = _topk_rows(s2, PEER_TOPK)
        cand, expert, flat = _pair_candidates(v1, i1, v2, i2)
        cv, ce = [], []
        for _ in range(PEER_TOPK):
            m = jnp.max(cand, axis=0, keepdims=True)
            ix = jnp.min(jnp.where(cand == m, flat, PEER_TOPK * PEER_TOPK), axis=0, keepdims=True)
            hit = flat == ix
            cv.append(m)
            ce.append(jnp.max(jnp.where(hit, expert, -1), axis=0, keepdims=True))
            cand = jnp.where(hit, -jnp.inf, cand)
        cvc = jnp.concatenate(cv, axis=0)
        e = jnp.exp(cvc - cv[0])
        gate = e / jnp.sum(e, axis=0, keepdims=True)
        row = pl.multiple_of(hd * PEER_TOPK, PEER_TOPK)
        idxt_ref[pl.ds(row, PEER_TOPK), :] = jnp.concatenate(ce, axis=0)
        gatet_ref[pl.ds(row, PEER_TOPK), :] = gate
        return carry

    lax.fori_loop(0, PEER_HEADS, head, 0)
    idx_ref[...] = idxt_ref[...].T
    gate_ref[...] = gatet_ref[...].T


def _router(x1, norm_ffn, wq, keys, tok0, T, after):
    D = x1.shape[1]
    tr = min(ROUTE_TILE, T)
    qw = wq.shape[1]
    assert tok0 % tr == 0 and T % tr == 0
    first = tok0 // tr
    full = lambda shape: pl.BlockSpec(shape, lambda i: (0,) * len(shape))
    return pl.pallas_call(
        _router_kernel,
        grid=(T // tr,),
        in_specs=[
            pl.BlockSpec((tr, D), lambda i: (first + i, 0)),
            full((1, D)),
            full((D, qw)),
            full((2, PEER_N_KEYS, PEER_D_HALF)),
            pl.BlockSpec(memory_space=pl.ANY),
        ],
        out_specs=[
            pl.BlockSpec((tr, D), lambda i: (i, 0)),
            pl.BlockSpec((tr, EXPERTS_PER_TOKEN), lambda i: (i, 0)),
            pl.BlockSpec((tr, EXPERTS_PER_TOKEN), lambda i: (i, 0)),
        ],
        out_shape=[
            jax.ShapeDtypeStruct((T, D), jnp.bfloat16),
            jax.ShapeDtypeStruct((T, EXPERTS_PER_TOKEN), jnp.int32),
            jax.ShapeDtypeStruct((T, EXPERTS_PER_TOKEN), jnp.float32),
        ],
        scratch_shapes=[
            pltpu.VMEM((tr, qw), jnp.bfloat16),
            pltpu.VMEM((EXPERTS_PER_TOKEN, tr), jnp.int32),
            pltpu.VMEM((EXPERTS_PER_TOKEN, tr), jnp.float32),
        ],
        compiler_params=pltpu.CompilerParams(
            dimension_semantics=("parallel",), vmem_limit_bytes=TC_VMEM_LIMIT),
        name="router",
    )(x1, norm_ffn.reshape(1, D), wq, keys, after)


def _scores_kernel(h_ref, ut_ref, idx_ref, gate_ref, w_ref, pre_ref):
    j = pl.program_id(1)
    acc = jnp.dot(h_ref[...], ut_ref[...], preferred_element_type=jnp.float32)
    groups = acc.shape[1] // V7X_LANES
    lane_bits = int(math.log2(V7X_LANES))
    idx = idx_ref[...]
    group, lane = idx >> lane_bits, idx & (V7X_LANES - 1)
    picked = jnp.zeros(idx.shape, jnp.float32)
    for n in range(groups):
        vals = jnp.take_along_axis(acc[:, n * V7X_LANES:(n + 1) * V7X_LANES], lane, axis=1)
        picked = jnp.where(group == j * groups + n, vals, picked)

    @pl.when(j == 0)
    def _():
        pre_ref[...] = picked

    @pl.when(j > 0)
    def _():
        pre_ref[...] += picked

    @pl.when(j == pl.num_programs(1) - 1)
    def _():
        w = (gate_ref[...] * _gelu(pre_ref[...])).astype(jnp.bfloat16).astype(jnp.float32)
        hi = pltpu.bitcast(w, jnp.uint32) & jnp.uint32(HI_HALF)
        w_ref[...] = pltpu.bitcast(hi | (hi >> 16), jnp.int32)


def _expert_weights(h2, u_t, idx, gate):
    T, D = h2.shape
    E = u_t.shape[1]
    K = idx.shape[1]
    tm, tn = min(SCORE_TOK_TILE, T), min(SCORE_EXP_TILE, E)
    assert T % tm == 0 and E % tn == 0 and tn % V7X_LANES == 0 and K == V7X_LANES
    per_token = pl.BlockSpec((tm, K), lambda i, j: (i, 0))
    return pl.pallas_call(
        _scores_kernel,
        grid=(T // tm, E // tn),
        in_specs=[pl.BlockSpec((tm, D), lambda i, j: (i, 0)), pl.BlockSpec((D, tn), lambda i, j: (0, j)),
                  per_token, per_token],
        out_specs=per_token,
        out_shape=jax.ShapeDtypeStruct((T, K), jnp.int32),
        scratch_shapes=[pltpu.VMEM((tm, K), jnp.float32)],
        compiler_params=pltpu.CompilerParams(
            dimension_semantics=("parallel", "arbitrary"), vmem_limit_bytes=TC_VMEM_LIMIT),
        name="expert_weights",
    )(h2, u_t, idx, gate)


def _final_kernel(x_ref, y_ref, g_ref, *rest):
    o_ref = rest[-1]
    o_ref[...] = _rms(x_ref[...] + y_ref[...], g_ref[...])


def _final(x1, x_tok0, peer, norm_final, out_prev, out_tok0, total):
    Tc, D = peer.shape
    te = min(EW_TILE, Tc)
    assert x_tok0 % te == 0 and out_tok0 % te == 0 and Tc % te == 0
    nblk = Tc // te
    spec = pl.BlockSpec((te, D), lambda i: (out_tok0 // te + i, 0))
    in_specs = [pl.BlockSpec((te, D), lambda i: (x_tok0 // te + i, 0)),
                pl.BlockSpec((te, D), lambda i: (i, 0)), pl.BlockSpec((1, D), lambda i: (0, 0))]
    args = [x1, peer, norm_final.reshape(1, D)]
    aliases = {}
    if out_prev is not None:
        in_specs.append(pl.BlockSpec(memory_space=pl.ANY))
        args.append(out_prev)
        aliases = {3: 0}
    return pl.pallas_call(
        _final_kernel, grid=(nblk,), in_specs=in_specs, out_specs=spec,
        out_shape=jax.ShapeDtypeStruct((total, D), jnp.float32),
        input_output_aliases=aliases,
        compiler_params=pltpu.CompilerParams(dimension_semantics=("parallel",)),
        name="final_norm",
    )(*args)


def _tree_sum(vals):
    while len(vals) > 1:
        nxt = [vals[i] + vals[i + 1] for i in range(0, len(vals) - 1, 2)]
        if len(vals) % 2:
            nxt.append(vals[-1])
        vals = nxt
    return vals[0]


def _sc_block_pipeline(nblk, items_per_token, loads, store, gather, compute):
    ipt = items_per_token
    assert SC_ROW_BUFS > ipt and nblk >= 1

    for c in loads(0, 0):
        c.start()
    for c in loads(0, 0):
        c.wait()
    if nblk > 1:
        for c in loads(1, 1):
            c.start()
    for q in range(ipt):
        gather(0, 0, q, q).start()

    @pl.loop(0, nblk)
    def _(b):
        slot = b % 2

        @pl.when(b >= 2)
        def _():
            store(b - 2, slot).wait()

        @pl.loop(0, SC_TOKENS)
        def _(t):
            item0 = (b * SC_TOKENS + t) * ipt
            for q in range(ipt):
                nxt_buf = (item0 + q + ipt) % SC_ROW_BUFS

                @pl.when(t + 1 < SC_TOKENS)
                def _():
                    gather(slot, t + 1, q, nxt_buf).start()

                @pl.when(jnp.logical_and(t + 1 == SC_TOKENS, b + 1 < nblk))
                def _():
                    if q == 0:
                        for c in loads(b + 1, 1 - slot):
                            c.wait()
                    gather(1 - slot, 0, q, nxt_buf).start()

                buf = (item0 + q) % SC_ROW_BUFS
                gather(slot, t, q, buf).wait()
                compute(slot, t, q, buf)

        store(b, slot).start()

        @pl.when(b + 2 < nblk)
        def _():
            for c in loads(b + 2, slot):
                c.start()

    if nblk >= 2:
        store(nblk - 2, nblk % 2).wait()
    store(nblk - 1, (nblk - 1) % 2).wait()


def _sc_mesh():
    return plsc.VectorSubcoreMesh(core_axis_name="c", subcore_axis_name="s")


def _sc_worker_id():
    return lax.axis_index("s") * V7X_SC_CORES + lax.axis_index("c")


def _sc_bf16(words):
    return plsc.bitcast(words, jnp.bfloat16)


def _sc_halves_f32(pairs):
    words = plsc.bitcast(pairs, jnp.uint32)
    return (plsc.bitcast(words << 16, jnp.float32),
            plsc.bitcast(words & jnp.uint32(HI_HALF), jnp.float32))


def _mix_parts(T, K, DW):
    D = 2 * DW
    L, G = V7X_SC_LANES, SC_GATHER
    nj = SC_MIX_CHUNK // L
    tok_per_w = T // V7X_SC_WORKERS
    assert T % (V7X_SC_WORKERS * SC_TOKENS) == 0 and K % (2 * G) == 0
    assert DW % SC_MIX_CHUNK == 0 and G % SC_MIX_GROUP == 0
    scratch = [
        pltpu.VMEM((2, SC_TOKENS, K), jnp.int32),
        pltpu.VMEM((2, SC_TOKENS, K), jnp.int32),
        pltpu.VMEM((SC_ROW_BUFS, G, DW), jnp.uint32),
        pltpu.VMEM((2, SC_TOKENS, D), jnp.float32),
        pltpu.SemaphoreType.DMA((2,)),
        pltpu.SemaphoreType.DMA((2,)),
        pltpu.SemaphoreType.DMA((2,)),
        pltpu.SemaphoreType.DMA((SC_ROW_BUFS,)),
    ]

    def run(w_hbm, idx_hbm, tab_hbm, out_hbm, idx_v, w_v, rows_v, out_v, idx_sems, w_sems, out_sems, row_sems):
        base = _sc_worker_id() * tok_per_w

        def loads(b, slot):
            toks = pl.ds(base + b * SC_TOKENS, SC_TOKENS)
            return [pltpu.make_async_copy(idx_hbm.at[toks], idx_v.at[slot], idx_sems.at[slot]),
                    pltpu.make_async_copy(w_hbm.at[toks], w_v.at[slot], w_sems.at[slot])]

        def store(b, slot):
            toks = pl.ds(base + b * SC_TOKENS, SC_TOKENS)
            return pltpu.make_async_copy(out_v.at[slot], out_hbm.at[toks], out_sems.at[slot])

        def gather(slot, t, q, buf):
            return pltpu.make_async_copy(
                tab_hbm.at[idx_v.at[slot, t, pl.ds(q * G, G)]], rows_v.at[buf], row_sems.at[buf])

        def compute(slot, t, q, buf):
            ssplat = jnp.full((L,), slot, jnp.int32)
            tsplat = jnp.full((L,), t, jnp.int32)
            for c in range(DW // SC_MIX_CHUNK):
                def body(kg, acc):
                    kk = kg * SC_MIX_GROUP
                    wks = [_sc_bf16(plsc.load_gather(
                        w_v, [ssplat, tsplat, jnp.full((L,), q * G + i, jnp.int32) + kk]))
                        for i in range(SC_MIX_GROUP)]
                    out = []
                    for j in range(nj):
                        prods = [wks[i] * _sc_bf16(rows_v[buf, kk + i, pl.ds(c * SC_MIX_CHUNK + j * L, L)])
                                 for i in range(SC_MIX_GROUP)]
                        lo, hi = _sc_halves_f32(_tree_sum(prods))
                        out += [acc[2 * j] + lo, acc[2 * j + 1] + hi]
                    return tuple(out)

                zero = jnp.zeros((L,), jnp.float32)
                acc = plsc.parallel_loop(0, G // SC_MIX_GROUP, carry=(zero,) * (2 * nj))(body)
                for j in range(nj):
                    for half in range(2):
                        dst = out_v.at[slot, t, pl.ds(half * DW + c * SC_MIX_CHUNK + j * L, L)]
                        if q == 0:
                            dst[...] = acc[2 * j + half]
                        else:
                            plsc.addupdate(dst, acc[2 * j + half])

        _sc_block_pipeline(tok_per_w // SC_TOKENS, K // G, loads, store, gather, compute)

    return scratch, run


def _sc_kernel(out_type, scratch_types, name):
    return functools.partial(
        pl.kernel, mesh=_sc_mesh(), out_type=out_type, scratch_types=scratch_types,
        compiler_params=pltpu.CompilerParams(needs_layout_passes=False), name=name)


def _pack_table(a):
    n, c = a.shape
    half = c // 2
    L, rb = V7X_SC_LANES, SC_PACK_ROWS
    rows_per_w = n // V7X_SC_WORKERS
    nblk = rows_per_w // rb
    assert n % (V7X_SC_WORKERS * rb) == 0 and half % L == 0

    @_sc_kernel(jax.ShapeDtypeStruct((n, half), jnp.uint32),
                [pltpu.VMEM((2, rb, c), jnp.float32), pltpu.VMEM((2, rb, half), jnp.uint32),
                 pltpu.SemaphoreType.DMA((2,)), pltpu.SemaphoreType.DMA((2,))], "pack_table")
    def k(a_hbm, o_hbm, a_v, o_v, in_sems, out_sems):
        base = _sc_worker_id() * rows_per_w

        def load(b, slot):
            return pltpu.make_async_copy(a_hbm.at[pl.ds(base + b * rb, rb)], a_v.at[slot], in_sems.at[slot])

        def store(b, slot):
            return pltpu.make_async_copy(o_v.at[slot], o_hbm.at[pl.ds(base + b * rb, rb)], out_sems.at[slot])

        load(0, 0).start()

        @pl.loop(0, nblk)
        def _(b):
            slot = b % 2
            load(b, slot).wait()

            @pl.when(b + 1 < nblk)
            def _():
                load(b + 1, 1 - slot).start()

            @pl.when(b >= 2)
            def _():
                store(b - 2, slot).wait()

            @pl.loop(0, rb)
            def _(r):
                for j in range(half // L):
                    pairs = plsc.pack(a_v[slot, r, pl.ds(j * L, L)], a_v[slot, r, pl.ds(half + j * L, L)],
                                      format=plsc.PackFormat.INTERLEAVED)
                    o_v[slot, r, pl.ds(j * L, L)] = plsc.bitcast(pairs, jnp.uint32)

            store(b, slot).start()

        if nblk >= 2:
            store(nblk - 2, nblk % 2).wait()
        store(nblk - 1, (nblk - 1) % 2).wait()

    return k(a)


def _expert_mix(w, idx, table):
    T, K = w.shape
    DW = table.shape[1]
    scratch, run = _mix_parts(T, K, DW)

    @_sc_kernel(jax.ShapeDtypeStruct((T, 2 * DW), jnp.float32), scratch, "expert_mix")
    def k(w_hbm, idx_hbm, tab_hbm, out_hbm, *mix_scratch):
        run(w_hbm, idx_hbm, tab_hbm, out_hbm, *mix_scratch)

    return k(w, idx, table)


def _chunk_sizes(total):
    ramp, size = [EDGE_CHUNK], EDGE_CHUNK
    while size < MAX_CHUNK:
        ramp.append(size)
        size *= 2
    middle = total - 2 * sum(ramp)
    assert middle >= 0 and middle % MAX_CHUNK == 0
    return ramp + [MAX_CHUNK] * (middle // MAX_CHUNK) + ramp[::-1]


def kernel(x, norm_mix, w_in, pool_w, pool_scale, sgu_ln_g, sgu_ln_b, sgu_w, sgu_b, out_norm_pool,
           out_norm_sgu, w_out, norm_ffn, peer_wq, peer_keys, peer_u, peer_v, norm_final):
    B, S, D = x.shape
    assert norm_mix.shape[0] == 1, "single-layer block"
    T = B * S
    mix_args = (norm_mix[0], w_in[0], pool_w[0], pool_scale[0], sgu_ln_g[0], sgu_ln_b[0],
                sgu_w[0], sgu_b[0], out_norm_pool[0], out_norm_sgu[0], w_out[0])
    x1_parts = [(b * S, _mixer(x, b, 1, *mix_args).reshape(S, D)) for b in range(B)]
    wq = peer_wq[0].astype(jnp.bfloat16)
    keys = peer_keys[0].astype(jnp.bfloat16)
    u_t = _transposed_bf16(peer_u[0])
    v_tab = _pack_table(peer_v[0])
    out = None
    tok0 = 0
    peers = [norm_ffn, v_tab]
    for tc in _chunk_sizes(T):
        part0, x1 = [p for p in x1_parts if p[0] <= tok0][-1]
        assert tok0 + tc <= part0 + x1.shape[0], "a token chunk must lie inside one mixer call"
        h2, idx, gate = _router(x1, norm_ffn[0], wq, keys, tok0 - part0, tc, peers[-2])
        w = _expert_weights(h2, u_t, idx, gate)
        peer = _expert_mix(w, idx, v_tab)
        peers.append(peer)
        out = _final(x1, tok0 - part0, peer, norm_final, out, tok0, T)
        tok0 += tc
    return out.reshape(B, S, D)
```

```python
import functools
import math

import jax
import jax.numpy as jnp
from jax import lax
from jax.experimental import pallas as pl
from jax.experimental.pallas import tpu as pltpu
from jax.experimental.pallas import tpu_sc as plsc

POOL_WINDOWS = (2, 4, 8, 16)
N_POOL_GROUPS = len(POOL_WINDOWS)
SGU_HEADS = 4
SGU_CHUNK = 128
PEER_HEADS = 8
PEER_N_KEYS = 128
PEER_D_HALF = 128
PEER_TOPK = 16
NORM_EPS = 1e-6
EXPERTS_PER_TOKEN = PEER_HEADS * PEER_TOPK

V7X_LANES = 128
V7X_SUBLANES = 8
V7X_SC_CORES = 2
V7X_SC_SUBCORES = 16
V7X_SC_LANES = 16
V7X_SC_WORKERS = V7X_SC_CORES * V7X_SC_SUBCORES

HALO = max(POOL_WINDOWS)
MIX_TILE = 512
ROUTE_TILE = 1024
EW_TILE = 512
SCORE_TOK_TILE = 2048
SCORE_EXP_TILE = 2048
SC_PACK_ROWS = 16
SC_GATHER = 64
SC_TOKENS = 8
SC_ROW_BUFS = 3
SC_MIX_CHUNK = 128
SC_MIX_GROUP = 4
HI_HALF = 0xFFFF0000
V7X_VMEM_BYTES = 64 * 1024 * 1024
TC_VMEM_LIMIT = V7X_VMEM_BYTES * 3 // 4
EDGE_CHUNK = 512
MAX_CHUNK = 2048


def _rms(x, g):
    inv = lax.rsqrt(jnp.mean(x * x, axis=-1, keepdims=True) + NORM_EPS)
    return x * inv * g


def _transpose_cast_kernel(a_ref, o_ref):
    o_ref[...] = a_ref[...].T.astype(o_ref.dtype)


def _transposed_bf16(a):
    n, c = a.shape
    rows = min(SCORE_EXP_TILE, n)
    return pl.pallas_call(
        _transpose_cast_kernel, grid=(n // rows,),
        in_specs=[pl.BlockSpec((rows, c), lambda i: (i, 0))],
        out_specs=pl.BlockSpec((c, rows), lambda i: (0, i)),
        out_shape=jax.ShapeDtypeStruct((c, n), jnp.bfloat16),
        compiler_params=pltpu.CompilerParams(dimension_semantics=("parallel",)),
        name="transpose_table",
    )(a)


def _gelu(x):
    return 0.5 * x * (1.0 + lax.erf(x * math.sqrt(0.5)))


def _mixer_kernel(x_ref, xh_ref, nmix_ref, win_ref, poolw_ref, pscale_ref, lng_ref, lnb_ref,
                  sguw_ref, sgub_ref, onp_ref, ons_ref, wout_ref, o_ref, pext_ref, mix_ref):
    i = pl.program_id(1)
    ts = x_ref.shape[1]
    pool_w = pscale_ref.shape[1]
    gdim = pool_w // N_POOL_GROUPS
    sgu_w = lng_ref.shape[1]
    hdim = sgu_w // SGU_HEADS

    x = x_ref[0]
    h = _rms(x, nmix_ref[...]).astype(jnp.bfloat16)
    z = jnp.dot(h, win_ref[...], preferred_element_type=jnp.float32)
    p = z[:, :pool_w]

    hh = _rms(xh_ref[0], nmix_ref[...]).astype(jnp.bfloat16)
    ph = jnp.dot(hh, win_ref[:, :pool_w], preferred_element_type=jnp.float32)
    ph = jnp.where(i > 0, ph, 0.0)
    pext_ref[0:HALO, :] = ph
    pext_ref[HALO:HALO + ts, :] = p

    pos = i * ts + lax.broadcasted_iota(jnp.int32, (ts, 1), 0)
    ssq = jnp.zeros((ts, 1), jnp.float32)
    a_parts = []
    for g, win in enumerate(POOL_WINDOWS):
        cols = slice(g * gdim, (g + 1) * gdim)
        s = pext_ref[HALO:HALO + ts, cols]
        for j in range(1, win):
            s = s + pext_ref[HALO - j:HALO - j + ts, cols]
        cnt = jnp.minimum(pos + 1, win).astype(jnp.float32)
        d = (s / cnt - p[:, cols]).astype(jnp.bfloat16)
        a = jnp.dot(d, poolw_ref[g], preferred_element_type=jnp.float32) * pscale_ref[:, cols]
        ssq = ssq + jnp.sum(a * a, axis=-1, keepdims=True)
        a_parts.append(a)
    inv_a = lax.rsqrt(ssq / pool_w + NORM_EPS)
    for g in range(N_POOL_GROUPS):
        cols = slice(g * gdim, (g + 1) * gdim)
        mix_ref[:, cols] = (a_parts[g] * inv_a * onp_ref[:, cols]).astype(jnp.bfloat16)

    gz = _gelu(z[:, pool_w:])
    tril = (lax.broadcasted_iota(jnp.int32, (SGU_CHUNK, SGU_CHUNK), 0)
            >= lax.broadcasted_iota(jnp.int32, (SGU_CHUNK, SGU_CHUNK), 1))
    ssq = jnp.zeros((ts, 1), jnp.float32)
    b_parts = []
    for hd in range(SGU_HEADS):
        cols = slice(hd * hdim, (hd + 1) * hdim)
        u = gz[:, hd * hdim:(hd + 1) * hdim]
        v = gz[:, sgu_w + hd * hdim:sgu_w + (hd + 1) * hdim]
        mu = jnp.mean(v, axis=-1, keepdims=True)
        vc = v - mu
        var = jnp.mean(vc * vc, axis=-1, keepdims=True)
        vn = (vc * lax.rsqrt(var + NORM_EPS) * lng_ref[:, cols] + lnb_ref[:, cols]).astype(jnp.bfloat16)
        w = jnp.where(tril, sguw_ref[hd], jnp.zeros((), sguw_ref.dtype))
        mixed = [jnp.dot(w, vn[n * SGU_CHUNK:(n + 1) * SGU_CHUNK], preferred_element_type=jnp.float32)
                 + sgub_ref[hd] for n in range(ts // SGU_CHUNK)]
        b = u * jnp.concatenate(mixed, axis=0)
        ssq = ssq + jnp.sum(b * b, axis=-1, keepdims=True)
        b_parts.append(b)
    inv_b = lax.rsqrt(ssq / sgu_w + NORM_EPS)
    for hd in range(SGU_HEADS):
        cols = slice(hd * hdim, (hd + 1) * hdim)
        mix_ref[:, pool_w + hd * hdim:pool_w + (hd + 1) * hdim] = (
            b_parts[hd] * inv_b * ons_ref[:, cols]).astype(jnp.bfloat16)

    o_ref[0] = x + jnp.dot(mix_ref[...], wout_ref[...], preferred_element_type=jnp.float32)


def _mixer(x, b0, nb, norm_mix, w_in, pool_w, pool_scale, ln_g, ln_b, sgu_w, sgu_b, on_pool, on_sgu, w_out):
    _, S, D = x.shape
    ts = min(MIX_TILE, S)
    pool_width = pool_scale.size
    sgu_width = ln_g.size
    in_width = w_in.shape[1]
    gdim = pool_width // N_POOL_GROUPS
    halo_blocks = ts // HALO
    full = lambda shape: pl.BlockSpec(shape, lambda b, i: (0,) * len(shape))
    return pl.pallas_call(
        _mixer_kernel,
        grid=(nb, S // ts),
        in_specs=[
            pl.BlockSpec((1, ts, D), lambda b, i: (b0 + b, i, 0)),
            pl.BlockSpec((1, HALO, D), lambda b, i: (b0 + b, jnp.maximum(i * halo_blocks - 1, 0), 0)),
            full((1, D)),
            full((D, in_width)),
            full((N_POOL_GROUPS, gdim, gdim)),
            full((1, pool_width)),
            full((1, sgu_width)),
            full((1, sgu_width)),
            full((SGU_HEADS, SGU_CHUNK, SGU_CHUNK)),
            full((SGU_HEADS, SGU_CHUNK, SGU_CHUNK)),
            full((1, pool_width)),
            full((1, sgu_width)),
            full((pool_width + sgu_width, D)),
        ],
        out_specs=pl.BlockSpec((1, ts, D), lambda b, i: (b, i, 0)),
        out_shape=jax.ShapeDtypeStruct((nb, S, D), jnp.float32),
        scratch_shapes=[
            pltpu.VMEM((HALO + ts, pool_width), jnp.float32),
            pltpu.VMEM((ts, pool_width + sgu_width), jnp.bfloat16),
        ],
        compiler_params=pltpu.CompilerParams(
            dimension_semantics=("parallel", "arbitrary"), vmem_limit_bytes=TC_VMEM_LIMIT),
        name="mixer",
    )(x, x, norm_mix.reshape(1, D), w_in.astype(jnp.bfloat16), pool_w.astype(jnp.bfloat16),
      pool_scale.reshape(1, pool_width), ln_g.reshape(1, sgu_width), ln_b.reshape(1, sgu_width),
      sgu_w.astype(jnp.bfloat16),
      jnp.broadcast_to(sgu_b[:, :, None], (SGU_HEADS, SGU_CHUNK, SGU_CHUNK)),
      on_pool.reshape(1, pool_width), on_sgu.reshape(1, sgu_width), w_out.astype(jnp.bfloat16))


def _topk_rows(s, k):
    sub = V7X_SUBLANES
    n, m = s.shape
    pieces = n // sub
    assert n % sub == 0 and pieces >= k, "lists shorter than k would need padding"
    row = lax.broadcasted_iota(jnp.int32, (sub, m), 0)
    v = [s[p * sub:(p + 1) * sub] for p in range(pieces)]
    ix = [row + p * sub for p in range(pieces)]
    for rnd in range(pieces):
        for p in range(rnd % 2, pieces - 1, 2):
            swap = v[p + 1] > v[p]
            v[p], v[p + 1] = jnp.where(swap, v[p + 1], v[p]), jnp.where(swap, v[p], v[p + 1])
            ix[p], ix[p + 1] = jnp.where(swap, ix[p + 1], ix[p]), jnp.where(swap, ix[p], ix[p + 1])
    vals, idxs = [], []
    for r in range(k):
        best = jnp.max(v[0], axis=0, keepdims=True)
        best_ix = jnp.min(jnp.where(v[0] == best, ix[0], n), axis=0, keepdims=True)
        vals.append(best)
        idxs.append(best_ix)
        taken = ix[0] == best_ix
        for p in range(k - 1 - r):
            v[p] = jnp.where(taken, v[p + 1], v[p])
            ix[p] = jnp.where(taken, ix[p + 1], ix[p])
    return vals, idxs


def _pair_candidates(v1, i1, v2, i2):
    k, sub = PEER_TOPK, V7X_SUBLANES
    assert k == 16 and sub == 8, "the tile plan below is written for 16 candidates per side"
    v1c, i1c = jnp.concatenate(v1, axis=0), jnp.concatenate(i1, axis=0)
    v2c, i2c = jnp.concatenate(v2, axis=0), jnp.concatenate(i2, axis=0)
    m = v2c.shape[1]
    r = lax.broadcasted_iota(jnp.int32, (sub, m), 0)

    def with_b(a, b0, row0):
        shift = lambda x: x if row0 == 0 else pltpu.roll(x, row0, 0)
        return (v1[a] + shift(v2c[b0:b0 + sub]), i1[a] * PEER_N_KEYS + shift(i2c[b0:b0 + sub]),
                a * k + b0 + (r - row0))

    def with_a(a0, row0):
        s = (row0 - (a0 - sub)) % sub
        shift = lambda x: x if s == 0 else pltpu.roll(x, s, 0)
        return (shift(v1c[sub:]) + v2[0], shift(i1c[sub:]) * PEER_N_KEYS + i2[0], (a0 + (r - row0)) * k)

    def tile(*parts):
        out = parts[0][1]
        for row0, seg in parts[1:]:
            out = tuple(jnp.where(r < row0, o, s) for o, s in zip(out, seg))
        return out

    tiles = [
        with_b(0, 0, 0), with_b(0, sub, 0),
        with_b(1, 0, 0),
        tile((0, with_b(2, 0, 0)), (5, with_b(4, 0, 5))),
        tile((0, with_b(3, 0, 0)), (4, with_b(5, 0, 4)), (6, with_b(6, 0, 6))),
        tile((0, with_b(7, 0, 0)), (2, with_a(8, 2))),
        tile((0, with_a(14, 0)), (2, with_b(1, sub, 2))),
    ]
    return tuple(jnp.concatenate([t[n] for t in tiles], axis=0) for n in range(3))


def _router_kernel(x_ref, nffn_ref, wq_ref, keys_ref, after_ref, h2_ref, idx_ref, gate_ref,
                   q_ref, idxt_ref, gatet_ref):
    del after_ref
    h2 = _rms(x_ref[...], nffn_ref[...]).astype(jnp.bfloat16)
    h2_ref[...] = h2
    q_ref[...] = jnp.dot(h2, wq_ref[...], preferred_element_type=jnp.float32).astype(jnp.bfloat16)
    dq = 2 * PEER_D_HALF
    nt = (((1,), (1,)), ((), ()))

    def head(hd, carry):
        off = pl.multiple_of(hd * dq, dq)
        s1 = lax.dot_general(keys_ref[0], q_ref[:, pl.ds(off, PEER_D_HALF)], nt,
                             preferred_element_type=jnp.float32)
        s2 = lax.dot_general(keys_ref[1], q_ref[:, pl.ds(off + PEER_D_HALF, PEER_D_HALF)], nt,
                             preferred_element_type=jnp.float32)
        v1, i1 = _topk_rows(s1, PEER_TOPK)
        v2, i2 = _topk_rows(s2, PEER_TOPK)
        cand, expert, flat = _pair_candidates(v1, i1, v2, i2)
        cv, ce = [], []
        for _ in range(PEER_TOPK):
            m = jnp.max(cand, axis=0, keepdims=True)
            ix = jnp.min(jnp.where(cand == m, flat, PEER_TOPK * PEER_TOPK), axis=0, keepdims=True)
            hit = flat == ix
            cv.append(m)
            ce.append(jnp.max(jnp.where(hit, expert, -1), axis=0, keepdims=True))
            cand = jnp.where(hit, -jnp.inf, cand)
        cvc = jnp.concatenate(cv, axis=0)
        e = jnp.exp(cvc - cv[0])
        gate = e / jnp.sum(e, axis=0, keepdims=True)
        row = pl.multiple_of(hd * PEER_TOPK, PEER_TOPK)
        idxt_ref[pl.ds(row, PEER_TOPK), :] = jnp.concatenate(ce, axis=0)
        gatet_ref[pl.ds(row, PEER_TOPK), :] = gate
        return carry

    lax.fori_loop(0, PEER_HEADS, head, 0)
    idx_ref[...] = idxt_ref[...].T
    gate_ref[...] = gatet_ref[...].T


def _router(x1, norm_ffn, wq, keys, tok0, T, after):
    D = x1.shape[1]
    tr = min(ROUTE_TILE, T)
    qw = wq.shape[1]
    assert tok0 % tr == 0 and T % tr == 0
    first = tok0 // tr
    full = lambda shape: pl.BlockSpec(shape, lambda i: (0,) * len(shape))
    return pl.pallas_call(
        _router_kernel,
        grid=(T // tr,),
        in_specs=[
            pl.BlockSpec((tr, D), lambda i: (first + i, 0)),
            full((1, D)),
            full((D, qw)),
            full((2, PEER_N_KEYS, PEER_D_HALF)),
            pl.BlockSpec(memory_space=pl.ANY),
        ],
        out_specs=[
            pl.BlockSpec((tr, D), lambda i: (i, 0)),
            pl.BlockSpec((tr, EXPERTS_PER_TOKEN), lambda i: (i, 0)),
            pl.BlockSpec((tr, EXPERTS_PER_TOKEN), lambda i: (i, 0)),
        ],
        out_shape=[
            jax.ShapeDtypeStruct((T, D), jnp.bfloat16),
            jax.ShapeDtypeStruct((T, EXPERTS_PER_TOKEN), jnp.int32),
            jax.ShapeDtypeStruct((T, EXPERTS_PER_TOKEN), jnp.float32),
        ],
        scratch_shapes=[
            pltpu.VMEM((tr, qw), jnp.bfloat16),
            pltpu.VMEM((EXPERTS_PER_TOKEN, tr), jnp.int32),
            pltpu.VMEM((EXPERTS_PER_TOKEN, tr), jnp.float32),
        ],
        compiler_params=pltpu.CompilerParams(
            dimension_semantics=("parallel",), vmem_limit_bytes=TC_VMEM_LIMIT),
        name="router",
    )(x1, norm_ffn.reshape(1, D), wq, keys, after)


def _scores_kernel(h_ref, ut_ref, idx_ref, gate_ref, w_ref, pre_ref):
    j = pl.program_id(1)
    acc = jnp.dot(h_ref[...], ut_ref[...], preferred_element_type=jnp.float32)
    groups = acc.shape[1] // V7X_LANES
    lane_bits = int(math.log2(V7X_LANES))
    idx = idx_ref[...]
    group, lane = idx >> lane_bits, idx & (V7X_LANES - 1)
    picked = jnp.zeros(idx.shape, jnp.float32)
    for n in range(groups):
        vals = jnp.take_along_axis(acc[:, n * V7X_LANES:(n + 1) * V7X_LANES], lane, axis=1)
        picked = jnp.where(group == j * groups + n, vals, picked)

    @pl.when(j == 0)
    def _():
        pre_ref[...] = picked

    @pl.when(j > 0)
    def _():
        pre_ref[...] += picked

    @pl.when(j == pl.num_programs(1) - 1)
    def _():
        w = (gate_ref[...] * _gelu(pre_ref[...])).astype(jnp.bfloat16).astype(jnp.float32)
        hi = pltpu.bitcast(w, jnp.uint32) & jnp.uint32(HI_HALF)
        w_ref[...] = pltpu.bitcast(hi | (hi >> 16), jnp.int32)


def _expert_weights(h2, u_t, idx, gate):
    T, D = h2.shape
    E = u_t.shape[1]
    K = idx.shape[1]
    tm, tn = min(SCORE_TOK_TILE, T), min(SCORE_EXP_TILE, E)
    assert T % tm == 0 and E % tn == 0 and tn % V7X_LANES == 0 and K == V7X_LANES
    per_token = pl.BlockSpec((tm, K), lambda i, j: (i, 0))
    return pl.pallas_call(
        _scores_kernel,
        grid=(T // tm, E // tn),
        in_specs=[pl.BlockSpec((tm, D), lambda i, j: (i, 0)), pl.BlockSpec((D, tn), lambda i, j: (0, j)),
                  per_token, per_token],
        out_specs=per_token,
        out_shape=jax.ShapeDtypeStruct((T, K), jnp.int32),
        scratch_shapes=[pltpu.VMEM((tm, K), jnp.float32)],
        compiler_params=pltpu.CompilerParams(
            dimension_semantics=("parallel", "arbitrary"), vmem_limit_bytes=TC_VMEM_LIMIT),
        name="expert_weights",
    )(h2, u_t, idx, gate)


def _final_kernel(x_ref, y_ref, g_ref, *rest):
    o_ref = rest[-1]
    o_ref[...] = _rms(x_ref[...] + y_ref[...], g_ref[...])


def _final(x1, x_tok0, peer, norm_final, out_prev, out_tok0, total):
    Tc, D = peer.shape
    te = min(EW_TILE, Tc)
    assert x_tok0 % te == 0 and out_tok0 % te == 0 and Tc % te == 0
    nblk = Tc // te
    spec = pl.BlockSpec((te, D), lambda i: (out_tok0 // te + i, 0))
    in_specs = [pl.BlockSpec((te, D), lambda i: (x_tok0 // te + i, 0)),
                pl.BlockSpec((te, D), lambda i: (i, 0)), pl.BlockSpec((1, D), lambda i: (0, 0))]
    args = [x1, peer, norm_final.reshape(1, D)]
    aliases = {}
    if out_prev is not None:
        in_specs.append(pl.BlockSpec(memory_space=pl.ANY))
        args.append(out_prev)
        aliases = {3: 0}
    return pl.pallas_call(
        _final_kernel, grid=(nblk,), in_specs=in_specs, out_specs=spec,
        out_shape=jax.ShapeDtypeStruct((total, D), jnp.float32),
        input_output_aliases=aliases,
        compiler_params=pltpu.CompilerParams(dimension_semantics=("parallel",)),
        name="final_norm",
    )(*args)


def _tree_sum(vals):
    while len(vals) > 1:
        nxt = [vals[i] + vals[i + 1] for i in range(0, len(vals) - 1, 2)]
        if len(vals) % 2:
            nxt.append(vals[-1])
        vals = nxt
    return vals[0]


def _sc_block_pipeline(nblk, items_per_token, loads, store, gather, compute):
    ipt = items_per_token
    assert SC_ROW_BUFS > ipt and nblk >= 1

    for c in loads(0, 0):
        c.start()
    for c in loads(0, 0):
        c.wait()
    if nblk > 1:
        for c in loads(1, 1):
            c.start()
    for q in range(ipt):
        gather(0, 0, q, q).start()

    @pl.loop(0, nblk)
    def _(b):
        slot = b % 2

        @pl.when(b >= 2)
        def _():
            store(b - 2, slot).wait()

        @pl.loop(0, SC_TOKENS)
        def _(t):
            item0 = (b * SC_TOKENS + t) * ipt
            for q in range(ipt):
                nxt_buf = (item0 + q + ipt) % SC_ROW_BUFS

                @pl.when(t + 1 < SC_TOKENS)
                def _():
                    gather(slot, t + 1, q, nxt_buf).start()

                @pl.when(jnp.logical_and(t + 1 == SC_TOKENS, b + 1 < nblk))
                def _():
                    if q == 0:
                        for c in loads(b + 1, 1 - slot):
                            c.wait()
                    gather(1 - slot, 0, q, nxt_buf).start()

                buf = (item0 + q) % SC_ROW_BUFS
                gather(slot, t, q, buf).wait()
                compute(slot, t, q, buf)

        store(b, slot).start()

        @pl.when(b + 2 < nblk)
        def _():
            for c in loads(b + 2, slot):
                c.start()

    if nblk >= 2:
        store(nblk - 2, nblk % 2).wait()
    store(nblk - 1, (nblk - 1) % 2).wait()


def _sc_mesh():
    return plsc.VectorSubcoreMesh(core_axis_name="c", subcore_axis_name="s")


def _sc_worker_id():
    return lax.axis_index("s") * V7X_SC_CORES + lax.axis_index("c")


def _sc_bf16(words):
    return plsc.bitcast(words, jnp.bfloat16)


def _sc_halves_f32(pairs):
    words = plsc.bitcast(pairs, jnp.uint32)
    return (plsc.bitcast(words << 16, jnp.float32),
            plsc.bitcast(words & jnp.uint32(HI_HALF), jnp.float32))


def _mix_parts(T, K, DW):
    D = 2 * DW
    L, G = V7X_SC_LANES, SC_GATHER
    nj = SC_MIX_CHUNK // L
    tok_per_w = T // V7X_SC_WORKERS
    assert T % (V7X_SC_WORKERS * SC_TOKENS) == 0 and K % (2 * G) == 0
    assert DW % SC_MIX_CHUNK == 0 and G % SC_MIX_GROUP == 0
    scratch = [
        pltpu.VMEM((2, SC_TOKENS, K), jnp.int32),
        pltpu.VMEM((2, SC_TOKENS, K), jnp.int32),
        pltpu.VMEM((SC_ROW_BUFS, G, DW), jnp.uint32),
        pltpu.VMEM((2, SC_TOKENS, D), jnp.float32),
        pltpu.SemaphoreType.DMA((2,)),
        pltpu.SemaphoreType.DMA((2,)),
        pltpu.SemaphoreType.DMA((2,)),
        pltpu.SemaphoreType.DMA((SC_ROW_BUFS,)),
    ]

    def run(w_hbm, idx_hbm, tab_hbm, out_hbm, idx_v, w_v, rows_v, out_v, idx_sems, w_sems, out_sems, row_sems):
        base = _sc_worker_id() * tok_per_w

        def loads(b, slot):
            toks = pl.ds(base + b * SC_TOKENS, SC_TOKENS)
            return [pltpu.make_async_copy(idx_hbm.at[toks], idx_v.at[slot], idx_sems.at[slot]),
                    pltpu.make_async_copy(w_hbm.at[toks], w_v.at[slot], w_sems.at[slot])]

        def store(b, slot):
            toks = pl.ds(base + b * SC_TOKENS, SC_TOKENS)
            return pltpu.make_async_copy(out_v.at[slot], out_hbm.at[toks], out_sems.at[slot])

        def gather(slot, t, q, buf):
            return pltpu.make_async_copy(
                tab_hbm.at[idx_v.at[slot, t, pl.ds(q * G, G)]], rows_v.at[buf], row_sems.at[buf])

        def compute(slot, t, q, buf):
            ssplat = jnp.full((L,), slot, jnp.int32)
            tsplat = jnp.full((L,), t, jnp.int32)
            for c in range(DW // SC_MIX_CHUNK):
                def body(kg, acc):
                    kk = kg * SC_MIX_GROUP
                    wks = [_sc_bf16(plsc.load_gather(
                        w_v, [ssplat, tsplat, jnp.full((L,), q * G + i, jnp.int32) + kk]))
                        for i in range(SC_MIX_GROUP)]
                    out = []
                    for j in range(nj):
                        prods = [wks[i] * _sc_bf16(rows_v[buf, kk + i, pl.ds(c * SC_MIX_CHUNK + j * L, L)])
                                 for i in range(SC_MIX_GROUP)]
                        lo, hi = _sc_halves_f32(_tree_sum(prods))
                        out += [acc[2 * j] + lo, acc[2 * j + 1] + hi]
                    return tuple(out)

                zero = jnp.zeros((L,), jnp.float32)
                acc = plsc.parallel_loop(0, G // SC_MIX_GROUP, carry=(zero,) * (2 * nj))(body)
                for j in range(nj):
                    for half in range(2):
                        dst = out_v.at[slot, t, pl.ds(half * DW + c * SC_MIX_CHUNK + j * L, L)]
                        if q == 0:
                            dst[...] = acc[2 * j + half]
                        else:
                            plsc.addupdate(dst, acc[2 * j + half])

        _sc_block_pipeline(tok_per_w // SC_TOKENS, K // G, loads, store, gather, compute)

    return scratch, run


def _sc_kernel(out_type, scratch_types, name):
    return functools.partial(
        pl.kernel, mesh=_sc_mesh(), out_type=out_type, scratch_types=scratch_types,
        compiler_params=pltpu.CompilerParams(needs_layout_passes=False), name=name)


def _pack_table(a):
    n, c = a.shape
    half = c // 2
    L, rb = V7X_SC_LANES, SC_PACK_ROWS
    rows_per_w = n // V7X_SC_WORKERS
    nblk = rows_per_w // rb
    assert n % (V7X_SC_WORKERS * rb) == 0 and half % L == 0

    @_sc_kernel(jax.ShapeDtypeStruct((n, half), jnp.uint32),
                [pltpu.VMEM((2, rb, c), jnp.float32), pltpu.VMEM((2, rb, half), jnp.uint32),
                 pltpu.SemaphoreType.DMA((2,)), pltpu.SemaphoreType.DMA((2,))], "pack_table")
    def k(a_hbm, o_hbm, a_v, o_v, in_sems, out_sems):
        base = _sc_worker_id() * rows_per_w

        def load(b, slot):
            return pltpu.make_async_copy(a_hbm.at[pl.ds(base + b * rb, rb)], a_v.at[slot], in_sems.at[slot])

        def store(b, slot):
            return pltpu.make_async_copy(o_v.at[slot], o_hbm.at[pl.ds(base + b * rb, rb)], out_sems.at[slot])

        load(0, 0).start()

        @pl.loop(0, nblk)
        def _(b):
            slot = b % 2
            load(b, slot).wait()

            @pl.when(b + 1 < nblk)
            def _():
                load(b + 1, 1 - slot).start()

            @pl.when(b >= 2)
            def _():
                store(b - 2, slot).wait()

            @plsc.parallel_loop(0, rb)
            def _(r):
                for j in range(half // L):
                    pairs = plsc.pack(a_v[slot, r, pl.ds(j * L, L)], a_v[slot, r, pl.ds(half + j * L, L)],
                                      format=plsc.PackFormat.INTERLEAVED)
                    o_v[slot, r, pl.ds(j * L, L)] = plsc.bitcast(pairs, jnp.uint32)

            store(b, slot).start()

        if nblk >= 2:
            store(nblk - 2, nblk % 2).wait()
        store(nblk - 1, (nblk - 1) % 2).wait()

    return k(a)


def _expert_mix(w, idx, table):
    T, K = w.shape
    DW = table.shape[1]
    scratch, run = _mix_parts(T, K, DW)

    @_sc_kernel(jax.ShapeDtypeStruct((T, 2 * DW), jnp.float32), scratch, "expert_mix")
    def k(w_hbm, idx_hbm, tab_hbm, out_hbm, *mix_scratch):
        run(w_hbm, idx_hbm, tab_hbm, out_hbm, *mix_scratch)

    return k(w, idx, table)


def _chunk_sizes(total):
    ramp, size = [EDGE_CHUNK], EDGE_CHUNK
    while size < MAX_CHUNK:
        ramp.append(size)
        size *= 2
    middle = total - 2 * sum(ramp)
    assert middle >= 0 and middle % MAX_CHUNK == 0
    return ramp + [MAX_CHUNK] * (middle // MAX_CHUNK) + ramp[::-1]


def kernel(x, norm_mix, w_in, pool_w, pool_scale, sgu_ln_g, sgu_ln_b, sgu_w, sgu_b, out_norm_pool,
           out_norm_sgu, w_out, norm_ffn, peer_wq, peer_keys, peer_u, peer_v, norm_final):
    B, S, D = x.shape
    assert norm_mix.shape[0] == 1, "single-layer block"
    T = B * S
    mix_args = (norm_mix[0], w_in[0], pool_w[0], pool_scale[0], sgu_ln_g[0], sgu_ln_b[0],
                sgu_w[0], sgu_b[0], out_norm_pool[0], out_norm_sgu[0], w_out[0])
    x1_parts = [(b * S, _mixer(x, b, 1, *mix_args).reshape(S, D)) for b in range(B)]
    wq = peer_wq[0].astype(jnp.bfloat16)
    keys = peer_keys[0].astype(jnp.bfloat16)
    u_t = _transposed_bf16(peer_u[0])
    v_tab = _pack_table(peer_v[0])
    out = None
    tok0 = 0
    peers = [norm_ffn, v_tab]
    for tc in _chunk_sizes(T):
        part0, x1 = [p for p in x1_parts if p[0] <= tok0][-1]
        assert tok0 + tc <= part0 + x1.shape[0], "a token chunk must lie inside one mixer call"
        h2, idx, gate = _router(x1, norm_ffn[0], wq, keys, tok0 - part0, tc, peers[-2])
        w = _expert_weights(h2, u_t, idx, gate)
        peer = _expert_mix(w, idx, v_tab)
        peers.append(peer)
        out = _final(x1, tok0 - part0, peer, norm_final, out, tok0, T)
        tok0 += tc
    return out.reshape(B, S, D)
```

```python
import functools
import math

import jax
import jax.numpy as jnp
from jax import lax
from jax.experimental import pallas as pl
from jax.experimental.pallas import tpu as pltpu
from jax.experimental.pallas import tpu_sc as plsc

POOL_WINDOWS = (2, 4, 8, 16)
N_POOL_GROUPS = len(POOL_WINDOWS)
SGU_HEADS = 4
SGU_CHUNK = 128
PEER_HEADS = 8
PEER_N_KEYS = 128
PEER_D_HALF = 128
PEER_TOPK = 16
NORM_EPS = 1e-6
EXPERTS_PER_TOKEN = PEER_HEADS * PEER_TOPK

V7X_LANES = 128
V7X_SUBLANES = 8
V7X_SC_CORES = 2
V7X_SC_SUBCORES = 16
V7X_SC_LANES = 16
V7X_SC_WORKERS = V7X_SC_CORES * V7X_SC_SUBCORES

HALO = max(POOL_WINDOWS)
MIX_TILE = 512
ROUTE_TILE = 1024
EW_TILE = 512
SCORE_TOK_TILE = 2048
SCORE_EXP_TILE = 2048
SC_PACK_ROWS = 16
SC_GATHER = 64
SC_TOKENS = 8
SC_ROW_BUFS = 3
SC_MIX_CHUNK = 128
SC_MIX_GROUP = 4
HI_HALF = 0xFFFF0000
V7X_VMEM_BYTES = 64 * 1024 * 1024
TC_VMEM_LIMIT = V7X_VMEM_BYTES * 3 // 4
EDGE_CHUNK = 512
MAX_CHUNK = 4096


def _rms(x, g):
    inv = lax.rsqrt(jnp.mean(x * x, axis=-1, keepdims=True) + NORM_EPS)
    return x * inv * g


def _transpose_cast_kernel(a_ref, o_ref):
    o_ref[...] = a_ref[...].T.astype(o_ref.dtype)


def _transposed_bf16(a):
    n, c = a.shape
    rows = min(SCORE_EXP_TILE, n)
    return pl.pallas_call(
        _transpose_cast_kernel, grid=(n // rows,),
        in_specs=[pl.BlockSpec((rows, c), lambda i: (i, 0))],
        out_specs=pl.BlockSpec((c, rows), lambda i: (0, i)),
        out_shape=jax.ShapeDtypeStruct((c, n), jnp.bfloat16),
        compiler_params=pltpu.CompilerParams(dimension_semantics=("parallel",)),
        name="transpose_table",
    )(a)


def _gelu(x):
    return 0.5 * x * (1.0 + lax.erf(x * math.sqrt(0.5)))


def _mixer_kernel(x_ref, xh_ref, nmix_ref, win_ref, poolw_ref, pscale_ref, lng_ref, lnb_ref,
                  sguw_ref, sgub_ref, onp_ref, ons_ref, wout_ref, o_ref, pext_ref, mix_ref):
    i = pl.program_id(1)
    ts = x_ref.shape[1]
    pool_w = pscale_ref.shape[1]
    gdim = pool_w // N_POOL_GROUPS
    sgu_w = lng_ref.shape[1]
    hdim = sgu_w // SGU_HEADS

    x = x_ref[0]
    h = _rms(x, nmix_ref[...]).astype(jnp.bfloat16)
    z = jnp.dot(h, win_ref[...], preferred_element_type=jnp.float32)
    p = z[:, :pool_w]

    hh = _rms(xh_ref[0], nmix_ref[...]).astype(jnp.bfloat16)
    ph = jnp.dot(hh, win_ref[:, :pool_w], preferred_element_type=jnp.float32)
    ph = jnp.where(i > 0, ph, 0.0)
    pext_ref[0:HALO, :] = ph
    pext_ref[HALO:HALO + ts, :] = p

    pos = i * ts + lax.broadcasted_iota(jnp.int32, (ts, 1), 0)
    ssq = jnp.zeros((ts, 1), jnp.float32)
    a_parts = []
    for g, win in enumerate(POOL_WINDOWS):
        cols = slice(g * gdim, (g + 1) * gdim)
        s = pext_ref[HALO:HALO + ts, cols]
        for j in range(1, win):
            s = s + pext_ref[HALO - j:HALO - j + ts, cols]
        cnt = jnp.minimum(pos + 1, win).astype(jnp.float32)
        d = (s / cnt - p[:, cols]).astype(jnp.bfloat16)
        a = jnp.dot(d, poolw_ref[g], preferred_element_type=jnp.float32) * pscale_ref[:, cols]
        ssq = ssq + jnp.sum(a * a, axis=-1, keepdims=True)
        a_parts.append(a)
    inv_a = lax.rsqrt(ssq / pool_w + NORM_EPS)
    for g in range(N_POOL_GROUPS):
        cols = slice(g * gdim, (g + 1) * gdim)
        mix_ref[:, cols] = (a_parts[g] * inv_a * onp_ref[:, cols]).astype(jnp.bfloat16)

    gz = _gelu(z[:, pool_w:])
    tril = (lax.broadcasted_iota(jnp.int32, (SGU_CHUNK, SGU_CHUNK), 0)
            >= lax.broadcasted_iota(jnp.int32, (SGU_CHUNK, SGU_CHUNK), 1))
    ssq = jnp.zeros((ts, 1), jnp.float32)
    b_parts = []
    for hd in range(SGU_HEADS):
        cols = slice(hd * hdim, (hd + 1) * hdim)
        u = gz[:, hd * hdim:(hd + 1) * hdim]
        v = gz[:, sgu_w + hd * hdim:sgu_w + (hd + 1) * hdim]
        mu = jnp.mean(v, axis=-1, keepdims=True)
        vc = v - mu
        var = jnp.mean(vc * vc, axis=-1, keepdims=True)
        vn = (vc * lax.rsqrt(var + NORM_EPS) * lng_ref[:, cols] + lnb_ref[:, cols]).astype(jnp.bfloat16)
        w = jnp.where(tril, sguw_ref[hd], jnp.zeros((), sguw_ref.dtype))
        mixed = [jnp.dot(w, vn[n * SGU_CHUNK:(n + 1) * SGU_CHUNK], preferred_element_type=jnp.float32)
                 + sgub_ref[hd] for n in range(ts // SGU_CHUNK)]
        b = u * jnp.concatenate(mixed, axis=0)
        ssq = ssq + jnp.sum(b * b, axis=-1, keepdims=True)
        b_parts.append(b)
    inv_b = lax.rsqrt(ssq / sgu_w + NORM_EPS)
    for hd in range(SGU_HEADS):
        cols = slice(hd * hdim, (hd + 1) * hdim)
        mix_ref[:, pool_w + hd * hdim:pool_w + (hd + 1) * hdim] = (
            b_parts[hd] * inv_b * ons_ref[:, cols]).astype(jnp.bfloat16)

    o_ref[0] = x + jnp.dot(mix_ref[...], wout_ref[...], preferred_element_type=jnp.float32)


def _mixer(x, b0, nb, norm_mix, w_in, pool_w, pool_scale, ln_g, ln_b, sgu_w, sgu_b, on_pool, on_sgu, w_out):
    _, S, D = x.shape
    ts = min(MIX_TILE, S)
    pool_width = pool_scale.size
    sgu_width = ln_g.size
    in_width = w_in.shape[1]
    gdim = pool_width // N_POOL_GROUPS
    halo_blocks = ts // HALO
    full = lambda shape: pl.BlockSpec(shape, lambda b, i: (0,) * len(shape))
    return pl.pallas_call(
        _mixer_kernel,
        grid=(nb, S // ts),
        in_specs=[
            pl.BlockSpec((1, ts, D), lambda b, i: (b0 + b, i, 0)),
            pl.BlockSpec((1, HALO, D), lambda b, i: (b0 + b, jnp.maximum(i * halo_blocks - 1, 0), 0)),
            full((1, D)),
            full((D, in_width)),
            full((N_POOL_GROUPS, gdim, gdim)),
            full((1, pool_width)),
            full((1, sgu_width)),
            full((1, sgu_width)),
            full((SGU_HEADS, SGU_CHUNK, SGU_CHUNK)),
            full((SGU_HEADS, SGU_CHUNK, SGU_CHUNK)),
            full((1, pool_width)),
            full((1, sgu_width)),
            full((pool_width + sgu_width, D)),
        ],
        out_specs=pl.BlockSpec((1, ts, D), lambda b, i: (b, i, 0)),
        out_shape=jax.ShapeDtypeStruct((nb, S, D), jnp.float32),
        scratch_shapes=[
            pltpu.VMEM((HALO + ts, pool_width), jnp.float32),
            pltpu.VMEM((ts, pool_width + sgu_width), jnp.bfloat16),
        ],
        compiler_params=pltpu.CompilerParams(
            dimension_semantics=("parallel", "arbitrary"), vmem_limit_bytes=TC_VMEM_LIMIT),
        name="mixer",
    )(x, x, norm_mix.reshape(1, D), w_in.astype(jnp.bfloat16), pool_w.astype(jnp.bfloat16),
      pool_scale.reshape(1, pool_width), ln_g.reshape(1, sgu_width), ln_b.reshape(1, sgu_width),
      sgu_w.astype(jnp.bfloat16),
      jnp.broadcast_to(sgu_b[:, :, None], (SGU_HEADS, SGU_CHUNK, SGU_CHUNK)),
      on_pool.reshape(1, pool_width), on_sgu.reshape(1, sgu_width), w_out.astype(jnp.bfloat16))


def _topk_rows(s, k):
    sub = V7X_SUBLANES
    n, m = s.shape
    pieces = n // sub
    assert n % sub == 0 and pieces >= k, "lists shorter than k would need padding"
    row = lax.broadcasted_iota(jnp.int32, (sub, m), 0)
    v = [s[p * sub:(p + 1) * sub] for p in range(pieces)]
    ix = [row + p * sub for p in range(pieces)]
    for rnd in range(pieces):
        for p in range(rnd % 2, pieces - 1, 2):
            swap = v[p + 1] > v[p]
            v[p], v[p + 1] = jnp.where(swap, v[p + 1], v[p]), jnp.where(swap, v[p], v[p + 1])
            ix[p], ix[p + 1] = jnp.where(swap, ix[p + 1], ix[p]), jnp.where(swap, ix[p], ix[p + 1])
    vals, idxs = [], []
    for r in range(k):
        best = jnp.max(v[0], axis=0, keepdims=True)
        best_ix = jnp.min(jnp.where(v[0] == best, ix[0], n), axis=0, keepdims=True)
        vals.append(best)
        idxs.append(best_ix)
        taken = ix[0] == best_ix
        for p in range(k - 1 - r):
            v[p] = jnp.where(taken, v[p + 1], v[p])
            ix[p] = jnp.where(taken, ix[p + 1], ix[p])
    return vals, idxs


def _pair_candidates(v1, i1, v2, i2):
    k, sub = PEER_TOPK, V7X_SUBLANES
    assert k == 16 and sub == 8, "the tile plan below is written for 16 candidates per side"
    v1c, i1c = jnp.concatenate(v1, axis=0), jnp.concatenate(i1, axis=0)
    v2c, i2c = jnp.concatenate(v2, axis=0), jnp.concatenate(i2, axis=0)
    m = v2c.shape[1]
    r = lax.broadcasted_iota(jnp.int32, (sub, m), 0)

    def with_b(a, b0, row0):
        shift = lambda x: x if row0 == 0 else pltpu.roll(x, row0, 0)
        return (v1[a] + shift(v2c[b0:b0 + sub]), i1[a] * PEER_N_KEYS + shift(i2c[b0:b0 + sub]),
                a * k + b0 + (r - row0))

    def with_a(a0, row0):
        s = (row0 - (a0 - sub)) % sub
        shift = lambda x: x if s == 0 else pltpu.roll(x, s, 0)
        return (shift(v1c[sub:]) + v2[0], shift(i1c[sub:]) * PEER_N_KEYS + i2[0], (a0 + (r - row0)) * k)

    def tile(*parts):
        out = parts[0][1]
        for row0, seg in parts[1:]:
            out = tuple(jnp.where(r < row0, o, s) for o, s in zip(out, seg))
        return out

    tiles = [
        with_b(0, 0, 0), with_b(0, sub, 0),
        with_b(1, 0, 0),
        tile((0, with_b(2, 0, 0)), (5, with_b(4, 0, 5))),
        tile((0, with_b(3, 0, 0)), (4, with_b(5, 0, 4)), (6, with_b(6, 0, 6))),
        tile((0, with_b(7, 0, 0)), (2, with_a(8, 2))),
        tile((0, with_a(14, 0)), (2, with_b(1, sub, 2))),
    ]
    return tuple(jnp.concatenate([t[n] for t in tiles], axis=0) for n in range(3))


def _router_kernel(x_ref, nffn_ref, wq_ref, keys_ref, after_ref, h2_ref, idx_ref, gate_ref,
                   q_ref, idxt_ref, gatet_ref):
    del after_ref
    h2 = _rms(x_ref[...], nffn_ref[...]).astype(jnp.bfloat16)
    h2_ref[...] = h2
    q_ref[...] = jnp.dot(h2, wq_ref[...], preferred_element_type=jnp.float32).astype(jnp.bfloat16)
    dq = 2 * PEER_D_HALF
    nt = (((1,), (1,)), ((), ()))

    def head(hd, carry):
        off = pl.multiple_of(hd * dq, dq)
        s1 = lax.dot_general(keys_ref[0], q_ref[:, pl.ds(off, PEER_D_HALF)], nt,
                             preferred_element_type=jnp.float32)
        s2 = lax.dot_general(keys_ref[1], q_ref[:, pl.ds(off + PEER_D_HALF, PEER_D_HALF)], nt,
                             preferred_element_type=jnp.float32)
        v1, i1 = _topk_rows(s1, PEER_TOPK)
        v2, i2 = _topk_rows(s2, PEER_TOPK)
        cand, expert, flat = _pair_candidates(v1, i1, v2, i2)
        cv, ce = [], []
        for _ in range(PEER_TOPK):
            m = jnp.max(cand, axis=0, keepdims=True)
            ix = jnp.min(jnp.where(cand == m, flat, PEER_TOPK * PEER_TOPK), axis=0, keepdims=True)
            hit = flat == ix
            cv.append(m)
            ce.append(jnp.max(jnp.where(hit, expert, -1), axis=0, keepdims=True))
            cand = jnp.where(hit, -jnp.inf, cand)
        cvc = jnp.concatenate(cv, axis=0)
        e = jnp.exp(cvc - cv[0])
        gate = e / jnp.sum(e, axis=0, keepdims=True)
        row = pl.multiple_of(hd * PEER_TOPK, PEER_TOPK)
        idxt_ref[pl.ds(row, PEER_TOPK), :] = jnp.concatenate(ce, axis=0)
        gatet_ref[pl.ds(row, PEER_TOPK), :] = gate
        return carry

    lax.fori_loop(0, PEER_HEADS, head, 0)
    idx_ref[...] = idxt_ref[...].T
    gate_ref[...] = gatet_ref[...].T


def _router(x1, norm_ffn, wq, keys, tok0, T, after):
    D = x1.shape[1]
    tr = min(ROUTE_TILE, T)
    qw = wq.shape[1]
    assert tok0 % tr == 0 and T % tr == 0
    first = tok0 // tr
    full = lambda shape: pl.BlockSpec(shape, lambda i: (0,) * len(shape))
    return pl.pallas_call(
        _router_kernel,
        grid=(T // tr,),
        in_specs=[
            pl.BlockSpec((tr, D), lambda i: (first + i, 0)),
            full((1, D)),
            full((D, qw)),
            full((2, PEER_N_KEYS, PEER_D_HALF)),
            pl.BlockSpec(memory_space=pl.ANY),
        ],
        out_specs=[
            pl.BlockSpec((tr, D), lambda i: (i, 0)),
            pl.BlockSpec((tr, EXPERTS_PER_TOKEN), lambda i: (i, 0)),
            pl.BlockSpec((tr, EXPERTS_PER_TOKEN), lambda i: (i, 0)),
        ],
        out_shape=[
            jax.ShapeDtypeStruct((T, D), jnp.bfloat16),
            jax.ShapeDtypeStruct((T, EXPERTS_PER_TOKEN), jnp.int32),
            jax.ShapeDtypeStruct((T, EXPERTS_PER_TOKEN), jnp.float32),
        ],
        scratch_shapes=[
            pltpu.VMEM((tr, qw), jnp.bfloat16),
            pltpu.VMEM((EXPERTS_PER_TOKEN, tr), jnp.int32),
            pltpu.VMEM((EXPERTS_PER_TOKEN, tr), jnp.float32),
        ],
        compiler_params=pltpu.CompilerParams(
            dimension_semantics=("parallel",), vmem_limit_bytes=TC_VMEM_LIMIT),
        name="router",
    )(x1, norm_ffn.reshape(1, D), wq, keys, after)


def _scores_kernel(h_ref, ut_ref, idx_ref, gate_ref, w_ref, pre_ref):
    j = pl.program_id(1)
    acc = jnp.dot(h_ref[...], ut_ref[...], preferred_element_type=jnp.float32)
    groups = acc.shape[1] // V7X_LANES
    lane_bits = int(math.log2(V7X_LANES))
    idx = idx_ref[...]
    group, lane = idx >> lane_bits, idx & (V7X_LANES - 1)
    picked = jnp.zeros(idx.shape, jnp.float32)
    for n in range(groups):
        vals = jnp.take_along_axis(acc[:, n * V7X_LANES:(n + 1) * V7X_LANES], lane, axis=1)
        picked = jnp.where(group == j * groups + n, vals, picked)

    @pl.when(j == 0)
    def _():
        pre_ref[...] = picked

    @pl.when(j > 0)
    def _():
        pre_ref[...] += picked

    @pl.when(j == pl.num_programs(1) - 1)
    def _():
        w = (gate_ref[...] * _gelu(pre_ref[...])).astype(jnp.bfloat16).astype(jnp.float32)
        hi = pltpu.bitcast(w, jnp.uint32) & jnp.uint32(HI_HALF)
        w_ref[...] = pltpu.bitcast(hi | (hi >> 16), jnp.int32)


def _expert_weights(h2, u_t, idx, gate):
    T, D = h2.shape
    E = u_t.shape[1]
    K = idx.shape[1]
    tm, tn = min(SCORE_TOK_TILE, T), min(SCORE_EXP_TILE, E)
    assert T % tm == 0 and E % tn == 0 and tn % V7X_LANES == 0 and K == V7X_LANES
    per_token = pl.BlockSpec((tm, K), lambda i, j: (i, 0))
    return pl.pallas_call(
        _scores_kernel,
        grid=(T // tm, E // tn),
        in_specs=[pl.BlockSpec((tm, D), lambda i, j: (i, 0)), pl.BlockSpec((D, tn), lambda i, j: (0, j)),
                  per_token, per_token],
        out_specs=per_token,
        out_shape=jax.ShapeDtypeStruct((T, K), jnp.int32),
        scratch_shapes=[pltpu.VMEM((tm, K), jnp.float32)],
        compiler_params=pltpu.CompilerParams(
            dimension_semantics=("parallel", "arbitrary"), vmem_limit_bytes=TC_VMEM_LIMIT),
        name="expert_weights",
    )(h2, u_t, idx, gate)


def _final_kernel(x_ref, y_ref, g_ref, *rest):
    o_ref = rest[-1]
    o_ref[...] = _rms(x_ref[...] + y_ref[...], g_ref[...])


def _final(x1, x_tok0, peer, norm_final, out_prev, out_tok0, total):
    Tc, D = peer.shape
    te = min(EW_TILE, Tc)
    assert x_tok0 % te == 0 and out_tok0 % te == 0 and Tc % te == 0
    nblk = Tc // te
    spec = pl.BlockSpec((te, D), lambda i: (out_tok0 // te + i, 0))
    in_specs = [pl.BlockSpec((te, D), lambda i: (x_tok0 // te + i, 0)),
                pl.BlockSpec((te, D), lambda i: (i, 0)), pl.BlockSpec((1, D), lambda i: (0, 0))]
    args = [x1, peer, norm_final.reshape(1, D)]
    aliases = {}
    if out_prev is not None:
        in_specs.append(pl.BlockSpec(memory_space=pl.ANY))
        args.append(out_prev)
        aliases = {3: 0}
    return pl.pallas_call(
        _final_kernel, grid=(nblk,), in_specs=in_specs, out_specs=spec,
        out_shape=jax.ShapeDtypeStruct((total, D), jnp.float32),
        input_output_aliases=aliases,
        compiler_params=pltpu.CompilerParams(dimension_semantics=("parallel",)),
        name="final_norm",
    )(*args)


def _tree_sum(vals):
    while len(vals) > 1:
        nxt = [vals[i] + vals[i + 1] for i in range(0, len(vals) - 1, 2)]
        if len(vals) % 2:
            nxt.append(vals[-1])
        vals = nxt
    return vals[0]


def _sc_block_pipeline(nblk, items_per_token, loads, store, gather, compute):
    ipt = items_per_token
    assert SC_ROW_BUFS > ipt and nblk >= 1

    for c in loads(0, 0):
        c.start()
    for c in loads(0, 0):
        c.wait()
    if nblk > 1:
        for c in loads(1, 1):
            c.start()
    for q in range(ipt):
        gather(0, 0, q, q).start()

    @pl.loop(0, nblk)
    def _(b):
        slot = b % 2

        @pl.when(b >= 2)
        def _():
            store(b - 2, slot).wait()

        @pl.loop(0, SC_TOKENS)
        def _(t):
            item0 = (b * SC_TOKENS + t) * ipt
            for q in range(ipt):
                nxt_buf = (item0 + q + ipt) % SC_ROW_BUFS

                @pl.when(t + 1 < SC_TOKENS)
                def _():
                    gather(slot, t + 1, q, nxt_buf).start()

                @pl.when(jnp.logical_and(t + 1 == SC_TOKENS, b + 1 < nblk))
                def _():
                    if q == 0:
                        for c in loads(b + 1, 1 - slot):
                            c.wait()
                    gather(1 - slot, 0, q, nxt_buf).start()

                buf = (item0 + q) % SC_ROW_BUFS
                gather(slot, t, q, buf).wait()
                compute(slot, t, q, buf)

        store(b, slot).start()

        @pl.when(b + 2 < nblk)
        def _():
            for c in loads(b + 2, slot):
                c.start()

    if nblk >= 2:
        store(nblk - 2, nblk % 2).wait()
    store(nblk - 1, (nblk - 1) % 2).wait()


def _sc_mesh():
    return plsc.VectorSubcoreMesh(core_axis_name="c", subcore_axis_name="s")


def _sc_worker_id():
    return lax.axis_index("s") * V7X_SC_CORES + lax.axis_index("c")


def _sc_bf16(words):
    return plsc.bitcast(words, jnp.bfloat16)


def _sc_halves_f32(pairs):
    words = plsc.bitcast(pairs, jnp.uint32)
    return (plsc.bitcast(words << 16, jnp.float32),
            plsc.bitcast(words & jnp.uint32(HI_HALF), jnp.float32))


def _mix_parts(T, K, DW):
    D = 2 * DW
    L, G = V7X_SC_LANES, SC_GATHER
    nj = SC_MIX_CHUNK // L
    tok_per_w = T // V7X_SC_WORKERS
    assert T % (V7X_SC_WORKERS * SC_TOKENS) == 0 and K % (2 * G) == 0
    assert DW % SC_MIX_CHUNK == 0 and G % SC_MIX_GROUP == 0
    scratch = [
        pltpu.VMEM((2, SC_TOKENS, K), jnp.int32),
        pltpu.VMEM((2, SC_TOKENS, K), jnp.int32),
        pltpu.VMEM((SC_ROW_BUFS, G, DW), jnp.uint32),
        pltpu.VMEM((2, SC_TOKENS, D), jnp.float32),
        pltpu.SemaphoreType.DMA((2,)),
        pltpu.SemaphoreType.DMA((2,)),
        pltpu.SemaphoreType.DMA((2,)),
        pltpu.SemaphoreType.DMA((SC_ROW_BUFS,)),
    ]

    def run(w_hbm, idx_hbm, tab_hbm, out_hbm, idx_v, w_v, rows_v, out_v, idx_sems, w_sems, out_sems, row_sems):
        base = _sc_worker_id() * tok_per_w

        def loads(b, slot):
            toks = pl.ds(base + b * SC_TOKENS, SC_TOKENS)
            return [pltpu.make_async_copy(idx_hbm.at[toks], idx_v.at[slot], idx_sems.at[slot]),
                    pltpu.make_async_copy(w_hbm.at[toks], w_v.at[slot], w_sems.at[slot])]

        def store(b, slot):
            toks = pl.ds(base + b * SC_TOKENS, SC_TOKENS)
            return pltpu.make_async_copy(out_v.at[slot], out_hbm.at[toks], out_sems.at[slot])

        def gather(slot, t, q, buf):
            return pltpu.make_async_copy(
                tab_hbm.at[idx_v.at[slot, t, pl.ds(q * G, G)]], rows_v.at[buf], row_sems.at[buf])

        def compute(slot, t, q, buf):
            ssplat = jnp.full((L,), slot, jnp.int32)
            tsplat = jnp.full((L,), t, jnp.int32)
            for c in range(DW // SC_MIX_CHUNK):
                def body(kg, acc):
                    kk = kg * SC_MIX_GROUP
                    wks = [_sc_bf16(plsc.load_gather(
                        w_v, [ssplat, tsplat, jnp.full((L,), q * G + i, jnp.int32) + kk]))
                        for i in range(SC_MIX_GROUP)]
                    out = []
                    for j in range(nj):
                        prods = [wks[i] * _sc_bf16(rows_v[buf, kk + i, pl.ds(c * SC_MIX_CHUNK + j * L, L)])
                                 for i in range(SC_MIX_GROUP)]
                        lo, hi = _sc_halves_f32(_tree_sum(prods))
                        out += [acc[2 * j] + lo, acc[2 * j + 1] + hi]
                    return tuple(out)

                zero = jnp.zeros((L,), jnp.float32)
                acc = plsc.parallel_loop(0, G // SC_MIX_GROUP, carry=(zero,) * (2 * nj))(body)
                for j in range(nj):
                    for half in range(2):
                        dst = out_v.at[slot, t, pl.ds(half * DW + c * SC_MIX_CHUNK + j * L, L)]
                        if q == 0:
                            dst[...] = acc[2 * j + half]
                        else:
                            plsc.addupdate(dst, acc[2 * j + half])

        _sc_block_pipeline(tok_per_w // SC_TOKENS, K // G, loads, store, gather, compute)

    return scratch, run


def _sc_kernel(out_type, scratch_types, name):
    return functools.partial(
        pl.kernel, mesh=_sc_mesh(), out_type=out_type, scratch_types=scratch_types,
        compiler_params=pltpu.CompilerParams(needs_layout_passes=False), name=name)


def _pack_table(a):
    n, c = a.shape
    half = c // 2
    L, rb = V7X_SC_LANES, SC_PACK_ROWS
    rows_per_w = n // V7X_SC_WORKERS
    nblk = rows_per_w // rb
    assert n % (V7X_SC_WORKERS * rb) == 0 and half % L == 0

    @_sc_kernel(jax.ShapeDtypeStruct((n, half), jnp.uint32),
                [pltpu.VMEM((2, rb, c), jnp.float32), pltpu.VMEM((2, rb, half), jnp.uint32),
                 pltpu.SemaphoreType.DMA((2,)), pltpu.SemaphoreType.DMA((2,))], "pack_table")
    def k(a_hbm, o_hbm, a_v, o_v, in_sems, out_sems):
        base = _sc_worker_id() * rows_per_w

        def load(b, slot):
            return pltpu.make_async_copy(a_hbm.at[pl.ds(base + b * rb, rb)], a_v.at[slot], in_sems.at[slot])

        def store(b, slot):
            return pltpu.make_async_copy(o_v.at[slot], o_hbm.at[pl.ds(base + b * rb, rb)], out_sems.at[slot])

        load(0, 0).start()

        @pl.loop(0, nblk)
        def _(b):
            slot = b % 2
            load(b, slot).wait()

            @pl.when(b + 1 < nblk)
            def _():
                load(b + 1, 1 - slot).start()

            @pl.when(b >= 2)
            def _():
                store(b - 2, slot).wait()

            @plsc.parallel_loop(0, rb)
            def _(r):
                for j in range(half // L):
                    pairs = plsc.pack(a_v[slot, r, pl.ds(j * L, L)], a_v[slot, r, pl.ds(half + j * L, L)],
                                      format=plsc.PackFormat.INTERLEAVED)
                    o_v[slot, r, pl.ds(j * L, L)] = plsc.bitcast(pairs, jnp.uint32)

            store(b, slot).start()

        if nblk >= 2:
            store(nblk - 2, nblk % 2).wait()
        store(nblk - 1, (nblk - 1) % 2).wait()

    return k(a)


def _expert_mix(w, idx, table):
    T, K = w.shape
    DW = table.shape[1]
    scratch, run = _mix_parts(T, K, DW)

    @_sc_kernel(jax.ShapeDtypeStruct((T, 2 * DW), jnp.float32), scratch, "expert_mix")
    def k(w_hbm, idx_hbm, tab_hbm, out_hbm, *mix_scratch):
        run(w_hbm, idx_hbm, tab_hbm, out_hbm, *mix_scratch)

    return k(w, idx, table)


def _chunk_sizes(total):
    ramp, size = [EDGE_CHUNK], EDGE_CHUNK
    while size < MAX_CHUNK:
        ramp.append(size)
        size *= 2
    middle = total - 2 * sum(ramp)
    assert middle >= 0 and middle % MAX_CHUNK == 0
    return ramp + [MAX_CHUNK] * (middle // MAX_CHUNK) + ramp[::-1]


def kernel(x, norm_mix, w_in, pool_w, pool_scale, sgu_ln_g, sgu_ln_b, sgu_w, sgu_b, out_norm_pool,
           out_norm_sgu, w_out, norm_ffn, peer_wq, peer_keys, peer_u, peer_v, norm_final):
    B, S, D = x.shape
    assert norm_mix.shape[0] == 1, "single-layer block"
    T = B * S
    mix_args = (norm_mix[0], w_in[0], pool_w[0], pool_scale[0], sgu_ln_g[0], sgu_ln_b[0],
                sgu_w[0], sgu_b[0], out_norm_pool[0], out_norm_sgu[0], w_out[0])
    x1_parts = [(b * S, _mixer(x, b, 1, *mix_args).reshape(S, D)) for b in range(B)]
    wq = peer_wq[0].astype(jnp.bfloat16)
    keys = peer_keys[0].astype(jnp.bfloat16)
    u_t = _transposed_bf16(peer_u[0])
    v_tab = _pack_table(peer_v[0])
    out = None
    tok0 = 0
    peers = [norm_ffn, v_tab]
    for tc in _chunk_sizes(T):
        part0, x1 = [p for p in x1_parts if p[0] <= tok0][-1]
        assert tok0 + tc <= part0 + x1.shape[0], "a token chunk must lie inside one mixer call"
        h2, idx, gate = _router(x1, norm_ffn[0], wq, keys, tok0 - part0, tc, peers[-2])
        w = _expert_weights(h2, u_t, idx, gate)
        peer = _expert_mix(w, idx, v_tab)
        peers.append(peer)
        out = _final(x1, tok0 - part0, peer, norm_final, out, tok0, T)
        tok0 += tc
    return out.reshape(B, S, D)
```

```python
import functools
import math

import jax
import jax.numpy as jnp
from jax import lax
from jax.experimental import pallas as pl
from jax.experimental.pallas import tpu as pltpu
from jax.experimental.pallas import tpu_sc as plsc

POOL_WINDOWS = (2, 4, 8, 16)
N_POOL_GROUPS = len(POOL_WINDOWS)
SGU_HEADS = 4
SGU_CHUNK = 128
PEER_HEADS = 8
PEER_N_KEYS = 128
PEER_D_HALF = 128
PEER_TOPK = 16
NORM_EPS = 1e-6
EXPERTS_PER_TOKEN = PEER_HEADS * PEER_TOPK

V7X_LANES = 128
V7X_SUBLANES = 8
V7X_SC_CORES = 2
V7X_SC_SUBCORES = 16
V7X_SC_LANES = 16
V7X_SC_WORKERS = V7X_SC_CORES * V7X_SC_SUBCORES

HALO = max(POOL_WINDOWS)
MIX_TILE = 512
ROUTE_TILE = 1024
EW_TILE = 512
SCORE_TOK_TILE = 2048
SCORE_EXP_TILE = 2048
SC_PACK_ROWS = 16
SC_GATHER = 64
SC_TOKENS = 8
SC_ROW_BUFS = 3
SC_MIX_CHUNK = 128
SC_MIX_GROUP = 4
HI_HALF = 0xFFFF0000
V7X_VMEM_BYTES = 64 * 1024 * 1024
TC_VMEM_LIMIT = V7X_VMEM_BYTES * 3 // 4
EDGE_CHUNK = 512
MAX_CHUNK = 2048


def _rms(x, g):
    inv = lax.rsqrt(jnp.mean(x * x, axis=-1, keepdims=True) + NORM_EPS)
    return x * inv * g


def _transpose_cast_kernel(a_ref, o_ref):
    o_ref[...] = a_ref[...].T.astype(o_ref.dtype)


def _transposed_bf16(a):
    n, c = a.shape
    rows = min(SCORE_EXP_TILE, n)
    return pl.pallas_call(
        _transpose_cast_kernel, grid=(n // rows,),
        in_specs=[pl.BlockSpec((rows, c), lambda i: (i, 0))],
        out_specs=pl.BlockSpec((c, rows), lambda i: (0, i)),
        out_shape=jax.ShapeDtypeStruct((c, n), jnp.bfloat16),
        compiler_params=pltpu.CompilerParams(dimension_semantics=("parallel",)),
        name="transpose_table",
    )(a)


def _gelu(x):
    return 0.5 * x * (1.0 + lax.erf(x * math.sqrt(0.5)))


def _mixer_kernel(x_ref, xh_ref, nmix_ref, win_ref, poolw_ref, pscale_ref, lng_ref, lnb_ref,
                  sguw_ref, sgub_ref, onp_ref, ons_ref, wout_ref, o_ref, pext_ref, mix_ref):
    i = pl.program_id(1)
    ts = x_ref.shape[1]
    pool_w = pscale_ref.shape[1]
    gdim = pool_w // N_POOL_GROUPS
    sgu_w = lng_ref.shape[1]
    hdim = sgu_w // SGU_HEADS

    x = x_ref[0]
    h = _rms(x, nmix_ref[...]).astype(jnp.bfloat16)
    z = jnp.dot(h, win_ref[...], preferred_element_type=jnp.float32)
    p = z[:, :pool_w]

    hh = _rms(xh_ref[0], nmix_ref[...]).astype(jnp.bfloat16)
    ph = jnp.dot(hh, win_ref[:, :pool_w], preferred_element_type=jnp.float32)
    ph = jnp.where(i > 0, ph, 0.0)
    pext_ref[0:HALO, :] = ph
    pext_ref[HALO:HALO + ts, :] = p

    pos = i * ts + lax.broadcasted_iota(jnp.int32, (ts, 1), 0)
    ssq = jnp.zeros((ts, 1), jnp.float32)
    a_parts = []
    for g, win in enumerate(POOL_WINDOWS):
        cols = slice(g * gdim, (g + 1) * gdim)
        s = pext_ref[HALO:HALO + ts, cols]
        for j in range(1, win):
            s = s + pext_ref[HALO - j:HALO - j + ts, cols]
        cnt = jnp.minimum(pos + 1, win).astype(jnp.float32)
        d = (s / cnt - p[:, cols]).astype(jnp.bfloat16)
        a = jnp.dot(d, poolw_ref[g], preferred_element_type=jnp.float32) * pscale_ref[:, cols]
        ssq = ssq + jnp.sum(a * a, axis=-1, keepdims=True)
        a_parts.append(a)
    inv_a = lax.rsqrt(ssq / pool_w + NORM_EPS)
    for g in range(N_POOL_GROUPS):
        cols = slice(g * gdim, (g + 1) * gdim)
        mix_ref[:, cols] = (a_parts[g] * inv_a * onp_ref[:, cols]).astype(jnp.bfloat16)

    gz = _gelu(z[:, pool_w:])
    tril = (lax.broadcasted_iota(jnp.int32, (SGU_CHUNK, SGU_CHUNK), 0)
            >= lax.broadcasted_iota(jnp.int32, (SGU_CHUNK, SGU_CHUNK), 1))
    ssq = jnp.zeros((ts, 1), jnp.float32)
    b_parts = []
    for hd in range(SGU_HEADS):
        cols = slice(hd * hdim, (hd + 1) * hdim)
        u = gz[:, hd * hdim:(hd + 1) * hdim]
        v = gz[:, sgu_w + hd * hdim:sgu_w + (hd + 1) * hdim]
        mu = jnp.mean(v, axis=-1, keepdims=True)
        vc = v - mu
        var = jnp.mean(vc * vc, axis=-1, keepdims=True)
        vn = (vc * lax.rsqrt(var + NORM_EPS) * lng_ref[:, cols] + lnb_ref[:, cols]).astype(jnp.bfloat16)
        w = jnp.where(tril, sguw_ref[hd], jnp.zeros((), sguw_ref.dtype))
        mixed = [jnp.dot(w, vn[n * SGU_CHUNK:(n + 1) * SGU_CHUNK], preferred_element_type=jnp.float32)
                 + sgub_ref[hd] for n in range(ts // SGU_CHUNK)]
        b = u * jnp.concatenate(mixed, axis=0)
        ssq = ssq + jnp.sum(b * b, axis=-1, keepdims=True)
        b_parts.append(b)
    inv_b = lax.rsqrt(ssq / sgu_w + NORM_EPS)
    for hd in range(SGU_HEADS):
        cols = slice(hd * hdim, (hd + 1) * hdim)
        mix_ref[:, pool_w + hd * hdim:pool_w + (hd + 1) * hdim] = (
            b_parts[hd] * inv_b * ons_ref[:, cols]).astype(jnp.bfloat16)

    o_ref[0] = x + jnp.dot(mix_ref[...], wout_ref[...], preferred_element_type=jnp.float32)


def _mixer(x, b0, nb, norm_mix, w_in, pool_w, pool_scale, ln_g, ln_b, sgu_w, sgu_b, on_pool, on_sgu, w_out):
    _, S, D = x.shape
    ts = min(MIX_TILE, S)
    pool_width = pool_scale.size
    sgu_width = ln_g.size
    in_width = w_in.shape[1]
    gdim = pool_width // N_POOL_GROUPS
    halo_blocks = ts // HALO
    full = lambda shape: pl.BlockSpec(shape, lambda b, i: (0,) * len(shape))
    return pl.pallas_call(
        _mixer_kernel,
        grid=(nb, S // ts),
        in_specs=[
            pl.BlockSpec((1, ts, D), lambda b, i: (b0 + b, i, 0)),
            pl.BlockSpec((1, HALO, D), lambda b, i: (b0 + b, jnp.maximum(i * halo_blocks - 1, 0), 0)),
            full((1, D)),
            full((D, in_width)),
            full((N_POOL_GROUPS, gdim, gdim)),
            full((1, pool_width)),
            full((1, sgu_width)),
            full((1, sgu_width)),
            full((SGU_HEADS, SGU_CHUNK, SGU_CHUNK)),
            full((SGU_HEADS, SGU_CHUNK, SGU_CHUNK)),
            full((1, pool_width)),
            full((1, sgu_width)),
            full((pool_width + sgu_width, D)),
        ],
        out_specs=pl.BlockSpec((1, ts, D), lambda b, i: (b, i, 0)),
        out_shape=jax.ShapeDtypeStruct((nb, S, D), jnp.float32),
        scratch_shapes=[
            pltpu.VMEM((HALO + ts, pool_width), jnp.float32),
            pltpu.VMEM((ts, pool_width + sgu_width), jnp.bfloat16),
        ],
        compiler_params=pltpu.CompilerParams(
            dimension_semantics=("parallel", "arbitrary"), vmem_limit_bytes=TC_VMEM_LIMIT),
        name="mixer",
    )(x, x, norm_mix.reshape(1, D), w_in.astype(jnp.bfloat16), pool_w.astype(jnp.bfloat16),
      pool_scale.reshape(1, pool_width), ln_g.reshape(1, sgu_width), ln_b.reshape(1, sgu_width),
      sgu_w.astype(jnp.bfloat16),
      jnp.broadcast_to(sgu_b[:, :, None], (SGU_HEADS, SGU_CHUNK, SGU_CHUNK)),
      on_pool.reshape(1, pool_width), on_sgu.reshape(1, sgu_width), w_out.astype(jnp.bfloat16))


def _topk_rows(s, k):
    sub = V7X_SUBLANES
    n, m = s.shape
    pieces = n // sub
    assert n % sub == 0 and pieces >= k, "lists shorter than k would need padding"
    row = lax.broadcasted_iota(jnp.int32, (sub, m), 0)
    v = [s[p * sub:(p + 1) * sub] for p in range(pieces)]
    ix = [row + p * sub for p in range(pieces)]
    for rnd in range(pieces):
        for p in range(rnd % 2, pieces - 1, 2):
            swap = v[p + 1] > v[p]
            v[p], v[p + 1] = jnp.where(swap, v[p + 1], v[p]), jnp.where(swap, v[p], v[p + 1])
            ix[p], ix[p + 1] = jnp.where(swap, ix[p + 1], ix[p]), jnp.where(swap, ix[p], ix[p + 1])
    vals, idxs = [], []
    for r in range(k):
        best = jnp.max(v[0], axis=0, keepdims=True)
        best_ix = jnp.min(jnp.where(v[0] == best, ix[0], n), axis=0, keepdims=True)
        vals.append(best)
        idxs.append(best_ix)
        taken = ix[0] == best_ix
        for p in range(k - 1 - r):
            v[p] = jnp.where(taken, v[p + 1], v[p])
            ix[p] = jnp.where(taken, ix[p + 1], ix[p])
    return vals, idxs


def _pair_candidates(v1, i1, v2, i2):
    k, sub = PEER_TOPK, V7X_SUBLANES
    assert k == 16 and sub == 8, "the tile plan below is written for 16 candidates per side"
    v1c, i1c = jnp.concatenate(v1, axis=0), jnp.concatenate(i1, axis=0)
    v2c, i2c = jnp.concatenate(v2, axis=0), jnp.concatenate(i2, axis=0)
    m = v2c.shape[1]
    r = lax.broadcasted_iota(jnp.int32, (sub, m), 0)

    def with_b(a, b0, row0):
        shift = lambda x: x if row0 == 0 else pltpu.roll(x, row0, 0)
        return (v1[a] + shift(v2c[b0:b0 + sub]), i1[a] * PEER_N_KEYS + shift(i2c[b0:b0 + sub]),
                a * k + b0 + (r - row0))

    def with_a(a0, row0):
        s = (row0 - (a0 - sub)) % sub
        shift = lambda x: x if s == 0 else pltpu.roll(x, s, 0)
        return (shift(v1c[sub:]) + v2[0], shift(i1c[sub:]) * PEER_N_KEYS + i2[0], (a0 + (r - row0)) * k)

    def tile(*parts):
        out = parts[0][1]
        for row0, seg in parts[1:]:
            out = tuple(jnp.where(r < row0, o, s) for o, s in zip(out, seg))
        return out

    tiles = [
        with_b(0, 0, 0), with_b(0, sub, 0),
        with_b(1, 0, 0),
        tile((0, with_b(2, 0, 0)), (5, with_b(4, 0, 5))),
        tile((0, with_b(3, 0, 0)), (4, with_b(5, 0, 4)), (6, with_b(6, 0, 6))),
        tile((0, with_b(7, 0, 0)), (2, with_a(8, 2))),
        tile((0, with_a(14, 0)), (2, with_b(1, sub, 2))),
    ]
    return tuple(jnp.concatenate([t[n] for t in tiles], axis=0) for n in range(3))


def _router_kernel(x_ref, nffn_ref, wq_ref, keys_ref, after_ref, h2_ref, idx_ref, gate_ref,
                   q_ref, idxt_ref, gatet_ref):
    del after_ref
    h2 = _rms(x_ref[...], nffn_ref[...]).astype(jnp.bfloat16)
    h2_ref[...] = h2
    q_ref[...] = jnp.dot(h2, wq_ref[...], preferred_element_type=jnp.float32).astype(jnp.bfloat16)
    dq = 2 * PEER_D_HALF
    nt = (((1,), (1,)), ((), ()))

    def head(hd, carry):
        off = pl.multiple_of(hd * dq, dq)
        s1 = lax.dot_general(keys_ref[0], q_ref[:, pl.ds(off, PEER_D_HALF)], nt,
                             preferred_element_type=jnp.float32)
        s2 = lax.dot_general(keys_ref[1], q_ref[:, pl.ds(off + PEER_D_HALF, PEER_D_HALF)], nt,
                             preferred_element_type=jnp.float32)
        v1, i1 = _topk_rows(s1, PEER_TOPK)
        v2, i2 = _topk_rows(s2, PEER_TOPK)
        cand, expert, flat = _pair_candidates(v1, i1, v2, i2)
        cv, ce = [], []
        for _ in range(PEER_TOPK):
            m = jnp.max(cand, axis=0, keepdims=True)
            ix = jnp.min(jnp.where(cand == m, flat, PEER_TOPK * PEER_TOPK), axis=0, keepdims=True)
            hit = flat == ix
            cv.append(m)
            ce.append(jnp.max(jnp.where(hit, expert, -1), axis=0, keepdims=True))
            cand = jnp.where(hit, -jnp.inf, cand)
        cvc = jnp.concatenate(cv, axis=0)
        e = jnp.exp(cvc - cv[0])
        gate = e / jnp.sum(e, axis=0, keepdims=True)
        row = pl.multiple_of(hd * PEER_TOPK, PEER_TOPK)
        idxt_ref[pl.ds(row, PEER_TOPK), :] = jnp.concatenate(ce, axis=0)
        gatet_ref[pl.ds(row, PEER_TOPK), :] = gate
        return carry

    lax.fori_loop(0, PEER_HEADS, head, 0)
    idx_ref[...] = idxt_ref[...].T
    gate_ref[...] = gatet_ref[...].T


def _router(x1, norm_ffn, wq, keys, tok0, T, after):
    D = x1.shape[1]
    tr = min(ROUTE_TILE, T)
    qw = wq.shape[1]
    assert tok0 % tr == 0 and T % tr == 0
    first = tok0 // tr
    full = lambda shape: pl.BlockSpec(shape, lambda i: (0,) * len(shape))
    return pl.pallas_call(
        _router_kernel,
        grid=(T // tr,),
        in_specs=[
            pl.BlockSpec((tr, D), lambda i: (first + i, 0)),
            full((1, D)),
            full((D, qw)),
            full((2, PEER_N_KEYS, PEER_D_HALF)),
            pl.BlockSpec(memory_space=pl.ANY),
        ],
        out_specs=[
            pl.BlockSpec((tr, D), lambda i: (i, 0)),
            pl.BlockSpec((tr, EXPERTS_PER_TOKEN), lambda i: (i, 0)),
            pl.BlockSpec((tr, EXPERTS_PER_TOKEN), lambda i: (i, 0)),
        ],
        out_shape=[
            jax.ShapeDtypeStruct((T, D), jnp.bfloat16),
            jax.ShapeDtypeStruct((T, EXPERTS_PER_TOKEN), jnp.int32),
            jax.ShapeDtypeStruct((T, EXPERTS_PER_TOKEN), jnp.float32),
        ],
        scratch_shapes=[
            pltpu.VMEM((tr, qw), jnp.bfloat16),
            pltpu.VMEM((EXPERTS_PER_TOKEN, tr), jnp.int32),
            pltpu.VMEM((EXPERTS_PER_TOKEN, tr), jnp.float32),
        ],
        compiler_params=pltpu.CompilerParams(
            dimension_semantics=("parallel",), vmem_limit_bytes=TC_VMEM_LIMIT),
        name="router",
    )(x1, norm_ffn.reshape(1, D), wq, keys, after)


def _scores_kernel(h_ref, ut_ref, idx_ref, gate_ref, w_ref, pre_ref):
    j = pl.program_id(1)
    acc = jnp.dot(h_ref[...], ut_ref[...], preferred_element_type=jnp.float32)
    groups = acc.shape[1] // V7X_LANES
    lane_bits = int(math.log2(V7X_LANES))
    idx = idx_ref[...]
    group, lane = idx >> lane_bits, idx & (V7X_LANES - 1)
    picked = jnp.zeros(idx.shape, jnp.float32)
    for n in range(groups):
        vals = jnp.take_along_axis(acc[:, n * V7X_LANES:(n + 1) * V7X_LANES], lane, axis=1)
        picked = jnp.where(group == j * groups + n, vals, picked)

    @pl.when(j == 0)
    def _():
        pre_ref[...] = picked

    @pl.when(j > 0)
    def _():
        pre_ref[...] += picked

    @pl.when(j == pl.num_programs(1) - 1)
    def _():
        w = (gate_ref[...] * _gelu(pre_ref[...])).astype(jnp.bfloat16).astype(jnp.float32)
        hi = pltpu.bitcast(w, jnp.uint32) & jnp.uint32(HI_HALF)
        w_ref[...] = pltpu.bitcast(hi | (hi >> 16), jnp.int32)


def _expert_weights(h2, u_t, idx, gate):
    T, D = h2.shape
    E = u_t.shape[1]
    K = idx.shape[1]
    tm, tn = min(SCORE_TOK_TILE, T), min(SCORE_EXP_TILE, E)
    assert T % tm == 0 and E % tn == 0 and tn % V7X_LANES == 0 and K == V7X_LANES
    per_token = pl.BlockSpec((tm, K), lambda i, j: (i, 0))
    return pl.pallas_call(
        _scores_kernel,
        grid=(T // tm, E // tn),
        in_specs=[pl.BlockSpec((tm, D), lambda i, j: (i, 0)), pl.BlockSpec((D, tn), lambda i, j: (0, j)),
                  per_token, per_token],
        out_specs=per_token,
        out_shape=jax.ShapeDtypeStruct((T, K), jnp.int32),
        scratch_shapes=[pltpu.VMEM((tm, K), jnp.float32)],
        compiler_params=pltpu.CompilerParams(
            dimension_semantics=("parallel", "arbitrary"), vmem_limit_bytes=TC_VMEM_LIMIT),
        name="expert_weights",
    )(h2, u_t, idx, gate)


def _final_kernel(x_ref, y_ref, g_ref, *rest):
    o_ref = rest[-1]
    o_ref[...] = _rms(x_ref[...] + y_ref[...], g_ref[...])


def _final(x1, x_tok0, peer, norm_final, out_prev, out_tok0, total):
    Tc, D = peer.shape
    te = min(EW_TILE, Tc)
    assert x_tok0 % te == 0 and out_tok0 % te == 0 and Tc % te == 0
    nblk = Tc // te
    spec = pl.BlockSpec((te, D), lambda i: (out_tok0 // te + i, 0))
    in_specs = [pl.BlockSpec((te, D), lambda i: (x_tok0 // te + i, 0)),
                pl.BlockSpec((te, D), lambda i: (i, 0)), pl.BlockSpec((1, D), lambda i: (0, 0))]
    args = [x1, peer, norm_final.reshape(1, D)]
    aliases = {}
    if out_prev is not None:
        in_specs.append(pl.BlockSpec(memory_space=pl.ANY))
        args.append(out_prev)
        aliases = {3: 0}
    return pl.pallas_call(
        _final_kernel, grid=(nblk,), in_specs=in_specs, out_specs=spec,
        out_shape=jax.ShapeDtypeStruct((total, D), jnp.float32),
        input_output_aliases=aliases,
        compiler_params=pltpu.CompilerParams(dimension_semantics=("parallel",)),
        name="final_norm",
    )(*args)


def _tree_sum(vals):
    while len(vals) > 1:
        nxt = [vals[i] + vals[i + 1] for i in range(0, len(vals) - 1, 2)]
        if len(vals) % 2:
            nxt.append(vals[-1])
        vals = nxt
    return vals[0]


def _sc_block_pipeline(nblk, items_per_token, loads, store, gather, compute):
    ipt = items_per_token
    assert SC_ROW_BUFS > ipt and nblk >= 1

    for c in loads(0, 0):
        c.start()
    for c in loads(0, 0):
        c.wait()
    if nblk > 1:
        for c in loads(1, 1):
            c.start()
    for q in range(ipt):
        gather(0, 0, q, q).start()

    @pl.loop(0, nblk)
    def _(b):
        slot = b % 2

        @pl.when(b >= 2)
        def _():
            store(b - 2, slot).wait()

        @pl.loop(0, SC_TOKENS)
        def _(t):
            item0 = (b * SC_TOKENS + t) * ipt
            for q in range(ipt):
                nxt_buf = (item0 + q + ipt) % SC_ROW_BUFS

                @pl.when(t + 1 < SC_TOKENS)
                def _():
                    gather(slot, t + 1, q, nxt_buf).start()

                @pl.when(jnp.logical_and(t + 1 == SC_TOKENS, b + 1 < nblk))
                def _():
                    if q == 0:
                        for c in loads(b + 1, 1 - slot):
                            c.wait()
                    gather(1 - slot, 0, q, nxt_buf).start()

                buf = (item0 + q) % SC_ROW_BUFS
                gather(slot, t, q, buf).wait()
                compute(slot, t, q, buf)

        store(b, slot).start()

        @pl.when(b + 2 < nblk)
        def _():
            for c in loads(b + 2, slot):
                c.start()

    if nblk >= 2:
        store(nblk - 2, nblk % 2).wait()
    store(nblk - 1, (nblk - 1) % 2).wait()


def _sc_mesh():
    return plsc.VectorSubcoreMesh(core_axis_name="c", subcore_axis_name="s")


def _sc_worker_id():
    return lax.axis_index("s") * V7X_SC_CORES + lax.axis_index("c")


def _sc_bf16(words):
    return plsc.bitcast(words, jnp.bfloat16)


def _sc_halves_f32(pairs):
    words = plsc.bitcast(pairs, jnp.uint32)
    return (plsc.bitcast(words << 16, jnp.float32),
            plsc.bitcast(words & jnp.uint32(HI_HALF), jnp.float32))


def _mix_parts(T, K, DW):
    D = 2 * DW
    L, G = V7X_SC_LANES, SC_GATHER
    nj = SC_MIX_CHUNK // L
    tok_per_w = T // V7X_SC_WORKERS
    assert T % (V7X_SC_WORKERS * SC_TOKENS) == 0 and K % (2 * G) == 0
    assert DW % SC_MIX_CHUNK == 0 and G % SC_MIX_GROUP == 0
    scratch = [
        pltpu.VMEM((2, SC_TOKENS, K), jnp.int32),
        pltpu.VMEM((2, SC_TOKENS, K), jnp.int32),
        pltpu.VMEM((SC_ROW_BUFS, G, DW), jnp.uint32),
        pltpu.VMEM((2, SC_TOKENS, D), jnp.float32),
        pltpu.SemaphoreType.DMA((2,)),
        pltpu.SemaphoreType.DMA((2,)),
        pltpu.SemaphoreType.DMA((2,)),
        pltpu.SemaphoreType.DMA((SC_ROW_BUFS,)),
    ]

    def run(w_hbm, idx_hbm, tab_hbm, out_hbm, idx_v, w_v, rows_v, out_v, idx_sems, w_sems, out_sems, row_sems):
        base = _sc_worker_id() * tok_per_w

        def loads(b, slot):
            toks = pl.ds(base + b * SC_TOKENS, SC_TOKENS)
            return [pltpu.make_async_copy(idx_hbm.at[toks], idx_v.at[slot], idx_sems.at[slot]),
                    pltpu.make_async_copy(w_hbm.at[toks], w_v.at[slot], w_sems.at[slot])]

        def store(b, slot):
            toks = pl.ds(base + b * SC_TOKENS, SC_TOKENS)
            return pltpu.make_async_copy(out_v.at[slot], out_hbm.at[toks], out_sems.at[slot])

        def gather(slot, t, q, buf):
            return pltpu.make_async_copy(
                tab_hbm.at[idx_v.at[slot, t, pl.ds(q * G, G)]], rows_v.at[buf], row_sems.at[buf])

        def compute(slot, t, q, buf):
            ssplat = jnp.full((L,), slot, jnp.int32)
            tsplat = jnp.full((L,), t, jnp.int32)
            for c in range(DW // SC_MIX_CHUNK):
                def body(kg, acc):
                    kk = kg * SC_MIX_GROUP
                    wks = [_sc_bf16(plsc.load_gather(
                        w_v, [ssplat, tsplat, jnp.full((L,), q * G + i, jnp.int32) + kk]))
                        for i in range(SC_MIX_GROUP)]
                    out = []
                    for j in range(nj):
                        prods = [wks[i] * _sc_bf16(rows_v[buf, kk + i, pl.ds(c * SC_MIX_CHUNK + j * L, L)])
                                 for i in range(SC_MIX_GROUP)]
                        lo, hi = _sc_halves_f32(_tree_sum(prods))
                        out += [acc[2 * j] + lo, acc[2 * j + 1] + hi]
                    return tuple(out)

                zero = jnp.zeros((L,), jnp.float32)
                acc = plsc.parallel_loop(0, G // SC_MIX_GROUP, carry=(zero,) * (2 * nj))(body)
                for j in range(nj):
                    for half in range(2):
                        dst = out_v.at[slot, t, pl.ds(half * DW + c * SC_MIX_CHUNK + j * L, L)]
                        if q == 0:
                            dst[...] = acc[2 * j + half]
                        else:
                            plsc.addupdate(dst, acc[2 * j + half])

        _sc_block_pipeline(tok_per_w // SC_TOKENS, K // G, loads, store, gather, compute)

    return scratch, run


def _sc_kernel(out_type, scratch_types, name):
    return functools.partial(
        pl.kernel, mesh=_sc_mesh(), out_type=out_type, scratch_types=scratch_types,
        compiler_params=pltpu.CompilerParams(needs_layout_passes=False), name=name)


def _pack_table(a):
    n, c = a.shape
    half = c // 2
    L, rb = V7X_SC_LANES, SC_PACK_ROWS
    rows_per_w = n // V7X_SC_WORKERS
    nblk = rows_per_w // rb
    assert n % (V7X_SC_WORKERS * rb) == 0 and half % L == 0

    @_sc_kernel(jax.ShapeDtypeStruct((n, half), jnp.uint32),
                [pltpu.VMEM((2, rb, c), jnp.float32), pltpu.VMEM((2, rb, half), jnp.uint32),
                 pltpu.SemaphoreType.DMA((2,)), pltpu.SemaphoreType.DMA((2,))], "pack_table")
    def k(a_hbm, o_hbm, a_v, o_v, in_sems, out_sems):
        base = _sc_worker_id() * rows_per_w

        def load(b, slot):
            return pltpu.make_async_copy(a_hbm.at[pl.ds(base + b * rb, rb)], a_v.at[slot], in_sems.at[slot])

        def store(b, slot):
            return pltpu.make_async_copy(o_v.at[slot], o_hbm.at[pl.ds(base + b * rb, rb)], out_sems.at[slot])

        load(0, 0).start()

        @pl.loop(0, nblk)
        def _(b):
            slot = b % 2
            load(b, slot).wait()

            @pl.when(b + 1 < nblk)
            def _():
                load(b + 1, 1 - slot).start()

            @pl.when(b >= 2)
            def _():
                store(b - 2, slot).wait()

            @plsc.parallel_loop(0, rb)
            def _(r):
                for j in range(half // L):
                    pairs = plsc.pack(a_v[slot, r, pl.ds(j * L, L)], a_v[slot, r, pl.ds(half + j * L, L)],
                                      format=plsc.PackFormat.INTERLEAVED)
                    o_v[slot, r, pl.ds(j * L, L)] = plsc.bitcast(pairs, jnp.uint32)

            store(b, slot).start()

        if nblk >= 2:
            store(nblk - 2, nblk % 2).wait()
        store(nblk - 1, (nblk - 1) % 2).wait()

    return k(a)


def _expert_mix(w, idx, table):
    T, K = w.shape
    DW = table.shape[1]
    scratch, run = _mix_parts(T, K, DW)

    @_sc_kernel(jax.ShapeDtypeStruct((T, 2 * DW), jnp.float32), scratch, "expert_mix")
    def k(w_hbm, idx_hbm, tab_hbm, out_hbm, *mix_scratch):
        run(w_hbm, idx_hbm, tab_hbm, out_hbm, *mix_scratch)

    return k(w, idx, table)


def _chunk_sizes(total):
    ramp, size = [EDGE_CHUNK], EDGE_CHUNK
    while size < MAX_CHUNK:
        ramp.append(size)
        size *= 2
    middle = total - 2 * sum(ramp)
    assert middle >= 0 and middle % MAX_CHUNK == 0
    return ramp + [MAX_CHUNK] * (middle // MAX_CHUNK) + ramp[::-1]


def kernel(x, norm_mix, w_in, pool_w, pool_scale, sgu_ln_g, sgu_ln_b, sgu_w, sgu_b, out_norm_pool,
           out_norm_sgu, w_out, norm_ffn, peer_wq, peer_keys, peer_u, peer_v, norm_final):
    B, S, D = x.shape
    assert norm_mix.shape[0] == 1, "single-layer block"
    T = B * S
    mix_args = (norm_mix[0], w_in[0], pool_w[0], pool_scale[0], sgu_ln_g[0], sgu_ln_b[0],
                sgu_w[0], sgu_b[0], out_norm_pool[0], out_norm_sgu[0], w_out[0])
    x1_parts = [(b * S, _mixer(x, b, 1, *mix_args).reshape(S, D)) for b in range(B)]
    wq = peer_wq[0].astype(jnp.bfloat16)
    keys = peer_keys[0].astype(jnp.bfloat16)
    u_t = _transposed_bf16(peer_u[0])
    v_tab = _pack_table(peer_v[0])
    out = None
    tok0 = 0
    peers = [norm_ffn, v_tab]
    outs = []
    chunks = _chunk_sizes(T)
    for c, tc in enumerate(chunks):
        part0, x1 = [p for p in x1_parts if p[0] <= tok0][-1]
        assert tok0 + tc <= part0 + x1.shape[0], "a token chunk must lie inside one mixer call"
        after = outs[-2] if c >= max(len(chunks) - 2, 2) else peers[-2]
        h2, idx, gate = _router(x1, norm_ffn[0], wq, keys, tok0 - part0, tc, after)
        w = _expert_weights(h2, u_t, idx, gate)
        peer = _expert_mix(w, idx, v_tab)
        peers.append(peer)
        out = _final(x1, tok0 - part0, peer, norm_final, out, tok0, T)
        outs.append(out)
        tok0 += tc
    return out.reshape(B, S, D)
```

```python
import functools
import math

import jax
import jax.numpy as jnp
from jax import lax
from jax.experimental import pallas as pl
from jax.experimental.pallas import tpu as pltpu
from jax.experimental.pallas import tpu_sc as plsc

POOL_WINDOWS = (2, 4, 8, 16)
N_POOL_GROUPS = len(POOL_WINDOWS)
SGU_HEADS = 4
SGU_CHUNK = 128
PEER_HEADS = 8
PEER_N_KEYS = 128
PEER_D_HALF = 128
PEER_TOPK = 16
NORM_EPS = 1e-6
EXPERTS_PER_TOKEN = PEER_HEADS * PEER_TOPK

V7X_LANES = 128
V7X_SUBLANES = 8
V7X_SC_CORES = 2
V7X_SC_SUBCORES = 16
V7X_SC_LANES = 16
V7X_SC_WORKERS = V7X_SC_CORES * V7X_SC_SUBCORES

HALO = max(POOL_WINDOWS)
MIX_TILE = 512
ROUTE_TILE = 1024
EW_TILE = 512
SCORE_TOK_TILE = 2048
SCORE_EXP_TILE = 2048
SC_PACK_ROWS = 16
SC_GATHER = 64
SC_TOKENS = 8
SC_ROW_BUFS = 3
SC_MIX_CHUNK = 128
SC_MIX_GROUP = 4
HI_HALF = 0xFFFF0000
V7X_VMEM_BYTES = 64 * 1024 * 1024
TC_VMEM_LIMIT = V7X_VMEM_BYTES * 3 // 4
EDGE_CHUNK = 512
MAX_CHUNK = 2048


def _rms(x, g):
    inv = lax.rsqrt(jnp.mean(x * x, axis=-1, keepdims=True) + NORM_EPS)
    return x * inv * g


def _transpose_cast_kernel(a_ref, o_ref):
    o_ref[...] = a_ref[...].T.astype(o_ref.dtype)


def _transposed_bf16(a):
    n, c = a.shape
    rows = min(SCORE_EXP_TILE, n)
    return pl.pallas_call(
        _transpose_cast_kernel, grid=(n // rows,),
        in_specs=[pl.BlockSpec((rows, c), lambda i: (i, 0))],
        out_specs=pl.BlockSpec((c, rows), lambda i: (0, i)),
        out_shape=jax.ShapeDtypeStruct((c, n), jnp.bfloat16),
        compiler_params=pltpu.CompilerParams(dimension_semantics=("parallel",)),
        name="transpose_table",
    )(a)


def _gelu(x):
    return 0.5 * x * (1.0 + lax.erf(x * math.sqrt(0.5)))


def _mixer_kernel(x_ref, xh_ref, nmix_ref, win_ref, poolw_ref, pscale_ref, lng_ref, lnb_ref,
                  sguw_ref, sgub_ref, onp_ref, ons_ref, wout_ref, o_ref, pext_ref, mix_ref):
    i = pl.program_id(1)
    ts = x_ref.shape[1]
    pool_w = pscale_ref.shape[1]
    gdim = pool_w // N_POOL_GROUPS
    sgu_w = lng_ref.shape[1]
    hdim = sgu_w // SGU_HEADS

    x = x_ref[0]
    h = _rms(x, nmix_ref[...]).astype(jnp.bfloat16)
    z = jnp.dot(h, win_ref[...], preferred_element_type=jnp.float32)
    p = z[:, :pool_w]

    hh = _rms(xh_ref[0], nmix_ref[...]).astype(jnp.bfloat16)
    ph = jnp.dot(hh, win_ref[:, :pool_w], preferred_element_type=jnp.float32)
    ph = jnp.where(i > 0, ph, 0.0)
    pext_ref[0:HALO, :] = ph
    pext_ref[HALO:HALO + ts, :] = p

    pos = i * ts + lax.broadcasted_iota(jnp.int32, (ts, 1), 0)
    ssq = jnp.zeros((ts, 1), jnp.float32)
    a_parts = []
    for g, win in enumerate(POOL_WINDOWS):
        cols = slice(g * gdim, (g + 1) * gdim)
        s = pext_ref[HALO:HALO + ts, cols]
        for j in range(1, win):
            s = s + pext_ref[HALO - j:HALO - j + ts, cols]
        cnt = jnp.minimum(pos + 1, win).astype(jnp.float32)
        d = (s / cnt - p[:, cols]).astype(jnp.bfloat16)
        a = jnp.dot(d, poolw_ref[g], preferred_element_type=jnp.float32) * pscale_ref[:, cols]
        ssq = ssq + jnp.sum(a * a, axis=-1, keepdims=True)
        a_parts.append(a)
    inv_a = lax.rsqrt(ssq / pool_w + NORM_EPS)
    for g in range(N_POOL_GROUPS):
        cols = slice(g * gdim, (g + 1) * gdim)
        mix_ref[:, cols] = (a_parts[g] * inv_a * onp_ref[:, cols]).astype(jnp.bfloat16)

    gz = _gelu(z[:, pool_w:])
    tril = (lax.broadcasted_iota(jnp.int32, (SGU_CHUNK, SGU_CHUNK), 0)
            >= lax.broadcasted_iota(jnp.int32, (SGU_CHUNK, SGU_CHUNK), 1))
    ssq = jnp.zeros((ts, 1), jnp.float32)
    b_parts = []
    for hd in range(SGU_HEADS):
        cols = slice(hd * hdim, (hd + 1) * hdim)
        u = gz[:, hd * hdim:(hd + 1) * hdim]
        v = gz[:, sgu_w + hd * hdim:sgu_w + (hd + 1) * hdim]
        mu = jnp.mean(v, axis=-1, keepdims=True)
        vc = v - mu
        var = jnp.mean(vc * vc, axis=-1, keepdims=True)
        vn = (vc * lax.rsqrt(var + NORM_EPS) * lng_ref[:, cols] + lnb_ref[:, cols]).astype(jnp.bfloat16)
        w = jnp.where(tril, sguw_ref[hd], jnp.zeros((), sguw_ref.dtype))
        mixed = [jnp.dot(w, vn[n * SGU_CHUNK:(n + 1) * SGU_CHUNK], preferred_element_type=jnp.float32)
                 + sgub_ref[hd] for n in range(ts // SGU_CHUNK)]
        b = u * jnp.concatenate(mixed, axis=0)
        ssq = ssq + jnp.sum(b * b, axis=-1, keepdims=True)
        b_parts.append(b)
    inv_b = lax.rsqrt(ssq / sgu_w + NORM_EPS)
    for hd in range(SGU_HEADS):
        cols = slice(hd * hdim, (hd + 1) * hdim)
        mix_ref[:, pool_w + hd * hdim:pool_w + (hd + 1) * hdim] = (
            b_parts[hd] * inv_b * ons_ref[:, cols]).astype(jnp.bfloat16)

    o_ref[0] = x + jnp.dot(mix_ref[...], wout_ref[...], preferred_element_type=jnp.float32)


def _mixer(x, b0, nb, norm_mix, w_in, pool_w, pool_scale, ln_g, ln_b, sgu_w, sgu_b, on_pool, on_sgu, w_out):
    _, S, D = x.shape
    ts = min(MIX_TILE, S)
    pool_width = pool_scale.size
    sgu_width = ln_g.size
    in_width = w_in.shape[1]
    gdim = pool_width // N_POOL_GROUPS
    halo_blocks = ts // HALO
    full = lambda shape: pl.BlockSpec(shape, lambda b, i: (0,) * len(shape))
    return pl.pallas_call(
        _mixer_kernel,
        grid=(nb, S // ts),
        in_specs=[
            pl.BlockSpec((1, ts, D), lambda b, i: (b0 + b, i, 0)),
            pl.BlockSpec((1, HALO, D), lambda b, i: (b0 + b, jnp.maximum(i * halo_blocks - 1, 0), 0)),
            full((1, D)),
            full((D, in_width)),
            full((N_POOL_GROUPS, gdim, gdim)),
            full((1, pool_width)),
            full((1, sgu_width)),
            full((1, sgu_width)),
            full((SGU_HEADS, SGU_CHUNK, SGU_CHUNK)),
            full((SGU_HEADS, SGU_CHUNK, SGU_CHUNK)),
            full((1, pool_width)),
            full((1, sgu_width)),
            full((pool_width + sgu_width, D)),
        ],
        out_specs=pl.BlockSpec((1, ts, D), lambda b, i: (b, i, 0)),
        out_shape=jax.ShapeDtypeStruct((nb, S, D), jnp.float32),
        scratch_shapes=[
            pltpu.VMEM((HALO + ts, pool_width), jnp.float32),
            pltpu.VMEM((ts, pool_width + sgu_width), jnp.bfloat16),
        ],
        compiler_params=pltpu.CompilerParams(
            dimension_semantics=("parallel", "arbitrary"), vmem_limit_bytes=TC_VMEM_LIMIT),
        name="mixer",
    )(x, x, norm_mix.reshape(1, D), w_in.astype(jnp.bfloat16), pool_w.astype(jnp.bfloat16),
      pool_scale.reshape(1, pool_width), ln_g.reshape(1, sgu_width), ln_b.reshape(1, sgu_width),
      sgu_w.astype(jnp.bfloat16),
      jnp.broadcast_to(sgu_b[:, :, None], (SGU_HEADS, SGU_CHUNK, SGU_CHUNK)),
      on_pool.reshape(1, pool_width), on_sgu.reshape(1, sgu_width), w_out.astype(jnp.bfloat16))


def _topk_rows(s, k):
    sub = V7X_SUBLANES
    n, m = s.shape
    pieces = n // sub
    assert n % sub == 0 and pieces >= k, "lists shorter than k would need padding"
    row = lax.broadcasted_iota(jnp.int32, (sub, m), 0)
    v = [s[p * sub:(p + 1) * sub] for p in range(pieces)]
    ix = [row + p * sub for p in range(pieces)]
    for rnd in range(pieces):
        for p in range(rnd % 2, pieces - 1, 2):
            swap = v[p + 1] > v[p]
            v[p], v[p + 1] = jnp.where(swap, v[p + 1], v[p]), jnp.where(swap, v[p], v[p + 1])
            ix[p], ix[p + 1] = jnp.where(swap, ix[p + 1], ix[p]), jnp.where(swap, ix[p], ix[p + 1])
    vals, idxs = [], []
    for r in range(k):
        best = jnp.max(v[0], axis=0, keepdims=True)
        best_ix = jnp.min(jnp.where(v[0] == best, ix[0], n), axis=0, keepdims=True)
        vals.append(best)
        idxs.append(best_ix)
        taken = ix[0] == best_ix
        for p in range(k - 1 - r):
            v[p] = jnp.where(taken, v[p + 1], v[p])
            ix[p] = jnp.where(taken, ix[p + 1], ix[p])
    return vals, idxs


def _pair_candidates(v1, i1, v2, i2):
    k, sub = PEER_TOPK, V7X_SUBLANES
    assert k == 16 and sub == 8, "the tile plan below is written for 16 candidates per side"
    v1c, i1c = jnp.concatenate(v1, axis=0), jnp.concatenate(i1, axis=0)
    v2c, i2c = jnp.concatenate(v2, axis=0), jnp.concatenate(i2, axis=0)
    m = v2c.shape[1]
    r = lax.broadcasted_iota(jnp.int32, (sub, m), 0)

    def with_b(a, b0, row0):
        shift = lambda x: x if row0 == 0 else pltpu.roll(x, row0, 0)
        return (v1[a] + shift(v2c[b0:b0 + sub]), i1[a] * PEER_N_KEYS + shift(i2c[b0:b0 + sub]),
                a * k + b0 + (r - row0))

    def with_a(a0, row0):
        s = (row0 - (a0 - sub)) % sub
        shift = lambda x: x if s == 0 else pltpu.roll(x, s, 0)
        return (shift(v1c[sub:]) + v2[0], shift(i1c[sub:]) * PEER_N_KEYS + i2[0], (a0 + (r - row0)) * k)

    def tile(*parts):
        out = parts[0][1]
        for row0, seg in parts[1:]:
            out = tuple(jnp.where(r < row0, o, s) for o, s in zip(out, seg))
        return out

    tiles = [
        with_b(0, 0, 0), with_b(0, sub, 0),
        with_b(1, 0, 0),
        tile((0, with_b(2, 0, 0)), (5, with_b(4, 0, 5))),
        tile((0, with_b(3, 0, 0)), (4, with_b(5, 0, 4)), (6, with_b(6, 0, 6))),
        tile((0, with_b(7, 0, 0)), (2, with_a(8, 2))),
        tile((0, with_a(14, 0)), (2, with_b(1, sub, 2))),
    ]
    return tuple(jnp.concatenate([t[n] for t in tiles], axis=0) for n in range(3))


def _router_kernel(x_ref, nffn_ref, wq_ref, keys_ref, after_ref, h2_ref, idx_ref, gate_ref,
                   q_ref, idxt_ref, gatet_ref):
    del after_ref
    h2 = _rms(x_ref[...], nffn_ref[...]).astype(jnp.bfloat16)
    h2_ref[...] = h2
    q_ref[...] = jnp.dot(h2, wq_ref[...], preferred_element_type=jnp.float32).astype(jnp.bfloat16)
    dq = 2 * PEER_D_HALF
    nt = (((1,), (1,)), ((), ()))

    def head(hd, carry):
        off = pl.multiple_of(hd * dq, dq)
        s1 = lax.dot_general(keys_ref[0], q_ref[:, pl.ds(off, PEER_D_HALF)], nt,
                             preferred_element_type=jnp.float32)
        s2 = lax.dot_general(keys_ref[1], q_ref[:, pl.ds(off + PEER_D_HALF, PEER_D_HALF)], nt,
                             preferred_element_type=jnp.float32)
        v1, i1 = _topk_rows(s1, PEER_TOPK)
        v2, i2 = _topk_rows(s2, PEER_TOPK)
        cand, expert, flat = _pair_candidates(v1, i1, v2, i2)
        cv, ce = [], []
        for _ in range(PEER_TOPK):
            m = jnp.max(cand, axis=0, keepdims=True)
            ix = jnp.min(jnp.where(cand == m, flat, PEER_TOPK * PEER_TOPK), axis=0, keepdims=True)
            hit = flat == ix
            cv.append(m)
            ce.append(jnp.max(jnp.where(hit, expert, -1), axis=0, keepdims=True))
            cand = jnp.where(hit, -jnp.inf, cand)
        cvc = jnp.concatenate(cv, axis=0)
        e = jnp.exp(cvc - cv[0])
        gate = e / jnp.sum(e, axis=0, keepdims=True)
        row = pl.multiple_of(hd * PEER_TOPK, PEER_TOPK)
        idxt_ref[pl.ds(row, PEER_TOPK), :] = jnp.concatenate(ce, axis=0)
        gatet_ref[pl.ds(row, PEER_TOPK), :] = gate
        return carry

    lax.fori_loop(0, PEER_HEADS, head, 0)
    idx_ref[...] = idxt_ref[...].T
    gate_ref[...] = gatet_ref[...].T


def _router(x1, norm_ffn, wq, keys, tok0, T, after):
    D = x1.shape[1]
    tr = min(ROUTE_TILE, T)
    qw = wq.shape[1]
    assert tok0 % tr == 0 and T % tr == 0
    first = tok0 // tr
    full = lambda shape: pl.BlockSpec(shape, lambda i: (0,) * len(shape))
    return pl.pallas_call(
        _router_kernel,
        grid=(T // tr,),
        in_specs=[
            pl.BlockSpec((tr, D), lambda i: (first + i, 0)),
            full((1, D)),
            full((D, qw)),
            full((2, PEER_N_KEYS, PEER_D_HALF)),
            pl.BlockSpec(memory_space=pl.ANY),
        ],
        out_specs=[
            pl.BlockSpec((tr, D), lambda i: (i, 0)),
            pl.BlockSpec((tr, EXPERTS_PER_TOKEN), lambda i: (i, 0)),
            pl.BlockSpec((tr, EXPERTS_PER_TOKEN), lambda i: (i, 0)),
        ],
        out_shape=[
            jax.ShapeDtypeStruct((T, D), jnp.bfloat16),
            jax.ShapeDtypeStruct((T, EXPERTS_PER_TOKEN), jnp.int32),
            jax.ShapeDtypeStruct((T, EXPERTS_PER_TOKEN), jnp.float32),
        ],
        scratch_shapes=[
            pltpu.VMEM((tr, qw), jnp.bfloat16),
            pltpu.VMEM((EXPERTS_PER_TOKEN, tr), jnp.int32),
            pltpu.VMEM((EXPERTS_PER_TOKEN, tr), jnp.float32),
        ],
        compiler_params=pltpu.CompilerParams(
            dimension_semantics=("parallel",), vmem_limit_bytes=TC_VMEM_LIMIT),
        name="router",
    )(x1, norm_ffn.reshape(1, D), wq, keys, after)


def _scores_kernel(h_ref, ut_lo_ref, ut_hi_ref, idx_ref, gate_ref, w_ref, pre_ref):
    j = pl.program_id(1)
    half_groups = ut_lo_ref.shape[1] // V7X_LANES
    lane_bits = int(math.log2(V7X_LANES))
    idx = idx_ref[...]
    group, lane = idx >> lane_bits, idx & (V7X_LANES - 1)
    picked = jnp.zeros(idx.shape, jnp.float32)
    for half, ut_ref in enumerate((ut_lo_ref, ut_hi_ref)):
        acc = jnp.dot(h_ref[...], ut_ref[...], preferred_element_type=jnp.float32)
        for n in range(half_groups):
            vals = jnp.take_along_axis(acc[:, n * V7X_LANES:(n + 1) * V7X_LANES], lane, axis=1)
            picked = jnp.where(group == (2 * j + half) * half_groups + n, vals, picked)

    @pl.when(j == 0)
    def _():
        pre_ref[...] = picked

    @pl.when(j > 0)
    def _():
        pre_ref[...] += picked

    @pl.when(j == pl.num_programs(1) - 1)
    def _():
        w = (gate_ref[...] * _gelu(pre_ref[...])).astype(jnp.bfloat16).astype(jnp.float32)
        hi = pltpu.bitcast(w, jnp.uint32) & jnp.uint32(HI_HALF)
        w_ref[...] = pltpu.bitcast(hi | (hi >> 16), jnp.int32)


def _expert_weights(h2, u_t, idx, gate):
    T, D = h2.shape
    E = u_t.shape[1]
    K = idx.shape[1]
    tm, tn = min(SCORE_TOK_TILE, T), min(SCORE_EXP_TILE, E)
    assert T % tm == 0 and E % tn == 0 and tn % (2 * V7X_LANES) == 0 and K == V7X_LANES
    per_token = pl.BlockSpec((tm, K), lambda i, j: (i, 0))
    return pl.pallas_call(
        _scores_kernel,
        grid=(T // tm, E // tn),
        in_specs=[pl.BlockSpec((tm, D), lambda i, j: (i, 0)),
                  pl.BlockSpec((D, tn // 2), lambda i, j: (0, 2 * j)),
                  pl.BlockSpec((D, tn // 2), lambda i, j: (0, 2 * j + 1)),
                  per_token, per_token],
        out_specs=per_token,
        out_shape=jax.ShapeDtypeStruct((T, K), jnp.int32),
        scratch_shapes=[pltpu.VMEM((tm, K), jnp.float32)],
        compiler_params=pltpu.CompilerParams(
            dimension_semantics=("parallel", "arbitrary"), vmem_limit_bytes=TC_VMEM_LIMIT),
        name="expert_weights",
    )(h2, u_t, u_t, idx, gate)


def _final_kernel(x_ref, y_ref, g_ref, *rest):
    o_ref = rest[-1]
    o_ref[...] = _rms(x_ref[...] + y_ref[...], g_ref[...])


def _final(x1, x_tok0, peer, norm_final, out_prev, out_tok0, total):
    Tc, D = peer.shape
    te = min(EW_TILE, Tc)
    assert x_tok0 % te == 0 and out_tok0 % te == 0 and Tc % te == 0
    nblk = Tc // te
    spec = pl.BlockSpec((te, D), lambda i: (out_tok0 // te + i, 0))
    in_specs = [pl.BlockSpec((te, D), lambda i: (x_tok0 // te + i, 0)),
                pl.BlockSpec((te, D), lambda i: (i, 0)), pl.BlockSpec((1, D), lambda i: (0, 0))]
    args = [x1, peer, norm_final.reshape(1, D)]
    aliases = {}
    if out_prev is not None:
        in_specs.append(pl.BlockSpec(memory_space=pl.ANY))
        args.append(out_prev)
        aliases = {3: 0}
    return pl.pallas_call(
        _final_kernel, grid=(nblk,), in_specs=in_specs, out_specs=spec,
        out_shape=jax.ShapeDtypeStruct((total, D), jnp.float32),
        input_output_aliases=aliases,
        compiler_params=pltpu.CompilerParams(dimension_semantics=("parallel",)),
        name="final_norm",
    )(*args)


def _tree_sum(vals):
    while len(vals) > 1:
        nxt = [vals[i] + vals[i + 1] for i in range(0, len(vals) - 1, 2)]
        if len(vals) % 2:
            nxt.append(vals[-1])
        vals = nxt
    return vals[0]


def _sc_block_pipeline(nblk, items_per_token, loads, store, gather, compute):
    ipt = items_per_token
    assert SC_ROW_BUFS > ipt and nblk >= 1

    for c in loads(0, 0):
        c.start()
    for c in loads(0, 0):
        c.wait()
    if nblk > 1:
        for c in loads(1, 1):
            c.start()
    for q in range(ipt):
        gather(0, 0, q, q).start()

    @pl.loop(0, nblk)
    def _(b):
        slot = b % 2

        @pl.when(b >= 2)
        def _():
            store(b - 2, slot).wait()

        @pl.loop(0, SC_TOKENS)
        def _(t):
            item0 = (b * SC_TOKENS + t) * ipt
            for q in range(ipt):
                nxt_buf = (item0 + q + ipt) % SC_ROW_BUFS

                @pl.when(t + 1 < SC_TOKENS)
                def _():
                    gather(slot, t + 1, q, nxt_buf).start()

                @pl.when(jnp.logical_and(t + 1 == SC_TOKENS, b + 1 < nblk))
                def _():
                    if q == 0:
                        for c in loads(b + 1, 1 - slot):
                            c.wait()
                    gather(1 - slot, 0, q, nxt_buf).start()

                buf = (item0 + q) % SC_ROW_BUFS
                gather(slot, t, q, buf).wait()
                compute(slot, t, q, buf)

        store(b, slot).start()

        @pl.when(b + 2 < nblk)
        def _():
            for c in loads(b + 2, slot):
                c.start()

    if nblk >= 2:
        store(nblk - 2, nblk % 2).wait()
    store(nblk - 1, (nblk - 1) % 2).wait()


def _sc_mesh():
    return plsc.VectorSubcoreMesh(core_axis_name="c", subcore_axis_name="s")


def _sc_worker_id():
    return lax.axis_index("s") * V7X_SC_CORES + lax.axis_index("c")


def _sc_bf16(words):
    return plsc.bitcast(words, jnp.bfloat16)


def _sc_halves_f32(pairs):
    words = plsc.bitcast(pairs, jnp.uint32)
    return (plsc.bitcast(words << 16, jnp.float32),
            plsc.bitcast(words & jnp.uint32(HI_HALF), jnp.float32))


def _mix_parts(T, K, DW):
    D = 2 * DW
    L, G = V7X_SC_LANES, SC_GATHER
    nj = SC_MIX_CHUNK // L
    tok_per_w = T // V7X_SC_WORKERS
    assert T % (V7X_SC_WORKERS * SC_TOKENS) == 0 and K % (2 * G) == 0
    assert DW % SC_MIX_CHUNK == 0 and G % SC_MIX_GROUP == 0
    scratch = [
        pltpu.VMEM((2, SC_TOKENS, K), jnp.int32),
        pltpu.VMEM((2, SC_TOKENS, K), jnp.int32),
        pltpu.VMEM((SC_ROW_BUFS, G, DW), jnp.uint32),
        pltpu.VMEM((2, SC_TOKENS, D), jnp.float32),
        pltpu.SemaphoreType.DMA((2,)),
        pltpu.SemaphoreType.DMA((2,)),
        pltpu.SemaphoreType.DMA((2,)),
        pltpu.SemaphoreType.DMA((SC_ROW_BUFS,)),
    ]

    def run(w_hbm, idx_hbm, tab_hbm, out_hbm, idx_v, w_v, rows_v, out_v, idx_sems, w_sems, out_sems, row_sems):
        base = _sc_worker_id() * tok_per_w

        def loads(b, slot):
            toks = pl.ds(base + b * SC_TOKENS, SC_TOKENS)
            return [pltpu.make_async_copy(idx_hbm.at[toks], idx_v.at[slot], idx_sems.at[slot]),
                    pltpu.make_async_copy(w_hbm.at[toks], w_v.at[slot], w_sems.at[slot])]

        def store(b, slot):
            toks = pl.ds(base + b * SC_TOKENS, SC_TOKENS)
            return pltpu.make_async_copy(out_v.at[slot], out_hbm.at[toks], out_sems.at[slot])

        def gather(slot, t, q, buf):
            return pltpu.make_async_copy(
                tab_hbm.at[idx_v.at[slot, t, pl.ds(q * G, G)]], rows_v.at[buf], row_sems.at[buf])

        def compute(slot, t, q, buf):
            ssplat = jnp.full((L,), slot, jnp.int32)
            tsplat = jnp.full((L,), t, jnp.int32)
            for c in range(DW // SC_MIX_CHUNK):
                def body(kg, acc):
                    kk = kg * SC_MIX_GROUP
                    wks = [_sc_bf16(plsc.load_gather(
                        w_v, [ssplat, tsplat, jnp.full((L,), q * G + i, jnp.int32) + kk]))
                        for i in range(SC_MIX_GROUP)]
                    out = []
                    for j in range(nj):
                        prods = [wks[i] * _sc_bf16(rows_v[buf, kk + i, pl.ds(c * SC_MIX_CHUNK + j * L, L)])
                                 for i in range(SC_MIX_GROUP)]
                        lo, hi = _sc_halves_f32(_tree_sum(prods))
                        out += [acc[2 * j] + lo, acc[2 * j + 1] + hi]
                    return tuple(out)

                zero = jnp.zeros((L,), jnp.float32)
                acc = plsc.parallel_loop(0, G // SC_MIX_GROUP, carry=(zero,) * (2 * nj))(body)
                for j in range(nj):
                    for half in range(2):
                        dst = out_v.at[slot, t, pl.ds(half * DW + c * SC_MIX_CHUNK + j * L, L)]
                        if q == 0:
                            dst[...] = acc[2 * j + half]
                        else:
                            plsc.addupdate(dst, acc[2 * j + half])

        _sc_block_pipeline(tok_per_w // SC_TOKENS, K // G, loads, store, gather, compute)

    return scratch, run


def _sc_kernel(out_type, scratch_types, name):
    return functools.partial(
        pl.kernel, mesh=_sc_mesh(), out_type=out_type, scratch_types=scratch_types,
        compiler_params=pltpu.CompilerParams(needs_layout_passes=False), name=name)


def _pack_table(a):
    n, c = a.shape
    half = c // 2
    L, rb = V7X_SC_LANES, SC_PACK_ROWS
    rows_per_w = n // V7X_SC_WORKERS
    nblk = rows_per_w // rb
    assert n % (V7X_SC_WORKERS * rb) == 0 and half % L == 0

    @_sc_kernel(jax.ShapeDtypeStruct((n, half), jnp.uint32),
                [pltpu.VMEM((2, rb, c), jnp.float32), pltpu.VMEM((2, rb, half), jnp.uint32),
                 pltpu.SemaphoreType.DMA((2,)), pltpu.SemaphoreType.DMA((2,))], "pack_table")
    def k(a_hbm, o_hbm, a_v, o_v, in_sems, out_sems):
        base = _sc_worker_id() * rows_per_w

        def load(b, slot):
            return pltpu.make_async_copy(a_hbm.at[pl.ds(base + b * rb, rb)], a_v.at[slot], in_sems.at[slot])

        def store(b, slot):
            return pltpu.make_async_copy(o_v.at[slot], o_hbm.at[pl.ds(base + b * rb, rb)], out_sems.at[slot])

        load(0, 0).start()

        @pl.loop(0, nblk)
        def _(b):
            slot = b % 2
            load(b, slot).wait()

            @pl.when(b + 1 < nblk)
            def _():
                load(b + 1, 1 - slot).start()

            @pl.when(b >= 2)
            def _():
                store(b - 2, slot).wait()

            @plsc.parallel_loop(0, rb)
            def _(r):
                for j in range(half // L):
                    pairs = plsc.pack(a_v[slot, r, pl.ds(j * L, L)], a_v[slot, r, pl.ds(half + j * L, L)],
                                      format=plsc.PackFormat.INTERLEAVED)
                    o_v[slot, r, pl.ds(j * L, L)] = plsc.bitcast(pairs, jnp.uint32)

            store(b, slot).start()

        if nblk >= 2:
            store(nblk - 2, nblk % 2).wait()
        store(nblk - 1, (nblk - 1) % 2).wait()

    return k(a)


def _expert_mix(w, idx, table):
    T, K = w.shape
    DW = table.shape[1]
    scratch, run = _mix_parts(T, K, DW)

    @_sc_kernel(jax.ShapeDtypeStruct((T, 2 * DW), jnp.float32), scratch, "expert_mix")
    def k(w_hbm, idx_hbm, tab_hbm, out_hbm, *mix_scratch):
        run(w_hbm, idx_hbm, tab_hbm, out_hbm, *mix_scratch)

    return k(w, idx, table)


def _chunk_sizes(total):
    ramp, size = [EDGE_CHUNK], EDGE_CHUNK
    while size < MAX_CHUNK:
        ramp.append(size)
        size *= 2
    middle = total - 2 * sum(ramp)
    assert middle >= 0 and middle % MAX_CHUNK == 0
    return ramp + [MAX_CHUNK] * (middle // MAX_CHUNK) + ramp[::-1]


def kernel(x, norm_mix, w_in, pool_w, pool_scale, sgu_ln_g, sgu_ln_b, sgu_w, sgu_b, out_norm_pool,
           out_norm_sgu, w_out, norm_ffn, peer_wq, peer_keys, peer_u, peer_v, norm_final):
    B, S, D = x.shape
    assert norm_mix.shape[0] == 1, "single-layer block"
    T = B * S
    mix_args = (norm_mix[0], w_in[0], pool_w[0], pool_scale[0], sgu_ln_g[0], sgu_ln_b[0],
                sgu_w[0], sgu_b[0], out_norm_pool[0], out_norm_sgu[0], w_out[0])
    x1_parts = [(b * S, _mixer(x, b, 1, *mix_args).reshape(S, D)) for b in range(B)]
    wq = peer_wq[0].astype(jnp.bfloat16)
    keys = peer_keys[0].astype(jnp.bfloat16)
    u_t = _transposed_bf16(peer_u[0])
    v_tab = _pack_table(peer_v[0])
    out = None
    tok0 = 0
    peers = [norm_ffn, v_tab]
    outs = []
    chunks = _chunk_sizes(T)
    for c, tc in enumerate(chunks):
        part0, x1 = [p for p in x1_parts if p[0] <= tok0][-1]
        assert tok0 + tc <= part0 + x1.shape[0], "a token chunk must lie inside one mixer call"
        after = outs[-2] if c >= max(len(chunks) - 2, 2) else peers[-2]
        h2, idx, gate = _router(x1, norm_ffn[0], wq, keys, tok0 - part0, tc, after)
        w = _expert_weights(h2, u_t, idx, gate)
        peer = _expert_mix(w, idx, v_tab)
        peers.append(peer)
        out = _final(x1, tok0 - part0, peer, norm_final, out, tok0, T)
        outs.append(out)
        tok0 += tc
    return out.reshape(B, S, D)
```

```python
import functools
import math

import jax
import jax.numpy as jnp
from jax import lax
from jax.experimental import pallas as pl
from jax.experimental.pallas import tpu as pltpu
from jax.experimental.pallas import tpu_sc as plsc

POOL_WINDOWS = (2, 4, 8, 16)
N_POOL_GROUPS = len(POOL_WINDOWS)
SGU_HEADS = 4
SGU_CHUNK = 128
PEER_HEADS = 8
PEER_N_KEYS = 128
PEER_D_HALF = 128
PEER_TOPK = 16
NORM_EPS = 1e-6
EXPERTS_PER_TOKEN = PEER_HEADS * PEER_TOPK

V7X_LANES = 128
V7X_SUBLANES = 8
V7X_SC_CORES = 2
V7X_SC_SUBCORES = 16
V7X_SC_LANES = 16
V7X_SC_WORKERS = V7X_SC_CORES * V7X_SC_SUBCORES

HALO = max(POOL_WINDOWS)
MIX_TILE = 512
ROUTE_TILE = 1024
EW_TILE = 512
SCORE_TOK_TILE = 2048
SCORE_EXP_TILE = 2048
SC_PACK_ROWS = 16
SC_GATHER = 64
SC_TOKENS = 8
SC_ROW_BUFS = 3
SC_MIX_CHUNK = 128
SC_MIX_GROUP = 4
HI_HALF = 0xFFFF0000
V7X_VMEM_BYTES = 64 * 1024 * 1024
TC_VMEM_LIMIT = V7X_VMEM_BYTES * 3 // 4
EDGE_CHUNK = 512
MAX_CHUNK = 2048


def _rms(x, g):
    inv = lax.rsqrt(jnp.mean(x * x, axis=-1, keepdims=True) + NORM_EPS)
    return x * inv * g


def _transpose_cast_kernel(a_ref, o_ref):
    o_ref[...] = a_ref[...].T.astype(o_ref.dtype)


def _transposed_bf16(a):
    n, c = a.shape
    rows = min(SCORE_EXP_TILE, n)
    return pl.pallas_call(
        _transpose_cast_kernel, grid=(n // rows,),
        in_specs=[pl.BlockSpec((rows, c), lambda i: (i, 0))],
        out_specs=pl.BlockSpec((c, rows), lambda i: (0, i)),
        out_shape=jax.ShapeDtypeStruct((c, n), jnp.bfloat16),
        compiler_params=pltpu.CompilerParams(dimension_semantics=("parallel",)),
        name="transpose_table",
    )(a)


def _gelu(x):
    return 0.5 * x * (1.0 + lax.erf(x * math.sqrt(0.5)))


def _mixer_kernel(x_ref, xh_ref, nmix_ref, win_ref, poolw_ref, pscale_ref, lng_ref, lnb_ref,
                  sguw_ref, sgub_ref, onp_ref, ons_ref, wout_ref, o_ref, pext_ref, mix_ref):
    i = pl.program_id(1)
    ts = x_ref.shape[1]
    pool_w = pscale_ref.shape[1]
    gdim = pool_w // N_POOL_GROUPS
    sgu_w = lng_ref.shape[1]
    hdim = sgu_w // SGU_HEADS

    x = x_ref[0]
    h = _rms(x, nmix_ref[...]).astype(jnp.bfloat16)
    z = jnp.dot(h, win_ref[...], preferred_element_type=jnp.float32)
    p = z[:, :pool_w]

    hh = _rms(xh_ref[0], nmix_ref[...]).astype(jnp.bfloat16)
    ph = jnp.dot(hh, win_ref[:, :pool_w], preferred_element_type=jnp.float32)
    ph = jnp.where(i > 0, ph, 0.0)
    pext_ref[0:HALO, :] = ph
    pext_ref[HALO:HALO + ts, :] = p

    pos = i * ts + lax.broadcasted_iota(jnp.int32, (ts, 1), 0)
    ssq = jnp.zeros((ts, 1), jnp.float32)
    a_parts = []
    for g, win in enumerate(POOL_WINDOWS):
        cols = slice(g * gdim, (g + 1) * gdim)
        s = pext_ref[HALO:HALO + ts, cols]
        for j in range(1, win):
            s = s + pext_ref[HALO - j:HALO - j + ts, cols]
        cnt = jnp.minimum(pos + 1, win).astype(jnp.float32)
        d = (s / cnt - p[:, cols]).astype(jnp.bfloat16)
        a = jnp.dot(d, poolw_ref[g], preferred_element_type=jnp.float32) * pscale_ref[:, cols]
        ssq = ssq + jnp.sum(a * a, axis=-1, keepdims=True)
        a_parts.append(a)
    inv_a = lax.rsqrt(ssq / pool_w + NORM_EPS)
    for g in range(N_POOL_GROUPS):
        cols = slice(g * gdim, (g + 1) * gdim)
        mix_ref[:, cols] = (a_parts[g] * inv_a * onp_ref[:, cols]).astype(jnp.bfloat16)

    gz = _gelu(z[:, pool_w:])
    tril = (lax.broadcasted_iota(jnp.int32, (SGU_CHUNK, SGU_CHUNK), 0)
            >= lax.broadcasted_iota(jnp.int32, (SGU_CHUNK, SGU_CHUNK), 1))
    ssq = jnp.zeros((ts, 1), jnp.float32)
    b_parts = []
    for hd in range(SGU_HEADS):
        cols = slice(hd * hdim, (hd + 1) * hdim)
        u = gz[:, hd * hdim:(hd + 1) * hdim]
        v = gz[:, sgu_w + hd * hdim:sgu_w + (hd + 1) * hdim]
        mu = jnp.mean(v, axis=-1, keepdims=True)
        vc = v - mu
        var = jnp.mean(vc * vc, axis=-1, keepdims=True)
        vn = (vc * lax.rsqrt(var + NORM_EPS) * lng_ref[:, cols] + lnb_ref[:, cols]).astype(jnp.bfloat16)
        w = jnp.where(tril, sguw_ref[hd], jnp.zeros((), sguw_ref.dtype))
        mixed = [jnp.dot(w, vn[n * SGU_CHUNK:(n + 1) * SGU_CHUNK], preferred_element_type=jnp.float32)
                 + sgub_ref[hd] for n in range(ts // SGU_CHUNK)]
        b = u * jnp.concatenate(mixed, axis=0)
        ssq = ssq + jnp.sum(b * b, axis=-1, keepdims=True)
        b_parts.append(b)
    inv_b = lax.rsqrt(ssq / sgu_w + NORM_EPS)
    for hd in range(SGU_HEADS):
        cols = slice(hd * hdim, (hd + 1) * hdim)
        mix_ref[:, pool_w + hd * hdim:pool_w + (hd + 1) * hdim] = (
            b_parts[hd] * inv_b * ons_ref[:, cols]).astype(jnp.bfloat16)

    o_ref[0] = x + jnp.dot(mix_ref[...], wout_ref[...], preferred_element_type=jnp.float32)


def _mixer(x, b0, nb, norm_mix, w_in, pool_w, pool_scale, ln_g, ln_b, sgu_w, sgu_b, on_pool, on_sgu, w_out):
    _, S, D = x.shape
    ts = min(MIX_TILE, S)
    pool_width = pool_scale.size
    sgu_width = ln_g.size
    in_width = w_in.shape[1]
    gdim = pool_width // N_POOL_GROUPS
    halo_blocks = ts // HALO
    full = lambda shape: pl.BlockSpec(shape, lambda b, i: (0,) * len(shape))
    return pl.pallas_call(
        _mixer_kernel,
        grid=(nb, S // ts),
        in_specs=[
            pl.BlockSpec((1, ts, D), lambda b, i: (b0 + b, i, 0)),
            pl.BlockSpec((1, HALO, D), lambda b, i: (b0 + b, jnp.maximum(i * halo_blocks - 1, 0), 0)),
            full((1, D)),
            full((D, in_width)),
            full((N_POOL_GROUPS, gdim, gdim)),
            full((1, pool_width)),
            full((1, sgu_width)),
            full((1, sgu_width)),
            full((SGU_HEADS, SGU_CHUNK, SGU_CHUNK)),
            full((SGU_HEADS, SGU_CHUNK, SGU_CHUNK)),
            full((1, pool_width)),
            full((1, sgu_width)),
            full((pool_width + sgu_width, D)),
        ],
        out_specs=pl.BlockSpec((1, ts, D), lambda b, i: (b, i, 0)),
        out_shape=jax.ShapeDtypeStruct((nb, S, D), jnp.float32),
        scratch_shapes=[
            pltpu.VMEM((HALO + ts, pool_width), jnp.float32),
            pltpu.VMEM((ts, pool_width + sgu_width), jnp.bfloat16),
        ],
        compiler_params=pltpu.CompilerParams(
            dimension_semantics=("parallel", "arbitrary"), vmem_limit_bytes=TC_VMEM_LIMIT),
        name="mixer",
    )(x, x, norm_mix.reshape(1, D), w_in.astype(jnp.bfloat16), pool_w.astype(jnp.bfloat16),
      pool_scale.reshape(1, pool_width), ln_g.reshape(1, sgu_width), ln_b.reshape(1, sgu_width),
      sgu_w.astype(jnp.bfloat16),
      jnp.broadcast_to(sgu_b[:, :, None], (SGU_HEADS, SGU_CHUNK, SGU_CHUNK)),
      on_pool.reshape(1, pool_width), on_sgu.reshape(1, sgu_width), w_out.astype(jnp.bfloat16))


def _topk_rows(s, k):
    sub = V7X_SUBLANES
    n, m = s.shape
    pieces = n // sub
    assert n % sub == 0 and pieces >= k, "lists shorter than k would need padding"
    row = lax.broadcasted_iota(jnp.int32, (sub, m), 0)
    v = [s[p * sub:(p + 1) * sub] for p in range(pieces)]
    ix = [row + p * sub for p in range(pieces)]
    for rnd in range(pieces):
        for p in range(rnd % 2, pieces - 1, 2):
            swap = v[p + 1] > v[p]
            v[p], v[p + 1] = jnp.where(swap, v[p + 1], v[p]), jnp.where(swap, v[p], v[p + 1])
            ix[p], ix[p + 1] = jnp.where(swap, ix[p + 1], ix[p]), jnp.where(swap, ix[p], ix[p + 1])
    vals, idxs = [], []
    for r in range(k):
        best = jnp.max(v[0], axis=0, keepdims=True)
        best_ix = jnp.min(jnp.where(v[0] == best, ix[0], n), axis=0, keepdims=True)
        vals.append(best)
        idxs.append(best_ix)
        taken = ix[0] == best_ix
        for p in range(k - 1 - r):
            v[p] = jnp.where(taken, v[p + 1], v[p])
            ix[p] = jnp.where(taken, ix[p + 1], ix[p])
    return vals, idxs


def _pair_candidates(v1, i1, v2, i2):
    k, sub = PEER_TOPK, V7X_SUBLANES
    assert k == 16 and sub == 8, "the tile plan below is written for 16 candidates per side"
    v1c, i1c = jnp.concatenate(v1, axis=0), jnp.concatenate(i1, axis=0)
    v2c, i2c = jnp.concatenate(v2, axis=0), jnp.concatenate(i2, axis=0)
    m = v2c.shape[1]
    r = lax.broadcasted_iota(jnp.int32, (sub, m), 0)

    def with_b(a, b0, row0):
        shift = lambda x: x if row0 == 0 else pltpu.roll(x, row0, 0)
        return (v1[a] + shift(v2c[b0:b0 + sub]), i1[a] * PEER_N_KEYS + shift(i2c[b0:b0 + sub]),
                a * k + b0 + (r - row0))

    def with_a(a0, row0):
        s = (row0 - (a0 - sub)) % sub
        shift = lambda x: x if s == 0 else pltpu.roll(x, s, 0)
        return (shift(v1c[sub:]) + v2[0], shift(i1c[sub:]) * PEER_N_KEYS + i2[0], (a0 + (r - row0)) * k)

    def tile(*parts):
        out = parts[0][1]
        for row0, seg in parts[1:]:
            out = tuple(jnp.where(r < row0, o, s) for o, s in zip(out, seg))
        return out

    tiles = [
        with_b(0, 0, 0), with_b(0, sub, 0),
        with_b(1, 0, 0),
        tile((0, with_b(2, 0, 0)), (5, with_b(4, 0, 5))),
        tile((0, with_b(3, 0, 0)), (4, with_b(5, 0, 4)), (6, with_b(6, 0, 6))),
        tile((0, with_b(7, 0, 0)), (2, with_a(8, 2))),
        tile((0, with_a(14, 0)), (2, with_b(1, sub, 2))),
    ]
    return tuple(jnp.concatenate([t[n] for t in tiles], axis=0) for n in range(3))


def _router_kernel(x_ref, nffn_ref, wq_ref, keys_ref, after_ref, h2_ref, idx_ref, gate_ref,
                   q_ref, idxt_ref, gatet_ref):
    del after_ref
    h2 = _rms(x_ref[...], nffn_ref[...]).astype(jnp.bfloat16)
    h2_ref[...] = h2
    q_ref[...] = jnp.dot(h2, wq_ref[...], preferred_element_type=jnp.float32).astype(jnp.bfloat16)
    dq = 2 * PEER_D_HALF
    nt = (((1,), (1,)), ((), ()))

    def head(hd, carry):
        off = pl.multiple_of(hd * dq, dq)
        s1 = lax.dot_general(keys_ref[0], q_ref[:, pl.ds(off, PEER_D_HALF)], nt,
                             preferred_element_type=jnp.float32)
        s2 = lax.dot_general(keys_ref[1], q_ref[:, pl.ds(off + PEER_D_HALF, PEER_D_HALF)], nt,
                             preferred_element_type=jnp.float32)
        v1, i1 = _topk_rows(s1, PEER_TOPK)
        v2, i2 = _topk_rows(s2, PEER_TOPK)
        cand, expert, flat = _pair_candidates(v1, i1, v2, i2)
        cv, ce = [], []
        for _ in range(PEER_TOPK):
            m = jnp.max(cand, axis=0, keepdims=True)
            ix = jnp.min(jnp.where(cand == m, flat, PEER_TOPK * PEER_TOPK), axis=0, keepdims=True)
            hit = flat == ix
            cv.append(m)
            ce.append(jnp.max(jnp.where(hit, expert, -1), axis=0, keepdims=True))
            cand = jnp.where(hit, -jnp.inf, cand)
        cvc = jnp.concatenate(cv, axis=0)
        e = jnp.exp(cvc - cv[0])
        gate = e / jnp.sum(e, axis=0, keepdims=True)
        row = pl.multiple_of(hd * PEER_TOPK, PEER_TOPK)
        idxt_ref[pl.ds(row, PEER_TOPK), :] = jnp.concatenate(ce, axis=0)
        gatet_ref[pl.ds(row, PEER_TOPK), :] = gate
        return carry

    lax.fori_loop(0, PEER_HEADS, head, 0)
    idx_ref[...] = idxt_ref[...].T
    gate_ref[...] = gatet_ref[...].T


def _router(x1, norm_ffn, wq, keys, tok0, T, after):
    D = x1.shape[1]
    tr = min(ROUTE_TILE, T)
    qw = wq.shape[1]
    assert tok0 % tr == 0 and T % tr == 0
    first = tok0 // tr
    full = lambda shape: pl.BlockSpec(shape, lambda i: (0,) * len(shape))
    return pl.pallas_call(
        _router_kernel,
        grid=(T // tr,),
        in_specs=[
            pl.BlockSpec((tr, D), lambda i: (first + i, 0)),
            full((1, D)),
            full((D, qw)),
            full((2, PEER_N_KEYS, PEER_D_HALF)),
            pl.BlockSpec(memory_space=pl.ANY),
        ],
        out_specs=[
            pl.BlockSpec((tr, D), lambda i: (i, 0)),
            pl.BlockSpec((tr, EXPERTS_PER_TOKEN), lambda i: (i, 0)),
            pl.BlockSpec((tr, EXPERTS_PER_TOKEN), lambda i: (i, 0)),
        ],
        out_shape=[
            jax.ShapeDtypeStruct((T, D), jnp.bfloat16),
            jax.ShapeDtypeStruct((T, EXPERTS_PER_TOKEN), jnp.int32),
            jax.ShapeDtypeStruct((T, EXPERTS_PER_TOKEN), jnp.float32),
        ],
        scratch_shapes=[
            pltpu.VMEM((tr, qw), jnp.bfloat16),
            pltpu.VMEM((EXPERTS_PER_TOKEN, tr), jnp.int32),
            pltpu.VMEM((EXPERTS_PER_TOKEN, tr), jnp.float32),
        ],
        compiler_params=pltpu.CompilerParams(
            dimension_semantics=("parallel",), vmem_limit_bytes=TC_VMEM_LIMIT),
        name="router",
    )(x1, norm_ffn.reshape(1, D), wq, keys, after)


def _scores_kernel(h_ref, ut_ref, idx_ref, gate_ref, w_ref, pre_ref):
    j = pl.program_id(1)
    acc = jnp.dot(h_ref[...], ut_ref[...], preferred_element_type=jnp.float32)
    groups = acc.shape[1] // V7X_LANES
    lane_bits = int(math.log2(V7X_LANES))
    idx = idx_ref[...]
    group, lane = idx >> lane_bits, idx & (V7X_LANES - 1)
    picked = jnp.zeros(idx.shape, jnp.float32)
    for n in range(groups):
        vals = jnp.take_along_axis(acc[:, n * V7X_LANES:(n + 1) * V7X_LANES], lane, axis=1)
        picked = jnp.where(group == j * groups + n, vals, picked)

    @pl.when(j == 0)
    def _():
        pre_ref[...] = picked

    @pl.when(j > 0)
    def _():
        pre_ref[...] += picked

    @pl.when(j == pl.num_programs(1) - 1)
    def _():
        w = (gate_ref[...] * _gelu(pre_ref[...])).astype(jnp.bfloat16).astype(jnp.float32)
        hi = pltpu.bitcast(w, jnp.uint32) & jnp.uint32(HI_HALF)
        w_ref[...] = pltpu.bitcast(hi | (hi >> 16), jnp.int32)


def _expert_weights(h2, u_t, idx, gate):
    T, D = h2.shape
    E = u_t.shape[1]
    K = idx.shape[1]
    tm, tn = min(SCORE_TOK_TILE, T), min(SCORE_EXP_TILE, E)
    assert T % tm == 0 and E % tn == 0 and tn % V7X_LANES == 0 and K == V7X_LANES
    per_token = pl.BlockSpec((tm, K), lambda i, j: (i, 0))
    return pl.pallas_call(
        _scores_kernel,
        grid=(T // tm, E // tn),
        in_specs=[pl.BlockSpec((tm, D), lambda i, j: (i, 0)), pl.BlockSpec((D, tn), lambda i, j: (0, j)),
                  per_token, per_token],
        out_specs=per_token,
        out_shape=jax.ShapeDtypeStruct((T, K), jnp.int32),
        scratch_shapes=[pltpu.VMEM((tm, K), jnp.float32)],
        compiler_params=pltpu.CompilerParams(
            dimension_semantics=("parallel", "arbitrary"), vmem_limit_bytes=TC_VMEM_LIMIT),
        name="expert_weights",
    )(h2, u_t, idx, gate)


def _final_kernel(x_ref, y_ref, g_ref, *rest):
    o_ref = rest[-1]
    o_ref[...] = _rms(x_ref[...] + y_ref[...], g_ref[...])


def _final(x1, x_tok0, peer, norm_final, out_prev, out_tok0, total):
    Tc, D = peer.shape
    te = min(EW_TILE, Tc)
    assert x_tok0 % te == 0 and out_tok0 % te == 0 and Tc % te == 0
    nblk = Tc // te
    spec = pl.BlockSpec((te, D), lambda i: (out_tok0 // te + i, 0))
    in_specs = [pl.BlockSpec((te, D), lambda i: (x_tok0 // te + i, 0)),
                pl.BlockSpec((te, D), lambda i: (i, 0)), pl.BlockSpec((1, D), lambda i: (0, 0))]
    args = [x1, peer, norm_final.reshape(1, D)]
    aliases = {}
    if out_prev is not None:
        in_specs.append(pl.BlockSpec(memory_space=pl.ANY))
        args.append(out_prev)
        aliases = {3: 0}
    return pl.pallas_call(
        _final_kernel, grid=(nblk,), in_specs=in_specs, out_specs=spec,
        out_shape=jax.ShapeDtypeStruct((total, D), jnp.float32),
        input_output_aliases=aliases,
        compiler_params=pltpu.CompilerParams(dimension_semantics=("parallel",)),
        name="final_norm",
    )(*args)


def _tree_sum(vals):
    while len(vals) > 1:
        nxt = [vals[i] + vals[i + 1] for i in range(0, len(vals) - 1, 2)]
        if len(vals) % 2:
            nxt.append(vals[-1])
        vals = nxt
    return vals[0]


def _sc_block_pipeline(nblk, items_per_token, loads, store, gather, compute):
    ipt = items_per_token
    assert SC_ROW_BUFS > ipt and nblk >= 1

    for c in loads(0, 0):
        c.start()
    for c in loads(0, 0):
        c.wait()
    if nblk > 1:
        for c in loads(1, 1):
            c.start()
    for q in range(ipt):
        gather(0, 0, q, q).start()

    @pl.loop(0, nblk)
    def _(b):
        slot = b % 2

        @pl.when(b >= 2)
        def _():
            store(b - 2, slot).wait()

        @pl.loop(0, SC_TOKENS)
        def _(t):
            item0 = (b * SC_TOKENS + t) * ipt
            for q in range(ipt):
                nxt_buf = (item0 + q + ipt) % SC_ROW_BUFS

                @pl.when(t + 1 < SC_TOKENS)
                def _():
                    gather(slot, t + 1, q, nxt_buf).start()

                @pl.when(jnp.logical_and(t + 1 == SC_TOKENS, b + 1 < nblk))
                def _():
                    if q == 0:
                        for c in loads(b + 1, 1 - slot):
                            c.wait()
                    gather(1 - slot, 0, q, nxt_buf).start()

                buf = (item0 + q) % SC_ROW_BUFS
                gather(slot, t, q, buf).wait()
                compute(slot, t, q, buf)

        store(b, slot).start()

        @pl.when(b + 2 < nblk)
        def _():
            for c in loads(b + 2, slot):
                c.start()

    if nblk >= 2:
        store(nblk - 2, nblk % 2).wait()
    store(nblk - 1, (nblk - 1) % 2).wait()


def _sc_mesh():
    return plsc.VectorSubcoreMesh(core_axis_name="c", subcore_axis_name="s")


def _sc_worker_id():
    return lax.axis_index("s") * V7X_SC_CORES + lax.axis_index("c")


def _sc_bf16(words):
    return plsc.bitcast(words, jnp.bfloat16)


def _sc_halves_f32(pairs):
    words = plsc.bitcast(pairs, jnp.uint32)
    return (plsc.bitcast(words << 16, jnp.float32),
            plsc.bitcast(words & jnp.uint32(HI_HALF), jnp.float32))


def _mix_parts(T, K, DW):
    D = 2 * DW
    L, G = V7X_SC_LANES, SC_GATHER
    nj = SC_MIX_CHUNK // L
    tok_per_w = T // V7X_SC_WORKERS
    assert T % (V7X_SC_WORKERS * SC_TOKENS) == 0 and K % (2 * G) == 0
    assert DW % SC_MIX_CHUNK == 0 and G % SC_MIX_GROUP == 0
    scratch = [
        pltpu.VMEM((2, SC_TOKENS, K), jnp.int32),
        pltpu.VMEM((2, SC_TOKENS, K), jnp.int32),
        pltpu.VMEM((SC_ROW_BUFS, G, DW), jnp.uint32),
        pltpu.VMEM((2, SC_TOKENS, D), jnp.float32),
        pltpu.SemaphoreType.DMA((2,)),
        pltpu.SemaphoreType.DMA((2,)),
        pltpu.SemaphoreType.DMA((2,)),
        pltpu.SemaphoreType.DMA((SC_ROW_BUFS,)),
    ]

    def run(w_hbm, idx_hbm, tab_hbm, out_hbm, idx_v, w_v, rows_v, out_v, idx_sems, w_sems, out_sems, row_sems):
        base = _sc_worker_id() * tok_per_w

        def loads(b, slot):
            toks = pl.ds(base + b * SC_TOKENS, SC_TOKENS)
            return [pltpu.make_async_copy(idx_hbm.at[toks], idx_v.at[slot], idx_sems.at[slot]),
                    pltpu.make_async_copy(w_hbm.at[toks], w_v.at[slot], w_sems.at[slot])]

        def store(b, slot):
            toks = pl.ds(base + b * SC_TOKENS, SC_TOKENS)
            return pltpu.make_async_copy(out_v.at[slot], out_hbm.at[toks], out_sems.at[slot])

        def gather(slot, t, q, buf):
            return pltpu.make_async_copy(
                tab_hbm.at[idx_v.at[slot, t, pl.ds(q * G, G)]], rows_v.at[buf], row_sems.at[buf])

        def compute(slot, t, q, buf):
            ssplat = jnp.full((L,), slot, jnp.int32)
            tsplat = jnp.full((L,), t, jnp.int32)
            for c in range(DW // SC_MIX_CHUNK):
                def body(kg, acc):
                    kk = kg * SC_MIX_GROUP
                    wks = [_sc_bf16(plsc.load_gather(
                        w_v, [ssplat, tsplat, jnp.full((L,), q * G + i, jnp.int32) + kk]))
                        for i in range(SC_MIX_GROUP)]
                    out = []
                    for j in range(nj):
                        prods = [wks[i] * _sc_bf16(rows_v[buf, kk + i, pl.ds(c * SC_MIX_CHUNK + j * L, L)])
                                 for i in range(SC_MIX_GROUP)]
                        lo, hi = _sc_halves_f32(_tree_sum(prods))
                        out += [acc[2 * j] + lo, acc[2 * j + 1] + hi]
                    return tuple(out)

                zero = jnp.zeros((L,), jnp.float32)
                acc = plsc.parallel_loop(0, G // SC_MIX_GROUP, carry=(zero,) * (2 * nj))(body)
                for j in range(nj):
                    for half in range(2):
                        dst = out_v.at[slot, t, pl.ds(half * DW + c * SC_MIX_CHUNK + j * L, L)]
                        if q == 0:
                            dst[...] = acc[2 * j + half]
                        else:
                            plsc.addupdate(dst, acc[2 * j + half])

        _sc_block_pipeline(tok_per_w // SC_TOKENS, K // G, loads, store, gather, compute)

    return scratch, run


def _sc_kernel(out_type, scratch_types, name):
    return functools.partial(
        pl.kernel, mesh=_sc_mesh(), out_type=out_type, scratch_types=scratch_types,
        compiler_params=pltpu.CompilerParams(needs_layout_passes=False), name=name)


def _pack_table(a):
    n, c = a.shape
    half = c // 2
    L, rb = V7X_SC_LANES, SC_PACK_ROWS
    rows_per_w = n // V7X_SC_WORKERS
    nblk = rows_per_w // rb
    assert n % (V7X_SC_WORKERS * rb) == 0 and half % L == 0

    @_sc_kernel(jax.ShapeDtypeStruct((n, half), jnp.uint32),
                [pltpu.VMEM((2, rb, c), jnp.float32), pltpu.VMEM((2, rb, half), jnp.uint32),
                 pltpu.SemaphoreType.DMA((2,)), pltpu.SemaphoreType.DMA((2,))], "pack_table")
    def k(a_hbm, o_hbm, a_v, o_v, in_sems, out_sems):
        base = _sc_worker_id() * rows_per_w

        def load(b, slot):
            return pltpu.make_async_copy(a_hbm.at[pl.ds(base + b * rb, rb)], a_v.at[slot], in_sems.at[slot])

        def store(b, slot):
            return pltpu.make_async_copy(o_v.at[slot], o_hbm.at[pl.ds(base + b * rb, rb)], out_sems.at[slot])

        load(0, 0).start()

        @pl.loop(0, nblk)
        def _(b):
            slot = b % 2
            load(b, slot).wait()

            @pl.when(b + 1 < nblk)
            def _():
                load(b + 1, 1 - slot).start()

            @pl.when(b >= 2)
            def _():
                store(b - 2, slot).wait()

            @plsc.parallel_loop(0, rb)
            def _(r):
                for j in range(half // L):
                    pairs = plsc.pack(a_v[slot, r, pl.ds(j * L, L)], a_v[slot, r, pl.ds(half + j * L, L)],
                                      format=plsc.PackFormat.INTERLEAVED)
                    o_v[slot, r, pl.ds(j * L, L)] = plsc.bitcast(pairs, jnp.uint32)

            store(b, slot).start()

        if nblk >= 2:
            store(nblk - 2, nblk % 2).wait()
        store(nblk - 1, (nblk - 1) % 2).wait()

    return k(a)


def _expert_mix(w, idx, table):
    T, K = w.shape
    DW = table.shape[1]
    scratch, run = _mix_parts(T, K, DW)

    @_sc_kernel(jax.ShapeDtypeStruct((T, 2 * DW), jnp.float32), scratch, "expert_mix")
    def k(w_hbm, idx_hbm, tab_hbm, out_hbm, *mix_scratch):
        run(w_hbm, idx_hbm, tab_hbm, out_hbm, *mix_scratch)

    return k(w, idx, table)


def _chunk_sizes(total):
    ramp, size = [EDGE_CHUNK], EDGE_CHUNK
    while size < MAX_CHUNK:
        ramp.append(size)
        size *= 2
    middle = total - 2 * sum(ramp)
    assert middle >= 0 and middle % MAX_CHUNK == 0
    return ramp + [MAX_CHUNK] * (middle // MAX_CHUNK) + ramp[::-1]


def kernel(x, norm_mix, w_in, pool_w, pool_scale, sgu_ln_g, sgu_ln_b, sgu_w, sgu_b, out_norm_pool,
           out_norm_sgu, w_out, norm_ffn, peer_wq, peer_keys, peer_u, peer_v, norm_final):
    B, S, D = x.shape
    assert norm_mix.shape[0] == 1, "single-layer block"
    T = B * S
    mix_args = (norm_mix[0], w_in[0], pool_w[0], pool_scale[0], sgu_ln_g[0], sgu_ln_b[0],
                sgu_w[0], sgu_b[0], out_norm_pool[0], out_norm_sgu[0], w_out[0])
    x1_parts = [(b * S, _mixer(x, b, 1, *mix_args).reshape(S, D)) for b in range(B)]
    wq = peer_wq[0].astype(jnp.bfloat16)
    keys = peer_keys[0].astype(jnp.bfloat16)
    u_t = _transposed_bf16(peer_u[0])
    v_tab = _pack_table(peer_v[0])
    out = None
    tok0 = 0
    peers = [norm_ffn, v_tab]
    outs = []
    chunks = _chunk_sizes(T)
    for c, tc in enumerate(chunks):
        part0, x1 = [p for p in x1_parts if p[0] <= tok0][-1]
        assert tok0 + tc <= part0 + x1.shape[0], "a token chunk must lie inside one mixer call"
        after = outs[-2] if c >= max(len(chunks) - 2, 2) else peers[-2]
        h2, idx, gate = _router(x1, norm_ffn[0], wq, keys, tok0 - part0, tc, after)
        w = _expert_weights(h2, u_t, idx, gate)
        peer = _expert_mix(w, idx, v_tab)
        peers.append(peer)
        out = _final(x1, tok0 - part0, peer, norm_final, out, tok0, T)
        outs.append(out)
        tok0 += tc
    return out.reshape(B, S, D)
```

```python
import functools
import math

import jax
import jax.numpy as jnp
from jax import lax
from jax.experimental import pallas as pl
from jax.experimental.pallas import tpu as pltpu
from jax.experimental.pallas import tpu_sc as plsc

POOL_WINDOWS = (2, 4, 8, 16)
N_POOL_GROUPS = len(POOL_WINDOWS)
SGU_HEADS = 4
SGU_CHUNK = 128
PEER_HEADS = 8
PEER_N_KEYS = 128
PEER_D_HALF = 128
PEER_TOPK = 16
NORM_EPS = 1e-6
EXPERTS_PER_TOKEN = PEER_HEADS * PEER_TOPK

V7X_LANES = 128
V7X_SUBLANES = 8
V7X_SC_CORES = 2
V7X_SC_SUBCORES = 16
V7X_SC_LANES = 16
V7X_SC_WORKERS = V7X_SC_CORES * V7X_SC_SUBCORES

HALO = max(POOL_WINDOWS)
MIX_TILE = 512
ROUTE_TILE = 1024
EW_TILE = 512
SCORE_TOK_TILE = 2048
SCORE_EXP_TILE = 2048
SC_PACK_ROWS = 16
SC_GATHER = 64
SC_TOKENS = 8
SC_ROW_BUFS = 3
SC_MIX_CHUNK = 128
SC_MIX_GROUP = 4
HI_HALF = 0xFFFF0000
V7X_VMEM_BYTES = 64 * 1024 * 1024
TC_VMEM_LIMIT = V7X_VMEM_BYTES * 3 // 4
EDGE_CHUNK = 512
MAX_CHUNK = 2048


def _rms(x, g):
    inv = lax.rsqrt(jnp.mean(x * x, axis=-1, keepdims=True) + NORM_EPS)
    return x * inv * g


def _transpose_cast_kernel(a_ref, o_ref):
    o_ref[...] = a_ref[...].T.astype(o_ref.dtype)


def _transposed_bf16(a):
    n, c = a.shape
    rows = min(SCORE_EXP_TILE, n)
    return pl.pallas_call(
        _transpose_cast_kernel, grid=(n // rows,),
        in_specs=[pl.BlockSpec((rows, c), lambda i: (i, 0))],
        out_specs=pl.BlockSpec((c, rows), lambda i: (0, i)),
        out_shape=jax.ShapeDtypeStruct((c, n), jnp.bfloat16),
        compiler_params=pltpu.CompilerParams(dimension_semantics=("parallel",)),
        name="transpose_table",
    )(a)


def _gelu(x):
    return 0.5 * x * (1.0 + lax.erf(x * math.sqrt(0.5)))


def _mixer_kernel(x_ref, xh_ref, nmix_ref, win_ref, poolw_ref, pscale_ref, lng_ref, lnb_ref,
                  sguw_ref, sgub_ref, onp_ref, ons_ref, wout_ref, o_ref, pext_ref, mix_ref):
    i = pl.program_id(1)
    ts = x_ref.shape[1]
    pool_w = pscale_ref.shape[1]
    gdim = pool_w // N_POOL_GROUPS
    sgu_w = lng_ref.shape[1]
    hdim = sgu_w // SGU_HEADS

    x = x_ref[0]
    h = _rms(x, nmix_ref[...]).astype(jnp.bfloat16)
    z = jnp.dot(h, win_ref[...], preferred_element_type=jnp.float32)
    p = z[:, :pool_w]

    hh = _rms(xh_ref[0], nmix_ref[...]).astype(jnp.bfloat16)
    ph = jnp.dot(hh, win_ref[:, :pool_w], preferred_element_type=jnp.float32)
    ph = jnp.where(i > 0, ph, 0.0)
    pext_ref[0:HALO, :] = ph
    pext_ref[HALO:HALO + ts, :] = p

    pos = i * ts + lax.broadcasted_iota(jnp.int32, (ts, 1), 0)
    ssq = jnp.zeros((ts, 1), jnp.float32)
    a_parts = []
    for g, win in enumerate(POOL_WINDOWS):
        cols = slice(g * gdim, (g + 1) * gdim)
        s = pext_ref[HALO:HALO + ts, cols]
        for j in range(1, win):
            s = s + pext_ref[HALO - j:HALO - j + ts, cols]
        cnt = jnp.minimum(pos + 1, win).astype(jnp.float32)
        d = (s / cnt - p[:, cols]).astype(jnp.bfloat16)
        a = jnp.dot(d, poolw_ref[g], preferred_element_type=jnp.float32) * pscale_ref[:, cols]
        ssq = ssq + jnp.sum(a * a, axis=-1, keepdims=True)
        a_parts.append(a)
    inv_a = lax.rsqrt(ssq / pool_w + NORM_EPS)
    for g in range(N_POOL_GROUPS):
        cols = slice(g * gdim, (g + 1) * gdim)
        mix_ref[:, cols] = (a_parts[g] * inv_a * onp_ref[:, cols]).astype(jnp.bfloat16)

    gz = _gelu(z[:, pool_w:])
    tril = (lax.broadcasted_iota(jnp.int32, (SGU_CHUNK, SGU_CHUNK), 0)
            >= lax.broadcasted_iota(jnp.int32, (SGU_CHUNK, SGU_CHUNK), 1))
    ssq = jnp.zeros((ts, 1), jnp.float32)
    b_parts = []
    for hd in range(SGU_HEADS):
        cols = slice(hd * hdim, (hd + 1) * hdim)
        u = gz[:, hd * hdim:(hd + 1) * hdim]
        v = gz[:, sgu_w + hd * hdim:sgu_w + (hd + 1) * hdim]
        mu = jnp.mean(v, axis=-1, keepdims=True)
        vc = v - mu
        var = jnp.mean(vc * vc, axis=-1, keepdims=True)
        vn = (vc * lax.rsqrt(var + NORM_EPS) * lng_ref[:, cols] + lnb_ref[:, cols]).astype(jnp.bfloat16)
        w = jnp.where(tril, sguw_ref[hd], jnp.zeros((), sguw_ref.dtype))
        mixed = [jnp.dot(w, vn[n * SGU_CHUNK:(n + 1) * SGU_CHUNK], preferred_element_type=jnp.float32)
                 + sgub_ref[hd] for n in range(ts // SGU_CHUNK)]
        b = u * jnp.concatenate(mixed, axis=0)
        ssq = ssq + jnp.sum(b * b, axis=-1, keepdims=True)
        b_parts.append(b)
    inv_b = lax.rsqrt(ssq / sgu_w + NORM_EPS)
    for hd in range(SGU_HEADS):
        cols = slice(hd * hdim, (hd + 1) * hdim)
        mix_ref[:, pool_w + hd * hdim:pool_w + (hd + 1) * hdim] = (
            b_parts[hd] * inv_b * ons_ref[:, cols]).astype(jnp.bfloat16)

    o_ref[0] = x + jnp.dot(mix_ref[...], wout_ref[...], preferred_element_type=jnp.float32)


def _mixer(x, b0, nb, norm_mix, w_in, pool_w, pool_scale, ln_g, ln_b, sgu_w, sgu_b, on_pool, on_sgu, w_out):
    _, S, D = x.shape
    ts = min(MIX_TILE, S)
    pool_width = pool_scale.size
    sgu_width = ln_g.size
    in_width = w_in.shape[1]
    gdim = pool_width // N_POOL_GROUPS
    halo_blocks = ts // HALO
    full = lambda shape: pl.BlockSpec(shape, lambda b, i: (0,) * len(shape))
    return pl.pallas_call(
        _mixer_kernel,
        grid=(nb, S // ts),
        in_specs=[
            pl.BlockSpec((1, ts, D), lambda b, i: (b0 + b, i, 0)),
            pl.BlockSpec((1, HALO, D), lambda b, i: (b0 + b, jnp.maximum(i * halo_blocks - 1, 0), 0)),
            full((1, D)),
            full((D, in_width)),
            full((N_POOL_GROUPS, gdim, gdim)),
            full((1, pool_width)),
            full((1, sgu_width)),
            full((1, sgu_width)),
            full((SGU_HEADS, SGU_CHUNK, SGU_CHUNK)),
            full((SGU_HEADS, SGU_CHUNK, SGU_CHUNK)),
            full((1, pool_width)),
            full((1, sgu_width)),
            full((pool_width + sgu_width, D)),
        ],
        out_specs=pl.BlockSpec((1, ts, D), lambda b, i: (b, i, 0)),
        out_shape=jax.ShapeDtypeStruct((nb, S, D), jnp.float32),
        scratch_shapes=[
            pltpu.VMEM((HALO + ts, pool_width), jnp.float32),
            pltpu.VMEM((ts, pool_width + sgu_width), jnp.bfloat16),
        ],
        compiler_params=pltpu.CompilerParams(
            dimension_semantics=("parallel", "arbitrary"), vmem_limit_bytes=TC_VMEM_LIMIT),
        name="mixer",
    )(x, x, norm_mix.reshape(1, D), w_in.astype(jnp.bfloat16), pool_w.astype(jnp.bfloat16),
      pool_scale.reshape(1, pool_width), ln_g.reshape(1, sgu_width), ln_b.reshape(1, sgu_width),
      sgu_w.astype(jnp.bfloat16),
      jnp.broadcast_to(sgu_b[:, :, None], (SGU_HEADS, SGU_CHUNK, SGU_CHUNK)),
      on_pool.reshape(1, pool_width), on_sgu.reshape(1, sgu_width), w_out.astype(jnp.bfloat16))


def _topk_rows(s, k):
    sub = V7X_SUBLANES
    n, m = s.shape
    pieces = n // sub
    assert n % sub == 0 and pieces >= k, "lists shorter than k would need padding"
    row = lax.broadcasted_iota(jnp.int32, (sub, m), 0)
    v = [s[p * sub:(p + 1) * sub] for p in range(pieces)]
    ix = [row + p * sub for p in range(pieces)]
    for rnd in range(pieces):
        for p in range(rnd % 2, pieces - 1, 2):
            swap = v[p + 1] > v[p]
            v[p], v[p + 1] = jnp.where(swap, v[p + 1], v[p]), jnp.where(swap, v[p], v[p + 1])
            ix[p], ix[p + 1] = jnp.where(swap, ix[p + 1], ix[p]), jnp.where(swap, ix[p], ix[p + 1])
    vals, idxs = [], []
    for r in range(k):
        best = jnp.max(v[0], axis=0, keepdims=True)
        best_ix = jnp.min(jnp.where(v[0] == best, ix[0], n), axis=0, keepdims=True)
        vals.append(best)
        idxs.append(best_ix)
        taken = ix[0] == best_ix
        for p in range(k - 1 - r):
            v[p] = jnp.where(taken, v[p + 1], v[p])
            ix[p] = jnp.where(taken, ix[p + 1], ix[p])
    return vals, idxs


def _pair_candidates(v1, i1, v2, i2):
    k, sub = PEER_TOPK, V7X_SUBLANES
    assert k == 16 and sub == 8, "the tile plan below is written for 16 candidates per side"
    v1c, i1c = jnp.concatenate(v1, axis=0), jnp.concatenate(i1, axis=0)
    v2c, i2c = jnp.concatenate(v2, axis=0), jnp.concatenate(i2, axis=0)
    m = v2c.shape[1]
    r = lax.broadcasted_iota(jnp.int32, (sub, m), 0)

    def with_b(a, b0, row0):
        shift = lambda x: x if row0 == 0 else pltpu.roll(x, row0, 0)
        return (v1[a] + shift(v2c[b0:b0 + sub]), i1[a] * PEER_N_KEYS + shift(i2c[b0:b0 + sub]),
                a * k + b0 + (r - row0))

    def with_a(a0, row0):
        s = (row0 - (a0 - sub)) % sub
        shift = lambda x: x if s == 0 else pltpu.roll(x, s, 0)
        return (shift(v1c[sub:]) + v2[0], shift(i1c[sub:]) * PEER_N_KEYS + i2[0], (a0 + (r - row0)) * k)

    def tile(*parts):
        out = parts[0][1]
        for row0, seg in parts[1:]:
            out = tuple(jnp.where(r < row0, o, s) for o, s in zip(out, seg))
        return out

    tiles = [
        with_b(0, 0, 0), with_b(0, sub, 0),
        with_b(1, 0, 0),
        tile((0, with_b(2, 0, 0)), (5, with_b(4, 0, 5))),
        tile((0, with_b(3, 0, 0)), (4, with_b(5, 0, 4)), (6, with_b(6, 0, 6))),
        tile((0, with_b(7, 0, 0)), (2, with_a(8, 2))),
        tile((0, with_a(14, 0)), (2, with_b(1, sub, 2))),
    ]
    return tuple(jnp.concatenate([t[n] for t in tiles], axis=0) for n in range(3))


def _router_kernel(x_ref, nffn_ref, wq_ref, keys_ref, after_ref, h2_ref, idx_ref, gate_ref,
                   q_ref, idxt_ref, gatet_ref):
    del after_ref
    h2 = _rms(x_ref[...], nffn_ref[...]).astype(jnp.bfloat16)
    h2_ref[...] = h2
    q_ref[...] = jnp.dot(h2, wq_ref[...], preferred_element_type=jnp.float32).astype(jnp.bfloat16)
    dq = 2 * PEER_D_HALF
    nt = (((1,), (1,)), ((), ()))

    def head(hd, carry):
        off = pl.multiple_of(hd * dq, dq)
        s1 = lax.dot_general(keys_ref[0], q_ref[:, pl.ds(off, PEER_D_HALF)], nt,
                             preferred_element_type=jnp.float32)
        s2 = lax.dot_general(keys_ref[1], q_ref[:, pl.ds(off + PEER_D_HALF, PEER_D_HALF)], nt,
                             preferred_element_type=jnp.float32)
        v1, i1 = _topk_rows(s1, PEER_TOPK)
        v2, i2 = _topk_rows(s2, PEER_TOPK)
        cand, expert, flat = _pair_candidates(v1, i1, v2, i2)
        cv, ce = [], []
        for _ in range(PEER_TOPK):
            m = jnp.max(cand, axis=0, keepdims=True)
            ix = jnp.min(jnp.where(cand == m, flat, PEER_TOPK * PEER_TOPK), axis=0, keepdims=True)
            hit = flat == ix
            cv.append(m)
            ce.append(jnp.max(jnp.where(hit, expert, -1), axis=0, keepdims=True))
            cand = jnp.where(hit, -jnp.inf, cand)
        cvc = jnp.concatenate(cv, axis=0)
        e = jnp.exp(cvc - cv[0])
        gate = e / jnp.sum(e, axis=0, keepdims=True)
        row = pl.multiple_of(hd * PEER_TOPK, PEER_TOPK)
        idxt_ref[pl.ds(row, PEER_TOPK), :] = jnp.concatenate(ce, axis=0)
        gatet_ref[pl.ds(row, PEER_TOPK), :] = gate
        return carry

    lax.fori_loop(0, PEER_HEADS, head, 0)
    idx_ref[...] = idxt_ref[...].T
    gate_ref[...] = gatet_ref[...].T


def _router(x1, norm_ffn, wq, keys, tok0, T, after):
    D = x1.shape[1]
    tr = min(ROUTE_TILE, T)
    qw = wq.shape[1]
    assert tok0 % tr == 0 and T % tr == 0
    first = tok0 // tr
    full = lambda shape: pl.BlockSpec(shape, lambda i: (0,) * len(shape))
    return pl.pallas_call(
        _router_kernel,
        grid=(T // tr,),
        in_specs=[
            pl.BlockSpec((tr, D), lambda i: (first + i, 0)),
            full((1, D)),
            full((D, qw)),
            full((2, PEER_N_KEYS, PEER_D_HALF)),
            pl.BlockSpec(memory_space=pl.ANY),
        ],
        out_specs=[
            pl.BlockSpec((tr, D), lambda i: (i, 0)),
            pl.BlockSpec((tr, EXPERTS_PER_TOKEN), lambda i: (i, 0)),
            pl.BlockSpec((tr, EXPERTS_PER_TOKEN), lambda i: (i, 0)),
        ],
        out_shape=[
            jax.ShapeDtypeStruct((T, D), jnp.bfloat16),
            jax.ShapeDtypeStruct((T, EXPERTS_PER_TOKEN), jnp.int32),
            jax.ShapeDtypeStruct((T, EXPERTS_PER_TOKEN), jnp.float32),
        ],
        scratch_shapes=[
            pltpu.VMEM((tr, qw), jnp.bfloat16),
            pltpu.VMEM((EXPERTS_PER_TOKEN, tr), jnp.int32),
            pltpu.VMEM((EXPERTS_PER_TOKEN, tr), jnp.float32),
        ],
        compiler_params=pltpu.CompilerParams(
            dimension_semantics=("parallel",), vmem_limit_bytes=TC_VMEM_LIMIT),
        name="router",
    )(x1, norm_ffn.reshape(1, D), wq, keys, after)


def _scores_kernel(h_ref, ut_ref, idx_ref, gate_ref, w_ref, pre_ref):
    j = pl.program_id(1)
    acc = jnp.dot(h_ref[...], ut_ref[...], preferred_element_type=jnp.float32)
    groups = acc.shape[1] // V7X_LANES
    lane_bits = int(math.log2(V7X_LANES))
    idx = idx_ref[...]
    group, lane = idx >> lane_bits, idx & (V7X_LANES - 1)
    picked = jnp.zeros(idx.shape, jnp.float32)
    for n in range(groups):
        vals = jnp.take_along_axis(acc[:, n * V7X_LANES:(n + 1) * V7X_LANES], lane, axis=1)
        picked = jnp.where(group == j * groups + n, vals, picked)

    @pl.when(j == 0)
    def _():
        pre_ref[...] = picked

    @pl.when(j > 0)
    def _():
        pre_ref[...] += picked

    @pl.when(j == pl.num_programs(1) - 1)
    def _():
        w = (gate_ref[...] * _gelu(pre_ref[...])).astype(jnp.bfloat16).astype(jnp.float32)
        hi = pltpu.bitcast(w, jnp.uint32) & jnp.uint32(HI_HALF)
        w_ref[...] = pltpu.bitcast(hi | (hi >> 16), jnp.int32)


def _expert_weights(h2, u_t, idx, gate):
    T, D = h2.shape
    E = u_t.shape[1]
    K = idx.shape[1]
    tm, tn = min(SCORE_TOK_TILE, T), min(SCORE_EXP_TILE, E)
    assert T % tm == 0 and E % tn == 0 and tn % V7X_LANES == 0 and K == V7X_LANES
    per_token = pl.BlockSpec((tm, K), lambda i, j: (i, 0))
    return pl.pallas_call(
        _scores_kernel,
        grid=(T // tm, E // tn),
        in_specs=[pl.BlockSpec((tm, D), lambda i, j: (i, 0)), pl.BlockSpec((D, tn), lambda i, j: (0, j)),
                  per_token, per_token],
        out_specs=per_token,
        out_shape=jax.ShapeDtypeStruct((T, K), jnp.int32),
        scratch_shapes=[pltpu.VMEM((tm, K), jnp.float32)],
        compiler_params=pltpu.CompilerParams(
            dimension_semantics=("parallel", "arbitrary"), vmem_limit_bytes=TC_VMEM_LIMIT),
        name="expert_weights",
    )(h2, u_t, idx, gate)


def _final_kernel(x_ref, y_ref, g_ref, *rest):
    o_ref = rest[-1]
    o_ref[...] = _rms(x_ref[...] + y_ref[...], g_ref[...])


def _final(x1, x_tok0, peer, norm_final, out_prev, out_tok0, total):
    Tc, D = peer.shape
    te = min(EW_TILE, Tc)
    assert x_tok0 % te == 0 and out_tok0 % te == 0 and Tc % te == 0
    nblk = Tc // te
    spec = pl.BlockSpec((te, D), lambda i: (out_tok0 // te + i, 0))
    in_specs = [pl.BlockSpec((te, D), lambda i: (x_tok0 // te + i, 0)),
                pl.BlockSpec((te, D), lambda i: (i, 0)), pl.BlockSpec((1, D), lambda i: (0, 0))]
    args = [x1, peer, norm_final.reshape(1, D)]
    aliases = {}
    if out_prev is not None:
        in_specs.append(pl.BlockSpec(memory_space=pl.ANY))
        args.append(out_prev)
        aliases = {3: 0}
    return pl.pallas_call(
        _final_kernel, grid=(nblk,), in_specs=in_specs, out_specs=spec,
        out_shape=jax.ShapeDtypeStruct((total, D), jnp.float32),
        input_output_aliases=aliases,
        compiler_params=pltpu.CompilerParams(dimension_semantics=("parallel",)),
        name="final_norm",
    )(*args)


def _tree_sum(vals):
    while len(vals) > 1:
        nxt = [vals[i] + vals[i + 1] for i in range(0, len(vals) - 1, 2)]
        if len(vals) % 2:
            nxt.append(vals[-1])
        vals = nxt
    return vals[0]


def _sc_block_pipeline(nblk, items_per_token, loads, store, gather, compute):
    ipt = items_per_token
    assert SC_ROW_BUFS > ipt and nblk >= 1

    for c in loads(0, 0):
        c.start()
    for c in loads(0, 0):
        c.wait()
    if nblk > 1:
        for c in loads(1, 1):
            c.start()
    for q in range(ipt):
        gather(0, 0, q, q).start()

    @pl.loop(0, nblk)
    def _(b):
        slot = b % 2

        @pl.when(b >= 2)
        def _():
            store(b - 2, slot).wait()

        @pl.loop(0, SC_TOKENS)
        def _(t):
            item0 = (b * SC_TOKENS + t) * ipt
            for q in range(ipt):
                nxt_buf = (item0 + q + ipt) % SC_ROW_BUFS

                @pl.when(t + 1 < SC_TOKENS)
                def _():
                    gather(slot, t + 1, q, nxt_buf).start()

                @pl.when(jnp.logical_and(t + 1 == SC_TOKENS, b + 1 < nblk))
                def _():
                    if q == 0:
                        for c in loads(b + 1, 1 - slot):
                            c.wait()
                    gather(1 - slot, 0, q, nxt_buf).start()

                buf = (item0 + q) % SC_ROW_BUFS
                gather(slot, t, q, buf).wait()
                compute(slot, t, q, buf)

        store(b, slot).start()

        @pl.when(b + 2 < nblk)
        def _():
            for c in loads(b + 2, slot):
                c.start()

    if nblk >= 2:
        store(nblk - 2, nblk % 2).wait()
    store(nblk - 1, (nblk - 1) % 2).wait()


def _sc_mesh():
    return plsc.VectorSubcoreMesh(core_axis_name="c", subcore_axis_name="s")


def _sc_worker_id():
    return lax.axis_index("s") * V7X_SC_CORES + lax.axis_index("c")


def _sc_bf16(words):
    return plsc.bitcast(words, jnp.bfloat16)


def _sc_halves_f32(pairs):
    words = plsc.bitcast(pairs, jnp.uint32)
    return (plsc.bitcast(words << 16, jnp.float32),
            plsc.bitcast(words & jnp.uint32(HI_HALF), jnp.float32))


def _mix_parts(T, K, DW):
    D = 2 * DW
    L, G = V7X_SC_LANES, SC_GATHER
    nj = SC_MIX_CHUNK // L
    tok_per_w = T // V7X_SC_WORKERS
    assert T % (V7X_SC_WORKERS * SC_TOKENS) == 0 and K % (2 * G) == 0
    assert DW % SC_MIX_CHUNK == 0 and G % SC_MIX_GROUP == 0
    scratch = [
        pltpu.VMEM((2, SC_TOKENS, K), jnp.int32),
        pltpu.VMEM((2, SC_TOKENS, K), jnp.int32),
        pltpu.VMEM((SC_ROW_BUFS, G, DW), jnp.uint32),
        pltpu.VMEM((2, SC_TOKENS, D), jnp.float32),
        pltpu.SemaphoreType.DMA((2,)),
        pltpu.SemaphoreType.DMA((2,)),
        pltpu.SemaphoreType.DMA((2,)),
        pltpu.SemaphoreType.DMA((SC_ROW_BUFS,)),
    ]

    def run(w_hbm, idx_hbm, tab_hbm, out_hbm, idx_v, w_v, rows_v, out_v, idx_sems, w_sems, out_sems, row_sems):
        base = _sc_worker_id() * tok_per_w

        def loads(b, slot):
            toks = pl.ds(base + b * SC_TOKENS, SC_TOKENS)
            return [pltpu.make_async_copy(idx_hbm.at[toks], idx_v.at[slot], idx_sems.at[slot]),
                    pltpu.make_async_copy(w_hbm.at[toks], w_v.at[slot], w_sems.at[slot])]

        def store(b, slot):
            toks = pl.ds(base + b * SC_TOKENS, SC_TOKENS)
            return pltpu.make_async_copy(out_v.at[slot], out_hbm.at[toks], out_sems.at[slot])

        def gather(slot, t, q, buf):
            return pltpu.make_async_copy(
                tab_hbm.at[idx_v.at[slot, t, pl.ds(q * G, G)]], rows_v.at[buf], row_sems.at[buf])

        def compute(slot, t, q, buf):
            ssplat = jnp.full((L,), slot, jnp.int32)
            tsplat = jnp.full((L,), t, jnp.int32)
            for c in range(DW // SC_MIX_CHUNK):
                def body(kg, acc):
                    kk = kg * SC_MIX_GROUP
                    wks = [_sc_bf16(plsc.load_gather(
                        w_v, [ssplat, tsplat, jnp.full((L,), q * G + i, jnp.int32) + kk]))
                        for i in range(SC_MIX_GROUP)]
                    out = []
                    for j in range(nj):
                        prods = [wks[i] * _sc_bf16(rows_v[buf, kk + i, pl.ds(c * SC_MIX_CHUNK + j * L, L)])
                                 for i in range(SC_MIX_GROUP)]
                        lo, hi = _sc_halves_f32(_tree_sum(prods))
                        out += [acc[2 * j] + lo, acc[2 * j + 1] + hi]
                    return tuple(out)

                zero = jnp.zeros((L,), jnp.float32)
                acc = plsc.parallel_loop(0, G // SC_MIX_GROUP, carry=(zero,) * (2 * nj))(body)
                for j in range(nj):
                    for half in range(2):
                        dst = out_v.at[slot, t, pl.ds(half * DW + c * SC_MIX_CHUNK + j * L, L)]
                        if q == 0:
                            dst[...] = acc[2 * j + half]
                        else:
                            plsc.addupdate(dst, acc[2 * j + half])

        _sc_block_pipeline(tok_per_w // SC_TOKENS, K // G, loads, store, gather, compute)

    return scratch, run


def _sc_kernel(out_type, scratch_types, name):
    return functools.partial(
        pl.kernel, mesh=_sc_mesh(), out_type=out_type, scratch_types=scratch_types,
        compiler_params=pltpu.CompilerParams(needs_layout_passes=False), name=name)


def _pack_table(a):
    n, c = a.shape
    half = c // 2
    L, rb = V7X_SC_LANES, SC_PACK_ROWS
    rows_per_w = n // V7X_SC_WORKERS
    nblk = rows_per_w // rb
    assert n % (V7X_SC_WORKERS * rb) == 0 and half % L == 0

    @_sc_kernel(jax.ShapeDtypeStruct((n, half), jnp.uint32),
                [pltpu.VMEM((2, rb, c), jnp.float32), pltpu.VMEM((2, rb, half), jnp.uint32),
                 pltpu.SemaphoreType.DMA((2,)), pltpu.SemaphoreType.DMA((2,))], "pack_table")
    def k(a_hbm, o_hbm, a_v, o_v, in_sems, out_sems):
        base = _sc_worker_id() * rows_per_w

        def load(b, slot):
            return pltpu.make_async_copy(a_hbm.at[pl.ds(base + b * rb, rb)], a_v.at[slot], in_sems.at[slot])

        def store(b, slot):
            return pltpu.make_async_copy(o_v.at[slot], o_hbm.at[pl.ds(base + b * rb, rb)], out_sems.at[slot])

        load(0, 0).start()

        @pl.loop(0, nblk)
        def _(b):
            slot = b % 2
            load(b, slot).wait()

            @pl.when(b + 1 < nblk)
            def _():
                load(b + 1, 1 - slot).start()

            @pl.when(b >= 2)
            def _():
                store(b - 2, slot).wait()

            @plsc.parallel_loop(0, rb)
            def _(r):
                for j in range(half // L):
                    pairs = plsc.pack(a_v[slot, r, pl.ds(j * L, L)], a_v[slot, r, pl.ds(half + j * L, L)],
                                      format=plsc.PackFormat.INTERLEAVED)
                    o_v[slot, r, pl.ds(j * L, L)] = plsc.bitcast(pairs, jnp.uint32)

            store(b, slot).start()

        if nblk >= 2:
            store(nblk - 2, nblk % 2).wait()
        store(nblk - 1, (nblk - 1) % 2).wait()

    return k(a)


def _expert_mix(w, idx, table):
    T, K = w.shape
    DW = table.shape[1]
    scratch, run = _mix_parts(T, K, DW)

    @_sc_kernel(jax.ShapeDtypeStruct((T, 2 * DW), jnp.float32), scratch, "expert_mix")
    def k(w_hbm, idx_hbm, tab_hbm, out_hbm, *mix_scratch):
        run(w_hbm, idx_hbm, tab_hbm, out_hbm, *mix_scratch)

    return k(w, idx, table)


def _chunk_sizes(total):
    ramp, size = [EDGE_CHUNK], EDGE_CHUNK
    while size < MAX_CHUNK:
        ramp.append(size)
        size *= 2
    drain = ramp[2:][::-1] + [2 * EDGE_CHUNK] if len(ramp) > 1 else ramp
    middle = total - sum(ramp) - sum(drain)
    assert middle >= 0 and middle % MAX_CHUNK == 0
    return ramp + [MAX_CHUNK] * (middle // MAX_CHUNK) + drain


def kernel(x, norm_mix, w_in, pool_w, pool_scale, sgu_ln_g, sgu_ln_b, sgu_w, sgu_b, out_norm_pool,
           out_norm_sgu, w_out, norm_ffn, peer_wq, peer_keys, peer_u, peer_v, norm_final):
    B, S, D = x.shape
    assert norm_mix.shape[0] == 1, "single-layer block"
    T = B * S
    mix_args = (norm_mix[0], w_in[0], pool_w[0], pool_scale[0], sgu_ln_g[0], sgu_ln_b[0],
                sgu_w[0], sgu_b[0], out_norm_pool[0], out_norm_sgu[0], w_out[0])
    x1_parts = [(b * S, _mixer(x, b, 1, *mix_args).reshape(S, D)) for b in range(B)]
    wq = peer_wq[0].astype(jnp.bfloat16)
    keys = peer_keys[0].astype(jnp.bfloat16)
    u_t = _transposed_bf16(peer_u[0])
    v_tab = _pack_table(peer_v[0])
    out = None
    tok0 = 0
    peers = [norm_ffn, v_tab]
    outs = []
    chunks = _chunk_sizes(T)
    for c, tc in enumerate(chunks):
        part0, x1 = [p for p in x1_parts if p[0] <= tok0][-1]
        assert tok0 + tc <= part0 + x1.shape[0], "a token chunk must lie inside one mixer call"
        after = outs[-2] if c >= max(len(chunks) - 2, 2) else peers[-2]
        h2, idx, gate = _router(x1, norm_ffn[0], wq, keys, tok0 - part0, tc, after)
        w = _expert_weights(h2, u_t, idx, gate)
        peer = _expert_mix(w, idx, v_tab)
        peers.append(peer)
        out = _final(x1, tok0 - part0, peer, norm_final, out, tok0, T)
        outs.append(out)
        tok0 += tc
    return out.reshape(B, S, D)
```

```python
import functools
import math

import jax
import jax.numpy as jnp
from jax import lax
from jax.experimental import pallas as pl
from jax.experimental.pallas import tpu as pltpu
from jax.experimental.pallas import tpu_sc as plsc

POOL_WINDOWS = (2, 4, 8, 16)
N_POOL_GROUPS = len(POOL_WINDOWS)
SGU_HEADS = 4
SGU_CHUNK = 128
PEER_HEADS = 8
PEER_N_KEYS = 128
PEER_D_HALF = 128
PEER_TOPK = 16
NORM_EPS = 1e-6
EXPERTS_PER_TOKEN = PEER_HEADS * PEER_TOPK

V7X_LANES = 128
V7X_SUBLANES = 8
V7X_SC_CORES = 2
V7X_SC_SUBCORES = 16
V7X_SC_LANES = 16
V7X_SC_WORKERS = V7X_SC_CORES * V7X_SC_SUBCORES

HALO = max(POOL_WINDOWS)
MIX_TILE = 512
ROUTE_TILE = 1024
EW_TILE = 512
SCORE_TOK_TILE = 2048
SCORE_EXP_TILE = 2048
SC_PACK_ROWS = 16
SC_GATHER = 64
SC_TOKENS = 8
SC_ROW_BUFS = 3
SC_MIX_CHUNK = 128
SC_MIX_GROUP = 4
HI_HALF = 0xFFFF0000
V7X_VMEM_BYTES = 64 * 1024 * 1024
TC_VMEM_LIMIT = V7X_VMEM_BYTES * 3 // 4
EDGE_CHUNK = 512
MAX_CHUNK = 2048


def _rms(x, g):
    inv = lax.rsqrt(jnp.mean(x * x, axis=-1, keepdims=True) + NORM_EPS)
    return x * inv * g


def _cast_kernel(a_ref, o_ref):
    o_ref[...] = a_ref[...].astype(o_ref.dtype)


def _table_bf16(a):
    n, c = a.shape
    rows = min(SCORE_EXP_TILE, n)
    return pl.pallas_call(
        _cast_kernel, grid=(n // rows,),
        in_specs=[pl.BlockSpec((rows, c), lambda i: (i, 0))],
        out_specs=pl.BlockSpec((rows, c), lambda i: (i, 0)),
        out_shape=jax.ShapeDtypeStruct((n, c), jnp.bfloat16),
        compiler_params=pltpu.CompilerParams(dimension_semantics=("parallel",)),
        name="cast_table",
    )(a)


def _gelu(x):
    return 0.5 * x * (1.0 + lax.erf(x * math.sqrt(0.5)))


def _mixer_kernel(x_ref, xh_ref, nmix_ref, win_ref, poolw_ref, pscale_ref, lng_ref, lnb_ref,
                  sguw_ref, sgub_ref, onp_ref, ons_ref, wout_ref, o_ref, pext_ref, mix_ref):
    i = pl.program_id(1)
    ts = x_ref.shape[1]
    pool_w = pscale_ref.shape[1]
    gdim = pool_w // N_POOL_GROUPS
    sgu_w = lng_ref.shape[1]
    hdim = sgu_w // SGU_HEADS

    x = x_ref[0]
    h = _rms(x, nmix_ref[...]).astype(jnp.bfloat16)
    z = jnp.dot(h, win_ref[...], preferred_element_type=jnp.float32)
    p = z[:, :pool_w]

    hh = _rms(xh_ref[0], nmix_ref[...]).astype(jnp.bfloat16)
    ph = jnp.dot(hh, win_ref[:, :pool_w], preferred_element_type=jnp.float32)
    ph = jnp.where(i > 0, ph, 0.0)
    pext_ref[0:HALO, :] = ph
    pext_ref[HALO:HALO + ts, :] = p

    pos = i * ts + lax.broadcasted_iota(jnp.int32, (ts, 1), 0)
    ssq = jnp.zeros((ts, 1), jnp.float32)
    a_parts = []
    for g, win in enumerate(POOL_WINDOWS):
        cols = slice(g * gdim, (g + 1) * gdim)
        s = pext_ref[HALO:HALO + ts, cols]
        for j in range(1, win):
            s = s + pext_ref[HALO - j:HALO - j + ts, cols]
        cnt = jnp.minimum(pos + 1, win).astype(jnp.float32)
        d = (s / cnt - p[:, cols]).astype(jnp.bfloat16)
        a = jnp.dot(d, poolw_ref[g], preferred_element_type=jnp.float32) * pscale_ref[:, cols]
        ssq = ssq + jnp.sum(a * a, axis=-1, keepdims=True)
        a_parts.append(a)
    inv_a = lax.rsqrt(ssq / pool_w + NORM_EPS)
    for g in range(N_POOL_GROUPS):
        cols = slice(g * gdim, (g + 1) * gdim)
        mix_ref[:, cols] = (a_parts[g] * inv_a * onp_ref[:, cols]).astype(jnp.bfloat16)

    gz = _gelu(z[:, pool_w:])
    tril = (lax.broadcasted_iota(jnp.int32, (SGU_CHUNK, SGU_CHUNK), 0)
            >= lax.broadcasted_iota(jnp.int32, (SGU_CHUNK, SGU_CHUNK), 1))
    ssq = jnp.zeros((ts, 1), jnp.float32)
    b_parts = []
    for hd in range(SGU_HEADS):
        cols = slice(hd * hdim, (hd + 1) * hdim)
        u = gz[:, hd * hdim:(hd + 1) * hdim]
        v = gz[:, sgu_w + hd * hdim:sgu_w + (hd + 1) * hdim]
        mu = jnp.mean(v, axis=-1, keepdims=True)
        vc = v - mu
        var = jnp.mean(vc * vc, axis=-1, keepdims=True)
        vn = (vc * lax.rsqrt(var + NORM_EPS) * lng_ref[:, cols] + lnb_ref[:, cols]).astype(jnp.bfloat16)
        w = jnp.where(tril, sguw_ref[hd], jnp.zeros((), sguw_ref.dtype))
        mixed = [jnp.dot(w, vn[n * SGU_CHUNK:(n + 1) * SGU_CHUNK], preferred_element_type=jnp.float32)
                 + sgub_ref[hd] for n in range(ts // SGU_CHUNK)]
        b = u * jnp.concatenate(mixed, axis=0)
        ssq = ssq + jnp.sum(b * b, axis=-1, keepdims=True)
        b_parts.append(b)
    inv_b = lax.rsqrt(ssq / sgu_w + NORM_EPS)
    for hd in range(SGU_HEADS):
        cols = slice(hd * hdim, (hd + 1) * hdim)
        mix_ref[:, pool_w + hd * hdim:pool_w + (hd + 1) * hdim] = (
            b_parts[hd] * inv_b * ons_ref[:, cols]).astype(jnp.bfloat16)

    o_ref[0] = x + jnp.dot(mix_ref[...], wout_ref[...], preferred_element_type=jnp.float32)


def _mixer(x, b0, nb, norm_mix, w_in, pool_w, pool_scale, ln_g, ln_b, sgu_w, sgu_b, on_pool, on_sgu, w_out):
    _, S, D = x.shape
    ts = min(MIX_TILE, S)
    pool_width = pool_scale.size
    sgu_width = ln_g.size
    in_width = w_in.shape[1]
    gdim = pool_width // N_POOL_GROUPS
    halo_blocks = ts // HALO
    full = lambda shape: pl.BlockSpec(shape, lambda b, i: (0,) * len(shape))
    return pl.pallas_call(
        _mixer_kernel,
        grid=(nb, S // ts),
        in_specs=[
            pl.BlockSpec((1, ts, D), lambda b, i: (b0 + b, i, 0)),
            pl.BlockSpec((1, HALO, D), lambda b, i: (b0 + b, jnp.maximum(i * halo_blocks - 1, 0), 0)),
            full((1, D)),
            full((D, in_width)),
            full((N_POOL_GROUPS, gdim, gdim)),
            full((1, pool_width)),
            full((1, sgu_width)),
            full((1, sgu_width)),
            full((SGU_HEADS, SGU_CHUNK, SGU_CHUNK)),
            full((SGU_HEADS, SGU_CHUNK, SGU_CHUNK)),
            full((1, pool_width)),
            full((1, sgu_width)),
            full((pool_width + sgu_width, D)),
        ],
        out_specs=pl.BlockSpec((1, ts, D), lambda b, i: (b, i, 0)),
        out_shape=jax.ShapeDtypeStruct((nb, S, D), jnp.float32),
        scratch_shapes=[
            pltpu.VMEM((HALO + ts, pool_width), jnp.float32),
            pltpu.VMEM((ts, pool_width + sgu_width), jnp.bfloat16),
        ],
        compiler_params=pltpu.CompilerParams(
            dimension_semantics=("parallel", "arbitrary"), vmem_limit_bytes=TC_VMEM_LIMIT),
        name="mixer",
    )(x, x, norm_mix.reshape(1, D), w_in.astype(jnp.bfloat16), pool_w.astype(jnp.bfloat16),
      pool_scale.reshape(1, pool_width), ln_g.reshape(1, sgu_width), ln_b.reshape(1, sgu_width),
      sgu_w.astype(jnp.bfloat16),
      jnp.broadcast_to(sgu_b[:, :, None], (SGU_HEADS, SGU_CHUNK, SGU_CHUNK)),
      on_pool.reshape(1, pool_width), on_sgu.reshape(1, sgu_width), w_out.astype(jnp.bfloat16))


def _topk_rows(s, k):
    sub = V7X_SUBLANES
    n, m = s.shape
    pieces = n // sub
    assert n % sub == 0 and pieces >= k, "lists shorter than k would need padding"
    row = lax.broadcasted_iota(jnp.int32, (sub, m), 0)
    v = [s[p * sub:(p + 1) * sub] for p in range(pieces)]
    ix = [row + p * sub for p in range(pieces)]
    for rnd in range(pieces):
        for p in range(rnd % 2, pieces - 1, 2):
            swap = v[p + 1] > v[p]
            v[p], v[p + 1] = jnp.where(swap, v[p + 1], v[p]), jnp.where(swap, v[p], v[p + 1])
            ix[p], ix[p + 1] = jnp.where(swap, ix[p + 1], ix[p]), jnp.where(swap, ix[p], ix[p + 1])
    vals, idxs = [], []
    for r in range(k):
        best = jnp.max(v[0], axis=0, keepdims=True)
        best_ix = jnp.min(jnp.where(v[0] == best, ix[0], n), axis=0, keepdims=True)
        vals.append(best)
        idxs.append(best_ix)
        taken = ix[0] == best_ix
        for p in range(k - 1 - r):
            v[p] = jnp.where(taken, v[p + 1], v[p])
            ix[p] = jnp.where(taken, ix[p + 1], ix[p])
    return vals, idxs


def _pair_candidates(v1, i1, v2, i2):
    k, sub = PEER_TOPK, V7X_SUBLANES
    assert k == 16 and sub == 8, "the tile plan below is written for 16 candidates per side"
    v1c, i1c = jnp.concatenate(v1, axis=0), jnp.concatenate(i1, axis=0)
    v2c, i2c = jnp.concatenate(v2, axis=0), jnp.concatenate(i2, axis=0)
    m = v2c.shape[1]
    r = lax.broadcasted_iota(jnp.int32, (sub, m), 0)

    def with_b(a, b0, row0):
        shift = lambda x: x if row0 == 0 else pltpu.roll(x, row0, 0)
        return (v1[a] + shift(v2c[b0:b0 + sub]), i1[a] * PEER_N_KEYS + shift(i2c[b0:b0 + sub]),
                a * k + b0 + (r - row0))

    def with_a(a0, row0):
        s = (row0 - (a0 - sub)) % sub
        shift = lambda x: x if s == 0 else pltpu.roll(x, s, 0)
        return (shift(v1c[sub:]) + v2[0], shift(i1c[sub:]) * PEER_N_KEYS + i2[0], (a0 + (r - row0)) * k)

    def tile(*parts):
        out = parts[0][1]
        for row0, seg in parts[1:]:
            out = tuple(jnp.where(r < row0, o, s) for o, s in zip(out, seg))
        return out

    tiles = [
        with_b(0, 0, 0), with_b(0, sub, 0),
        with_b(1, 0, 0),
        tile((0, with_b(2, 0, 0)), (5, with_b(4, 0, 5))),
        tile((0, with_b(3, 0, 0)), (4, with_b(5, 0, 4)), (6, with_b(6, 0, 6))),
        tile((0, with_b(7, 0, 0)), (2, with_a(8, 2))),
        tile((0, with_a(14, 0)), (2, with_b(1, sub, 2))),
    ]
    return tuple(jnp.concatenate([t[n] for t in tiles], axis=0) for n in range(3))


def _router_kernel(x_ref, nffn_ref, wq_ref, keys_ref, after_ref, h2_ref, idx_ref, gate_ref,
                   q_ref, idxt_ref, gatet_ref):
    del after_ref
    h2 = _rms(x_ref[...], nffn_ref[...]).astype(jnp.bfloat16)
    h2_ref[...] = h2
    q_ref[...] = jnp.dot(h2, wq_ref[...], preferred_element_type=jnp.float32).astype(jnp.bfloat16)
    dq = 2 * PEER_D_HALF
    nt = (((1,), (1,)), ((), ()))

    def head(hd, carry):
        off = pl.multiple_of(hd * dq, dq)
        s1 = lax.dot_general(keys_ref[0], q_ref[:, pl.ds(off, PEER_D_HALF)], nt,
                             preferred_element_type=jnp.float32)
        s2 = lax.dot_general(keys_ref[1], q_ref[:, pl.ds(off + PEER_D_HALF, PEER_D_HALF)], nt,
                             preferred_element_type=jnp.float32)
        v1, i1 = _topk_rows(s1, PEER_TOPK)
        v2, i2 = _topk_rows(s2, PEER_TOPK)
        cand, expert, flat = _pair_candidates(v1, i1, v2, i2)
        cv, ce = [], []
        for _ in range(PEER_TOPK):
            m = jnp.max(cand, axis=0, keepdims=True)
            ix = jnp.min(jnp.where(cand == m, flat, PEER_TOPK * PEER_TOPK), axis=0, keepdims=True)
            hit = flat == ix
            cv.append(m)
            ce.append(jnp.max(jnp.where(hit, expert, -1), axis=0, keepdims=True))
            cand = jnp.where(hit, -jnp.inf, cand)
        cvc = jnp.concatenate(cv, axis=0)
        e = jnp.exp(cvc - cv[0])
        gate = e / jnp.sum(e, axis=0, keepdims=True)
        row = pl.multiple_of(hd * PEER_TOPK, PEER_TOPK)
        idxt_ref[pl.ds(row, PEER_TOPK), :] = jnp.concatenate(ce, axis=0)
        gatet_ref[pl.ds(row, PEER_TOPK), :] = gate
        return carry

    lax.fori_loop(0, PEER_HEADS, head, 0)
    idx_ref[...] = idxt_ref[...].T
    gate_ref[...] = gatet_ref[...].T


def _router(x1, norm_ffn, wq, keys, tok0, T, after):
    D = x1.shape[1]
    tr = min(ROUTE_TILE, T)
    qw = wq.shape[1]
    assert tok0 % tr == 0 and T % tr == 0
    first = tok0 // tr
    full = lambda shape: pl.BlockSpec(shape, lambda i: (0,) * len(shape))
    return pl.pallas_call(
        _router_kernel,
        grid=(T // tr,),
        in_specs=[
            pl.BlockSpec((tr, D), lambda i: (first + i, 0)),
            full((1, D)),
            full((D, qw)),
            full((2, PEER_N_KEYS, PEER_D_HALF)),
            pl.BlockSpec(memory_space=pl.ANY),
        ],
        out_specs=[
            pl.BlockSpec((tr, D), lambda i: (i, 0)),
            pl.BlockSpec((tr, EXPERTS_PER_TOKEN), lambda i: (i, 0)),
            pl.BlockSpec((tr, EXPERTS_PER_TOKEN), lambda i: (i, 0)),
        ],
        out_shape=[
            jax.ShapeDtypeStruct((T, D), jnp.bfloat16),
            jax.ShapeDtypeStruct((T, EXPERTS_PER_TOKEN), jnp.int32),
            jax.ShapeDtypeStruct((T, EXPERTS_PER_TOKEN), jnp.float32),
        ],
        scratch_shapes=[
            pltpu.VMEM((tr, qw), jnp.bfloat16),
            pltpu.VMEM((EXPERTS_PER_TOKEN, tr), jnp.int32),
            pltpu.VMEM((EXPERTS_PER_TOKEN, tr), jnp.float32),
        ],
        compiler_params=pltpu.CompilerParams(
            dimension_semantics=("parallel",), vmem_limit_bytes=TC_VMEM_LIMIT),
        name="router",
    )(x1, norm_ffn.reshape(1, D), wq, keys, after)


def _scores_kernel(h_ref, u_ref, idx_ref, gate_ref, w_ref, pre_ref):
    j = pl.program_id(1)
    acc = lax.dot_general(h_ref[...], u_ref[...], (((1,), (1,)), ((), ())),
                          preferred_element_type=jnp.float32)
    groups = acc.shape[1] // V7X_LANES
    lane_bits = int(math.log2(V7X_LANES))
    idx = idx_ref[...]
    group, lane = idx >> lane_bits, idx & (V7X_LANES - 1)
    picked = jnp.zeros(idx.shape, jnp.float32)
    for n in range(groups):
        vals = jnp.take_along_axis(acc[:, n * V7X_LANES:(n + 1) * V7X_LANES], lane, axis=1)
        picked = jnp.where(group == j * groups + n, vals, picked)

    @pl.when(j == 0)
    def _():
        pre_ref[...] = picked

    @pl.when(j > 0)
    def _():
        pre_ref[...] += picked

    @pl.when(j == pl.num_programs(1) - 1)
    def _():
        w = (gate_ref[...] * _gelu(pre_ref[...])).astype(jnp.bfloat16).astype(jnp.float32)
        hi = pltpu.bitcast(w, jnp.uint32) & jnp.uint32(HI_HALF)
        w_ref[...] = pltpu.bitcast(hi | (hi >> 16), jnp.int32)


def _expert_weights(h2, u, idx, gate):
    T, D = h2.shape
    E = u.shape[0]
    K = idx.shape[1]
    tm, tn = min(SCORE_TOK_TILE, T), min(SCORE_EXP_TILE, E)
    assert T % tm == 0 and E % tn == 0 and tn % V7X_LANES == 0 and K == V7X_LANES
    per_token = pl.BlockSpec((tm, K), lambda i, j: (i, 0))
    return pl.pallas_call(
        _scores_kernel,
        grid=(T // tm, E // tn),
        in_specs=[pl.BlockSpec((tm, D), lambda i, j: (i, 0)), pl.BlockSpec((tn, D), lambda i, j: (j, 0)),
                  per_token, per_token],
        out_specs=per_token,
        out_shape=jax.ShapeDtypeStruct((T, K), jnp.int32),
        scratch_shapes=[pltpu.VMEM((tm, K), jnp.float32)],
        compiler_params=pltpu.CompilerParams(
            dimension_semantics=("parallel", "arbitrary"), vmem_limit_bytes=TC_VMEM_LIMIT),
        name="expert_weights",
    )(h2, u, idx, gate)


def _final_kernel(x_ref, y_ref, g_ref, *rest):
    o_ref = rest[-1]
    o_ref[...] = _rms(x_ref[...] + y_ref[...], g_ref[...])


def _final(x1, x_tok0, peer, norm_final, out_prev, out_tok0, total):
    Tc, D = peer.shape
    te = min(EW_TILE, Tc)
    assert x_tok0 % te == 0 and out_tok0 % te == 0 and Tc % te == 0
    nblk = Tc // te
    spec = pl.BlockSpec((te, D), lambda i: (out_tok0 // te + i, 0))
    in_specs = [pl.BlockSpec((te, D), lambda i: (x_tok0 // te + i, 0)),
                pl.BlockSpec((te, D), lambda i: (i, 0)), pl.BlockSpec((1, D), lambda i: (0, 0))]
    args = [x1, peer, norm_final.reshape(1, D)]
    aliases = {}
    if out_prev is not None:
        in_specs.append(pl.BlockSpec(memory_space=pl.ANY))
        args.append(out_prev)
        aliases = {3: 0}
    return pl.pallas_call(
        _final_kernel, grid=(nblk,), in_specs=in_specs, out_specs=spec,
        out_shape=jax.ShapeDtypeStruct((total, D), jnp.float32),
        input_output_aliases=aliases,
        compiler_params=pltpu.CompilerParams(dimension_semantics=("parallel",)),
        name="final_norm",
    )(*args)


def _tree_sum(vals):
    while len(vals) > 1:
        nxt = [vals[i] + vals[i + 1] for i in range(0, len(vals) - 1, 2)]
        if len(vals) % 2:
            nxt.append(vals[-1])
        vals = nxt
    return vals[0]


def _sc_block_pipeline(nblk, items_per_token, loads, store, gather, compute):
    ipt = items_per_token
    assert SC_ROW_BUFS > ipt and nblk >= 1

    for c in loads(0, 0):
        c.start()
    for c in loads(0, 0):
        c.wait()
    if nblk > 1:
        for c in loads(1, 1):
            c.start()
    for q in range(ipt):
        gather(0, 0, q, q).start()

    @pl.loop(0, nblk)
    def _(b):
        slot = b % 2

        @pl.when(b >= 2)
        def _():
            store(b - 2, slot).wait()

        @pl.loop(0, SC_TOKENS)
        def _(t):
            item0 = (b * SC_TOKENS + t) * ipt
            for q in range(ipt):
                nxt_buf = (item0 + q + ipt) % SC_ROW_BUFS

                @pl.when(t + 1 < SC_TOKENS)
                def _():
                    gather(slot, t + 1, q, nxt_buf).start()

                @pl.when(jnp.logical_and(t + 1 == SC_TOKENS, b + 1 < nblk))
                def _():
                    if q == 0:
                        for c in loads(b + 1, 1 - slot):
                            c.wait()
                    gather(1 - slot, 0, q, nxt_buf).start()

                buf = (item0 + q) % SC_ROW_BUFS
                gather(slot, t, q, buf).wait()
                compute(slot, t, q, buf)

        store(b, slot).start()

        @pl.when(b + 2 < nblk)
        def _():
            for c in loads(b + 2, slot):
                c.start()

    if nblk >= 2:
        store(nblk - 2, nblk % 2).wait()
    store(nblk - 1, (nblk - 1) % 2).wait()


def _sc_mesh():
    return plsc.VectorSubcoreMesh(core_axis_name="c", subcore_axis_name="s")


def _sc_worker_id():
    return lax.axis_index("s") * V7X_SC_CORES + lax.axis_index("c")


def _sc_bf16(words):
    return plsc.bitcast(words, jnp.bfloat16)


def _sc_halves_f32(pairs):
    words = plsc.bitcast(pairs, jnp.uint32)
    return (plsc.bitcast(words << 16, jnp.float32),
            plsc.bitcast(words & jnp.uint32(HI_HALF), jnp.float32))


def _mix_parts(T, K, DW):
    D = 2 * DW
    L, G = V7X_SC_LANES, SC_GATHER
    nj = SC_MIX_CHUNK // L
    tok_per_w = T // V7X_SC_WORKERS
    assert T % (V7X_SC_WORKERS * SC_TOKENS) == 0 and K % (2 * G) == 0
    assert DW % SC_MIX_CHUNK == 0 and G % SC_MIX_GROUP == 0
    scratch = [
        pltpu.VMEM((2, SC_TOKENS, K), jnp.int32),
        pltpu.VMEM((2, SC_TOKENS, K), jnp.int32),
        pltpu.VMEM((SC_ROW_BUFS, G, DW), jnp.uint32),
        pltpu.VMEM((2, SC_TOKENS, D), jnp.float32),
        pltpu.SemaphoreType.DMA((2,)),
        pltpu.SemaphoreType.DMA((2,)),
        pltpu.SemaphoreType.DMA((2,)),
        pltpu.SemaphoreType.DMA((SC_ROW_BUFS,)),
    ]

    def run(w_hbm, idx_hbm, tab_hbm, out_hbm, idx_v, w_v, rows_v, out_v, idx_sems, w_sems, out_sems, row_sems):
        base = _sc_worker_id() * tok_per_w

        def loads(b, slot):
            toks = pl.ds(base + b * SC_TOKENS, SC_TOKENS)
            return [pltpu.make_async_copy(idx_hbm.at[toks], idx_v.at[slot], idx_sems.at[slot]),
                    pltpu.make_async_copy(w_hbm.at[toks], w_v.at[slot], w_sems.at[slot])]

        def store(b, slot):
            toks = pl.ds(base + b * SC_TOKENS, SC_TOKENS)
            return pltpu.make_async_copy(out_v.at[slot], out_hbm.at[toks], out_sems.at[slot])

        def gather(slot, t, q, buf):
            return pltpu.make_async_copy(
                tab_hbm.at[idx_v.at[slot, t, pl.ds(q * G, G)]], rows_v.at[buf], row_sems.at[buf])

        def compute(slot, t, q, buf):
            ssplat = jnp.full((L,), slot, jnp.int32)
            tsplat = jnp.full((L,), t, jnp.int32)
            for c in range(DW // SC_MIX_CHUNK):
                def body(kg, acc):
                    kk = kg * SC_MIX_GROUP
                    wks = [_sc_bf16(plsc.load_gather(
                        w_v, [ssplat, tsplat, jnp.full((L,), q * G + i, jnp.int32) + kk]))
                        for i in range(SC_MIX_GROUP)]
                    out = []
                    for j in range(nj):
                        prods = [wks[i] * _sc_bf16(rows_v[buf, kk + i, pl.ds(c * SC_MIX_CHUNK + j * L, L)])
                                 for i in range(SC_MIX_GROUP)]
                        lo, hi = _sc_halves_f32(_tree_sum(prods))
                        out += [acc[2 * j] + lo, acc[2 * j + 1] + hi]
                    return tuple(out)

                zero = jnp.zeros((L,), jnp.float32)
                acc = plsc.parallel_loop(0, G // SC_MIX_GROUP, carry=(zero,) * (2 * nj))(body)
                for j in range(nj):
                    for half in range(2):
                        dst = out_v.at[slot, t, pl.ds(half * DW + c * SC_MIX_CHUNK + j * L, L)]
                        if q == 0:
                            dst[...] = acc[2 * j + half]
                        else:
                            plsc.addupdate(dst, acc[2 * j + half])

        _sc_block_pipeline(tok_per_w // SC_TOKENS, K // G, loads, store, gather, compute)

    return scratch, run


def _sc_kernel(out_type, scratch_types, name):
    return functools.partial(
        pl.kernel, mesh=_sc_mesh(), out_type=out_type, scratch_types=scratch_types,
        compiler_params=pltpu.CompilerParams(needs_layout_passes=False), name=name)


def _pack_table(a):
    n, c = a.shape
    half = c // 2
    L, rb = V7X_SC_LANES, SC_PACK_ROWS
    rows_per_w = n // V7X_SC_WORKERS
    nblk = rows_per_w // rb
    assert n % (V7X_SC_WORKERS * rb) == 0 and half % L == 0

    @_sc_kernel(jax.ShapeDtypeStruct((n, half), jnp.uint32),
                [pltpu.VMEM((2, rb, c), jnp.float32), pltpu.VMEM((2, rb, half), jnp.uint32),
                 pltpu.SemaphoreType.DMA((2,)), pltpu.SemaphoreType.DMA((2,))], "pack_table")
    def k(a_hbm, o_hbm, a_v, o_v, in_sems, out_sems):
        base = _sc_worker_id() * rows_per_w

        def load(b, slot):
            return pltpu.make_async_copy(a_hbm.at[pl.ds(base + b * rb, rb)], a_v.at[slot], in_sems.at[slot])

        def store(b, slot):
            return pltpu.make_async_copy(o_v.at[slot], o_hbm.at[pl.ds(base + b * rb, rb)], out_sems.at[slot])

        load(0, 0).start()

        @pl.loop(0, nblk)
        def _(b):
            slot = b % 2
            load(b, slot).wait()

            @pl.when(b + 1 < nblk)
            def _():
                load(b + 1, 1 - slot).start()

            @pl.when(b >= 2)
            def _():
                store(b - 2, slot).wait()

            @plsc.parallel_loop(0, rb)
            def _(r):
                for j in range(half // L):
                    pairs = plsc.pack(a_v[slot, r, pl.ds(j * L, L)], a_v[slot, r, pl.ds(half + j * L, L)],
                                      format=plsc.PackFormat.INTERLEAVED)
                    o_v[slot, r, pl.ds(j * L, L)] = plsc.bitcast(pairs, jnp.uint32)

            store(b, slot).start()

        if nblk >= 2:
            store(nblk - 2, nblk % 2).wait()
        store(nblk - 1, (nblk - 1) % 2).wait()

    return k(a)


def _expert_mix(w, idx, table):
    T, K = w.shape
    DW = table.shape[1]
    scratch, run = _mix_parts(T, K, DW)

    @_sc_kernel(jax.ShapeDtypeStruct((T, 2 * DW), jnp.float32), scratch, "expert_mix")
    def k(w_hbm, idx_hbm, tab_hbm, out_hbm, *mix_scratch):
        run(w_hbm, idx_hbm, tab_hbm, out_hbm, *mix_scratch)

    return k(w, idx, table)


def _chunk_sizes(total):
    ramp, size = [EDGE_CHUNK], EDGE_CHUNK
    while size < MAX_CHUNK:
        ramp.append(size)
        size *= 2
    drain = ramp[2:][::-1] + [2 * EDGE_CHUNK] if len(ramp) > 1 else ramp
    middle = total - sum(ramp) - sum(drain)
    assert middle >= 0 and middle % MAX_CHUNK == 0
    return ramp + [MAX_CHUNK] * (middle // MAX_CHUNK) + drain


def kernel(x, norm_mix, w_in, pool_w, pool_scale, sgu_ln_g, sgu_ln_b, sgu_w, sgu_b, out_norm_pool,
           out_norm_sgu, w_out, norm_ffn, peer_wq, peer_keys, peer_u, peer_v, norm_final):
    B, S, D = x.shape
    assert norm_mix.shape[0] == 1, "single-layer block"
    T = B * S
    mix_args = (norm_mix[0], w_in[0], pool_w[0], pool_scale[0], sgu_ln_g[0], sgu_ln_b[0],
                sgu_w[0], sgu_b[0], out_norm_pool[0], out_norm_sgu[0], w_out[0])
    x1_parts = [(b * S, _mixer(x, b, 1, *mix_args).reshape(S, D)) for b in range(B)]
    wq = peer_wq[0].astype(jnp.bfloat16)
    keys = peer_keys[0].astype(jnp.bfloat16)
    u_tab = _table_bf16(peer_u[0])
    v_tab = _pack_table(peer_v[0])
    out = None
    tok0 = 0
    peers = [norm_ffn, v_tab]
    outs = []
    chunks = _chunk_sizes(T)
    for c, tc in enumerate(chunks):
        part0, x1 = [p for p in x1_parts if p[0] <= tok0][-1]
        assert tok0 + tc <= part0 + x1.shape[0], "a token chunk must lie inside one mixer call"
        after = outs[-2] if c >= max(len(chunks) - 2, 2) else peers[-2]
        h2, idx, gate = _router(x1, norm_ffn[0], wq, keys, tok0 - part0, tc, after)
        w = _expert_weights(h2, u_tab, idx, gate)
        peer = _expert_mix(w, idx, v_tab)
        peers.append(peer)
        out = _final(x1, tok0 - part0, peer, norm_final, out, tok0, T)
        outs.append(out)
        tok0 += tc
    return out.reshape(B, S, D)
```

```python
import functools
import math

import jax
import jax.numpy as jnp
from jax import lax
from jax.experimental import pallas as pl
from jax.experimental.pallas import tpu as pltpu
from jax.experimental.pallas import tpu_sc as plsc

POOL_WINDOWS = (2, 4, 8, 16)
N_POOL_GROUPS = len(POOL_WINDOWS)
SGU_HEADS = 4
SGU_CHUNK = 128
PEER_HEADS = 8
PEER_N_KEYS = 128
PEER_D_HALF = 128
PEER_TOPK = 16
NORM_EPS = 1e-6
EXPERTS_PER_TOKEN = PEER_HEADS * PEER_TOPK

V7X_LANES = 128
V7X_SUBLANES = 8
V7X_SC_CORES = 2
V7X_SC_SUBCORES = 16
V7X_SC_LANES = 16
V7X_SC_WORKERS = V7X_SC_CORES * V7X_SC_SUBCORES

HALO = max(POOL_WINDOWS)
MIX_TILE = 512
ROUTE_TILE = 1024
EW_TILE = 512
SCORE_TOK_TILE = 2048
SCORE_EXP_TILE = 2048
SC_PACK_ROWS = 16
SC_GATHER = 64
SC_TOKENS = 8
SC_ROW_BUFS = 3
SC_MIX_CHUNK = 128
SC_MIX_GROUP = 4
HI_HALF = 0xFFFF0000
V7X_VMEM_BYTES = 64 * 1024 * 1024
TC_VMEM_LIMIT = V7X_VMEM_BYTES * 3 // 4
EDGE_CHUNK = 512
MAX_CHUNK = 2048


def _rms(x, g):
    inv = lax.rsqrt(jnp.mean(x * x, axis=-1, keepdims=True) + NORM_EPS)
    return x * inv * g


def _transpose_cast_kernel(a_ref, after_ref, o_ref):
    del after_ref
    o_ref[...] = a_ref[...].T.astype(o_ref.dtype)


def _transposed_bf16(a, after):
    n, c = a.shape
    rows = min(SCORE_EXP_TILE, n)
    return pl.pallas_call(
        _transpose_cast_kernel, grid=(n // rows,),
        in_specs=[pl.BlockSpec((rows, c), lambda i: (i, 0)), pl.BlockSpec(memory_space=pl.ANY)],
        out_specs=pl.BlockSpec((c, rows), lambda i: (0, i)),
        out_shape=jax.ShapeDtypeStruct((c, n), jnp.bfloat16),
        compiler_params=pltpu.CompilerParams(dimension_semantics=("parallel",)),
        name="transpose_table",
    )(a, after)


def _gelu(x):
    return 0.5 * x * (1.0 + lax.erf(x * math.sqrt(0.5)))


def _mixer_kernel(x_ref, xh_ref, nmix_ref, win_ref, poolw_ref, pscale_ref, lng_ref, lnb_ref,
                  sguw_ref, sgub_ref, onp_ref, ons_ref, wout_ref, o_ref, pext_ref, mix_ref):
    i = pl.program_id(1)
    ts = x_ref.shape[1]
    pool_w = pscale_ref.shape[1]
    gdim = pool_w // N_POOL_GROUPS
    sgu_w = lng_ref.shape[1]
    hdim = sgu_w // SGU_HEADS

    x = x_ref[0]
    h = _rms(x, nmix_ref[...]).astype(jnp.bfloat16)
    z = jnp.dot(h, win_ref[...], preferred_element_type=jnp.float32)
    p = z[:, :pool_w]

    hh = _rms(xh_ref[0], nmix_ref[...]).astype(jnp.bfloat16)
    ph = jnp.dot(hh, win_ref[:, :pool_w], preferred_element_type=jnp.float32)
    ph = jnp.where(i > 0, ph, 0.0)
    pext_ref[0:HALO, :] = ph
    pext_ref[HALO:HALO + ts, :] = p

    pos = i * ts + lax.broadcasted_iota(jnp.int32, (ts, 1), 0)
    ssq = jnp.zeros((ts, 1), jnp.float32)
    a_parts = []
    for g, win in enumerate(POOL_WINDOWS):
        cols = slice(g * gdim, (g + 1) * gdim)
        s = pext_ref[HALO:HALO + ts, cols]
        for j in range(1, win):
            s = s + pext_ref[HALO - j:HALO - j + ts, cols]
        cnt = jnp.minimum(pos + 1, win).astype(jnp.float32)
        d = (s / cnt - p[:, cols]).astype(jnp.bfloat16)
        a = jnp.dot(d, poolw_ref[g], preferred_element_type=jnp.float32) * pscale_ref[:, cols]
        ssq = ssq + jnp.sum(a * a, axis=-1, keepdims=True)
        a_parts.append(a)
    inv_a = lax.rsqrt(ssq / pool_w + NORM_EPS)
    for g in range(N_POOL_GROUPS):
        cols = slice(g * gdim, (g + 1) * gdim)
        mix_ref[:, cols] = (a_parts[g] * inv_a * onp_ref[:, cols]).astype(jnp.bfloat16)

    gz = _gelu(z[:, pool_w:])
    tril = (lax.broadcasted_iota(jnp.int32, (SGU_CHUNK, SGU_CHUNK), 0)
            >= lax.broadcasted_iota(jnp.int32, (SGU_CHUNK, SGU_CHUNK), 1))
    ssq = jnp.zeros((ts, 1), jnp.float32)
    b_parts = []
    for hd in range(SGU_HEADS):
        cols = slice(hd * hdim, (hd + 1) * hdim)
        u = gz[:, hd * hdim:(hd + 1) * hdim]
        v = gz[:, sgu_w + hd * hdim:sgu_w + (hd + 1) * hdim]
        mu = jnp.mean(v, axis=-1, keepdims=True)
        vc = v - mu
        var = jnp.mean(vc * vc, axis=-1, keepdims=True)
        vn = (vc * lax.rsqrt(var + NORM_EPS) * lng_ref[:, cols] + lnb_ref[:, cols]).astype(jnp.bfloat16)
        w = jnp.where(tril, sguw_ref[hd], jnp.zeros((), sguw_ref.dtype))
        mixed = [jnp.dot(w, vn[n * SGU_CHUNK:(n + 1) * SGU_CHUNK], preferred_element_type=jnp.float32)
                 + sgub_ref[hd] for n in range(ts // SGU_CHUNK)]
        b = u * jnp.concatenate(mixed, axis=0)
        ssq = ssq + jnp.sum(b * b, axis=-1, keepdims=True)
        b_parts.append(b)
    inv_b = lax.rsqrt(ssq / sgu_w + NORM_EPS)
    for hd in range(SGU_HEADS):
        cols = slice(hd * hdim, (hd + 1) * hdim)
        mix_ref[:, pool_w + hd * hdim:pool_w + (hd + 1) * hdim] = (
            b_parts[hd] * inv_b * ons_ref[:, cols]).astype(jnp.bfloat16)

    o_ref[0] = x + jnp.dot(mix_ref[...], wout_ref[...], preferred_element_type=jnp.float32)


def _mixer(x, b0, nb, norm_mix, w_in, pool_w, pool_scale, ln_g, ln_b, sgu_w, sgu_b, on_pool, on_sgu, w_out):
    _, S, D = x.shape
    ts = min(MIX_TILE, S)
    pool_width = pool_scale.size
    sgu_width = ln_g.size
    in_width = w_in.shape[1]
    gdim = pool_width // N_POOL_GROUPS
    halo_blocks = ts // HALO
    full = lambda shape: pl.BlockSpec(shape, lambda b, i: (0,) * len(shape))
    return pl.pallas_call(
        _mixer_kernel,
        grid=(nb, S // ts),
        in_specs=[
            pl.BlockSpec((1, ts, D), lambda b, i: (b0 + b, i, 0)),
            pl.BlockSpec((1, HALO, D), lambda b, i: (b0 + b, jnp.maximum(i * halo_blocks - 1, 0), 0)),
            full((1, D)),
            full((D, in_width)),
            full((N_POOL_GROUPS, gdim, gdim)),
            full((1, pool_width)),
            full((1, sgu_width)),
            full((1, sgu_width)),
            full((SGU_HEADS, SGU_CHUNK, SGU_CHUNK)),
            full((SGU_HEADS, SGU_CHUNK, SGU_CHUNK)),
            full((1, pool_width)),
            full((1, sgu_width)),
            full((pool_width + sgu_width, D)),
        ],
        out_specs=pl.BlockSpec((1, ts, D), lambda b, i: (b, i, 0)),
        out_shape=jax.ShapeDtypeStruct((nb, S, D), jnp.float32),
        scratch_shapes=[
            pltpu.VMEM((HALO + ts, pool_width), jnp.float32),
            pltpu.VMEM((ts, pool_width + sgu_width), jnp.bfloat16),
        ],
        compiler_params=pltpu.CompilerParams(
            dimension_semantics=("parallel", "arbitrary"), vmem_limit_bytes=TC_VMEM_LIMIT),
        name="mixer",
    )(x, x, norm_mix.reshape(1, D), w_in.astype(jnp.bfloat16), pool_w.astype(jnp.bfloat16),
      pool_scale.reshape(1, pool_width), ln_g.reshape(1, sgu_width), ln_b.reshape(1, sgu_width),
      sgu_w.astype(jnp.bfloat16),
      jnp.broadcast_to(sgu_b[:, :, None], (SGU_HEADS, SGU_CHUNK, SGU_CHUNK)),
      on_pool.reshape(1, pool_width), on_sgu.reshape(1, sgu_width), w_out.astype(jnp.bfloat16))


def _topk_rows(s, k):
    sub = V7X_SUBLANES
    n, m = s.shape
    pieces = n // sub
    assert n % sub == 0 and pieces >= k, "lists shorter than k would need padding"
    row = lax.broadcasted_iota(jnp.int32, (sub, m), 0)
    v = [s[p * sub:(p + 1) * sub] for p in range(pieces)]
    ix = [row + p * sub for p in range(pieces)]
    for rnd in range(pieces):
        for p in range(rnd % 2, pieces - 1, 2):
            swap = v[p + 1] > v[p]
            v[p], v[p + 1] = jnp.where(swap, v[p + 1], v[p]), jnp.where(swap, v[p], v[p + 1])
            ix[p], ix[p + 1] = jnp.where(swap, ix[p + 1], ix[p]), jnp.where(swap, ix[p], ix[p + 1])
    vals, idxs = [], []
    for r in range(k):
        best = jnp.max(v[0], axis=0, keepdims=True)
        best_ix = jnp.min(jnp.where(v[0] == best, ix[0], n), axis=0, keepdims=True)
        vals.append(best)
        idxs.append(best_ix)
        taken = ix[0] == best_ix
        for p in range(k - 1 - r):
            v[p] = jnp.where(taken, v[p + 1], v[p])
            ix[p] = jnp.where(taken, ix[p + 1], ix[p])
    return vals, idxs


def _pair_candidates(v1, i1, v2, i2):
    k, sub = PEER_TOPK, V7X_SUBLANES
    assert k == 16 and sub == 8, "the tile plan below is written for 16 candidates per side"
    v1c, i1c = jnp.concatenate(v1, axis=0), jnp.concatenate(i1, axis=0)
    v2c, i2c = jnp.concatenate(v2, axis=0), jnp.concatenate(i2, axis=0)
    m = v2c.shape[1]
    r = lax.broadcasted_iota(jnp.int32, (sub, m), 0)

    def with_b(a, b0, row0):
        shift = lambda x: x if row0 == 0 else pltpu.roll(x, row0, 0)
        return (v1[a] + shift(v2c[b0:b0 + sub]), i1[a] * PEER_N_KEYS + shift(i2c[b0:b0 + sub]),
                a * k + b0 + (r - row0))

    def with_a(a0, row0):
        s = (row0 - (a0 - sub)) % sub
        shift = lambda x: x if s == 0 else pltpu.roll(x, s, 0)
        return (shift(v1c[sub:]) + v2[0], shift(i1c[sub:]) * PEER_N_KEYS + i2[0], (a0 + (r - row0)) * k)

    def tile(*parts):
        out = parts[0][1]
        for row0, seg in parts[1:]:
            out = tuple(jnp.where(r < row0, o, s) for o, s in zip(out, seg))
        return out

    tiles = [
        with_b(0, 0, 0), with_b(0, sub, 0),
        with_b(1, 0, 0),
        tile((0, with_b(2, 0, 0)), (5, with_b(4, 0, 5))),
        tile((0, with_b(3, 0, 0)), (4, with_b(5, 0, 4)), (6, with_b(6, 0, 6))),
        tile((0, with_b(7, 0, 0)), (2, with_a(8, 2))),
        tile((0, with_a(14, 0)), (2, with_b(1, sub, 2))),
    ]
    return tuple(jnp.concatenate([t[n] for t in tiles], axis=0) for n in range(3))


def _router_kernel(x_ref, nffn_ref, wq_ref, keys_ref, after_ref, h2_ref, idx_ref, gate_ref,
                   q_ref, idxt_ref, gatet_ref):
    del after_ref
    h2 = _rms(x_ref[...], nffn_ref[...]).astype(jnp.bfloat16)
    h2_ref[...] = h2
    q_ref[...] = jnp.dot(h2, wq_ref[...], preferred_element_type=jnp.float32).astype(jnp.bfloat16)
    dq = 2 * PEER_D_HALF
    nt = (((1,), (1,)), ((), ()))

    def head(hd, carry):
        off = pl.multiple_of(hd * dq, dq)
        s1 = lax.dot_general(keys_ref[0], q_ref[:, pl.ds(off, PEER_D_HALF)], nt,
                             preferred_element_type=jnp.float32)
        s2 = lax.dot_general(keys_ref[1], q_ref[:, pl.ds(off + PEER_D_HALF, PEER_D_HALF)], nt,
                             preferred_element_type=jnp.float32)
        v1, i1 = _topk_rows(s1, PEER_TOPK)
        v2, i2 = _topk_rows(s2, PEER_TOPK)
        cand, expert, flat = _pair_candidates(v1, i1, v2, i2)
        cv, ce = [], []
        for _ in range(PEER_TOPK):
            m = jnp.max(cand, axis=0, keepdims=True)
            ix = jnp.min(jnp.where(cand == m, flat, PEER_TOPK * PEER_TOPK), axis=0, keepdims=True)
            hit = flat == ix
            cv.append(m)
            ce.append(jnp.max(jnp.where(hit, expert, -1), axis=0, keepdims=True))
            cand = jnp.where(hit, -jnp.inf, cand)
        cvc = jnp.concatenate(cv, axis=0)
        e = jnp.exp(cvc - cv[0])
        gate = e / jnp.sum(e, axis=0, keepdims=True)
        row = pl.multiple_of(hd * PEER_TOPK, PEER_TOPK)
        idxt_ref[pl.ds(row, PEER_TOPK), :] = jnp.concatenate(ce, axis=0)
        gatet_ref[pl.ds(row, PEER_TOPK), :] = gate
        return carry

    lax.fori_loop(0, PEER_HEADS, head, 0)
    idx_ref[...] = idxt_ref[...].T
    gate_ref[...] = gatet_ref[...].T


def _router(x1, norm_ffn, wq, keys, tok0, T, after):
    D = x1.shape[1]
    tr = min(ROUTE_TILE, T)
    qw = wq.shape[1]
    assert tok0 % tr == 0 and T % tr == 0
    first = tok0 // tr
    full = lambda shape: pl.BlockSpec(shape, lambda i: (0,) * len(shape))
    return pl.pallas_call(
        _router_kernel,
        grid=(T // tr,),
        in_specs=[
            pl.BlockSpec((tr, D), lambda i: (first + i, 0)),
            full((1, D)),
            full((D, qw)),
            full((2, PEER_N_KEYS, PEER_D_HALF)),
            pl.BlockSpec(memory_space=pl.ANY),
        ],
        out_specs=[
            pl.BlockSpec((tr, D), lambda i: (i, 0)),
            pl.BlockSpec((tr, EXPERTS_PER_TOKEN), lambda i: (i, 0)),
            pl.BlockSpec((tr, EXPERTS_PER_TOKEN), lambda i: (i, 0)),
        ],
        out_shape=[
            jax.ShapeDtypeStruct((T, D), jnp.bfloat16),
            jax.ShapeDtypeStruct((T, EXPERTS_PER_TOKEN), jnp.int32),
            jax.ShapeDtypeStruct((T, EXPERTS_PER_TOKEN), jnp.float32),
        ],
        scratch_shapes=[
            pltpu.VMEM((tr, qw), jnp.bfloat16),
            pltpu.VMEM((EXPERTS_PER_TOKEN, tr), jnp.int32),
            pltpu.VMEM((EXPERTS_PER_TOKEN, tr), jnp.float32),
        ],
        compiler_params=pltpu.CompilerParams(
            dimension_semantics=("parallel",), vmem_limit_bytes=TC_VMEM_LIMIT),
        name="router",
    )(x1, norm_ffn.reshape(1, D), wq, keys, after)


def _scores_kernel(h_ref, ut_ref, idx_ref, gate_ref, w_ref, pre_ref):
    j = pl.program_id(1)
    acc = jnp.dot(h_ref[...], ut_ref[...], preferred_element_type=jnp.float32)
    groups = acc.shape[1] // V7X_LANES
    lane_bits = int(math.log2(V7X_LANES))
    idx = idx_ref[...]
    group, lane = idx >> lane_bits, idx & (V7X_LANES - 1)
    picked = jnp.zeros(idx.shape, jnp.float32)
    for n in range(groups):
        vals = jnp.take_along_axis(acc[:, n * V7X_LANES:(n + 1) * V7X_LANES], lane, axis=1)
        picked = jnp.where(group == j * groups + n, vals, picked)

    @pl.when(j == 0)
    def _():
        pre_ref[...] = picked

    @pl.when(j > 0)
    def _():
        pre_ref[...] += picked

    @pl.when(j == pl.num_programs(1) - 1)
    def _():
        w = (gate_ref[...] * _gelu(pre_ref[...])).astype(jnp.bfloat16).astype(jnp.float32)
        hi = pltpu.bitcast(w, jnp.uint32) & jnp.uint32(HI_HALF)
        w_ref[...] = pltpu.bitcast(hi | (hi >> 16), jnp.int32)


def _expert_weights(h2, u_t, idx, gate):
    T, D = h2.shape
    E = u_t.shape[1]
    K = idx.shape[1]
    tm, tn = min(SCORE_TOK_TILE, T), min(SCORE_EXP_TILE, E)
    assert T % tm == 0 and E % tn == 0 and tn % V7X_LANES == 0 and K == V7X_LANES
    per_token = pl.BlockSpec((tm, K), lambda i, j: (i, 0))
    return pl.pallas_call(
        _scores_kernel,
        grid=(T // tm, E // tn),
        in_specs=[pl.BlockSpec((tm, D), lambda i, j: (i, 0)), pl.BlockSpec((D, tn), lambda i, j: (0, j)),
                  per_token, per_token],
        out_specs=per_token,
        out_shape=jax.ShapeDtypeStruct((T, K), jnp.int32),
        scratch_shapes=[pltpu.VMEM((tm, K), jnp.float32)],
        compiler_params=pltpu.CompilerParams(
            dimension_semantics=("parallel", "arbitrary"), vmem_limit_bytes=TC_VMEM_LIMIT),
        name="expert_weights",
    )(h2, u_t, idx, gate)


def _final_kernel(x_ref, y_ref, g_ref, *rest):
    o_ref = rest[-1]
    o_ref[...] = _rms(x_ref[...] + y_ref[...], g_ref[...])


def _final(x1, x_tok0, peer, norm_final, out_prev, out_tok0, total):
    Tc, D = peer.shape
    te = min(EW_TILE, Tc)
    assert x_tok0 % te == 0 and out_tok0 % te == 0 and Tc % te == 0
    nblk = Tc // te
    spec = pl.BlockSpec((te, D), lambda i: (out_tok0 // te + i, 0))
    in_specs = [pl.BlockSpec((te, D), lambda i: (x_tok0 // te + i, 0)),
                pl.BlockSpec((te, D), lambda i: (i, 0)), pl.BlockSpec((1, D), lambda i: (0, 0))]
    args = [x1, peer, norm_final.reshape(1, D)]
    aliases = {}
    if out_prev is not None:
        in_specs.append(pl.BlockSpec(memory_space=pl.ANY))
        args.append(out_prev)
        aliases = {3: 0}
    return pl.pallas_call(
        _final_kernel, grid=(nblk,), in_specs=in_specs, out_specs=spec,
        out_shape=jax.ShapeDtypeStruct((total, D), jnp.float32),
        input_output_aliases=aliases,
        compiler_params=pltpu.CompilerParams(dimension_semantics=("parallel",)),
        name="final_norm",
    )(*args)


def _tree_sum(vals):
    while len(vals) > 1:
        nxt = [vals[i] + vals[i + 1] for i in range(0, len(vals) - 1, 2)]
        if len(vals) % 2:
            nxt.append(vals[-1])
        vals = nxt
    return vals[0]


def _sc_block_pipeline(nblk, items_per_token, loads, store, gather, compute):
    ipt = items_per_token
    assert SC_ROW_BUFS > ipt and nblk >= 1

    for c in loads(0, 0):
        c.start()
    for c in loads(0, 0):
        c.wait()
    if nblk > 1:
        for c in loads(1, 1):
            c.start()
    for q in range(ipt):
        gather(0, 0, q, q).start()

    @pl.loop(0, nblk)
    def _(b):
        slot = b % 2

        @pl.when(b >= 2)
        def _():
            store(b - 2, slot).wait()

        @pl.loop(0, SC_TOKENS)
        def _(t):
            item0 = (b * SC_TOKENS + t) * ipt
            for q in range(ipt):
                nxt_buf = (item0 + q + ipt) % SC_ROW_BUFS

                @pl.when(t + 1 < SC_TOKENS)
                def _():
                    gather(slot, t + 1, q, nxt_buf).start()

                @pl.when(jnp.logical_and(t + 1 == SC_TOKENS, b + 1 < nblk))
                def _():
                    if q == 0:
                        for c in loads(b + 1, 1 - slot):
                            c.wait()
                    gather(1 - slot, 0, q, nxt_buf).start()

                buf = (item0 + q) % SC_ROW_BUFS
                gather(slot, t, q, buf).wait()
                compute(slot, t, q, buf)

        store(b, slot).start()

        @pl.when(b + 2 < nblk)
        def _():
            for c in loads(b + 2, slot):
                c.start()

    if nblk >= 2:
        store(nblk - 2, nblk % 2).wait()
    store(nblk - 1, (nblk - 1) % 2).wait()


def _sc_mesh():
    return plsc.VectorSubcoreMesh(core_axis_name="c", subcore_axis_name="s")


def _sc_worker_id():
    return lax.axis_index("s") * V7X_SC_CORES + lax.axis_index("c")


def _sc_bf16(words):
    return plsc.bitcast(words, jnp.bfloat16)


def _sc_halves_f32(pairs):
    words = plsc.bitcast(pairs, jnp.uint32)
    return (plsc.bitcast(words << 16, jnp.float32),
            plsc.bitcast(words & jnp.uint32(HI_HALF), jnp.float32))


def _mix_parts(T, K, DW):
    D = 2 * DW
    L, G = V7X_SC_LANES, SC_GATHER
    nj = SC_MIX_CHUNK // L
    tok_per_w = T // V7X_SC_WORKERS
    assert T % (V7X_SC_WORKERS * SC_TOKENS) == 0 and K % (2 * G) == 0
    assert DW % SC_MIX_CHUNK == 0 and G % SC_MIX_GROUP == 0
    scratch = [
        pltpu.VMEM((2, SC_TOKENS, K), jnp.int32),
        pltpu.VMEM((2, SC_TOKENS, K), jnp.int32),
        pltpu.VMEM((SC_ROW_BUFS, G, DW), jnp.uint32),
        pltpu.VMEM((2, SC_TOKENS, D), jnp.float32),
        pltpu.SemaphoreType.DMA((2,)),
        pltpu.SemaphoreType.DMA((2,)),
        pltpu.SemaphoreType.DMA((2,)),
        pltpu.SemaphoreType.DMA((SC_ROW_BUFS,)),
    ]

    def run(w_hbm, idx_hbm, tab_hbm, out_hbm, idx_v, w_v, rows_v, out_v, idx_sems, w_sems, out_sems, row_sems):
        base = _sc_worker_id() * tok_per_w

        def loads(b, slot):
            toks = pl.ds(base + b * SC_TOKENS, SC_TOKENS)
            return [pltpu.make_async_copy(idx_hbm.at[toks], idx_v.at[slot], idx_sems.at[slot]),
                    pltpu.make_async_copy(w_hbm.at[toks], w_v.at[slot], w_sems.at[slot])]

        def store(b, slot):
            toks = pl.ds(base + b * SC_TOKENS, SC_TOKENS)
            return pltpu.make_async_copy(out_v.at[slot], out_hbm.at[toks], out_sems.at[slot])

        def gather(slot, t, q, buf):
            return pltpu.make_async_copy(
                tab_hbm.at[idx_v.at[slot, t, pl.ds(q * G, G)]], rows_v.at[buf], row_sems.at[buf])

        def compute(slot, t, q, buf):
            ssplat = jnp.full((L,), slot, jnp.int32)
            tsplat = jnp.full((L,), t, jnp.int32)
            for c in range(DW // SC_MIX_CHUNK):
                def body(kg, acc):
                    kk = kg * SC_MIX_GROUP
                    wks = [_sc_bf16(plsc.load_gather(
                        w_v, [ssplat, tsplat, jnp.full((L,), q * G + i, jnp.int32) + kk]))
                        for i in range(SC_MIX_GROUP)]
                    out = []
                    for j in range(nj):
                        prods = [wks[i] * _sc_bf16(rows_v[buf, kk + i, pl.ds(c * SC_MIX_CHUNK + j * L, L)])
                                 for i in range(SC_MIX_GROUP)]
                        lo, hi = _sc_halves_f32(_tree_sum(prods))
                        out += [acc[2 * j] + lo, acc[2 * j + 1] + hi]
                    return tuple(out)

                zero = jnp.zeros((L,), jnp.float32)
                acc = plsc.parallel_loop(0, G // SC_MIX_GROUP, carry=(zero,) * (2 * nj))(body)
                for j in range(nj):
                    for half in range(2):
                        dst = out_v.at[slot, t, pl.ds(half * DW + c * SC_MIX_CHUNK + j * L, L)]
                        if q == 0:
                            dst[...] = acc[2 * j + half]
                        else:
                            plsc.addupdate(dst, acc[2 * j + half])

        _sc_block_pipeline(tok_per_w // SC_TOKENS, K // G, loads, store, gather, compute)

    return scratch, run


def _sc_kernel(out_type, scratch_types, name):
    return functools.partial(
        pl.kernel, mesh=_sc_mesh(), out_type=out_type, scratch_types=scratch_types,
        compiler_params=pltpu.CompilerParams(needs_layout_passes=False), name=name)


def _pack_table(a):
    n, c = a.shape
    half = c // 2
    L, rb = V7X_SC_LANES, SC_PACK_ROWS
    rows_per_w = n // V7X_SC_WORKERS
    nblk = rows_per_w // rb
    assert n % (V7X_SC_WORKERS * rb) == 0 and half % L == 0

    @_sc_kernel(jax.ShapeDtypeStruct((n, half), jnp.uint32),
                [pltpu.VMEM((2, rb, c), jnp.float32), pltpu.VMEM((2, rb, half), jnp.uint32),
                 pltpu.SemaphoreType.DMA((2,)), pltpu.SemaphoreType.DMA((2,))], "pack_table")
    def k(a_hbm, o_hbm, a_v, o_v, in_sems, out_sems):
        base = _sc_worker_id() * rows_per_w

        def load(b, slot):
            return pltpu.make_async_copy(a_hbm.at[pl.ds(base + b * rb, rb)], a_v.at[slot], in_sems.at[slot])

        def store(b, slot):
            return pltpu.make_async_copy(o_v.at[slot], o_hbm.at[pl.ds(base + b * rb, rb)], out_sems.at[slot])

        load(0, 0).start()

        @pl.loop(0, nblk)
        def _(b):
            slot = b % 2
            load(b, slot).wait()

            @pl.when(b + 1 < nblk)
            def _():
                load(b + 1, 1 - slot).start()

            @pl.when(b >= 2)
            def _():
                store(b - 2, slot).wait()

            @plsc.parallel_loop(0, rb)
            def _(r):
                for j in range(half // L):
                    pairs = plsc.pack(a_v[slot, r, pl.ds(j * L, L)], a_v[slot, r, pl.ds(half + j * L, L)],
                                      format=plsc.PackFormat.INTERLEAVED)
                    o_v[slot, r, pl.ds(j * L, L)] = plsc.bitcast(pairs, jnp.uint32)

            store(b, slot).start()

        if nblk >= 2:
            store(nblk - 2, nblk % 2).wait()
        store(nblk - 1, (nblk - 1) % 2).wait()

    return k(a)


def _expert_mix(w, idx, table):
    T, K = w.shape
    DW = table.shape[1]
    scratch, run = _mix_parts(T, K, DW)

    @_sc_kernel(jax.ShapeDtypeStruct((T, 2 * DW), jnp.float32), scratch, "expert_mix")
    def k(w_hbm, idx_hbm, tab_hbm, out_hbm, *mix_scratch):
        run(w_hbm, idx_hbm, tab_hbm, out_hbm, *mix_scratch)

    return k(w, idx, table)


def _chunk_sizes(total):
    ramp, size = [EDGE_CHUNK], EDGE_CHUNK
    while size < MAX_CHUNK:
        ramp.append(size)
        size *= 2
    drain = ramp[2:][::-1] + [2 * EDGE_CHUNK] if len(ramp) > 1 else ramp
    middle = total - sum(ramp) - sum(drain)
    assert middle >= 0 and middle % MAX_CHUNK == 0
    return ramp + [MAX_CHUNK] * (middle // MAX_CHUNK) + drain


def kernel(x, norm_mix, w_in, pool_w, pool_scale, sgu_ln_g, sgu_ln_b, sgu_w, sgu_b, out_norm_pool,
           out_norm_sgu, w_out, norm_ffn, peer_wq, peer_keys, peer_u, peer_v, norm_final):
    B, S, D = x.shape
    assert norm_mix.shape[0] == 1, "single-layer block"
    T = B * S
    mix_args = (norm_mix[0], w_in[0], pool_w[0], pool_scale[0], sgu_ln_g[0], sgu_ln_b[0],
                sgu_w[0], sgu_b[0], out_norm_pool[0], out_norm_sgu[0], w_out[0])
    x1_parts = [(b * S, _mixer(x, b, 1, *mix_args).reshape(S, D)) for b in range(B)]
    wq = peer_wq[0].astype(jnp.bfloat16)
    keys = peer_keys[0].astype(jnp.bfloat16)
    u_t = _transposed_bf16(peer_u[0], x1_parts[0][1])
    v_tab = _pack_table(peer_v[0])
    out = None
    tok0 = 0
    peers = [norm_ffn, v_tab]
    outs = []
    chunks = _chunk_sizes(T)
    for c, tc in enumerate(chunks):
        part0, x1 = [p for p in x1_parts if p[0] <= tok0][-1]
        assert tok0 + tc <= part0 + x1.shape[0], "a token chunk must lie inside one mixer call"
        after = outs[-2] if c >= max(len(chunks) - 2, 2) else peers[-2]
        h2, idx, gate = _router(x1, norm_ffn[0], wq, keys, tok0 - part0, tc, after)
        w = _expert_weights(h2, u_t, idx, gate)
        peer = _expert_mix(w, idx, v_tab)
        peers.append(peer)
        out = _final(x1, tok0 - part0, peer, norm_final, out, tok0, T)
        outs.append(out)
        tok0 += tc
    return out.reshape(B, S, D)
```
